```python
import jax, jax.numpy as jnp
from jax import lax
import numpy as np


D_MODEL = 1024
BATCH = 8
SEQ = 4096
DEPTH = 2

D_FF = 2816
FFN_RES_SCALE = 0.5
RMS_EPS = 1e-6
PLE_DIM = 256
QBLK = 128
A_HEADS = 8
A_KV_HEADS = 2
A_GROUP = A_HEADS // A_KV_HEADS
A_HEAD_DIM = 64
WINDOW = 128
B_HEADS = 8
B_Q_LORA = 256
B_KV_LORA = 128
B_NOPE_DIM = 64
B_ROPE_DIM = 32
B_V_DIM = 64
ROPE_THETA = 10000.0
C_HEADS = 16
C_HEAD_DIM = 64
FORGET_BIAS_CENTER = 3.0
N_EVEN = (DEPTH + 1) // 2
N_ODD = DEPTH // 2
EVEN_IN_SPLITS = (A_HEADS * A_HEAD_DIM, A_KV_HEADS * A_HEAD_DIM, A_KV_HEADS * A_HEAD_DIM, B_Q_LORA, B_KV_LORA, B_ROPE_DIM)
EVEN_IN_DIM = A_HEADS * A_HEAD_DIM + 2 * A_KV_HEADS * A_HEAD_DIM + B_Q_LORA + B_KV_LORA + B_ROPE_DIM
EVEN_MIX_DIM = A_HEADS * A_HEAD_DIM + B_HEADS * B_V_DIM
ODD_MIX_DIM = C_HEADS * C_HEAD_DIM
ODD_IN_DIM = 3 * ODD_MIX_DIM + C_HEADS

kernel_name = 'hybrid_swa_mla_fox_macaron'


def rms_norm(x, g):
    xf = x.astype(jnp.float32)
    y = xf * lax.rsqrt(jnp.mean(xf * xf, axis=-1, keepdims=True) + RMS_EPS)
    return (y * g.astype(jnp.float32)).astype(x.dtype)


def swiglu(x, w_gate_up, w_down):
    g, u = jnp.split(x @ w_gate_up, 2, axis=-1)
    return (jax.nn.silu(g) * u) @ w_down


def alibi_slopes(n):
    return 2.0 ** (-8.0 * jnp.arange(1, n + 1, dtype=jnp.float32) / n)


def rope_tables(seq, dim):
    inv = ROPE_THETA ** (-jnp.arange(0, dim, 2, dtype=jnp.float32) / dim)
    ang = jnp.arange(seq, dtype=jnp.float32)[:, None] * inv[None, :]
    return jnp.cos(ang), jnp.sin(ang)


def apply_rope(x, cos, sin):
    half = x.shape[-1] // 2
    x1 = x[..., :half].astype(jnp.float32)
    x2 = x[..., half:].astype(jnp.float32)
    return jnp.concatenate([x1 * cos - x2 * sin, x1 * sin + x2 * cos], axis=-1).astype(x.dtype)


def swa_sink_attention(q, k, v, sinks):
    B, S = q.shape[0], q.shape[1]
    nb = S // WINDOW
    qb = q.reshape(B, nb, WINDOW, A_KV_HEADS, A_GROUP, A_HEAD_DIM)
    pad = jnp.zeros((B, WINDOW, A_KV_HEADS, A_HEAD_DIM), k.dtype)
    kp = jnp.concatenate([pad, k], axis=1).reshape(B, nb + 1, WINDOW, A_KV_HEADS, A_HEAD_DIM)
    vp = jnp.concatenate([pad, v], axis=1).reshape(B, nb + 1, WINDOW, A_KV_HEADS, A_HEAD_DIM)
    kb = jnp.concatenate([kp[:, :-1], kp[:, 1:]], axis=2)
    vb = jnp.concatenate([vp[:, :-1], vp[:, 1:]], axis=2)
    s = jnp.einsum('bnqkgd,bnskd->bnkgqs', qb, kb).astype(jnp.float32) * (A_HEAD_DIM ** -0.5)
    qi = jnp.arange(WINDOW)[:, None]
    kj = jnp.arange(2 * WINDOW)[None, :]
    dist = qi + WINDOW - kj
    band = (dist >= 0) & (dist < WINDOW)
    start_ok = (jnp.arange(nb)[:, None, None] * WINDOW + kj[None] - WINDOW) >= 0
    mask = band[None] & start_ok
    slopes = alibi_slopes(A_HEADS).reshape(A_KV_HEADS, A_GROUP)
    s = s - slopes[None, None, :, :, None, None] * dist.astype(jnp.float32)[None, None, None, None]
    s = jnp.where(mask[None, :, None, None], s, -jnp.inf)
    sink = sinks.astype(jnp.float32).reshape(A_KV_HEADS, A_GROUP)[None, None, :, :, None, None]
    m = jnp.maximum(jnp.max(s, axis=-1, keepdims=True), sink)
    e = jnp.exp(s - m)
    pr = e / (jnp.sum(e, axis=-1, keepdims=True) + jnp.exp(sink - m))
    out = jnp.einsum('bnkgqs,bnskd->bnqkgd', pr.astype(v.dtype), vb)
    return out.reshape(B, S, A_HEADS * A_HEAD_DIM)


def mla_attention(q_nope, q_rope, k_nope, k_rope, v):
    B, S = q_nope.shape[0], q_nope.shape[1]
    nb = S // QBLK
    qn = q_nope.reshape(B, nb, QBLK, B_HEADS, B_NOPE_DIM).transpose(1, 0, 2, 3, 4)
    qr = q_rope.reshape(B, nb, QBLK, B_HEADS, B_ROPE_DIM).transpose(1, 0, 2, 3, 4)
    kpos = jnp.arange(S)
    scale = (B_NOPE_DIM + B_ROPE_DIM) ** -0.5

    def one_block(args):
        qn_b, qr_b, n = args
        s = jnp.einsum('bqhd,bkhd->bhqk', qn_b, k_nope) + jnp.einsum('bqhd,bkd->bhqk', qr_b, k_rope)
        s = s.astype(jnp.float32) * scale
        qpos = n * QBLK + jnp.arange(QBLK)
        s = jnp.where(kpos[None, :] <= qpos[:, None], s, -jnp.inf)
        pr = jax.nn.softmax(s, axis=-1)
        return jnp.einsum('bhqk,bkhd->bqhd', pr.astype(v.dtype), v)

    out = lax.map(one_block, (qn, qr, jnp.arange(nb)))
    return out.transpose(1, 0, 2, 3, 4).reshape(B, S, B_HEADS * B_V_DIM)


def fox_attention(q, k, v, logc):
    B, S = q.shape[0], q.shape[1]
    nb = S // QBLK
    qb = q.reshape(B, nb, QBLK, C_HEADS, C_HEAD_DIM).transpose(1, 0, 2, 3, 4)
    cb = logc.reshape(B, nb, QBLK, C_HEADS).transpose(1, 0, 2, 3)
    c_keys = logc.transpose(0, 2, 1)
    kpos = jnp.arange(S)

    def one_block(args):
        q_b, c_b, n = args
        s = jnp.einsum('bqhd,bkhd->bhqk', q_b, k).astype(jnp.float32) * (C_HEAD_DIM ** -0.5)
        s = s + c_b.transpose(0, 2, 1)[..., None] - c_keys[:, :, None, :]
        qpos = n * QBLK + jnp.arange(QBLK)
        s = jnp.where(kpos[None, :] <= qpos[:, None], s, -jnp.inf)
        pr = jax.nn.softmax(s, axis=-1)
        return jnp.einsum('bhqk,bkhd->bqhd', pr.astype(v.dtype), v)

    out = lax.map(one_block, (qb, cb, jnp.arange(nb)))
    return out.transpose(1, 0, 2, 3, 4).reshape(B, S, C_HEADS * C_HEAD_DIM)


def even_mixer(h, w_in, sinks, cq_norm, w_uq, ckv_norm, w_ukv, w_out):
    B, S = h.shape[0], h.shape[1]
    idx = [int(i) for i in np.cumsum(EVEN_IN_SPLITS)[:-1]]
    a_q, a_k, a_v, c_q, c_kv, k_rope = jnp.split(h @ w_in, idx, axis=-1)
    out_a = swa_sink_attention(a_q.reshape(B, S, A_HEADS, A_HEAD_DIM),
                               a_k.reshape(B, S, A_KV_HEADS, A_HEAD_DIM),
                               a_v.reshape(B, S, A_KV_HEADS, A_HEAD_DIM), sinks)
    q = (rms_norm(c_q, cq_norm) @ w_uq).reshape(B, S, B_HEADS, B_NOPE_DIM + B_ROPE_DIM)
    kv = (rms_norm(c_kv, ckv_norm) @ w_ukv).reshape(B, S, B_HEADS, B_NOPE_DIM + B_V_DIM)
    cos, sin = rope_tables(S, B_ROPE_DIM)
    q_nope = q[..., :B_NOPE_DIM]
    q_rope = apply_rope(q[..., B_NOPE_DIM:], cos[None, :, None], sin[None, :, None])
    k_nope = kv[..., :B_NOPE_DIM]
    v = kv[..., B_NOPE_DIM:]
    k_rope = apply_rope(k_rope, cos[None], sin[None])
    out_b = mla_attention(q_nope, q_rope, k_nope, k_rope, v)
    return jnp.concatenate([out_a, out_b], axis=-1) @ w_out


def odd_mixer(h, w_in, b_f, w_out):
    B, S = h.shape[0], h.shape[1]
    w = ODD_MIX_DIM
    q, k, v, f_logit = jnp.split(h @ w_in, [w, 2 * w, 3 * w], axis=-1)
    logf = jax.nn.log_sigmoid(f_logit.astype(jnp.float32) + b_f.astype(jnp.float32))
    logc = jnp.cumsum(logf, axis=1)
    shp = (B, S, C_HEADS, C_HEAD_DIM)
    out = fox_attention(q.reshape(shp), k.reshape(shp), v.reshape(shp), logc)
    return out @ w_out


def _normal(key, shape, scale):
    return jax.random.normal(key, shape, jnp.float32) * scale


def _fwd_setup_inputs(seed: int = 0) -> dict:
    key = jax.random.key(seed)
    ks = jax.random.split(key, 23)
    D = D_MODEL
    return {
        'x': _normal(ks[0], (BATCH, SEQ, D), 1.0),
        'p': _normal(ks[1], (DEPTH, BATCH, SEQ, PLE_DIM), 1.0),
        'ffa_norm': 1.0 + _normal(ks[2], (DEPTH, D), 0.05),
        'ffa_w_gate_up': _normal(ks[3], (DEPTH, D, 2 * D_FF), D ** -0.5),
        'ffa_w_down': _normal(ks[4], (DEPTH, D_FF, D), D_FF ** -0.5),
        'mix_norm': 1.0 + _normal(ks[5], (DEPTH, D), 0.05),
        'ffb_norm': 1.0 + _normal(ks[6], (DEPTH, D), 0.05),
        'ffb_w_gate_up': _normal(ks[7], (DEPTH, D, 2 * D_FF), D ** -0.5),
        'ffb_w_down': _normal(ks[8], (DEPTH, D_FF, D), D_FF ** -0.5),
        'ple_norm': 1.0 + _normal(ks[9], (DEPTH, D), 0.05),
        'ple_w_gate': _normal(ks[10], (DEPTH, D, D), D ** -0.5),
        'ple_w_proj': _normal(ks[11], (DEPTH, PLE_DIM, D), PLE_DIM ** -0.5),
        'ev_w_in': _normal(ks[12], (N_EVEN, D, EVEN_IN_DIM), D ** -0.5),
        'ev_sinks': _normal(ks[13], (N_EVEN, A_HEADS), 0.5),
        'ev_cq_norm': 1.0 + _normal(ks[14], (N_EVEN, B_Q_LORA), 0.05),
        'ev_w_uq': _normal(ks[15], (N_EVEN, B_Q_LORA, B_HEADS * (B_NOPE_DIM + B_ROPE_DIM)), B_Q_LORA ** -0.5),
        'ev_ckv_norm': 1.0 + _normal(ks[16], (N_EVEN, B_KV_LORA), 0.05),
        'ev_w_ukv': _normal(ks[17], (N_EVEN, B_KV_LORA, B_HEADS * (B_NOPE_DIM + B_V_DIM)), B_KV_LORA ** -0.5),
        'ev_w_out': _normal(ks[18], (N_EVEN, EVEN_MIX_DIM, D), EVEN_MIX_DIM ** -0.5),
        'od_w_in': _normal(ks[19], (N_ODD, D, ODD_IN_DIM), D ** -0.5),
        'od_b_f': FORGET_BIAS_CENTER + _normal(ks[20], (N_ODD, C_HEADS), 0.5),
        'od_w_out': _normal(ks[21], (N_ODD, ODD_MIX_DIM, D), ODD_MIX_DIM ** -0.5),
        'final_norm': 1.0 + _normal(ks[22], (D,), 0.05),
    }


def _fwd_reference(x, p, ffa_norm, ffa_w_gate_up, ffa_w_down, mix_norm, ffb_norm, ffb_w_gate_up, ffb_w_down,
              ple_norm, ple_w_gate, ple_w_proj, ev_w_in, ev_sinks, ev_cq_norm, ev_w_uq, ev_ckv_norm,
              ev_w_ukv, ev_w_out, od_w_in, od_b_f, od_w_out, final_norm):
    h = x
    for i in range(DEPTH):
        j = i // 2
        h = h + FFN_RES_SCALE * swiglu(rms_norm(h, ffa_norm[i]), ffa_w_gate_up[i], ffa_w_down[i])
        hn = rms_norm(h, mix_norm[i])
        if i % 2 == 0:
            h = h + even_mixer(hn, ev_w_in[j], ev_sinks[j], ev_cq_norm[j], ev_w_uq[j],
                               ev_ckv_norm[j], ev_w_ukv[j], ev_w_out[j])
        else:
            h = h + odd_mixer(hn, od_w_in[j], od_b_f[j], od_w_out[j])
        h = h + FFN_RES_SCALE * swiglu(rms_norm(h, ffb_norm[i]), ffb_w_gate_up[i], ffb_w_down[i])
        gate = jax.nn.sigmoid(rms_norm(h, ple_norm[i]) @ ple_w_gate[i])
        h = h + gate * (p[i] @ ple_w_proj[i])
    return rms_norm(h, final_norm)


import jax as _jax
import jax.numpy as _jnp

TWIN_FORMAT = 'train_step'
FWD_PARAMS = ['x', 'p', 'ffa_norm', 'ffa_w_gate_up', 'ffa_w_down', 'mix_norm', 'ffb_norm', 'ffb_w_gate_up', 'ffb_w_down', 'ple_norm', 'ple_w_gate', 'ple_w_proj', 'ev_w_in', 'ev_sinks', 'ev_cq_norm', 'ev_w_uq', 'ev_ckv_norm', 'ev_w_ukv', 'ev_w_out', 'od_w_in', 'od_b_f', 'od_w_out', 'final_norm']
TWIN_WEIGHTS = ['ffa_norm', 'ffa_w_gate_up', 'ffa_w_down', 'mix_norm', 'ffb_norm', 'ffb_w_gate_up', 'ffb_w_down', 'ple_norm', 'ple_w_gate', 'ple_w_proj', 'ev_w_in', 'ev_sinks', 'ev_cq_norm', 'ev_w_uq', 'ev_ckv_norm', 'ev_w_ukv', 'ev_w_out', 'od_w_in', 'od_b_f', 'od_w_out', 'final_norm']
TWIN_DIFF_INPUT = 'x'
TWIN_INPUTS = ['x', 'p', 'ffa_norm', 'ffa_w_gate_up', 'ffa_w_down', 'mix_norm', 'ffb_norm', 'ffb_w_gate_up', 'ffb_w_down', 'ple_norm', 'ple_w_gate', 'ple_w_proj', 'ev_w_in', 'ev_sinks', 'ev_cq_norm', 'ev_w_uq', 'ev_ckv_norm', 'ev_w_ukv', 'ev_w_out', 'od_w_in', 'od_b_f', 'od_w_out', 'final_norm', 'loss_target', 'm_ffa_norm', 'm_ffa_w_gate_up', 'm_ffa_w_down', 'm_mix_norm', 'm_ffb_norm', 'm_ffb_w_gate_up', 'm_ffb_w_down', 'm_ple_norm', 'm_ple_w_gate', 'm_ple_w_proj', 'm_ev_w_in', 'm_ev_sinks', 'm_ev_cq_norm', 'm_ev_w_uq', 'm_ev_ckv_norm', 'm_ev_w_ukv', 'm_ev_w_out', 'm_od_w_in', 'm_od_b_f', 'm_od_w_out', 'm_final_norm', 'v_ffa_norm', 'v_ffa_w_gate_up', 'v_ffa_w_down', 'v_mix_norm', 'v_ffb_norm', 'v_ffb_w_gate_up', 'v_ffb_w_down', 'v_ple_norm', 'v_ple_w_gate', 'v_ple_w_proj', 'v_ev_w_in', 'v_ev_sinks', 'v_ev_cq_norm', 'v_ev_w_uq', 'v_ev_ckv_norm', 'v_ev_w_ukv', 'v_ev_w_out', 'v_od_w_in', 'v_od_b_f', 'v_od_w_out', 'v_final_norm']
TWIN_OUTPUTS = ['loss', 'grad_x', 'grad_ffa_norm', 'grad_ffa_w_gate_up', 'grad_ffa_w_down', 'grad_mix_norm', 'grad_ffb_norm', 'grad_ffb_w_gate_up', 'grad_ffb_w_down', 'grad_ple_norm', 'grad_ple_w_gate', 'grad_ple_w_proj', 'grad_ev_w_in', 'grad_ev_sinks', 'grad_ev_cq_norm', 'grad_ev_w_uq', 'grad_ev_ckv_norm', 'grad_ev_w_ukv', 'grad_ev_w_out', 'grad_od_w_in', 'grad_od_b_f', 'grad_od_w_out', 'grad_final_norm', 'delta_ffa_norm', 'delta_ffa_w_gate_up', 'delta_ffa_w_down', 'delta_mix_norm', 'delta_ffb_norm', 'delta_ffb_w_gate_up', 'delta_ffb_w_down', 'delta_ple_norm', 'delta_ple_w_gate', 'delta_ple_w_proj', 'delta_ev_w_in', 'delta_ev_sinks', 'delta_ev_cq_norm', 'delta_ev_w_uq', 'delta_ev_ckv_norm', 'delta_ev_w_ukv', 'delta_ev_w_out', 'delta_od_w_in', 'delta_od_b_f', 'delta_od_w_out', 'delta_final_norm', 'new_m_ffa_norm', 'new_m_ffa_w_gate_up', 'new_m_ffa_w_down', 'new_m_mix_norm', 'new_m_ffb_norm', 'new_m_ffb_w_gate_up', 'new_m_ffb_w_down', 'new_m_ple_norm', 'new_m_ple_w_gate', 'new_m_ple_w_proj', 'new_m_ev_w_in', 'new_m_ev_sinks', 'new_m_ev_cq_norm', 'new_m_ev_w_uq', 'new_m_ev_ckv_norm', 'new_m_ev_w_ukv', 'new_m_ev_w_out', 'new_m_od_w_in', 'new_m_od_b_f', 'new_m_od_w_out', 'new_m_final_norm', 'new_v_ffa_norm', 'new_v_ffa_w_gate_up', 'new_v_ffa_w_down', 'new_v_mix_norm', 'new_v_ffb_norm', 'new_v_ffb_w_gate_up', 'new_v_ffb_w_down', 'new_v_ple_norm', 'new_v_ple_w_gate', 'new_v_ple_w_proj', 'new_v_ev_w_in', 'new_v_ev_sinks', 'new_v_ev_cq_norm', 'new_v_ev_w_uq', 'new_v_ev_ckv_norm', 'new_v_ev_w_ukv', 'new_v_ev_w_out', 'new_v_od_w_in', 'new_v_od_b_f', 'new_v_od_w_out', 'new_v_final_norm']
TWIN_LEAF_KINDS = {'loss': 'loss', 'grad_x': 'grad_x', 'grad_ffa_norm': 'grad_w', 'grad_ffa_w_gate_up': 'grad_w', 'grad_ffa_w_down': 'grad_w', 'grad_mix_norm': 'grad_w', 'grad_ffb_norm': 'grad_w', 'grad_ffb_w_gate_up': 'grad_w', 'grad_ffb_w_down': 'grad_w', 'grad_ple_norm': 'grad_w', 'grad_ple_w_gate': 'grad_w', 'grad_ple_w_proj': 'grad_w', 'grad_ev_w_in': 'grad_w', 'grad_ev_sinks': 'grad_w', 'grad_ev_cq_norm': 'grad_w', 'grad_ev_w_uq': 'grad_w', 'grad_ev_ckv_norm': 'grad_w', 'grad_ev_w_ukv': 'grad_w', 'grad_ev_w_out': 'grad_w', 'grad_od_w_in': 'grad_w', 'grad_od_b_f': 'grad_w', 'grad_od_w_out': 'grad_w', 'grad_final_norm': 'grad_w', 'delta_ffa_norm': 'delta_w', 'delta_ffa_w_gate_up': 'delta_w', 'delta_ffa_w_down': 'delta_w', 'delta_mix_norm': 'delta_w', 'delta_ffb_norm': 'delta_w', 'delta_ffb_w_gate_up': 'delta_w', 'delta_ffb_w_down': 'delta_w', 'delta_ple_norm': 'delta_w', 'delta_ple_w_gate': 'delta_w', 'delta_ple_w_proj': 'delta_w', 'delta_ev_w_in': 'delta_w', 'delta_ev_sinks': 'delta_w', 'delta_ev_cq_norm': 'delta_w', 'delta_ev_w_uq': 'delta_w', 'delta_ev_ckv_norm': 'delta_w', 'delta_ev_w_ukv': 'delta_w', 'delta_ev_w_out': 'delta_w', 'delta_od_w_in': 'delta_w', 'delta_od_b_f': 'delta_w', 'delta_od_w_out': 'delta_w', 'delta_final_norm': 'delta_w', 'new_m_ffa_norm': 'new_m', 'new_m_ffa_w_gate_up': 'new_m', 'new_m_ffa_w_down': 'new_m', 'new_m_mix_norm': 'new_m', 'new_m_ffb_norm': 'new_m', 'new_m_ffb_w_gate_up': 'new_m', 'new_m_ffb_w_down': 'new_m', 'new_m_ple_norm': 'new_m', 'new_m_ple_w_gate': 'new_m', 'new_m_ple_w_proj': 'new_m', 'new_m_ev_w_in': 'new_m', 'new_m_ev_sinks': 'new_m', 'new_m_ev_cq_norm': 'new_m', 'new_m_ev_w_uq': 'new_m', 'new_m_ev_ckv_norm': 'new_m', 'new_m_ev_w_ukv': 'new_m', 'new_m_ev_w_out': 'new_m', 'new_m_od_w_in': 'new_m', 'new_m_od_b_f': 'new_m', 'new_m_od_w_out': 'new_m', 'new_m_final_norm': 'new_m', 'new_v_ffa_norm': 'new_v', 'new_v_ffa_w_gate_up': 'new_v', 'new_v_ffa_w_down': 'new_v', 'new_v_mix_norm': 'new_v', 'new_v_ffb_norm': 'new_v', 'new_v_ffb_w_gate_up': 'new_v', 'new_v_ffb_w_down': 'new_v', 'new_v_ple_norm': 'new_v', 'new_v_ple_w_gate': 'new_v', 'new_v_ple_w_proj': 'new_v', 'new_v_ev_w_in': 'new_v', 'new_v_ev_sinks': 'new_v', 'new_v_ev_cq_norm': 'new_v', 'new_v_ev_w_uq': 'new_v', 'new_v_ev_ckv_norm': 'new_v', 'new_v_ev_w_ukv': 'new_v', 'new_v_ev_w_out': 'new_v', 'new_v_od_w_in': 'new_v', 'new_v_od_b_f': 'new_v', 'new_v_od_w_out': 'new_v', 'new_v_final_norm': 'new_v'}


def _forward(args):
    return _fwd_reference(*[args[k] for k in FWD_PARAMS])


def _output_shape():
    out = _jax.eval_shape(lambda: _forward(_fwd_setup_inputs(0)))
    return out.shape, out.dtype

N_MICROBATCH = 1
ADAM_LR = 0.001
ADAM_B1 = 0.9
ADAM_B2 = 0.999
ADAM_EPS = 1e-08
ADAM_WD = 0.01
ADAM_STEP = 10
PER_EXAMPLE_BATCH_AXIS = {'x': 0, 'p': 1, 'loss_target': 0}
SHARED_INPUTS = []
_WEIGHT_DTYPES = {'ffa_norm': _jnp.float32, 'ffa_w_gate_up': _jnp.float32, 'ffa_w_down': _jnp.float32, 'mix_norm': _jnp.float32, 'ffb_norm': _jnp.float32, 'ffb_w_gate_up': _jnp.float32, 'ffb_w_down': _jnp.float32, 'ple_norm': _jnp.float32, 'ple_w_gate': _jnp.float32, 'ple_w_proj': _jnp.float32, 'ev_w_in': _jnp.float32, 'ev_sinks': _jnp.float32, 'ev_cq_norm': _jnp.float32, 'ev_w_uq': _jnp.float32, 'ev_ckv_norm': _jnp.float32, 'ev_w_ukv': _jnp.float32, 'ev_w_out': _jnp.float32, 'od_w_in': _jnp.float32, 'od_b_f': _jnp.float32, 'od_w_out': _jnp.float32, 'final_norm': _jnp.float32}
MOMENT_SCALE = {'ffa_norm': 6.940814e-02, 'ffa_w_gate_up': 2.871204e-02, 'ffa_w_down': 4.686240e-02, 'mix_norm': 6.464819e-02, 'ffb_norm': 5.983734e-02, 'ffb_w_gate_up': 2.560957e-02, 'ffb_w_down': 4.194750e-02, 'ple_norm': 2.968966e-02, 'ple_w_gate': 2.918254e-02, 'ple_w_proj': 7.587155e-02, 'ev_w_in': 6.065036e-02, 'ev_sinks': 3.091446e-02, 'ev_cq_norm': 4.614480e-02, 'ev_w_uq': 2.468347e-02, 'ev_ckv_norm': 9.672518e-02, 'ev_w_ukv': 3.288672e-02, 'ev_w_out': 4.476269e-02, 'od_w_in': 3.551108e-02, 'od_b_f': 2.738606e-01, 'od_w_out': 4.148182e-02, 'final_norm': 3.194571e+01}


def _to_microbatches(a, axis):
    t = _jnp.moveaxis(a, axis, 0)
    t = t.reshape((N_MICROBATCH, t.shape[0] // N_MICROBATCH) + t.shape[1:])
    return _jnp.moveaxis(t, 1, axis + 1)


def setup_inputs(seed: int = 0) -> dict:
    inp = _fwd_setup_inputs(seed)
    key = _jax.random.fold_in(_jax.random.key(seed), 7919)
    shape, _ = _output_shape()
    out = dict(inp)
    out["loss_target"] = _jax.random.normal(_jax.random.fold_in(key, 0), shape, _jnp.float32)
    for i, name in enumerate(TWIN_WEIGHTS):
        w = inp[name].astype(_jnp.float32)
        if MOMENT_SCALE is None:
            s = _jnp.sqrt(_jnp.mean(_jnp.square(w)) + 1e-30)
        else:
            s = MOMENT_SCALE[name]
        km, kv = _jax.random.split(_jax.random.fold_in(key, i + 1))
        out[name] = w
        out["m_" + name] = s * _jax.random.normal(km, w.shape, _jnp.float32)
        out["v_" + name] = (s * s) * _jax.random.uniform(kv, w.shape, _jnp.float32, 0.5, 1.5)
    if N_MICROBATCH > 1:
        for name, axis in PER_EXAMPLE_BATCH_AXIS.items():
            out[name] = _to_microbatches(out[name], axis)
    return {'x': out['x'], 'p': out['p'], 'ffa_norm': out['ffa_norm'], 'ffa_w_gate_up': out['ffa_w_gate_up'], 'ffa_w_down': out['ffa_w_down'], 'mix_norm': out['mix_norm'], 'ffb_norm': out['ffb_norm'], 'ffb_w_gate_up': out['ffb_w_gate_up'], 'ffb_w_down': out['ffb_w_down'], 'ple_norm': out['ple_norm'], 'ple_w_gate': out['ple_w_gate'], 'ple_w_proj': out['ple_w_proj'], 'ev_w_in': out['ev_w_in'], 'ev_sinks': out['ev_sinks'], 'ev_cq_norm': out['ev_cq_norm'], 'ev_w_uq': out['ev_w_uq'], 'ev_ckv_norm': out['ev_ckv_norm'], 'ev_w_ukv': out['ev_w_ukv'], 'ev_w_out': out['ev_w_out'], 'od_w_in': out['od_w_in'], 'od_b_f': out['od_b_f'], 'od_w_out': out['od_w_out'], 'final_norm': out['final_norm'], 'loss_target': out['loss_target'], 'm_ffa_norm': out['m_ffa_norm'], 'm_ffa_w_gate_up': out['m_ffa_w_gate_up'], 'm_ffa_w_down': out['m_ffa_w_down'], 'm_mix_norm': out['m_mix_norm'], 'm_ffb_norm': out['m_ffb_norm'], 'm_ffb_w_gate_up': out['m_ffb_w_gate_up'], 'm_ffb_w_down': out['m_ffb_w_down'], 'm_ple_norm': out['m_ple_norm'], 'm_ple_w_gate': out['m_ple_w_gate'], 'm_ple_w_proj': out['m_ple_w_proj'], 'm_ev_w_in': out['m_ev_w_in'], 'm_ev_sinks': out['m_ev_sinks'], 'm_ev_cq_norm': out['m_ev_cq_norm'], 'm_ev_w_uq': out['m_ev_w_uq'], 'm_ev_ckv_norm': out['m_ev_ckv_norm'], 'm_ev_w_ukv': out['m_ev_w_ukv'], 'm_ev_w_out': out['m_ev_w_out'], 'm_od_w_in': out['m_od_w_in'], 'm_od_b_f': out['m_od_b_f'], 'm_od_w_out': out['m_od_w_out'], 'm_final_norm': out['m_final_norm'], 'v_ffa_norm': out['v_ffa_norm'], 'v_ffa_w_gate_up': out['v_ffa_w_gate_up'], 'v_ffa_w_down': out['v_ffa_w_down'], 'v_mix_norm': out['v_mix_norm'], 'v_ffb_norm': out['v_ffb_norm'], 'v_ffb_w_gate_up': out['v_ffb_w_gate_up'], 'v_ffb_w_down': out['v_ffb_w_down'], 'v_ple_norm': out['v_ple_norm'], 'v_ple_w_gate': out['v_ple_w_gate'], 'v_ple_w_proj': out['v_ple_w_proj'], 'v_ev_w_in': out['v_ev_w_in'], 'v_ev_sinks': out['v_ev_sinks'], 'v_ev_cq_norm': out['v_ev_cq_norm'], 'v_ev_w_uq': out['v_ev_w_uq'], 'v_ev_ckv_norm': out['v_ev_ckv_norm'], 'v_ev_w_ukv': out['v_ev_w_ukv'], 'v_ev_w_out': out['v_ev_w_out'], 'v_od_w_in': out['v_od_w_in'], 'v_od_b_f': out['v_od_b_f'], 'v_od_w_out': out['v_od_w_out'], 'v_final_norm': out['v_final_norm']}


def _loss(weights, diff, rest, loss_target):
    with _jax.named_scope("forward"):
        args = {**rest, TWIN_DIFF_INPUT: diff, **{k: w.astype(_WEIGHT_DTYPES[k]) for k, w in weights.items()}}
        y = _forward(args)
    with _jax.named_scope("loss_head"):
        err = _jnp.square(y.astype(_jnp.float32) - loss_target)
        return 0.5 * _jnp.sum(_jnp.mean(err, axis=-1)) if err.ndim else 0.5 * err


def _adamw(w, g, m, v):
    m = ADAM_B1 * m + (1.0 - ADAM_B1) * g
    v = ADAM_B2 * v + (1.0 - ADAM_B2) * _jnp.square(g)
    m_hat = m / (1.0 - ADAM_B1 ** ADAM_STEP)
    v_hat = v / (1.0 - ADAM_B2 ** ADAM_STEP)
    delta = -ADAM_LR * (m_hat / (_jnp.sqrt(v_hat) + ADAM_EPS) + ADAM_WD * w)
    return delta, m, v


def reference(x, p, ffa_norm, ffa_w_gate_up, ffa_w_down, mix_norm, ffb_norm, ffb_w_gate_up, ffb_w_down, ple_norm, ple_w_gate, ple_w_proj, ev_w_in, ev_sinks, ev_cq_norm, ev_w_uq, ev_ckv_norm, ev_w_ukv, ev_w_out, od_w_in, od_b_f, od_w_out, final_norm, loss_target, m_ffa_norm, m_ffa_w_gate_up, m_ffa_w_down, m_mix_norm, m_ffb_norm, m_ffb_w_gate_up, m_ffb_w_down, m_ple_norm, m_ple_w_gate, m_ple_w_proj, m_ev_w_in, m_ev_sinks, m_ev_cq_norm, m_ev_w_uq, m_ev_ckv_norm, m_ev_w_ukv, m_ev_w_out, m_od_w_in, m_od_b_f, m_od_w_out, m_final_norm, v_ffa_norm, v_ffa_w_gate_up, v_ffa_w_down, v_mix_norm, v_ffb_norm, v_ffb_w_gate_up, v_ffb_w_down, v_ple_norm, v_ple_w_gate, v_ple_w_proj, v_ev_w_in, v_ev_sinks, v_ev_cq_norm, v_ev_w_uq, v_ev_ckv_norm, v_ev_w_ukv, v_ev_w_out, v_od_w_in, v_od_b_f, v_od_w_out, v_final_norm):
    given = dict(x=x, p=p, ffa_norm=ffa_norm, ffa_w_gate_up=ffa_w_gate_up, ffa_w_down=ffa_w_down, mix_norm=mix_norm, ffb_norm=ffb_norm, ffb_w_gate_up=ffb_w_gate_up, ffb_w_down=ffb_w_down, ple_norm=ple_norm, ple_w_gate=ple_w_gate, ple_w_proj=ple_w_proj, ev_w_in=ev_w_in, ev_sinks=ev_sinks, ev_cq_norm=ev_cq_norm, ev_w_uq=ev_w_uq, ev_ckv_norm=ev_ckv_norm, ev_w_ukv=ev_w_ukv, ev_w_out=ev_w_out, od_w_in=od_w_in, od_b_f=od_b_f, od_w_out=od_w_out, final_norm=final_norm, loss_target=loss_target, m_ffa_norm=m_ffa_norm, m_ffa_w_gate_up=m_ffa_w_gate_up, m_ffa_w_down=m_ffa_w_down, m_mix_norm=m_mix_norm, m_ffb_norm=m_ffb_norm, m_ffb_w_gate_up=m_ffb_w_gate_up, m_ffb_w_down=m_ffb_w_down, m_ple_norm=m_ple_norm, m_ple_w_gate=m_ple_w_gate, m_ple_w_proj=m_ple_w_proj, m_ev_w_in=m_ev_w_in, m_ev_sinks=m_ev_sinks, m_ev_cq_norm=m_ev_cq_norm, m_ev_w_uq=m_ev_w_uq, m_ev_ckv_norm=m_ev_ckv_norm, m_ev_w_ukv=m_ev_w_ukv, m_ev_w_out=m_ev_w_out, m_od_w_in=m_od_w_in, m_od_b_f=m_od_b_f, m_od_w_out=m_od_w_out, m_final_norm=m_final_norm, v_ffa_norm=v_ffa_norm, v_ffa_w_gate_up=v_ffa_w_gate_up, v_ffa_w_down=v_ffa_w_down, v_mix_norm=v_mix_norm, v_ffb_norm=v_ffb_norm, v_ffb_w_gate_up=v_ffb_w_gate_up, v_ffb_w_down=v_ffb_w_down, v_ple_norm=v_ple_norm, v_ple_w_gate=v_ple_w_gate, v_ple_w_proj=v_ple_w_proj, v_ev_w_in=v_ev_w_in, v_ev_sinks=v_ev_sinks, v_ev_cq_norm=v_ev_cq_norm, v_ev_w_uq=v_ev_w_uq, v_ev_ckv_norm=v_ev_ckv_norm, v_ev_w_ukv=v_ev_w_ukv, v_ev_w_out=v_ev_w_out, v_od_w_in=v_od_w_in, v_od_b_f=v_od_b_f, v_od_w_out=v_od_w_out, v_final_norm=v_final_norm)
    weights = {n: given[n] for n in TWIN_WEIGHTS}
    shared = {n: given[n] for n in SHARED_INPUTS}
    per_example = {n: given[n] for n in ['x', 'p']}
    grad_fn = _jax.value_and_grad(_loss, argnums=(0, 1))

    def one_microbatch(ex, loss_target):
        ex = dict(ex)
        diff = ex.pop(TWIN_DIFF_INPUT)
        return grad_fn(weights, diff, {**shared, **ex}, loss_target)

    if N_MICROBATCH == 1:
        loss, (grad_w, grad_x) = one_microbatch(per_example, given["loss_target"])
    else:
        def body(carry, xs):
            loss_sum, grad_sum = carry
            l_k, (gw_k, gx_k) = one_microbatch(xs[0], xs[1])
            with _jax.named_scope("update"):
                return (loss_sum + l_k, _jax.tree.map(_jnp.add, grad_sum, gw_k)), gx_k

        init = (_jnp.zeros((), _jnp.float32), _jax.tree.map(_jnp.zeros_like, weights))
        (loss, grad_w), grad_x = _jax.lax.scan(body, init, (per_example, given["loss_target"]))
    with _jax.named_scope("update"):
        delta_w, new_m, new_v = {}, {}, {}
        for n in TWIN_WEIGHTS:
            delta_w[n], new_m[n], new_v[n] = _adamw(weights[n], grad_w[n], given["m_" + n], given["v_" + n])
    return (loss, grad_x, *[grad_w[n] for n in TWIN_WEIGHTS], *[delta_w[n] for n in TWIN_WEIGHTS],
            *[new_m[n] for n in TWIN_WEIGHTS], *[new_v[n] for n in TWIN_WEIGHTS])
```

```python
import functools

import numpy as np
import jax
import jax.numpy as jnp
from jax import lax
from jax.experimental import pallas as pl
from jax.experimental.pallas import tpu as pltpu

F32 = jnp.float32
BF16 = jnp.bfloat16
MESH = pl.DeviceIdType.MESH

D_MODEL = 1024
D_FF = 2816
RMS_EPS = 1e-6
PLE_DIM = 256
A_HEADS, A_KV_HEADS, A_HEAD_DIM, WINDOW = 8, 2, 64, 128
A_GROUP = A_HEADS // A_KV_HEADS
B_HEADS, B_Q_LORA, B_KV_LORA, B_NOPE, B_ROPE, B_V = 8, 256, 128, 64, 32, 64
ROPE_THETA = 10000.0
C_HEADS, C_HEAD_DIM = 16, 64
EVEN_IN = 1184
EVEN_IN_PAD = 1280
ODD_IN = 3088
ODD_IN_PAD = 3200
DEPTH = 2
ADAM_LR, ADAM_B1, ADAM_B2, ADAM_EPS, ADAM_WD, ADAM_STEP = 0.001, 0.9, 0.999, 1e-08, 0.01, 10

N_DEV = 8
LANES = 128
SUBLANES = 8
EW_TILE_BYTES = 3 << 20
MM_VMEM_BYTES = 26 << 20
NEG = -1e30

BIG = {
    "ffa_w_gate_up": 2, "ffa_w_down": 1, "ffb_w_gate_up": 2, "ffb_w_down": 1, "ple_w_gate": 1, "ple_w_proj": 2,
    "ev_w_in": 2, "ev_w_uq": 2, "ev_w_ukv": 2, "ev_w_out": 1, "od_w_in": 2, "od_w_out": 1,
}
SMALL = ["ffa_norm", "mix_norm", "ffb_norm", "ple_norm", "ev_sinks", "ev_cq_norm", "ev_ckv_norm", "od_b_f", "final_norm"]
WEIGHTS = ["ffa_norm", "ffa_w_gate_up", "ffa_w_down", "mix_norm", "ffb_norm", "ffb_w_gate_up", "ffb_w_down", "ple_norm",
           "ple_w_gate", "ple_w_proj", "ev_w_in", "ev_sinks", "ev_cq_norm", "ev_w_uq", "ev_ckv_norm", "ev_w_ukv", "ev_w_out",
           "od_w_in", "od_b_f", "od_w_out", "final_norm"]
SMALL_COLS = 1280


def _divisor(n, cap, mult):
    if n <= cap:
        return n
    for t in range(cap - cap % mult, 0, -mult):
        if n % t == 0:
            return t
    raise ValueError(f"no tile for {n} under {cap} in steps of {mult}")


def _lanes(c):
    return -(-c // LANES) * LANES


def _ew(fn, rows, vecs, outs, reds=(), *, name):
    R = rows[0].shape[0]
    per_row = sum(_lanes(a.shape[1]) * a.dtype.itemsize for a in rows) + sum(_lanes(c) * jnp.dtype(d).itemsize for c, d in outs)
    tm = _divisor(R, max(SUBLANES, EW_TILE_BYTES // per_row // 16 * 16), 16) if R % 16 == 0 else R
    n_r, n_v, n_o = len(rows), len(vecs), len(outs)

    def body(*refs):
        ins = [r[...] for r in refs[: n_r + n_v]]
        res = fn(*ins)
        if not isinstance(res, (tuple, list)):
            res = (res,)
        o_refs = refs[n_r + n_v: n_r + n_v + n_o]
        r_refs = refs[n_r + n_v + n_o:]
        for ref, val in zip(o_refs, res[:n_o]):
            ref[...] = val.astype(ref.dtype)
        if r_refs:
            @pl.when(pl.program_id(0) == 0)
            def _():
                for ref in r_refs:
                    ref[...] = jnp.zeros_like(ref)
            for ref, val in zip(r_refs, res[n_o:]):
                ref[...] += val

    in_specs = [pl.BlockSpec((tm, a.shape[1]), lambda i: (i, 0)) for a in rows]
    in_specs += [pl.BlockSpec((1, a.shape[1]), lambda i: (0, 0)) for a in vecs]
    out_specs = [pl.BlockSpec((tm, c), lambda i: (i, 0)) for c, _ in outs]
    out_specs += [pl.BlockSpec((1, c), lambda i: (0, 0)) for c in reds]
    out_shape = [jax.ShapeDtypeStruct((R, c), d) for c, d in outs] + [jax.ShapeDtypeStruct((1, c), F32) for c in reds]
    res = pl.pallas_call(body, name=name, grid=(R // tm,), in_specs=in_specs, out_specs=out_specs, out_shape=out_shape)(*rows, *vecs)
    return res[0] if len(res) == 1 else res


def _rms_fwd(x, w, name):
    def fn(x, w):
        y = x * lax.rsqrt(jnp.mean(x * x, axis=-1, keepdims=True) + RMS_EPS)
        return y * w
    return _ew(fn, [x], [w], [(x.shape[1], BF16)], name=name)


def _rms_bwd(dn, x, w, dres, name):
    def fn(dn, x, *rest):
        w = rest[-1]
        r = lax.rsqrt(jnp.mean(x * x, axis=-1, keepdims=True) + RMS_EPS)
        xh = x * r
        gw = dn * w
        dx = r * (gw - xh * jnp.mean(gw * xh, axis=-1, keepdims=True))
        if len(rest) == 2:
            dx = dx + rest[0]
        return dx, jnp.sum(dn * xh, axis=0, keepdims=True)
    rows = [dn, x] + ([dres] if dres is not None else [])
    return _ew(fn, rows, [w], [(x.shape[1], F32)], [x.shape[1]], name=name)


def _swiglu_fwd(gu, name):
    half = gu.shape[1] // 2

    def fn(gu):
        g = gu[:, :half].astype(F32)
        u = gu[:, half:].astype(F32)
        return g * jax.nn.sigmoid(g) * u
    return _ew(fn, [gu], [], [(half, BF16)], name=name)


def _swiglu_bwd(gu, dact, name):
    half = gu.shape[1] // 2

    def fn(gu, dact):
        g = gu[:, :half].astype(F32)
        u = gu[:, half:].astype(F32)
        sg = jax.nn.sigmoid(g)
        dg = dact * u * (sg * (1.0 + g * (1.0 - sg)))
        du = dact * (g * sg)
        return jnp.concatenate([dg, du], axis=1)
    return _ew(fn, [gu, dact], [], [(gu.shape[1], BF16)], name=name)


def _ple_fwd(h, gpre, pp, name):
    return _ew(lambda h, g, q: h + jax.nn.sigmoid(g) * q, [h, gpre, pp], [], [(h.shape[1], F32)], name=name)


def _ple_bwd(dh, gpre, pp, name):
    def fn(dh, g, q):
        sg = jax.nn.sigmoid(g)
        return dh * q * (sg * (1.0 - sg)), dh * sg
    return _ew(fn, [dh, gpre, pp], [], [(dh.shape[1], BF16), (dh.shape[1], BF16)], name=name)


def _rope(x1, x2, cos, sin, name):
    c = x1.shape[1]
    return _ew(lambda a, b, co, si: (a * co - b * si, a * si + b * co), [x1, x2, cos, sin], [], [(c, F32), (c, F32)], name=name)


def _logsig_fwd(f, b, name):
    def fn(f, b):
        z = f + b
        return jnp.minimum(z, 0.0) - jnp.log(1.0 + jnp.exp(-jnp.abs(z)))
    return _ew(fn, [f], [b], [(f.shape[1], F32)], name=name)


def _logsig_bwd(dlogf, f, b, name):
    def fn(d, f, b):
        df = d * jax.nn.sigmoid(-(f + b))
        return df, jnp.sum(df, axis=0, keepdims=True)
    return _ew(fn, [dlogf, f], [b], [(f.shape[1], F32)], [f.shape[1]], name=name)


def _final_fwd_bwd(h, w, target, name):
    d = h.shape[1]

    def fn(h, t, w):
        r = lax.rsqrt(jnp.mean(h * h, axis=-1, keepdims=True) + RMS_EPS)
        xh = h * r
        y = xh * w
        err = y - t
        dy = err * (1.0 / d)
        gw = dy * w
        dx = r * (gw - xh * jnp.mean(gw * xh, axis=-1, keepdims=True))
        return dx, jnp.sum(dy * xh, axis=0, keepdims=True), jnp.sum(err * err, axis=0, keepdims=True) * (0.5 / d)
    return _ew(fn, [h, target], [w], [(d, F32)], [d, d], name=name)


def _adamw(w, g, m, v, name):
    shape = w.shape
    c = shape[-1]
    w2, g2, m2, v2 = (a.reshape(-1, c) for a in (w, g, m, v))

    def fn(w, g, m, v):
        m = ADAM_B1 * m + (1.0 - ADAM_B1) * g
        v = ADAM_B2 * v + (1.0 - ADAM_B2) * jnp.square(g)
        m_hat = m / (1.0 - ADAM_B1 ** ADAM_STEP)
        v_hat = v / (1.0 - ADAM_B2 ** ADAM_STEP)
        delta = -ADAM_LR * (m_hat / (jnp.sqrt(v_hat) + ADAM_EPS) + ADAM_WD * w)
        return delta, m, v
    d, nm, nv = _ew(fn, [w2, g2, m2, v2], [], [(c, F32)] * 3, name=name)
    return d.reshape(shape), nm.reshape(shape), nv.reshape(shape)


def _cumsum(x, reverse, name):
    S, C = x.shape
    tm = _divisor(S, 512, 16)
    nt = S // tm

    def split3(v):
        hi = v.astype(BF16)
        r1 = v - hi.astype(F32)
        mid = r1.astype(BF16)
        lo = (r1 - mid.astype(F32)).astype(BF16)
        return hi, mid, lo

    def body(x_ref, o_ref, carry):
        @pl.when(pl.program_id(0) == 0)
        def _():
            carry[...] = jnp.zeros_like(carry)
        r = lax.broadcasted_iota(jnp.int32, (tm, tm), 0)
        c = lax.broadcasted_iota(jnp.int32, (tm, tm), 1)
        tri = jnp.where((c >= r) if reverse else (c <= r), 1.0, 0.0).astype(BF16)
        xv = x_ref[...]
        acc = jnp.zeros((tm, C), F32)
        for part in split3(xv):
            acc = acc + jnp.dot(tri, part, preferred_element_type=F32)
        o_ref[...] = acc + carry[...]
        carry[...] += jnp.sum(xv, axis=0, keepdims=True)

    idx = (lambda i: (nt - 1 - i, 0)) if reverse else (lambda i: (i, 0))
    return pl.pallas_call(
        body, name=name, grid=(nt,), in_specs=[pl.BlockSpec((tm, C), idx)], out_specs=pl.BlockSpec((tm, C), idx),
        out_shape=jax.ShapeDtypeStruct((S, C), F32), scratch_shapes=[pltpu.VMEM((1, C), F32)],
    )(x)


def _mm(a, b, *, ta=False, tb=False, out=F32, res=None, alpha=1.0, name):
    K, M = a.shape if ta else a.shape[::-1]
    N = b.shape[0] if tb else b.shape[1]
    assert (b.shape[1] if tb else b.shape[0]) == K, (a.shape, b.shape, ta, tb)
    tk = _divisor(K, 1024, LANES)
    tn = _divisor(N, 1408, LANES)
    tm = None
    for cap in (1024, 512, 256, 128):
        tm = _divisor(M, cap, LANES if ta else 16) if M > cap else M
        est = 2 * (tm * tk * a.dtype.itemsize + tk * tn * b.dtype.itemsize + tm * tn * jnp.dtype(out).itemsize)
        est += tm * tn * 4 + (2 * tm * tn * 4 if res is not None else 0)
        if est <= MM_VMEM_BYTES:
            break
    nk = K // tk
    dims = (((0 if ta else 1,), (1 if tb else 0,)), ((), ()))

    def body(*refs):
        a_ref, b_ref = refs[0], refs[1]
        res_ref = refs[2] if res is not None else None
        o_ref, acc_ref = refs[-2], refs[-1]
        k = pl.program_id(2)

        @pl.when(k == 0)
        def _():
            acc_ref[...] = jnp.zeros_like(acc_ref)

        acc_ref[...] += lax.dot_general(a_ref[...].astype(BF16), b_ref[...].astype(BF16), dims, preferred_element_type=F32)

        @pl.when(k == nk - 1)
        def _():
            r = acc_ref[...]
            if alpha != 1.0:
                r = r * alpha
            if res_ref is not None:
                r = res_ref[...] + r
            o_ref[...] = r.astype(o_ref.dtype)

    a_spec = pl.BlockSpec((tk, tm), lambda i, j, k: (k, i)) if ta else pl.BlockSpec((tm, tk), lambda i, j, k: (i, k))
    b_spec = pl.BlockSpec((tn, tk), lambda i, j, k: (j, k)) if tb else pl.BlockSpec((tk, tn), lambda i, j, k: (k, j))
    o_spec = pl.BlockSpec((tm, tn), lambda i, j, k: (i, j))
    in_specs = [a_spec, b_spec] + ([o_spec] if res is not None else [])
    args = [a, b] + ([res] if res is not None else [])
    return pl.pallas_call(
        body, name=name, grid=(M // tm, N // tn, nk), in_specs=in_specs, out_specs=o_spec,
        out_shape=jax.ShapeDtypeStruct((M, N), out), scratch_shapes=[pltpu.VMEM((tm, tn), F32)],
        compiler_params=pltpu.CompilerParams(dimension_semantics=("parallel", "parallel", "arbitrary")),
    )(*args)


def _scores_bias(s, dist, slope):
    return s if slope is None else s - slope * dist.astype(F32)


def _flash_fwd(q, k, v, *, scale, tile, window=None, c_col=None, c_row=None, prm=None, name):
    H, S, dqk = q.shape
    G = H // k.shape[0]
    dv = v.shape[2]
    tq = tk = tile
    has_c, has_p = c_col is not None, prm is not None

    def body(*refs):
        q_ref, k_ref, v_ref = refs[:3]
        pos = 3
        if has_c:
            cc_ref, cr_ref = refs[pos], refs[pos + 1]
            pos += 2
        if has_p:
            p_ref = refs[pos]
            pos += 1
        o_ref, lse_ref = refs[pos], refs[pos + 1]
        i = pl.program_id(1)
        qv = q_ref[0]
        row = i * tq + lax.broadcasted_iota(jnp.int32, (tq, tk), 0)
        if has_p:
            sink = p_ref[0, :, 0:1]
            slope = p_ref[0, :, 1:2]
            m0 = jnp.zeros((tq, 1), F32) + sink
            l0 = jnp.ones((tq, 1), F32)
        else:
            slope = None
            m0 = jnp.full((tq, 1), NEG, F32)
            l0 = jnp.zeros((tq, 1), F32)
        lo = 0 if window is None else jnp.maximum((i * tq - (window - 1)) // tk, 0)
        hi = ((i + 1) * tq - 1) // tk

        def step(j, carry):
            m, l, acc = carry
            off = pl.multiple_of(j * tk, tk)
            kj = k_ref[0, pl.ds(off, tk), :]
            vj = v_ref[0, pl.ds(off, tk), :]
            s = lax.dot_general(qv, kj, (((1,), (1,)), ((), ())), preferred_element_type=F32) * scale
            if has_c:
                s = s + cc_ref[0] - cr_ref[0, :, pl.ds(off, tk)]
            dist = row - (j * tk + lax.broadcasted_iota(jnp.int32, (tq, tk), 1))
            s = _scores_bias(s, dist, slope)
            mask = dist >= 0
            if window is not None:
                mask = mask & (dist < window)
            s = jnp.where(mask, s, NEG)
            m_new = jnp.maximum(m, jnp.max(s, axis=-1, keepdims=True))
            a = jnp.exp(m - m_new)
            pr = jnp.exp(s - m_new)
            l = a * l + jnp.sum(pr, axis=-1, keepdims=True)
            acc = a * acc + jnp.dot(pr.astype(BF16), vj, preferred_element_type=F32)
            return m_new, l, acc

        m, l, acc = lax.fori_loop(lo, hi + 1, step, (m0, l0, jnp.zeros((tq, dv), F32)))
        o_ref[0] = (acc / l).astype(o_ref.dtype)
        lse_ref[0] = m + jnp.log(l)

    in_specs = [
        pl.BlockSpec((1, tq, dqk), lambda h, i: (h, i, 0)),
        pl.BlockSpec((1, S, dqk), lambda h, i: (h // G, 0, 0)),
        pl.BlockSpec((1, S, dv), lambda h, i: (h // G, 0, 0)),
    ]
    args = [q, k, v]
    if has_c:
        in_specs += [pl.BlockSpec((1, tq, 1), lambda h, i: (h, i, 0)), pl.BlockSpec((1, 1, S), lambda h, i: (h, 0, 0))]
        args += [c_col, c_row]
    if has_p:
        in_specs += [pl.BlockSpec((1, 1, LANES), lambda h, i: (h, 0, 0))]
        args += [prm]
    return pl.pallas_call(
        body, name=name, grid=(H, S // tq), in_specs=in_specs,
        out_specs=[pl.BlockSpec((1, tq, dv), lambda h, i: (h, i, 0)), pl.BlockSpec((1, tq, 1), lambda h, i: (h, i, 0))],
        out_shape=[jax.ShapeDtypeStruct((H, S, dv), F32), jax.ShapeDtypeStruct((H, S, 1), F32)],
    )(*args)


def _flash_bwd(q, k, v, o, do, lse_row, *, scale, tile, window=None, c_col=None, c_row=None, prm=None, name):
    H, S, dqk = q.shape
    G = H // k.shape[0]
    dv = v.shape[2]
    tq = tk = tile
    nq = S // tq
    has_c, has_p = c_col is not None, prm is not None

    def split3(x):
        hi = x.astype(BF16)
        r1 = x - hi.astype(F32)
        mid = r1.astype(BF16)
        lo = (r1 - mid.astype(F32)).astype(BF16)
        return hi, mid, lo

    def body(*refs):
        q_ref, k_ref, v_ref, o_ref, do_ref, lse_ref = refs[:6]
        pos = 6
        if has_c:
            cc_ref, cr_ref = refs[pos], refs[pos + 1]
            pos += 2
        if has_p:
            p_ref = refs[pos]
            pos += 1
        dq_ref, dk_ref, dv_ref = refs[pos: pos + 3]
        pos += 3
        if has_c:
            dc_ref, dcq_ref = refs[pos], refs[pos + 1]
            pos += 2
        if has_p:
            ds_ref = refs[pos]
            pos += 1
        delta = refs[pos]
        j = pl.program_id(1)

        @pl.when(j == 0)
        def _():
            dq_ref[...] = jnp.zeros_like(dq_ref)
            if has_c:
                dcq_ref[...] = jnp.zeros_like(dcq_ref)
            ones = jnp.ones((SUBLANES, dv), BF16)

            def dstep(i, tot):
                off = pl.multiple_of(i * tq, tq)
                x = do_ref[0, pl.ds(off, tq), :].astype(F32) * o_ref[0, pl.ds(off, tq), :]
                d = jnp.zeros((SUBLANES, tq), F32)
                for part in split3(x):
                    d = d + lax.dot_general(ones, part, (((1,), (1,)), ((), ())), preferred_element_type=F32)
                drow = d[0:1, :]
                delta[:, pl.ds(off, tq)] = drow
                if has_p:
                    w = jnp.exp(p_ref[0, :, 0:1] - lse_ref[0, :, pl.ds(off, tq)])
                    tot = tot - jnp.sum(w * drow, axis=1, keepdims=True)
                return tot

            tot = lax.fori_loop(0, nq, dstep, jnp.zeros((1, 1), F32))
            if has_p:
                ds_ref[0] = jnp.zeros((1, LANES), F32) + tot

        kj = k_ref[0]
        vj = v_ref[0]
        slope = p_ref[0, :, 1:2] if has_p else None
        kidx = j * tk + lax.broadcasted_iota(jnp.int32, (tk, tq), 0)
        lo = (j * tk) // tq
        hi = nq - 1 if window is None else jnp.minimum(nq - 1, ((j + 1) * tk + window - 2) // tq)

        def step(i, carry):
            dk, dvv, dc = carry
            off = pl.multiple_of(i * tq, tq)
            qi = q_ref[0, pl.ds(off, tq), :]
            doi = do_ref[0, pl.ds(off, tq), :]
            st = lax.dot_general(kj, qi, (((1,), (1,)), ((), ())), preferred_element_type=F32) * scale
            if has_c:
                st = st + cr_ref[0, :, pl.ds(off, tq)] - cc_ref[0]
            dist = (i * tq + lax.broadcasted_iota(jnp.int32, (tk, tq), 1)) - kidx
            st = _scores_bias(st, dist, slope)
            mask = dist >= 0
            if window is not None:
                mask = mask & (dist < window)
            st = jnp.where(mask, st, NEG)
            pt = jnp.exp(st - lse_ref[0, :, pl.ds(off, tq)])
            dvv = dvv + jnp.dot(pt.astype(BF16), doi, preferred_element_type=F32)
            dpt = lax.dot_general(vj, doi, (((1,), (1,)), ((), ())), preferred_element_type=F32)
            dst = pt * (dpt - delta[:, pl.ds(off, tq)])
            if has_c:
                dc = dc - jnp.sum(dst, axis=1, keepdims=True)
                dcq_ref[0, :, pl.ds(off, tq)] += jnp.sum(dst, axis=0, keepdims=True)
            dsb = dst.astype(BF16)
            dk = dk + jnp.dot(dsb, qi, preferred_element_type=F32)
            dq_ref[0, pl.ds(off, tq), :] += lax.dot_general(dsb, kj, (((0,), (0,)), ((), ())), preferred_element_type=F32) * scale
            return dk, dvv, dc

        dk, dvv, dc = lax.fori_loop(lo, hi + 1, step, (jnp.zeros((tk, dqk), F32), jnp.zeros((tk, dv), F32), jnp.zeros((tk, 1), F32)))
        dk_ref[0] = dk * scale
        dv_ref[0] = dvv
        if has_c:
            dc_ref[0] = dc

    whole = lambda d: pl.BlockSpec((1, S, d), lambda h, j: (h, 0, 0))
    in_specs = [
        whole(dqk),
        pl.BlockSpec((1, tk, dqk), lambda h, j: (h // G, j, 0)),
        pl.BlockSpec((1, tk, dv), lambda h, j: (h // G, j, 0)),
        whole(dv), whole(dv),
        pl.BlockSpec((1, 1, S), lambda h, j: (h, 0, 0)),
    ]
    args = [q, k, v, o, do, lse_row]
    if has_c:
        in_specs += [pl.BlockSpec((1, tk, 1), lambda h, j: (h, j, 0)), pl.BlockSpec((1, 1, S), lambda h, j: (h, 0, 0))]
        args += [c_col, c_row]
    if has_p:
        in_specs += [pl.BlockSpec((1, 1, LANES), lambda h, j: (h, 0, 0))]
        args += [prm]
    out_specs = [whole(dqk), pl.BlockSpec((1, tk, dqk), lambda h, j: (h, j, 0)), pl.BlockSpec((1, tk, dv), lambda h, j: (h, j, 0))]
    out_shape = [jax.ShapeDtypeStruct((H, S, dqk), F32), jax.ShapeDtypeStruct((H, S, dqk), F32), jax.ShapeDtypeStruct((H, S, dv), F32)]
    if has_c:
        out_specs += [pl.BlockSpec((1, tk, 1), lambda h, j: (h, j, 0)), pl.BlockSpec((1, 1, S), lambda h, j: (h, 0, 0))]
        out_shape += [jax.ShapeDtypeStruct((H, S, 1), F32), jax.ShapeDtypeStruct((H, 1, S), F32)]
    if has_p:
        out_specs += [pl.BlockSpec((1, 1, LANES), lambda h, j: (h, 0, 0))]
        out_shape += [jax.ShapeDtypeStruct((H, 1, LANES), F32)]
    return pl.pallas_call(
        body, name=name, grid=(H, S // tk), in_specs=in_specs, out_specs=out_specs, out_shape=out_shape,
        scratch_shapes=[pltpu.VMEM((1, S), F32)],
        compiler_params=pltpu.CompilerParams(dimension_semantics=("parallel", "arbitrary")),
    )(*args)


def _heads(x, h):
    S = x.shape[0]
    return jnp.transpose(x.reshape(S, h, -1), (1, 0, 2))


def _unheads(x):
    h, S, d = x.shape
    return jnp.transpose(x, (1, 0, 2)).reshape(S, h * d)


def _coords():
    return lax.axis_index("x"), lax.axis_index("y"), lax.axis_index("c")


def _peer(axis):
    x, y, c = _coords()
    return {"x": (1 - x, y, c), "y": (x, 1 - y, c), "c": (x, y, 1 - c)}[axis]


HBM_SPEC = pl.BlockSpec(memory_space=pl.ANY)


def _gather_pair(v, axis, name):
    def body(v_ref, o_ref, send_sem, recv_sem, loc_sem):
        me = lax.axis_index(axis)
        loc = pltpu.make_async_copy(v_ref, o_ref.at[me], loc_sem)
        loc.start()
        out = pltpu.make_async_remote_copy(v_ref, o_ref.at[me], send_sem, recv_sem, device_id=_peer(axis), device_id_type=MESH)
        out.start()
        out.wait_send()
        pltpu.make_async_remote_copy(v_ref, o_ref.at[1 - me], send_sem, recv_sem, device_id=_peer(axis), device_id_type=MESH).wait_recv()
        loc.wait()

    return pl.pallas_call(
        body, name=name, in_specs=[HBM_SPEC], out_specs=HBM_SPEC, out_shape=jax.ShapeDtypeStruct((2,) + v.shape, v.dtype),
        scratch_shapes=[pltpu.SemaphoreType.DMA(()), pltpu.SemaphoreType.DMA(()), pltpu.SemaphoreType.DMA(())],
    )(v)


def _scatter_pair(v, axis, name):
    def body(v_ref, o_ref, send_sem, recv_sem):
        me = lax.axis_index(axis)
        cp = pltpu.make_async_remote_copy(v_ref.at[1 - me], o_ref, send_sem, recv_sem, device_id=_peer(axis), device_id_type=MESH)
        cp.start()
        cp.wait()

    return pl.pallas_call(
        body, name=name, in_specs=[HBM_SPEC], out_specs=HBM_SPEC, out_shape=jax.ShapeDtypeStruct(v.shape[1:], v.dtype),
        scratch_shapes=[pltpu.SemaphoreType.DMA(()), pltpu.SemaphoreType.DMA(())],
    )(v)


def _add_kept(v, got, axis, out, name):
    _, R, C = v.shape
    tm = _divisor(R, max(16, EW_TILE_BYTES // (C * (v.dtype.itemsize + got.dtype.itemsize + jnp.dtype(out).itemsize)) // 16 * 16), 16)
    me = lax.axis_index(axis).astype(jnp.int32).reshape(1)

    def body(me_ref, v_ref, g_ref, o_ref):
        o_ref[...] = (v_ref[0].astype(F32) + g_ref[...].astype(F32)).astype(o_ref.dtype)

    return pl.pallas_call(
        body, name=name, out_shape=jax.ShapeDtypeStruct((R, C), out),
        grid_spec=pltpu.PrefetchScalarGridSpec(
            num_scalar_prefetch=1, grid=(R // tm,),
            in_specs=[pl.BlockSpec((1, tm, C), lambda i, me: (me[0], i, 0)), pl.BlockSpec((tm, C), lambda i, me: (i, 0))],
            out_specs=pl.BlockSpec((tm, C), lambda i, me: (i, 0))),
    )(me, v, got)


def _all_gather(v):
    for axis in ("y", "x", "c"):
        v = _gather_pair(v, axis, f"all_gather_{axis}")
        v = v.reshape(-1, v.shape[-1])
    return v.reshape(N_DEV, -1, v.shape[-1])


def _reduce_scatter(v):
    C = v.shape[-1]
    for axis, wire, out in (("c", F32, BF16), ("x", BF16, BF16), ("y", BF16, F32)):
        v = v.reshape(2, -1, C)
        assert v.dtype == wire
        got = _scatter_pair(v, axis, f"reduce_scatter_{axis}")
        v = _add_kept(v, got, axis, out, f"reduce_scatter_add_{axis}")
    return v


def _all_reduce_small(v):
    def body(v_ref, o_ref, buf, send_sems, recv_sems):
        x, y, c = _coords()
        me = 4 * x + 2 * y + c
        buf[me] = v_ref[...]
        copies = []
        for k in range(1, N_DEV):
            peer = tuple((1 - a) if (k >> s) & 1 else a for a, s in ((x, 2), (y, 1), (c, 0)))
            cp = pltpu.make_async_remote_copy(v_ref, buf.at[me], send_sems.at[k - 1], recv_sems.at[k - 1], device_id=peer, device_id_type=MESH)
            cp.start()
            copies.append(cp)
        for cp in copies:
            cp.wait()
        acc = buf[0]
        for d in range(1, N_DEV):
            acc = acc + buf[d]
        o_ref[...] = acc

    vm = pl.BlockSpec(memory_space=pltpu.VMEM)
    return pl.pallas_call(
        body, name="all_reduce_small", in_specs=[vm], out_specs=vm, out_shape=jax.ShapeDtypeStruct(v.shape, F32),
        scratch_shapes=[pltpu.VMEM((N_DEV,) + v.shape, F32), pltpu.SemaphoreType.DMA((N_DEV - 1,)), pltpu.SemaphoreType.DMA((N_DEV - 1,))],
    )(v)


PACK_COLS = 1024


def _pack_rows(n):
    return -(-n // (PACK_COLS * 256)) * 256


def _pack_local(shards, dtype):
    flat = jnp.concatenate([shards[n].astype(dtype).reshape(-1) for n in BIG])
    rows = _pack_rows(flat.shape[0])
    return jnp.pad(flat, (0, rows * PACK_COLS - flat.shape[0])).reshape(rows, PACK_COLS)


def _unpack_full(gathered, shard_shapes):
    g = gathered.reshape(N_DEV, -1)
    full, off = {}, 0
    for n, axis in BIG.items():
        shp = shard_shapes[n]
        size = int(np.prod(shp))
        seg = g[:, off: off + size].reshape(2, 2, 2, *shp)
        seg = jnp.transpose(seg, (1, 2, 0, 3, 4, 5)).reshape(N_DEV, *shp)
        if axis == 1:
            full[n] = jnp.transpose(seg, (1, 0, 2, 3)).reshape(shp[0], N_DEV * shp[1], shp[2])
        else:
            full[n] = jnp.transpose(seg, (1, 2, 0, 3)).reshape(shp[0], shp[1], N_DEV * shp[2])
        off += size
    return full


def _pack_grads(grads, shard_shapes):
    parts = []
    for n, axis in BIG.items():
        shp = shard_shapes[n]
        g = grads[n]
        if axis == 1:
            seg = jnp.transpose(g.reshape(shp[0], N_DEV, shp[1], shp[2]), (1, 0, 2, 3))
        else:
            seg = jnp.transpose(g.reshape(shp[0], shp[1], N_DEV, shp[2]), (2, 0, 1, 3))
        seg = jnp.transpose(seg.reshape(2, 2, 2, *shp), (2, 0, 1, 3, 4, 5)).reshape(N_DEV, -1)
        parts.append(seg)
    flat = jnp.concatenate(parts, axis=1)
    rows = _pack_rows(flat.shape[1])
    return jnp.pad(flat, ((0, 0), (0, rows * PACK_COLS - flat.shape[1]))).reshape(N_DEV, rows, PACK_COLS)


def _unpack_local(flat2d, shard_shapes):
    flat = flat2d.reshape(-1)
    out, off = {}, 0
    for n in BIG:
        shp = shard_shapes[n]
        size = int(np.prod(shp))
        out[n] = flat[off: off + size].reshape(shp)
        off += size
    return out


def _ffn_fwd(h, norm_w, w_gu, w_d, tag):
    n = _rms_fwd(h, norm_w, f"{tag}_norm")
    gu = _mm(n, w_gu, out=BF16, name=f"{tag}_gate_up")
    act = _swiglu_fwd(gu, f"{tag}_swiglu")
    out = _mm(act, w_d, res=h, alpha=0.5, name=f"{tag}_down")
    return out, (h, n, gu, act)


def _ffn_bwd(dout, saved, norm_w, w_gu, w_d, tag):
    h, n, gu, act = saved
    dact = _mm(dout, w_d, tb=True, alpha=0.5, name=f"{tag}_down_dx")
    dw_d = _mm(act, dout, ta=True, alpha=0.5, name=f"{tag}_down_dw")
    dgu = _swiglu_bwd(gu, dact, f"{tag}_swiglu_bwd")
    dw_gu = _mm(n, dgu, ta=True, name=f"{tag}_gate_up_dw")
    dn = _mm(dgu, w_gu, tb=True, name=f"{tag}_gate_up_dx")
    dh, dnorm = _rms_bwd(dn, h, norm_w, dout, f"{tag}_norm_bwd")
    return dh, dnorm, dw_gu, dw_d


def _rope_tables(S):
    inv = ROPE_THETA ** (-jnp.arange(0, B_ROPE, 2, dtype=F32) / B_ROPE)
    ang = jnp.arange(S, dtype=F32)[:, None] * inv[None, :]
    return jnp.cos(ang), jnp.sin(ang)


def _uq_perm():
    half = B_ROPE // 2
    per = B_NOPE + B_ROPE
    nope = [h * per + d for h in range(B_HEADS) for d in range(B_NOPE)]
    x1 = [h * per + B_NOPE + e for h in range(B_HEADS) for e in range(half)]
    x2 = [h * per + B_NOPE + half + e for h in range(B_HEADS) for e in range(half)]
    return np.array(nope + x1 + x2)


def _ukv_perm():
    per = B_NOPE + B_V
    kn = [h * per + d for h in range(B_HEADS) for d in range(B_NOPE)]
    vv = [h * per + B_NOPE + d for h in range(B_HEADS) for d in range(B_V)]
    return np.array(kn + vv)


def _alibi_prm(sinks):
    slopes = 2.0 ** (-8.0 * jnp.arange(1, A_HEADS + 1, dtype=F32) / A_HEADS)
    prm = jnp.zeros((A_HEADS, 1, LANES), F32)
    return prm.at[:, 0, 0].set(sinks.astype(F32)).at[:, 0, 1].set(slopes)


def _even_fwd(hn, h, W):
    S = hn.shape[0]
    proj = _mm(hn, W["ev_w_in"], name="ev_in")
    a_q, a_k, a_v = proj[:, :512], proj[:, 512:640], proj[:, 640:768]
    c_q, c_kv = proj[:, 768:1024], proj[:, 1024:1152]
    kr1, kr2 = proj[:, 1152:1168], proj[:, 1168:1184]
    qa, ka, va = _heads(a_q.astype(BF16), A_HEADS), _heads(a_k.astype(BF16), A_KV_HEADS), _heads(a_v.astype(BF16), A_KV_HEADS)
    prm = _alibi_prm(W["ev_sinks"][0])
    oa, lse_a = _flash_fwd(qa, ka, va, scale=A_HEAD_DIM ** -0.5, tile=128, window=WINDOW, prm=prm, name="swa_fwd")
    cqn = _rms_fwd(c_q, W["ev_cq_norm"], "ev_cq_norm")
    q_all = _mm(cqn, W["ev_w_uq"], name="ev_uq")
    ckvn = _rms_fwd(c_kv, W["ev_ckv_norm"], "ev_ckv_norm")
    kv_all = _mm(ckvn, W["ev_w_ukv"], name="ev_ukv")
    cos, sin = _rope_tables(S)
    cos8, sin8 = jnp.tile(cos, (1, B_HEADS)), jnp.tile(sin, (1, B_HEADS))
    q1, q2 = _rope(q_all[:, 512:640], q_all[:, 640:768], cos8, sin8, "ev_rope_q")
    k1, k2 = _rope(kr1, kr2, cos, sin, "ev_rope_k")
    half = B_ROPE // 2
    qb = jnp.concatenate([q_all[:, :512].reshape(S, B_HEADS, B_NOPE), q1.reshape(S, B_HEADS, half), q2.reshape(S, B_HEADS, half)], axis=-1)
    qb = jnp.transpose(qb.astype(BF16), (1, 0, 2))
    kro = jnp.broadcast_to(jnp.concatenate([k1, k2], axis=1)[:, None, :], (S, B_HEADS, B_ROPE))
    kb = jnp.transpose(jnp.concatenate([kv_all[:, :512].reshape(S, B_HEADS, B_NOPE), kro], axis=-1).astype(BF16), (1, 0, 2))
    vb = _heads(kv_all[:, 512:].astype(BF16), B_HEADS)
    ob, lse_b = _flash_fwd(qb, kb, vb, scale=(B_NOPE + B_ROPE) ** -0.5, tile=256, name="mla_fwd")
    cat = jnp.concatenate([_unheads(oa), _unheads(ob)], axis=1)
    out = _mm(cat, W["ev_w_out"], res=h, name="ev_out")
    return out, (hn, proj, qa, ka, va, prm, oa, lse_a, cqn, ckvn, qb, kb, vb, ob, lse_b, cat)


def _even_bwd(dout, saved, W):
    hn, proj, qa, ka, va, prm, oa, lse_a, cqn, ckvn, qb, kb, vb, ob, lse_b, cat = saved
    S = hn.shape[0]
    G = {}
    dcat = _mm(dout, W["ev_w_out"], tb=True, out=BF16, name="ev_out_dx")
    G["ev_w_out"] = _mm(cat, dout, ta=True, name="ev_out_dw")
    doa = _heads(dcat[:, :512], A_HEADS)
    dqa, dka, dva, dsink = _flash_bwd(qa, ka, va, oa, doa, lse_a.reshape(A_HEADS, 1, S), scale=A_HEAD_DIM ** -0.5, tile=128,
                                      window=WINDOW, prm=prm, name="swa_bwd")
    G["ev_sinks"] = dsink[:, 0, 0]
    dka = dka.reshape(A_KV_HEADS, A_GROUP, S, A_HEAD_DIM).sum(axis=1)
    dva = dva.reshape(A_KV_HEADS, A_GROUP, S, A_HEAD_DIM).sum(axis=1)
    dob = _heads(dcat[:, 512:], B_HEADS)
    dqb, dkb, dvb = _flash_bwd(qb, kb, vb, ob, dob, lse_b.reshape(B_HEADS, 1, S), scale=(B_NOPE + B_ROPE) ** -0.5, tile=256, name="mla_bwd")
    half = B_ROPE // 2
    dqb = jnp.transpose(dqb, (1, 0, 2))
    dkb = jnp.transpose(dkb, (1, 0, 2))
    cos, sin = _rope_tables(S)
    cos8, sin8 = jnp.tile(cos, (1, B_HEADS)), jnp.tile(sin, (1, B_HEADS))
    dq1, dq2 = _rope(dqb[:, :, B_NOPE:B_NOPE + half].reshape(S, -1), dqb[:, :, B_NOPE + half:].reshape(S, -1), cos8, -sin8, "ev_rope_q_bwd")
    dq_all = jnp.concatenate([dqb[:, :, :B_NOPE].reshape(S, -1), dq1, dq2], axis=1).astype(BF16)
    dkr = dkb[:, :, B_NOPE:].sum(axis=1)
    dk1, dk2 = _rope(dkr[:, :half], dkr[:, half:], cos, -sin, "ev_rope_k_bwd")
    dkv_all = jnp.concatenate([dkb[:, :, :B_NOPE].reshape(S, -1), _unheads(dvb)], axis=1).astype(BF16)
    G["ev_w_uq"] = _mm(cqn, dq_all, ta=True, name="ev_uq_dw")
    dcqn = _mm(dq_all, W["ev_w_uq"], tb=True, name="ev_uq_dx")
    dc_q, G["ev_cq_norm"] = _rms_bwd(dcqn, proj[:, 768:1024], W["ev_cq_norm"], None, "ev_cq_norm_bwd")
    G["ev_w_ukv"] = _mm(ckvn, dkv_all, ta=True, name="ev_ukv_dw")
    dckvn = _mm(dkv_all, W["ev_w_ukv"], tb=True, name="ev_ukv_dx")
    dc_kv, G["ev_ckv_norm"] = _rms_bwd(dckvn, proj[:, 1024:1152], W["ev_ckv_norm"], None, "ev_ckv_norm_bwd")
    dproj = jnp.concatenate([_unheads(dqa), _unheads(dka), _unheads(dva), dc_q, dc_kv, dk1, dk2,
                             jnp.zeros((S, EVEN_IN_PAD - EVEN_IN), F32)], axis=1).astype(BF16)
    G["ev_w_in"] = _mm(hn, dproj, ta=True, name="ev_in_dw")
    dhn = _mm(dproj, W["ev_w_in"], tb=True, name="ev_in_dx")
    return dhn, G


def _odd_fwd(hn, h, W):
    S = hn.shape[0]
    w = C_HEADS * C_HEAD_DIM
    proj = _mm(hn, W["od_w_in"], name="od_in")
    q, k, v = (_heads(proj[:, i * w:(i + 1) * w].astype(BF16), C_HEADS) for i in range(3))
    f_logit = proj[:, 3 * w: 3 * w + C_HEADS]
    logf = _logsig_fwd(f_logit, W["od_b_f"], "od_logsig")
    logc = _cumsum(logf, False, "od_cumsum")
    c_col = jnp.transpose(logc)[:, :, None]
    c_row = jnp.transpose(logc)[:, None, :]
    o, lse = _flash_fwd(q, k, v, scale=C_HEAD_DIM ** -0.5, tile=256, c_col=c_col, c_row=c_row, name="fox_fwd")
    cat = _unheads(o)
    out = _mm(cat, W["od_w_out"], res=h, name="od_out")
    return out, (hn, q, k, v, f_logit, c_col, c_row, o, lse, cat)


def _odd_bwd(dout, saved, W):
    hn, q, k, v, f_logit, c_col, c_row, o, lse, cat = saved
    S = hn.shape[0]
    G = {}
    dcat = _mm(dout, W["od_w_out"], tb=True, out=BF16, name="od_out_dx")
    G["od_w_out"] = _mm(cat, dout, ta=True, name="od_out_dw")
    do = _heads(dcat, C_HEADS)
    dq, dk, dv, dc, dcq = _flash_bwd(q, k, v, o, do, lse.reshape(C_HEADS, 1, S), scale=C_HEAD_DIM ** -0.5, tile=256,
                                     c_col=c_col, c_row=c_row, name="fox_bwd")
    dlogc = jnp.transpose(dc[:, :, 0] + dcq[:, 0, :])
    dlogf = _cumsum(dlogc, True, "od_cumsum_bwd")
    df, db = _logsig_bwd(dlogf, f_logit, W["od_b_f"], "od_logsig_bwd")
    G["od_b_f"] = db
    dproj = jnp.concatenate([_unheads(dq), _unheads(dk), _unheads(dv), df, jnp.zeros((S, ODD_IN_PAD - ODD_IN), F32)], axis=1).astype(BF16)
    G["od_w_in"] = _mm(hn, dproj, ta=True, name="od_in_dw")
    dhn = _mm(dproj, W["od_w_in"], tb=True, name="od_in_dx")
    return dhn, G


def _local_step(x, p, target, W):
    h = x
    saved = []
    for i in range(DEPTH):
        t = f"l{i}"
        h1, s_a = _ffn_fwd(h, W["ffa_norm"][i:i + 1], W["ffa_w_gate_up"][i], W["ffa_w_down"][i], f"{t}_ffa")
        nm = _rms_fwd(h1, W["mix_norm"][i:i + 1], f"{t}_mix_norm")
        h2, s_m = (_even_fwd if i % 2 == 0 else _odd_fwd)(nm, h1, W)
        h3, s_b = _ffn_fwd(h2, W["ffb_norm"][i:i + 1], W["ffb_w_gate_up"][i], W["ffb_w_down"][i], f"{t}_ffb")
        npl = _rms_fwd(h3, W["ple_norm"][i:i + 1], f"{t}_ple_norm")
        gpre = _mm(npl, W["ple_w_gate"][i], name=f"{t}_ple_gate")
        pp = _mm(p[i], W["ple_w_proj"][i], name=f"{t}_ple_proj")
        h4 = _ple_fwd(h3, gpre, pp, f"{t}_ple")
        saved.append((s_a, h1, s_m, s_b, h3, npl, gpre, pp))
        h = h4
    dh, g_final, loss_cols = _final_fwd_bwd(h, W["final_norm"], target, "final")
    G = {"final_norm": g_final}
    per_layer = {n: [None] * DEPTH for n in ("ffa_norm", "ffa_w_gate_up", "ffa_w_down", "mix_norm", "ffb_norm", "ffb_w_gate_up",
                                             "ffb_w_down", "ple_norm", "ple_w_gate", "ple_w_proj")}
    for i in reversed(range(DEPTH)):
        t = f"l{i}"
        s_a, h1, s_m, s_b, h3, npl, gpre, pp = saved[i]
        dgpre, dpp = _ple_bwd(dh, gpre, pp, f"{t}_ple_bwd")
        per_layer["ple_w_proj"][i] = _mm(p[i], dpp, ta=True, name=f"{t}_ple_proj_dw")
        per_layer["ple_w_gate"][i] = _mm(npl, dgpre, ta=True, name=f"{t}_ple_gate_dw")
        dnpl = _mm(dgpre, W["ple_w_gate"][i], tb=True, name=f"{t}_ple_gate_dx")
        dh, per_layer["ple_norm"][i] = _rms_bwd(dnpl, h3, W["ple_norm"][i:i + 1], dh, f"{t}_ple_norm_bwd")
        dh, per_layer["ffb_norm"][i], per_layer["ffb_w_gate_up"][i], per_layer["ffb_w_down"][i] = _ffn_bwd(
            dh, s_b, W["ffb_norm"][i:i + 1], W["ffb_w_gate_up"][i], W["ffb_w_down"][i], f"{t}_ffb")
        dnm, g_mix = (_even_bwd if i % 2 == 0 else _odd_bwd)(dh, s_m, W)
        G.update(g_mix)
        dh, per_layer["mix_norm"][i] = _rms_bwd(dnm, h1, W["mix_norm"][i:i + 1], dh, f"{t}_mix_norm_bwd")
        dh, per_layer["ffa_norm"][i], per_layer["ffa_w_gate_up"][i], per_layer["ffa_w_down"][i] = _ffn_bwd(
            dh, s_a, W["ffa_norm"][i:i + 1], W["ffa_w_gate_up"][i], W["ffa_w_down"][i], f"{t}_ffa")
    for n, parts in per_layer.items():
        G[n] = jnp.concatenate(parts, axis=0) if parts[0].shape[0] == 1 else jnp.stack(parts)
    return loss_cols, dh, G


def _prepare_weights(full, small):
    W = dict(small)
    for n in ("ffa_w_gate_up", "ffa_w_down", "ffb_w_gate_up", "ffb_w_down", "ple_w_gate", "ple_w_proj"):
        W[n] = full[n]
    W["ev_w_in"] = jnp.pad(full["ev_w_in"][0], ((0, 0), (0, EVEN_IN_PAD - EVEN_IN)))
    W["ev_w_uq"] = full["ev_w_uq"][0][:, _uq_perm()]
    W["ev_w_ukv"] = full["ev_w_ukv"][0][:, _ukv_perm()]
    W["ev_w_out"] = full["ev_w_out"][0]
    W["od_w_in"] = jnp.pad(full["od_w_in"][0], ((0, 0), (0, ODD_IN_PAD - ODD_IN)))
    W["od_w_out"] = full["od_w_out"][0]
    return W


def _finish_grads(G):
    out = dict(G)
    out["ev_w_in"] = G["ev_w_in"][None, :, :EVEN_IN]
    out["ev_w_uq"] = G["ev_w_uq"][:, np.argsort(_uq_perm())][None]
    out["ev_w_ukv"] = G["ev_w_ukv"][:, np.argsort(_ukv_perm())][None]
    out["ev_w_out"] = G["ev_w_out"][None]
    out["od_w_in"] = G["od_w_in"][None, :, :ODD_IN]
    out["od_w_out"] = G["od_w_out"][None]
    return out


def _small_layout(arrs):
    return [(n, int(np.prod(arrs[n].shape))) for n in SMALL]


def kernel(x, p, ffa_norm, ffa_w_gate_up, ffa_w_down, mix_norm, ffb_norm, ffb_w_gate_up, ffb_w_down, ple_norm, ple_w_gate, ple_w_proj, ev_w_in, ev_sinks, ev_cq_norm, ev_w_uq, ev_ckv_norm, ev_w_ukv, ev_w_out, od_w_in, od_b_f, od_w_out, final_norm, loss_target, m_ffa_norm, m_ffa_w_gate_up, m_ffa_w_down, m_mix_norm, m_ffb_norm, m_ffb_w_gate_up, m_ffb_w_down, m_ple_norm, m_ple_w_gate, m_ple_w_proj, m_ev_w_in, m_ev_sinks, m_ev_cq_norm, m_ev_w_uq, m_ev_ckv_norm, m_ev_w_ukv, m_ev_w_out, m_od_w_in, m_od_b_f, m_od_w_out, m_final_norm, v_ffa_norm, v_ffa_w_gate_up, v_ffa_w_down, v_mix_norm, v_ffb_norm, v_ffb_w_gate_up, v_ffb_w_down, v_ple_norm, v_ple_w_gate, v_ple_w_proj, v_ev_w_in, v_ev_sinks, v_ev_cq_norm, v_ev_w_uq, v_ev_ckv_norm, v_ev_w_ukv, v_ev_w_out, v_od_w_in, v_od_b_f, v_od_w_out, v_final_norm):
    given = dict(locals())
    w_in = {n: given[n] for n in WEIGHTS}
    shard_shapes = {n: w_in[n].shape for n in BIG}

    gathered = _all_gather(_pack_local(w_in, BF16))
    full = _unpack_full(gathered, shard_shapes)
    small = {n: w_in[n] for n in SMALL}
    small["final_norm"] = final_norm.reshape(1, -1)
    W = _prepare_weights(full, small)

    loss_cols, dx, G = _local_step(x[0], p[:, 0], loss_target[0], W)
    G = _finish_grads(G)

    g_big = _unpack_local(_reduce_scatter(_pack_grads(G, shard_shapes)), shard_shapes)
    layout = _small_layout(w_in)
    vec = jnp.concatenate([G[n].astype(F32).reshape(-1) for n, _ in layout] + [jnp.sum(loss_cols).reshape(1)])
    vec = jnp.pad(vec, (0, N_DEV * SMALL_COLS - vec.shape[0])).reshape(N_DEV, SMALL_COLS)
    vec = _all_reduce_small(vec).reshape(-1)
    grads, off = dict(g_big), 0
    for n, size in layout:
        grads[n] = vec[off: off + size].reshape(w_in[n].shape)
        off += size
    loss = vec[off]

    delta, new_m, new_v = {}, {}, {}
    for n in WEIGHTS:
        shp = w_in[n].shape
        as2d = (lambda a: a.reshape(1, -1)) if len(shp) == 1 else (lambda a: a)
        d, nm, nv = _adamw(as2d(w_in[n]), as2d(grads[n]), as2d(given["m_" + n]), as2d(given["v_" + n]), f"adamw_{n}")
        delta[n], new_m[n], new_v[n] = d.reshape(shp), nm.reshape(shp), nv.reshape(shp)
    return (loss, dx[None], *[grads[n] for n in WEIGHTS], *[delta[n] for n in WEIGHTS],
            *[new_m[n] for n in WEIGHTS], *[new_v[n] for n in WEIGHTS])
```

```python
import functools

import numpy as np
import jax
import jax.numpy as jnp
from jax import lax
from jax.experimental import pallas as pl
from jax.experimental.pallas import tpu as pltpu

F32 = jnp.float32
BF16 = jnp.bfloat16
MESH = pl.DeviceIdType.MESH

D_MODEL = 1024
D_FF = 2816
RMS_EPS = 1e-6
PLE_DIM = 256
A_HEADS, A_KV_HEADS, A_HEAD_DIM, WINDOW = 8, 2, 64, 128
A_GROUP = A_HEADS // A_KV_HEADS
B_HEADS, B_Q_LORA, B_KV_LORA, B_NOPE, B_ROPE, B_V = 8, 256, 128, 64, 32, 64
ROPE_THETA = 10000.0
C_HEADS, C_HEAD_DIM = 16, 64
EVEN_IN = 1184
EVEN_IN_PAD = 1280
ODD_IN = 3088
ODD_IN_PAD = 3200
DEPTH = 2
ADAM_LR, ADAM_B1, ADAM_B2, ADAM_EPS, ADAM_WD, ADAM_STEP = 0.001, 0.9, 0.999, 1e-08, 0.01, 10

N_DEV = 8
LANES = 128
SUBLANES = 8
EW_TILE_BYTES = 3 << 20
MM_VMEM_BYTES = 26 << 20
NEG = -1e30
QK_PAD = 72

FF_BLK = D_FF // 4
DOWN_ROWS = D_FF // N_DEV
G1_ROWS, G2_ROWS, G3_ROWS, G3_COLS = 1920, 4096, 1024, 768
G1_128 = {"ple_w_gate0": 11, "ple_w_gate1": 12, "ev_w_out": 13, "od_w_out": 14}
OD_C, EV_C, STRIP_C = 386, 148, 128
STRIP0 = OD_C + EV_C

SMALL = ["ffa_norm", "mix_norm", "ffb_norm", "ple_norm", "ev_sinks", "ev_cq_norm", "ev_ckv_norm", "od_b_f", "final_norm"]
WEIGHTS = ["ffa_norm", "ffa_w_gate_up", "ffa_w_down", "mix_norm", "ffb_norm", "ffb_w_gate_up", "ffb_w_down", "ple_norm",
           "ple_w_gate", "ple_w_proj", "ev_w_in", "ev_sinks", "ev_cq_norm", "ev_w_uq", "ev_ckv_norm", "ev_w_ukv", "ev_w_out",
           "od_w_in", "od_b_f", "od_w_out", "final_norm"]
SMALL_COLS = 1280


def _divisor(n, cap, mult):
    if n <= cap:
        return n
    for t in range(cap - cap % mult, 0, -mult):
        if n % t == 0:
            return t
    raise ValueError(f"no tile for {n} under {cap} in steps of {mult}")


def _lanes(c):
    return -(-c // LANES) * LANES


def _ew(fn, rows, vecs, outs, reds=(), *, name):
    R = rows[0].shape[0]
    per_row = sum(_lanes(a.shape[1]) * a.dtype.itemsize for a in rows) + sum(_lanes(c) * jnp.dtype(d).itemsize for c, d in outs)
    tm = _divisor(R, max(16, EW_TILE_BYTES // per_row // 16 * 16), 16) if R % 16 == 0 else R
    n_r, n_v, n_o = len(rows), len(vecs), len(outs)

    def body(*refs):
        ins = [r[...] for r in refs[: n_r + n_v]]
        res = fn(*ins)
        if not isinstance(res, (tuple, list)):
            res = (res,)
        o_refs = refs[n_r + n_v: n_r + n_v + n_o]
        r_refs = refs[n_r + n_v + n_o:]
        for ref, val in zip(o_refs, res[:n_o]):
            ref[...] = val.astype(ref.dtype)
        if r_refs:
            @pl.when(pl.program_id(0) == 0)
            def _():
                for ref in r_refs:
                    ref[...] = jnp.zeros_like(ref)
            for ref, val in zip(r_refs, res[n_o:]):
                ref[...] += val

    in_specs = [pl.BlockSpec((tm, a.shape[1]), lambda i: (i, 0)) for a in rows]
    in_specs += [pl.BlockSpec((1, a.shape[1]), lambda i: (0, 0)) for a in vecs]
    out_specs = [pl.BlockSpec((tm, c), lambda i: (i, 0)) for c, _ in outs]
    out_specs += [pl.BlockSpec((1, c), lambda i: (0, 0)) for c in reds]
    out_shape = [jax.ShapeDtypeStruct((R, c), d) for c, d in outs] + [jax.ShapeDtypeStruct((1, c), F32) for c in reds]
    res = pl.pallas_call(body, name=name, grid=(R // tm,), in_specs=in_specs, out_specs=out_specs, out_shape=out_shape)(*rows, *vecs)
    return res[0] if len(res) == 1 else res


def _rms_fwd(x, w, name):
    def fn(x, w):
        y = x * lax.rsqrt(jnp.mean(x * x, axis=-1, keepdims=True) + RMS_EPS)
        return y * w
    return _ew(fn, [x], [w], [(x.shape[1], BF16)], name=name)


def _rms_bwd(dn, x, w, dres, name):
    def fn(dn, x, *rest):
        w = rest[-1]
        r = lax.rsqrt(jnp.mean(x * x, axis=-1, keepdims=True) + RMS_EPS)
        xh = x * r
        gw = dn * w
        dx = r * (gw - xh * jnp.mean(gw * xh, axis=-1, keepdims=True))
        if len(rest) == 2:
            dx = dx + rest[0]
        return dx, jnp.sum(dn * xh, axis=0, keepdims=True)
    rows = [dn, x] + ([dres] if dres is not None else [])
    return _ew(fn, rows, [w], [(x.shape[1], F32)], [x.shape[1]], name=name)


def _ple_fwd(h, gpre, pp, name):
    return _ew(lambda h, g, q: h + jax.nn.sigmoid(g) * q, [h, gpre, pp], [], [(h.shape[1], F32)], name=name)


def _ple_bwd(dh, gpre, pp, name):
    def fn(dh, g, q):
        sg = jax.nn.sigmoid(g)
        return dh * q * (sg * (1.0 - sg)), dh * sg
    return _ew(fn, [dh, gpre, pp], [], [(dh.shape[1], BF16), (dh.shape[1], BF16)], name=name)


def _rope(x1, x2, cos, sin, name):
    c = x1.shape[1]
    return _ew(lambda a, b, co, si: (a * co - b * si, a * si + b * co), [x1, x2, cos, sin], [], [(c, F32), (c, F32)], name=name)


def _logsig_fwd(f, b, name):
    def fn(f, b):
        z = f + b
        return jnp.minimum(z, 0.0) - jnp.log(1.0 + jnp.exp(-jnp.abs(z)))
    return _ew(fn, [f], [b], [(f.shape[1], F32)], name=name)


def _logsig_bwd(dlogf, f, b, name):
    def fn(d, f, b):
        df = d * jax.nn.sigmoid(-(f + b))
        return df, jnp.sum(df, axis=0, keepdims=True)
    return _ew(fn, [dlogf, f], [b], [(f.shape[1], F32)], [f.shape[1]], name=name)


def _final_fwd_bwd(h, w, target, name):
    d = h.shape[1]

    def fn(h, t, w):
        r = lax.rsqrt(jnp.mean(h * h, axis=-1, keepdims=True) + RMS_EPS)
        xh = h * r
        y = xh * w
        err = y - t
        dy = err * (1.0 / d)
        gw = dy * w
        dx = r * (gw - xh * jnp.mean(gw * xh, axis=-1, keepdims=True))
        return dx, jnp.sum(dy * xh, axis=0, keepdims=True), jnp.sum(err * err, axis=0, keepdims=True) * (0.5 / d)
    return _ew(fn, [h, target], [w], [(d, F32)], [d, d], name=name)


def _adamw(w, g, m, v, name):
    shape = w.shape
    c = shape[-1]
    w2, g2, m2, v2 = (a.reshape(-1, c) for a in (w, g, m, v))

    def fn(w, g, m, v):
        m = ADAM_B1 * m + (1.0 - ADAM_B1) * g
        v = ADAM_B2 * v + (1.0 - ADAM_B2) * jnp.square(g)
        m_hat = m / (1.0 - ADAM_B1 ** ADAM_STEP)
        v_hat = v / (1.0 - ADAM_B2 ** ADAM_STEP)
        delta = -ADAM_LR * (m_hat / (jnp.sqrt(v_hat) + ADAM_EPS) + ADAM_WD * w)
        return delta, m, v
    d, nm, nv = _ew(fn, [w2, g2, m2, v2], [], [(c, F32)] * 3, name=name)
    return d.reshape(shape), nm.reshape(shape), nv.reshape(shape)


def _split3(v):
    hi = v.astype(BF16)
    r1 = v - hi.astype(F32)
    mid = r1.astype(BF16)
    lo = (r1 - mid.astype(F32)).astype(BF16)
    return hi, mid, lo


def _cumsum(x, reverse, name):
    S, C = x.shape
    tm = _divisor(S, 512, 16)
    nt = S // tm

    def body(x_ref, o_ref, carry):
        @pl.when(pl.program_id(0) == 0)
        def _():
            carry[...] = jnp.zeros_like(carry)
        r = lax.broadcasted_iota(jnp.int32, (tm, tm), 0)
        c = lax.broadcasted_iota(jnp.int32, (tm, tm), 1)
        tri = jnp.where((c >= r) if reverse else (c <= r), 1.0, 0.0).astype(BF16)
        xv = x_ref[...]
        acc = jnp.zeros((tm, C), F32)
        for part in _split3(xv):
            acc = acc + jnp.dot(tri, part, preferred_element_type=F32)
        o_ref[...] = acc + carry[...]
        carry[...] += jnp.sum(xv, axis=0, keepdims=True)

    idx = (lambda i: (nt - 1 - i, 0)) if reverse else (lambda i: (i, 0))
    return pl.pallas_call(
        body, name=name, grid=(nt,), in_specs=[pl.BlockSpec((tm, C), idx)], out_specs=pl.BlockSpec((tm, C), idx),
        out_shape=jax.ShapeDtypeStruct((S, C), F32), scratch_shapes=[pltpu.VMEM((1, C), F32)],
    )(x)


NN = (((1,), (0,)), ((), ()))
NT = (((1,), (1,)), ((), ()))
TN = (((0,), (0,)), ((), ()))


def _mm_call(name, grid, k_axis, a, a_spec, a2d, b, b_spec, b2d, dims, out_sds, out_spec, o2d, *,
             alpha=1.0, res=None, res_spec=None, into=None):
    nk = grid[k_axis]

    def body(*refs):
        a_ref, b_ref = refs[0], refs[1]
        res_ref = refs[2] if res is not None else None
        o_ref, acc_ref = refs[-2], refs[-1]
        k = pl.program_id(k_axis)

        @pl.when(k == 0)
        def _():
            acc_ref[...] = jnp.zeros_like(acc_ref)

        av = a_ref[...].reshape(a2d).astype(BF16)
        bv = b_ref[...].reshape(b2d).astype(BF16)
        acc_ref[...] += lax.dot_general(av, bv, dims, preferred_element_type=F32)

        @pl.when(k == nk - 1)
        def _():
            r = acc_ref[...]
            if alpha != 1.0:
                r = r * alpha
            if res_ref is not None:
                r = res_ref[...].reshape(o2d) + r
            o_ref[...] = r.reshape(o_ref.shape).astype(o_ref.dtype)

    in_specs, args = [a_spec, b_spec], [a, b]
    if res is not None:
        in_specs.append(res_spec)
        args.append(res)
    aliases = {}
    if into is not None:
        aliases = {len(args): 0}
        in_specs.append(pl.BlockSpec(memory_space=pl.ANY))
        args.append(into)
        out_sds = jax.ShapeDtypeStruct(into.shape, into.dtype)
    sem = tuple("arbitrary" if d == k_axis else "parallel" for d in range(len(grid)))
    return pl.pallas_call(
        body, name=name, grid=grid, in_specs=in_specs, out_specs=out_spec, out_shape=out_sds,
        scratch_shapes=[pltpu.VMEM(o2d, F32)], input_output_aliases=aliases,
        compiler_params=pltpu.CompilerParams(dimension_semantics=sem),
    )(*args)


def _mm(a, b, *, ta=False, tb=False, out=F32, res=None, alpha=1.0, name):
    K, M = a.shape if ta else a.shape[::-1]
    N = b.shape[0] if tb else b.shape[1]
    assert (b.shape[1] if tb else b.shape[0]) == K, (a.shape, b.shape, ta, tb)
    tk = _divisor(K, 1024, LANES)
    tn = _divisor(N, 1408, LANES)
    for cap in (1024, 512, 256, 128):
        tm = _divisor(M, cap, LANES if ta else 16)
        est = 2 * (tm * tk * a.dtype.itemsize + tk * tn * b.dtype.itemsize + tm * tn * jnp.dtype(out).itemsize)
        est += tm * tn * 4 + (2 * tm * tn * 4 if res is not None else 0)
        if est <= MM_VMEM_BYTES:
            break
    a_spec = pl.BlockSpec((tk, tm), lambda i, j, k: (k, i)) if ta else pl.BlockSpec((tm, tk), lambda i, j, k: (i, k))
    b_spec = pl.BlockSpec((tn, tk), lambda i, j, k: (j, k)) if tb else pl.BlockSpec((tk, tn), lambda i, j, k: (k, j))
    o_spec = pl.BlockSpec((tm, tn), lambda i, j, k: (i, j))
    dims = (((0 if ta else 1,), (1 if tb else 0,)), ((), ()))
    return _mm_call(name, (M // tm, N // tn, K // tk), 2, a, a_spec, (tk, tm) if ta else (tm, tk), b, b_spec,
                    (tn, tk) if tb else (tk, tn), dims, jax.ShapeDtypeStruct((M, N), out), o_spec, (tm, tn),
                    alpha=alpha, res=res, res_spec=o_spec)


def _w128_spec(blk):
    return pl.BlockSpec((N_DEV, 128, D_MODEL), lambda *_: (0, blk, 0))


def _mm_w128(a, G1, blk, *, tb=False, res=None, out=F32, name):
    S = a.shape[0]
    tm = _divisor(S, 512, 16)
    row = pl.BlockSpec((tm, D_MODEL), lambda i, k: (i, 0))
    return _mm_call(name, (S // tm, 1), 1, a, row, (tm, D_MODEL), G1, _w128_spec(blk), (D_MODEL, D_MODEL), NT if tb else NN,
                    jax.ShapeDtypeStruct((S, D_MODEL), out), row, (tm, D_MODEL), res=res, res_spec=row)


def _mm_w128_dw(a, b, blk, into, name):
    S = a.shape[0]
    tk = _divisor(S, 1024, 16)
    row = pl.BlockSpec((tk, D_MODEL), lambda i, k: (k, 0))
    return _mm_call(name, (1, S // tk), 1, a, row, (tk, D_MODEL), b, row, (tk, D_MODEL), TN, None, _w128_spec(blk),
                    (D_MODEL, D_MODEL), into=into)


def _ffn_gate_up(n, G2v, rb, name):
    S = n.shape[0]
    tm = _divisor(S, 512, 16)

    def body(n_ref, w_ref, gu_ref, act_ref):
        nv = n_ref[...]
        g = jnp.dot(nv, w_ref[0, 0], preferred_element_type=F32)
        u = jnp.dot(nv, w_ref[1, 0], preferred_element_type=F32)
        gu_ref[0, 0] = g.astype(BF16)
        gu_ref[1, 0] = u.astype(BF16)
        act_ref[0] = (g * jax.nn.sigmoid(g) * u).astype(BF16)

    return pl.pallas_call(
        body, name=name, grid=(4, S // tm),
        in_specs=[pl.BlockSpec((tm, D_MODEL), lambda j, i: (i, 0)), pl.BlockSpec((2, 1, D_MODEL, FF_BLK), lambda j, i: (0, j, rb, 0))],
        out_specs=[pl.BlockSpec((2, 1, tm, FF_BLK), lambda j, i: (0, j, i, 0)), pl.BlockSpec((1, tm, FF_BLK), lambda j, i: (j, i, 0))],
        out_shape=[jax.ShapeDtypeStruct((2, 4, S, FF_BLK), BF16), jax.ShapeDtypeStruct((4, S, FF_BLK), BF16)],
    )(n, G2v)


def _ffn_down(act, G1, ob, h, name):
    S = h.shape[0]
    tm = _divisor(S, 512, 16)
    row = pl.BlockSpec((tm, D_MODEL), lambda i, k: (i, 0))
    return _mm_call(name, (S // tm, 4), 1, act, pl.BlockSpec((1, tm, FF_BLK), lambda i, k: (k, i, 0)), (tm, FF_BLK),
                    G1, pl.BlockSpec((2, DOWN_ROWS, D_MODEL), lambda i, k: (k, ob, 0)), (FF_BLK, D_MODEL), NN,
                    jax.ShapeDtypeStruct((S, D_MODEL), F32), row, (tm, D_MODEL), alpha=0.5, res=h, res_spec=row)


def _ffn_down_dx(dh, G1, ob, gu, name):
    S = dh.shape[0]
    tm = _divisor(S, 512, 16)

    def body(dh_ref, w_ref, gu_ref, o_ref):
        w = w_ref[...].reshape(FF_BLK, D_MODEL)
        dact = lax.dot_general(dh_ref[...].astype(BF16), w, NT, preferred_element_type=F32) * 0.5
        g = gu_ref[0, 0].astype(F32)
        u = gu_ref[1, 0].astype(F32)
        sg = jax.nn.sigmoid(g)
        o_ref[0, 0] = (dact * u * (sg * (1.0 + g * (1.0 - sg)))).astype(BF16)
        o_ref[1, 0] = (dact * (g * sg)).astype(BF16)

    blk = pl.BlockSpec((2, 1, tm, FF_BLK), lambda j, i: (0, j, i, 0))
    return pl.pallas_call(
        body, name=name, grid=(4, S // tm),
        in_specs=[pl.BlockSpec((tm, D_MODEL), lambda j, i: (i, 0)), pl.BlockSpec((2, DOWN_ROWS, D_MODEL), lambda j, i: (j, ob, 0)), blk],
        out_specs=blk, out_shape=jax.ShapeDtypeStruct((2, 4, S, FF_BLK), BF16),
    )(dh, G1, gu)


def _ffn_down_dw(act, dh, ob, into, name):
    S = dh.shape[0]
    tk = _divisor(S, 1024, 16)
    return _mm_call(name, (4, S // tk), 1, act, pl.BlockSpec((1, tk, FF_BLK), lambda j, k: (j, k, 0)), (tk, FF_BLK),
                    dh, pl.BlockSpec((tk, D_MODEL), lambda j, k: (k, 0)), (tk, D_MODEL), TN, None,
                    pl.BlockSpec((2, DOWN_ROWS, D_MODEL), lambda j, k: (j, ob, 0)), (FF_BLK, D_MODEL), alpha=0.5, into=into)


def _ffn_gate_up_dw(n, dgu8, rb, into, name):
    S = n.shape[0]
    tk = _divisor(S, 1024, 16)
    return _mm_call(name, (N_DEV, S // tk), 1, n, pl.BlockSpec((tk, D_MODEL), lambda b, k: (k, 0)), (tk, D_MODEL),
                    dgu8, pl.BlockSpec((1, tk, FF_BLK), lambda b, k: (b, k, 0)), (tk, FF_BLK), TN, None,
                    pl.BlockSpec((1, D_MODEL, FF_BLK), lambda b, k: (b, rb, 0)), (D_MODEL, FF_BLK), into=into)


def _ffn_gate_up_dx(dgu8, G2, rb, name):
    S = dgu8.shape[1]
    tm = _divisor(S, 512, 16)
    row = pl.BlockSpec((tm, D_MODEL), lambda i, k: (i, 0))
    return _mm_call(name, (S // tm, N_DEV), 1, dgu8, pl.BlockSpec((1, tm, FF_BLK), lambda i, k: (k, i, 0)), (tm, FF_BLK),
                    G2, pl.BlockSpec((1, D_MODEL, FF_BLK), lambda i, k: (k, rb, 0)), (D_MODEL, FF_BLK), NT,
                    jax.ShapeDtypeStruct((S, D_MODEL), F32), row, (tm, D_MODEL))


def _flash_fwd(q, k, v, *, tile, window=None, sink=None, name):
    H, S, dqk = q.shape
    G = H // k.shape[0]
    dv = v.shape[2]
    tq = tk = tile

    def body(*refs):
        q_ref, k_ref, v_ref = refs[:3]
        o_ref, lse_ref = refs[-2], refs[-1]
        i = pl.program_id(1)
        qv = q_ref[0]
        if sink is not None:
            m0 = jnp.zeros((tq, 1), F32) + refs[3][0, :, 0:1]
            l0 = jnp.ones((tq, 1), F32)
        else:
            m0 = jnp.full((tq, 1), NEG, F32)
            l0 = jnp.zeros((tq, 1), F32)

        def step(j, carry, masked):
            m, l, acc = carry
            off = pl.multiple_of(j * tk, tk)
            kj = k_ref[0, pl.ds(off, tk), :]
            vj = v_ref[0, pl.ds(off, tk), :]
            s = lax.dot_general(qv, kj, NT, preferred_element_type=F32)
            if masked:
                dist = (i * tq + lax.broadcasted_iota(jnp.int32, (tq, tk), 0)) - (j * tk + lax.broadcasted_iota(jnp.int32, (tq, tk), 1))
                mask = dist >= 0
                if window is not None:
                    mask = mask & (dist < window)
                s = jnp.where(mask, s, NEG)
            m_new = jnp.maximum(m, jnp.max(s, axis=-1, keepdims=True))
            a = jnp.exp(m - m_new)
            pr = jnp.exp(s - m_new)
            l = a * l + jnp.sum(pr, axis=-1, keepdims=True)
            acc = a * acc + jnp.dot(pr.astype(BF16), vj, preferred_element_type=F32)
            return m_new, l, acc

        carry = (m0, l0, jnp.zeros((tq, dv), F32))
        if window is None:
            carry = lax.fori_loop(0, i, functools.partial(step, masked=False), carry)
            carry = step(i, carry, True)
        else:
            lo = jnp.maximum((i * tq - (window - 1)) // tk, 0)
            carry = lax.fori_loop(lo, i + 1, functools.partial(step, masked=True), carry)
        m, l, acc = carry
        o_ref[0] = acc / l
        lse_ref[0] = m + jnp.log(l)

    in_specs = [
        pl.BlockSpec((1, tq, dqk), lambda h, i: (h, i, 0)),
        pl.BlockSpec((1, S, dqk), lambda h, i: (h // G, 0, 0)),
        pl.BlockSpec((1, S, dv), lambda h, i: (h // G, 0, 0)),
    ]
    args = [q, k, v]
    if sink is not None:
        in_specs += [pl.BlockSpec((1, 1, LANES), lambda h, i: (h, 0, 0))]
        args += [sink]
    return pl.pallas_call(
        body, name=name, grid=(H, S // tq), in_specs=in_specs,
        out_specs=[pl.BlockSpec((1, tq, dv), lambda h, i: (h, i, 0)), pl.BlockSpec((1, tq, 1), lambda h, i: (h, i, 0))],
        out_shape=[jax.ShapeDtypeStruct((H, S, dv), F32), jax.ShapeDtypeStruct((H, S, 1), F32)],
    )(*args)


def _flash_bwd(q, k, v, o, do, lse_row, *, tile, window=None, sink=None, name):
    H, S, dqk = q.shape
    G = H // k.shape[0]
    dv = v.shape[2]
    tq = tk = tile
    nq = S // tq
    has_p = sink is not None

    def body(*refs):
        q_ref, k_ref, v_ref, o_ref, do_ref, lse_ref = refs[:6]
        p_ref = refs[6] if has_p else None
        pos = 7 if has_p else 6
        dq_ref, dk_ref, dv_ref = refs[pos: pos + 3]
        ds_ref = refs[pos + 3] if has_p else None
        delta = refs[-1]
        j = pl.program_id(1)

        @pl.when(j == 0)
        def _():
            dq_ref[...] = jnp.zeros_like(dq_ref)
            ones = jnp.ones((SUBLANES, dv), BF16)

            def dstep(i, tot):
                off = pl.multiple_of(i * tq, tq)
                x = do_ref[0, pl.ds(off, tq), :].astype(F32) * o_ref[0, pl.ds(off, tq), :]
                d = jnp.zeros((SUBLANES, tq), F32)
                for part in _split3(x):
                    d = d + lax.dot_general(ones, part, NT, preferred_element_type=F32)
                drow = d[0:1, :]
                delta[:, pl.ds(off, tq)] = drow
                if has_p:
                    w = jnp.exp(p_ref[0, :, 0:1] - lse_ref[0, :, pl.ds(off, tq)])
                    tot = tot - jnp.sum(w * drow, axis=1, keepdims=True)
                return tot

            tot = lax.fori_loop(0, nq, dstep, jnp.zeros((1, 1), F32))
            if has_p:
                ds_ref[0] = jnp.zeros((1, LANES), F32) + tot

        kj = k_ref[0]
        vj = v_ref[0]

        def step(i, carry, masked):
            dk, dvv = carry
            off = pl.multiple_of(i * tq, tq)
            qi = q_ref[0, pl.ds(off, tq), :]
            doi = do_ref[0, pl.ds(off, tq), :]
            st = lax.dot_general(kj, qi, NT, preferred_element_type=F32)
            if masked:
                dist = (i * tq + lax.broadcasted_iota(jnp.int32, (tk, tq), 1)) - (j * tk + lax.broadcasted_iota(jnp.int32, (tk, tq), 0))
                mask = dist >= 0
                if window is not None:
                    mask = mask & (dist < window)
                st = jnp.where(mask, st, NEG)
            pt = jnp.exp(st - lse_ref[0, :, pl.ds(off, tq)])
            dvv = dvv + jnp.dot(pt.astype(BF16), doi, preferred_element_type=F32)
            dpt = lax.dot_general(vj, doi, NT, preferred_element_type=F32)
            dsb = (pt * (dpt - delta[:, pl.ds(off, tq)])).astype(BF16)
            dk = dk + jnp.dot(dsb, qi, preferred_element_type=F32)
            dq_ref[0, pl.ds(off, tq), :] += lax.dot_general(dsb, kj, TN, preferred_element_type=F32)
            return dk, dvv

        carry = (jnp.zeros((tk, dqk), F32), jnp.zeros((tk, dv), F32))
        if window is None:
            carry = step(j, carry, True)
            carry = lax.fori_loop(j + 1, nq, functools.partial(step, masked=False), carry)
        else:
            hi = jnp.minimum(nq - 1, ((j + 1) * tk + window - 2) // tq)
            carry = lax.fori_loop(j, hi + 1, functools.partial(step, masked=True), carry)
        dk_ref[0] = carry[0]
        dv_ref[0] = carry[1]

    whole = lambda d: pl.BlockSpec((1, S, d), lambda h, j: (h, 0, 0))
    in_specs = [
        whole(dqk),
        pl.BlockSpec((1, tk, dqk), lambda h, j: (h // G, j, 0)),
        pl.BlockSpec((1, tk, dv), lambda h, j: (h // G, j, 0)),
        whole(dv), whole(dv),
        pl.BlockSpec((1, 1, S), lambda h, j: (h, 0, 0)),
    ]
    args = [q, k, v, o, do, lse_row]
    if has_p:
        in_specs += [pl.BlockSpec((1, 1, LANES), lambda h, j: (h, 0, 0))]
        args += [sink]
    out_specs = [whole(dqk), pl.BlockSpec((1, tk, dqk), lambda h, j: (h, j, 0)), pl.BlockSpec((1, tk, dv), lambda h, j: (h, j, 0))]
    out_shape = [jax.ShapeDtypeStruct((H, S, dqk), F32), jax.ShapeDtypeStruct((H, S, dqk), F32), jax.ShapeDtypeStruct((H, S, dv), F32)]
    if has_p:
        out_specs += [pl.BlockSpec((1, 1, LANES), lambda h, j: (h, 0, 0))]
        out_shape += [jax.ShapeDtypeStruct((H, 1, LANES), F32)]
    return pl.pallas_call(
        body, name=name, grid=(H, S // tk), in_specs=in_specs, out_specs=out_specs, out_shape=out_shape,
        scratch_shapes=[pltpu.VMEM((1, S), F32)],
        compiler_params=pltpu.CompilerParams(dimension_semantics=("parallel", "arbitrary")),
    )(*args)


def _heads(x, h):
    S = x.shape[0]
    return jnp.transpose(x.reshape(S, h, -1), (1, 0, 2))


def _unheads(x):
    h, S, d = x.shape
    return jnp.transpose(x, (1, 0, 2)).reshape(S, h * d)


def _exact3(v):
    rnd = lambda a: lax.reduce_precision(a, exponent_bits=8, mantissa_bits=7)
    hi = rnd(v)
    mid = rnd(v - hi)
    return hi, mid, rnd(v - hi - mid)


def _coords():
    return lax.axis_index("x"), lax.axis_index("y"), lax.axis_index("c")


def _peer(axis):
    x, y, c = _coords()
    return {"x": (1 - x, y, c), "y": (x, 1 - y, c), "c": (x, y, 1 - c)}[axis]


HBM_SPEC = pl.BlockSpec(memory_space=pl.ANY)


def _all_gather(bufs):
    n = len(bufs)

    def body(*refs):
        outs = refs[n: 2 * n]
        send_sems, recv_sems = refs[2 * n], refs[2 * n + 1]
        x, y, c = _coords()
        me, sibling = (x, y, c), (x, y, 1 - c)
        chips = [(1 - x, y), (x, 1 - y), (1 - x, 1 - y)]

        def copy(t, k, block, to):
            px, py, pc = block
            ref = outs[t].at[4 * px + 2 * py + pc]
            return pltpu.make_async_remote_copy(ref, ref, send_sems.at[7 * t + k], recv_sems.at[7 * t + k], device_id=to, device_id_type=MESH)

        first = []
        for t in range(n):
            first.append(copy(t, 0, me, sibling))
            first += [copy(t, 1 + j, me, (*chip, c)) for j, chip in enumerate(chips)]
        for cp in first:
            cp.start()
        passed = []
        for j, chip in enumerate(chips):
            for t in range(n):
                copy(t, 1 + j, (*chip, c), me).wait_recv()
                cp = copy(t, 4 + j, (*chip, c), sibling)
                cp.start()
                passed.append(cp)
        for t in range(n):
            copy(t, 0, sibling, me).wait_recv()
            for j, chip in enumerate(chips):
                copy(t, 4 + j, (*chip, 1 - c), me).wait_recv()
        for cp in first + passed:
            cp.wait_send()

    return pl.pallas_call(
        body, name="all_gather", in_specs=[HBM_SPEC] * n, out_specs=[HBM_SPEC] * n,
        out_shape=[jax.ShapeDtypeStruct(b.shape, b.dtype) for b in bufs], input_output_aliases={t: t for t in range(n)},
        scratch_shapes=[pltpu.SemaphoreType.DMA((7 * n,)), pltpu.SemaphoreType.DMA((7 * n,))],
    )(*bufs)


def _in_slot(local):
    x, y, c = _coords()
    buf = jnp.zeros((N_DEV,) + local.shape, local.dtype)
    return lax.dynamic_update_slice(buf, local[None], (4 * x + 2 * y + c, 0, 0))


def _scatter_pair(vs, axis, name):
    n = len(vs)

    def body(*refs):
        send_sems, recv_sems = refs[2 * n], refs[2 * n + 1]
        me = lax.axis_index(axis)
        copies = []
        for t in range(n):
            v_ref, o_ref = refs[t], refs[n + t]
            src = v_ref.at[1 - me] if len(v_ref.shape) == 3 else v_ref.at[:, 1 - me]
            cp = pltpu.make_async_remote_copy(src, o_ref, send_sems.at[t], recv_sems.at[t], device_id=_peer(axis), device_id_type=MESH)
            cp.start()
            copies.append(cp)
        for cp in copies:
            cp.wait()

    out_shape = [jax.ShapeDtypeStruct(v.shape[:-3] + v.shape[-2:], v.dtype) for v in vs]
    return pl.pallas_call(
        body, name=name, in_specs=[HBM_SPEC] * n, out_specs=[HBM_SPEC] * n, out_shape=out_shape,
        scratch_shapes=[pltpu.SemaphoreType.DMA((n,)), pltpu.SemaphoreType.DMA((n,))],
    )(*vs)


def _add_kept(v, got, axis, out, name):
    R, C = v.shape[-2:]
    lead = v.shape[0] if v.ndim == 4 else 1
    tm = _divisor(R, max(16, EW_TILE_BYTES // (_lanes(C) * (v.dtype.itemsize + got.dtype.itemsize + jnp.dtype(out).itemsize)) // 16 * 16), 16)
    me = lax.axis_index(axis).astype(jnp.int32).reshape(1)
    v4 = v.reshape(lead, 2, R, C)
    g3 = got.reshape(lead, R, C)

    def body(me_ref, v_ref, g_ref, o_ref):
        o_ref[...] = (v_ref[0].astype(F32) + g_ref[...].astype(F32)).astype(o_ref.dtype)

    res = pl.pallas_call(
        body, name=name, out_shape=jax.ShapeDtypeStruct((lead, R, C), out),
        grid_spec=pltpu.PrefetchScalarGridSpec(
            num_scalar_prefetch=1, grid=(lead, R // tm),
            in_specs=[pl.BlockSpec((1, 1, tm, C), lambda b, i, me: (b, me[0], i, 0)), pl.BlockSpec((1, tm, C), lambda b, i, me: (b, i, 0))],
            out_specs=pl.BlockSpec((1, tm, C), lambda b, i, me: (b, i, 0))),
    )(me, v4, g3)
    return res


def _reduce_scatter(gs):
    vs = [g.reshape(4, 2, *g.shape[1:]) for g in gs]
    got = _scatter_pair(vs, "c", "reduce_scatter_c")
    vs = [_add_kept(v, r, "c", BF16, f"reduce_scatter_add_c{t}") for t, (v, r) in enumerate(zip(vs, got))]
    vs = [v.reshape(2, 2 * v.shape[1], v.shape[2]) for v in vs]
    got = _scatter_pair(vs, "x", "reduce_scatter_x")
    vs = [_add_kept(v, r, "x", BF16, f"reduce_scatter_add_x{t}")[0] for t, (v, r) in enumerate(zip(vs, got))]
    vs = [v.reshape(2, v.shape[0] // 2, v.shape[1]) for v in vs]
    got = _scatter_pair(vs, "y", "reduce_scatter_y")
    return [_add_kept(v, r, "y", F32, f"reduce_scatter_add_y{t}")[0] for t, (v, r) in enumerate(zip(vs, got))]


def _all_reduce_small(v):
    def body(v_ref, o_ref, buf, send_sems, recv_sems):
        x, y, c = _coords()
        me = 4 * x + 2 * y + c
        buf[me] = v_ref[...]
        copies = []
        for k in range(1, N_DEV):
            peer = tuple((1 - a) if (k >> s) & 1 else a for a, s in ((x, 2), (y, 1), (c, 0)))
            cp = pltpu.make_async_remote_copy(v_ref, buf.at[me], send_sems.at[k - 1], recv_sems.at[k - 1], device_id=peer, device_id_type=MESH)
            cp.start()
            copies.append(cp)
        for cp in copies:
            cp.wait()
        acc = buf[0]
        for d in range(1, N_DEV):
            acc = acc + buf[d]
        o_ref[...] = acc

    vm = pl.BlockSpec(memory_space=pltpu.VMEM)
    return pl.pallas_call(
        body, name="all_reduce_small", in_specs=[vm], out_specs=vm, out_shape=jax.ShapeDtypeStruct(v.shape, F32),
        scratch_shapes=[pltpu.VMEM((N_DEV,) + v.shape, F32), pltpu.SemaphoreType.DMA((N_DEV - 1,)), pltpu.SemaphoreType.DMA((N_DEV - 1,))],
    )(v)


def _local_groups(w, dtype):
    g1 = jnp.concatenate([w["ffa_w_down"].reshape(-1, D_MODEL), w["ffb_w_down"].reshape(-1, D_MODEL),
                          w["ple_w_gate"].reshape(-1, D_MODEL), w["ev_w_out"][0], w["od_w_out"][0]], axis=0).astype(dtype)
    g2 = jnp.concatenate([w["ffa_w_gate_up"].reshape(-1, FF_BLK), w["ffb_w_gate_up"].reshape(-1, FF_BLK)], axis=0).astype(dtype)
    strip = jnp.concatenate([w["ple_w_proj"].reshape(-1, STRIP_C), w["ev_w_ukv"][0], jnp.pad(w["ev_w_uq"][0], ((0, 0), (0, STRIP_C - 96))),
                             jnp.zeros((G3_ROWS - 896, STRIP_C), F32)], axis=0)
    g3 = jnp.concatenate([w["od_w_in"][0], w["ev_w_in"][0], strip, jnp.zeros((G3_ROWS, G3_COLS - STRIP0 - STRIP_C), F32)], axis=1).astype(dtype)
    return g1, g2, g3


def _ungroup_local(r1, r2, r3):
    out = {
        "ffa_w_down": r1[:704].reshape(2, DOWN_ROWS, D_MODEL), "ffb_w_down": r1[704:1408].reshape(2, DOWN_ROWS, D_MODEL),
        "ple_w_gate": r1[1408:1664].reshape(2, 128, D_MODEL), "ev_w_out": r1[1664:1792][None], "od_w_out": r1[1792:1920][None],
        "ffa_w_gate_up": r2[:2048].reshape(2, D_MODEL, FF_BLK), "ffb_w_gate_up": r2[2048:].reshape(2, D_MODEL, FF_BLK),
        "od_w_in": r3[:, :OD_C][None], "ev_w_in": r3[:, OD_C:STRIP0][None],
    }
    strip = r3[:, STRIP0:STRIP0 + STRIP_C]
    out["ple_w_proj"] = strip[:512].reshape(2, PLE_DIM, STRIP_C)
    out["ev_w_ukv"] = strip[512:640][None]
    out["ev_w_uq"] = strip[640:896, :96][None]
    return out


def _cols(a):
    return jnp.transpose(a, (1, 0, 2)).reshape(a.shape[1], -1)


def _blocks(g, c):
    return jnp.transpose(g.reshape(g.shape[0], N_DEV, c), (1, 0, 2))


def _uq_permute(w):
    r = w.shape[0]
    w3 = w.reshape(r, B_HEADS, B_NOPE + B_ROPE)
    half = B_ROPE // 2
    return jnp.concatenate([w3[:, :, :B_NOPE].reshape(r, -1), w3[:, :, B_NOPE:B_NOPE + half].reshape(r, -1), w3[:, :, B_NOPE + half:].reshape(r, -1)], axis=1)


def _uq_unpermute(g):
    r = g.shape[0]
    half = B_ROPE // 2
    n = B_HEADS * B_NOPE
    parts = [g[:, :n].reshape(r, B_HEADS, B_NOPE), g[:, n:n + B_HEADS * half].reshape(r, B_HEADS, half), g[:, n + B_HEADS * half:].reshape(r, B_HEADS, half)]
    return jnp.concatenate(parts, axis=2).reshape(r, -1)


def _ukv_permute(w):
    r = w.shape[0]
    return jnp.transpose(w.reshape(r, B_HEADS, 2, B_NOPE), (0, 2, 1, 3)).reshape(r, -1)


def _ukv_unpermute(g):
    r = g.shape[0]
    return jnp.transpose(g.reshape(r, 2, B_HEADS, B_NOPE), (0, 2, 1, 3)).reshape(r, -1)


def _misc_weights(G3):
    strip = G3[:, :, STRIP0:STRIP0 + STRIP_C]
    return {
        "od_w_in": jnp.pad(_cols(G3[:, :, :OD_C]), ((0, 0), (0, ODD_IN_PAD - ODD_IN))),
        "ev_w_in": jnp.pad(_cols(G3[:, :, OD_C:STRIP0]), ((0, 0), (0, EVEN_IN_PAD - EVEN_IN))),
        "ple_w_proj": [_cols(strip[:, i * PLE_DIM:(i + 1) * PLE_DIM]) for i in range(DEPTH)],
        "ev_w_ukv": _ukv_permute(_cols(strip[:, 512:640])),
        "ev_w_uq": _uq_permute(_cols(strip[:, 640:896, :96])),
    }


def _misc_grads(G):
    strip = jnp.concatenate([
        _blocks(G["ple_w_proj"][0], STRIP_C), _blocks(G["ple_w_proj"][1], STRIP_C), _blocks(_ukv_unpermute(G["ev_w_ukv"]), STRIP_C),
        jnp.pad(_blocks(_uq_unpermute(G["ev_w_uq"]), 96), ((0, 0), (0, 0), (0, STRIP_C - 96))),
        jnp.zeros((N_DEV, G3_ROWS - 896, STRIP_C), F32)], axis=1)
    return jnp.concatenate([_blocks(G["od_w_in"][:, :ODD_IN], OD_C), _blocks(G["ev_w_in"][:, :EVEN_IN], EV_C), strip,
                            jnp.zeros((N_DEV, G3_ROWS, G3_COLS - STRIP0 - STRIP_C), F32)], axis=2)


def _ffn_fwd(h, norm_w, W, f, i, tag):
    n = _rms_fwd(h, norm_w, f"{tag}_norm")
    gu, act = _ffn_gate_up(n, W["G2v"], 2 * f + i, f"{tag}_gate_up")
    out = _ffn_down(act, W["G1"], 2 * f + i, h, f"{tag}_down")
    return out, (h, n, gu, act)


def _ffn_bwd(dout, saved, norm_w, W, GB, f, i, tag):
    h, n, gu, act = saved
    S = h.shape[0]
    blk = 2 * f + i
    GB["g1"] = _ffn_down_dw(act, dout, blk, GB["g1"], f"{tag}_down_dw")
    dgu = _ffn_down_dx(dout, W["G1"], blk, gu, f"{tag}_down_dx").reshape(N_DEV, S, FF_BLK)
    GB["g2"] = _ffn_gate_up_dw(n, dgu, blk, GB["g2"], f"{tag}_gate_up_dw")
    dn = _ffn_gate_up_dx(dgu, W["G2"], blk, f"{tag}_gate_up_dx")
    return _rms_bwd(dn, h, norm_w, dout, f"{tag}_norm_bwd")


def _rope_tables(S):
    inv = ROPE_THETA ** (-jnp.arange(0, B_ROPE, 2, dtype=F32) / B_ROPE)
    ang = jnp.arange(S, dtype=F32)[:, None] * inv[None, :]
    return jnp.cos(ang), jnp.sin(ang)


def _alibi_columns(S):
    t = jnp.arange(S, dtype=jnp.int32)
    hi = ((t // 16) * 16).astype(F32)
    lo = (t % 16).astype(F32)
    slopes = 2.0 ** (-8.0 * jnp.arange(1, A_HEADS + 1, dtype=F32) / A_HEADS)
    zq = jnp.zeros((S, A_HEADS), F32)
    qc = jnp.stack([-slopes[None, :] * hi[:, None], -slopes[None, :] * lo[:, None], zq + slopes[None, :], zq + slopes[None, :], zq, zq, zq, zq], axis=-1)
    one = jnp.ones((S, A_KV_HEADS), F32)
    zk = jnp.zeros((S, A_KV_HEADS), F32)
    kc = jnp.stack([one, one, zk + hi[:, None], zk + lo[:, None], zk, zk, zk, zk], axis=-1)
    return qc, kc


def _sink_prm(sinks):
    return jnp.zeros((A_HEADS, 1, LANES), F32).at[:, 0, 0].set(sinks.astype(F32))


def _even_fwd(hn, h, W):
    S = hn.shape[0]
    proj = _mm(hn, W["ev_w_in"], name="ev_in")
    a_q, a_k, a_v = proj[:, :512], proj[:, 512:640], proj[:, 640:768]
    c_q, c_kv = proj[:, 768:1024], proj[:, 1024:1152]
    kr1, kr2 = proj[:, 1152:1168], proj[:, 1168:1184]
    qc, kc = _alibi_columns(S)
    qa = jnp.concatenate([(a_q * A_HEAD_DIM ** -0.5).reshape(S, A_HEADS, A_HEAD_DIM), qc], axis=-1)
    qa = jnp.transpose(qa.astype(BF16), (1, 0, 2))
    ka = jnp.transpose(jnp.concatenate([a_k.reshape(S, A_KV_HEADS, A_HEAD_DIM), kc], axis=-1).astype(BF16), (1, 0, 2))
    va = _heads(a_v.astype(BF16), A_KV_HEADS)
    prm = _sink_prm(W["ev_sinks"][0])
    oa, lse_a = _flash_fwd(qa, ka, va, tile=128, window=WINDOW, sink=prm, name="swa_fwd")
    cqn = _rms_fwd(c_q, W["ev_cq_norm"], "ev_cq_norm")
    q_all = _mm(cqn, W["ev_w_uq"], name="ev_uq")
    ckvn = _rms_fwd(c_kv, W["ev_ckv_norm"], "ev_ckv_norm")
    kv_all = _mm(ckvn, W["ev_w_ukv"], name="ev_ukv")
    cos, sin = _rope_tables(S)
    cos8, sin8 = jnp.tile(cos, (1, B_HEADS)), jnp.tile(sin, (1, B_HEADS))
    q1, q2 = _rope(q_all[:, 512:640], q_all[:, 640:768], cos8, sin8, "ev_rope_q")
    k1, k2 = _rope(kr1, kr2, cos, sin, "ev_rope_k")
    half = B_ROPE // 2
    scale = (B_NOPE + B_ROPE) ** -0.5
    qb = jnp.concatenate([q_all[:, :512].reshape(S, B_HEADS, B_NOPE), q1.reshape(S, B_HEADS, half), q2.reshape(S, B_HEADS, half)], axis=-1) * scale
    qb = jnp.transpose(qb.astype(BF16), (1, 0, 2))
    kro = jnp.broadcast_to(jnp.concatenate([k1, k2], axis=1)[:, None, :], (S, B_HEADS, B_ROPE))
    kb = jnp.transpose(jnp.concatenate([kv_all[:, :512].reshape(S, B_HEADS, B_NOPE), kro], axis=-1).astype(BF16), (1, 0, 2))
    vb = _heads(kv_all[:, 512:].astype(BF16), B_HEADS)
    ob, lse_b = _flash_fwd(qb, kb, vb, tile=256, name="mla_fwd")
    cat = jnp.concatenate([_unheads(oa), _unheads(ob)], axis=1)
    out = _mm_w128(cat, W["G1"], G1_128["ev_w_out"], res=h, name="ev_out")
    return out, (hn, proj, qa, ka, va, prm, oa, lse_a, cqn, ckvn, qb, kb, vb, ob, lse_b, cat)


def _even_bwd(dout, saved, W, GB):
    hn, proj, qa, ka, va, prm, oa, lse_a, cqn, ckvn, qb, kb, vb, ob, lse_b, cat = saved
    S = hn.shape[0]
    G = {}
    dcat = _mm_w128(dout, W["G1"], G1_128["ev_w_out"], tb=True, out=BF16, name="ev_out_dx")
    GB["g1"] = _mm_w128_dw(cat, dout, G1_128["ev_w_out"], GB["g1"], "ev_out_dw")
    doa = _heads(dcat[:, :512], A_HEADS)
    dqa, dka, dva, dsink = _flash_bwd(qa, ka, va, oa, doa, lse_a.reshape(A_HEADS, 1, S), tile=128, window=WINDOW, sink=prm, name="swa_bwd")
    G["ev_sinks"] = dsink[:, 0, 0]
    dqa = dqa[:, :, :A_HEAD_DIM] * A_HEAD_DIM ** -0.5
    dka = dka[:, :, :A_HEAD_DIM].reshape(A_KV_HEADS, A_GROUP, S, A_HEAD_DIM).sum(axis=1)
    dva = dva.reshape(A_KV_HEADS, A_GROUP, S, A_HEAD_DIM).sum(axis=1)
    dob = _heads(dcat[:, 512:], B_HEADS)
    dqb, dkb, dvb = _flash_bwd(qb, kb, vb, ob, dob, lse_b.reshape(B_HEADS, 1, S), tile=256, name="mla_bwd")
    half = B_ROPE // 2
    dqb = jnp.transpose(dqb, (1, 0, 2)) * (B_NOPE + B_ROPE) ** -0.5
    dkb = jnp.transpose(dkb, (1, 0, 2))
    cos, sin = _rope_tables(S)
    cos8, sin8 = jnp.tile(cos, (1, B_HEADS)), jnp.tile(sin, (1, B_HEADS))
    dq1, dq2 = _rope(dqb[:, :, B_NOPE:B_NOPE + half].reshape(S, -1), dqb[:, :, B_NOPE + half:].reshape(S, -1), cos8, -sin8, "ev_rope_q_bwd")
    dq_all = jnp.concatenate([dqb[:, :, :B_NOPE].reshape(S, -1), dq1, dq2], axis=1).astype(BF16)
    dkr = dkb[:, :, B_NOPE:].sum(axis=1)
    dk1, dk2 = _rope(dkr[:, :half], dkr[:, half:], cos, -sin, "ev_rope_k_bwd")
    dkv_all = jnp.concatenate([dkb[:, :, :B_NOPE].reshape(S, -1), _unheads(dvb)], axis=1).astype(BF16)
    G["ev_w_uq"] = _mm(cqn, dq_all, ta=True, name="ev_uq_dw")
    dcqn = _mm(dq_all, W["ev_w_uq"], tb=True, name="ev_uq_dx")
    dc_q, G["ev_cq_norm"] = _rms_bwd(dcqn, proj[:, 768:1024], W["ev_cq_norm"], None, "ev_cq_norm_bwd")
    G["ev_w_ukv"] = _mm(ckvn, dkv_all, ta=True, name="ev_ukv_dw")
    dckvn = _mm(dkv_all, W["ev_w_ukv"], tb=True, name="ev_ukv_dx")
    dc_kv, G["ev_ckv_norm"] = _rms_bwd(dckvn, proj[:, 1024:1152], W["ev_ckv_norm"], None, "ev_ckv_norm_bwd")
    dproj = jnp.concatenate([_unheads(dqa), _unheads(dka), _unheads(dva), dc_q, dc_kv, dk1, dk2,
                             jnp.zeros((S, EVEN_IN_PAD - EVEN_IN), F32)], axis=1).astype(BF16)
    G["ev_w_in"] = _mm(hn, dproj, ta=True, name="ev_in_dw")
    dhn = _mm(dproj, W["ev_w_in"], tb=True, name="ev_in_dx")
    return dhn, G


def _odd_fwd(hn, h, W):
    S = hn.shape[0]
    w = C_HEADS * C_HEAD_DIM
    proj = _mm(hn, W["od_w_in"], name="od_in")
    f_logit = proj[:, 3 * w: 3 * w + C_HEADS]
    logf = _logsig_fwd(f_logit, W["od_b_f"], "od_logsig")
    logc = _cumsum(logf, False, "od_cumsum")
    parts = [p[:, :, None] for p in _exact3(logc)]
    ones = [jnp.ones((S, C_HEADS, 1), F32)] * 3
    pad = [jnp.zeros((S, C_HEADS, QK_PAD - C_HEAD_DIM - 6), F32)]
    q3 = (proj[:, :w] * C_HEAD_DIM ** -0.5).reshape(S, C_HEADS, C_HEAD_DIM)
    k3 = proj[:, w:2 * w].reshape(S, C_HEADS, C_HEAD_DIM)
    q = jnp.transpose(jnp.concatenate([q3] + parts + ones + pad, axis=-1).astype(BF16), (1, 0, 2))
    k = jnp.transpose(jnp.concatenate([k3] + ones + [-p for p in parts] + pad, axis=-1).astype(BF16), (1, 0, 2))
    v = _heads(proj[:, 2 * w:3 * w].astype(BF16), C_HEADS)
    o, lse = _flash_fwd(q, k, v, tile=256, name="fox_fwd")
    cat = _unheads(o)
    out = _mm_w128(cat, W["G1"], G1_128["od_w_out"], res=h, name="od_out")
    return out, (hn, q, k, v, f_logit, o, lse, cat)


def _odd_bwd(dout, saved, W, GB):
    hn, q, k, v, f_logit, o, lse, cat = saved
    S = hn.shape[0]
    G = {}
    dcat = _mm_w128(dout, W["G1"], G1_128["od_w_out"], tb=True, out=BF16, name="od_out_dx")
    GB["g1"] = _mm_w128_dw(cat, dout, G1_128["od_w_out"], GB["g1"], "od_out_dw")
    do = _heads(dcat, C_HEADS)
    dq, dk, dv = _flash_bwd(q, k, v, o, do, lse.reshape(C_HEADS, 1, S), tile=256, name="fox_bwd")
    dlogc = jnp.transpose(dq[:, :, C_HEAD_DIM] - dk[:, :, C_HEAD_DIM + 3])
    dlogf = _cumsum(dlogc, True, "od_cumsum_bwd")
    df, db = _logsig_bwd(dlogf, f_logit, W["od_b_f"], "od_logsig_bwd")
    G["od_b_f"] = db
    dproj = jnp.concatenate([_unheads(dq[:, :, :C_HEAD_DIM]) * C_HEAD_DIM ** -0.5, _unheads(dk[:, :, :C_HEAD_DIM]), _unheads(dv), df,
                             jnp.zeros((S, ODD_IN_PAD - ODD_IN), F32)], axis=1).astype(BF16)
    G["od_w_in"] = _mm(hn, dproj, ta=True, name="od_in_dw")
    dhn = _mm(dproj, W["od_w_in"], tb=True, name="od_in_dx")
    return dhn, G


def _local_step(x, p, target, W):
    h = x
    saved = []
    for i in range(DEPTH):
        t = f"l{i}"
        h1, s_a = _ffn_fwd(h, W["ffa_norm"][i:i + 1], W, 0, i, f"{t}_ffa")
        nm = _rms_fwd(h1, W["mix_norm"][i:i + 1], f"{t}_mix_norm")
        h2, s_m = (_even_fwd if i % 2 == 0 else _odd_fwd)(nm, h1, W)
        h3, s_b = _ffn_fwd(h2, W["ffb_norm"][i:i + 1], W, 1, i, f"{t}_ffb")
        npl = _rms_fwd(h3, W["ple_norm"][i:i + 1], f"{t}_ple_norm")
        gpre = _mm_w128(npl, W["G1"], G1_128[f"ple_w_gate{i}"], name=f"{t}_ple_gate")
        pp = _mm(p[i], W["ple_w_proj"][i], name=f"{t}_ple_proj")
        h4 = _ple_fwd(h3, gpre, pp, f"{t}_ple")
        saved.append((s_a, h1, s_m, s_b, h3, npl, gpre, pp))
        h = h4
    dh, g_final, loss_cols = _final_fwd_bwd(h, W["final_norm"], target, "final")
    G = {"final_norm": g_final}
    GB = {"g1": lax.empty((N_DEV, G1_ROWS, D_MODEL), F32), "g2": lax.empty((N_DEV, G2_ROWS, FF_BLK), F32)}
    per_layer = {n: [None] * DEPTH for n in ("ffa_norm", "mix_norm", "ffb_norm", "ple_norm", "ple_w_proj")}
    for i in reversed(range(DEPTH)):
        t = f"l{i}"
        s_a, h1, s_m, s_b, h3, npl, gpre, pp = saved[i]
        dgpre, dpp = _ple_bwd(dh, gpre, pp, f"{t}_ple_bwd")
        per_layer["ple_w_proj"][i] = _mm(p[i], dpp, ta=True, name=f"{t}_ple_proj_dw")
        GB["g1"] = _mm_w128_dw(npl, dgpre, G1_128[f"ple_w_gate{i}"], GB["g1"], f"{t}_ple_gate_dw")
        dnpl = _mm_w128(dgpre, W["G1"], G1_128[f"ple_w_gate{i}"], tb=True, name=f"{t}_ple_gate_dx")
        dh, per_layer["ple_norm"][i] = _rms_bwd(dnpl, h3, W["ple_norm"][i:i + 1], dh, f"{t}_ple_norm_bwd")
        dh, per_layer["ffb_norm"][i] = _ffn_bwd(dh, s_b, W["ffb_norm"][i:i + 1], W, GB, 1, i, f"{t}_ffb")
        dnm, g_mix = (_even_bwd if i % 2 == 0 else _odd_bwd)(dh, s_m, W, GB)
        G.update(g_mix)
        dh, per_layer["mix_norm"][i] = _rms_bwd(dnm, h1, W["mix_norm"][i:i + 1], dh, f"{t}_mix_norm_bwd")
        dh, per_layer["ffa_norm"][i] = _ffn_bwd(dh, s_a, W["ffa_norm"][i:i + 1], W, GB, 0, i, f"{t}_ffa")
    for n in ("ffa_norm", "mix_norm", "ffb_norm", "ple_norm"):
        G[n] = jnp.concatenate(per_layer[n], axis=0)
    G["ple_w_proj"] = per_layer["ple_w_proj"]
    return loss_cols, dh, GB, G


def kernel(x, p, ffa_norm, ffa_w_gate_up, ffa_w_down, mix_norm, ffb_norm, ffb_w_gate_up, ffb_w_down, ple_norm, ple_w_gate, ple_w_proj, ev_w_in, ev_sinks, ev_cq_norm, ev_w_uq, ev_ckv_norm, ev_w_ukv, ev_w_out, od_w_in, od_b_f, od_w_out, final_norm, loss_target, m_ffa_norm, m_ffa_w_gate_up, m_ffa_w_down, m_mix_norm, m_ffb_norm, m_ffb_w_gate_up, m_ffb_w_down, m_ple_norm, m_ple_w_gate, m_ple_w_proj, m_ev_w_in, m_ev_sinks, m_ev_cq_norm, m_ev_w_uq, m_ev_ckv_norm, m_ev_w_ukv, m_ev_w_out, m_od_w_in, m_od_b_f, m_od_w_out, m_final_norm, v_ffa_norm, v_ffa_w_gate_up, v_ffa_w_down, v_mix_norm, v_ffb_norm, v_ffb_w_gate_up, v_ffb_w_down, v_ple_norm, v_ple_w_gate, v_ple_w_proj, v_ev_w_in, v_ev_sinks, v_ev_cq_norm, v_ev_w_uq, v_ev_ckv_norm, v_ev_w_ukv, v_ev_w_out, v_od_w_in, v_od_b_f, v_od_w_out, v_final_norm):
    given = dict(locals())
    w_in = {n: given[n] for n in WEIGHTS}

    G1, G2, G3 = _all_gather([_in_slot(g) for g in _local_groups(w_in, BF16)])
    W = {n: w_in[n] for n in SMALL}
    W["final_norm"] = final_norm.reshape(1, -1)
    W.update(_misc_weights(G3))
    W.update(G1=G1, G2=G2, G2v=G2.reshape(2, 4, G2_ROWS, FF_BLK))

    loss_cols, dx, GB, G = _local_step(x[0], p[:, 0], loss_target[0], W)

    grads = _ungroup_local(*[r for r in _reduce_scatter([GB["g1"], GB["g2"], _misc_grads(G)])])
    layout = [(n, int(np.prod(w_in[n].shape))) for n in SMALL]
    vec = jnp.concatenate([G[n].astype(F32).reshape(-1) for n, _ in layout] + [jnp.sum(loss_cols).reshape(1)])
    vec = jnp.pad(vec, (0, N_DEV * SMALL_COLS - vec.shape[0])).reshape(N_DEV, SMALL_COLS)
    vec = _all_reduce_small(vec).reshape(-1)
    off = 0
    for n, size in layout:
        grads[n] = vec[off: off + size].reshape(w_in[n].shape)
        off += size
    loss = vec[off]

    delta, new_m, new_v = {}, {}, {}
    for n in WEIGHTS:
        shp = w_in[n].shape
        as2d = (lambda a: a.reshape(1, -1)) if len(shp) == 1 else (lambda a: a)
        d, nm, nv = _adamw(as2d(w_in[n]), as2d(grads[n]), as2d(given["m_" + n]), as2d(given["v_" + n]), f"adamw_{n}")
        delta[n], new_m[n], new_v[n] = d.reshape(shp), nm.reshape(shp), nv.reshape(shp)
    return (loss, dx[None], *[grads[n] for n in WEIGHTS], *[delta[n] for n in WEIGHTS],
            *[new_m[n] for n in WEIGHTS], *[new_v[n] for n in WEIGHTS])
```

```python
import functools

import numpy as np
import jax
import jax.numpy as jnp
from jax import lax
from jax.experimental import pallas as pl
from jax.experimental.pallas import tpu as pltpu

F32 = jnp.float32
BF16 = jnp.bfloat16
MESH = pl.DeviceIdType.MESH

D_MODEL = 1024
D_FF = 2816
RMS_EPS = 1e-6
PLE_DIM = 256
A_HEADS, A_KV_HEADS, A_HEAD_DIM, WINDOW = 8, 2, 64, 128
A_GROUP = A_HEADS // A_KV_HEADS
B_HEADS, B_Q_LORA, B_KV_LORA, B_NOPE, B_ROPE, B_V = 8, 256, 128, 64, 32, 64
ROPE_THETA = 10000.0
C_HEADS, C_HEAD_DIM = 16, 64
EVEN_IN = 1184
EVEN_IN_PAD = 1280
ODD_IN = 3088
ODD_IN_PAD = 3200
DEPTH = 2
ADAM_LR, ADAM_B1, ADAM_B2, ADAM_EPS, ADAM_WD, ADAM_STEP = 0.001, 0.9, 0.999, 1e-08, 0.01, 10

N_DEV = 8
LANES = 128
SUBLANES = 8
EW_TILE_BYTES = 3 << 20
MM_VMEM_BYTES = 26 << 20
NEG = -1e30
QK_PAD = 80

FF_BLK = D_FF // 4
DOWN_ROWS = D_FF // N_DEV
G1_ROWS, G2_ROWS, G3_ROWS, G3_COLS = 1920, 4096, 1024, 768
G1_128 = {"ple_w_gate0": 11, "ple_w_gate1": 12, "ev_w_out": 13, "od_w_out": 14}
OD_C, EV_C, STRIP_C = 386, 148, 128
STRIP0 = OD_C + EV_C

SMALL = ["ffa_norm", "mix_norm", "ffb_norm", "ple_norm", "ev_sinks", "ev_cq_norm", "ev_ckv_norm", "od_b_f", "final_norm"]
WEIGHTS = ["ffa_norm", "ffa_w_gate_up", "ffa_w_down", "mix_norm", "ffb_norm", "ffb_w_gate_up", "ffb_w_down", "ple_norm",
           "ple_w_gate", "ple_w_proj", "ev_w_in", "ev_sinks", "ev_cq_norm", "ev_w_uq", "ev_ckv_norm", "ev_w_ukv", "ev_w_out",
           "od_w_in", "od_b_f", "od_w_out", "final_norm"]
SMALL_COLS = 1280


def _divisor(n, cap, mult):
    if n <= cap:
        return n
    for t in range(cap - cap % mult, 0, -mult):
        if n % t == 0:
            return t
    raise ValueError(f"no tile for {n} under {cap} in steps of {mult}")


def _lanes(c):
    return -(-c // LANES) * LANES


def _ew(fn, rows, vecs, outs, reds=(), *, name):
    R = rows[0].shape[0]
    per_row = sum(_lanes(a.shape[1]) * a.dtype.itemsize for a in rows) + sum(_lanes(c) * jnp.dtype(d).itemsize for c, d in outs)
    tm = _divisor(R, max(16, EW_TILE_BYTES // per_row // 16 * 16), 16) if R % 16 == 0 else R
    n_r, n_v, n_o = len(rows), len(vecs), len(outs)

    def body(*refs):
        ins = [r[...] for r in refs[: n_r + n_v]]
        res = fn(*ins)
        if not isinstance(res, (tuple, list)):
            res = (res,)
        o_refs = refs[n_r + n_v: n_r + n_v + n_o]
        r_refs = refs[n_r + n_v + n_o:]
        for ref, val in zip(o_refs, res[:n_o]):
            ref[...] = val.astype(ref.dtype)
        if r_refs:
            @pl.when(pl.program_id(0) == 0)
            def _():
                for ref in r_refs:
                    ref[...] = jnp.zeros_like(ref)
            for ref, val in zip(r_refs, res[n_o:]):
                ref[...] += val

    in_specs = [pl.BlockSpec((tm, a.shape[1]), lambda i: (i, 0)) for a in rows]
    in_specs += [pl.BlockSpec((1, a.shape[1]), lambda i: (0, 0)) for a in vecs]
    out_specs = [pl.BlockSpec((tm, c), lambda i: (i, 0)) for c, _ in outs]
    out_specs += [pl.BlockSpec((1, c), lambda i: (0, 0)) for c in reds]
    out_shape = [jax.ShapeDtypeStruct((R, c), d) for c, d in outs] + [jax.ShapeDtypeStruct((1, c), F32) for c in reds]
    res = pl.pallas_call(body, name=name, grid=(R // tm,), in_specs=in_specs, out_specs=out_specs, out_shape=out_shape)(*rows, *vecs)
    return res[0] if len(res) == 1 else res


def _rms_fwd(x, w, name):
    def fn(x, w):
        y = x * lax.rsqrt(jnp.mean(x * x, axis=-1, keepdims=True) + RMS_EPS)
        return y * w
    return _ew(fn, [x], [w], [(x.shape[1], BF16)], name=name)


def _rms_bwd(dn, x, w, dres, name):
    def fn(dn, x, *rest):
        w = rest[-1]
        r = lax.rsqrt(jnp.mean(x * x, axis=-1, keepdims=True) + RMS_EPS)
        xh = x * r
        gw = dn * w
        dx = r * (gw - xh * jnp.mean(gw * xh, axis=-1, keepdims=True))
        if len(rest) == 2:
            dx = dx + rest[0]
        return dx, jnp.sum(dn * xh, axis=0, keepdims=True)
    rows = [dn, x] + ([dres] if dres is not None else [])
    return _ew(fn, rows, [w], [(x.shape[1], F32)], [x.shape[1]], name=name)


def _ple_fwd(h, gpre, pp, name):
    return _ew(lambda h, g, q: h + jax.nn.sigmoid(g) * q, [h, gpre, pp], [], [(h.shape[1], F32)], name=name)


def _ple_bwd(dh, gpre, pp, name):
    def fn(dh, g, q):
        sg = jax.nn.sigmoid(g)
        return dh * q * (sg * (1.0 - sg)), dh * sg
    return _ew(fn, [dh, gpre, pp], [], [(dh.shape[1], BF16), (dh.shape[1], BF16)], name=name)


def _rope(x1, x2, cos, sin, name):
    c = x1.shape[1]
    return _ew(lambda a, b, co, si: (a * co - b * si, a * si + b * co), [x1, x2, cos, sin], [], [(c, F32), (c, F32)], name=name)


def _logsig_fwd(f, b, name):
    def fn(f, b):
        z = f + b
        return jnp.minimum(z, 0.0) - jnp.log(1.0 + jnp.exp(-jnp.abs(z)))
    return _ew(fn, [f], [b], [(f.shape[1], F32)], name=name)


def _logsig_bwd(dlogf, f, b, name):
    def fn(d, f, b):
        df = d * jax.nn.sigmoid(-(f + b))
        return df, jnp.sum(df, axis=0, keepdims=True)
    return _ew(fn, [dlogf, f], [b], [(f.shape[1], F32)], [f.shape[1]], name=name)


def _final_fwd_bwd(h, w, target, name):
    d = h.shape[1]

    def fn(h, t, w):
        r = lax.rsqrt(jnp.mean(h * h, axis=-1, keepdims=True) + RMS_EPS)
        xh = h * r
        y = xh * w
        err = y - t
        dy = err * (1.0 / d)
        gw = dy * w
        dx = r * (gw - xh * jnp.mean(gw * xh, axis=-1, keepdims=True))
        return dx, jnp.sum(dy * xh, axis=0, keepdims=True), jnp.sum(err * err, axis=0, keepdims=True) * (0.5 / d)
    return _ew(fn, [h, target], [w], [(d, F32)], [d, d], name=name)


def _adamw(w, g, m, v, name):
    shape = w.shape
    c = shape[-1]
    w2, g2, m2, v2 = (a.reshape(-1, c) for a in (w, g, m, v))

    def fn(w, g, m, v):
        m = ADAM_B1 * m + (1.0 - ADAM_B1) * g
        v = ADAM_B2 * v + (1.0 - ADAM_B2) * jnp.square(g)
        m_hat = m / (1.0 - ADAM_B1 ** ADAM_STEP)
        v_hat = v / (1.0 - ADAM_B2 ** ADAM_STEP)
        delta = -ADAM_LR * (m_hat / (jnp.sqrt(v_hat) + ADAM_EPS) + ADAM_WD * w)
        return delta, m, v
    d, nm, nv = _ew(fn, [w2, g2, m2, v2], [], [(c, F32)] * 3, name=name)
    return d.reshape(shape), nm.reshape(shape), nv.reshape(shape)


def _split3(v):
    hi = v.astype(BF16)
    r1 = v - hi.astype(F32)
    mid = r1.astype(BF16)
    lo = (r1 - mid.astype(F32)).astype(BF16)
    return hi, mid, lo


def _cumsum(x, reverse, name):
    S, C = x.shape
    tm = _divisor(S, 512, 16)
    nt = S // tm

    def body(x_ref, o_ref, carry):
        @pl.when(pl.program_id(0) == 0)
        def _():
            carry[...] = jnp.zeros_like(carry)
        r = lax.broadcasted_iota(jnp.int32, (tm, tm), 0)
        c = lax.broadcasted_iota(jnp.int32, (tm, tm), 1)
        tri = jnp.where((c >= r) if reverse else (c <= r), 1.0, 0.0).astype(BF16)
        xv = x_ref[...]
        acc = jnp.zeros((tm, C), F32)
        for part in _split3(xv):
            acc = acc + jnp.dot(tri, part, preferred_element_type=F32)
        o_ref[...] = acc + carry[...]
        carry[...] += jnp.sum(xv, axis=0, keepdims=True)

    idx = (lambda i: (nt - 1 - i, 0)) if reverse else (lambda i: (i, 0))
    return pl.pallas_call(
        body, name=name, grid=(nt,), in_specs=[pl.BlockSpec((tm, C), idx)], out_specs=pl.BlockSpec((tm, C), idx),
        out_shape=jax.ShapeDtypeStruct((S, C), F32), scratch_shapes=[pltpu.VMEM((1, C), F32)],
    )(x)


NN = (((1,), (0,)), ((), ()))
NT = (((1,), (1,)), ((), ()))
TN = (((0,), (0,)), ((), ()))


def _mm_call(name, grid, k_axis, a, a_spec, a2d, b, b_spec, b2d, dims, out_sds, out_spec, o2d, *,
             alpha=1.0, res=None, res_spec=None, into=None):
    nk = grid[k_axis]

    def body(*refs):
        a_ref, b_ref = refs[0], refs[1]
        res_ref = refs[2] if res is not None else None
        o_ref, acc_ref = refs[-2], refs[-1]
        k = pl.program_id(k_axis)

        @pl.when(k == 0)
        def _():
            acc_ref[...] = jnp.zeros_like(acc_ref)

        av = a_ref[...].reshape(a2d).astype(BF16)
        bv = b_ref[...].reshape(b2d).astype(BF16)
        acc_ref[...] += lax.dot_general(av, bv, dims, preferred_element_type=F32)

        @pl.when(k == nk - 1)
        def _():
            r = acc_ref[...]
            if alpha != 1.0:
                r = r * alpha
            if res_ref is not None:
                r = res_ref[...].reshape(o2d) + r
            o_ref[...] = r.reshape(o_ref.shape).astype(o_ref.dtype)

    in_specs, args = [a_spec, b_spec], [a, b]
    if res is not None:
        in_specs.append(res_spec)
        args.append(res)
    aliases = {}
    if into is not None:
        aliases = {len(args): 0}
        in_specs.append(pl.BlockSpec(memory_space=pl.ANY))
        args.append(into)
        out_sds = jax.ShapeDtypeStruct(into.shape, into.dtype)
    sem = tuple("arbitrary" if d == k_axis else "parallel" for d in range(len(grid)))
    return pl.pallas_call(
        body, name=name, grid=grid, in_specs=in_specs, out_specs=out_spec, out_shape=out_sds,
        scratch_shapes=[pltpu.VMEM(o2d, F32)], input_output_aliases=aliases,
        compiler_params=pltpu.CompilerParams(dimension_semantics=sem),
    )(*args)


def _mm(a, b, *, ta=False, tb=False, out=F32, res=None, alpha=1.0, name):
    K, M = a.shape if ta else a.shape[::-1]
    N = b.shape[0] if tb else b.shape[1]
    assert (b.shape[1] if tb else b.shape[0]) == K, (a.shape, b.shape, ta, tb)
    tk = _divisor(K, 1024, LANES)
    tn = _divisor(N, 1408, LANES)
    for cap in (1024, 512, 256, 128):
        tm = _divisor(M, cap, LANES if ta else 16)
        est = 2 * (tm * tk * a.dtype.itemsize + tk * tn * b.dtype.itemsize + tm * tn * jnp.dtype(out).itemsize)
        est += tm * tn * 4 + (2 * tm * tn * 4 if res is not None else 0)
        if est <= MM_VMEM_BYTES:
            break
    a_spec = pl.BlockSpec((tk, tm), lambda i, j, k: (k, i)) if ta else pl.BlockSpec((tm, tk), lambda i, j, k: (i, k))
    b_spec = pl.BlockSpec((tn, tk), lambda i, j, k: (j, k)) if tb else pl.BlockSpec((tk, tn), lambda i, j, k: (k, j))
    o_spec = pl.BlockSpec((tm, tn), lambda i, j, k: (i, j))
    dims = (((0 if ta else 1,), (1 if tb else 0,)), ((), ()))
    return _mm_call(name, (M // tm, N // tn, K // tk), 2, a, a_spec, (tk, tm) if ta else (tm, tk), b, b_spec,
                    (tn, tk) if tb else (tk, tn), dims, jax.ShapeDtypeStruct((M, N), out), o_spec, (tm, tn),
                    alpha=alpha, res=res, res_spec=o_spec)


def _w128_spec(blk):
    return pl.BlockSpec((N_DEV, 128, D_MODEL), lambda *_: (0, blk, 0))


def _mm_w128(a, G1, blk, *, tb=False, res=None, out=F32, name):
    S = a.shape[0]
    tm = _divisor(S, 512, 16)
    row = pl.BlockSpec((tm, D_MODEL), lambda i, k: (i, 0))
    return _mm_call(name, (S // tm, 1), 1, a, row, (tm, D_MODEL), G1, _w128_spec(blk), (D_MODEL, D_MODEL), NT if tb else NN,
                    jax.ShapeDtypeStruct((S, D_MODEL), out), row, (tm, D_MODEL), res=res, res_spec=row)


def _mm_w128_dw(a, b, blk, into, name):
    S = a.shape[0]
    tk = _divisor(S, 1024, 16)
    row = pl.BlockSpec((tk, D_MODEL), lambda i, k: (k, 0))
    return _mm_call(name, (1, S // tk), 1, a, row, (tk, D_MODEL), b, row, (tk, D_MODEL), TN, None, _w128_spec(blk),
                    (D_MODEL, D_MODEL), into=into)


def _ffn_gate_up(n, G2v, rb, name):
    S = n.shape[0]
    tm = _divisor(S, 512, 16)

    def body(n_ref, w_ref, gu_ref, act_ref):
        nv = n_ref[...]
        g = jnp.dot(nv, w_ref[0, 0], preferred_element_type=F32)
        u = jnp.dot(nv, w_ref[1, 0], preferred_element_type=F32)
        gu_ref[0, 0] = g.astype(BF16)
        gu_ref[1, 0] = u.astype(BF16)
        act_ref[0] = (g * jax.nn.sigmoid(g) * u).astype(BF16)

    return pl.pallas_call(
        body, name=name, grid=(4, S // tm),
        in_specs=[pl.BlockSpec((tm, D_MODEL), lambda j, i: (i, 0)), pl.BlockSpec((2, 1, D_MODEL, FF_BLK), lambda j, i: (0, j, rb, 0))],
        out_specs=[pl.BlockSpec((2, 1, tm, FF_BLK), lambda j, i: (0, j, i, 0)), pl.BlockSpec((1, tm, FF_BLK), lambda j, i: (j, i, 0))],
        out_shape=[jax.ShapeDtypeStruct((2, 4, S, FF_BLK), BF16), jax.ShapeDtypeStruct((4, S, FF_BLK), BF16)],
    )(n, G2v)


def _ffn_down(act, G1, ob, h, name):
    S = h.shape[0]
    tm = _divisor(S, 512, 16)
    row = pl.BlockSpec((tm, D_MODEL), lambda i, k: (i, 0))
    return _mm_call(name, (S // tm, 4), 1, act, pl.BlockSpec((1, tm, FF_BLK), lambda i, k: (k, i, 0)), (tm, FF_BLK),
                    G1, pl.BlockSpec((2, DOWN_ROWS, D_MODEL), lambda i, k: (k, ob, 0)), (FF_BLK, D_MODEL), NN,
                    jax.ShapeDtypeStruct((S, D_MODEL), F32), row, (tm, D_MODEL), alpha=0.5, res=h, res_spec=row)


def _ffn_down_dx(dh, G1, ob, gu, name):
    S = dh.shape[0]
    tm = _divisor(S, 512, 16)

    def body(dh_ref, w_ref, gu_ref, o_ref):
        w = w_ref[...].reshape(FF_BLK, D_MODEL)
        dact = lax.dot_general(dh_ref[...].astype(BF16), w, NT, preferred_element_type=F32) * 0.5
        g = gu_ref[0, 0].astype(F32)
        u = gu_ref[1, 0].astype(F32)
        sg = jax.nn.sigmoid(g)
        o_ref[0, 0] = (dact * u * (sg * (1.0 + g * (1.0 - sg)))).astype(BF16)
        o_ref[1, 0] = (dact * (g * sg)).astype(BF16)

    blk = pl.BlockSpec((2, 1, tm, FF_BLK), lambda j, i: (0, j, i, 0))
    return pl.pallas_call(
        body, name=name, grid=(4, S // tm),
        in_specs=[pl.BlockSpec((tm, D_MODEL), lambda j, i: (i, 0)), pl.BlockSpec((2, DOWN_ROWS, D_MODEL), lambda j, i: (j, ob, 0)), blk],
        out_specs=blk, out_shape=jax.ShapeDtypeStruct((2, 4, S, FF_BLK), BF16),
    )(dh, G1, gu)


def _ffn_down_dw(act, dh, ob, into, name):
    S = dh.shape[0]
    tk = _divisor(S, 1024, 16)
    return _mm_call(name, (4, S // tk), 1, act, pl.BlockSpec((1, tk, FF_BLK), lambda j, k: (j, k, 0)), (tk, FF_BLK),
                    dh, pl.BlockSpec((tk, D_MODEL), lambda j, k: (k, 0)), (tk, D_MODEL), TN, None,
                    pl.BlockSpec((2, DOWN_ROWS, D_MODEL), lambda j, k: (j, ob, 0)), (FF_BLK, D_MODEL), alpha=0.5, into=into)


def _ffn_gate_up_dw(n, dgu8, rb, into, name):
    S = n.shape[0]
    tk = _divisor(S, 1024, 16)
    return _mm_call(name, (N_DEV, S // tk), 1, n, pl.BlockSpec((tk, D_MODEL), lambda b, k: (k, 0)), (tk, D_MODEL),
                    dgu8, pl.BlockSpec((1, tk, FF_BLK), lambda b, k: (b, k, 0)), (tk, FF_BLK), TN, None,
                    pl.BlockSpec((1, D_MODEL, FF_BLK), lambda b, k: (b, rb, 0)), (D_MODEL, FF_BLK), into=into)


def _ffn_gate_up_dx(dgu8, G2, rb, name):
    S = dgu8.shape[1]
    tm = _divisor(S, 512, 16)
    row = pl.BlockSpec((tm, D_MODEL), lambda i, k: (i, 0))
    return _mm_call(name, (S // tm, N_DEV), 1, dgu8, pl.BlockSpec((1, tm, FF_BLK), lambda i, k: (k, i, 0)), (tm, FF_BLK),
                    G2, pl.BlockSpec((1, D_MODEL, FF_BLK), lambda i, k: (k, rb, 0)), (D_MODEL, FF_BLK), NT,
                    jax.ShapeDtypeStruct((S, D_MODEL), F32), row, (tm, D_MODEL))


def _flash_fwd(q, k, v, *, tile, window=None, sink=None, name):
    H, S, dqk = q.shape
    G = H // k.shape[0]
    dv = v.shape[2]
    tq = tk = tile

    def body(*refs):
        q_ref, k_ref, v_ref = refs[:3]
        o_ref, lse_ref = refs[-2], refs[-1]
        i = pl.program_id(1)
        qv = q_ref[0]
        if sink is not None:
            m0 = jnp.zeros((tq, 1), F32) + refs[3][0, :, 0:1]
            l0 = jnp.ones((tq, 1), F32)
        else:
            m0 = jnp.full((tq, 1), NEG, F32)
            l0 = jnp.zeros((tq, 1), F32)

        def step(j, carry, masked):
            m, l, acc = carry
            off = pl.multiple_of(j * tk, tk)
            kj = k_ref[0, pl.ds(off, tk), :]
            vj = v_ref[0, pl.ds(off, tk), :]
            s = lax.dot_general(qv, kj, NT, preferred_element_type=F32)
            if masked:
                dist = (i * tq + lax.broadcasted_iota(jnp.int32, (tq, tk), 0)) - (j * tk + lax.broadcasted_iota(jnp.int32, (tq, tk), 1))
                mask = dist >= 0
                if window is not None:
                    mask = mask & (dist < window)
                s = jnp.where(mask, s, NEG)
            m_new = jnp.maximum(m, jnp.max(s, axis=-1, keepdims=True))
            a = jnp.exp(m - m_new)
            pr = jnp.exp(s - m_new)
            l = a * l + jnp.sum(pr, axis=-1, keepdims=True)
            acc = a * acc + jnp.dot(pr.astype(BF16), vj, preferred_element_type=F32)
            return m_new, l, acc

        carry = (m0, l0, jnp.zeros((tq, dv), F32))
        if window is None:
            carry = lax.fori_loop(0, i, functools.partial(step, masked=False), carry)
            carry = step(i, carry, True)
        else:
            lo = jnp.maximum((i * tq - (window - 1)) // tk, 0)
            carry = lax.fori_loop(lo, i + 1, functools.partial(step, masked=True), carry)
        m, l, acc = carry
        o_ref[0] = acc / l
        lse_ref[0] = m + jnp.log(l)

    in_specs = [
        pl.BlockSpec((1, tq, dqk), lambda h, i: (h, i, 0)),
        pl.BlockSpec((1, S, dqk), lambda h, i: (h // G, 0, 0)),
        pl.BlockSpec((1, S, dv), lambda h, i: (h // G, 0, 0)),
    ]
    args = [q, k, v]
    if sink is not None:
        in_specs += [pl.BlockSpec((1, 1, LANES), lambda h, i: (h, 0, 0))]
        args += [sink]
    return pl.pallas_call(
        body, name=name, grid=(H, S // tq), in_specs=in_specs,
        out_specs=[pl.BlockSpec((1, tq, dv), lambda h, i: (h, i, 0)), pl.BlockSpec((1, tq, 1), lambda h, i: (h, i, 0))],
        out_shape=[jax.ShapeDtypeStruct((H, S, dv), F32), jax.ShapeDtypeStruct((H, S, 1), F32)],
    )(*args)


def _flash_bwd(q, k, v, o, do, lse_row, *, tile, window=None, sink=None, name):
    H, S, dqk = q.shape
    G = H // k.shape[0]
    dv = v.shape[2]
    tq = tk = tile
    nq = S // tq
    has_p = sink is not None

    def body(*refs):
        q_ref, k_ref, v_ref, o_ref, do_ref, lse_ref = refs[:6]
        p_ref = refs[6] if has_p else None
        pos = 7 if has_p else 6
        dq_ref, dk_ref, dv_ref = refs[pos: pos + 3]
        ds_ref = refs[pos + 3] if has_p else None
        delta = refs[-1]
        j = pl.program_id(1)

        @pl.when(j == 0)
        def _():
            dq_ref[...] = jnp.zeros_like(dq_ref)
            ones = jnp.ones((SUBLANES, dv), BF16)

            def dstep(i, tot):
                off = pl.multiple_of(i * tq, tq)
                x = do_ref[0, pl.ds(off, tq), :].astype(F32) * o_ref[0, pl.ds(off, tq), :]
                d = jnp.zeros((SUBLANES, tq), F32)
                for part in _split3(x):
                    d = d + lax.dot_general(ones, part, NT, preferred_element_type=F32)
                drow = d[0:1, :]
                delta[:, pl.ds(off, tq)] = drow
                if has_p:
                    w = jnp.exp(p_ref[0, :, 0:1] - lse_ref[0, :, pl.ds(off, tq)])
                    tot = tot - jnp.sum(w * drow, axis=1, keepdims=True)
                return tot

            tot = lax.fori_loop(0, nq, dstep, jnp.zeros((1, 1), F32))
            if has_p:
                ds_ref[0] = jnp.zeros((1, LANES), F32) + tot

        kj = k_ref[0]
        vj = v_ref[0]

        def step(i, carry, masked):
            dk, dvv = carry
            off = pl.multiple_of(i * tq, tq)
            qi = q_ref[0, pl.ds(off, tq), :]
            doi = do_ref[0, pl.ds(off, tq), :]
            st = lax.dot_general(kj, qi, NT, preferred_element_type=F32)
            if masked:
                dist = (i * tq + lax.broadcasted_iota(jnp.int32, (tk, tq), 1)) - (j * tk + lax.broadcasted_iota(jnp.int32, (tk, tq), 0))
                mask = dist >= 0
                if window is not None:
                    mask = mask & (dist < window)
                st = jnp.where(mask, st, NEG)
            pt = jnp.exp(st - lse_ref[0, :, pl.ds(off, tq)])
            dvv = dvv + jnp.dot(pt.astype(BF16), doi, preferred_element_type=F32)
            dpt = lax.dot_general(vj, doi, NT, preferred_element_type=F32)
            dsb = (pt * (dpt - delta[:, pl.ds(off, tq)])).astype(BF16)
            dk = dk + jnp.dot(dsb, qi, preferred_element_type=F32)
            dq_ref[0, pl.ds(off, tq), :] += lax.dot_general(dsb, kj, TN, preferred_element_type=F32)
            return dk, dvv

        carry = (jnp.zeros((tk, dqk), F32), jnp.zeros((tk, dv), F32))
        if window is None:
            carry = step(j, carry, True)
            carry = lax.fori_loop(j + 1, nq, functools.partial(step, masked=False), carry)
        else:
            hi = jnp.minimum(nq - 1, ((j + 1) * tk + window - 2) // tq)
            carry = lax.fori_loop(j, hi + 1, functools.partial(step, masked=True), carry)
        dk_ref[0] = carry[0]
        dv_ref[0] = carry[1]

    whole = lambda d: pl.BlockSpec((1, S, d), lambda h, j: (h, 0, 0))
    in_specs = [
        whole(dqk),
        pl.BlockSpec((1, tk, dqk), lambda h, j: (h // G, j, 0)),
        pl.BlockSpec((1, tk, dv), lambda h, j: (h // G, j, 0)),
        whole(dv), whole(dv),
        pl.BlockSpec((1, 1, S), lambda h, j: (h, 0, 0)),
    ]
    args = [q, k, v, o, do, lse_row]
    if has_p:
        in_specs += [pl.BlockSpec((1, 1, LANES), lambda h, j: (h, 0, 0))]
        args += [sink]
    out_specs = [whole(dqk), pl.BlockSpec((1, tk, dqk), lambda h, j: (h, j, 0)), pl.BlockSpec((1, tk, dv), lambda h, j: (h, j, 0))]
    out_shape = [jax.ShapeDtypeStruct((H, S, dqk), F32), jax.ShapeDtypeStruct((H, S, dqk), F32), jax.ShapeDtypeStruct((H, S, dv), F32)]
    if has_p:
        out_specs += [pl.BlockSpec((1, 1, LANES), lambda h, j: (h, 0, 0))]
        out_shape += [jax.ShapeDtypeStruct((H, 1, LANES), F32)]
    return pl.pallas_call(
        body, name=name, grid=(H, S // tk), in_specs=in_specs, out_specs=out_specs, out_shape=out_shape,
        scratch_shapes=[pltpu.VMEM((1, S), F32)],
        compiler_params=pltpu.CompilerParams(dimension_semantics=("parallel", "arbitrary")),
    )(*args)


def _heads(x, h):
    S = x.shape[0]
    return jnp.transpose(x.reshape(S, h, -1), (1, 0, 2))


def _unheads(x):
    h, S, d = x.shape
    return jnp.transpose(x, (1, 0, 2)).reshape(S, h * d)


def _exact3(v):
    rnd = lambda a: lax.reduce_precision(a, exponent_bits=8, mantissa_bits=7)
    hi = rnd(v)
    mid = rnd(v - hi)
    return hi, mid, rnd(v - hi - mid)


def _causal_mask(st, i, j, tq, tk, window):
    dist = (i * tq + lax.broadcasted_iota(jnp.int32, (tk, tq), 1)) - (j * tk + lax.broadcasted_iota(jnp.int32, (tk, tq), 0))
    mask = dist >= 0
    if window is not None:
        mask = mask & (dist < window)
    return jnp.where(mask, st, NEG)


def _attn_fwd(qT, k, vT1, *, tile, window=None, sink=None, name):
    H, dqk, S = qT.shape
    G = H // k.shape[0]
    dvp = vT1.shape[1]
    dv = dvp - 16
    tq = tk = tile

    def body(*refs):
        q_ref, k_ref, v_ref = refs[:3]
        o_ref, lse_ref = refs[-2], refs[-1]
        i = pl.program_id(1)
        qv = q_ref[0]
        acc0 = jnp.zeros((dvp, tq), F32)
        if sink is not None:
            m0 = jnp.zeros((1, tq), F32) + refs[3][0, :, 0:1]
            acc0 = jnp.where(lax.broadcasted_iota(jnp.int32, (dvp, tq), 0) == dv, 1.0, 0.0)
        else:
            m0 = jnp.full((1, tq), NEG, F32)

        def step(j, carry, masked):
            m, acc = carry
            off = pl.multiple_of(j * tk, tk)
            st = jnp.dot(k_ref[0, pl.ds(off, tk), :], qv, preferred_element_type=F32)
            if masked:
                st = _causal_mask(st, i, j, tq, tk, window)
            m_new = jnp.maximum(m, jnp.max(st, axis=0, keepdims=True))
            pt = jnp.exp(st - m_new).astype(BF16)
            acc = jnp.exp(m - m_new) * acc + jnp.dot(v_ref[0, :, pl.ds(off, tk)], pt, preferred_element_type=F32)
            return m_new, acc

        carry = (m0, acc0)
        if window is None:
            carry = lax.fori_loop(0, i, functools.partial(step, masked=False), carry)
            carry = step(i, carry, True)
        else:
            lo = jnp.maximum((i * tq - (window - 1)) // tk, 0)
            carry = lax.fori_loop(lo, i + 1, functools.partial(step, masked=True), carry)
        m, acc = carry
        l = acc[dv:dv + 1, :]
        o_ref[0] = acc[:dv, :] / l
        lse_ref[0] = m + jnp.log(l)

    in_specs = [
        pl.BlockSpec((1, dqk, tq), lambda h, i: (h, 0, i)),
        pl.BlockSpec((1, S, dqk), lambda h, i: (h // G, 0, 0)),
        pl.BlockSpec((1, dvp, S), lambda h, i: (h // G, 0, 0)),
    ]
    args = [qT, k, vT1]
    if sink is not None:
        in_specs += [pl.BlockSpec((1, 1, LANES), lambda h, i: (h, 0, 0))]
        args += [sink]
    return pl.pallas_call(
        body, name=name, grid=(H, S // tq), in_specs=in_specs,
        out_specs=[pl.BlockSpec((1, dv, tq), lambda h, i: (h, 0, i)), pl.BlockSpec((1, 1, tq), lambda h, i: (h, 0, i))],
        out_shape=[jax.ShapeDtypeStruct((H, dv, S), F32), jax.ShapeDtypeStruct((H, 1, S), F32)],
    )(*args)


def _attn_bwd(q, qT, k, kT, v, oT, do, doT, lse, *, tile, window=None, sink=None, name):
    H, S, dqk = q.shape
    G = H // k.shape[0]
    dv = v.shape[2]
    tq = tk = tile
    nq = S // tq
    has_p = sink is not None

    def body(*refs):
        q_ref, qT_ref, k_ref, kT_ref, v_ref, oT_ref, do_ref, doT_ref, lse_ref = refs[:9]
        p_ref = refs[9] if has_p else None
        pos = 10 if has_p else 9
        dq_ref, dk_ref, dv_ref = refs[pos: pos + 3]
        ds_ref = refs[pos + 3] if has_p else None
        delta = refs[-1]
        j = pl.program_id(1)

        @pl.when(j == 0)
        def _():
            dq_ref[...] = jnp.zeros_like(dq_ref)
            drow = jnp.sum(doT_ref[0].astype(F32) * oT_ref[0], axis=0, keepdims=True)
            delta[...] = drow
            if has_p:
                w = jnp.exp(p_ref[0, :, 0:1] - lse_ref[0])
                ds_ref[0] = jnp.zeros((1, LANES), F32) - jnp.sum(w * drow, axis=1, keepdims=True)

        kj = k_ref[0]
        kTj = kT_ref[0]
        vj = v_ref[0]

        def step(i, carry, masked):
            dk, dvv = carry
            off = pl.multiple_of(i * tq, tq)
            st = jnp.dot(kj, qT_ref[0, :, pl.ds(off, tq)], preferred_element_type=F32)
            if masked:
                st = _causal_mask(st, i, j, tq, tk, window)
            pt = jnp.exp(st - lse_ref[0, :, pl.ds(off, tq)])
            dvv = dvv + jnp.dot(pt.astype(BF16), do_ref[0, pl.ds(off, tq), :], preferred_element_type=F32)
            dpt = jnp.dot(vj, doT_ref[0, :, pl.ds(off, tq)], preferred_element_type=F32)
            dsb = (pt * (dpt - delta[:, pl.ds(off, tq)])).astype(BF16)
            dk = dk + jnp.dot(dsb, q_ref[0, pl.ds(off, tq), :], preferred_element_type=F32)
            dq_ref[0, :, pl.ds(off, tq)] += jnp.dot(kTj, dsb, preferred_element_type=F32)
            return dk, dvv

        carry = (jnp.zeros((tk, dqk), F32), jnp.zeros((tk, dv), F32))
        if window is None:
            carry = step(j, carry, True)
            carry = lax.fori_loop(j + 1, nq, functools.partial(step, masked=False), carry)
        else:
            hi = jnp.minimum(nq - 1, ((j + 1) * tk + window - 2) // tq)
            carry = lax.fori_loop(j, hi + 1, functools.partial(step, masked=True), carry)
        dk_ref[0] = carry[0]
        dv_ref[0] = carry[1]

    rows = lambda d: pl.BlockSpec((1, S, d), lambda h, j: (h, 0, 0))
    colsT = lambda d: pl.BlockSpec((1, d, S), lambda h, j: (h, 0, 0))
    in_specs = [
        rows(dqk), colsT(dqk),
        pl.BlockSpec((1, tk, dqk), lambda h, j: (h // G, j, 0)),
        pl.BlockSpec((1, dqk, tk), lambda h, j: (h // G, 0, j)),
        pl.BlockSpec((1, tk, dv), lambda h, j: (h // G, j, 0)),
        colsT(dv), rows(dv), colsT(dv),
        pl.BlockSpec((1, 1, S), lambda h, j: (h, 0, 0)),
    ]
    args = [q, qT, k, kT, v, oT, do, doT, lse]
    if has_p:
        in_specs += [pl.BlockSpec((1, 1, LANES), lambda h, j: (h, 0, 0))]
        args += [sink]
    out_specs = [colsT(dqk), pl.BlockSpec((1, tk, dqk), lambda h, j: (h, j, 0)), pl.BlockSpec((1, tk, dv), lambda h, j: (h, j, 0))]
    out_shape = [jax.ShapeDtypeStruct((H, dqk, S), F32), jax.ShapeDtypeStruct((H, S, dqk), F32), jax.ShapeDtypeStruct((H, S, dv), F32)]
    if has_p:
        out_specs += [pl.BlockSpec((1, 1, LANES), lambda h, j: (h, 0, 0))]
        out_shape += [jax.ShapeDtypeStruct((H, 1, LANES), F32)]
    return pl.pallas_call(
        body, name=name, grid=(H, S // tk), in_specs=in_specs, out_specs=out_specs, out_shape=out_shape,
        scratch_shapes=[pltpu.VMEM((1, S), F32)],
        compiler_params=pltpu.CompilerParams(dimension_semantics=("parallel", "arbitrary")),
    )(*args)


def _rows_and_cols(x3):
    xb = x3.astype(BF16)
    return jnp.transpose(xb, (1, 0, 2)), jnp.transpose(xb, (1, 2, 0))


def _v_with_ones(v3):
    S, h, _ = v3.shape
    vT = jnp.transpose(v3.astype(BF16), (1, 2, 0))
    return jnp.concatenate([vT, jnp.ones((h, 1, S), BF16), jnp.zeros((h, 15, S), BF16)], axis=1)


def _from_T(oT):
    h, d, S = oT.shape
    return jnp.transpose(oT, (2, 0, 1)).reshape(S, h * d)


def _coords():
    return lax.axis_index("x"), lax.axis_index("y"), lax.axis_index("c")


def _peer(axis):
    x, y, c = _coords()
    return {"x": (1 - x, y, c), "y": (x, 1 - y, c), "c": (x, y, 1 - c)}[axis]


HBM_SPEC = pl.BlockSpec(memory_space=pl.ANY)


def _all_gather(bufs):
    n = len(bufs)

    def body(*refs):
        outs = refs[n: 2 * n]
        send_sems, recv_sems = refs[2 * n], refs[2 * n + 1]
        x, y, c = _coords()
        me, sibling = (x, y, c), (x, y, 1 - c)
        chips = [(1 - x, y), (x, 1 - y), (1 - x, 1 - y)]

        def copy(t, k, block, to):
            px, py, pc = block
            ref = outs[t].at[4 * px + 2 * py + pc]
            return pltpu.make_async_remote_copy(ref, ref, send_sems.at[7 * t + k], recv_sems.at[7 * t + k], device_id=to, device_id_type=MESH)

        first = []
        for t in range(n):
            first.append(copy(t, 0, me, sibling))
            first += [copy(t, 1 + j, me, (*chip, c)) for j, chip in enumerate(chips)]
        for cp in first:
            cp.start()
        passed = []
        for j, chip in enumerate(chips):
            for t in range(n):
                copy(t, 1 + j, (*chip, c), me).wait_recv()
                cp = copy(t, 4 + j, (*chip, c), sibling)
                cp.start()
                passed.append(cp)
        for t in range(n):
            copy(t, 0, sibling, me).wait_recv()
            for j, chip in enumerate(chips):
                copy(t, 4 + j, (*chip, 1 - c), me).wait_recv()
        for cp in first + passed:
            cp.wait_send()

    return pl.pallas_call(
        body, name="all_gather", in_specs=[HBM_SPEC] * n, out_specs=[HBM_SPEC] * n,
        out_shape=[jax.ShapeDtypeStruct(b.shape, b.dtype) for b in bufs], input_output_aliases={t: t for t in range(n)},
        scratch_shapes=[pltpu.SemaphoreType.DMA((7 * n,)), pltpu.SemaphoreType.DMA((7 * n,))],
    )(*bufs)


def _in_slot(local):
    x, y, c = _coords()
    buf = jnp.zeros((N_DEV,) + local.shape, local.dtype)
    return lax.dynamic_update_slice(buf, local[None], (4 * x + 2 * y + c, 0, 0))


def _scatter_pair(vs, axis, name):
    n = len(vs)

    def body(*refs):
        send_sems, recv_sems = refs[2 * n], refs[2 * n + 1]
        me = lax.axis_index(axis)
        copies = []
        for t in range(n):
            v_ref, o_ref = refs[t], refs[n + t]
            src = v_ref.at[1 - me] if len(v_ref.shape) == 3 else v_ref.at[:, 1 - me]
            cp = pltpu.make_async_remote_copy(src, o_ref, send_sems.at[t], recv_sems.at[t], device_id=_peer(axis), device_id_type=MESH)
            cp.start()
            copies.append(cp)
        for cp in copies:
            cp.wait()

    out_shape = [jax.ShapeDtypeStruct(v.shape[:-3] + v.shape[-2:], v.dtype) for v in vs]
    return pl.pallas_call(
        body, name=name, in_specs=[HBM_SPEC] * n, out_specs=[HBM_SPEC] * n, out_shape=out_shape,
        scratch_shapes=[pltpu.SemaphoreType.DMA((n,)), pltpu.SemaphoreType.DMA((n,))],
    )(*vs)


def _add_kept(v, got, axis, out, name):
    R, C = v.shape[-2:]
    lead = v.shape[0] if v.ndim == 4 else 1
    tm = _divisor(R, max(16, EW_TILE_BYTES // (_lanes(C) * (v.dtype.itemsize + got.dtype.itemsize + jnp.dtype(out).itemsize)) // 16 * 16), 16)
    me = lax.axis_index(axis).astype(jnp.int32).reshape(1)
    v4 = v.reshape(lead, 2, R, C)
    g3 = got.reshape(lead, R, C)

    def body(me_ref, v_ref, g_ref, o_ref):
        o_ref[...] = (v_ref[0].astype(F32) + g_ref[...].astype(F32)).astype(o_ref.dtype)

    res = pl.pallas_call(
        body, name=name, out_shape=jax.ShapeDtypeStruct((lead, R, C), out),
        grid_spec=pltpu.PrefetchScalarGridSpec(
            num_scalar_prefetch=1, grid=(lead, R // tm),
            in_specs=[pl.BlockSpec((1, 1, tm, C), lambda b, i, me: (b, me[0], i, 0)), pl.BlockSpec((1, tm, C), lambda b, i, me: (b, i, 0))],
            out_specs=pl.BlockSpec((1, tm, C), lambda b, i, me: (b, i, 0))),
    )(me, v4, g3)
    return res


def _reduce_scatter(gs):
    vs = [g.reshape(4, 2, *g.shape[1:]) for g in gs]
    got = _scatter_pair(vs, "c", "reduce_scatter_c")
    vs = [_add_kept(v, r, "c", BF16, f"reduce_scatter_add_c{t}") for t, (v, r) in enumerate(zip(vs, got))]
    vs = [v.reshape(2, 2 * v.shape[1], v.shape[2]) for v in vs]
    got = _scatter_pair(vs, "x", "reduce_scatter_x")
    vs = [_add_kept(v, r, "x", BF16, f"reduce_scatter_add_x{t}")[0] for t, (v, r) in enumerate(zip(vs, got))]
    vs = [v.reshape(2, v.shape[0] // 2, v.shape[1]) for v in vs]
    got = _scatter_pair(vs, "y", "reduce_scatter_y")
    return [_add_kept(v, r, "y", F32, f"reduce_scatter_add_y{t}")[0] for t, (v, r) in enumerate(zip(vs, got))]


def _all_reduce_small(v):
    def body(v_ref, o_ref, buf, send_sems, recv_sems):
        x, y, c = _coords()
        me = 4 * x + 2 * y + c
        buf[me] = v_ref[...]
        copies = []
        for k in range(1, N_DEV):
            peer = tuple((1 - a) if (k >> s) & 1 else a for a, s in ((x, 2), (y, 1), (c, 0)))
            cp = pltpu.make_async_remote_copy(v_ref, buf.at[me], send_sems.at[k - 1], recv_sems.at[k - 1], device_id=peer, device_id_type=MESH)
            cp.start()
            copies.append(cp)
        for cp in copies:
            cp.wait()
        acc = buf[0]
        for d in range(1, N_DEV):
            acc = acc + buf[d]
        o_ref[...] = acc

    vm = pl.BlockSpec(memory_space=pltpu.VMEM)
    return pl.pallas_call(
        body, name="all_reduce_small", in_specs=[vm], out_specs=vm, out_shape=jax.ShapeDtypeStruct(v.shape, F32),
        scratch_shapes=[pltpu.VMEM((N_DEV,) + v.shape, F32), pltpu.SemaphoreType.DMA((N_DEV - 1,)), pltpu.SemaphoreType.DMA((N_DEV - 1,))],
    )(v)


def _local_groups(w, dtype):
    g1 = jnp.concatenate([w["ffa_w_down"].reshape(-1, D_MODEL), w["ffb_w_down"].reshape(-1, D_MODEL),
                          w["ple_w_gate"].reshape(-1, D_MODEL), w["ev_w_out"][0], w["od_w_out"][0]], axis=0).astype(dtype)
    g2 = jnp.concatenate([w["ffa_w_gate_up"].reshape(-1, FF_BLK), w["ffb_w_gate_up"].reshape(-1, FF_BLK)], axis=0).astype(dtype)
    strip = jnp.concatenate([w["ple_w_proj"].reshape(-1, STRIP_C), w["ev_w_ukv"][0], jnp.pad(w["ev_w_uq"][0], ((0, 0), (0, STRIP_C - 96))),
                             jnp.zeros((G3_ROWS - 896, STRIP_C), F32)], axis=0)
    g3 = jnp.concatenate([w["od_w_in"][0], w["ev_w_in"][0], strip, jnp.zeros((G3_ROWS, G3_COLS - STRIP0 - STRIP_C), F32)], axis=1).astype(dtype)
    return g1, g2, g3


def _ungroup_local(r1, r2, r3):
    out = {
        "ffa_w_down": r1[:704].reshape(2, DOWN_ROWS, D_MODEL), "ffb_w_down": r1[704:1408].reshape(2, DOWN_ROWS, D_MODEL),
        "ple_w_gate": r1[1408:1664].reshape(2, 128, D_MODEL), "ev_w_out": r1[1664:1792][None], "od_w_out": r1[1792:1920][None],
        "ffa_w_gate_up": r2[:2048].reshape(2, D_MODEL, FF_BLK), "ffb_w_gate_up": r2[2048:].reshape(2, D_MODEL, FF_BLK),
        "od_w_in": r3[:, :OD_C][None], "ev_w_in": r3[:, OD_C:STRIP0][None],
    }
    strip = r3[:, STRIP0:STRIP0 + STRIP_C]
    out["ple_w_proj"] = strip[:512].reshape(2, PLE_DIM, STRIP_C)
    out["ev_w_ukv"] = strip[512:640][None]
    out["ev_w_uq"] = strip[640:896, :96][None]
    return out


def _cols(a):
    return jnp.transpose(a, (1, 0, 2)).reshape(a.shape[1], -1)


def _blocks(g, c):
    return jnp.transpose(g.reshape(g.shape[0], N_DEV, c), (1, 0, 2))


def _uq_permute(w):
    r = w.shape[0]
    w3 = w.reshape(r, B_HEADS, B_NOPE + B_ROPE)
    half = B_ROPE // 2
    return jnp.concatenate([w3[:, :, :B_NOPE].reshape(r, -1), w3[:, :, B_NOPE:B_NOPE + half].reshape(r, -1), w3[:, :, B_NOPE + half:].reshape(r, -1)], axis=1)


def _uq_unpermute(g):
    r = g.shape[0]
    half = B_ROPE // 2
    n = B_HEADS * B_NOPE
    parts = [g[:, :n].reshape(r, B_HEADS, B_NOPE), g[:, n:n + B_HEADS * half].reshape(r, B_HEADS, half), g[:, n + B_HEADS * half:].reshape(r, B_HEADS, half)]
    return jnp.concatenate(parts, axis=2).reshape(r, -1)


def _ukv_permute(w):
    r = w.shape[0]
    return jnp.transpose(w.reshape(r, B_HEADS, 2, B_NOPE), (0, 2, 1, 3)).reshape(r, -1)


def _ukv_unpermute(g):
    r = g.shape[0]
    return jnp.transpose(g.reshape(r, 2, B_HEADS, B_NOPE), (0, 2, 1, 3)).reshape(r, -1)


def _misc_weights(G3):
    strip = G3[:, :, STRIP0:STRIP0 + STRIP_C]
    return {
        "od_w_in": jnp.pad(_cols(G3[:, :, :OD_C]), ((0, 0), (0, ODD_IN_PAD - ODD_IN))),
        "ev_w_in": jnp.pad(_cols(G3[:, :, OD_C:STRIP0]), ((0, 0), (0, EVEN_IN_PAD - EVEN_IN))),
        "ple_w_proj": [_cols(strip[:, i * PLE_DIM:(i + 1) * PLE_DIM]) for i in range(DEPTH)],
        "ev_w_ukv": _ukv_permute(_cols(strip[:, 512:640])),
        "ev_w_uq": _uq_permute(_cols(strip[:, 640:896, :96])),
    }


def _misc_grads(G):
    strip = jnp.concatenate([
        _blocks(G["ple_w_proj"][0], STRIP_C), _blocks(G["ple_w_proj"][1], STRIP_C), _blocks(_ukv_unpermute(G["ev_w_ukv"]), STRIP_C),
        jnp.pad(_blocks(_uq_unpermute(G["ev_w_uq"]), 96), ((0, 0), (0, 0), (0, STRIP_C - 96))),
        jnp.zeros((N_DEV, G3_ROWS - 896, STRIP_C), F32)], axis=1)
    return jnp.concatenate([_blocks(G["od_w_in"][:, :ODD_IN], OD_C), _blocks(G["ev_w_in"][:, :EVEN_IN], EV_C), strip,
                            jnp.zeros((N_DEV, G3_ROWS, G3_COLS - STRIP0 - STRIP_C), F32)], axis=2)


def _ffn_fwd(h, norm_w, W, f, i, tag):
    n = _rms_fwd(h, norm_w, f"{tag}_norm")
    gu, act = _ffn_gate_up(n, W["G2v"], 2 * f + i, f"{tag}_gate_up")
    out = _ffn_down(act, W["G1"], 2 * f + i, h, f"{tag}_down")
    return out, (h, n, gu, act)


def _ffn_bwd(dout, saved, norm_w, W, GB, f, i, tag):
    h, n, gu, act = saved
    S = h.shape[0]
    blk = 2 * f + i
    GB["g1"] = _ffn_down_dw(act, dout, blk, GB["g1"], f"{tag}_down_dw")
    dgu = _ffn_down_dx(dout, W["G1"], blk, gu, f"{tag}_down_dx").reshape(N_DEV, S, FF_BLK)
    GB["g2"] = _ffn_gate_up_dw(n, dgu, blk, GB["g2"], f"{tag}_gate_up_dw")
    dn = _ffn_gate_up_dx(dgu, W["G2"], blk, f"{tag}_gate_up_dx")
    return _rms_bwd(dn, h, norm_w, dout, f"{tag}_norm_bwd")


def _rope_tables(S):
    inv = ROPE_THETA ** (-jnp.arange(0, B_ROPE, 2, dtype=F32) / B_ROPE)
    ang = jnp.arange(S, dtype=F32)[:, None] * inv[None, :]
    return jnp.cos(ang), jnp.sin(ang)


def _alibi_columns(S):
    t = jnp.arange(S, dtype=jnp.int32)
    hi = ((t // 16) * 16).astype(F32)
    lo = (t % 16).astype(F32)
    slopes = 2.0 ** (-8.0 * jnp.arange(1, A_HEADS + 1, dtype=F32) / A_HEADS)
    zq = jnp.zeros((S, A_HEADS), F32)
    rest = QK_PAD - A_HEAD_DIM - 4
    qc = jnp.stack([-slopes[None, :] * hi[:, None], -slopes[None, :] * lo[:, None], zq + slopes[None, :], zq + slopes[None, :]] + [zq] * rest, axis=-1)
    one = jnp.ones((S, A_KV_HEADS), F32)
    zk = jnp.zeros((S, A_KV_HEADS), F32)
    kc = jnp.stack([one, one, zk + hi[:, None], zk + lo[:, None]] + [zk] * rest, axis=-1)
    return qc, kc


def _sink_prm(sinks):
    return jnp.zeros((A_HEADS, 1, LANES), F32).at[:, 0, 0].set(sinks.astype(F32))


def _even_fwd(hn, h, W):
    S = hn.shape[0]
    proj = _mm(hn, W["ev_w_in"], name="ev_in")
    a_q, a_k, a_v = proj[:, :512], proj[:, 512:640], proj[:, 640:768]
    c_q, c_kv = proj[:, 768:1024], proj[:, 1024:1152]
    kr1, kr2 = proj[:, 1152:1168], proj[:, 1168:1184]
    qc, kc = _alibi_columns(S)
    qa, qaT = _rows_and_cols(jnp.concatenate([(a_q * A_HEAD_DIM ** -0.5).reshape(S, A_HEADS, A_HEAD_DIM), qc], axis=-1))
    ka, kaT = _rows_and_cols(jnp.concatenate([a_k.reshape(S, A_KV_HEADS, A_HEAD_DIM), kc], axis=-1))
    va3 = a_v.reshape(S, A_KV_HEADS, A_HEAD_DIM)
    va = jnp.transpose(va3.astype(BF16), (1, 0, 2))
    prm = _sink_prm(W["ev_sinks"][0])
    oaT, lse_a = _attn_fwd(qaT, ka, _v_with_ones(va3), tile=128, window=WINDOW, sink=prm, name="swa_fwd")
    cqn = _rms_fwd(c_q, W["ev_cq_norm"], "ev_cq_norm")
    q_all = _mm(cqn, W["ev_w_uq"], name="ev_uq")
    ckvn = _rms_fwd(c_kv, W["ev_ckv_norm"], "ev_ckv_norm")
    kv_all = _mm(ckvn, W["ev_w_ukv"], name="ev_ukv")
    cos, sin = _rope_tables(S)
    cos8, sin8 = jnp.tile(cos, (1, B_HEADS)), jnp.tile(sin, (1, B_HEADS))
    q1, q2 = _rope(q_all[:, 512:640], q_all[:, 640:768], cos8, sin8, "ev_rope_q")
    k1, k2 = _rope(kr1, kr2, cos, sin, "ev_rope_k")
    half = B_ROPE // 2
    scale = (B_NOPE + B_ROPE) ** -0.5
    qb, qbT = _rows_and_cols(jnp.concatenate([q_all[:, :512].reshape(S, B_HEADS, B_NOPE), q1.reshape(S, B_HEADS, half), q2.reshape(S, B_HEADS, half)], axis=-1) * scale)
    kro = jnp.broadcast_to(jnp.concatenate([k1, k2], axis=1)[:, None, :], (S, B_HEADS, B_ROPE))
    kb, kbT = _rows_and_cols(jnp.concatenate([kv_all[:, :512].reshape(S, B_HEADS, B_NOPE), kro], axis=-1))
    vb3 = kv_all[:, 512:].reshape(S, B_HEADS, B_V)
    vb = jnp.transpose(vb3.astype(BF16), (1, 0, 2))
    obT, lse_b = _attn_fwd(qbT, kb, _v_with_ones(vb3), tile=256, name="mla_fwd")
    cat = jnp.concatenate([_from_T(oaT), _from_T(obT)], axis=1)
    out = _mm_w128(cat, W["G1"], G1_128["ev_w_out"], res=h, name="ev_out")
    return out, (hn, proj, (qa, qaT, ka, kaT, va, oaT, lse_a), prm, cqn, ckvn, (qb, qbT, kb, kbT, vb, obT, lse_b), cat)


def _even_bwd(dout, saved, W, GB):
    hn, proj, (qa, qaT, ka, kaT, va, oaT, lse_a), prm, cqn, ckvn, (qb, qbT, kb, kbT, vb, obT, lse_b), cat = saved
    S = hn.shape[0]
    G = {}
    dcat = _mm_w128(dout, W["G1"], G1_128["ev_w_out"], tb=True, out=BF16, name="ev_out_dx")
    GB["g1"] = _mm_w128_dw(cat, dout, G1_128["ev_w_out"], GB["g1"], "ev_out_dw")
    doa, doaT = _rows_and_cols(dcat[:, :512].reshape(S, A_HEADS, A_HEAD_DIM))
    dqaT, dka, dva, dsink = _attn_bwd(qa, qaT, ka, kaT, va, oaT, doa, doaT, lse_a, tile=128, window=WINDOW, sink=prm, name="swa_bwd")
    G["ev_sinks"] = dsink[:, 0, 0]
    dqa = _from_T(dqaT[:, :A_HEAD_DIM, :]) * A_HEAD_DIM ** -0.5
    dka = dka[:, :, :A_HEAD_DIM].reshape(A_KV_HEADS, A_GROUP, S, A_HEAD_DIM).sum(axis=1)
    dva = dva.reshape(A_KV_HEADS, A_GROUP, S, A_HEAD_DIM).sum(axis=1)
    dob, dobT = _rows_and_cols(dcat[:, 512:].reshape(S, B_HEADS, B_V))
    dqbT, dkb, dvb = _attn_bwd(qb, qbT, kb, kbT, vb, obT, dob, dobT, lse_b, tile=256, name="mla_bwd")
    half = B_ROPE // 2
    dqb = jnp.transpose(dqbT, (2, 0, 1)) * (B_NOPE + B_ROPE) ** -0.5
    dkb = jnp.transpose(dkb, (1, 0, 2))
    cos, sin = _rope_tables(S)
    cos8, sin8 = jnp.tile(cos, (1, B_HEADS)), jnp.tile(sin, (1, B_HEADS))
    dq1, dq2 = _rope(dqb[:, :, B_NOPE:B_NOPE + half].reshape(S, -1), dqb[:, :, B_NOPE + half:].reshape(S, -1), cos8, -sin8, "ev_rope_q_bwd")
    dq_all = jnp.concatenate([dqb[:, :, :B_NOPE].reshape(S, -1), dq1, dq2], axis=1).astype(BF16)
    dkr = dkb[:, :, B_NOPE:].sum(axis=1)
    dk1, dk2 = _rope(dkr[:, :half], dkr[:, half:], cos, -sin, "ev_rope_k_bwd")
    dkv_all = jnp.concatenate([dkb[:, :, :B_NOPE].reshape(S, -1), _unheads(dvb)], axis=1).astype(BF16)
    G["ev_w_uq"] = _mm(cqn, dq_all, ta=True, name="ev_uq_dw")
    dcqn = _mm(dq_all, W["ev_w_uq"], tb=True, name="ev_uq_dx")
    dc_q, G["ev_cq_norm"] = _rms_bwd(dcqn, proj[:, 768:1024], W["ev_cq_norm"], None, "ev_cq_norm_bwd")
    G["ev_w_ukv"] = _mm(ckvn, dkv_all, ta=True, name="ev_ukv_dw")
    dckvn = _mm(dkv_all, W["ev_w_ukv"], tb=True, name="ev_ukv_dx")
    dc_kv, G["ev_ckv_norm"] = _rms_bwd(dckvn, proj[:, 1024:1152], W["ev_ckv_norm"], None, "ev_ckv_norm_bwd")
    dproj = jnp.concatenate([dqa, _unheads(dka), _unheads(dva), dc_q, dc_kv, dk1, dk2,
                             jnp.zeros((S, EVEN_IN_PAD - EVEN_IN), F32)], axis=1).astype(BF16)
    G["ev_w_in"] = _mm(hn, dproj, ta=True, name="ev_in_dw")
    dhn = _mm(dproj, W["ev_w_in"], tb=True, name="ev_in_dx")
    return dhn, G


def _odd_fwd(hn, h, W):
    S = hn.shape[0]
    w = C_HEADS * C_HEAD_DIM
    proj = _mm(hn, W["od_w_in"], name="od_in")
    f_logit = proj[:, 3 * w: 3 * w + C_HEADS]
    logf = _logsig_fwd(f_logit, W["od_b_f"], "od_logsig")
    logc = _cumsum(logf, False, "od_cumsum")
    parts = [p[:, :, None] for p in _exact3(logc)]
    ones = [jnp.ones((S, C_HEADS, 1), F32)] * 3
    pad = [jnp.zeros((S, C_HEADS, QK_PAD - C_HEAD_DIM - 6), F32)]
    q3 = (proj[:, :w] * C_HEAD_DIM ** -0.5).reshape(S, C_HEADS, C_HEAD_DIM)
    k3 = proj[:, w:2 * w].reshape(S, C_HEADS, C_HEAD_DIM)
    q, qT = _rows_and_cols(jnp.concatenate([q3] + parts + ones + pad, axis=-1))
    k, kT = _rows_and_cols(jnp.concatenate([k3] + ones + [-p for p in parts] + pad, axis=-1))
    v3 = proj[:, 2 * w:3 * w].reshape(S, C_HEADS, C_HEAD_DIM)
    v = jnp.transpose(v3.astype(BF16), (1, 0, 2))
    oT, lse = _attn_fwd(qT, k, _v_with_ones(v3), tile=256, name="fox_fwd")
    cat = _from_T(oT)
    out = _mm_w128(cat, W["G1"], G1_128["od_w_out"], res=h, name="od_out")
    return out, (hn, q, qT, k, kT, v, f_logit, oT, lse, cat)


def _odd_bwd(dout, saved, W, GB):
    hn, q, qT, k, kT, v, f_logit, oT, lse, cat = saved
    S = hn.shape[0]
    G = {}
    dcat = _mm_w128(dout, W["G1"], G1_128["od_w_out"], tb=True, out=BF16, name="od_out_dx")
    GB["g1"] = _mm_w128_dw(cat, dout, G1_128["od_w_out"], GB["g1"], "od_out_dw")
    do, doT = _rows_and_cols(dcat.reshape(S, C_HEADS, C_HEAD_DIM))
    dqT, dk, dv = _attn_bwd(q, qT, k, kT, v, oT, do, doT, lse, tile=256, name="fox_bwd")
    dlogc = jnp.transpose(dqT[:, C_HEAD_DIM, :] - dk[:, :, C_HEAD_DIM + 3])
    dlogf = _cumsum(dlogc, True, "od_cumsum_bwd")
    df, db = _logsig_bwd(dlogf, f_logit, W["od_b_f"], "od_logsig_bwd")
    G["od_b_f"] = db
    dproj = jnp.concatenate([_from_T(dqT[:, :C_HEAD_DIM, :]) * C_HEAD_DIM ** -0.5, _unheads(dk[:, :, :C_HEAD_DIM]), _unheads(dv), df,
                             jnp.zeros((S, ODD_IN_PAD - ODD_IN), F32)], axis=1).astype(BF16)
    G["od_w_in"] = _mm(hn, dproj, ta=True, name="od_in_dw")
    dhn = _mm(dproj, W["od_w_in"], tb=True, name="od_in_dx")
    return dhn, G


def _local_step(x, p, target, W):
    h = x
    saved = []
    for i in range(DEPTH):
        t = f"l{i}"
        h1, s_a = _ffn_fwd(h, W["ffa_norm"][i:i + 1], W, 0, i, f"{t}_ffa")
        nm = _rms_fwd(h1, W["mix_norm"][i:i + 1], f"{t}_mix_norm")
        h2, s_m = (_even_fwd if i % 2 == 0 else _odd_fwd)(nm, h1, W)
        h3, s_b = _ffn_fwd(h2, W["ffb_norm"][i:i + 1], W, 1, i, f"{t}_ffb")
        npl = _rms_fwd(h3, W["ple_norm"][i:i + 1], f"{t}_ple_norm")
        gpre = _mm_w128(npl, W["G1"], G1_128[f"ple_w_gate{i}"], name=f"{t}_ple_gate")
        pp = _mm(p[i], W["ple_w_proj"][i], name=f"{t}_ple_proj")
        h4 = _ple_fwd(h3, gpre, pp, f"{t}_ple")
        saved.append((s_a, h1, s_m, s_b, h3, npl, gpre, pp))
        h = h4
    dh, g_final, loss_cols = _final_fwd_bwd(h, W["final_norm"], target, "final")
    G = {"final_norm": g_final}
    GB = {"g1": lax.empty((N_DEV, G1_ROWS, D_MODEL), F32), "g2": lax.empty((N_DEV, G2_ROWS, FF_BLK), F32)}
    per_layer = {n: [None] * DEPTH for n in ("ffa_norm", "mix_norm", "ffb_norm", "ple_norm", "ple_w_proj")}
    for i in reversed(range(DEPTH)):
        t = f"l{i}"
        s_a, h1, s_m, s_b, h3, npl, gpre, pp = saved[i]
        dgpre, dpp = _ple_bwd(dh, gpre, pp, f"{t}_ple_bwd")
        per_layer["ple_w_proj"][i] = _mm(p[i], dpp, ta=True, name=f"{t}_ple_proj_dw")
        GB["g1"] = _mm_w128_dw(npl, dgpre, G1_128[f"ple_w_gate{i}"], GB["g1"], f"{t}_ple_gate_dw")
        dnpl = _mm_w128(dgpre, W["G1"], G1_128[f"ple_w_gate{i}"], tb=True, name=f"{t}_ple_gate_dx")
        dh, per_layer["ple_norm"][i] = _rms_bwd(dnpl, h3, W["ple_norm"][i:i + 1], dh, f"{t}_ple_norm_bwd")
        dh, per_layer["ffb_norm"][i] = _ffn_bwd(dh, s_b, W["ffb_norm"][i:i + 1], W, GB, 1, i, f"{t}_ffb")
        dnm, g_mix = (_even_bwd if i % 2 == 0 else _odd_bwd)(dh, s_m, W, GB)
        G.update(g_mix)
        dh, per_layer["mix_norm"][i] = _rms_bwd(dnm, h1, W["mix_norm"][i:i + 1], dh, f"{t}_mix_norm_bwd")
        dh, per_layer["ffa_norm"][i] = _ffn_bwd(dh, s_a, W["ffa_norm"][i:i + 1], W, GB, 0, i, f"{t}_ffa")
    for n in ("ffa_norm", "mix_norm", "ffb_norm", "ple_norm"):
        G[n] = jnp.concatenate(per_layer[n], axis=0)
    G["ple_w_proj"] = per_layer["ple_w_proj"]
    return loss_cols, dh, GB, G


def kernel(x, p, ffa_norm, ffa_w_gate_up, ffa_w_down, mix_norm, ffb_norm, ffb_w_gate_up, ffb_w_down, ple_norm, ple_w_gate, ple_w_proj, ev_w_in, ev_sinks, ev_cq_norm, ev_w_uq, ev_ckv_norm, ev_w_ukv, ev_w_out, od_w_in, od_b_f, od_w_out, final_norm, loss_target, m_ffa_norm, m_ffa_w_gate_up, m_ffa_w_down, m_mix_norm, m_ffb_norm, m_ffb_w_gate_up, m_ffb_w_down, m_ple_norm, m_ple_w_gate, m_ple_w_proj, m_ev_w_in, m_ev_sinks, m_ev_cq_norm, m_ev_w_uq, m_ev_ckv_norm, m_ev_w_ukv, m_ev_w_out, m_od_w_in, m_od_b_f, m_od_w_out, m_final_norm, v_ffa_norm, v_ffa_w_gate_up, v_ffa_w_down, v_mix_norm, v_ffb_norm, v_ffb_w_gate_up, v_ffb_w_down, v_ple_norm, v_ple_w_gate, v_ple_w_proj, v_ev_w_in, v_ev_sinks, v_ev_cq_norm, v_ev_w_uq, v_ev_ckv_norm, v_ev_w_ukv, v_ev_w_out, v_od_w_in, v_od_b_f, v_od_w_out, v_final_norm):
    given = dict(locals())
    w_in = {n: given[n] for n in WEIGHTS}

    G1, G2, G3 = _all_gather([_in_slot(g) for g in _local_groups(w_in, BF16)])
    W = {n: w_in[n] for n in SMALL}
    W["final_norm"] = final_norm.reshape(1, -1)
    W.update(_misc_weights(G3))
    W.update(G1=G1, G2=G2, G2v=G2.reshape(2, 4, G2_ROWS, FF_BLK))

    loss_cols, dx, GB, G = _local_step(x[0], p[:, 0], loss_target[0], W)

    grads = _ungroup_local(*[r for r in _reduce_scatter([GB["g1"], GB["g2"], _misc_grads(G)])])
    layout = [(n, int(np.prod(w_in[n].shape))) for n in SMALL]
    vec = jnp.concatenate([G[n].astype(F32).reshape(-1) for n, _ in layout] + [jnp.sum(loss_cols).reshape(1)])
    vec = jnp.pad(vec, (0, N_DEV * SMALL_COLS - vec.shape[0])).reshape(N_DEV, SMALL_COLS)
    vec = _all_reduce_small(vec).reshape(-1)
    off = 0
    for n, size in layout:
        grads[n] = vec[off: off + size].reshape(w_in[n].shape)
        off += size
    loss = vec[off]

    delta, new_m, new_v = {}, {}, {}
    for n in WEIGHTS:
        shp = w_in[n].shape
        as2d = (lambda a: a.reshape(1, -1)) if len(shp) == 1 else (lambda a: a)
        d, nm, nv = _adamw(as2d(w_in[n]), as2d(grads[n]), as2d(given["m_" + n]), as2d(given["v_" + n]), f"adamw_{n}")
        delta[n], new_m[n], new_v[n] = d.reshape(shp), nm.reshape(shp), nv.reshape(shp)
    return (loss, dx[None], *[grads[n] for n in WEIGHTS], *[delta[n] for n in WEIGHTS],
            *[new_m[n] for n in WEIGHTS], *[new_v[n] for n in WEIGHTS])
```

```python
import functools

import numpy as np
import jax
import jax.numpy as jnp
from jax import lax
from jax.experimental import pallas as pl
from jax.experimental.pallas import tpu as pltpu

F32 = jnp.float32
BF16 = jnp.bfloat16
MESH = pl.DeviceIdType.MESH

D_MODEL = 1024
D_FF = 2816
RMS_EPS = 1e-6
PLE_DIM = 256
A_HEADS, A_KV_HEADS, A_HEAD_DIM, WINDOW = 8, 2, 64, 128
A_GROUP = A_HEADS // A_KV_HEADS
B_HEADS, B_Q_LORA, B_KV_LORA, B_NOPE, B_ROPE, B_V = 8, 256, 128, 64, 32, 64
ROPE_THETA = 10000.0
C_HEADS, C_HEAD_DIM = 16, 64
EVEN_IN = 1184
EVEN_IN_PAD = 1280
ODD_IN = 3088
ODD_IN_PAD = 3200
DEPTH = 2
ADAM_LR, ADAM_B1, ADAM_B2, ADAM_EPS, ADAM_WD, ADAM_STEP = 0.001, 0.9, 0.999, 1e-08, 0.01, 10

N_DEV = 8
LANES = 128
SUBLANES = 8
EW_TILE_BYTES = 3 << 20
MM_VMEM_BYTES = 26 << 20
NEG = -1e30
ATTN_TILE = 512
SWA_TILE = 256
QK_PAD = 80

FF_BLK = D_FF // 4
DOWN_ROWS = D_FF // N_DEV
G1_ROWS, G2_ROWS, G3_ROWS, G3_COLS = 1920, 4096, 1024, 768
G1_128 = {"ple_w_gate0": 11, "ple_w_gate1": 12, "ev_w_out": 13, "od_w_out": 14}
OD_C, EV_C, STRIP_C = 386, 148, 128
STRIP0 = OD_C + EV_C

SMALL = ["ffa_norm", "mix_norm", "ffb_norm", "ple_norm", "ev_sinks", "ev_cq_norm", "ev_ckv_norm", "od_b_f", "final_norm"]
WEIGHTS = ["ffa_norm", "ffa_w_gate_up", "ffa_w_down", "mix_norm", "ffb_norm", "ffb_w_gate_up", "ffb_w_down", "ple_norm",
           "ple_w_gate", "ple_w_proj", "ev_w_in", "ev_sinks", "ev_cq_norm", "ev_w_uq", "ev_ckv_norm", "ev_w_ukv", "ev_w_out",
           "od_w_in", "od_b_f", "od_w_out", "final_norm"]
SMALL_COLS = 1280


def _divisor(n, cap, mult):
    if n <= cap:
        return n
    for t in range(cap - cap % mult, 0, -mult):
        if n % t == 0:
            return t
    raise ValueError(f"no tile for {n} under {cap} in steps of {mult}")


def _lanes(c):
    return -(-c // LANES) * LANES


def _ew(fn, rows, vecs, outs, reds=(), *, name):
    R = rows[0].shape[0]
    per_row = sum(_lanes(a.shape[1]) * a.dtype.itemsize for a in rows) + sum(_lanes(c) * jnp.dtype(d).itemsize for c, d in outs)
    tm = _divisor(R, max(16, EW_TILE_BYTES // per_row // 16 * 16), 16) if R % 16 == 0 else R
    n_r, n_v, n_o = len(rows), len(vecs), len(outs)

    def body(*refs):
        ins = [r[...] for r in refs[: n_r + n_v]]
        res = fn(*ins)
        if not isinstance(res, (tuple, list)):
            res = (res,)
        o_refs = refs[n_r + n_v: n_r + n_v + n_o]
        r_refs = refs[n_r + n_v + n_o:]
        for ref, val in zip(o_refs, res[:n_o]):
            ref[...] = val.astype(ref.dtype)
        if r_refs:
            @pl.when(pl.program_id(0) == 0)
            def _():
                for ref in r_refs:
                    ref[...] = jnp.zeros_like(ref)
            for ref, val in zip(r_refs, res[n_o:]):
                ref[...] += val

    in_specs = [pl.BlockSpec((tm, a.shape[1]), lambda i: (i, 0)) for a in rows]
    in_specs += [pl.BlockSpec((1, a.shape[1]), lambda i: (0, 0)) for a in vecs]
    out_specs = [pl.BlockSpec((tm, c), lambda i: (i, 0)) for c, _ in outs]
    out_specs += [pl.BlockSpec((1, c), lambda i: (0, 0)) for c in reds]
    out_shape = [jax.ShapeDtypeStruct((R, c), d) for c, d in outs] + [jax.ShapeDtypeStruct((1, c), F32) for c in reds]
    res = pl.pallas_call(body, name=name, grid=(R // tm,), in_specs=in_specs, out_specs=out_specs, out_shape=out_shape)(*rows, *vecs)
    return res[0] if len(res) == 1 else res


def _rms_fwd(x, w, name):
    def fn(x, w):
        y = x * lax.rsqrt(jnp.mean(x * x, axis=-1, keepdims=True) + RMS_EPS)
        return y * w
    return _ew(fn, [x], [w], [(x.shape[1], BF16)], name=name)


def _rms_bwd(dn, x, w, dres, name):
    def fn(dn, x, *rest):
        w = rest[-1]
        r = lax.rsqrt(jnp.mean(x * x, axis=-1, keepdims=True) + RMS_EPS)
        xh = x * r
        gw = dn * w
        dx = r * (gw - xh * jnp.mean(gw * xh, axis=-1, keepdims=True))
        if len(rest) == 2:
            dx = dx + rest[0]
        return dx, jnp.sum(dn * xh, axis=0, keepdims=True)
    rows = [dn, x] + ([dres] if dres is not None else [])
    return _ew(fn, rows, [w], [(x.shape[1], F32)], [x.shape[1]], name=name)


def _ple_fwd(h, gpre, pp, name):
    return _ew(lambda h, g, q: h + jax.nn.sigmoid(g) * q, [h, gpre, pp], [], [(h.shape[1], F32)], name=name)


def _ple_bwd(dh, gpre, pp, name):
    def fn(dh, g, q):
        sg = jax.nn.sigmoid(g)
        return dh * q * (sg * (1.0 - sg)), dh * sg
    return _ew(fn, [dh, gpre, pp], [], [(dh.shape[1], BF16), (dh.shape[1], BF16)], name=name)


def _rope(x1, x2, cos, sin, name):
    c = x1.shape[1]
    return _ew(lambda a, b, co, si: (a * co - b * si, a * si + b * co), [x1, x2, cos, sin], [], [(c, F32), (c, F32)], name=name)


def _logsig_fwd(f, b, name):
    def fn(f, b):
        z = f + b
        return jnp.minimum(z, 0.0) - jnp.log(1.0 + jnp.exp(-jnp.abs(z)))
    return _ew(fn, [f], [b], [(f.shape[1], F32)], name=name)


def _logsig_bwd(dlogf, f, b, name):
    def fn(d, f, b):
        df = d * jax.nn.sigmoid(-(f + b))
        return df, jnp.sum(df, axis=0, keepdims=True)
    return _ew(fn, [dlogf, f], [b], [(f.shape[1], F32)], [f.shape[1]], name=name)


def _final_fwd_bwd(h, w, target, name):
    d = h.shape[1]

    def fn(h, t, w):
        r = lax.rsqrt(jnp.mean(h * h, axis=-1, keepdims=True) + RMS_EPS)
        xh = h * r
        y = xh * w
        err = y - t
        dy = err * (1.0 / d)
        gw = dy * w
        dx = r * (gw - xh * jnp.mean(gw * xh, axis=-1, keepdims=True))
        return dx, jnp.sum(dy * xh, axis=0, keepdims=True), jnp.sum(err * err, axis=0, keepdims=True) * (0.5 / d)
    return _ew(fn, [h, target], [w], [(d, F32)], [d, d], name=name)


def _adamw(w, g, m, v, name):
    shape = w.shape
    c = shape[-1]
    w2, g2, m2, v2 = (a.reshape(-1, c) for a in (w, g, m, v))

    def fn(w, g, m, v):
        m = ADAM_B1 * m + (1.0 - ADAM_B1) * g
        v = ADAM_B2 * v + (1.0 - ADAM_B2) * jnp.square(g)
        m_hat = m / (1.0 - ADAM_B1 ** ADAM_STEP)
        v_hat = v / (1.0 - ADAM_B2 ** ADAM_STEP)
        delta = -ADAM_LR * (m_hat / (jnp.sqrt(v_hat) + ADAM_EPS) + ADAM_WD * w)
        return delta, m, v
    d, nm, nv = _ew(fn, [w2, g2, m2, v2], [], [(c, F32)] * 3, name=name)
    return d.reshape(shape), nm.reshape(shape), nv.reshape(shape)


def _split3(v):
    hi = v.astype(BF16)
    r1 = v - hi.astype(F32)
    mid = r1.astype(BF16)
    lo = (r1 - mid.astype(F32)).astype(BF16)
    return hi, mid, lo


def _cumsum(x, reverse, name):
    S, C = x.shape
    tm = _divisor(S, 512, 16)
    nt = S // tm

    def body(x_ref, o_ref, carry):
        @pl.when(pl.program_id(0) == 0)
        def _():
            carry[...] = jnp.zeros_like(carry)
        r = lax.broadcasted_iota(jnp.int32, (tm, tm), 0)
        c = lax.broadcasted_iota(jnp.int32, (tm, tm), 1)
        tri = jnp.where((c >= r) if reverse else (c <= r), 1.0, 0.0).astype(BF16)
        xv = x_ref[...]
        acc = jnp.zeros((tm, C), F32)
        for part in _split3(xv):
            acc = acc + jnp.dot(tri, part, preferred_element_type=F32)
        o_ref[...] = acc + carry[...]
        carry[...] += jnp.sum(xv, axis=0, keepdims=True)

    idx = (lambda i: (nt - 1 - i, 0)) if reverse else (lambda i: (i, 0))
    return pl.pallas_call(
        body, name=name, grid=(nt,), in_specs=[pl.BlockSpec((tm, C), idx)], out_specs=pl.BlockSpec((tm, C), idx),
        out_shape=jax.ShapeDtypeStruct((S, C), F32), scratch_shapes=[pltpu.VMEM((1, C), F32)],
    )(x)


NN = (((1,), (0,)), ((), ()))
NT = (((1,), (1,)), ((), ()))
TN = (((0,), (0,)), ((), ()))


def _mm_call(name, grid, k_axis, a, a_spec, a2d, b, b_spec, b2d, dims, out_sds, out_spec, o2d, *,
             alpha=1.0, res=None, res_spec=None, into=None):
    nk = grid[k_axis]

    def body(*refs):
        a_ref, b_ref = refs[0], refs[1]
        res_ref = refs[2] if res is not None else None
        o_ref, acc_ref = refs[-2], refs[-1]
        k = pl.program_id(k_axis)

        @pl.when(k == 0)
        def _():
            acc_ref[...] = jnp.zeros_like(acc_ref)

        av = a_ref[...].reshape(a2d).astype(BF16)
        bv = b_ref[...].reshape(b2d).astype(BF16)
        acc_ref[...] += lax.dot_general(av, bv, dims, preferred_element_type=F32)

        @pl.when(k == nk - 1)
        def _():
            r = acc_ref[...]
            if alpha != 1.0:
                r = r * alpha
            if res_ref is not None:
                r = res_ref[...].reshape(o2d) + r
            o_ref[...] = r.reshape(o_ref.shape).astype(o_ref.dtype)

    in_specs, args = [a_spec, b_spec], [a, b]
    if res is not None:
        in_specs.append(res_spec)
        args.append(res)
    aliases = {}
    if into is not None:
        aliases = {len(args): 0}
        in_specs.append(pl.BlockSpec(memory_space=pl.ANY))
        args.append(into)
        out_sds = jax.ShapeDtypeStruct(into.shape, into.dtype)
    sem = tuple("arbitrary" if d == k_axis else "parallel" for d in range(len(grid)))
    return pl.pallas_call(
        body, name=name, grid=grid, in_specs=in_specs, out_specs=out_spec, out_shape=out_sds,
        scratch_shapes=[pltpu.VMEM(o2d, F32)], input_output_aliases=aliases,
        compiler_params=pltpu.CompilerParams(dimension_semantics=sem),
    )(*args)


def _mm(a, b, *, ta=False, tb=False, out=F32, res=None, alpha=1.0, name):
    K, M = a.shape if ta else a.shape[::-1]
    N = b.shape[0] if tb else b.shape[1]
    assert (b.shape[1] if tb else b.shape[0]) == K, (a.shape, b.shape, ta, tb)
    tk = _divisor(K, 1024, LANES)
    tn = _divisor(N, 1408, LANES)
    for cap in (1024, 512, 256, 128):
        tm = _divisor(M, cap, LANES if ta else 16)
        est = 2 * (tm * tk * a.dtype.itemsize + tk * tn * b.dtype.itemsize + tm * tn * jnp.dtype(out).itemsize)
        est += tm * tn * 4 + (2 * tm * tn * 4 if res is not None else 0)
        if est <= MM_VMEM_BYTES:
            break
    a_spec = pl.BlockSpec((tk, tm), lambda i, j, k: (k, i)) if ta else pl.BlockSpec((tm, tk), lambda i, j, k: (i, k))
    b_spec = pl.BlockSpec((tn, tk), lambda i, j, k: (j, k)) if tb else pl.BlockSpec((tk, tn), lambda i, j, k: (k, j))
    o_spec = pl.BlockSpec((tm, tn), lambda i, j, k: (i, j))
    dims = (((0 if ta else 1,), (1 if tb else 0,)), ((), ()))
    return _mm_call(name, (M // tm, N // tn, K // tk), 2, a, a_spec, (tk, tm) if ta else (tm, tk), b, b_spec,
                    (tn, tk) if tb else (tk, tn), dims, jax.ShapeDtypeStruct((M, N), out), o_spec, (tm, tn),
                    alpha=alpha, res=res, res_spec=o_spec)


def _w128_spec(blk):
    return pl.BlockSpec((N_DEV, 128, D_MODEL), lambda *_: (0, blk, 0))


def _mm_w128(a, G1, blk, *, tb=False, res=None, out=F32, name):
    S = a.shape[0]
    tm = _divisor(S, 512, 16)
    row = pl.BlockSpec((tm, D_MODEL), lambda i, k: (i, 0))
    return _mm_call(name, (S // tm, 1), 1, a, row, (tm, D_MODEL), G1, _w128_spec(blk), (D_MODEL, D_MODEL), NT if tb else NN,
                    jax.ShapeDtypeStruct((S, D_MODEL), out), row, (tm, D_MODEL), res=res, res_spec=row)


def _mm_w128_dw(a, b, blk, into, name):
    S = a.shape[0]
    tk = _divisor(S, 1024, 16)
    row = pl.BlockSpec((tk, D_MODEL), lambda i, k: (k, 0))
    return _mm_call(name, (1, S // tk), 1, a, row, (tk, D_MODEL), b, row, (tk, D_MODEL), TN, None, _w128_spec(blk),
                    (D_MODEL, D_MODEL), into=into)


def _ffn_gate_up(n, G2v, rb, name):
    S = n.shape[0]
    tm = _divisor(S, 512, 16)

    def body(n_ref, w_ref, gu_ref, act_ref):
        nv = n_ref[...]
        g = jnp.dot(nv, w_ref[0, 0], preferred_element_type=F32)
        u = jnp.dot(nv, w_ref[1, 0], preferred_element_type=F32)
        gu_ref[0, 0] = g.astype(BF16)
        gu_ref[1, 0] = u.astype(BF16)
        act_ref[0] = (g * jax.nn.sigmoid(g) * u).astype(BF16)

    return pl.pallas_call(
        body, name=name, grid=(4, S // tm),
        in_specs=[pl.BlockSpec((tm, D_MODEL), lambda j, i: (i, 0)), pl.BlockSpec((2, 1, D_MODEL, FF_BLK), lambda j, i: (0, j, rb, 0))],
        out_specs=[pl.BlockSpec((2, 1, tm, FF_BLK), lambda j, i: (0, j, i, 0)), pl.BlockSpec((1, tm, FF_BLK), lambda j, i: (j, i, 0))],
        out_shape=[jax.ShapeDtypeStruct((2, 4, S, FF_BLK), BF16), jax.ShapeDtypeStruct((4, S, FF_BLK), BF16)],
    )(n, G2v)


def _ffn_down(act, G1, ob, h, name):
    S = h.shape[0]
    tm = _divisor(S, 512, 16)
    row = pl.BlockSpec((tm, D_MODEL), lambda i, k: (i, 0))
    return _mm_call(name, (S // tm, 4), 1, act, pl.BlockSpec((1, tm, FF_BLK), lambda i, k: (k, i, 0)), (tm, FF_BLK),
                    G1, pl.BlockSpec((2, DOWN_ROWS, D_MODEL), lambda i, k: (k, ob, 0)), (FF_BLK, D_MODEL), NN,
                    jax.ShapeDtypeStruct((S, D_MODEL), F32), row, (tm, D_MODEL), alpha=0.5, res=h, res_spec=row)


def _ffn_down_dx(dh, G1, ob, gu, name):
    S = dh.shape[0]
    tm = _divisor(S, 512, 16)

    def body(dh_ref, w_ref, gu_ref, o_ref):
        w = w_ref[...].reshape(FF_BLK, D_MODEL)
        dact = lax.dot_general(dh_ref[...].astype(BF16), w, NT, preferred_element_type=F32) * 0.5
        g = gu_ref[0, 0].astype(F32)
        u = gu_ref[1, 0].astype(F32)
        sg = jax.nn.sigmoid(g)
        o_ref[0, 0] = (dact * u * (sg * (1.0 + g * (1.0 - sg)))).astype(BF16)
        o_ref[1, 0] = (dact * (g * sg)).astype(BF16)

    blk = pl.BlockSpec((2, 1, tm, FF_BLK), lambda j, i: (0, j, i, 0))
    return pl.pallas_call(
        body, name=name, grid=(4, S // tm),
        in_specs=[pl.BlockSpec((tm, D_MODEL), lambda j, i: (i, 0)), pl.BlockSpec((2, DOWN_ROWS, D_MODEL), lambda j, i: (j, ob, 0)), blk],
        out_specs=blk, out_shape=jax.ShapeDtypeStruct((2, 4, S, FF_BLK), BF16),
    )(dh, G1, gu)


def _ffn_down_dw(act, dh, ob, into, name):
    S = dh.shape[0]
    tk = _divisor(S, 1024, 16)
    return _mm_call(name, (4, S // tk), 1, act, pl.BlockSpec((1, tk, FF_BLK), lambda j, k: (j, k, 0)), (tk, FF_BLK),
                    dh, pl.BlockSpec((tk, D_MODEL), lambda j, k: (k, 0)), (tk, D_MODEL), TN, None,
                    pl.BlockSpec((2, DOWN_ROWS, D_MODEL), lambda j, k: (j, ob, 0)), (FF_BLK, D_MODEL), alpha=0.5, into=into)


def _ffn_gate_up_dw(n, dgu8, rb, into, name):
    S = n.shape[0]
    tk = _divisor(S, 1024, 16)
    return _mm_call(name, (N_DEV, S // tk), 1, n, pl.BlockSpec((tk, D_MODEL), lambda b, k: (k, 0)), (tk, D_MODEL),
                    dgu8, pl.BlockSpec((1, tk, FF_BLK), lambda b, k: (b, k, 0)), (tk, FF_BLK), TN, None,
                    pl.BlockSpec((1, D_MODEL, FF_BLK), lambda b, k: (b, rb, 0)), (D_MODEL, FF_BLK), into=into)


def _ffn_gate_up_dx(dgu8, G2, rb, name):
    S = dgu8.shape[1]
    tm = _divisor(S, 512, 16)
    row = pl.BlockSpec((tm, D_MODEL), lambda i, k: (i, 0))
    return _mm_call(name, (S // tm, N_DEV), 1, dgu8, pl.BlockSpec((1, tm, FF_BLK), lambda i, k: (k, i, 0)), (tm, FF_BLK),
                    G2, pl.BlockSpec((1, D_MODEL, FF_BLK), lambda i, k: (k, rb, 0)), (D_MODEL, FF_BLK), NT,
                    jax.ShapeDtypeStruct((S, D_MODEL), F32), row, (tm, D_MODEL))


def _unheads(x):
    h, S, d = x.shape
    return jnp.transpose(x, (1, 0, 2)).reshape(S, h * d)


def _exact3(v):
    rnd = lambda a: lax.reduce_precision(a, exponent_bits=8, mantissa_bits=7)
    hi = rnd(v)
    mid = rnd(v - hi)
    return hi, mid, rnd(v - hi - mid)


def _causal_mask(st, i, j, tq, tk, window):
    dist = (i * tq + lax.broadcasted_iota(jnp.int32, (tk, tq), 1)) - (j * tk + lax.broadcasted_iota(jnp.int32, (tk, tq), 0))
    mask = dist >= 0
    if window is not None:
        mask = mask & (dist < window)
    return jnp.where(mask, st, NEG)


def _attn_fwd(qT, k, vT1, *, tile, hb, window=None, sink=None, name):
    H, dqk, S = qT.shape
    G = H // k.shape[0]
    dvp = vT1.shape[1]
    dv = dvp - 16
    tq = tk = tile
    assert H % hb == 0 and (G == 1 or G % hb == 0)
    kvb = hb if G == 1 else 1

    def body(*refs):
        q_ref, k_ref, v_ref = refs[:3]
        o_ref, lse_ref = refs[-2], refs[-1]
        i = pl.program_id(1)
        carry = []
        for a in range(hb):
            if sink is not None:
                carry.append(jnp.zeros((1, tq), F32) + refs[3][a, :, 0:1])
                carry.append(jnp.where(lax.broadcasted_iota(jnp.int32, (dvp, tq), 0) == dv, 1.0, 0.0))
            else:
                carry.append(jnp.full((1, tq), NEG, F32))
                carry.append(jnp.zeros((dvp, tq), F32))

        def step(j, carry, masked):
            off = pl.multiple_of(j * tk, tk)
            out = []
            for a in range(hb):
                m, acc = carry[2 * a], carry[2 * a + 1]
                kv = a if kvb > 1 else 0
                st = jnp.dot(k_ref[kv, pl.ds(off, tk), :], q_ref[a], preferred_element_type=F32)
                if masked:
                    st = _causal_mask(st, i, j, tq, tk, window)
                m_new = jnp.maximum(m, jnp.max(st, axis=0, keepdims=True))
                pt = jnp.exp(st - m_new).astype(BF16)
                acc = jnp.exp(m - m_new) * acc + jnp.dot(v_ref[kv, :, pl.ds(off, tk)], pt, preferred_element_type=F32)
                out += [m_new, acc]
            return tuple(out)

        carry = tuple(carry)
        if window is None:
            carry = lax.fori_loop(0, i, functools.partial(step, masked=False), carry)
            carry = step(i, carry, True)
        else:
            lo = jnp.maximum((i * tq - (window - 1)) // tk, 0)
            carry = lax.fori_loop(lo, i + 1, functools.partial(step, masked=True), carry)
        for a in range(hb):
            m, acc = carry[2 * a], carry[2 * a + 1]
            l = acc[dv:dv + 1, :]
            o_ref[a] = acc[:dv, :] / l
            lse_ref[a] = m + jnp.log(l)

    kv_idx = (lambda b: b) if G == 1 else (lambda b: (b * hb) // G)
    in_specs = [
        pl.BlockSpec((hb, dqk, tq), lambda b, i: (b, 0, i)),
        pl.BlockSpec((kvb, S, dqk), lambda b, i: (kv_idx(b), 0, 0)),
        pl.BlockSpec((kvb, dvp, S), lambda b, i: (kv_idx(b), 0, 0)),
    ]
    args = [qT, k, vT1]
    if sink is not None:
        in_specs += [pl.BlockSpec((hb, 1, LANES), lambda b, i: (b, 0, 0))]
        args += [sink]
    return pl.pallas_call(
        body, name=name, grid=(H // hb, S // tq), in_specs=in_specs,
        out_specs=[pl.BlockSpec((hb, dv, tq), lambda b, i: (b, 0, i)), pl.BlockSpec((hb, 1, tq), lambda b, i: (b, 0, i))],
        out_shape=[jax.ShapeDtypeStruct((H, dv, S), F32), jax.ShapeDtypeStruct((H, 1, S), F32)],
    )(*args)


def _attn_bwd(q, qT, k, kT, v, oT, do, doT, lse, *, tile, hb, window=None, sink=None, name):
    H, S, dqk = q.shape
    G = H // k.shape[0]
    dv = v.shape[2]
    tq = tk = tile
    nq = S // tq
    has_p = sink is not None
    assert H % hb == 0 and (G == 1 or G % hb == 0)
    kvb = hb if G == 1 else 1

    def body(*refs):
        q_ref, qT_ref, k_ref, kT_ref, v_ref, oT_ref, do_ref, doT_ref, lse_ref = refs[:9]
        p_ref = refs[9] if has_p else None
        pos = 10 if has_p else 9
        dq_ref, dk_ref, dv_ref = refs[pos: pos + 3]
        ds_ref = refs[pos + 3] if has_p else None
        delta = refs[-1]
        j = pl.program_id(1)

        @pl.when(j == 0)
        def _():
            dq_ref[...] = jnp.zeros_like(dq_ref)
            for a in range(hb):
                drow = jnp.sum(doT_ref[a].astype(F32) * oT_ref[a], axis=0, keepdims=True)
                delta[a] = drow
                if has_p:
                    w = jnp.exp(p_ref[a, :, 0:1] - lse_ref[a])
                    ds_ref[a] = jnp.zeros((1, LANES), F32) - jnp.sum(w * drow, axis=1, keepdims=True)

        def step(i, carry, masked):
            off = pl.multiple_of(i * tq, tq)
            out = []
            for a in range(hb):
                dk, dvv = carry[2 * a], carry[2 * a + 1]
                kv = a if kvb > 1 else 0
                st = jnp.dot(k_ref[kv], qT_ref[a, :, pl.ds(off, tq)], preferred_element_type=F32)
                if masked:
                    st = _causal_mask(st, i, j, tq, tk, window)
                pt = jnp.exp(st - lse_ref[a, :, pl.ds(off, tq)])
                dvv = dvv + jnp.dot(pt.astype(BF16), do_ref[a, pl.ds(off, tq), :], preferred_element_type=F32)
                dpt = jnp.dot(v_ref[kv], doT_ref[a, :, pl.ds(off, tq)], preferred_element_type=F32)
                dsb = (pt * (dpt - delta[a, :, pl.ds(off, tq)])).astype(BF16)
                dk = dk + jnp.dot(dsb, q_ref[a, pl.ds(off, tq), :], preferred_element_type=F32)
                dq_ref[a, :, pl.ds(off, tq)] += jnp.dot(kT_ref[kv], dsb, preferred_element_type=F32)
                out += [dk, dvv]
            return tuple(out)

        carry = (jnp.zeros((tk, dqk), F32), jnp.zeros((tk, dv), F32)) * hb
        if window is None:
            carry = step(j, carry, True)
            carry = lax.fori_loop(j + 1, nq, functools.partial(step, masked=False), carry)
        else:
            hi = jnp.minimum(nq - 1, ((j + 1) * tk + window - 2) // tq)
            carry = lax.fori_loop(j, hi + 1, functools.partial(step, masked=True), carry)
        for a in range(hb):
            dk_ref[a] = carry[2 * a]
            dv_ref[a] = carry[2 * a + 1]

    kv_idx = (lambda b: b) if G == 1 else (lambda b: (b * hb) // G)
    rows = lambda d: pl.BlockSpec((hb, S, d), lambda b, j: (b, 0, 0))
    colsT = lambda d: pl.BlockSpec((hb, d, S), lambda b, j: (b, 0, 0))
    in_specs = [
        rows(dqk), colsT(dqk),
        pl.BlockSpec((kvb, tk, dqk), lambda b, j: (kv_idx(b), j, 0)),
        pl.BlockSpec((kvb, dqk, tk), lambda b, j: (kv_idx(b), 0, j)),
        pl.BlockSpec((kvb, tk, dv), lambda b, j: (kv_idx(b), j, 0)),
        colsT(dv), rows(dv), colsT(dv),
        pl.BlockSpec((hb, 1, S), lambda b, j: (b, 0, 0)),
    ]
    args = [q, qT, k, kT, v, oT, do, doT, lse]
    if has_p:
        in_specs += [pl.BlockSpec((hb, 1, LANES), lambda b, j: (b, 0, 0))]
        args += [sink]
    out_specs = [colsT(dqk), pl.BlockSpec((hb, tk, dqk), lambda b, j: (b, j, 0)), pl.BlockSpec((hb, tk, dv), lambda b, j: (b, j, 0))]
    out_shape = [jax.ShapeDtypeStruct((H, dqk, S), F32), jax.ShapeDtypeStruct((H, S, dqk), F32), jax.ShapeDtypeStruct((H, S, dv), F32)]
    if has_p:
        out_specs += [pl.BlockSpec((hb, 1, LANES), lambda b, j: (b, 0, 0))]
        out_shape += [jax.ShapeDtypeStruct((H, 1, LANES), F32)]
    return pl.pallas_call(
        body, name=name, grid=(H // hb, S // tk), in_specs=in_specs, out_specs=out_specs, out_shape=out_shape,
        scratch_shapes=[pltpu.VMEM((hb, 1, S), F32)],
        compiler_params=pltpu.CompilerParams(dimension_semantics=("parallel", "arbitrary")),
    )(*args)


def _rows_and_cols(x3):
    xb = x3.astype(BF16)
    return jnp.transpose(xb, (1, 0, 2)), jnp.transpose(xb, (1, 2, 0))


def _v_with_ones(v3):
    S, h, _ = v3.shape
    vT = jnp.transpose(v3.astype(BF16), (1, 2, 0))
    return jnp.concatenate([vT, jnp.ones((h, 1, S), BF16), jnp.zeros((h, 15, S), BF16)], axis=1)


def _from_T(oT):
    h, d, S = oT.shape
    return jnp.transpose(oT, (2, 0, 1)).reshape(S, h * d)


def _coords():
    return lax.axis_index("x"), lax.axis_index("y"), lax.axis_index("c")


def _peer(axis):
    x, y, c = _coords()
    return {"x": (1 - x, y, c), "y": (x, 1 - y, c), "c": (x, y, 1 - c)}[axis]


HBM_SPEC = pl.BlockSpec(memory_space=pl.ANY)


def _all_gather(bufs):
    n = len(bufs)

    def body(*refs):
        outs = refs[n: 2 * n]
        send_sems, recv_sems = refs[2 * n], refs[2 * n + 1]
        x, y, c = _coords()
        me, sibling = (x, y, c), (x, y, 1 - c)
        chips = [(1 - x, y), (x, 1 - y), (1 - x, 1 - y)]

        def copy(t, k, block, to):
            px, py, pc = block
            ref = outs[t].at[4 * px + 2 * py + pc]
            return pltpu.make_async_remote_copy(ref, ref, send_sems.at[7 * t + k], recv_sems.at[7 * t + k], device_id=to, device_id_type=MESH)

        first = []
        for t in range(n):
            first.append(copy(t, 0, me, sibling))
            first += [copy(t, 1 + j, me, (*chip, c)) for j, chip in enumerate(chips)]
        for cp in first:
            cp.start()
        passed = []
        for j, chip in enumerate(chips):
            for t in range(n):
                copy(t, 1 + j, (*chip, c), me).wait_recv()
                cp = copy(t, 4 + j, (*chip, c), sibling)
                cp.start()
                passed.append(cp)
        for t in range(n):
            copy(t, 0, sibling, me).wait_recv()
            for j, chip in enumerate(chips):
                copy(t, 4 + j, (*chip, 1 - c), me).wait_recv()
        for cp in first + passed:
            cp.wait_send()

    return pl.pallas_call(
        body, name="all_gather", in_specs=[HBM_SPEC] * n, out_specs=[HBM_SPEC] * n,
        out_shape=[jax.ShapeDtypeStruct(b.shape, b.dtype) for b in bufs], input_output_aliases={t: t for t in range(n)},
        scratch_shapes=[pltpu.SemaphoreType.DMA((7 * n,)), pltpu.SemaphoreType.DMA((7 * n,))],
    )(*bufs)


def _in_slot(local):
    x, y, c = _coords()
    buf = jnp.zeros((N_DEV,) + local.shape, local.dtype)
    return lax.dynamic_update_slice(buf, local[None], (4 * x + 2 * y + c, 0, 0))


def _scatter_pair(vs, axis, name):
    n = len(vs)

    def body(*refs):
        send_sems, recv_sems = refs[2 * n], refs[2 * n + 1]
        me = lax.axis_index(axis)
        copies = []
        for t in range(n):
            v_ref, o_ref = refs[t], refs[n + t]
            src = v_ref.at[1 - me] if len(v_ref.shape) == 3 else v_ref.at[:, 1 - me]
            cp = pltpu.make_async_remote_copy(src, o_ref, send_sems.at[t], recv_sems.at[t], device_id=_peer(axis), device_id_type=MESH)
            cp.start()
            copies.append(cp)
        for cp in copies:
            cp.wait()

    out_shape = [jax.ShapeDtypeStruct(v.shape[:-3] + v.shape[-2:], v.dtype) for v in vs]
    return pl.pallas_call(
        body, name=name, in_specs=[HBM_SPEC] * n, out_specs=[HBM_SPEC] * n, out_shape=out_shape,
        scratch_shapes=[pltpu.SemaphoreType.DMA((n,)), pltpu.SemaphoreType.DMA((n,))],
    )(*vs)


def _add_kept(v, got, axis, out, name):
    R, C = v.shape[-2:]
    lead = v.shape[0] if v.ndim == 4 else 1
    tm = _divisor(R, max(16, EW_TILE_BYTES // (_lanes(C) * (v.dtype.itemsize + got.dtype.itemsize + jnp.dtype(out).itemsize)) // 16 * 16), 16)
    me = lax.axis_index(axis).astype(jnp.int32).reshape(1)
    v4 = v.reshape(lead, 2, R, C)
    g3 = got.reshape(lead, R, C)

    def body(me_ref, v_ref, g_ref, o_ref):
        o_ref[...] = (v_ref[0].astype(F32) + g_ref[...].astype(F32)).astype(o_ref.dtype)

    res = pl.pallas_call(
        body, name=name, out_shape=jax.ShapeDtypeStruct((lead, R, C), out),
        grid_spec=pltpu.PrefetchScalarGridSpec(
            num_scalar_prefetch=1, grid=(lead, R // tm),
            in_specs=[pl.BlockSpec((1, 1, tm, C), lambda b, i, me: (b, me[0], i, 0)), pl.BlockSpec((1, tm, C), lambda b, i, me: (b, i, 0))],
            out_specs=pl.BlockSpec((1, tm, C), lambda b, i, me: (b, i, 0))),
    )(me, v4, g3)
    return res


def _reduce_scatter(gs):
    vs = [g.reshape(4, 2, *g.shape[1:]) for g in gs]
    got = _scatter_pair(vs, "c", "reduce_scatter_c")
    vs = [_add_kept(v, r, "c", BF16, f"reduce_scatter_add_c{t}") for t, (v, r) in enumerate(zip(vs, got))]
    vs = [v.reshape(2, 2 * v.shape[1], v.shape[2]) for v in vs]
    got = _scatter_pair(vs, "x", "reduce_scatter_x")
    vs = [_add_kept(v, r, "x", BF16, f"reduce_scatter_add_x{t}")[0] for t, (v, r) in enumerate(zip(vs, got))]
    vs = [v.reshape(2, v.shape[0] // 2, v.shape[1]) for v in vs]
    got = _scatter_pair(vs, "y", "reduce_scatter_y")
    return [_add_kept(v, r, "y", F32, f"reduce_scatter_add_y{t}")[0] for t, (v, r) in enumerate(zip(vs, got))]


def _all_reduce_small(v):
    def body(v_ref, o_ref, buf, send_sems, recv_sems):
        x, y, c = _coords()
        me = 4 * x + 2 * y + c
        buf[me] = v_ref[...]
        copies = []
        for k in range(1, N_DEV):
            peer = tuple((1 - a) if (k >> s) & 1 else a for a, s in ((x, 2), (y, 1), (c, 0)))
            cp = pltpu.make_async_remote_copy(v_ref, buf.at[me], send_sems.at[k - 1], recv_sems.at[k - 1], device_id=peer, device_id_type=MESH)
            cp.start()
            copies.append(cp)
        for cp in copies:
            cp.wait()
        acc = buf[0]
        for d in range(1, N_DEV):
            acc = acc + buf[d]
        o_ref[...] = acc

    vm = pl.BlockSpec(memory_space=pltpu.VMEM)
    return pl.pallas_call(
        body, name="all_reduce_small", in_specs=[vm], out_specs=vm, out_shape=jax.ShapeDtypeStruct(v.shape, F32),
        scratch_shapes=[pltpu.VMEM((N_DEV,) + v.shape, F32), pltpu.SemaphoreType.DMA((N_DEV - 1,)), pltpu.SemaphoreType.DMA((N_DEV - 1,))],
    )(v)


def _local_groups(w, dtype):
    g1 = jnp.concatenate([w["ffa_w_down"].reshape(-1, D_MODEL), w["ffb_w_down"].reshape(-1, D_MODEL),
                          w["ple_w_gate"].reshape(-1, D_MODEL), w["ev_w_out"][0], w["od_w_out"][0]], axis=0).astype(dtype)
    g2 = jnp.concatenate([w["ffa_w_gate_up"].reshape(-1, FF_BLK), w["ffb_w_gate_up"].reshape(-1, FF_BLK)], axis=0).astype(dtype)
    strip = jnp.concatenate([w["ple_w_proj"].reshape(-1, STRIP_C), w["ev_w_ukv"][0], jnp.pad(w["ev_w_uq"][0], ((0, 0), (0, STRIP_C - 96))),
                             jnp.zeros((G3_ROWS - 896, STRIP_C), F32)], axis=0)
    g3 = jnp.concatenate([w["od_w_in"][0], w["ev_w_in"][0], strip, jnp.zeros((G3_ROWS, G3_COLS - STRIP0 - STRIP_C), F32)], axis=1).astype(dtype)
    return g1, g2, g3


def _ungroup_local(r1, r2, r3):
    out = {
        "ffa_w_down": r1[:704].reshape(2, DOWN_ROWS, D_MODEL), "ffb_w_down": r1[704:1408].reshape(2, DOWN_ROWS, D_MODEL),
        "ple_w_gate": r1[1408:1664].reshape(2, 128, D_MODEL), "ev_w_out": r1[1664:1792][None], "od_w_out": r1[1792:1920][None],
        "ffa_w_gate_up": r2[:2048].reshape(2, D_MODEL, FF_BLK), "ffb_w_gate_up": r2[2048:].reshape(2, D_MODEL, FF_BLK),
        "od_w_in": r3[:, :OD_C][None], "ev_w_in": r3[:, OD_C:STRIP0][None],
    }
    strip = r3[:, STRIP0:STRIP0 + STRIP_C]
    out["ple_w_proj"] = strip[:512].reshape(2, PLE_DIM, STRIP_C)
    out["ev_w_ukv"] = strip[512:640][None]
    out["ev_w_uq"] = strip[640:896, :96][None]
    return out


def _cols(a):
    return jnp.transpose(a, (1, 0, 2)).reshape(a.shape[1], -1)


def _blocks(g, c):
    return jnp.transpose(g.reshape(g.shape[0], N_DEV, c), (1, 0, 2))


def _uq_permute(w):
    r = w.shape[0]
    w3 = w.reshape(r, B_HEADS, B_NOPE + B_ROPE)
    half = B_ROPE // 2
    return jnp.concatenate([w3[:, :, :B_NOPE].reshape(r, -1), w3[:, :, B_NOPE:B_NOPE + half].reshape(r, -1), w3[:, :, B_NOPE + half:].reshape(r, -1)], axis=1)


def _uq_unpermute(g):
    r = g.shape[0]
    half = B_ROPE // 2
    n = B_HEADS * B_NOPE
    parts = [g[:, :n].reshape(r, B_HEADS, B_NOPE), g[:, n:n + B_HEADS * half].reshape(r, B_HEADS, half), g[:, n + B_HEADS * half:].reshape(r, B_HEADS, half)]
    return jnp.concatenate(parts, axis=2).reshape(r, -1)


def _ukv_permute(w):
    r = w.shape[0]
    return jnp.transpose(w.reshape(r, B_HEADS, 2, B_NOPE), (0, 2, 1, 3)).reshape(r, -1)


def _ukv_unpermute(g):
    r = g.shape[0]
    return jnp.transpose(g.reshape(r, 2, B_HEADS, B_NOPE), (0, 2, 1, 3)).reshape(r, -1)


def _misc_weights(G3):
    strip = G3[:, :, STRIP0:STRIP0 + STRIP_C]
    return {
        "od_w_in": jnp.pad(_cols(G3[:, :, :OD_C]), ((0, 0), (0, ODD_IN_PAD - ODD_IN))),
        "ev_w_in": jnp.pad(_cols(G3[:, :, OD_C:STRIP0]), ((0, 0), (0, EVEN_IN_PAD - EVEN_IN))),
        "ple_w_proj": [_cols(strip[:, i * PLE_DIM:(i + 1) * PLE_DIM]) for i in range(DEPTH)],
        "ev_w_ukv": _ukv_permute(_cols(strip[:, 512:640])),
        "ev_w_uq": _uq_permute(_cols(strip[:, 640:896, :96])),
    }


def _misc_grads(G):
    strip = jnp.concatenate([
        _blocks(G["ple_w_proj"][0], STRIP_C), _blocks(G["ple_w_proj"][1], STRIP_C), _blocks(_ukv_unpermute(G["ev_w_ukv"]), STRIP_C),
        jnp.pad(_blocks(_uq_unpermute(G["ev_w_uq"]), 96), ((0, 0), (0, 0), (0, STRIP_C - 96))),
        jnp.zeros((N_DEV, G3_ROWS - 896, STRIP_C), F32)], axis=1)
    return jnp.concatenate([_blocks(G["od_w_in"][:, :ODD_IN], OD_C), _blocks(G["ev_w_in"][:, :EVEN_IN], EV_C), strip,
                            jnp.zeros((N_DEV, G3_ROWS, G3_COLS - STRIP0 - STRIP_C), F32)], axis=2)


def _ffn_fwd(h, norm_w, W, f, i, tag):
    n = _rms_fwd(h, norm_w, f"{tag}_norm")
    gu, act = _ffn_gate_up(n, W["G2v"], 2 * f + i, f"{tag}_gate_up")
    out = _ffn_down(act, W["G1"], 2 * f + i, h, f"{tag}_down")
    return out, (h, n, gu, act)


def _ffn_bwd(dout, saved, norm_w, W, GB, f, i, tag):
    h, n, gu, act = saved
    S = h.shape[0]
    blk = 2 * f + i
    GB["g1"] = _ffn_down_dw(act, dout, blk, GB["g1"], f"{tag}_down_dw")
    dgu = _ffn_down_dx(dout, W["G1"], blk, gu, f"{tag}_down_dx").reshape(N_DEV, S, FF_BLK)
    GB["g2"] = _ffn_gate_up_dw(n, dgu, blk, GB["g2"], f"{tag}_gate_up_dw")
    dn = _ffn_gate_up_dx(dgu, W["G2"], blk, f"{tag}_gate_up_dx")
    return _rms_bwd(dn, h, norm_w, dout, f"{tag}_norm_bwd")


def _rope_tables(S):
    inv = ROPE_THETA ** (-jnp.arange(0, B_ROPE, 2, dtype=F32) / B_ROPE)
    ang = jnp.arange(S, dtype=F32)[:, None] * inv[None, :]
    return jnp.cos(ang), jnp.sin(ang)


def _alibi_columns(S):
    t = jnp.arange(S, dtype=jnp.int32)
    hi = ((t // 16) * 16).astype(F32)
    lo = (t % 16).astype(F32)
    slopes = 2.0 ** (-8.0 * jnp.arange(1, A_HEADS + 1, dtype=F32) / A_HEADS)
    zq = jnp.zeros((S, A_HEADS), F32)
    rest = QK_PAD - A_HEAD_DIM - 4
    qc = jnp.stack([-slopes[None, :] * hi[:, None], -slopes[None, :] * lo[:, None], zq + slopes[None, :], zq + slopes[None, :]] + [zq] * rest, axis=-1)
    one = jnp.ones((S, A_KV_HEADS), F32)
    zk = jnp.zeros((S, A_KV_HEADS), F32)
    kc = jnp.stack([one, one, zk + hi[:, None], zk + lo[:, None]] + [zk] * rest, axis=-1)
    return qc, kc


def _sink_prm(sinks):
    return jnp.zeros((A_HEADS, 1, LANES), F32).at[:, 0, 0].set(sinks.astype(F32))


def _even_fwd(hn, h, W):
    S = hn.shape[0]
    proj = _mm(hn, W["ev_w_in"], name="ev_in")
    a_q, a_k, a_v = proj[:, :512], proj[:, 512:640], proj[:, 640:768]
    c_q, c_kv = proj[:, 768:1024], proj[:, 1024:1152]
    kr1, kr2 = proj[:, 1152:1168], proj[:, 1168:1184]
    qc, kc = _alibi_columns(S)
    qa, qaT = _rows_and_cols(jnp.concatenate([(a_q * A_HEAD_DIM ** -0.5).reshape(S, A_HEADS, A_HEAD_DIM), qc], axis=-1))
    ka, kaT = _rows_and_cols(jnp.concatenate([a_k.reshape(S, A_KV_HEADS, A_HEAD_DIM), kc], axis=-1))
    va3 = a_v.reshape(S, A_KV_HEADS, A_HEAD_DIM)
    va = jnp.transpose(va3.astype(BF16), (1, 0, 2))
    prm = _sink_prm(W["ev_sinks"][0])
    oaT, lse_a = _attn_fwd(qaT, ka, _v_with_ones(va3), tile=SWA_TILE, hb=2, window=WINDOW, sink=prm, name="swa_fwd")
    cqn = _rms_fwd(c_q, W["ev_cq_norm"], "ev_cq_norm")
    q_all = _mm(cqn, W["ev_w_uq"], name="ev_uq")
    ckvn = _rms_fwd(c_kv, W["ev_ckv_norm"], "ev_ckv_norm")
    kv_all = _mm(ckvn, W["ev_w_ukv"], name="ev_ukv")
    cos, sin = _rope_tables(S)
    cos8, sin8 = jnp.tile(cos, (1, B_HEADS)), jnp.tile(sin, (1, B_HEADS))
    q1, q2 = _rope(q_all[:, 512:640], q_all[:, 640:768], cos8, sin8, "ev_rope_q")
    k1, k2 = _rope(kr1, kr2, cos, sin, "ev_rope_k")
    half = B_ROPE // 2
    scale = (B_NOPE + B_ROPE) ** -0.5
    qb, qbT = _rows_and_cols(jnp.concatenate([q_all[:, :512].reshape(S, B_HEADS, B_NOPE), q1.reshape(S, B_HEADS, half), q2.reshape(S, B_HEADS, half)], axis=-1) * scale)
    kro = jnp.broadcast_to(jnp.concatenate([k1, k2], axis=1)[:, None, :], (S, B_HEADS, B_ROPE))
    kb, kbT = _rows_and_cols(jnp.concatenate([kv_all[:, :512].reshape(S, B_HEADS, B_NOPE), kro], axis=-1))
    vb3 = kv_all[:, 512:].reshape(S, B_HEADS, B_V)
    vb = jnp.transpose(vb3.astype(BF16), (1, 0, 2))
    obT, lse_b = _attn_fwd(qbT, kb, _v_with_ones(vb3), tile=ATTN_TILE, hb=2, name="mla_fwd")
    cat = jnp.concatenate([_from_T(oaT), _from_T(obT)], axis=1)
    out = _mm_w128(cat, W["G1"], G1_128["ev_w_out"], res=h, name="ev_out")
    return out, (hn, proj, (qa, qaT, ka, kaT, va, oaT, lse_a), prm, cqn, ckvn, (qb, qbT, kb, kbT, vb, obT, lse_b), cat)


def _even_bwd(dout, saved, W, GB):
    hn, proj, (qa, qaT, ka, kaT, va, oaT, lse_a), prm, cqn, ckvn, (qb, qbT, kb, kbT, vb, obT, lse_b), cat = saved
    S = hn.shape[0]
    G = {}
    dcat = _mm_w128(dout, W["G1"], G1_128["ev_w_out"], tb=True, out=BF16, name="ev_out_dx")
    GB["g1"] = _mm_w128_dw(cat, dout, G1_128["ev_w_out"], GB["g1"], "ev_out_dw")
    doa, doaT = _rows_and_cols(dcat[:, :512].reshape(S, A_HEADS, A_HEAD_DIM))
    dqaT, dka, dva, dsink = _attn_bwd(qa, qaT, ka, kaT, va, oaT, doa, doaT, lse_a, tile=SWA_TILE, hb=2, window=WINDOW, sink=prm, name="swa_bwd")
    G["ev_sinks"] = dsink[:, 0, 0]
    dqa = _from_T(dqaT[:, :A_HEAD_DIM, :]) * A_HEAD_DIM ** -0.5
    dka = dka[:, :, :A_HEAD_DIM].reshape(A_KV_HEADS, A_GROUP, S, A_HEAD_DIM).sum(axis=1)
    dva = dva.reshape(A_KV_HEADS, A_GROUP, S, A_HEAD_DIM).sum(axis=1)
    dob, dobT = _rows_and_cols(dcat[:, 512:].reshape(S, B_HEADS, B_V))
    dqbT, dkb, dvb = _attn_bwd(qb, qbT, kb, kbT, vb, obT, dob, dobT, lse_b, tile=ATTN_TILE, hb=2, name="mla_bwd")
    half = B_ROPE // 2
    dqb = jnp.transpose(dqbT, (2, 0, 1)) * (B_NOPE + B_ROPE) ** -0.5
    dkb = jnp.transpose(dkb, (1, 0, 2))
    cos, sin = _rope_tables(S)
    cos8, sin8 = jnp.tile(cos, (1, B_HEADS)), jnp.tile(sin, (1, B_HEADS))
    dq1, dq2 = _rope(dqb[:, :, B_NOPE:B_NOPE + half].reshape(S, -1), dqb[:, :, B_NOPE + half:].reshape(S, -1), cos8, -sin8, "ev_rope_q_bwd")
    dq_all = jnp.concatenate([dqb[:, :, :B_NOPE].reshape(S, -1), dq1, dq2], axis=1).astype(BF16)
    dkr = dkb[:, :, B_NOPE:].sum(axis=1)
    dk1, dk2 = _rope(dkr[:, :half], dkr[:, half:], cos, -sin, "ev_rope_k_bwd")
    dkv_all = jnp.concatenate([dkb[:, :, :B_NOPE].reshape(S, -1), _unheads(dvb)], axis=1).astype(BF16)
    G["ev_w_uq"] = _mm(cqn, dq_all, ta=True, name="ev_uq_dw")
    dcqn = _mm(dq_all, W["ev_w_uq"], tb=True, name="ev_uq_dx")
    dc_q, G["ev_cq_norm"] = _rms_bwd(dcqn, proj[:, 768:1024], W["ev_cq_norm"], None, "ev_cq_norm_bwd")
    G["ev_w_ukv"] = _mm(ckvn, dkv_all, ta=True, name="ev_ukv_dw")
    dckvn = _mm(dkv_all, W["ev_w_ukv"], tb=True, name="ev_ukv_dx")
    dc_kv, G["ev_ckv_norm"] = _rms_bwd(dckvn, proj[:, 1024:1152], W["ev_ckv_norm"], None, "ev_ckv_norm_bwd")
    dproj = jnp.concatenate([dqa, _unheads(dka), _unheads(dva), dc_q, dc_kv, dk1, dk2,
                             jnp.zeros((S, EVEN_IN_PAD - EVEN_IN), F32)], axis=1).astype(BF16)
    G["ev_w_in"] = _mm(hn, dproj, ta=True, name="ev_in_dw")
    dhn = _mm(dproj, W["ev_w_in"], tb=True, name="ev_in_dx")
    return dhn, G


def _odd_fwd(hn, h, W):
    S = hn.shape[0]
    w = C_HEADS * C_HEAD_DIM
    proj = _mm(hn, W["od_w_in"], name="od_in")
    f_logit = proj[:, 3 * w: 3 * w + C_HEADS]
    logf = _logsig_fwd(f_logit, W["od_b_f"], "od_logsig")
    logc = _cumsum(logf, False, "od_cumsum")
    parts = [p[:, :, None] for p in _exact3(logc)]
    ones = [jnp.ones((S, C_HEADS, 1), F32)] * 3
    pad = [jnp.zeros((S, C_HEADS, QK_PAD - C_HEAD_DIM - 6), F32)]
    q3 = (proj[:, :w] * C_HEAD_DIM ** -0.5).reshape(S, C_HEADS, C_HEAD_DIM)
    k3 = proj[:, w:2 * w].reshape(S, C_HEADS, C_HEAD_DIM)
    q, qT = _rows_and_cols(jnp.concatenate([q3, jnp.concatenate(parts + ones + pad, axis=-1)], axis=-1))
    k, kT = _rows_and_cols(jnp.concatenate([k3, jnp.concatenate(ones + [-p for p in parts] + pad, axis=-1)], axis=-1))
    v3 = proj[:, 2 * w:3 * w].reshape(S, C_HEADS, C_HEAD_DIM)
    v = jnp.transpose(v3.astype(BF16), (1, 0, 2))
    oT, lse = _attn_fwd(qT, k, _v_with_ones(v3), tile=ATTN_TILE, hb=2, name="fox_fwd")
    cat = _from_T(oT)
    out = _mm_w128(cat, W["G1"], G1_128["od_w_out"], res=h, name="od_out")
    return out, (hn, q, qT, k, kT, v, f_logit, oT, lse, cat)


def _odd_bwd(dout, saved, W, GB):
    hn, q, qT, k, kT, v, f_logit, oT, lse, cat = saved
    S = hn.shape[0]
    G = {}
    dcat = _mm_w128(dout, W["G1"], G1_128["od_w_out"], tb=True, out=BF16, name="od_out_dx")
    GB["g1"] = _mm_w128_dw(cat, dout, G1_128["od_w_out"], GB["g1"], "od_out_dw")
    do, doT = _rows_and_cols(dcat.reshape(S, C_HEADS, C_HEAD_DIM))
    dqT, dk, dv = _attn_bwd(q, qT, k, kT, v, oT, do, doT, lse, tile=ATTN_TILE, hb=2, name="fox_bwd")
    dlogc = jnp.transpose(dqT[:, C_HEAD_DIM, :] - dk[:, :, C_HEAD_DIM + 3])
    dlogf = _cumsum(dlogc, True, "od_cumsum_bwd")
    df, db = _logsig_bwd(dlogf, f_logit, W["od_b_f"], "od_logsig_bwd")
    G["od_b_f"] = db
    dproj = jnp.concatenate([_from_T(dqT[:, :C_HEAD_DIM, :]) * C_HEAD_DIM ** -0.5, _unheads(dk[:, :, :C_HEAD_DIM]), _unheads(dv), df,
                             jnp.zeros((S, ODD_IN_PAD - ODD_IN), F32)], axis=1).astype(BF16)
    G["od_w_in"] = _mm(hn, dproj, ta=True, name="od_in_dw")
    dhn = _mm(dproj, W["od_w_in"], tb=True, name="od_in_dx")
    return dhn, G


def _local_step(x, p, target, W):
    h = x
    saved = []
    for i in range(DEPTH):
        t = f"l{i}"
        h1, s_a = _ffn_fwd(h, W["ffa_norm"][i:i + 1], W, 0, i, f"{t}_ffa")
        nm = _rms_fwd(h1, W["mix_norm"][i:i + 1], f"{t}_mix_norm")
        h2, s_m = (_even_fwd if i % 2 == 0 else _odd_fwd)(nm, h1, W)
        h3, s_b = _ffn_fwd(h2, W["ffb_norm"][i:i + 1], W, 1, i, f"{t}_ffb")
        npl = _rms_fwd(h3, W["ple_norm"][i:i + 1], f"{t}_ple_norm")
        gpre = _mm_w128(npl, W["G1"], G1_128[f"ple_w_gate{i}"], name=f"{t}_ple_gate")
        pp = _mm(p[i], W["ple_w_proj"][i], name=f"{t}_ple_proj")
        h4 = _ple_fwd(h3, gpre, pp, f"{t}_ple")
        saved.append((s_a, h1, s_m, s_b, h3, npl, gpre, pp))
        h = h4
    dh, g_final, loss_cols = _final_fwd_bwd(h, W["final_norm"], target, "final")
    G = {"final_norm": g_final}
    GB = {"g1": lax.empty((N_DEV, G1_ROWS, D_MODEL), F32), "g2": lax.empty((N_DEV, G2_ROWS, FF_BLK), F32)}
    per_layer = {n: [None] * DEPTH for n in ("ffa_norm", "mix_norm", "ffb_norm", "ple_norm", "ple_w_proj")}
    for i in reversed(range(DEPTH)):
        t = f"l{i}"
        s_a, h1, s_m, s_b, h3, npl, gpre, pp = saved[i]
        dgpre, dpp = _ple_bwd(dh, gpre, pp, f"{t}_ple_bwd")
        per_layer["ple_w_proj"][i] = _mm(p[i], dpp, ta=True, name=f"{t}_ple_proj_dw")
        GB["g1"] = _mm_w128_dw(npl, dgpre, G1_128[f"ple_w_gate{i}"], GB["g1"], f"{t}_ple_gate_dw")
        dnpl = _mm_w128(dgpre, W["G1"], G1_128[f"ple_w_gate{i}"], tb=True, name=f"{t}_ple_gate_dx")
        dh, per_layer["ple_norm"][i] = _rms_bwd(dnpl, h3, W["ple_norm"][i:i + 1], dh, f"{t}_ple_norm_bwd")
        dh, per_layer["ffb_norm"][i] = _ffn_bwd(dh, s_b, W["ffb_norm"][i:i + 1], W, GB, 1, i, f"{t}_ffb")
        dnm, g_mix = (_even_bwd if i % 2 == 0 else _odd_bwd)(dh, s_m, W, GB)
        G.update(g_mix)
        dh, per_layer["mix_norm"][i] = _rms_bwd(dnm, h1, W["mix_norm"][i:i + 1], dh, f"{t}_mix_norm_bwd")
        dh, per_layer["ffa_norm"][i] = _ffn_bwd(dh, s_a, W["ffa_norm"][i:i + 1], W, GB, 0, i, f"{t}_ffa")
    for n in ("ffa_norm", "mix_norm", "ffb_norm", "ple_norm"):
        G[n] = jnp.concatenate(per_layer[n], axis=0)
    G["ple_w_proj"] = per_layer["ple_w_proj"]
    return loss_cols, dh, GB, G


def kernel(x, p, ffa_norm, ffa_w_gate_up, ffa_w_down, mix_norm, ffb_norm, ffb_w_gate_up, ffb_w_down, ple_norm, ple_w_gate, ple_w_proj, ev_w_in, ev_sinks, ev_cq_norm, ev_w_uq, ev_ckv_norm, ev_w_ukv, ev_w_out, od_w_in, od_b_f, od_w_out, final_norm, loss_target, m_ffa_norm, m_ffa_w_gate_up, m_ffa_w_down, m_mix_norm, m_ffb_norm, m_ffb_w_gate_up, m_ffb_w_down, m_ple_norm, m_ple_w_gate, m_ple_w_proj, m_ev_w_in, m_ev_sinks, m_ev_cq_norm, m_ev_w_uq, m_ev_ckv_norm, m_ev_w_ukv, m_ev_w_out, m_od_w_in, m_od_b_f, m_od_w_out, m_final_norm, v_ffa_norm, v_ffa_w_gate_up, v_ffa_w_down, v_mix_norm, v_ffb_norm, v_ffb_w_gate_up, v_ffb_w_down, v_ple_norm, v_ple_w_gate, v_ple_w_proj, v_ev_w_in, v_ev_sinks, v_ev_cq_norm, v_ev_w_uq, v_ev_ckv_norm, v_ev_w_ukv, v_ev_w_out, v_od_w_in, v_od_b_f, v_od_w_out, v_final_norm):
    given = dict(locals())
    w_in = {n: given[n] for n in WEIGHTS}

    G1, G2, G3 = _all_gather([_in_slot(g) for g in _local_groups(w_in, BF16)])
    W = {n: w_in[n] for n in SMALL}
    W["final_norm"] = final_norm.reshape(1, -1)
    W.update(_misc_weights(G3))
    W.update(G1=G1, G2=G2, G2v=G2.reshape(2, 4, G2_ROWS, FF_BLK))

    loss_cols, dx, GB, G = _local_step(x[0], p[:, 0], loss_target[0], W)

    grads = _ungroup_local(*[r for r in _reduce_scatter([GB["g1"], GB["g2"], _misc_grads(G)])])
    layout = [(n, int(np.prod(w_in[n].shape))) for n in SMALL]
    vec = jnp.concatenate([G[n].astype(F32).reshape(-1) for n, _ in layout] + [jnp.sum(loss_cols).reshape(1)])
    vec = jnp.pad(vec, (0, N_DEV * SMALL_COLS - vec.shape[0])).reshape(N_DEV, SMALL_COLS)
    vec = _all_reduce_small(vec).reshape(-1)
    off = 0
    for n, size in layout:
        grads[n] = vec[off: off + size].reshape(w_in[n].shape)
        off += size
    loss = vec[off]

    delta, new_m, new_v = {}, {}, {}
    for n in WEIGHTS:
        shp = w_in[n].shape
        as2d = (lambda a: a.reshape(1, -1)) if len(shp) == 1 else (lambda a: a)
        d, nm, nv = _adamw(as2d(w_in[n]), as2d(grads[n]), as2d(given["m_" + n]), as2d(given["v_" + n]), f"adamw_{n}")
        delta[n], new_m[n], new_v[n] = d.reshape(shp), nm.reshape(shp), nv.reshape(shp)
    return (loss, dx[None], *[grads[n] for n in WEIGHTS], *[delta[n] for n in WEIGHTS],
            *[new_m[n] for n in WEIGHTS], *[new_v[n] for n in WEIGHTS])
```

```python
import functools

import numpy as np
import jax
import jax.numpy as jnp
from jax import lax
from jax.experimental import pallas as pl
from jax.experimental.pallas import tpu as pltpu

F32 = jnp.float32
BF16 = jnp.bfloat16
MESH = pl.DeviceIdType.MESH

D_MODEL = 1024
D_FF = 2816
RMS_EPS = 1e-6
PLE_DIM = 256
A_HEADS, A_KV_HEADS, A_HEAD_DIM, WINDOW = 8, 2, 64, 128
A_GROUP = A_HEADS // A_KV_HEADS
B_HEADS, B_Q_LORA, B_KV_LORA, B_NOPE, B_ROPE, B_V = 8, 256, 128, 64, 32, 64
ROPE_THETA = 10000.0
C_HEADS, C_HEAD_DIM = 16, 64
EVEN_IN = 1184
EVEN_IN_PAD = 1280
ODD_IN = 3088
ODD_IN_PAD = 3200
DEPTH = 2
ADAM_LR, ADAM_B1, ADAM_B2, ADAM_EPS, ADAM_WD, ADAM_STEP = 0.001, 0.9, 0.999, 1e-08, 0.01, 10

N_DEV = 8
LANES = 128
SUBLANES = 8
EW_TILE_BYTES = 3 << 20
MM_VMEM_BYTES = 26 << 20
NEG = -1e30
ATTN_TILE = 512
SWA_TILE = 256
QK_PAD = 80

FF_BLK = D_FF // 4
DOWN_ROWS = D_FF // N_DEV
G1_ROWS, G2_ROWS, G3_ROWS, G3_COLS = 1920, 4096, 1024, 768
G1_128 = {"ple_w_gate0": 11, "ple_w_gate1": 12, "ev_w_out": 13, "od_w_out": 14}
OD_C, EV_C, STRIP_C = 386, 148, 128
STRIP0 = OD_C + EV_C

SMALL = ["ffa_norm", "mix_norm", "ffb_norm", "ple_norm", "ev_sinks", "ev_cq_norm", "ev_ckv_norm", "od_b_f", "final_norm"]
WEIGHTS = ["ffa_norm", "ffa_w_gate_up", "ffa_w_down", "mix_norm", "ffb_norm", "ffb_w_gate_up", "ffb_w_down", "ple_norm",
           "ple_w_gate", "ple_w_proj", "ev_w_in", "ev_sinks", "ev_cq_norm", "ev_w_uq", "ev_ckv_norm", "ev_w_ukv", "ev_w_out",
           "od_w_in", "od_b_f", "od_w_out", "final_norm"]
SMALL_COLS = 1280


def _divisor(n, cap, mult):
    if n <= cap:
        return n
    for t in range(cap - cap % mult, 0, -mult):
        if n % t == 0:
            return t
    raise ValueError(f"no tile for {n} under {cap} in steps of {mult}")


def _lanes(c):
    return -(-c // LANES) * LANES


def _ew(fn, rows, vecs, outs, reds=(), *, name):
    R = rows[0].shape[0]
    per_row = sum(_lanes(a.shape[1]) * a.dtype.itemsize for a in rows) + sum(_lanes(c) * jnp.dtype(d).itemsize for c, d in outs)
    tm = _divisor(R, max(16, EW_TILE_BYTES // per_row // 16 * 16), 16) if R % 16 == 0 else R
    n_r, n_v, n_o = len(rows), len(vecs), len(outs)

    def body(*refs):
        ins = [r[...] for r in refs[: n_r + n_v]]
        res = fn(*ins)
        if not isinstance(res, (tuple, list)):
            res = (res,)
        o_refs = refs[n_r + n_v: n_r + n_v + n_o]
        r_refs = refs[n_r + n_v + n_o:]
        for ref, val in zip(o_refs, res[:n_o]):
            ref[...] = val.astype(ref.dtype)
        if r_refs:
            @pl.when(pl.program_id(0) == 0)
            def _():
                for ref in r_refs:
                    ref[...] = jnp.zeros_like(ref)
            for ref, val in zip(r_refs, res[n_o:]):
                ref[...] += val

    in_specs = [pl.BlockSpec((tm, a.shape[1]), lambda i: (i, 0)) for a in rows]
    in_specs += [pl.BlockSpec((1, a.shape[1]), lambda i: (0, 0)) for a in vecs]
    out_specs = [pl.BlockSpec((tm, c), lambda i: (i, 0)) for c, _ in outs]
    out_specs += [pl.BlockSpec((1, c), lambda i: (0, 0)) for c in reds]
    out_shape = [jax.ShapeDtypeStruct((R, c), d) for c, d in outs] + [jax.ShapeDtypeStruct((1, c), F32) for c in reds]
    res = pl.pallas_call(body, name=name, grid=(R // tm,), in_specs=in_specs, out_specs=out_specs, out_shape=out_shape)(*rows, *vecs)
    return res[0] if len(res) == 1 else res


def _rms_fwd(x, w, name):
    def fn(x, w):
        y = x * lax.rsqrt(jnp.mean(x * x, axis=-1, keepdims=True) + RMS_EPS)
        return y * w
    return _ew(fn, [x], [w], [(x.shape[1], BF16)], name=name)


def _rms_bwd(dn, x, w, dres, name):
    def fn(dn, x, *rest):
        w = rest[-1]
        r = lax.rsqrt(jnp.mean(x * x, axis=-1, keepdims=True) + RMS_EPS)
        xh = x * r
        gw = dn * w
        dx = r * (gw - xh * jnp.mean(gw * xh, axis=-1, keepdims=True))
        if len(rest) == 2:
            dx = dx + rest[0]
        return dx, jnp.sum(dn * xh, axis=0, keepdims=True)
    rows = [dn, x] + ([dres] if dres is not None else [])
    return _ew(fn, rows, [w], [(x.shape[1], F32)], [x.shape[1]], name=name)


def _ple_fwd(h, gpre, pp, name):
    return _ew(lambda h, g, q: h + jax.nn.sigmoid(g) * q, [h, gpre, pp], [], [(h.shape[1], F32)], name=name)


def _ple_bwd(dh, gpre, pp, name):
    def fn(dh, g, q):
        sg = jax.nn.sigmoid(g)
        return dh * q * (sg * (1.0 - sg)), dh * sg
    return _ew(fn, [dh, gpre, pp], [], [(dh.shape[1], BF16), (dh.shape[1], BF16)], name=name)


def _rope(x1, x2, cos, sin, name):
    c = x1.shape[1]
    return _ew(lambda a, b, co, si: (a * co - b * si, a * si + b * co), [x1, x2, cos, sin], [], [(c, F32), (c, F32)], name=name)


def _logsig_fwd(f, b, name):
    def fn(f, b):
        z = f + b
        return jnp.minimum(z, 0.0) - jnp.log(1.0 + jnp.exp(-jnp.abs(z)))
    return _ew(fn, [f], [b], [(f.shape[1], F32)], name=name)


def _logsig_bwd(dlogf, f, b, name):
    def fn(d, f, b):
        df = d * jax.nn.sigmoid(-(f + b))
        return df, jnp.sum(df, axis=0, keepdims=True)
    return _ew(fn, [dlogf, f], [b], [(f.shape[1], F32)], [f.shape[1]], name=name)


def _final_fwd_bwd(h, w, target, name):
    d = h.shape[1]

    def fn(h, t, w):
        r = lax.rsqrt(jnp.mean(h * h, axis=-1, keepdims=True) + RMS_EPS)
        xh = h * r
        y = xh * w
        err = y - t
        dy = err * (1.0 / d)
        gw = dy * w
        dx = r * (gw - xh * jnp.mean(gw * xh, axis=-1, keepdims=True))
        return dx, jnp.sum(dy * xh, axis=0, keepdims=True), jnp.sum(err * err, axis=0, keepdims=True) * (0.5 / d)
    return _ew(fn, [h, target], [w], [(d, F32)], [d, d], name=name)


def _adamw(w, g, m, v, name):
    shape = w.shape
    c = shape[-1]
    w2, g2, m2, v2 = (a.reshape(-1, c) for a in (w, g, m, v))

    def fn(w, g, m, v):
        m = ADAM_B1 * m + (1.0 - ADAM_B1) * g
        v = ADAM_B2 * v + (1.0 - ADAM_B2) * jnp.square(g)
        m_hat = m / (1.0 - ADAM_B1 ** ADAM_STEP)
        v_hat = v / (1.0 - ADAM_B2 ** ADAM_STEP)
        delta = -ADAM_LR * (m_hat / (jnp.sqrt(v_hat) + ADAM_EPS) + ADAM_WD * w)
        return delta, m, v
    d, nm, nv = _ew(fn, [w2, g2, m2, v2], [], [(c, F32)] * 3, name=name)
    return d.reshape(shape), nm.reshape(shape), nv.reshape(shape)


def _split3(v):
    hi = v.astype(BF16)
    r1 = v - hi.astype(F32)
    mid = r1.astype(BF16)
    lo = (r1 - mid.astype(F32)).astype(BF16)
    return hi, mid, lo


def _cumsum(x, reverse, name):
    S, C = x.shape
    tm = _divisor(S, 512, 16)
    nt = S // tm

    def body(x_ref, o_ref, carry):
        @pl.when(pl.program_id(0) == 0)
        def _():
            carry[...] = jnp.zeros_like(carry)
        r = lax.broadcasted_iota(jnp.int32, (tm, tm), 0)
        c = lax.broadcasted_iota(jnp.int32, (tm, tm), 1)
        tri = jnp.where((c >= r) if reverse else (c <= r), 1.0, 0.0).astype(BF16)
        xv = x_ref[...]
        acc = jnp.zeros((tm, C), F32)
        for part in _split3(xv):
            acc = acc + jnp.dot(tri, part, preferred_element_type=F32)
        o_ref[...] = acc + carry[...]
        carry[...] += jnp.sum(xv, axis=0, keepdims=True)

    idx = (lambda i: (nt - 1 - i, 0)) if reverse else (lambda i: (i, 0))
    return pl.pallas_call(
        body, name=name, grid=(nt,), in_specs=[pl.BlockSpec((tm, C), idx)], out_specs=pl.BlockSpec((tm, C), idx),
        out_shape=jax.ShapeDtypeStruct((S, C), F32), scratch_shapes=[pltpu.VMEM((1, C), F32)],
    )(x)


NN = (((1,), (0,)), ((), ()))
NT = (((1,), (1,)), ((), ()))
TN = (((0,), (0,)), ((), ()))


def _mm_call(name, grid, k_axis, a, a_spec, a2d, b, b_spec, b2d, dims, out_sds, out_spec, o2d, *,
             alpha=1.0, res=None, res_spec=None, into=None):
    nk = grid[k_axis]

    def body(*refs):
        a_ref, b_ref = refs[0], refs[1]
        res_ref = refs[2] if res is not None else None
        o_ref, acc_ref = refs[-2], refs[-1]
        k = pl.program_id(k_axis)

        @pl.when(k == 0)
        def _():
            acc_ref[...] = jnp.zeros_like(acc_ref)

        av = a_ref[...].reshape(a2d).astype(BF16)
        bv = b_ref[...].reshape(b2d).astype(BF16)
        acc_ref[...] += lax.dot_general(av, bv, dims, preferred_element_type=F32)

        @pl.when(k == nk - 1)
        def _():
            r = acc_ref[...]
            if alpha != 1.0:
                r = r * alpha
            if res_ref is not None:
                r = res_ref[...].reshape(o2d) + r
            o_ref[...] = r.reshape(o_ref.shape).astype(o_ref.dtype)

    in_specs, args = [a_spec, b_spec], [a, b]
    if res is not None:
        in_specs.append(res_spec)
        args.append(res)
    aliases = {}
    if into is not None:
        aliases = {len(args): 0}
        in_specs.append(pl.BlockSpec(memory_space=pl.ANY))
        args.append(into)
        out_sds = jax.ShapeDtypeStruct(into.shape, into.dtype)
    sem = tuple("arbitrary" if d == k_axis else "parallel" for d in range(len(grid)))
    return pl.pallas_call(
        body, name=name, grid=grid, in_specs=in_specs, out_specs=out_spec, out_shape=out_sds,
        scratch_shapes=[pltpu.VMEM(o2d, F32)], input_output_aliases=aliases,
        compiler_params=pltpu.CompilerParams(dimension_semantics=sem),
    )(*args)


def _mm(a, b, *, ta=False, tb=False, out=F32, res=None, alpha=1.0, name):
    K, M = a.shape if ta else a.shape[::-1]
    N = b.shape[0] if tb else b.shape[1]
    assert (b.shape[1] if tb else b.shape[0]) == K, (a.shape, b.shape, ta, tb)
    tk = _divisor(K, 1024, LANES)
    tn = _divisor(N, 1408, LANES)
    for cap in (1024, 512, 256, 128):
        tm = _divisor(M, cap, LANES if ta else 16)
        est = 2 * (tm * tk * a.dtype.itemsize + tk * tn * b.dtype.itemsize + tm * tn * jnp.dtype(out).itemsize)
        est += tm * tn * 4 + (2 * tm * tn * 4 if res is not None else 0)
        if est <= MM_VMEM_BYTES:
            break
    a_spec = pl.BlockSpec((tk, tm), lambda i, j, k: (k, i)) if ta else pl.BlockSpec((tm, tk), lambda i, j, k: (i, k))
    b_spec = pl.BlockSpec((tn, tk), lambda i, j, k: (j, k)) if tb else pl.BlockSpec((tk, tn), lambda i, j, k: (k, j))
    o_spec = pl.BlockSpec((tm, tn), lambda i, j, k: (i, j))
    dims = (((0 if ta else 1,), (1 if tb else 0,)), ((), ()))
    return _mm_call(name, (M // tm, N // tn, K // tk), 2, a, a_spec, (tk, tm) if ta else (tm, tk), b, b_spec,
                    (tn, tk) if tb else (tk, tn), dims, jax.ShapeDtypeStruct((M, N), out), o_spec, (tm, tn),
                    alpha=alpha, res=res, res_spec=o_spec)


def _w128_spec(blk):
    return pl.BlockSpec((N_DEV, 128, D_MODEL), lambda *_: (0, blk, 0))


def _mm_w128(a, G1, blk, *, tb=False, res=None, out=F32, name):
    S = a.shape[0]
    tm = _divisor(S, 512, 16)
    row = pl.BlockSpec((tm, D_MODEL), lambda i, k: (i, 0))
    return _mm_call(name, (S // tm, 1), 1, a, row, (tm, D_MODEL), G1, _w128_spec(blk), (D_MODEL, D_MODEL), NT if tb else NN,
                    jax.ShapeDtypeStruct((S, D_MODEL), out), row, (tm, D_MODEL), res=res, res_spec=row)


def _mm_w128_dw(a, b, blk, into, name):
    S = a.shape[0]
    tk = _divisor(S, 1024, 16)
    row = pl.BlockSpec((tk, D_MODEL), lambda i, k: (k, 0))
    return _mm_call(name, (1, S // tk), 1, a, row, (tk, D_MODEL), b, row, (tk, D_MODEL), TN, None, _w128_spec(blk),
                    (D_MODEL, D_MODEL), into=into)


def _ffn_gate_up(n, G2v, rb, name):
    S = n.shape[0]
    tm = _divisor(S, 512, 16)

    def body(n_ref, w_ref, gu_ref, act_ref):
        nv = n_ref[...]
        g = jnp.dot(nv, w_ref[0, 0], preferred_element_type=F32)
        u = jnp.dot(nv, w_ref[1, 0], preferred_element_type=F32)
        gu_ref[0, 0] = g.astype(BF16)
        gu_ref[1, 0] = u.astype(BF16)
        act_ref[0] = (g * jax.nn.sigmoid(g) * u).astype(BF16)

    return pl.pallas_call(
        body, name=name, grid=(4, S // tm),
        in_specs=[pl.BlockSpec((tm, D_MODEL), lambda j, i: (i, 0)), pl.BlockSpec((2, 1, D_MODEL, FF_BLK), lambda j, i: (0, j, rb, 0))],
        out_specs=[pl.BlockSpec((2, 1, tm, FF_BLK), lambda j, i: (0, j, i, 0)), pl.BlockSpec((1, tm, FF_BLK), lambda j, i: (j, i, 0))],
        out_shape=[jax.ShapeDtypeStruct((2, 4, S, FF_BLK), BF16), jax.ShapeDtypeStruct((4, S, FF_BLK), BF16)],
    )(n, G2v)


def _ffn_down(act, G1, ob, h, name):
    S = h.shape[0]
    tm = _divisor(S, 512, 16)
    row = pl.BlockSpec((tm, D_MODEL), lambda i, k: (i, 0))
    return _mm_call(name, (S // tm, 4), 1, act, pl.BlockSpec((1, tm, FF_BLK), lambda i, k: (k, i, 0)), (tm, FF_BLK),
                    G1, pl.BlockSpec((2, DOWN_ROWS, D_MODEL), lambda i, k: (k, ob, 0)), (FF_BLK, D_MODEL), NN,
                    jax.ShapeDtypeStruct((S, D_MODEL), F32), row, (tm, D_MODEL), alpha=0.5, res=h, res_spec=row)


def _ffn_down_dx(dh, G1, ob, gu, name):
    S = dh.shape[0]
    tm = _divisor(S, 512, 16)

    def body(dh_ref, w_ref, gu_ref, o_ref):
        w = w_ref[...].reshape(FF_BLK, D_MODEL)
        dact = lax.dot_general(dh_ref[...].astype(BF16), w, NT, preferred_element_type=F32) * 0.5
        g = gu_ref[0, 0].astype(F32)
        u = gu_ref[1, 0].astype(F32)
        sg = jax.nn.sigmoid(g)
        o_ref[0, 0] = (dact * u * (sg * (1.0 + g * (1.0 - sg)))).astype(BF16)
        o_ref[1, 0] = (dact * (g * sg)).astype(BF16)

    blk = pl.BlockSpec((2, 1, tm, FF_BLK), lambda j, i: (0, j, i, 0))
    return pl.pallas_call(
        body, name=name, grid=(4, S // tm),
        in_specs=[pl.BlockSpec((tm, D_MODEL), lambda j, i: (i, 0)), pl.BlockSpec((2, DOWN_ROWS, D_MODEL), lambda j, i: (j, ob, 0)), blk],
        out_specs=blk, out_shape=jax.ShapeDtypeStruct((2, 4, S, FF_BLK), BF16),
    )(dh, G1, gu)


def _ffn_down_dw(act, dh, ob, into, name):
    S = dh.shape[0]
    tk = _divisor(S, 1024, 16)
    return _mm_call(name, (4, S // tk), 1, act, pl.BlockSpec((1, tk, FF_BLK), lambda j, k: (j, k, 0)), (tk, FF_BLK),
                    dh, pl.BlockSpec((tk, D_MODEL), lambda j, k: (k, 0)), (tk, D_MODEL), TN, None,
                    pl.BlockSpec((2, DOWN_ROWS, D_MODEL), lambda j, k: (j, ob, 0)), (FF_BLK, D_MODEL), alpha=0.5, into=into)


def _ffn_gate_up_dw(n, dgu8, rb, into, name):
    S = n.shape[0]
    tk = _divisor(S, 1024, 16)
    return _mm_call(name, (N_DEV, S // tk), 1, n, pl.BlockSpec((tk, D_MODEL), lambda b, k: (k, 0)), (tk, D_MODEL),
                    dgu8, pl.BlockSpec((1, tk, FF_BLK), lambda b, k: (b, k, 0)), (tk, FF_BLK), TN, None,
                    pl.BlockSpec((1, D_MODEL, FF_BLK), lambda b, k: (b, rb, 0)), (D_MODEL, FF_BLK), into=into)


def _ffn_gate_up_dx(dgu8, G2, rb, name):
    S = dgu8.shape[1]
    tm = _divisor(S, 512, 16)
    row = pl.BlockSpec((tm, D_MODEL), lambda i, k: (i, 0))
    return _mm_call(name, (S // tm, N_DEV), 1, dgu8, pl.BlockSpec((1, tm, FF_BLK), lambda i, k: (k, i, 0)), (tm, FF_BLK),
                    G2, pl.BlockSpec((1, D_MODEL, FF_BLK), lambda i, k: (k, rb, 0)), (D_MODEL, FF_BLK), NT,
                    jax.ShapeDtypeStruct((S, D_MODEL), F32), row, (tm, D_MODEL))


def _unheads(x):
    h, S, d = x.shape
    return jnp.transpose(x, (1, 0, 2)).reshape(S, h * d)


def _exact3(v):
    rnd = lambda a: lax.reduce_precision(a, exponent_bits=8, mantissa_bits=7)
    hi = rnd(v)
    mid = rnd(v - hi)
    return hi, mid, rnd(v - hi - mid)


def _causal_mask(st, i, j, tq, tk, window):
    dist = (i * tq + lax.broadcasted_iota(jnp.int32, (tk, tq), 1)) - (j * tk + lax.broadcasted_iota(jnp.int32, (tk, tq), 0))
    mask = dist >= 0
    if window is not None:
        mask = mask & (dist < window)
    return jnp.where(mask, st, NEG)


def _attn_fwd(qT, k, vT1, *, tile, hb, window=None, sink=None, name):
    H, dqk, S = qT.shape
    G = H // k.shape[0]
    dvp = vT1.shape[1]
    dv = dvp - 16
    tq = tk = tile
    assert H % hb == 0 and (G == 1 or G % hb == 0)
    kvb = hb if G == 1 else 1

    def body(*refs):
        q_ref, k_ref, v_ref = refs[:3]
        o_ref, lse_ref = refs[-2], refs[-1]
        i = pl.program_id(1)
        carry = []
        for a in range(hb):
            if sink is not None:
                carry.append(jnp.zeros((1, tq), F32) + refs[3][a, :, 0:1])
                carry.append(jnp.where(lax.broadcasted_iota(jnp.int32, (dvp, tq), 0) == dv, 1.0, 0.0))
            else:
                carry.append(jnp.full((1, tq), NEG, F32))
                carry.append(jnp.zeros((dvp, tq), F32))

        def step(j, carry, masked):
            off = pl.multiple_of(j * tk, tk)
            out = []
            for a in range(hb):
                m, acc = carry[2 * a], carry[2 * a + 1]
                kv = a if kvb > 1 else 0
                st = jnp.dot(k_ref[kv, pl.ds(off, tk), :], q_ref[a], preferred_element_type=F32)
                if masked:
                    st = _causal_mask(st, i, j, tq, tk, window)
                m_new = jnp.maximum(m, jnp.max(st, axis=0, keepdims=True))
                pt = jnp.exp(st - m_new).astype(BF16)
                acc = jnp.exp(m - m_new) * acc + jnp.dot(v_ref[kv, :, pl.ds(off, tk)], pt, preferred_element_type=F32)
                out += [m_new, acc]
            return tuple(out)

        carry = tuple(carry)
        if window is None:
            carry = lax.fori_loop(0, i, functools.partial(step, masked=False), carry)
            carry = step(i, carry, True)
        else:
            lo = jnp.maximum((i * tq - (window - 1)) // tk, 0)
            carry = lax.fori_loop(lo, i + 1, functools.partial(step, masked=True), carry)
        for a in range(hb):
            m, acc = carry[2 * a], carry[2 * a + 1]
            l = acc[dv:dv + 1, :]
            o_ref[a] = acc[:dv, :] / l
            lse_ref[a] = m + jnp.log(l)

    kv_idx = (lambda b: b) if G == 1 else (lambda b: (b * hb) // G)
    in_specs = [
        pl.BlockSpec((hb, dqk, tq), lambda b, i: (b, 0, i)),
        pl.BlockSpec((kvb, S, dqk), lambda b, i: (kv_idx(b), 0, 0)),
        pl.BlockSpec((kvb, dvp, S), lambda b, i: (kv_idx(b), 0, 0)),
    ]
    args = [qT, k, vT1]
    if sink is not None:
        in_specs += [pl.BlockSpec((hb, 1, LANES), lambda b, i: (b, 0, 0))]
        args += [sink]
    return pl.pallas_call(
        body, name=name, grid=(H // hb, S // tq), in_specs=in_specs,
        out_specs=[pl.BlockSpec((hb, dv, tq), lambda b, i: (b, 0, i)), pl.BlockSpec((hb, 1, tq), lambda b, i: (b, 0, i))],
        out_shape=[jax.ShapeDtypeStruct((H, dv, S), F32), jax.ShapeDtypeStruct((H, 1, S), F32)],
    )(*args)


def _attn_bwd(q, qT, k, kT, v, oT, do, doT, lse, *, tile, hb, window=None, sink=None, real=None, extra=False, name):
    H, S, dqk = q.shape
    G = H // k.shape[0]
    dv = v.shape[2]
    tq = tk = tile
    nq = S // tq
    has_p = sink is not None
    real = dqk if real is None else real
    assert H % hb == 0 and (G == 1 or G % hb == 0) and not (extra and real == dqk)
    kvb = hb if G == 1 else 1

    def body(*refs):
        q_ref, qT_ref, k_ref, kT_ref, v_ref, oT_ref, do_ref, doT_ref, lse_ref = refs[:9]
        p_ref = refs[9] if has_p else None
        pos = 10 if has_p else 9
        dq_ref, dk_ref, dv_ref = refs[pos: pos + 3]
        pos += 3
        ds_ref = refs[pos] if has_p else None
        pos += has_p
        dqx_ref, dkx_ref = (refs[pos], refs[pos + 1]) if extra else (None, None)
        delta = refs[-1]
        j = pl.program_id(1)

        @pl.when(j == 0)
        def _():
            dq_ref[...] = jnp.zeros_like(dq_ref)
            if extra:
                dqx_ref[...] = jnp.zeros_like(dqx_ref)
            for a in range(hb):
                drow = jnp.sum(doT_ref[a].astype(F32) * oT_ref[a], axis=0, keepdims=True)
                delta[a] = drow
                if has_p:
                    w = jnp.exp(p_ref[a, :, 0:1] - lse_ref[a])
                    ds_ref[a] = jnp.zeros((1, LANES), F32) - jnp.sum(w * drow, axis=1, keepdims=True)

        def step(i, carry, masked):
            off = pl.multiple_of(i * tq, tq)
            out = []
            for a in range(hb):
                dk, dvv = carry[2 * a], carry[2 * a + 1]
                kv = a if kvb > 1 else 0
                st = jnp.dot(k_ref[kv], qT_ref[a, :, pl.ds(off, tq)], preferred_element_type=F32)
                if masked:
                    st = _causal_mask(st, i, j, tq, tk, window)
                pt = jnp.exp(st - lse_ref[a, :, pl.ds(off, tq)])
                dvv = dvv + jnp.dot(pt.astype(BF16), do_ref[a, pl.ds(off, tq), :], preferred_element_type=F32)
                dpt = jnp.dot(v_ref[kv], doT_ref[a, :, pl.ds(off, tq)], preferred_element_type=F32)
                dsb = (pt * (dpt - delta[a, :, pl.ds(off, tq)])).astype(BF16)
                dk = dk + jnp.dot(dsb, q_ref[a, pl.ds(off, tq), :], preferred_element_type=F32)
                dqt = jnp.dot(kT_ref[kv], dsb, preferred_element_type=F32)
                dq_ref[a, :, pl.ds(off, tq)] += dqt[:real]
                if extra:
                    dqx_ref[a, :, pl.ds(off, tq)] += dqt[real:]
                out += [dk, dvv]
            return tuple(out)

        carry = (jnp.zeros((tk, dqk), F32), jnp.zeros((tk, dv), F32)) * hb
        if window is None:
            carry = step(j, carry, True)
            carry = lax.fori_loop(j + 1, nq, functools.partial(step, masked=False), carry)
        else:
            hi = jnp.minimum(nq - 1, ((j + 1) * tk + window - 2) // tq)
            carry = lax.fori_loop(j, hi + 1, functools.partial(step, masked=True), carry)
        for a in range(hb):
            dk_ref[a] = carry[2 * a][:, :real]
            if extra:
                dkx_ref[a] = carry[2 * a][:, real:]
            dv_ref[a] = carry[2 * a + 1]

    kv_idx = (lambda b: b) if G == 1 else (lambda b: (b * hb) // G)
    rows = lambda d: pl.BlockSpec((hb, S, d), lambda b, j: (b, 0, 0))
    colsT = lambda d: pl.BlockSpec((hb, d, S), lambda b, j: (b, 0, 0))
    in_specs = [
        rows(dqk), colsT(dqk),
        pl.BlockSpec((kvb, tk, dqk), lambda b, j: (kv_idx(b), j, 0)),
        pl.BlockSpec((kvb, dqk, tk), lambda b, j: (kv_idx(b), 0, j)),
        pl.BlockSpec((kvb, tk, dv), lambda b, j: (kv_idx(b), j, 0)),
        colsT(dv), rows(dv), colsT(dv),
        pl.BlockSpec((hb, 1, S), lambda b, j: (b, 0, 0)),
    ]
    args = [q, qT, k, kT, v, oT, do, doT, lse]
    if has_p:
        in_specs += [pl.BlockSpec((hb, 1, LANES), lambda b, j: (b, 0, 0))]
        args += [sink]
    out_specs = [colsT(real), pl.BlockSpec((hb, tk, real), lambda b, j: (b, j, 0)), pl.BlockSpec((hb, tk, dv), lambda b, j: (b, j, 0))]
    out_shape = [jax.ShapeDtypeStruct((H, real, S), F32), jax.ShapeDtypeStruct((H, S, real), F32), jax.ShapeDtypeStruct((H, S, dv), F32)]
    if has_p:
        out_specs += [pl.BlockSpec((hb, 1, LANES), lambda b, j: (b, 0, 0))]
        out_shape += [jax.ShapeDtypeStruct((H, 1, LANES), F32)]
    if extra:
        out_specs += [colsT(dqk - real), pl.BlockSpec((hb, tk, dqk - real), lambda b, j: (b, j, 0))]
        out_shape += [jax.ShapeDtypeStruct((H, dqk - real, S), F32), jax.ShapeDtypeStruct((H, S, dqk - real), F32)]
    return pl.pallas_call(
        body, name=name, grid=(H // hb, S // tk), in_specs=in_specs, out_specs=out_specs, out_shape=out_shape,
        scratch_shapes=[pltpu.VMEM((hb, 1, S), F32)],
        compiler_params=pltpu.CompilerParams(dimension_semantics=("parallel", "arbitrary")),
    )(*args)


def _rows_and_cols(x3):
    xb = x3.astype(BF16)
    return jnp.transpose(xb, (1, 0, 2)), jnp.transpose(xb, (1, 2, 0))


def _v_with_ones(v3):
    S, h, _ = v3.shape
    vT = jnp.transpose(v3.astype(BF16), (1, 2, 0))
    return jnp.concatenate([vT, jnp.ones((h, 1, S), BF16), jnp.zeros((h, 15, S), BF16)], axis=1)


def _from_T(oT):
    h, d, S = oT.shape
    return jnp.transpose(oT, (2, 0, 1)).reshape(S, h * d)


def _coords():
    return lax.axis_index("x"), lax.axis_index("y"), lax.axis_index("c")


def _peer(axis):
    x, y, c = _coords()
    return {"x": (1 - x, y, c), "y": (x, 1 - y, c), "c": (x, y, 1 - c)}[axis]


HBM_SPEC = pl.BlockSpec(memory_space=pl.ANY)


def _all_gather(bufs):
    n = len(bufs)

    def body(*refs):
        outs = refs[n: 2 * n]
        send_sems, recv_sems = refs[2 * n], refs[2 * n + 1]
        x, y, c = _coords()
        me, sibling = (x, y, c), (x, y, 1 - c)
        chips = [(1 - x, y), (x, 1 - y), (1 - x, 1 - y)]

        def copy(t, k, block, to):
            px, py, pc = block
            ref = outs[t].at[4 * px + 2 * py + pc]
            return pltpu.make_async_remote_copy(ref, ref, send_sems.at[7 * t + k], recv_sems.at[7 * t + k], device_id=to, device_id_type=MESH)

        first = []
        for t in range(n):
            first.append(copy(t, 0, me, sibling))
            first += [copy(t, 1 + j, me, (*chip, c)) for j, chip in enumerate(chips)]
        for cp in first:
            cp.start()
        passed = []
        for j, chip in enumerate(chips):
            for t in range(n):
                copy(t, 1 + j, (*chip, c), me).wait_recv()
                cp = copy(t, 4 + j, (*chip, c), sibling)
                cp.start()
                passed.append(cp)
        for t in range(n):
            copy(t, 0, sibling, me).wait_recv()
            for j, chip in enumerate(chips):
                copy(t, 4 + j, (*chip, 1 - c), me).wait_recv()
        for cp in first + passed:
            cp.wait_send()

    return pl.pallas_call(
        body, name="all_gather", in_specs=[HBM_SPEC] * n, out_specs=[HBM_SPEC] * n,
        out_shape=[jax.ShapeDtypeStruct(b.shape, b.dtype) for b in bufs], input_output_aliases={t: t for t in range(n)},
        scratch_shapes=[pltpu.SemaphoreType.DMA((7 * n,)), pltpu.SemaphoreType.DMA((7 * n,))],
    )(*bufs)


def _in_slot(local):
    x, y, c = _coords()
    buf = jnp.zeros((N_DEV,) + local.shape, local.dtype)
    return lax.dynamic_update_slice(buf, local[None], (4 * x + 2 * y + c, 0, 0))


def _scatter_pair(vs, axes, name):
    n = len(vs)
    axes = [axes] * n if isinstance(axes, str) else axes

    def body(*refs):
        send_sems, recv_sems = refs[2 * n], refs[2 * n + 1]
        copies = []
        for t in range(n):
            v_ref, o_ref = refs[t], refs[n + t]
            me = lax.axis_index(axes[t])
            src = v_ref.at[1 - me] if len(v_ref.shape) == 3 else v_ref.at[:, 1 - me]
            cp = pltpu.make_async_remote_copy(src, o_ref, send_sems.at[t], recv_sems.at[t], device_id=_peer(axes[t]), device_id_type=MESH)
            cp.start()
            copies.append(cp)
        for cp in copies:
            cp.wait()

    out_shape = [jax.ShapeDtypeStruct(v.shape[:-3] + v.shape[-2:], v.dtype) for v in vs]
    return pl.pallas_call(
        body, name=name, in_specs=[HBM_SPEC] * n, out_specs=[HBM_SPEC] * n, out_shape=out_shape,
        scratch_shapes=[pltpu.SemaphoreType.DMA((n,)), pltpu.SemaphoreType.DMA((n,))],
    )(*vs)


def _add_kept(v, got, axis, out, name):
    R, C = v.shape[-2:]
    lead = v.shape[0] if v.ndim == 4 else 1
    tm = _divisor(R, max(16, EW_TILE_BYTES // (_lanes(C) * (v.dtype.itemsize + got.dtype.itemsize + jnp.dtype(out).itemsize)) // 16 * 16), 16)
    me = lax.axis_index(axis).astype(jnp.int32).reshape(1)
    v4 = v.reshape(lead, 2, R, C)
    g3 = got.reshape(lead, R, C)

    def body(me_ref, v_ref, g_ref, o_ref):
        o_ref[...] = (v_ref[0].astype(F32) + g_ref[...].astype(F32)).astype(o_ref.dtype)

    res = pl.pallas_call(
        body, name=name, out_shape=jax.ShapeDtypeStruct((lead, R, C), out),
        grid_spec=pltpu.PrefetchScalarGridSpec(
            num_scalar_prefetch=1, grid=(lead, R // tm),
            in_specs=[pl.BlockSpec((1, 1, tm, C), lambda b, i, me: (b, me[0], i, 0)), pl.BlockSpec((1, tm, C), lambda b, i, me: (b, i, 0))],
            out_specs=pl.BlockSpec((1, tm, C), lambda b, i, me: (b, i, 0))),
    )(me, v4, g3)
    return res


def _scatter_cross(vs, name):
    n = len(vs)

    def body(*refs):
        send_sems, recv_sems = refs[3 * n], refs[3 * n + 1]
        x, y, _ = _coords()
        copies = []
        for t in range(n):
            v_ref, a_ref, b_ref = refs[t], refs[n + 2 * t], refs[n + 2 * t + 1]
            h = v_ref.shape[2] // 2
            copies.append(pltpu.make_async_remote_copy(v_ref.at[1 - x, :, pl.ds(0, h)], a_ref, send_sems.at[2 * t], recv_sems.at[2 * t],
                                                       device_id=_peer("x"), device_id_type=MESH))
            copies.append(pltpu.make_async_remote_copy(v_ref.at[:, 1 - y, pl.ds(h, h)], b_ref, send_sems.at[2 * t + 1], recv_sems.at[2 * t + 1],
                                                       device_id=_peer("y"), device_id_type=MESH))
        for cp in copies:
            cp.start()
        for cp in copies:
            cp.wait()

    out_shape = []
    for v in vs:
        half = jax.ShapeDtypeStruct((2, v.shape[2] // 2, v.shape[3]), v.dtype)
        out_shape += [half, half]
    res = pl.pallas_call(
        body, name=name, in_specs=[HBM_SPEC] * n, out_specs=[HBM_SPEC] * 2 * n, out_shape=out_shape,
        scratch_shapes=[pltpu.SemaphoreType.DMA((2 * n,)), pltpu.SemaphoreType.DMA((2 * n,))],
    )(*vs)
    return res[0::2], res[1::2]


def _add_picked(v, got, axis, out, name):
    _, _, R, C = v.shape
    h = R // 2
    tm = _divisor(h, max(16, EW_TILE_BYTES // (_lanes(C) * (v.dtype.itemsize + got.dtype.itemsize + jnp.dtype(out).itemsize)) // 16 * 16), 16)
    me = lax.axis_index(axis).astype(jnp.int32).reshape(1)
    if axis == "x":
        v_map = lambda b, i, me: (me[0], b, i, 0)
    else:
        v_map = lambda b, i, me: (b, me[0], i + h // tm, 0)

    def body(me_ref, v_ref, g_ref, o_ref):
        o_ref[...] = (v_ref[0].astype(F32) + g_ref[...].astype(F32)).astype(o_ref.dtype)

    return pl.pallas_call(
        body, name=name, out_shape=jax.ShapeDtypeStruct((2, h, C), out),
        grid_spec=pltpu.PrefetchScalarGridSpec(
            num_scalar_prefetch=1, grid=(2, h // tm),
            in_specs=[pl.BlockSpec((1, 1, tm, C), v_map), pl.BlockSpec((1, tm, C), lambda b, i, me: (b, i, 0))],
            out_specs=pl.BlockSpec((1, tm, C), lambda b, i, me: (b, i, 0))),
    )(me, v, got)


def _reduce_scatter(gs):
    vs = [g.reshape(4, 2, *g.shape[1:]) for g in gs]
    got = _scatter_pair(vs, "c", "reduce_scatter_c")
    vs = [_add_kept(v, r, "c", BF16, f"reduce_scatter_add_c{t}") for t, (v, r) in enumerate(zip(vs, got))]
    vs = [v.reshape(2, 2, v.shape[1], v.shape[2]) for v in vs]
    got_a, got_b = _scatter_cross(vs, "reduce_scatter_xy")
    up = [_add_picked(v, r, "x", BF16, f"reduce_scatter_add_x{t}") for t, (v, r) in enumerate(zip(vs, got_a))]
    lo = [_add_picked(v, r, "y", BF16, f"reduce_scatter_add_y{t}") for t, (v, r) in enumerate(zip(vs, got_b))]
    n = len(gs)
    got = _scatter_pair(up + lo, ["y"] * n + ["x"] * n, "reduce_scatter_yx")
    out = []
    for t in range(n):
        a = _add_kept(up[t], got[t], "y", F32, f"reduce_scatter_add_y2{t}")[0]
        b = _add_kept(lo[t], got[n + t], "x", F32, f"reduce_scatter_add_x2{t}")[0]
        out.append(jnp.concatenate([a, b], axis=0))
    return out


def _all_reduce_small(v):
    def body(v_ref, o_ref, buf, send_sems, recv_sems):
        x, y, c = _coords()
        me = 4 * x + 2 * y + c
        buf[me] = v_ref[...]
        copies = []
        for k in range(1, N_DEV):
            peer = tuple((1 - a) if (k >> s) & 1 else a for a, s in ((x, 2), (y, 1), (c, 0)))
            cp = pltpu.make_async_remote_copy(v_ref, buf.at[me], send_sems.at[k - 1], recv_sems.at[k - 1], device_id=peer, device_id_type=MESH)
            cp.start()
            copies.append(cp)
        for cp in copies:
            cp.wait()
        acc = buf[0]
        for d in range(1, N_DEV):
            acc = acc + buf[d]
        o_ref[...] = acc

    vm = pl.BlockSpec(memory_space=pltpu.VMEM)
    return pl.pallas_call(
        body, name="all_reduce_small", in_specs=[vm], out_specs=vm, out_shape=jax.ShapeDtypeStruct(v.shape, F32),
        scratch_shapes=[pltpu.VMEM((N_DEV,) + v.shape, F32), pltpu.SemaphoreType.DMA((N_DEV - 1,)), pltpu.SemaphoreType.DMA((N_DEV - 1,))],
    )(v)


def _local_groups(w, dtype):
    g1 = jnp.concatenate([w["ffa_w_down"].reshape(-1, D_MODEL), w["ffb_w_down"].reshape(-1, D_MODEL),
                          w["ple_w_gate"].reshape(-1, D_MODEL), w["ev_w_out"][0], w["od_w_out"][0]], axis=0).astype(dtype)
    g2 = jnp.concatenate([w["ffa_w_gate_up"].reshape(-1, FF_BLK), w["ffb_w_gate_up"].reshape(-1, FF_BLK)], axis=0).astype(dtype)
    strip = jnp.concatenate([w["ple_w_proj"].reshape(-1, STRIP_C), w["ev_w_ukv"][0], jnp.pad(w["ev_w_uq"][0], ((0, 0), (0, STRIP_C - 96))),
                             jnp.zeros((G3_ROWS - 896, STRIP_C), F32)], axis=0)
    g3 = jnp.concatenate([w["od_w_in"][0], w["ev_w_in"][0], strip, jnp.zeros((G3_ROWS, G3_COLS - STRIP0 - STRIP_C), F32)], axis=1).astype(dtype)
    return g1, g2, g3


def _ungroup_local(r1, r2, r3):
    out = {
        "ffa_w_down": r1[:704].reshape(2, DOWN_ROWS, D_MODEL), "ffb_w_down": r1[704:1408].reshape(2, DOWN_ROWS, D_MODEL),
        "ple_w_gate": r1[1408:1664].reshape(2, 128, D_MODEL), "ev_w_out": r1[1664:1792][None], "od_w_out": r1[1792:1920][None],
        "ffa_w_gate_up": r2[:2048].reshape(2, D_MODEL, FF_BLK), "ffb_w_gate_up": r2[2048:].reshape(2, D_MODEL, FF_BLK),
        "od_w_in": r3[:, :OD_C][None], "ev_w_in": r3[:, OD_C:STRIP0][None],
    }
    strip = r3[:, STRIP0:STRIP0 + STRIP_C]
    out["ple_w_proj"] = strip[:512].reshape(2, PLE_DIM, STRIP_C)
    out["ev_w_ukv"] = strip[512:640][None]
    out["ev_w_uq"] = strip[640:896, :96][None]
    return out


def _cols(a):
    return jnp.transpose(a, (1, 0, 2)).reshape(a.shape[1], -1)


def _blocks(g, c):
    return jnp.transpose(g.reshape(g.shape[0], N_DEV, c), (1, 0, 2))


def _uq_permute(w):
    r = w.shape[0]
    w3 = w.reshape(r, B_HEADS, B_NOPE + B_ROPE)
    half = B_ROPE // 2
    return jnp.concatenate([w3[:, :, :B_NOPE].reshape(r, -1), w3[:, :, B_NOPE:B_NOPE + half].reshape(r, -1), w3[:, :, B_NOPE + half:].reshape(r, -1)], axis=1)


def _uq_unpermute(g):
    r = g.shape[0]
    half = B_ROPE // 2
    n = B_HEADS * B_NOPE
    parts = [g[:, :n].reshape(r, B_HEADS, B_NOPE), g[:, n:n + B_HEADS * half].reshape(r, B_HEADS, half), g[:, n + B_HEADS * half:].reshape(r, B_HEADS, half)]
    return jnp.concatenate(parts, axis=2).reshape(r, -1)


def _ukv_permute(w):
    r = w.shape[0]
    return jnp.transpose(w.reshape(r, B_HEADS, 2, B_NOPE), (0, 2, 1, 3)).reshape(r, -1)


def _ukv_unpermute(g):
    r = g.shape[0]
    return jnp.transpose(g.reshape(r, 2, B_HEADS, B_NOPE), (0, 2, 1, 3)).reshape(r, -1)


def _misc_weights(G3):
    strip = G3[:, :, STRIP0:STRIP0 + STRIP_C]
    return {
        "od_w_in": jnp.pad(_cols(G3[:, :, :OD_C]), ((0, 0), (0, ODD_IN_PAD - ODD_IN))),
        "ev_w_in": jnp.pad(_cols(G3[:, :, OD_C:STRIP0]), ((0, 0), (0, EVEN_IN_PAD - EVEN_IN))),
        "ple_w_proj": [_cols(strip[:, i * PLE_DIM:(i + 1) * PLE_DIM]) for i in range(DEPTH)],
        "ev_w_ukv": _ukv_permute(_cols(strip[:, 512:640])),
        "ev_w_uq": _uq_permute(_cols(strip[:, 640:896, :96])),
    }


def _misc_grads(G):
    strip = jnp.concatenate([
        _blocks(G["ple_w_proj"][0], STRIP_C), _blocks(G["ple_w_proj"][1], STRIP_C), _blocks(_ukv_unpermute(G["ev_w_ukv"]), STRIP_C),
        jnp.pad(_blocks(_uq_unpermute(G["ev_w_uq"]), 96), ((0, 0), (0, 0), (0, STRIP_C - 96))),
        jnp.zeros((N_DEV, G3_ROWS - 896, STRIP_C), F32)], axis=1)
    return jnp.concatenate([_blocks(G["od_w_in"][:, :ODD_IN], OD_C), _blocks(G["ev_w_in"][:, :EVEN_IN], EV_C), strip,
                            jnp.zeros((N_DEV, G3_ROWS, G3_COLS - STRIP0 - STRIP_C), F32)], axis=2)


def _ffn_fwd(h, norm_w, W, f, i, tag):
    n = _rms_fwd(h, norm_w, f"{tag}_norm")
    gu, act = _ffn_gate_up(n, W["G2v"], 2 * f + i, f"{tag}_gate_up")
    out = _ffn_down(act, W["G1"], 2 * f + i, h, f"{tag}_down")
    return out, (h, n, gu, act)


def _ffn_bwd(dout, saved, norm_w, W, GB, f, i, tag):
    h, n, gu, act = saved
    S = h.shape[0]
    blk = 2 * f + i
    GB["g1"] = _ffn_down_dw(act, dout, blk, GB["g1"], f"{tag}_down_dw")
    dgu = _ffn_down_dx(dout, W["G1"], blk, gu, f"{tag}_down_dx").reshape(N_DEV, S, FF_BLK)
    GB["g2"] = _ffn_gate_up_dw(n, dgu, blk, GB["g2"], f"{tag}_gate_up_dw")
    dn = _ffn_gate_up_dx(dgu, W["G2"], blk, f"{tag}_gate_up_dx")
    return _rms_bwd(dn, h, norm_w, dout, f"{tag}_norm_bwd")


def _rope_tables(S):
    inv = ROPE_THETA ** (-jnp.arange(0, B_ROPE, 2, dtype=F32) / B_ROPE)
    ang = jnp.arange(S, dtype=F32)[:, None] * inv[None, :]
    return jnp.cos(ang), jnp.sin(ang)


def _alibi_columns(S):
    t = jnp.arange(S, dtype=jnp.int32)
    hi = ((t // 16) * 16).astype(F32)
    lo = (t % 16).astype(F32)
    slopes = 2.0 ** (-8.0 * jnp.arange(1, A_HEADS + 1, dtype=F32) / A_HEADS)
    zq = jnp.zeros((S, A_HEADS), F32)
    rest = QK_PAD - A_HEAD_DIM - 4
    qc = jnp.stack([-slopes[None, :] * hi[:, None], -slopes[None, :] * lo[:, None], zq + slopes[None, :], zq + slopes[None, :]] + [zq] * rest, axis=-1)
    one = jnp.ones((S, A_KV_HEADS), F32)
    zk = jnp.zeros((S, A_KV_HEADS), F32)
    kc = jnp.stack([one, one, zk + hi[:, None], zk + lo[:, None]] + [zk] * rest, axis=-1)
    return qc, kc


def _sink_prm(sinks):
    return jnp.zeros((A_HEADS, 1, LANES), F32).at[:, 0, 0].set(sinks.astype(F32))


def _even_fwd(hn, h, W):
    S = hn.shape[0]
    proj = _mm(hn, W["ev_w_in"], name="ev_in")
    a_q, a_k, a_v = proj[:, :512], proj[:, 512:640], proj[:, 640:768]
    c_q, c_kv = proj[:, 768:1024], proj[:, 1024:1152]
    kr1, kr2 = proj[:, 1152:1168], proj[:, 1168:1184]
    qc, kc = _alibi_columns(S)
    qa, qaT = _rows_and_cols(jnp.concatenate([(a_q * A_HEAD_DIM ** -0.5).reshape(S, A_HEADS, A_HEAD_DIM), qc], axis=-1))
    ka, kaT = _rows_and_cols(jnp.concatenate([a_k.reshape(S, A_KV_HEADS, A_HEAD_DIM), kc], axis=-1))
    va3 = a_v.reshape(S, A_KV_HEADS, A_HEAD_DIM)
    va = jnp.transpose(va3.astype(BF16), (1, 0, 2))
    prm = _sink_prm(W["ev_sinks"][0])
    oaT, lse_a = _attn_fwd(qaT, ka, _v_with_ones(va3), tile=SWA_TILE, hb=2, window=WINDOW, sink=prm, name="swa_fwd")
    cqn = _rms_fwd(c_q, W["ev_cq_norm"], "ev_cq_norm")
    q_all = _mm(cqn, W["ev_w_uq"], name="ev_uq")
    ckvn = _rms_fwd(c_kv, W["ev_ckv_norm"], "ev_ckv_norm")
    kv_all = _mm(ckvn, W["ev_w_ukv"], name="ev_ukv")
    cos, sin = _rope_tables(S)
    cos8, sin8 = jnp.tile(cos, (1, B_HEADS)), jnp.tile(sin, (1, B_HEADS))
    q1, q2 = _rope(q_all[:, 512:640], q_all[:, 640:768], cos8, sin8, "ev_rope_q")
    k1, k2 = _rope(kr1, kr2, cos, sin, "ev_rope_k")
    half = B_ROPE // 2
    scale = (B_NOPE + B_ROPE) ** -0.5
    qb, qbT = _rows_and_cols(jnp.concatenate([q_all[:, :512].reshape(S, B_HEADS, B_NOPE), q1.reshape(S, B_HEADS, half), q2.reshape(S, B_HEADS, half)], axis=-1) * scale)
    kro = jnp.broadcast_to(jnp.concatenate([k1, k2], axis=1)[:, None, :], (S, B_HEADS, B_ROPE))
    kb, kbT = _rows_and_cols(jnp.concatenate([kv_all[:, :512].reshape(S, B_HEADS, B_NOPE), kro], axis=-1))
    vb3 = kv_all[:, 512:].reshape(S, B_HEADS, B_V)
    vb = jnp.transpose(vb3.astype(BF16), (1, 0, 2))
    obT, lse_b = _attn_fwd(qbT, kb, _v_with_ones(vb3), tile=ATTN_TILE, hb=2, name="mla_fwd")
    cat = jnp.concatenate([_from_T(oaT), _from_T(obT)], axis=1)
    out = _mm_w128(cat, W["G1"], G1_128["ev_w_out"], res=h, name="ev_out")
    return out, (hn, proj, (qa, qaT, ka, kaT, va, oaT, lse_a), prm, cqn, ckvn, (qb, qbT, kb, kbT, vb, obT, lse_b), cat)


def _even_bwd(dout, saved, W, GB):
    hn, proj, (qa, qaT, ka, kaT, va, oaT, lse_a), prm, cqn, ckvn, (qb, qbT, kb, kbT, vb, obT, lse_b), cat = saved
    S = hn.shape[0]
    G = {}
    dcat = _mm_w128(dout, W["G1"], G1_128["ev_w_out"], tb=True, out=BF16, name="ev_out_dx")
    GB["g1"] = _mm_w128_dw(cat, dout, G1_128["ev_w_out"], GB["g1"], "ev_out_dw")
    doa, doaT = _rows_and_cols(dcat[:, :512].reshape(S, A_HEADS, A_HEAD_DIM))
    dqaT, dka, dva, dsink = _attn_bwd(qa, qaT, ka, kaT, va, oaT, doa, doaT, lse_a, tile=SWA_TILE, hb=2, window=WINDOW, sink=prm, real=A_HEAD_DIM,
                                       name="swa_bwd")
    G["ev_sinks"] = dsink[:, 0, 0]
    dqa = _from_T(dqaT) * A_HEAD_DIM ** -0.5
    dka = dka.reshape(A_KV_HEADS, A_GROUP, S, A_HEAD_DIM).sum(axis=1)
    dva = dva.reshape(A_KV_HEADS, A_GROUP, S, A_HEAD_DIM).sum(axis=1)
    dob, dobT = _rows_and_cols(dcat[:, 512:].reshape(S, B_HEADS, B_V))
    dqbT, dkb, dvb = _attn_bwd(qb, qbT, kb, kbT, vb, obT, dob, dobT, lse_b, tile=ATTN_TILE, hb=2, name="mla_bwd")
    half = B_ROPE // 2
    dqb = jnp.transpose(dqbT, (2, 0, 1)) * (B_NOPE + B_ROPE) ** -0.5
    dkb = jnp.transpose(dkb, (1, 0, 2))
    cos, sin = _rope_tables(S)
    cos8, sin8 = jnp.tile(cos, (1, B_HEADS)), jnp.tile(sin, (1, B_HEADS))
    dq1, dq2 = _rope(dqb[:, :, B_NOPE:B_NOPE + half].reshape(S, -1), dqb[:, :, B_NOPE + half:].reshape(S, -1), cos8, -sin8, "ev_rope_q_bwd")
    dq_all = jnp.concatenate([dqb[:, :, :B_NOPE].reshape(S, -1), dq1, dq2], axis=1).astype(BF16)
    dkr = dkb[:, :, B_NOPE:].sum(axis=1)
    dk1, dk2 = _rope(dkr[:, :half], dkr[:, half:], cos, -sin, "ev_rope_k_bwd")
    dkv_all = jnp.concatenate([dkb[:, :, :B_NOPE].reshape(S, -1), _unheads(dvb)], axis=1).astype(BF16)
    G["ev_w_uq"] = _mm(cqn, dq_all, ta=True, name="ev_uq_dw")
    dcqn = _mm(dq_all, W["ev_w_uq"], tb=True, name="ev_uq_dx")
    dc_q, G["ev_cq_norm"] = _rms_bwd(dcqn, proj[:, 768:1024], W["ev_cq_norm"], None, "ev_cq_norm_bwd")
    G["ev_w_ukv"] = _mm(ckvn, dkv_all, ta=True, name="ev_ukv_dw")
    dckvn = _mm(dkv_all, W["ev_w_ukv"], tb=True, name="ev_ukv_dx")
    dc_kv, G["ev_ckv_norm"] = _rms_bwd(dckvn, proj[:, 1024:1152], W["ev_ckv_norm"], None, "ev_ckv_norm_bwd")
    dproj = jnp.concatenate([dqa, _unheads(dka), _unheads(dva), dc_q, dc_kv, dk1, dk2,
                             jnp.zeros((S, EVEN_IN_PAD - EVEN_IN), F32)], axis=1).astype(BF16)
    G["ev_w_in"] = _mm(hn, dproj, ta=True, name="ev_in_dw")
    dhn = _mm(dproj, W["ev_w_in"], tb=True, name="ev_in_dx")
    return dhn, G


def _odd_fwd(hn, h, W):
    S = hn.shape[0]
    w = C_HEADS * C_HEAD_DIM
    proj = _mm(hn, W["od_w_in"], name="od_in")
    f_logit = proj[:, 3 * w: 3 * w + C_HEADS]
    logf = _logsig_fwd(f_logit, W["od_b_f"], "od_logsig")
    logc = _cumsum(logf, False, "od_cumsum")
    parts = [p[:, :, None] for p in _exact3(logc)]
    ones = [jnp.ones((S, C_HEADS, 1), F32)] * 3
    pad = [jnp.zeros((S, C_HEADS, QK_PAD - C_HEAD_DIM - 6), F32)]
    q3 = (proj[:, :w] * C_HEAD_DIM ** -0.5).reshape(S, C_HEADS, C_HEAD_DIM)
    k3 = proj[:, w:2 * w].reshape(S, C_HEADS, C_HEAD_DIM)
    q, qT = _rows_and_cols(jnp.concatenate([q3, jnp.concatenate(parts + ones + pad, axis=-1)], axis=-1))
    k, kT = _rows_and_cols(jnp.concatenate([k3, jnp.concatenate(ones + [-p for p in parts] + pad, axis=-1)], axis=-1))
    v3 = proj[:, 2 * w:3 * w].reshape(S, C_HEADS, C_HEAD_DIM)
    v = jnp.transpose(v3.astype(BF16), (1, 0, 2))
    oT, lse = _attn_fwd(qT, k, _v_with_ones(v3), tile=ATTN_TILE, hb=2, name="fox_fwd")
    cat = _from_T(oT)
    out = _mm_w128(cat, W["G1"], G1_128["od_w_out"], res=h, name="od_out")
    return out, (hn, q, qT, k, kT, v, f_logit, oT, lse, cat)


def _odd_bwd(dout, saved, W, GB):
    hn, q, qT, k, kT, v, f_logit, oT, lse, cat = saved
    S = hn.shape[0]
    G = {}
    dcat = _mm_w128(dout, W["G1"], G1_128["od_w_out"], tb=True, out=BF16, name="od_out_dx")
    GB["g1"] = _mm_w128_dw(cat, dout, G1_128["od_w_out"], GB["g1"], "od_out_dw")
    do, doT = _rows_and_cols(dcat.reshape(S, C_HEADS, C_HEAD_DIM))
    dqT, dk, dv, dqxT, dkx = _attn_bwd(q, qT, k, kT, v, oT, do, doT, lse, tile=ATTN_TILE, hb=2, real=C_HEAD_DIM, extra=True, name="fox_bwd")
    dlogc = jnp.transpose(dqxT[:, 0, :] - dkx[:, :, 3])
    dlogf = _cumsum(dlogc, True, "od_cumsum_bwd")
    df, db = _logsig_bwd(dlogf, f_logit, W["od_b_f"], "od_logsig_bwd")
    G["od_b_f"] = db
    dproj = jnp.concatenate([_from_T(dqT) * C_HEAD_DIM ** -0.5, _unheads(dk), _unheads(dv), df,
                             jnp.zeros((S, ODD_IN_PAD - ODD_IN), F32)], axis=1).astype(BF16)
    G["od_w_in"] = _mm(hn, dproj, ta=True, name="od_in_dw")
    dhn = _mm(dproj, W["od_w_in"], tb=True, name="od_in_dx")
    return dhn, G


def _local_step(x, p, target, W):
    h = x
    saved = []
    for i in range(DEPTH):
        t = f"l{i}"
        h1, s_a = _ffn_fwd(h, W["ffa_norm"][i:i + 1], W, 0, i, f"{t}_ffa")
        nm = _rms_fwd(h1, W["mix_norm"][i:i + 1], f"{t}_mix_norm")
        h2, s_m = (_even_fwd if i % 2 == 0 else _odd_fwd)(nm, h1, W)
        h3, s_b = _ffn_fwd(h2, W["ffb_norm"][i:i + 1], W, 1, i, f"{t}_ffb")
        npl = _rms_fwd(h3, W["ple_norm"][i:i + 1], f"{t}_ple_norm")
        gpre = _mm_w128(npl, W["G1"], G1_128[f"ple_w_gate{i}"], name=f"{t}_ple_gate")
        pp = _mm(p[i], W["ple_w_proj"][i], name=f"{t}_ple_proj")
        h4 = _ple_fwd(h3, gpre, pp, f"{t}_ple")
        saved.append((s_a, h1, s_m, s_b, h3, npl, gpre, pp))
        h = h4
    dh, g_final, loss_cols = _final_fwd_bwd(h, W["final_norm"], target, "final")
    G = {"final_norm": g_final}
    GB = {"g1": lax.empty((N_DEV, G1_ROWS, D_MODEL), BF16), "g2": lax.empty((N_DEV, G2_ROWS, FF_BLK), BF16)}
    per_layer = {n: [None] * DEPTH for n in ("ffa_norm", "mix_norm", "ffb_norm", "ple_norm", "ple_w_proj")}
    for i in reversed(range(DEPTH)):
        t = f"l{i}"
        s_a, h1, s_m, s_b, h3, npl, gpre, pp = saved[i]
        dgpre, dpp = _ple_bwd(dh, gpre, pp, f"{t}_ple_bwd")
        per_layer["ple_w_proj"][i] = _mm(p[i], dpp, ta=True, name=f"{t}_ple_proj_dw")
        GB["g1"] = _mm_w128_dw(npl, dgpre, G1_128[f"ple_w_gate{i}"], GB["g1"], f"{t}_ple_gate_dw")
        dnpl = _mm_w128(dgpre, W["G1"], G1_128[f"ple_w_gate{i}"], tb=True, name=f"{t}_ple_gate_dx")
        dh, per_layer["ple_norm"][i] = _rms_bwd(dnpl, h3, W["ple_norm"][i:i + 1], dh, f"{t}_ple_norm_bwd")
        dh, per_layer["ffb_norm"][i] = _ffn_bwd(dh, s_b, W["ffb_norm"][i:i + 1], W, GB, 1, i, f"{t}_ffb")
        dnm, g_mix = (_even_bwd if i % 2 == 0 else _odd_bwd)(dh, s_m, W, GB)
        G.update(g_mix)
        dh, per_layer["mix_norm"][i] = _rms_bwd(dnm, h1, W["mix_norm"][i:i + 1], dh, f"{t}_mix_norm_bwd")
        dh, per_layer["ffa_norm"][i] = _ffn_bwd(dh, s_a, W["ffa_norm"][i:i + 1], W, GB, 0, i, f"{t}_ffa")
    for n in ("ffa_norm", "mix_norm", "ffb_norm", "ple_norm"):
        G[n] = jnp.concatenate(per_layer[n], axis=0)
    G["ple_w_proj"] = per_layer["ple_w_proj"]
    return loss_cols, dh, GB, G


def kernel(x, p, ffa_norm, ffa_w_gate_up, ffa_w_down, mix_norm, ffb_norm, ffb_w_gate_up, ffb_w_down, ple_norm, ple_w_gate, ple_w_proj, ev_w_in, ev_sinks, ev_cq_norm, ev_w_uq, ev_ckv_norm, ev_w_ukv, ev_w_out, od_w_in, od_b_f, od_w_out, final_norm, loss_target, m_ffa_norm, m_ffa_w_gate_up, m_ffa_w_down, m_mix_norm, m_ffb_norm, m_ffb_w_gate_up, m_ffb_w_down, m_ple_norm, m_ple_w_gate, m_ple_w_proj, m_ev_w_in, m_ev_sinks, m_ev_cq_norm, m_ev_w_uq, m_ev_ckv_norm, m_ev_w_ukv, m_ev_w_out, m_od_w_in, m_od_b_f, m_od_w_out, m_final_norm, v_ffa_norm, v_ffa_w_gate_up, v_ffa_w_down, v_mix_norm, v_ffb_norm, v_ffb_w_gate_up, v_ffb_w_down, v_ple_norm, v_ple_w_gate, v_ple_w_proj, v_ev_w_in, v_ev_sinks, v_ev_cq_norm, v_ev_w_uq, v_ev_ckv_norm, v_ev_w_ukv, v_ev_w_out, v_od_w_in, v_od_b_f, v_od_w_out, v_final_norm):
    given = dict(locals())
    w_in = {n: given[n] for n in WEIGHTS}

    G1, G2, G3 = _all_gather([_in_slot(g) for g in _local_groups(w_in, BF16)])
    W = {n: w_in[n] for n in SMALL}
    W["final_norm"] = final_norm.reshape(1, -1)
    W.update(_misc_weights(G3))
    W.update(G1=G1, G2=G2, G2v=G2.reshape(2, 4, G2_ROWS, FF_BLK))

    loss_cols, dx, GB, G = _local_step(x[0], p[:, 0], loss_target[0], W)

    grads = _ungroup_local(*_reduce_scatter([GB["g1"], GB["g2"], _misc_grads(G).astype(BF16)]))
    layout = [(n, int(np.prod(w_in[n].shape))) for n in SMALL]
    vec = jnp.concatenate([G[n].astype(F32).reshape(-1) for n, _ in layout] + [jnp.sum(loss_cols).reshape(1)])
    vec = jnp.pad(vec, (0, N_DEV * SMALL_COLS - vec.shape[0])).reshape(N_DEV, SMALL_COLS)
    vec = _all_reduce_small(vec).reshape(-1)
    off = 0
    for n, size in layout:
        grads[n] = vec[off: off + size].reshape(w_in[n].shape)
        off += size
    loss = vec[off]

    delta, new_m, new_v = {}, {}, {}
    for n in WEIGHTS:
        shp = w_in[n].shape
        as2d = (lambda a: a.reshape(1, -1)) if len(shp) == 1 else (lambda a: a)
        d, nm, nv = _adamw(as2d(w_in[n]), as2d(grads[n]), as2d(given["m_" + n]), as2d(given["v_" + n]), f"adamw_{n}")
        delta[n], new_m[n], new_v[n] = d.reshape(shp), nm.reshape(shp), nv.reshape(shp)
    return (loss, dx[None], *[grads[n] for n in WEIGHTS], *[delta[n] for n in WEIGHTS],
            *[new_m[n] for n in WEIGHTS], *[new_v[n] for n in WEIGHTS])
```

```python
import functools

import numpy as np
import jax
import jax.numpy as jnp
from jax import lax
from jax.experimental import pallas as pl
from jax.experimental.pallas import tpu as pltpu

F32 = jnp.float32
BF16 = jnp.bfloat16
MESH = pl.DeviceIdType.MESH

D_MODEL = 1024
D_FF = 2816
RMS_EPS = 1e-6
PLE_DIM = 256
A_HEADS, A_KV_HEADS, A_HEAD_DIM, WINDOW = 8, 2, 64, 128
A_GROUP = A_HEADS // A_KV_HEADS
B_HEADS, B_Q_LORA, B_KV_LORA, B_NOPE, B_ROPE, B_V = 8, 256, 128, 64, 32, 64
ROPE_THETA = 10000.0
C_HEADS, C_HEAD_DIM = 16, 64
EVEN_IN = 1184
EVEN_IN_PAD = 1280
ODD_IN = 3088
ODD_IN_PAD = 3200
DEPTH = 2
ADAM_LR, ADAM_B1, ADAM_B2, ADAM_EPS, ADAM_WD, ADAM_STEP = 0.001, 0.9, 0.999, 1e-08, 0.01, 10

N_DEV = 8
LANES = 128
SUBLANES = 8
EW_TILE_BYTES = 3 << 20
MM_VMEM_BYTES = 26 << 20
NEG = -1e30
ATTN_TILE = 512
ATTN_TILE_FWD = 1024
SWA_TILE = 256
QK_PAD = 80

FF_BLK = D_FF // 4
DOWN_ROWS = D_FF // N_DEV
G1_ROWS, G2_ROWS, G3_ROWS, G3_COLS = 1920, 4096, 1024, 768
G1_128 = {"ple_w_gate0": 11, "ple_w_gate1": 12, "ev_w_out": 13, "od_w_out": 14}
OD_C, EV_C, STRIP_C = 386, 148, 128
STRIP0 = OD_C + EV_C

SMALL = ["ffa_norm", "mix_norm", "ffb_norm", "ple_norm", "ev_sinks", "ev_cq_norm", "ev_ckv_norm", "od_b_f", "final_norm"]
WEIGHTS = ["ffa_norm", "ffa_w_gate_up", "ffa_w_down", "mix_norm", "ffb_norm", "ffb_w_gate_up", "ffb_w_down", "ple_norm",
           "ple_w_gate", "ple_w_proj", "ev_w_in", "ev_sinks", "ev_cq_norm", "ev_w_uq", "ev_ckv_norm", "ev_w_ukv", "ev_w_out",
           "od_w_in", "od_b_f", "od_w_out", "final_norm"]
SMALL_COLS = 1280


def _divisor(n, cap, mult):
    if n <= cap:
        return n
    for t in range(cap - cap % mult, 0, -mult):
        if n % t == 0:
            return t
    raise ValueError(f"no tile for {n} under {cap} in steps of {mult}")


def _lanes(c):
    return -(-c // LANES) * LANES


def _ew(fn, rows, vecs, outs, reds=(), *, name):
    R = rows[0].shape[0]
    per_row = sum(_lanes(a.shape[1]) * a.dtype.itemsize for a in rows) + sum(_lanes(c) * jnp.dtype(d).itemsize for c, d in outs)
    tm = _divisor(R, max(16, EW_TILE_BYTES // per_row // 16 * 16), 16) if R % 16 == 0 else R
    n_r, n_v, n_o = len(rows), len(vecs), len(outs)

    def body(*refs):
        ins = [r[...] for r in refs[: n_r + n_v]]
        res = fn(*ins)
        if not isinstance(res, (tuple, list)):
            res = (res,)
        o_refs = refs[n_r + n_v: n_r + n_v + n_o]
        r_refs = refs[n_r + n_v + n_o:]
        for ref, val in zip(o_refs, res[:n_o]):
            ref[...] = val.astype(ref.dtype)
        if r_refs:
            @pl.when(pl.program_id(0) == 0)
            def _():
                for ref in r_refs:
                    ref[...] = jnp.zeros_like(ref)
            for ref, val in zip(r_refs, res[n_o:]):
                ref[...] += val

    in_specs = [pl.BlockSpec((tm, a.shape[1]), lambda i: (i, 0)) for a in rows]
    in_specs += [pl.BlockSpec((1, a.shape[1]), lambda i: (0, 0)) for a in vecs]
    out_specs = [pl.BlockSpec((tm, c), lambda i: (i, 0)) for c, _ in outs]
    out_specs += [pl.BlockSpec((1, c), lambda i: (0, 0)) for c in reds]
    out_shape = [jax.ShapeDtypeStruct((R, c), d) for c, d in outs] + [jax.ShapeDtypeStruct((1, c), F32) for c in reds]
    res = pl.pallas_call(body, name=name, grid=(R // tm,), in_specs=in_specs, out_specs=out_specs, out_shape=out_shape)(*rows, *vecs)
    return res[0] if len(res) == 1 else res


def _rms_fwd(x, w, name):
    def fn(x, w):
        y = x * lax.rsqrt(jnp.mean(x * x, axis=-1, keepdims=True) + RMS_EPS)
        return y * w
    return _ew(fn, [x], [w], [(x.shape[1], BF16)], name=name)


def _rms_bwd(dn, x, w, dres, name):
    def fn(dn, x, *rest):
        w = rest[-1]
        r = lax.rsqrt(jnp.mean(x * x, axis=-1, keepdims=True) + RMS_EPS)
        xh = x * r
        gw = dn * w
        dx = r * (gw - xh * jnp.mean(gw * xh, axis=-1, keepdims=True))
        if len(rest) == 2:
            dx = dx + rest[0]
        return dx, jnp.sum(dn * xh, axis=0, keepdims=True)
    rows = [dn, x] + ([dres] if dres is not None else [])
    return _ew(fn, rows, [w], [(x.shape[1], F32)], [x.shape[1]], name=name)


def _ple_fwd(h, gpre, pp, name):
    return _ew(lambda h, g, q: h + jax.nn.sigmoid(g) * q, [h, gpre, pp], [], [(h.shape[1], F32)], name=name)


def _ple_bwd(dh, gpre, pp, name):
    def fn(dh, g, q):
        sg = jax.nn.sigmoid(g)
        return dh * q * (sg * (1.0 - sg)), dh * sg
    return _ew(fn, [dh, gpre, pp], [], [(dh.shape[1], BF16), (dh.shape[1], BF16)], name=name)


def _rope(x1, x2, cos, sin, name):
    c = x1.shape[1]
    return _ew(lambda a, b, co, si: (a * co - b * si, a * si + b * co), [x1, x2, cos, sin], [], [(c, F32), (c, F32)], name=name)


def _logsig_fwd(f, b, name):
    def fn(f, b):
        z = f + b
        return jnp.minimum(z, 0.0) - jnp.log(1.0 + jnp.exp(-jnp.abs(z)))
    return _ew(fn, [f], [b], [(f.shape[1], F32)], name=name)


def _logsig_bwd(dlogf, f, b, name):
    def fn(d, f, b):
        df = d * jax.nn.sigmoid(-(f + b))
        return df, jnp.sum(df, axis=0, keepdims=True)
    return _ew(fn, [dlogf, f], [b], [(f.shape[1], F32)], [f.shape[1]], name=name)


def _final_fwd_bwd(h, w, target, name):
    d = h.shape[1]

    def fn(h, t, w):
        r = lax.rsqrt(jnp.mean(h * h, axis=-1, keepdims=True) + RMS_EPS)
        xh = h * r
        y = xh * w
        err = y - t
        dy = err * (1.0 / d)
        gw = dy * w
        dx = r * (gw - xh * jnp.mean(gw * xh, axis=-1, keepdims=True))
        return dx, jnp.sum(dy * xh, axis=0, keepdims=True), jnp.sum(err * err, axis=0, keepdims=True) * (0.5 / d)
    return _ew(fn, [h, target], [w], [(d, F32)], [d, d], name=name)


def _adamw(w, g, m, v, name):
    shape = w.shape
    c = shape[-1]
    w2, g2, m2, v2 = (a.reshape(-1, c) for a in (w, g, m, v))

    def fn(w, g, m, v):
        m = ADAM_B1 * m + (1.0 - ADAM_B1) * g
        v = ADAM_B2 * v + (1.0 - ADAM_B2) * jnp.square(g)
        m_hat = m / (1.0 - ADAM_B1 ** ADAM_STEP)
        v_hat = v / (1.0 - ADAM_B2 ** ADAM_STEP)
        delta = -ADAM_LR * (m_hat / (jnp.sqrt(v_hat) + ADAM_EPS) + ADAM_WD * w)
        return delta, m, v
    d, nm, nv = _ew(fn, [w2, g2, m2, v2], [], [(c, F32)] * 3, name=name)
    return d.reshape(shape), nm.reshape(shape), nv.reshape(shape)


def _split3(v):
    hi = v.astype(BF16)
    r1 = v - hi.astype(F32)
    mid = r1.astype(BF16)
    lo = (r1 - mid.astype(F32)).astype(BF16)
    return hi, mid, lo


def _cumsum(x, reverse, name):
    S, C = x.shape
    tm = _divisor(S, 512, 16)
    nt = S // tm

    def body(x_ref, o_ref, carry):
        @pl.when(pl.program_id(0) == 0)
        def _():
            carry[...] = jnp.zeros_like(carry)
        r = lax.broadcasted_iota(jnp.int32, (tm, tm), 0)
        c = lax.broadcasted_iota(jnp.int32, (tm, tm), 1)
        tri = jnp.where((c >= r) if reverse else (c <= r), 1.0, 0.0).astype(BF16)
        xv = x_ref[...]
        acc = jnp.zeros((tm, C), F32)
        for part in _split3(xv):
            acc = acc + jnp.dot(tri, part, preferred_element_type=F32)
        o_ref[...] = acc + carry[...]
        carry[...] += jnp.sum(xv, axis=0, keepdims=True)

    idx = (lambda i: (nt - 1 - i, 0)) if reverse else (lambda i: (i, 0))
    return pl.pallas_call(
        body, name=name, grid=(nt,), in_specs=[pl.BlockSpec((tm, C), idx)], out_specs=pl.BlockSpec((tm, C), idx),
        out_shape=jax.ShapeDtypeStruct((S, C), F32), scratch_shapes=[pltpu.VMEM((1, C), F32)],
    )(x)


NN = (((1,), (0,)), ((), ()))
NT = (((1,), (1,)), ((), ()))
TN = (((0,), (0,)), ((), ()))


def _mm_call(name, grid, k_axis, a, a_spec, a2d, b, b_spec, b2d, dims, out_sds, out_spec, o2d, *,
             alpha=1.0, res=None, res_spec=None, into=None):
    nk = grid[k_axis]

    def body(*refs):
        a_ref, b_ref = refs[0], refs[1]
        res_ref = refs[2] if res is not None else None
        o_ref, acc_ref = refs[-2], refs[-1]
        k = pl.program_id(k_axis)

        @pl.when(k == 0)
        def _():
            acc_ref[...] = jnp.zeros_like(acc_ref)

        av = a_ref[...].reshape(a2d).astype(BF16)
        bv = b_ref[...].reshape(b2d).astype(BF16)
        acc_ref[...] += lax.dot_general(av, bv, dims, preferred_element_type=F32)

        @pl.when(k == nk - 1)
        def _():
            r = acc_ref[...]
            if alpha != 1.0:
                r = r * alpha
            if res_ref is not None:
                r = res_ref[...].reshape(o2d) + r
            o_ref[...] = r.reshape(o_ref.shape).astype(o_ref.dtype)

    in_specs, args = [a_spec, b_spec], [a, b]
    if res is not None:
        in_specs.append(res_spec)
        args.append(res)
    aliases = {}
    if into is not None:
        aliases = {len(args): 0}
        in_specs.append(pl.BlockSpec(memory_space=pl.ANY))
        args.append(into)
        out_sds = jax.ShapeDtypeStruct(into.shape, into.dtype)
    sem = tuple("arbitrary" if d == k_axis else "parallel" for d in range(len(grid)))
    return pl.pallas_call(
        body, name=name, grid=grid, in_specs=in_specs, out_specs=out_spec, out_shape=out_sds,
        scratch_shapes=[pltpu.VMEM(o2d, F32)], input_output_aliases=aliases,
        compiler_params=pltpu.CompilerParams(dimension_semantics=sem),
    )(*args)


def _mm(a, b, *, ta=False, tb=False, out=F32, res=None, alpha=1.0, name):
    K, M = a.shape if ta else a.shape[::-1]
    N = b.shape[0] if tb else b.shape[1]
    assert (b.shape[1] if tb else b.shape[0]) == K, (a.shape, b.shape, ta, tb)
    tk = _divisor(K, 1024, LANES)
    tn = _divisor(N, 1408, LANES)
    for cap in (1024, 512, 256, 128):
        tm = _divisor(M, cap, LANES if ta else 16)
        est = 2 * (tm * tk * a.dtype.itemsize + tk * tn * b.dtype.itemsize + tm * tn * jnp.dtype(out).itemsize)
        est += tm * tn * 4 + (2 * tm * tn * 4 if res is not None else 0)
        if est <= MM_VMEM_BYTES:
            break
    a_spec = pl.BlockSpec((tk, tm), lambda i, j, k: (k, i)) if ta else pl.BlockSpec((tm, tk), lambda i, j, k: (i, k))
    b_spec = pl.BlockSpec((tn, tk), lambda i, j, k: (j, k)) if tb else pl.BlockSpec((tk, tn), lambda i, j, k: (k, j))
    o_spec = pl.BlockSpec((tm, tn), lambda i, j, k: (i, j))
    dims = (((0 if ta else 1,), (1 if tb else 0,)), ((), ()))
    return _mm_call(name, (M // tm, N // tn, K // tk), 2, a, a_spec, (tk, tm) if ta else (tm, tk), b, b_spec,
                    (tn, tk) if tb else (tk, tn), dims, jax.ShapeDtypeStruct((M, N), out), o_spec, (tm, tn),
                    alpha=alpha, res=res, res_spec=o_spec)


def _w128_spec(blk):
    return pl.BlockSpec((N_DEV, 128, D_MODEL), lambda *_: (0, blk, 0))


def _mm_w128(a, G1, blk, *, tb=False, res=None, out=F32, name):
    S = a.shape[0]
    tm = _divisor(S, 512, 16)
    row = pl.BlockSpec((tm, D_MODEL), lambda i, k: (i, 0))
    return _mm_call(name, (S // tm, 1), 1, a, row, (tm, D_MODEL), G1, _w128_spec(blk), (D_MODEL, D_MODEL), NT if tb else NN,
                    jax.ShapeDtypeStruct((S, D_MODEL), out), row, (tm, D_MODEL), res=res, res_spec=row)


def _mm_w128_dw(a, b, blk, into, name):
    S = a.shape[0]
    tk = _divisor(S, 1024, 16)
    row = pl.BlockSpec((tk, D_MODEL), lambda i, k: (k, 0))
    return _mm_call(name, (1, S // tk), 1, a, row, (tk, D_MODEL), b, row, (tk, D_MODEL), TN, None, _w128_spec(blk),
                    (D_MODEL, D_MODEL), into=into)


def _ffn_gate_up(h, norm_w, G2v, rb, name):
    S = h.shape[0]
    tm = _divisor(S, 1024, 16)

    def body(h_ref, nw_ref, w_ref, n_ref, gu_ref, act_ref, n_scr):
        @pl.when(pl.program_id(1) == 0)
        def _():
            x = h_ref[...]
            y = x * lax.rsqrt(jnp.mean(x * x, axis=-1, keepdims=True) + RMS_EPS)
            n_scr[...] = (y * nw_ref[...]).astype(BF16)
            n_ref[...] = n_scr[...]

        nv = n_scr[...]
        g = jnp.dot(nv, w_ref[0, 0], preferred_element_type=F32)
        u = jnp.dot(nv, w_ref[1, 0], preferred_element_type=F32)
        gu_ref[0, 0] = g.astype(BF16)
        gu_ref[1, 0] = u.astype(BF16)
        act_ref[0] = (g * jax.nn.sigmoid(g) * u).astype(BF16)

    row = pl.BlockSpec((tm, D_MODEL), lambda i, j: (i, 0))
    return pl.pallas_call(
        body, name=name, grid=(S // tm, 4),
        in_specs=[row, pl.BlockSpec((1, D_MODEL), lambda i, j: (0, 0)), pl.BlockSpec((2, 1, D_MODEL, FF_BLK), lambda i, j: (0, j, rb, 0))],
        out_specs=[row, pl.BlockSpec((2, 1, tm, FF_BLK), lambda i, j: (0, j, i, 0)), pl.BlockSpec((1, tm, FF_BLK), lambda i, j: (j, i, 0))],
        out_shape=[jax.ShapeDtypeStruct((S, D_MODEL), BF16), jax.ShapeDtypeStruct((2, 4, S, FF_BLK), BF16), jax.ShapeDtypeStruct((4, S, FF_BLK), BF16)],
        scratch_shapes=[pltpu.VMEM((tm, D_MODEL), BF16)],
        compiler_params=pltpu.CompilerParams(dimension_semantics=("parallel", "arbitrary")),
    )(h, norm_w, G2v)


def _ffn_down(act, G1, ob, h, name):
    S = h.shape[0]
    tm = _divisor(S, 512, 16)
    row = pl.BlockSpec((tm, D_MODEL), lambda i, k: (i, 0))
    return _mm_call(name, (S // tm, 4), 1, act, pl.BlockSpec((1, tm, FF_BLK), lambda i, k: (k, i, 0)), (tm, FF_BLK),
                    G1, pl.BlockSpec((2, DOWN_ROWS, D_MODEL), lambda i, k: (k, ob, 0)), (FF_BLK, D_MODEL), NN,
                    jax.ShapeDtypeStruct((S, D_MODEL), F32), row, (tm, D_MODEL), alpha=0.5, res=h, res_spec=row)


def _ffn_down_dx(dh, G1, ob, gu, name):
    S = dh.shape[0]
    tm = _divisor(S, 512, 16)

    def body(dh_ref, w_ref, gu_ref, o_ref):
        w = w_ref[...].reshape(FF_BLK, D_MODEL)
        dact = lax.dot_general(dh_ref[...].astype(BF16), w, NT, preferred_element_type=F32) * 0.5
        g = gu_ref[0, 0].astype(F32)
        u = gu_ref[1, 0].astype(F32)
        sg = jax.nn.sigmoid(g)
        o_ref[0, 0] = (dact * u * (sg * (1.0 + g * (1.0 - sg)))).astype(BF16)
        o_ref[1, 0] = (dact * (g * sg)).astype(BF16)

    blk = pl.BlockSpec((2, 1, tm, FF_BLK), lambda j, i: (0, j, i, 0))
    return pl.pallas_call(
        body, name=name, grid=(4, S // tm),
        in_specs=[pl.BlockSpec((tm, D_MODEL), lambda j, i: (i, 0)), pl.BlockSpec((2, DOWN_ROWS, D_MODEL), lambda j, i: (j, ob, 0)), blk],
        out_specs=blk, out_shape=jax.ShapeDtypeStruct((2, 4, S, FF_BLK), BF16),
    )(dh, G1, gu)


def _ffn_down_dw(act, dh, ob, into, name):
    S = dh.shape[0]
    tk = _divisor(S, 1024, 16)
    return _mm_call(name, (4, S // tk), 1, act, pl.BlockSpec((1, tk, FF_BLK), lambda j, k: (j, k, 0)), (tk, FF_BLK),
                    dh, pl.BlockSpec((tk, D_MODEL), lambda j, k: (k, 0)), (tk, D_MODEL), TN, None,
                    pl.BlockSpec((2, DOWN_ROWS, D_MODEL), lambda j, k: (j, ob, 0)), (FF_BLK, D_MODEL), alpha=0.5, into=into)


def _ffn_gate_up_dw(n, dgu8, rb, into, name):
    S = n.shape[0]
    tk = _divisor(S, 1024, 16)
    return _mm_call(name, (N_DEV, S // tk), 1, n, pl.BlockSpec((tk, D_MODEL), lambda b, k: (k, 0)), (tk, D_MODEL),
                    dgu8, pl.BlockSpec((1, tk, FF_BLK), lambda b, k: (b, k, 0)), (tk, FF_BLK), TN, None,
                    pl.BlockSpec((1, D_MODEL, FF_BLK), lambda b, k: (b, rb, 0)), (D_MODEL, FF_BLK), into=into)


def _ffn_gate_up_dx(dgu8, G2, rb, h, norm_w, dres, name):
    S = h.shape[0]
    tm = _divisor(S, 512, 16)

    def body(a_ref, w_ref, h_ref, nw_ref, r_ref, o_ref, dw_ref, acc_ref):
        i, k = pl.program_id(0), pl.program_id(1)

        @pl.when(k == 0)
        def _():
            acc_ref[...] = jnp.zeros_like(acc_ref)

        @pl.when((k == 0) & (i == 0))
        def _():
            dw_ref[...] = jnp.zeros_like(dw_ref)

        acc_ref[...] += lax.dot_general(a_ref[0], w_ref[0], NT, preferred_element_type=F32)

        @pl.when(k == N_DEV - 1)
        def _():
            dn = acc_ref[...]
            x = h_ref[...]
            r = lax.rsqrt(jnp.mean(x * x, axis=-1, keepdims=True) + RMS_EPS)
            xh = x * r
            gw = dn * nw_ref[...]
            o_ref[...] = r_ref[...] + r * (gw - xh * jnp.mean(gw * xh, axis=-1, keepdims=True))
            dw_ref[...] += jnp.sum(dn * xh, axis=0, keepdims=True)

    row = pl.BlockSpec((tm, D_MODEL), lambda i, k: (i, 0))
    vec = pl.BlockSpec((1, D_MODEL), lambda i, k: (0, 0))
    return pl.pallas_call(
        body, name=name, grid=(S // tm, N_DEV),
        in_specs=[pl.BlockSpec((1, tm, FF_BLK), lambda i, k: (k, i, 0)), pl.BlockSpec((1, D_MODEL, FF_BLK), lambda i, k: (k, rb, 0)), row, vec, row],
        out_specs=[row, vec], out_shape=[jax.ShapeDtypeStruct((S, D_MODEL), F32), jax.ShapeDtypeStruct((1, D_MODEL), F32)],
        scratch_shapes=[pltpu.VMEM((tm, D_MODEL), F32)],
        compiler_params=pltpu.CompilerParams(dimension_semantics=("arbitrary", "arbitrary")),
    )(dgu8, G2, h, norm_w, dres)


def _unheads(x):
    h, S, d = x.shape
    return jnp.transpose(x, (1, 0, 2)).reshape(S, h * d)


def _exact3(v):
    rnd = lambda a: lax.reduce_precision(a, exponent_bits=8, mantissa_bits=7)
    hi = rnd(v)
    mid = rnd(v - hi)
    return hi, mid, rnd(v - hi - mid)


def _causal_mask(st, i, j, tq, tk, window):
    dist = (i * tq + lax.broadcasted_iota(jnp.int32, (tk, tq), 1)) - (j * tk + lax.broadcasted_iota(jnp.int32, (tk, tq), 0))
    mask = dist >= 0
    if window is not None:
        mask = mask & (dist < window)
    return jnp.where(mask, st, NEG)


def _attn_fwd(qT, k, vT1, *, tile, hb, window=None, sink=None, name):
    H, dqk, S = qT.shape
    G = H // k.shape[0]
    dvp = vT1.shape[1]
    dv = dvp - 16
    tq = tk = tile
    assert H % hb == 0 and (G == 1 or G % hb == 0)
    kvb = hb if G == 1 else 1

    def body(*refs):
        q_ref, k_ref, v_ref = refs[:3]
        o_ref, lse_ref = refs[-2], refs[-1]
        i = pl.program_id(1)
        carry = []
        for a in range(hb):
            if sink is not None:
                carry.append(jnp.zeros((1, tq), F32) + refs[3][a, :, 0:1])
                carry.append(jnp.where(lax.broadcasted_iota(jnp.int32, (dvp, tq), 0) == dv, 1.0, 0.0))
            else:
                carry.append(jnp.full((1, tq), NEG, F32))
                carry.append(jnp.zeros((dvp, tq), F32))

        def step(j, carry, masked):
            off = pl.multiple_of(j * tk, tk)
            out = []
            for a in range(hb):
                m, acc = carry[2 * a], carry[2 * a + 1]
                kv = a if kvb > 1 else 0
                st = jnp.dot(k_ref[kv, pl.ds(off, tk), :], q_ref[a], preferred_element_type=F32)
                if masked:
                    st = _causal_mask(st, i, j, tq, tk, window)
                m_new = jnp.maximum(m, jnp.max(st, axis=0, keepdims=True))
                pt = jnp.exp(st - m_new).astype(BF16)
                acc = jnp.exp(m - m_new) * acc + jnp.dot(v_ref[kv, :, pl.ds(off, tk)], pt, preferred_element_type=F32)
                out += [m_new, acc]
            return tuple(out)

        carry = tuple(carry)
        if window is None:
            carry = lax.fori_loop(0, i, functools.partial(step, masked=False), carry)
            carry = step(i, carry, True)
        else:
            lo = jnp.maximum((i * tq - (window - 1)) // tk, 0)
            carry = lax.fori_loop(lo, i + 1, functools.partial(step, masked=True), carry)
        for a in range(hb):
            m, acc = carry[2 * a], carry[2 * a + 1]
            l = acc[dv:dv + 1, :]
            o_ref[a] = acc[:dv, :] / l
            lse_ref[a] = m + jnp.log(l)

    kv_idx = (lambda b: b) if G == 1 else (lambda b: (b * hb) // G)
    in_specs = [
        pl.BlockSpec((hb, dqk, tq), lambda b, i: (b, 0, i)),
        pl.BlockSpec((kvb, S, dqk), lambda b, i: (kv_idx(b), 0, 0)),
        pl.BlockSpec((kvb, dvp, S), lambda b, i: (kv_idx(b), 0, 0)),
    ]
    args = [qT, k, vT1]
    if sink is not None:
        in_specs += [pl.BlockSpec((hb, 1, LANES), lambda b, i: (b, 0, 0))]
        args += [sink]
    return pl.pallas_call(
        body, name=name, grid=(H // hb, S // tq), in_specs=in_specs,
        out_specs=[pl.BlockSpec((hb, dv, tq), lambda b, i: (b, 0, i)), pl.BlockSpec((hb, 1, tq), lambda b, i: (b, 0, i))],
        out_shape=[jax.ShapeDtypeStruct((H, dv, S), F32), jax.ShapeDtypeStruct((H, 1, S), F32)],
    )(*args)


def _attn_bwd(q, qT, k, kT, v, oT, do, doT, lse, *, tile, hb, window=None, sink=None, real=None, extra=False, name):
    H, S, dqk = q.shape
    G = H // k.shape[0]
    dv = v.shape[2]
    tq = tk = tile
    nq = S // tq
    has_p = sink is not None
    real = dqk if real is None else real
    assert H % hb == 0 and (G == 1 or G % hb == 0) and not (extra and real == dqk)
    kvb = hb if G == 1 else 1

    def body(*refs):
        q_ref, qT_ref, k_ref, kT_ref, v_ref, oT_ref, do_ref, doT_ref, lse_ref = refs[:9]
        p_ref = refs[9] if has_p else None
        pos = 10 if has_p else 9
        dq_ref, dk_ref, dv_ref = refs[pos: pos + 3]
        pos += 3
        ds_ref = refs[pos] if has_p else None
        pos += has_p
        dqx_ref, dkx_ref = (refs[pos], refs[pos + 1]) if extra else (None, None)
        delta = refs[-1]
        j = pl.program_id(1)

        @pl.when(j == 0)
        def _():
            dq_ref[...] = jnp.zeros_like(dq_ref)
            if extra:
                dqx_ref[...] = jnp.zeros_like(dqx_ref)
            for a in range(hb):
                drow = jnp.sum(doT_ref[a].astype(F32) * oT_ref[a], axis=0, keepdims=True)
                delta[a] = drow
                if has_p:
                    w = jnp.exp(p_ref[a, :, 0:1] - lse_ref[a])
                    ds_ref[a] = jnp.zeros((1, LANES), F32) - jnp.sum(w * drow, axis=1, keepdims=True)

        def step(i, carry, masked):
            off = pl.multiple_of(i * tq, tq)
            out = []
            for a in range(hb):
                dk, dvv = carry[2 * a], carry[2 * a + 1]
                kv = a if kvb > 1 else 0
                st = jnp.dot(k_ref[kv], qT_ref[a, :, pl.ds(off, tq)], preferred_element_type=F32)
                if masked:
                    st = _causal_mask(st, i, j, tq, tk, window)
                pt = jnp.exp(st - lse_ref[a, :, pl.ds(off, tq)])
                dvv = dvv + jnp.dot(pt.astype(BF16), do_ref[a, pl.ds(off, tq), :], preferred_element_type=F32)
                dpt = jnp.dot(v_ref[kv], doT_ref[a, :, pl.ds(off, tq)], preferred_element_type=F32)
                dsb = (pt * (dpt - delta[a, :, pl.ds(off, tq)])).astype(BF16)
                dk = dk + jnp.dot(dsb, q_ref[a, pl.ds(off, tq), :], preferred_element_type=F32)
                dqt = jnp.dot(kT_ref[kv], dsb, preferred_element_type=F32)
                dq_ref[a, :, pl.ds(off, tq)] += dqt[:real]
                if extra:
                    dqx_ref[a, :, pl.ds(off, tq)] += dqt[real:]
                out += [dk, dvv]
            return tuple(out)

        carry = (jnp.zeros((tk, dqk), F32), jnp.zeros((tk, dv), F32)) * hb
        if window is None:
            carry = step(j, carry, True)
            carry = lax.fori_loop(j + 1, nq, functools.partial(step, masked=False), carry)
        else:
            hi = jnp.minimum(nq - 1, ((j + 1) * tk + window - 2) // tq)
            carry = lax.fori_loop(j, hi + 1, functools.partial(step, masked=True), carry)
        for a in range(hb):
            dk_ref[a] = carry[2 * a][:, :real]
            if extra:
                dkx_ref[a] = carry[2 * a][:, real:]
            dv_ref[a] = carry[2 * a + 1]

    kv_idx = (lambda b: b) if G == 1 else (lambda b: (b * hb) // G)
    rows = lambda d: pl.BlockSpec((hb, S, d), lambda b, j: (b, 0, 0))
    colsT = lambda d: pl.BlockSpec((hb, d, S), lambda b, j: (b, 0, 0))
    in_specs = [
        rows(dqk), colsT(dqk),
        pl.BlockSpec((kvb, tk, dqk), lambda b, j: (kv_idx(b), j, 0)),
        pl.BlockSpec((kvb, dqk, tk), lambda b, j: (kv_idx(b), 0, j)),
        pl.BlockSpec((kvb, tk, dv), lambda b, j: (kv_idx(b), j, 0)),
        colsT(dv), rows(dv), colsT(dv),
        pl.BlockSpec((hb, 1, S), lambda b, j: (b, 0, 0)),
    ]
    args = [q, qT, k, kT, v, oT, do, doT, lse]
    if has_p:
        in_specs += [pl.BlockSpec((hb, 1, LANES), lambda b, j: (b, 0, 0))]
        args += [sink]
    out_specs = [colsT(real), pl.BlockSpec((hb, tk, real), lambda b, j: (b, j, 0)), pl.BlockSpec((hb, tk, dv), lambda b, j: (b, j, 0))]
    out_shape = [jax.ShapeDtypeStruct((H, real, S), F32), jax.ShapeDtypeStruct((H, S, real), F32), jax.ShapeDtypeStruct((H, S, dv), F32)]
    if has_p:
        out_specs += [pl.BlockSpec((hb, 1, LANES), lambda b, j: (b, 0, 0))]
        out_shape += [jax.ShapeDtypeStruct((H, 1, LANES), F32)]
    if extra:
        out_specs += [colsT(dqk - real), pl.BlockSpec((hb, tk, dqk - real), lambda b, j: (b, j, 0))]
        out_shape += [jax.ShapeDtypeStruct((H, dqk - real, S), F32), jax.ShapeDtypeStruct((H, S, dqk - real), F32)]
    return pl.pallas_call(
        body, name=name, grid=(H // hb, S // tk), in_specs=in_specs, out_specs=out_specs, out_shape=out_shape,
        scratch_shapes=[pltpu.VMEM((hb, 1, S), F32)],
        compiler_params=pltpu.CompilerParams(dimension_semantics=("parallel", "arbitrary")),
    )(*args)


def _rows_and_cols(x3):
    xb = x3.astype(BF16)
    return jnp.transpose(xb, (1, 0, 2)), jnp.transpose(xb, (1, 2, 0))


def _v_with_ones(v3):
    S, h, _ = v3.shape
    vT = jnp.transpose(v3.astype(BF16), (1, 2, 0))
    return jnp.concatenate([vT, jnp.ones((h, 1, S), BF16), jnp.zeros((h, 15, S), BF16)], axis=1)


def _from_T(oT):
    h, d, S = oT.shape
    return jnp.transpose(oT, (2, 0, 1)).reshape(S, h * d)


def _coords():
    return lax.axis_index("x"), lax.axis_index("y"), lax.axis_index("c")


def _peer(axis):
    x, y, c = _coords()
    return {"x": (1 - x, y, c), "y": (x, 1 - y, c), "c": (x, y, 1 - c)}[axis]


HBM_SPEC = pl.BlockSpec(memory_space=pl.ANY)


def _all_gather(bufs):
    n = len(bufs)

    def body(*refs):
        outs = refs[n: 2 * n]
        send_sems, recv_sems = refs[2 * n], refs[2 * n + 1]
        x, y, c = _coords()
        me, sibling = (x, y, c), (x, y, 1 - c)
        chips = [(1 - x, y), (x, 1 - y), (1 - x, 1 - y)]

        def copy(t, k, block, to):
            px, py, pc = block
            ref = outs[t].at[4 * px + 2 * py + pc]
            return pltpu.make_async_remote_copy(ref, ref, send_sems.at[7 * t + k], recv_sems.at[7 * t + k], device_id=to, device_id_type=MESH)

        first = []
        for t in range(n):
            first.append(copy(t, 0, me, sibling))
            first += [copy(t, 1 + j, me, (*chip, c)) for j, chip in enumerate(chips)]
        for cp in first:
            cp.start()
        passed = []
        for j, chip in enumerate(chips):
            for t in range(n):
                copy(t, 1 + j, (*chip, c), me).wait_recv()
                cp = copy(t, 4 + j, (*chip, c), sibling)
                cp.start()
                passed.append(cp)
        for t in range(n):
            copy(t, 0, sibling, me).wait_recv()
            for j, chip in enumerate(chips):
                copy(t, 4 + j, (*chip, 1 - c), me).wait_recv()
        for cp in first + passed:
            cp.wait_send()

    return pl.pallas_call(
        body, name="all_gather", in_specs=[HBM_SPEC] * n, out_specs=[HBM_SPEC] * n,
        out_shape=[jax.ShapeDtypeStruct(b.shape, b.dtype) for b in bufs], input_output_aliases={t: t for t in range(n)},
        scratch_shapes=[pltpu.SemaphoreType.DMA((7 * n,)), pltpu.SemaphoreType.DMA((7 * n,))],
    )(*bufs)


def _in_slot(local):
    x, y, c = _coords()
    buf = jnp.zeros((N_DEV,) + local.shape, local.dtype)
    return lax.dynamic_update_slice(buf, local[None], (4 * x + 2 * y + c, 0, 0))


def _scatter_pair(vs, axes, name):
    n = len(vs)
    axes = [axes] * n if isinstance(axes, str) else axes

    def body(*refs):
        send_sems, recv_sems = refs[2 * n], refs[2 * n + 1]
        copies = []
        for t in range(n):
            v_ref, o_ref = refs[t], refs[n + t]
            me = lax.axis_index(axes[t])
            src = v_ref.at[1 - me] if len(v_ref.shape) == 3 else v_ref.at[:, 1 - me]
            cp = pltpu.make_async_remote_copy(src, o_ref, send_sems.at[t], recv_sems.at[t], device_id=_peer(axes[t]), device_id_type=MESH)
            cp.start()
            copies.append(cp)
        for cp in copies:
            cp.wait()

    out_shape = [jax.ShapeDtypeStruct(v.shape[:-3] + v.shape[-2:], v.dtype) for v in vs]
    return pl.pallas_call(
        body, name=name, in_specs=[HBM_SPEC] * n, out_specs=[HBM_SPEC] * n, out_shape=out_shape,
        scratch_shapes=[pltpu.SemaphoreType.DMA((n,)), pltpu.SemaphoreType.DMA((n,))],
    )(*vs)


def _add_kept(v, got, axis, out, name):
    R, C = v.shape[-2:]
    lead = v.shape[0] if v.ndim == 4 else 1
    tm = _divisor(R, max(16, EW_TILE_BYTES // (_lanes(C) * (v.dtype.itemsize + got.dtype.itemsize + jnp.dtype(out).itemsize)) // 16 * 16), 16)
    me = lax.axis_index(axis).astype(jnp.int32).reshape(1)
    v4 = v.reshape(lead, 2, R, C)
    g3 = got.reshape(lead, R, C)

    def body(me_ref, v_ref, g_ref, o_ref):
        o_ref[...] = (v_ref[0].astype(F32) + g_ref[...].astype(F32)).astype(o_ref.dtype)

    res = pl.pallas_call(
        body, name=name, out_shape=jax.ShapeDtypeStruct((lead, R, C), out),
        grid_spec=pltpu.PrefetchScalarGridSpec(
            num_scalar_prefetch=1, grid=(lead, R // tm),
            in_specs=[pl.BlockSpec((1, 1, tm, C), lambda b, i, me: (b, me[0], i, 0)), pl.BlockSpec((1, tm, C), lambda b, i, me: (b, i, 0))],
            out_specs=pl.BlockSpec((1, tm, C), lambda b, i, me: (b, i, 0))),
    )(me, v4, g3)
    return res


def _scatter_cross(vs, name):
    n = len(vs)

    def body(*refs):
        send_sems, recv_sems = refs[3 * n], refs[3 * n + 1]
        x, y, _ = _coords()
        copies = []
        for t in range(n):
            v_ref, a_ref, b_ref = refs[t], refs[n + 2 * t], refs[n + 2 * t + 1]
            h = v_ref.shape[2] // 2
            copies.append(pltpu.make_async_remote_copy(v_ref.at[1 - x, :, pl.ds(0, h)], a_ref, send_sems.at[2 * t], recv_sems.at[2 * t],
                                                       device_id=_peer("x"), device_id_type=MESH))
            copies.append(pltpu.make_async_remote_copy(v_ref.at[:, 1 - y, pl.ds(h, h)], b_ref, send_sems.at[2 * t + 1], recv_sems.at[2 * t + 1],
                                                       device_id=_peer("y"), device_id_type=MESH))
        for cp in copies:
            cp.start()
        for cp in copies:
            cp.wait()

    out_shape = []
    for v in vs:
        half = jax.ShapeDtypeStruct((2, v.shape[2] // 2, v.shape[3]), v.dtype)
        out_shape += [half, half]
    res = pl.pallas_call(
        body, name=name, in_specs=[HBM_SPEC] * n, out_specs=[HBM_SPEC] * 2 * n, out_shape=out_shape,
        scratch_shapes=[pltpu.SemaphoreType.DMA((2 * n,)), pltpu.SemaphoreType.DMA((2 * n,))],
    )(*vs)
    return res[0::2], res[1::2]


def _add_picked(v, got, axis, out, name):
    _, _, R, C = v.shape
    h = R // 2
    tm = _divisor(h, max(16, EW_TILE_BYTES // (_lanes(C) * (v.dtype.itemsize + got.dtype.itemsize + jnp.dtype(out).itemsize)) // 16 * 16), 16)
    me = lax.axis_index(axis).astype(jnp.int32).reshape(1)
    if axis == "x":
        v_map = lambda b, i, me: (me[0], b, i, 0)
    else:
        v_map = lambda b, i, me: (b, me[0], i + h // tm, 0)

    def body(me_ref, v_ref, g_ref, o_ref):
        o_ref[...] = (v_ref[0].astype(F32) + g_ref[...].astype(F32)).astype(o_ref.dtype)

    return pl.pallas_call(
        body, name=name, out_shape=jax.ShapeDtypeStruct((2, h, C), out),
        grid_spec=pltpu.PrefetchScalarGridSpec(
            num_scalar_prefetch=1, grid=(2, h // tm),
            in_specs=[pl.BlockSpec((1, 1, tm, C), v_map), pl.BlockSpec((1, tm, C), lambda b, i, me: (b, i, 0))],
            out_specs=pl.BlockSpec((1, tm, C), lambda b, i, me: (b, i, 0))),
    )(me, v, got)


def _reduce_scatter(gs):
    vs = [g.reshape(4, 2, *g.shape[1:]) for g in gs]
    got = _scatter_pair(vs, "c", "reduce_scatter_c")
    vs = [_add_kept(v, r, "c", BF16, f"reduce_scatter_add_c{t}") for t, (v, r) in enumerate(zip(vs, got))]
    vs = [v.reshape(2, 2, v.shape[1], v.shape[2]) for v in vs]
    got_a, got_b = _scatter_cross(vs, "reduce_scatter_xy")
    up = [_add_picked(v, r, "x", BF16, f"reduce_scatter_add_x{t}") for t, (v, r) in enumerate(zip(vs, got_a))]
    lo = [_add_picked(v, r, "y", BF16, f"reduce_scatter_add_y{t}") for t, (v, r) in enumerate(zip(vs, got_b))]
    n = len(gs)
    got = _scatter_pair(up + lo, ["y"] * n + ["x"] * n, "reduce_scatter_yx")
    out = []
    for t in range(n):
        a = _add_kept(up[t], got[t], "y", F32, f"reduce_scatter_add_y2{t}")[0]
        b = _add_kept(lo[t], got[n + t], "x", F32, f"reduce_scatter_add_x2{t}")[0]
        out.append(jnp.concatenate([a, b], axis=0))
    return out


def _all_reduce_small(v):
    def body(v_ref, o_ref, buf, send_sems, recv_sems):
        x, y, c = _coords()
        me = 4 * x + 2 * y + c
        buf[me] = v_ref[...]
        copies = []
        for k in range(1, N_DEV):
            peer = tuple((1 - a) if (k >> s) & 1 else a for a, s in ((x, 2), (y, 1), (c, 0)))
            cp = pltpu.make_async_remote_copy(v_ref, buf.at[me], send_sems.at[k - 1], recv_sems.at[k - 1], device_id=peer, device_id_type=MESH)
            cp.start()
            copies.append(cp)
        for cp in copies:
            cp.wait()
        acc = buf[0]
        for d in range(1, N_DEV):
            acc = acc + buf[d]
        o_ref[...] = acc

    vm = pl.BlockSpec(memory_space=pltpu.VMEM)
    return pl.pallas_call(
        body, name="all_reduce_small", in_specs=[vm], out_specs=vm, out_shape=jax.ShapeDtypeStruct(v.shape, F32),
        scratch_shapes=[pltpu.VMEM((N_DEV,) + v.shape, F32), pltpu.SemaphoreType.DMA((N_DEV - 1,)), pltpu.SemaphoreType.DMA((N_DEV - 1,))],
    )(v)


def _local_groups(w, dtype):
    g1 = jnp.concatenate([w["ffa_w_down"].reshape(-1, D_MODEL), w["ffb_w_down"].reshape(-1, D_MODEL),
                          w["ple_w_gate"].reshape(-1, D_MODEL), w["ev_w_out"][0], w["od_w_out"][0]], axis=0).astype(dtype)
    g2 = jnp.concatenate([w["ffa_w_gate_up"].reshape(-1, FF_BLK), w["ffb_w_gate_up"].reshape(-1, FF_BLK)], axis=0).astype(dtype)
    strip = jnp.concatenate([w["ple_w_proj"].reshape(-1, STRIP_C), w["ev_w_ukv"][0], jnp.pad(w["ev_w_uq"][0], ((0, 0), (0, STRIP_C - 96))),
                             jnp.zeros((G3_ROWS - 896, STRIP_C), F32)], axis=0)
    g3 = jnp.concatenate([w["od_w_in"][0], w["ev_w_in"][0], strip, jnp.zeros((G3_ROWS, G3_COLS - STRIP0 - STRIP_C), F32)], axis=1).astype(dtype)
    return g1, g2, g3


def _ungroup_local(r1, r2, r3):
    out = {
        "ffa_w_down": r1[:704].reshape(2, DOWN_ROWS, D_MODEL), "ffb_w_down": r1[704:1408].reshape(2, DOWN_ROWS, D_MODEL),
        "ple_w_gate": r1[1408:1664].reshape(2, 128, D_MODEL), "ev_w_out": r1[1664:1792][None], "od_w_out": r1[1792:1920][None],
        "ffa_w_gate_up": r2[:2048].reshape(2, D_MODEL, FF_BLK), "ffb_w_gate_up": r2[2048:].reshape(2, D_MODEL, FF_BLK),
        "od_w_in": r3[:, :OD_C][None], "ev_w_in": r3[:, OD_C:STRIP0][None],
    }
    strip = r3[:, STRIP0:STRIP0 + STRIP_C]
    out["ple_w_proj"] = strip[:512].reshape(2, PLE_DIM, STRIP_C)
    out["ev_w_ukv"] = strip[512:640][None]
    out["ev_w_uq"] = strip[640:896, :96][None]
    return out


def _cols(a):
    return jnp.transpose(a, (1, 0, 2)).reshape(a.shape[1], -1)


def _blocks(g, c):
    return jnp.transpose(g.reshape(g.shape[0], N_DEV, c), (1, 0, 2))


def _uq_permute(w):
    r = w.shape[0]
    w3 = w.reshape(r, B_HEADS, B_NOPE + B_ROPE)
    half = B_ROPE // 2
    return jnp.concatenate([w3[:, :, :B_NOPE].reshape(r, -1), w3[:, :, B_NOPE:B_NOPE + half].reshape(r, -1), w3[:, :, B_NOPE + half:].reshape(r, -1)], axis=1)


def _uq_unpermute(g):
    r = g.shape[0]
    half = B_ROPE // 2
    n = B_HEADS * B_NOPE
    parts = [g[:, :n].reshape(r, B_HEADS, B_NOPE), g[:, n:n + B_HEADS * half].reshape(r, B_HEADS, half), g[:, n + B_HEADS * half:].reshape(r, B_HEADS, half)]
    return jnp.concatenate(parts, axis=2).reshape(r, -1)


def _ukv_permute(w):
    r = w.shape[0]
    return jnp.transpose(w.reshape(r, B_HEADS, 2, B_NOPE), (0, 2, 1, 3)).reshape(r, -1)


def _ukv_unpermute(g):
    r = g.shape[0]
    return jnp.transpose(g.reshape(r, 2, B_HEADS, B_NOPE), (0, 2, 1, 3)).reshape(r, -1)


def _misc_weights(G3):
    strip = G3[:, :, STRIP0:STRIP0 + STRIP_C]
    return {
        "od_w_in": jnp.pad(_cols(G3[:, :, :OD_C]), ((0, 0), (0, ODD_IN_PAD - ODD_IN))),
        "ev_w_in": jnp.pad(_cols(G3[:, :, OD_C:STRIP0]), ((0, 0), (0, EVEN_IN_PAD - EVEN_IN))),
        "ple_w_proj": [_cols(strip[:, i * PLE_DIM:(i + 1) * PLE_DIM]) for i in range(DEPTH)],
        "ev_w_ukv": _ukv_permute(_cols(strip[:, 512:640])),
        "ev_w_uq": _uq_permute(_cols(strip[:, 640:896, :96])),
    }


def _misc_grads(G):
    strip = jnp.concatenate([
        _blocks(G["ple_w_proj"][0], STRIP_C), _blocks(G["ple_w_proj"][1], STRIP_C), _blocks(_ukv_unpermute(G["ev_w_ukv"]), STRIP_C),
        jnp.pad(_blocks(_uq_unpermute(G["ev_w_uq"]), 96), ((0, 0), (0, 0), (0, STRIP_C - 96))),
        jnp.zeros((N_DEV, G3_ROWS - 896, STRIP_C), F32)], axis=1)
    return jnp.concatenate([_blocks(G["od_w_in"][:, :ODD_IN], OD_C), _blocks(G["ev_w_in"][:, :EVEN_IN], EV_C), strip,
                            jnp.zeros((N_DEV, G3_ROWS, G3_COLS - STRIP0 - STRIP_C), F32)], axis=2)


def _ffn_fwd(h, norm_w, W, f, i, tag):
    n, gu, act = _ffn_gate_up(h, norm_w, W["G2v"], 2 * f + i, f"{tag}_gate_up")
    out = _ffn_down(act, W["G1"], 2 * f + i, h, f"{tag}_down")
    return out, (h, n, gu, act)


def _ffn_bwd(dout, saved, norm_w, W, GB, f, i, tag):
    h, n, gu, act = saved
    S = h.shape[0]
    blk = 2 * f + i
    GB["g1"] = _ffn_down_dw(act, dout, blk, GB["g1"], f"{tag}_down_dw")
    dgu = _ffn_down_dx(dout, W["G1"], blk, gu, f"{tag}_down_dx").reshape(N_DEV, S, FF_BLK)
    GB["g2"] = _ffn_gate_up_dw(n, dgu, blk, GB["g2"], f"{tag}_gate_up_dw")
    return _ffn_gate_up_dx(dgu, W["G2"], blk, h, norm_w, dout, f"{tag}_gate_up_dx")


def _rope_tables(S):
    inv = ROPE_THETA ** (-jnp.arange(0, B_ROPE, 2, dtype=F32) / B_ROPE)
    ang = jnp.arange(S, dtype=F32)[:, None] * inv[None, :]
    return jnp.cos(ang), jnp.sin(ang)


def _alibi_columns(S):
    t = jnp.arange(S, dtype=jnp.int32)
    hi = ((t // 16) * 16).astype(F32)
    lo = (t % 16).astype(F32)
    slopes = 2.0 ** (-8.0 * jnp.arange(1, A_HEADS + 1, dtype=F32) / A_HEADS)
    zq = jnp.zeros((S, A_HEADS), F32)
    rest = QK_PAD - A_HEAD_DIM - 4
    qc = jnp.stack([-slopes[None, :] * hi[:, None], -slopes[None, :] * lo[:, None], zq + slopes[None, :], zq + slopes[None, :]] + [zq] * rest, axis=-1)
    one = jnp.ones((S, A_KV_HEADS), F32)
    zk = jnp.zeros((S, A_KV_HEADS), F32)
    kc = jnp.stack([one, one, zk + hi[:, None], zk + lo[:, None]] + [zk] * rest, axis=-1)
    return qc, kc


def _sink_prm(sinks):
    return jnp.zeros((A_HEADS, 1, LANES), F32).at[:, 0, 0].set(sinks.astype(F32))


def _even_fwd(hn, h, W):
    S = hn.shape[0]
    proj = _mm(hn, W["ev_w_in"], name="ev_in")
    a_q, a_k, a_v = proj[:, :512], proj[:, 512:640], proj[:, 640:768]
    c_q, c_kv = proj[:, 768:1024], proj[:, 1024:1152]
    kr1, kr2 = proj[:, 1152:1168], proj[:, 1168:1184]
    qc, kc = _alibi_columns(S)
    qa, qaT = _rows_and_cols(jnp.concatenate([(a_q * A_HEAD_DIM ** -0.5).reshape(S, A_HEADS, A_HEAD_DIM), qc], axis=-1))
    ka, kaT = _rows_and_cols(jnp.concatenate([a_k.reshape(S, A_KV_HEADS, A_HEAD_DIM), kc], axis=-1))
    va3 = a_v.reshape(S, A_KV_HEADS, A_HEAD_DIM)
    va = jnp.transpose(va3.astype(BF16), (1, 0, 2))
    prm = _sink_prm(W["ev_sinks"][0])
    oaT, lse_a = _attn_fwd(qaT, ka, _v_with_ones(va3), tile=SWA_TILE, hb=2, window=WINDOW, sink=prm, name="swa_fwd")
    cqn = _rms_fwd(c_q, W["ev_cq_norm"], "ev_cq_norm")
    q_all = _mm(cqn, W["ev_w_uq"], name="ev_uq")
    ckvn = _rms_fwd(c_kv, W["ev_ckv_norm"], "ev_ckv_norm")
    kv_all = _mm(ckvn, W["ev_w_ukv"], name="ev_ukv")
    cos, sin = _rope_tables(S)
    cos8, sin8 = jnp.tile(cos, (1, B_HEADS)), jnp.tile(sin, (1, B_HEADS))
    q1, q2 = _rope(q_all[:, 512:640], q_all[:, 640:768], cos8, sin8, "ev_rope_q")
    k1, k2 = _rope(kr1, kr2, cos, sin, "ev_rope_k")
    half = B_ROPE // 2
    scale = (B_NOPE + B_ROPE) ** -0.5
    qb, qbT = _rows_and_cols(jnp.concatenate([q_all[:, :512].reshape(S, B_HEADS, B_NOPE), q1.reshape(S, B_HEADS, half), q2.reshape(S, B_HEADS, half)], axis=-1) * scale)
    kro = jnp.broadcast_to(jnp.concatenate([k1, k2], axis=1)[:, None, :], (S, B_HEADS, B_ROPE))
    kb, kbT = _rows_and_cols(jnp.concatenate([kv_all[:, :512].reshape(S, B_HEADS, B_NOPE), kro], axis=-1))
    vb3 = kv_all[:, 512:].reshape(S, B_HEADS, B_V)
    vb = jnp.transpose(vb3.astype(BF16), (1, 0, 2))
    obT, lse_b = _attn_fwd(qbT, kb, _v_with_ones(vb3), tile=min(ATTN_TILE_FWD, S), hb=2, name="mla_fwd")
    cat = jnp.concatenate([_from_T(oaT), _from_T(obT)], axis=1)
    out = _mm_w128(cat, W["G1"], G1_128["ev_w_out"], res=h, name="ev_out")
    return out, (hn, proj, (qa, qaT, ka, kaT, va, oaT, lse_a), prm, cqn, ckvn, (qb, qbT, kb, kbT, vb, obT, lse_b), cat)


def _even_bwd(dout, saved, W, GB):
    hn, proj, (qa, qaT, ka, kaT, va, oaT, lse_a), prm, cqn, ckvn, (qb, qbT, kb, kbT, vb, obT, lse_b), cat = saved
    S = hn.shape[0]
    G = {}
    dcat = _mm_w128(dout, W["G1"], G1_128["ev_w_out"], tb=True, out=BF16, name="ev_out_dx")
    GB["g1"] = _mm_w128_dw(cat, dout, G1_128["ev_w_out"], GB["g1"], "ev_out_dw")
    doa, doaT = _rows_and_cols(dcat[:, :512].reshape(S, A_HEADS, A_HEAD_DIM))
    dqaT, dka, dva, dsink = _attn_bwd(qa, qaT, ka, kaT, va, oaT, doa, doaT, lse_a, tile=SWA_TILE, hb=2, window=WINDOW, sink=prm, real=A_HEAD_DIM,
                                       name="swa_bwd")
    G["ev_sinks"] = dsink[:, 0, 0]
    dqa = _from_T(dqaT) * A_HEAD_DIM ** -0.5
    dka = dka.reshape(A_KV_HEADS, A_GROUP, S, A_HEAD_DIM).sum(axis=1)
    dva = dva.reshape(A_KV_HEADS, A_GROUP, S, A_HEAD_DIM).sum(axis=1)
    dob, dobT = _rows_and_cols(dcat[:, 512:].reshape(S, B_HEADS, B_V))
    dqbT, dkb, dvb = _attn_bwd(qb, qbT, kb, kbT, vb, obT, dob, dobT, lse_b, tile=ATTN_TILE, hb=2, name="mla_bwd")
    half = B_ROPE // 2
    dqb = jnp.transpose(dqbT, (2, 0, 1)) * (B_NOPE + B_ROPE) ** -0.5
    dkb = jnp.transpose(dkb, (1, 0, 2))
    cos, sin = _rope_tables(S)
    cos8, sin8 = jnp.tile(cos, (1, B_HEADS)), jnp.tile(sin, (1, B_HEADS))
    dq1, dq2 = _rope(dqb[:, :, B_NOPE:B_NOPE + half].reshape(S, -1), dqb[:, :, B_NOPE + half:].reshape(S, -1), cos8, -sin8, "ev_rope_q_bwd")
    dq_all = jnp.concatenate([dqb[:, :, :B_NOPE].reshape(S, -1), dq1, dq2], axis=1).astype(BF16)
    dkr = dkb[:, :, B_NOPE:].sum(axis=1)
    dk1, dk2 = _rope(dkr[:, :half], dkr[:, half:], cos, -sin, "ev_rope_k_bwd")
    dkv_all = jnp.concatenate([dkb[:, :, :B_NOPE].reshape(S, -1), _unheads(dvb)], axis=1).astype(BF16)
    G["ev_w_uq"] = _mm(cqn, dq_all, ta=True, name="ev_uq_dw")
    dcqn = _mm(dq_all, W["ev_w_uq"], tb=True, name="ev_uq_dx")
    dc_q, G["ev_cq_norm"] = _rms_bwd(dcqn, proj[:, 768:1024], W["ev_cq_norm"], None, "ev_cq_norm_bwd")
    G["ev_w_ukv"] = _mm(ckvn, dkv_all, ta=True, name="ev_ukv_dw")
    dckvn = _mm(dkv_all, W["ev_w_ukv"], tb=True, name="ev_ukv_dx")
    dc_kv, G["ev_ckv_norm"] = _rms_bwd(dckvn, proj[:, 1024:1152], W["ev_ckv_norm"], None, "ev_ckv_norm_bwd")
    dproj = jnp.concatenate([dqa, _unheads(dka), _unheads(dva), dc_q, dc_kv, dk1, dk2,
                             jnp.zeros((S, EVEN_IN_PAD - EVEN_IN), F32)], axis=1).astype(BF16)
    G["ev_w_in"] = _mm(hn, dproj, ta=True, name="ev_in_dw")
    dhn = _mm(dproj, W["ev_w_in"], tb=True, name="ev_in_dx")
    return dhn, G


def _odd_fwd(hn, h, W):
    S = hn.shape[0]
    w = C_HEADS * C_HEAD_DIM
    proj = _mm(hn, W["od_w_in"], name="od_in")
    f_logit = proj[:, 3 * w: 3 * w + C_HEADS]
    logf = _logsig_fwd(f_logit, W["od_b_f"], "od_logsig")
    logc = _cumsum(logf, False, "od_cumsum")
    parts = [p[:, :, None] for p in _exact3(logc)]
    ones = [jnp.ones((S, C_HEADS, 1), F32)] * 3
    pad = [jnp.zeros((S, C_HEADS, QK_PAD - C_HEAD_DIM - 6), F32)]
    q3 = (proj[:, :w] * C_HEAD_DIM ** -0.5).reshape(S, C_HEADS, C_HEAD_DIM)
    k3 = proj[:, w:2 * w].reshape(S, C_HEADS, C_HEAD_DIM)
    q, qT = _rows_and_cols(jnp.concatenate([q3, jnp.concatenate(parts + ones + pad, axis=-1)], axis=-1))
    k, kT = _rows_and_cols(jnp.concatenate([k3, jnp.concatenate(ones + [-p for p in parts] + pad, axis=-1)], axis=-1))
    v3 = proj[:, 2 * w:3 * w].reshape(S, C_HEADS, C_HEAD_DIM)
    v = jnp.transpose(v3.astype(BF16), (1, 0, 2))
    oT, lse = _attn_fwd(qT, k, _v_with_ones(v3), tile=min(ATTN_TILE_FWD, S), hb=2, name="fox_fwd")
    cat = _from_T(oT)
    out = _mm_w128(cat, W["G1"], G1_128["od_w_out"], res=h, name="od_out")
    return out, (hn, q, qT, k, kT, v, f_logit, oT, lse, cat)


def _odd_bwd(dout, saved, W, GB):
    hn, q, qT, k, kT, v, f_logit, oT, lse, cat = saved
    S = hn.shape[0]
    G = {}
    dcat = _mm_w128(dout, W["G1"], G1_128["od_w_out"], tb=True, out=BF16, name="od_out_dx")
    GB["g1"] = _mm_w128_dw(cat, dout, G1_128["od_w_out"], GB["g1"], "od_out_dw")
    do, doT = _rows_and_cols(dcat.reshape(S, C_HEADS, C_HEAD_DIM))
    dqT, dk, dv, dqxT, dkx = _attn_bwd(q, qT, k, kT, v, oT, do, doT, lse, tile=ATTN_TILE, hb=2, real=C_HEAD_DIM, extra=True, name="fox_bwd")
    dlogc = jnp.transpose(dqxT[:, 0, :] - dkx[:, :, 3])
    dlogf = _cumsum(dlogc, True, "od_cumsum_bwd")
    df, db = _logsig_bwd(dlogf, f_logit, W["od_b_f"], "od_logsig_bwd")
    G["od_b_f"] = db
    dproj = jnp.concatenate([_from_T(dqT) * C_HEAD_DIM ** -0.5, _unheads(dk), _unheads(dv), df,
                             jnp.zeros((S, ODD_IN_PAD - ODD_IN), F32)], axis=1).astype(BF16)
    G["od_w_in"] = _mm(hn, dproj, ta=True, name="od_in_dw")
    dhn = _mm(dproj, W["od_w_in"], tb=True, name="od_in_dx")
    return dhn, G


def _local_step(x, p, target, W):
    h = x
    saved = []
    for i in range(DEPTH):
        t = f"l{i}"
        h1, s_a = _ffn_fwd(h, W["ffa_norm"][i:i + 1], W, 0, i, f"{t}_ffa")
        nm = _rms_fwd(h1, W["mix_norm"][i:i + 1], f"{t}_mix_norm")
        h2, s_m = (_even_fwd if i % 2 == 0 else _odd_fwd)(nm, h1, W)
        h3, s_b = _ffn_fwd(h2, W["ffb_norm"][i:i + 1], W, 1, i, f"{t}_ffb")
        npl = _rms_fwd(h3, W["ple_norm"][i:i + 1], f"{t}_ple_norm")
        gpre = _mm_w128(npl, W["G1"], G1_128[f"ple_w_gate{i}"], name=f"{t}_ple_gate")
        pp = _mm(p[i], W["ple_w_proj"][i], name=f"{t}_ple_proj")
        h4 = _ple_fwd(h3, gpre, pp, f"{t}_ple")
        saved.append((s_a, h1, s_m, s_b, h3, npl, gpre, pp))
        h = h4
    dh, g_final, loss_cols = _final_fwd_bwd(h, W["final_norm"], target, "final")
    G = {"final_norm": g_final}
    GB = {"g1": lax.empty((N_DEV, G1_ROWS, D_MODEL), BF16), "g2": lax.empty((N_DEV, G2_ROWS, FF_BLK), BF16)}
    per_layer = {n: [None] * DEPTH for n in ("ffa_norm", "mix_norm", "ffb_norm", "ple_norm", "ple_w_proj")}
    for i in reversed(range(DEPTH)):
        t = f"l{i}"
        s_a, h1, s_m, s_b, h3, npl, gpre, pp = saved[i]
        dgpre, dpp = _ple_bwd(dh, gpre, pp, f"{t}_ple_bwd")
        per_layer["ple_w_proj"][i] = _mm(p[i], dpp, ta=True, name=f"{t}_ple_proj_dw")
        GB["g1"] = _mm_w128_dw(npl, dgpre, G1_128[f"ple_w_gate{i}"], GB["g1"], f"{t}_ple_gate_dw")
        dnpl = _mm_w128(dgpre, W["G1"], G1_128[f"ple_w_gate{i}"], tb=True, name=f"{t}_ple_gate_dx")
        dh, per_layer["ple_norm"][i] = _rms_bwd(dnpl, h3, W["ple_norm"][i:i + 1], dh, f"{t}_ple_norm_bwd")
        dh, per_layer["ffb_norm"][i] = _ffn_bwd(dh, s_b, W["ffb_norm"][i:i + 1], W, GB, 1, i, f"{t}_ffb")
        dnm, g_mix = (_even_bwd if i % 2 == 0 else _odd_bwd)(dh, s_m, W, GB)
        G.update(g_mix)
        dh, per_layer["mix_norm"][i] = _rms_bwd(dnm, h1, W["mix_norm"][i:i + 1], dh, f"{t}_mix_norm_bwd")
        dh, per_layer["ffa_norm"][i] = _ffn_bwd(dh, s_a, W["ffa_norm"][i:i + 1], W, GB, 0, i, f"{t}_ffa")
    for n in ("ffa_norm", "mix_norm", "ffb_norm", "ple_norm"):
        G[n] = jnp.concatenate(per_layer[n], axis=0)
    G["ple_w_proj"] = per_layer["ple_w_proj"]
    return loss_cols, dh, GB, G


def kernel(x, p, ffa_norm, ffa_w_gate_up, ffa_w_down, mix_norm, ffb_norm, ffb_w_gate_up, ffb_w_down, ple_norm, ple_w_gate, ple_w_proj, ev_w_in, ev_sinks, ev_cq_norm, ev_w_uq, ev_ckv_norm, ev_w_ukv, ev_w_out, od_w_in, od_b_f, od_w_out, final_norm, loss_target, m_ffa_norm, m_ffa_w_gate_up, m_ffa_w_down, m_mix_norm, m_ffb_norm, m_ffb_w_gate_up, m_ffb_w_down, m_ple_norm, m_ple_w_gate, m_ple_w_proj, m_ev_w_in, m_ev_sinks, m_ev_cq_norm, m_ev_w_uq, m_ev_ckv_norm, m_ev_w_ukv, m_ev_w_out, m_od_w_in, m_od_b_f, m_od_w_out, m_final_norm, v_ffa_norm, v_ffa_w_gate_up, v_ffa_w_down, v_mix_norm, v_ffb_norm, v_ffb_w_gate_up, v_ffb_w_down, v_ple_norm, v_ple_w_gate, v_ple_w_proj, v_ev_w_in, v_ev_sinks, v_ev_cq_norm, v_ev_w_uq, v_ev_ckv_norm, v_ev_w_ukv, v_ev_w_out, v_od_w_in, v_od_b_f, v_od_w_out, v_final_norm):
    given = dict(locals())
    w_in = {n: given[n] for n in WEIGHTS}

    G1, G2, G3 = _all_gather([_in_slot(g) for g in _local_groups(w_in, BF16)])
    W = {n: w_in[n] for n in SMALL}
    W["final_norm"] = final_norm.reshape(1, -1)
    W.update(_misc_weights(G3))
    W.update(G1=G1, G2=G2, G2v=G2.reshape(2, 4, G2_ROWS, FF_BLK))

    loss_cols, dx, GB, G = _local_step(x[0], p[:, 0], loss_target[0], W)

    grads = _ungroup_local(*_reduce_scatter([GB["g1"], GB["g2"], _misc_grads(G).astype(BF16)]))
    layout = [(n, int(np.prod(w_in[n].shape))) for n in SMALL]
    vec = jnp.concatenate([G[n].astype(F32).reshape(-1) for n, _ in layout] + [jnp.sum(loss_cols).reshape(1)])
    vec = jnp.pad(vec, (0, N_DEV * SMALL_COLS - vec.shape[0])).reshape(N_DEV, SMALL_COLS)
    vec = _all_reduce_small(vec).reshape(-1)
    off = 0
    for n, size in layout:
        grads[n] = vec[off: off + size].reshape(w_in[n].shape)
        off += size
    loss = vec[off]

    delta, new_m, new_v = {}, {}, {}
    for n in WEIGHTS:
        shp = w_in[n].shape
        as2d = (lambda a: a.reshape(1, -1)) if len(shp) == 1 else (lambda a: a)
        d, nm, nv = _adamw(as2d(w_in[n]), as2d(grads[n]), as2d(given["m_" + n]), as2d(given["v_" + n]), f"adamw_{n}")
        delta[n], new_m[n], new_v[n] = d.reshape(shp), nm.reshape(shp), nv.reshape(shp)
    return (loss, dx[None], *[grads[n] for n in WEIGHTS], *[delta[n] for n in WEIGHTS],
            *[new_m[n] for n in WEIGHTS], *[new_v[n] for n in WEIGHTS])
```

```python
import functools

import numpy as np
import jax
import jax.numpy as jnp
from jax import lax
from jax.experimental import pallas as pl
from jax.experimental.pallas import tpu as pltpu

F32 = jnp.float32
BF16 = jnp.bfloat16
MESH = pl.DeviceIdType.MESH

D_MODEL = 1024
D_FF = 2816
RMS_EPS = 1e-6
PLE_DIM = 256
A_HEADS, A_KV_HEADS, A_HEAD_DIM, WINDOW = 8, 2, 64, 128
A_GROUP = A_HEADS // A_KV_HEADS
B_HEADS, B_Q_LORA, B_KV_LORA, B_NOPE, B_ROPE, B_V = 8, 256, 128, 64, 32, 64
ROPE_THETA = 10000.0
C_HEADS, C_HEAD_DIM = 16, 64
EVEN_IN = 1184
EVEN_IN_PAD = 1280
ODD_IN = 3088
ODD_IN_PAD = 3200
DEPTH = 2
ADAM_LR, ADAM_B1, ADAM_B2, ADAM_EPS, ADAM_WD, ADAM_STEP = 0.001, 0.9, 0.999, 1e-08, 0.01, 10

N_DEV = 8
LANES = 128
SUBLANES = 8
EW_TILE_BYTES = 3 << 20
MM_VMEM_BYTES = 26 << 20
NEG = -1e30
ATTN_TILE = 512
ATTN_TILE_FWD = 1024
SWA_TILE = 256
QK_PAD = 80

FF_BLK = D_FF // 4
DOWN_ROWS = D_FF // N_DEV
A_ROWS, B_ROWS, C_ROWS, G3_ROWS, G3_COLS = 2 * DOWN_ROWS, 256, 2 * D_MODEL, 1024, 768
PLE_GATE_BLK, MIX_OUT_BLK = 0, 1
OD_C, EV_C, STRIP_C = 386, 148, 128
STRIP0 = OD_C + EV_C

SMALL = ["ffa_norm", "mix_norm", "ffb_norm", "ple_norm", "ev_sinks", "ev_cq_norm", "ev_ckv_norm", "od_b_f", "final_norm"]
WEIGHTS = ["ffa_norm", "ffa_w_gate_up", "ffa_w_down", "mix_norm", "ffb_norm", "ffb_w_gate_up", "ffb_w_down", "ple_norm",
           "ple_w_gate", "ple_w_proj", "ev_w_in", "ev_sinks", "ev_cq_norm", "ev_w_uq", "ev_ckv_norm", "ev_w_ukv", "ev_w_out",
           "od_w_in", "od_b_f", "od_w_out", "final_norm"]
SMALL_COLS = 1280


def _divisor(n, cap, mult):
    if n <= cap:
        return n
    for t in range(cap - cap % mult, 0, -mult):
        if n % t == 0:
            return t
    raise ValueError(f"no tile for {n} under {cap} in steps of {mult}")


def _lanes(c):
    return -(-c // LANES) * LANES


def _ew(fn, rows, vecs, outs, reds=(), *, name):
    R = rows[0].shape[0]
    per_row = sum(_lanes(a.shape[1]) * a.dtype.itemsize for a in rows) + sum(_lanes(c) * jnp.dtype(d).itemsize for c, d in outs)
    tm = _divisor(R, max(16, EW_TILE_BYTES // per_row // 16 * 16), 16) if R % 16 == 0 else R
    n_r, n_v, n_o = len(rows), len(vecs), len(outs)

    def body(*refs):
        ins = [r[...] for r in refs[: n_r + n_v]]
        res = fn(*ins)
        if not isinstance(res, (tuple, list)):
            res = (res,)
        o_refs = refs[n_r + n_v: n_r + n_v + n_o]
        r_refs = refs[n_r + n_v + n_o:]
        for ref, val in zip(o_refs, res[:n_o]):
            ref[...] = val.astype(ref.dtype)
        if r_refs:
            @pl.when(pl.program_id(0) == 0)
            def _():
                for ref in r_refs:
                    ref[...] = jnp.zeros_like(ref)
            for ref, val in zip(r_refs, res[n_o:]):
                ref[...] += val

    in_specs = [pl.BlockSpec((tm, a.shape[1]), lambda i: (i, 0)) for a in rows]
    in_specs += [pl.BlockSpec((1, a.shape[1]), lambda i: (0, 0)) for a in vecs]
    out_specs = [pl.BlockSpec((tm, c), lambda i: (i, 0)) for c, _ in outs]
    out_specs += [pl.BlockSpec((1, c), lambda i: (0, 0)) for c in reds]
    out_shape = [jax.ShapeDtypeStruct((R, c), d) for c, d in outs] + [jax.ShapeDtypeStruct((1, c), F32) for c in reds]
    res = pl.pallas_call(body, name=name, grid=(R // tm,), in_specs=in_specs, out_specs=out_specs, out_shape=out_shape)(*rows, *vecs)
    return res[0] if len(res) == 1 else res


def _rms_fwd(x, w, name):
    def fn(x, w):
        y = x * lax.rsqrt(jnp.mean(x * x, axis=-1, keepdims=True) + RMS_EPS)
        return y * w
    return _ew(fn, [x], [w], [(x.shape[1], BF16)], name=name)


def _rms_bwd(dn, x, w, dres, name):
    def fn(dn, x, *rest):
        w = rest[-1]
        r = lax.rsqrt(jnp.mean(x * x, axis=-1, keepdims=True) + RMS_EPS)
        xh = x * r
        gw = dn * w
        dx = r * (gw - xh * jnp.mean(gw * xh, axis=-1, keepdims=True))
        if len(rest) == 2:
            dx = dx + rest[0]
        return dx, jnp.sum(dn * xh, axis=0, keepdims=True)
    rows = [dn, x] + ([dres] if dres is not None else [])
    return _ew(fn, rows, [w], [(x.shape[1], F32)], [x.shape[1]], name=name)


def _ple_fwd(h, gpre, pp, name):
    return _ew(lambda h, g, q: h + jax.nn.sigmoid(g) * q, [h, gpre, pp], [], [(h.shape[1], F32)], name=name)


def _ple_bwd(dh, gpre, pp, name):
    def fn(dh, g, q):
        sg = jax.nn.sigmoid(g)
        return dh * q * (sg * (1.0 - sg)), dh * sg
    return _ew(fn, [dh, gpre, pp], [], [(dh.shape[1], BF16), (dh.shape[1], BF16)], name=name)


def _rope(x1, x2, cos, sin, name):
    c = x1.shape[1]
    return _ew(lambda a, b, co, si: (a * co - b * si, a * si + b * co), [x1, x2, cos, sin], [], [(c, F32), (c, F32)], name=name)


def _logsig_fwd(f, b, name):
    def fn(f, b):
        z = f + b
        return jnp.minimum(z, 0.0) - jnp.log(1.0 + jnp.exp(-jnp.abs(z)))
    return _ew(fn, [f], [b], [(f.shape[1], F32)], name=name)


def _logsig_bwd(dlogf, f, b, name):
    def fn(d, f, b):
        df = d * jax.nn.sigmoid(-(f + b))
        return df, jnp.sum(df, axis=0, keepdims=True)
    return _ew(fn, [dlogf, f], [b], [(f.shape[1], F32)], [f.shape[1]], name=name)


def _final_fwd_bwd(h, w, target, name):
    d = h.shape[1]

    def fn(h, t, w):
        r = lax.rsqrt(jnp.mean(h * h, axis=-1, keepdims=True) + RMS_EPS)
        xh = h * r
        y = xh * w
        err = y - t
        dy = err * (1.0 / d)
        gw = dy * w
        dx = r * (gw - xh * jnp.mean(gw * xh, axis=-1, keepdims=True))
        return dx, jnp.sum(dy * xh, axis=0, keepdims=True), jnp.sum(err * err, axis=0, keepdims=True) * (0.5 / d)
    return _ew(fn, [h, target], [w], [(d, F32)], [d, d], name=name)


def _adamw(w, g, m, v, name):
    shape = w.shape
    c = shape[-1]
    w2, g2, m2, v2 = (a.reshape(-1, c) for a in (w, g, m, v))

    def fn(w, g, m, v):
        m = ADAM_B1 * m + (1.0 - ADAM_B1) * g
        v = ADAM_B2 * v + (1.0 - ADAM_B2) * jnp.square(g)
        m_hat = m / (1.0 - ADAM_B1 ** ADAM_STEP)
        v_hat = v / (1.0 - ADAM_B2 ** ADAM_STEP)
        delta = -ADAM_LR * (m_hat / (jnp.sqrt(v_hat) + ADAM_EPS) + ADAM_WD * w)
        return delta, m, v
    d, nm, nv = _ew(fn, [w2, g2, m2, v2], [], [(c, F32)] * 3, name=name)
    return d.reshape(shape), nm.reshape(shape), nv.reshape(shape)


def _split3(v):
    hi = v.astype(BF16)
    r1 = v - hi.astype(F32)
    mid = r1.astype(BF16)
    lo = (r1 - mid.astype(F32)).astype(BF16)
    return hi, mid, lo


def _cumsum(x, reverse, name):
    S, C = x.shape
    tm = _divisor(S, 512, 16)
    nt = S // tm

    def body(x_ref, o_ref, carry):
        @pl.when(pl.program_id(0) == 0)
        def _():
            carry[...] = jnp.zeros_like(carry)
        r = lax.broadcasted_iota(jnp.int32, (tm, tm), 0)
        c = lax.broadcasted_iota(jnp.int32, (tm, tm), 1)
        tri = jnp.where((c >= r) if reverse else (c <= r), 1.0, 0.0).astype(BF16)
        xv = x_ref[...]
        acc = jnp.zeros((tm, C), F32)
        for part in _split3(xv):
            acc = acc + jnp.dot(tri, part, preferred_element_type=F32)
        o_ref[...] = acc + carry[...]
        carry[...] += jnp.sum(xv, axis=0, keepdims=True)

    idx = (lambda i: (nt - 1 - i, 0)) if reverse else (lambda i: (i, 0))
    return pl.pallas_call(
        body, name=name, grid=(nt,), in_specs=[pl.BlockSpec((tm, C), idx)], out_specs=pl.BlockSpec((tm, C), idx),
        out_shape=jax.ShapeDtypeStruct((S, C), F32), scratch_shapes=[pltpu.VMEM((1, C), F32)],
    )(x)


NN = (((1,), (0,)), ((), ()))
NT = (((1,), (1,)), ((), ()))
TN = (((0,), (0,)), ((), ()))

HBM_SPEC = pl.BlockSpec(memory_space=pl.ANY)


def _job_in_body(job, refs, n_in, n_out, n_scr, grid):
    if job is None:
        return refs[n_in:], lambda: None
    ji, jo = len(job["ins"]), len(job["outs"])
    j_in = refs[n_in: n_in + ji]
    pos = n_in + ji
    own = list(refs[pos: pos + n_out])
    pos += n_out
    j_out = refs[pos: pos + jo]
    pos += jo
    own += list(refs[pos: pos + n_scr])
    ss, rs = refs[-2], refs[-1]
    first = functools.reduce(jnp.logical_and, [pl.program_id(d) == 0 for d in range(len(grid))])
    last = functools.reduce(jnp.logical_and, [pl.program_id(d) == n - 1 for d, n in enumerate(grid)])

    @pl.when(first)
    def _():
        job["start"](j_in, j_out, ss, rs)

    def finish():
        @pl.when(last)
        def _():
            job["finish"](j_in, j_out, ss, rs)

    return own, finish


def _job_call(job, body, *, name, grid, in_specs, out_specs, out_shape, args, scratch_shapes, aliases, dimension_semantics):
    in_specs, out_specs, out_shape, args, scratch_shapes = list(in_specs), list(out_specs), list(out_shape), list(args), list(scratch_shapes)
    aliases = dict(aliases)
    if job is not None:
        for i, o in job["aliases"].items():
            aliases[len(args) + i] = len(out_shape) + o
        in_specs += [HBM_SPEC] * len(job["ins"])
        args += list(job["ins"])
        out_specs += [HBM_SPEC] * len(job["outs"])
        out_shape += list(job["outs"])
        scratch_shapes += [pltpu.SemaphoreType.DMA((job["n_sems"],)), pltpu.SemaphoreType.DMA((job["n_sems"],))]
    return pl.pallas_call(
        body, name=name, grid=grid, in_specs=in_specs, out_specs=out_specs, out_shape=out_shape,
        scratch_shapes=scratch_shapes, input_output_aliases=aliases,
        compiler_params=pltpu.CompilerParams(dimension_semantics=dimension_semantics),
    )(*args)


def _comm_call(job, name):
    def body(*refs):
        ji, jo = len(job["ins"]), len(job["outs"])
        job["start"](refs[:ji], refs[ji: ji + jo], refs[-2], refs[-1])
        job["finish"](refs[:ji], refs[ji: ji + jo], refs[-2], refs[-1])

    return pl.pallas_call(
        body, name=name, in_specs=[HBM_SPEC] * len(job["ins"]), out_specs=[HBM_SPEC] * len(job["outs"]), out_shape=list(job["outs"]),
        input_output_aliases=dict(job["aliases"]),
        scratch_shapes=[pltpu.SemaphoreType.DMA((job["n_sems"],)), pltpu.SemaphoreType.DMA((job["n_sems"],))],
    )(*job["ins"])


def _mm_call(name, grid, k_axis, a, a_spec, a2d, b, b_spec, b2d, dims, out_sds, out_spec, o2d, *,
             alpha=1.0, res=None, res_spec=None, into=None, job=None):
    nk = grid[k_axis]
    n_in = 2 + (res is not None) + (into is not None)

    def body(*refs):
        a_ref, b_ref = refs[0], refs[1]
        res_ref = refs[2] if res is not None else None
        (o_ref, acc_ref), finish_job = _job_in_body(job, refs, n_in, 1, 1, grid)
        k = pl.program_id(k_axis)

        @pl.when(k == 0)
        def _():
            acc_ref[...] = jnp.zeros_like(acc_ref)

        av = a_ref[...].reshape(a2d).astype(BF16)
        bv = b_ref[...].reshape(b2d).astype(BF16)
        acc_ref[...] += lax.dot_general(av, bv, dims, preferred_element_type=F32)

        @pl.when(k == nk - 1)
        def _():
            r = acc_ref[...]
            if alpha != 1.0:
                r = r * alpha
            if res_ref is not None:
                r = res_ref[...].reshape(o2d) + r
            o_ref[...] = r.reshape(o_ref.shape).astype(o_ref.dtype)

        finish_job()

    in_specs, args = [a_spec, b_spec], [a, b]
    if res is not None:
        in_specs.append(res_spec)
        args.append(res)
    aliases = {}
    if into is not None:
        aliases = {len(args): 0}
        in_specs.append(pl.BlockSpec(memory_space=pl.ANY))
        args.append(into)
        out_sds = jax.ShapeDtypeStruct(into.shape, into.dtype)
    sem = tuple("arbitrary" if d == k_axis or job is not None else "parallel" for d in range(len(grid)))
    res_all = _job_call(
        job, body, name=name, grid=grid, in_specs=in_specs, out_specs=[out_spec], out_shape=[out_sds], args=args,
        scratch_shapes=[pltpu.VMEM(o2d, F32)], aliases=aliases, dimension_semantics=sem)
    return res_all[0] if job is None else (res_all[0], res_all[1:])


def _mm(a, b, *, ta=False, tb=False, out=F32, res=None, alpha=1.0, name):
    K, M = a.shape if ta else a.shape[::-1]
    N = b.shape[0] if tb else b.shape[1]
    assert (b.shape[1] if tb else b.shape[0]) == K, (a.shape, b.shape, ta, tb)
    tk = _divisor(K, 1024, LANES)
    tn = _divisor(N, 1408, LANES)
    for cap in (1024, 512, 256, 128):
        tm = _divisor(M, cap, LANES if ta else 16)
        est = 2 * (tm * tk * a.dtype.itemsize + tk * tn * b.dtype.itemsize + tm * tn * jnp.dtype(out).itemsize)
        est += tm * tn * 4 + (2 * tm * tn * 4 if res is not None else 0)
        if est <= MM_VMEM_BYTES:
            break
    a_spec = pl.BlockSpec((tk, tm), lambda i, j, k: (k, i)) if ta else pl.BlockSpec((tm, tk), lambda i, j, k: (i, k))
    b_spec = pl.BlockSpec((tn, tk), lambda i, j, k: (j, k)) if tb else pl.BlockSpec((tk, tn), lambda i, j, k: (k, j))
    o_spec = pl.BlockSpec((tm, tn), lambda i, j, k: (i, j))
    dims = (((0 if ta else 1,), (1 if tb else 0,)), ((), ()))
    return _mm_call(name, (M // tm, N // tn, K // tk), 2, a, a_spec, (tk, tm) if ta else (tm, tk), b, b_spec,
                    (tn, tk) if tb else (tk, tn), dims, jax.ShapeDtypeStruct((M, N), out), o_spec, (tm, tn),
                    alpha=alpha, res=res, res_spec=o_spec)


def _w128_spec(blk):
    return pl.BlockSpec((N_DEV, 128, D_MODEL), lambda *_: (0, blk, 0))


def _mm_w128(a, G1, blk, *, tb=False, res=None, out=F32, name):
    S = a.shape[0]
    tm = _divisor(S, 512, 16)
    row = pl.BlockSpec((tm, D_MODEL), lambda i, k: (i, 0))
    return _mm_call(name, (S // tm, 1), 1, a, row, (tm, D_MODEL), G1, _w128_spec(blk), (D_MODEL, D_MODEL), NT if tb else NN,
                    jax.ShapeDtypeStruct((S, D_MODEL), out), row, (tm, D_MODEL), res=res, res_spec=row)


def _mm_w128_dw(a, b, blk, into, name):
    S = a.shape[0]
    tk = _divisor(S, 1024, 16)
    row = pl.BlockSpec((tk, D_MODEL), lambda i, k: (k, 0))
    return _mm_call(name, (1, S // tk), 1, a, row, (tk, D_MODEL), b, row, (tk, D_MODEL), TN, None, _w128_spec(blk),
                    (D_MODEL, D_MODEL), into=into)


def _ffn_gate_up(h, norm_w, G2v, rb, name, job=None):
    S = h.shape[0]
    tm = _divisor(S, 1024, 16)
    grid = (S // tm, 4)

    def body(*refs):
        h_ref, nw_ref, w_ref = refs[:3]
        (n_ref, gu_ref, act_ref, n_scr), finish_job = _job_in_body(job, refs, 3, 3, 1, grid)

        @pl.when(pl.program_id(1) == 0)
        def _():
            x = h_ref[...]
            y = x * lax.rsqrt(jnp.mean(x * x, axis=-1, keepdims=True) + RMS_EPS)
            n_scr[...] = (y * nw_ref[...]).astype(BF16)
            n_ref[...] = n_scr[...]

        nv = n_scr[...]
        g = jnp.dot(nv, w_ref[0, 0], preferred_element_type=F32)
        u = jnp.dot(nv, w_ref[1, 0], preferred_element_type=F32)
        gu_ref[0, 0] = g.astype(BF16)
        gu_ref[1, 0] = u.astype(BF16)
        act_ref[0] = (g * jax.nn.sigmoid(g) * u).astype(BF16)
        finish_job()

    row = pl.BlockSpec((tm, D_MODEL), lambda i, j: (i, 0))
    return _job_call(
        job, body, name=name, grid=grid,
        in_specs=[row, pl.BlockSpec((1, D_MODEL), lambda i, j: (0, 0)), pl.BlockSpec((2, 1, D_MODEL, FF_BLK), lambda i, j: (0, j, rb, 0))],
        out_specs=[row, pl.BlockSpec((2, 1, tm, FF_BLK), lambda i, j: (0, j, i, 0)), pl.BlockSpec((1, tm, FF_BLK), lambda i, j: (j, i, 0))],
        out_shape=[jax.ShapeDtypeStruct((S, D_MODEL), BF16), jax.ShapeDtypeStruct((2, 4, S, FF_BLK), BF16), jax.ShapeDtypeStruct((4, S, FF_BLK), BF16)],
        args=[h, norm_w, G2v], scratch_shapes=[pltpu.VMEM((tm, D_MODEL), BF16)], aliases={},
        dimension_semantics=("arbitrary" if job is not None else "parallel", "arbitrary"))


def _ffn_down(act, G1, ob, h, name, job=None):
    S = h.shape[0]
    tm = _divisor(S, 512, 16)
    row = pl.BlockSpec((tm, D_MODEL), lambda i, k: (i, 0))
    return _mm_call(name, (S // tm, 4), 1, act, pl.BlockSpec((1, tm, FF_BLK), lambda i, k: (k, i, 0)), (tm, FF_BLK),
                    G1, pl.BlockSpec((2, DOWN_ROWS, D_MODEL), lambda i, k: (k, ob, 0)), (FF_BLK, D_MODEL), NN,
                    jax.ShapeDtypeStruct((S, D_MODEL), F32), row, (tm, D_MODEL), alpha=0.5, res=h, res_spec=row, job=job)


def _ffn_down_dx(dh, G1, ob, gu, name):
    S = dh.shape[0]
    tm = _divisor(S, 512, 16)

    def body(dh_ref, w_ref, gu_ref, o_ref):
        w = w_ref[...].reshape(FF_BLK, D_MODEL)
        dact = lax.dot_general(dh_ref[...].astype(BF16), w, NT, preferred_element_type=F32) * 0.5
        g = gu_ref[0, 0].astype(F32)
        u = gu_ref[1, 0].astype(F32)
        sg = jax.nn.sigmoid(g)
        o_ref[0, 0] = (dact * u * (sg * (1.0 + g * (1.0 - sg)))).astype(BF16)
        o_ref[1, 0] = (dact * (g * sg)).astype(BF16)

    blk = pl.BlockSpec((2, 1, tm, FF_BLK), lambda j, i: (0, j, i, 0))
    return pl.pallas_call(
        body, name=name, grid=(4, S // tm),
        in_specs=[pl.BlockSpec((tm, D_MODEL), lambda j, i: (i, 0)), pl.BlockSpec((2, DOWN_ROWS, D_MODEL), lambda j, i: (j, ob, 0)), blk],
        out_specs=blk, out_shape=jax.ShapeDtypeStruct((2, 4, S, FF_BLK), BF16),
    )(dh, G1, gu)


def _ffn_down_dw(act, dh, ob, into, name, job=None):
    S = dh.shape[0]
    tk = _divisor(S, 1024, 16)
    return _mm_call(name, (4, S // tk), 1, act, pl.BlockSpec((1, tk, FF_BLK), lambda j, k: (j, k, 0)), (tk, FF_BLK),
                    dh, pl.BlockSpec((tk, D_MODEL), lambda j, k: (k, 0)), (tk, D_MODEL), TN, None,
                    pl.BlockSpec((2, DOWN_ROWS, D_MODEL), lambda j, k: (j, ob, 0)), (FF_BLK, D_MODEL), alpha=0.5, into=into, job=job)


def _ffn_gate_up_dw(n, dgu8, rb, into, name, job=None):
    S = n.shape[0]
    tk = _divisor(S, 1024, 16)
    return _mm_call(name, (N_DEV, S // tk), 1, n, pl.BlockSpec((tk, D_MODEL), lambda b, k: (k, 0)), (tk, D_MODEL),
                    dgu8, pl.BlockSpec((1, tk, FF_BLK), lambda b, k: (b, k, 0)), (tk, FF_BLK), TN, None,
                    pl.BlockSpec((1, D_MODEL, FF_BLK), lambda b, k: (b, rb, 0)), (D_MODEL, FF_BLK), into=into, job=job)


def _ffn_gate_up_dx(dgu8, G2, rb, h, norm_w, dres, name):
    S = h.shape[0]
    tm = _divisor(S, 512, 16)

    def body(a_ref, w_ref, h_ref, nw_ref, r_ref, o_ref, dw_ref, acc_ref):
        i, k = pl.program_id(0), pl.program_id(1)

        @pl.when(k == 0)
        def _():
            acc_ref[...] = jnp.zeros_like(acc_ref)

        @pl.when((k == 0) & (i == 0))
        def _():
            dw_ref[...] = jnp.zeros_like(dw_ref)

        acc_ref[...] += lax.dot_general(a_ref[0], w_ref[0], NT, preferred_element_type=F32)

        @pl.when(k == N_DEV - 1)
        def _():
            dn = acc_ref[...]
            x = h_ref[...]
            r = lax.rsqrt(jnp.mean(x * x, axis=-1, keepdims=True) + RMS_EPS)
            xh = x * r
            gw = dn * nw_ref[...]
            o_ref[...] = r_ref[...] + r * (gw - xh * jnp.mean(gw * xh, axis=-1, keepdims=True))
            dw_ref[...] += jnp.sum(dn * xh, axis=0, keepdims=True)

    row = pl.BlockSpec((tm, D_MODEL), lambda i, k: (i, 0))
    vec = pl.BlockSpec((1, D_MODEL), lambda i, k: (0, 0))
    return pl.pallas_call(
        body, name=name, grid=(S // tm, N_DEV),
        in_specs=[pl.BlockSpec((1, tm, FF_BLK), lambda i, k: (k, i, 0)), pl.BlockSpec((1, D_MODEL, FF_BLK), lambda i, k: (k, rb, 0)), row, vec, row],
        out_specs=[row, vec], out_shape=[jax.ShapeDtypeStruct((S, D_MODEL), F32), jax.ShapeDtypeStruct((1, D_MODEL), F32)],
        scratch_shapes=[pltpu.VMEM((tm, D_MODEL), F32)],
        compiler_params=pltpu.CompilerParams(dimension_semantics=("arbitrary", "arbitrary")),
    )(dgu8, G2, h, norm_w, dres)


def _unheads(x):
    h, S, d = x.shape
    return jnp.transpose(x, (1, 0, 2)).reshape(S, h * d)


def _exact3(v):
    rnd = lambda a: lax.reduce_precision(a, exponent_bits=8, mantissa_bits=7)
    hi = rnd(v)
    mid = rnd(v - hi)
    return hi, mid, rnd(v - hi - mid)


def _causal_mask(st, i, j, tq, tk, window):
    dist = (i * tq + lax.broadcasted_iota(jnp.int32, (tk, tq), 1)) - (j * tk + lax.broadcasted_iota(jnp.int32, (tk, tq), 0))
    mask = dist >= 0
    if window is not None:
        mask = mask & (dist < window)
    return jnp.where(mask, st, NEG)


def _attn_fwd(qT, k, vT1, *, tile, hb, window=None, sink=None, name):
    H, dqk, S = qT.shape
    G = H // k.shape[0]
    dvp = vT1.shape[1]
    dv = dvp - 16
    tq = tk = tile
    assert H % hb == 0 and (G == 1 or G % hb == 0)
    kvb = hb if G == 1 else 1

    def body(*refs):
        q_ref, k_ref, v_ref = refs[:3]
        o_ref, lse_ref = refs[-2], refs[-1]
        i = pl.program_id(1)
        carry = []
        for a in range(hb):
            if sink is not None:
                carry.append(jnp.zeros((1, tq), F32) + refs[3][a, :, 0:1])
                carry.append(jnp.where(lax.broadcasted_iota(jnp.int32, (dvp, tq), 0) == dv, 1.0, 0.0))
            else:
                carry.append(jnp.full((1, tq), NEG, F32))
                carry.append(jnp.zeros((dvp, tq), F32))

        def step(j, carry, masked):
            off = pl.multiple_of(j * tk, tk)
            out = []
            for a in range(hb):
                m, acc = carry[2 * a], carry[2 * a + 1]
                kv = a if kvb > 1 else 0
                st = jnp.dot(k_ref[kv, pl.ds(off, tk), :], q_ref[a], preferred_element_type=F32)
                if masked:
                    st = _causal_mask(st, i, j, tq, tk, window)
                m_new = jnp.maximum(m, jnp.max(st, axis=0, keepdims=True))
                pt = jnp.exp(st - m_new).astype(BF16)
                acc = jnp.exp(m - m_new) * acc + jnp.dot(v_ref[kv, :, pl.ds(off, tk)], pt, preferred_element_type=F32)
                out += [m_new, acc]
            return tuple(out)

        carry = tuple(carry)
        if window is None:
            carry = lax.fori_loop(0, i, functools.partial(step, masked=False), carry)
            carry = step(i, carry, True)
        else:
            lo = jnp.maximum((i * tq - (window - 1)) // tk, 0)
            carry = lax.fori_loop(lo, i + 1, functools.partial(step, masked=True), carry)
        for a in range(hb):
            m, acc = carry[2 * a], carry[2 * a + 1]
            l = acc[dv:dv + 1, :]
            o_ref[a] = acc[:dv, :] / l
            lse_ref[a] = m + jnp.log(l)

    kv_idx = (lambda b: b) if G == 1 else (lambda b: (b * hb) // G)
    in_specs = [
        pl.BlockSpec((hb, dqk, tq), lambda b, i: (b, 0, i)),
        pl.BlockSpec((kvb, S, dqk), lambda b, i: (kv_idx(b), 0, 0)),
        pl.BlockSpec((kvb, dvp, S), lambda b, i: (kv_idx(b), 0, 0)),
    ]
    args = [qT, k, vT1]
    if sink is not None:
        in_specs += [pl.BlockSpec((hb, 1, LANES), lambda b, i: (b, 0, 0))]
        args += [sink]
    return pl.pallas_call(
        body, name=name, grid=(H // hb, S // tq), in_specs=in_specs,
        out_specs=[pl.BlockSpec((hb, dv, tq), lambda b, i: (b, 0, i)), pl.BlockSpec((hb, 1, tq), lambda b, i: (b, 0, i))],
        out_shape=[jax.ShapeDtypeStruct((H, dv, S), F32), jax.ShapeDtypeStruct((H, 1, S), F32)],
    )(*args)


def _attn_bwd(q, qT, k, kT, v, oT, do, doT, lse, *, tile, hb, window=None, sink=None, real=None, extra=False, name):
    H, S, dqk = q.shape
    G = H // k.shape[0]
    dv = v.shape[2]
    tq = tk = tile
    nq = S // tq
    has_p = sink is not None
    real = dqk if real is None else real
    assert H % hb == 0 and (G == 1 or G % hb == 0) and not (extra and real == dqk)
    kvb = hb if G == 1 else 1

    def body(*refs):
        q_ref, qT_ref, k_ref, kT_ref, v_ref, oT_ref, do_ref, doT_ref, lse_ref = refs[:9]
        p_ref = refs[9] if has_p else None
        pos = 10 if has_p else 9
        dq_ref, dk_ref, dv_ref = refs[pos: pos + 3]
        pos += 3
        ds_ref = refs[pos] if has_p else None
        pos += has_p
        dqx_ref, dkx_ref = (refs[pos], refs[pos + 1]) if extra else (None, None)
        delta = refs[-1]
        j = pl.program_id(1)

        @pl.when(j == 0)
        def _():
            dq_ref[...] = jnp.zeros_like(dq_ref)
            if extra:
                dqx_ref[...] = jnp.zeros_like(dqx_ref)
            for a in range(hb):
                drow = jnp.sum(doT_ref[a].astype(F32) * oT_ref[a], axis=0, keepdims=True)
                delta[a] = drow
                if has_p:
                    w = jnp.exp(p_ref[a, :, 0:1] - lse_ref[a])
                    ds_ref[a] = jnp.zeros((1, LANES), F32) - jnp.sum(w * drow, axis=1, keepdims=True)

        def step(i, carry, masked):
            off = pl.multiple_of(i * tq, tq)
            out = []
            for a in range(hb):
                dk, dvv = carry[2 * a], carry[2 * a + 1]
                kv = a if kvb > 1 else 0
                st = jnp.dot(k_ref[kv], qT_ref[a, :, pl.ds(off, tq)], preferred_element_type=F32)
                if masked:
                    st = _causal_mask(st, i, j, tq, tk, window)
                pt = jnp.exp(st - lse_ref[a, :, pl.ds(off, tq)])
                dvv = dvv + jnp.dot(pt.astype(BF16), do_ref[a, pl.ds(off, tq), :], preferred_element_type=F32)
                dpt = jnp.dot(v_ref[kv], doT_ref[a, :, pl.ds(off, tq)], preferred_element_type=F32)
                dsb = (pt * (dpt - delta[a, :, pl.ds(off, tq)])).astype(BF16)
                dk = dk + jnp.dot(dsb, q_ref[a, pl.ds(off, tq), :], preferred_element_type=F32)
                dqt = jnp.dot(kT_ref[kv], dsb, preferred_element_type=F32)
                dq_ref[a, :, pl.ds(off, tq)] += dqt[:real]
                if extra:
                    dqx_ref[a, :, pl.ds(off, tq)] += dqt[real:]
                out += [dk, dvv]
            return tuple(out)

        carry = (jnp.zeros((tk, dqk), F32), jnp.zeros((tk, dv), F32)) * hb
        if window is None:
            carry = step(j, carry, True)
            carry = lax.fori_loop(j + 1, nq, functools.partial(step, masked=False), carry)
        else:
            hi = jnp.minimum(nq - 1, ((j + 1) * tk + window - 2) // tq)
            carry = lax.fori_loop(j, hi + 1, functools.partial(step, masked=True), carry)
        for a in range(hb):
            dk_ref[a] = carry[2 * a][:, :real]
            if extra:
                dkx_ref[a] = carry[2 * a][:, real:]
            dv_ref[a] = carry[2 * a + 1]

    kv_idx = (lambda b: b) if G == 1 else (lambda b: (b * hb) // G)
    rows = lambda d: pl.BlockSpec((hb, S, d), lambda b, j: (b, 0, 0))
    colsT = lambda d: pl.BlockSpec((hb, d, S), lambda b, j: (b, 0, 0))
    in_specs = [
        rows(dqk), colsT(dqk),
        pl.BlockSpec((kvb, tk, dqk), lambda b, j: (kv_idx(b), j, 0)),
        pl.BlockSpec((kvb, dqk, tk), lambda b, j: (kv_idx(b), 0, j)),
        pl.BlockSpec((kvb, tk, dv), lambda b, j: (kv_idx(b), j, 0)),
        colsT(dv), rows(dv), colsT(dv),
        pl.BlockSpec((hb, 1, S), lambda b, j: (b, 0, 0)),
    ]
    args = [q, qT, k, kT, v, oT, do, doT, lse]
    if has_p:
        in_specs += [pl.BlockSpec((hb, 1, LANES), lambda b, j: (b, 0, 0))]
        args += [sink]
    out_specs = [colsT(real), pl.BlockSpec((hb, tk, real), lambda b, j: (b, j, 0)), pl.BlockSpec((hb, tk, dv), lambda b, j: (b, j, 0))]
    out_shape = [jax.ShapeDtypeStruct((H, real, S), F32), jax.ShapeDtypeStruct((H, S, real), F32), jax.ShapeDtypeStruct((H, S, dv), F32)]
    if has_p:
        out_specs += [pl.BlockSpec((hb, 1, LANES), lambda b, j: (b, 0, 0))]
        out_shape += [jax.ShapeDtypeStruct((H, 1, LANES), F32)]
    if extra:
        out_specs += [colsT(dqk - real), pl.BlockSpec((hb, tk, dqk - real), lambda b, j: (b, j, 0))]
        out_shape += [jax.ShapeDtypeStruct((H, dqk - real, S), F32), jax.ShapeDtypeStruct((H, S, dqk - real), F32)]
    return pl.pallas_call(
        body, name=name, grid=(H // hb, S // tk), in_specs=in_specs, out_specs=out_specs, out_shape=out_shape,
        scratch_shapes=[pltpu.VMEM((hb, 1, S), F32)],
        compiler_params=pltpu.CompilerParams(dimension_semantics=("parallel", "arbitrary")),
    )(*args)


def _rows_and_cols(x3):
    xb = x3.astype(BF16)
    return jnp.transpose(xb, (1, 0, 2)), jnp.transpose(xb, (1, 2, 0))


def _v_with_ones(v3):
    S, h, _ = v3.shape
    vT = jnp.transpose(v3.astype(BF16), (1, 2, 0))
    return jnp.concatenate([vT, jnp.ones((h, 1, S), BF16), jnp.zeros((h, 15, S), BF16)], axis=1)


def _from_T(oT):
    h, d, S = oT.shape
    return jnp.transpose(oT, (2, 0, 1)).reshape(S, h * d)


def _coords():
    return lax.axis_index("x"), lax.axis_index("y"), lax.axis_index("c")


def _peer(axis):
    x, y, c = _coords()
    return {"x": (1 - x, y, c), "y": (x, 1 - y, c), "c": (x, y, 1 - c)}[axis]


def _gather_job(bufs, rows=None):
    n = len(bufs)

    def copies(outs, send_sems, recv_sems):
        x, y, c = _coords()
        me, sibling = (x, y, c), (x, y, 1 - c)
        chips = [(1 - x, y), (x, 1 - y), (1 - x, 1 - y)]

        def copy(t, k, block, to):
            px, py, pc = block
            ref = outs[t].at[4 * px + 2 * py + pc]
            if rows is not None:
                ref = ref.at[pl.ds(rows[0], rows[1])]
            return pltpu.make_async_remote_copy(ref, ref, send_sems.at[7 * t + k], recv_sems.at[7 * t + k], device_id=to, device_id_type=MESH)

        return copy, me, sibling, chips, c

    def start(ins, outs, send_sems, recv_sems):
        copy, me, sibling, chips, c = copies(outs, send_sems, recv_sems)
        for t in range(n):
            copy(t, 0, me, sibling).start()
            for j, chip in enumerate(chips):
                copy(t, 1 + j, me, (*chip, c)).start()

    def finish(ins, outs, send_sems, recv_sems):
        copy, me, sibling, chips, c = copies(outs, send_sems, recv_sems)
        for j, chip in enumerate(chips):
            for t in range(n):
                copy(t, 1 + j, (*chip, c), me).wait_recv()
                copy(t, 4 + j, (*chip, c), sibling).start()
        for t in range(n):
            copy(t, 0, sibling, me).wait_recv()
            for j, chip in enumerate(chips):
                copy(t, 4 + j, (*chip, 1 - c), me).wait_recv()
        for t in range(n):
            copy(t, 0, me, sibling).wait_send()
            for j, chip in enumerate(chips):
                copy(t, 1 + j, me, (*chip, c)).wait_send()
                copy(t, 4 + j, (*chip, c), sibling).wait_send()

    return dict(ins=list(bufs), outs=[jax.ShapeDtypeStruct(b.shape, b.dtype) for b in bufs], aliases={t: t for t in range(n)},
                n_sems=7 * n, start=start, finish=finish)


def _in_slot(local):
    x, y, c = _coords()
    buf = jnp.zeros((N_DEV,) + local.shape, local.dtype)
    return lax.dynamic_update_slice(buf, local[None], (4 * x + 2 * y + c, 0, 0))


def _pair_job(vs, axes):
    n = len(vs)
    axes = [axes] * n if isinstance(axes, str) else axes

    def copies(ins, outs, send_sems, recv_sems):
        out = []
        for t in range(n):
            me = lax.axis_index(axes[t])
            src = ins[t].at[1 - me] if len(ins[t].shape) == 3 else ins[t].at[:, 1 - me]
            out.append(pltpu.make_async_remote_copy(src, outs[t], send_sems.at[t], recv_sems.at[t], device_id=_peer(axes[t]), device_id_type=MESH))
        return out

    def start(*refs):
        for cp in copies(*refs):
            cp.start()

    def finish(*refs):
        for cp in copies(*refs):
            cp.wait()

    return dict(ins=list(vs), outs=[jax.ShapeDtypeStruct(v.shape[:-3] + v.shape[-2:], v.dtype) for v in vs], aliases={}, n_sems=n,
                start=start, finish=finish)


def _add_kept(v, got, axis, out, name):
    R, C = v.shape[-2:]
    lead = v.shape[0] if v.ndim == 4 else 1
    tm = _divisor(R, max(16, EW_TILE_BYTES // (_lanes(C) * (v.dtype.itemsize + got.dtype.itemsize + jnp.dtype(out).itemsize)) // 16 * 16), 16)
    me = lax.axis_index(axis).astype(jnp.int32).reshape(1)
    v4 = v.reshape(lead, 2, R, C)
    g3 = got.reshape(lead, R, C)

    def body(me_ref, v_ref, g_ref, o_ref):
        o_ref[...] = (v_ref[0].astype(F32) + g_ref[...].astype(F32)).astype(o_ref.dtype)

    res = pl.pallas_call(
        body, name=name, out_shape=jax.ShapeDtypeStruct((lead, R, C), out),
        grid_spec=pltpu.PrefetchScalarGridSpec(
            num_scalar_prefetch=1, grid=(lead, R // tm),
            in_specs=[pl.BlockSpec((1, 1, tm, C), lambda b, i, me: (b, me[0], i, 0)), pl.BlockSpec((1, tm, C), lambda b, i, me: (b, i, 0))],
            out_specs=pl.BlockSpec((1, tm, C), lambda b, i, me: (b, i, 0))),
    )(me, v4, g3)
    return res


def _cross_job(vs):
    n = len(vs)

    def copies(ins, outs, send_sems, recv_sems):
        x, y, _ = _coords()
        out = []
        for t in range(n):
            h = ins[t].shape[2] // 2
            out.append(pltpu.make_async_remote_copy(ins[t].at[1 - x, :, pl.ds(0, h)], outs[2 * t], send_sems.at[2 * t], recv_sems.at[2 * t],
                                                    device_id=_peer("x"), device_id_type=MESH))
            out.append(pltpu.make_async_remote_copy(ins[t].at[:, 1 - y, pl.ds(h, h)], outs[2 * t + 1], send_sems.at[2 * t + 1], recv_sems.at[2 * t + 1],
                                                    device_id=_peer("y"), device_id_type=MESH))
        return out

    def start(*refs):
        for cp in copies(*refs):
            cp.start()

    def finish(*refs):
        for cp in copies(*refs):
            cp.wait()

    outs = []
    for v in vs:
        outs += [jax.ShapeDtypeStruct((2, v.shape[2] // 2, v.shape[3]), v.dtype)] * 2
    return dict(ins=list(vs), outs=outs, aliases={}, n_sems=2 * n, start=start, finish=finish)


def _add_picked(v, got, axis, out, name):
    _, _, R, C = v.shape
    h = R // 2
    tm = _divisor(h, max(16, EW_TILE_BYTES // (_lanes(C) * (v.dtype.itemsize + got.dtype.itemsize + jnp.dtype(out).itemsize)) // 16 * 16), 16)
    me = lax.axis_index(axis).astype(jnp.int32).reshape(1)
    if axis == "x":
        v_map = lambda b, i, me: (me[0], b, i, 0)
    else:
        v_map = lambda b, i, me: (b, me[0], i + h // tm, 0)

    def body(me_ref, v_ref, g_ref, o_ref):
        o_ref[...] = (v_ref[0].astype(F32) + g_ref[...].astype(F32)).astype(o_ref.dtype)

    return pl.pallas_call(
        body, name=name, out_shape=jax.ShapeDtypeStruct((2, h, C), out),
        grid_spec=pltpu.PrefetchScalarGridSpec(
            num_scalar_prefetch=1, grid=(2, h // tm),
            in_specs=[pl.BlockSpec((1, 1, tm, C), v_map), pl.BlockSpec((1, tm, C), lambda b, i, me: (b, i, 0))],
            out_specs=pl.BlockSpec((1, tm, C), lambda b, i, me: (b, i, 0))),
    )(me, v, got)


def _reduce_scatter_steps(gs, tag):
    n = len(gs)
    vs = [g.reshape(4, 2, *g.shape[1:]) for g in gs]
    got = yield _pair_job(vs, "c")
    vs = [_add_kept(v, r, "c", BF16, f"rs_{tag}_add_c{t}") for t, (v, r) in enumerate(zip(vs, got))]
    vs = [v.reshape(2, 2, v.shape[1], v.shape[2]) for v in vs]
    got = yield _cross_job(vs)
    up = [_add_picked(v, r, "x", BF16, f"rs_{tag}_add_x{t}") for t, (v, r) in enumerate(zip(vs, got[0::2]))]
    lo = [_add_picked(v, r, "y", BF16, f"rs_{tag}_add_y{t}") for t, (v, r) in enumerate(zip(vs, got[1::2]))]
    got = yield _pair_job(up + lo, ["y"] * n + ["x"] * n)
    out = []
    for t in range(n):
        a = _add_kept(up[t], got[t], "y", F32, f"rs_{tag}_add_y2{t}")[0]
        b = _add_kept(lo[t], got[n + t], "x", F32, f"rs_{tag}_add_x2{t}")[0]
        out.append(jnp.concatenate([a, b], axis=0))
    return out


def _reduce_scatter(gs, tag):
    steps = _reduce_scatter_steps(gs, tag)
    job = next(steps)
    for stage in ("c", "xy", "yx"):
        got = _comm_call(job, f"rs_{tag}_{stage}")
        try:
            job = steps.send(got)
        except StopIteration as done:
            return done.value


def _all_reduce_small(v):
    def body(v_ref, o_ref, buf, send_sems, recv_sems):
        x, y, c = _coords()
        me = 4 * x + 2 * y + c
        buf[me] = v_ref[...]
        copies = []
        for k in range(1, N_DEV):
            peer = tuple((1 - a) if (k >> s) & 1 else a for a, s in ((x, 2), (y, 1), (c, 0)))
            cp = pltpu.make_async_remote_copy(v_ref, buf.at[me], send_sems.at[k - 1], recv_sems.at[k - 1], device_id=peer, device_id_type=MESH)
            cp.start()
            copies.append(cp)
        for cp in copies:
            cp.wait()
        acc = buf[0]
        for d in range(1, N_DEV):
            acc = acc + buf[d]
        o_ref[...] = acc

    vm = pl.BlockSpec(memory_space=pltpu.VMEM)
    return pl.pallas_call(
        body, name="all_reduce_small", in_specs=[vm], out_specs=vm, out_shape=jax.ShapeDtypeStruct(v.shape, F32),
        scratch_shapes=[pltpu.VMEM((N_DEV,) + v.shape, F32), pltpu.SemaphoreType.DMA((N_DEV - 1,)), pltpu.SemaphoreType.DMA((N_DEV - 1,))],
    )(v)


def _local_groups(w, dtype):
    mix_out = [w["ev_w_out"][0], w["od_w_out"][0]]
    layers = []
    for l in range(DEPTH):
        a = jnp.concatenate([w["ffa_w_down"][l], w["ffb_w_down"][l]], axis=0).astype(dtype)
        b = jnp.concatenate([w["ple_w_gate"][l], mix_out[l]], axis=0).astype(dtype)
        c = jnp.concatenate([w["ffa_w_gate_up"][l], w["ffb_w_gate_up"][l]], axis=0).astype(dtype)
        layers.append((a, b, c))
    strip = jnp.concatenate([w["ple_w_proj"].reshape(-1, STRIP_C), w["ev_w_ukv"][0], jnp.pad(w["ev_w_uq"][0], ((0, 0), (0, STRIP_C - 96))),
                             jnp.zeros((G3_ROWS - 896, STRIP_C), F32)], axis=0)
    m = jnp.concatenate([w["od_w_in"][0], w["ev_w_in"][0], strip, jnp.zeros((G3_ROWS, G3_COLS - STRIP0 - STRIP_C), F32)], axis=1).astype(dtype)
    return layers, m


def _ungroup_local(layers, r3):
    a, b, c = zip(*layers)
    out = {
        "ffa_w_down": jnp.stack([x[:DOWN_ROWS] for x in a]), "ffb_w_down": jnp.stack([x[DOWN_ROWS:] for x in a]),
        "ple_w_gate": jnp.stack([x[:128] for x in b]), "ev_w_out": b[0][128:][None], "od_w_out": b[1][128:][None],
        "ffa_w_gate_up": jnp.stack([x[:D_MODEL] for x in c]), "ffb_w_gate_up": jnp.stack([x[D_MODEL:] for x in c]),
        "od_w_in": r3[:, :OD_C][None], "ev_w_in": r3[:, OD_C:STRIP0][None],
    }
    strip = r3[:, STRIP0:STRIP0 + STRIP_C]
    out["ple_w_proj"] = strip[:512].reshape(2, PLE_DIM, STRIP_C)
    out["ev_w_ukv"] = strip[512:640][None]
    out["ev_w_uq"] = strip[640:896, :96][None]
    return out


def _cols(a):
    return jnp.transpose(a, (1, 0, 2)).reshape(a.shape[1], -1)


def _blocks(g, c):
    return jnp.transpose(g.reshape(g.shape[0], N_DEV, c), (1, 0, 2))


def _uq_permute(w):
    r = w.shape[0]
    w3 = w.reshape(r, B_HEADS, B_NOPE + B_ROPE)
    half = B_ROPE // 2
    return jnp.concatenate([w3[:, :, :B_NOPE].reshape(r, -1), w3[:, :, B_NOPE:B_NOPE + half].reshape(r, -1), w3[:, :, B_NOPE + half:].reshape(r, -1)], axis=1)


def _uq_unpermute(g):
    r = g.shape[0]
    half = B_ROPE // 2
    n = B_HEADS * B_NOPE
    parts = [g[:, :n].reshape(r, B_HEADS, B_NOPE), g[:, n:n + B_HEADS * half].reshape(r, B_HEADS, half), g[:, n + B_HEADS * half:].reshape(r, B_HEADS, half)]
    return jnp.concatenate(parts, axis=2).reshape(r, -1)


def _ukv_permute(w):
    r = w.shape[0]
    return jnp.transpose(w.reshape(r, B_HEADS, 2, B_NOPE), (0, 2, 1, 3)).reshape(r, -1)


def _ukv_unpermute(g):
    r = g.shape[0]
    return jnp.transpose(g.reshape(r, 2, B_HEADS, B_NOPE), (0, 2, 1, 3)).reshape(r, -1)


def _misc_weights(G3):
    strip = G3[:, :, STRIP0:STRIP0 + STRIP_C]
    return {
        "od_w_in": jnp.pad(_cols(G3[:, :, :OD_C]), ((0, 0), (0, ODD_IN_PAD - ODD_IN))),
        "ev_w_in": jnp.pad(_cols(G3[:, :, OD_C:STRIP0]), ((0, 0), (0, EVEN_IN_PAD - EVEN_IN))),
        "ple_w_proj": [_cols(strip[:, i * PLE_DIM:(i + 1) * PLE_DIM]) for i in range(DEPTH)],
        "ev_w_ukv": _ukv_permute(_cols(strip[:, 512:640])),
        "ev_w_uq": _uq_permute(_cols(strip[:, 640:896, :96])),
    }


def _misc_grads(G):
    strip = jnp.concatenate([
        _blocks(G["ple_w_proj"][0], STRIP_C), _blocks(G["ple_w_proj"][1], STRIP_C), _blocks(_ukv_unpermute(G["ev_w_ukv"]), STRIP_C),
        jnp.pad(_blocks(_uq_unpermute(G["ev_w_uq"]), 96), ((0, 0), (0, 0), (0, STRIP_C - 96))),
        jnp.zeros((N_DEV, G3_ROWS - 896, STRIP_C), F32)], axis=1)
    return jnp.concatenate([_blocks(G["od_w_in"][:, :ODD_IN], OD_C), _blocks(G["ev_w_in"][:, :EVEN_IN], EV_C), strip,
                            jnp.zeros((N_DEV, G3_ROWS, G3_COLS - STRIP0 - STRIP_C), F32)], axis=2)


def _ffn_fwd(h, norm_w, W, f, i, tag, ride=None):
    job = ride() if ride else None
    res = _ffn_gate_up(h, norm_w, W["C"][i].reshape(2, 4, C_ROWS, FF_BLK), f, f"{tag}_gate_up", job=job)
    n, gu, act = res[:3]
    if job is not None:
        ride(res[3:])
    job = ride() if ride else None
    out = _ffn_down(act, W["A"][i], f, h, f"{tag}_down", job=job)
    if job is not None:
        out, got = out
        ride(got)
    return out, (h, n, gu, act)


def _ffn_bwd(dout, saved, norm_w, W, GB, f, i, tag, ride=None):
    h, n, gu, act = saved
    S = h.shape[0]
    job = ride() if ride else None
    res = _ffn_down_dw(act, dout, f, GB["A"][i], f"{tag}_down_dw", job=job)
    if job is not None:
        res, got = res
        ride(got)
    GB["A"][i] = res
    dgu = _ffn_down_dx(dout, W["A"][i], f, gu, f"{tag}_down_dx").reshape(N_DEV, S, FF_BLK)
    job = ride() if ride else None
    res = _ffn_gate_up_dw(n, dgu, f, GB["C"][i], f"{tag}_gate_up_dw", job=job)
    if job is not None:
        res, got = res
        ride(got)
    GB["C"][i] = res
    return _ffn_gate_up_dx(dgu, W["C"][i], f, h, norm_w, dout, f"{tag}_gate_up_dx")


def _rope_tables(S):
    inv = ROPE_THETA ** (-jnp.arange(0, B_ROPE, 2, dtype=F32) / B_ROPE)
    ang = jnp.arange(S, dtype=F32)[:, None] * inv[None, :]
    return jnp.cos(ang), jnp.sin(ang)


def _alibi_columns(S):
    t = jnp.arange(S, dtype=jnp.int32)
    hi = ((t // 16) * 16).astype(F32)
    lo = (t % 16).astype(F32)
    slopes = 2.0 ** (-8.0 * jnp.arange(1, A_HEADS + 1, dtype=F32) / A_HEADS)
    zq = jnp.zeros((S, A_HEADS), F32)
    rest = QK_PAD - A_HEAD_DIM - 4
    qc = jnp.stack([-slopes[None, :] * hi[:, None], -slopes[None, :] * lo[:, None], zq + slopes[None, :], zq + slopes[None, :]] + [zq] * rest, axis=-1)
    one = jnp.ones((S, A_KV_HEADS), F32)
    zk = jnp.zeros((S, A_KV_HEADS), F32)
    kc = jnp.stack([one, one, zk + hi[:, None], zk + lo[:, None]] + [zk] * rest, axis=-1)
    return qc, kc


def _sink_prm(sinks):
    return jnp.zeros((A_HEADS, 1, LANES), F32).at[:, 0, 0].set(sinks.astype(F32))


def _even_fwd(hn, h, W):
    S = hn.shape[0]
    proj = _mm(hn, W["ev_w_in"], name="ev_in")
    a_q, a_k, a_v = proj[:, :512], proj[:, 512:640], proj[:, 640:768]
    c_q, c_kv = proj[:, 768:1024], proj[:, 1024:1152]
    kr1, kr2 = proj[:, 1152:1168], proj[:, 1168:1184]
    qc, kc = _alibi_columns(S)
    qa, qaT = _rows_and_cols(jnp.concatenate([(a_q * A_HEAD_DIM ** -0.5).reshape(S, A_HEADS, A_HEAD_DIM), qc], axis=-1))
    ka, kaT = _rows_and_cols(jnp.concatenate([a_k.reshape(S, A_KV_HEADS, A_HEAD_DIM), kc], axis=-1))
    va3 = a_v.reshape(S, A_KV_HEADS, A_HEAD_DIM)
    va = jnp.transpose(va3.astype(BF16), (1, 0, 2))
    prm = _sink_prm(W["ev_sinks"][0])
    oaT, lse_a = _attn_fwd(qaT, ka, _v_with_ones(va3), tile=SWA_TILE, hb=2, window=WINDOW, sink=prm, name="swa_fwd")
    cqn = _rms_fwd(c_q, W["ev_cq_norm"], "ev_cq_norm")
    q_all = _mm(cqn, W["ev_w_uq"], name="ev_uq")
    ckvn = _rms_fwd(c_kv, W["ev_ckv_norm"], "ev_ckv_norm")
    kv_all = _mm(ckvn, W["ev_w_ukv"], name="ev_ukv")
    cos, sin = _rope_tables(S)
    cos8, sin8 = jnp.tile(cos, (1, B_HEADS)), jnp.tile(sin, (1, B_HEADS))
    q1, q2 = _rope(q_all[:, 512:640], q_all[:, 640:768], cos8, sin8, "ev_rope_q")
    k1, k2 = _rope(kr1, kr2, cos, sin, "ev_rope_k")
    half = B_ROPE // 2
    scale = (B_NOPE + B_ROPE) ** -0.5
    qb, qbT = _rows_and_cols(jnp.concatenate([q_all[:, :512].reshape(S, B_HEADS, B_NOPE), q1.reshape(S, B_HEADS, half), q2.reshape(S, B_HEADS, half)], axis=-1) * scale)
    kro = jnp.broadcast_to(jnp.concatenate([k1, k2], axis=1)[:, None, :], (S, B_HEADS, B_ROPE))
    kb, kbT = _rows_and_cols(jnp.concatenate([kv_all[:, :512].reshape(S, B_HEADS, B_NOPE), kro], axis=-1))
    vb3 = kv_all[:, 512:].reshape(S, B_HEADS, B_V)
    vb = jnp.transpose(vb3.astype(BF16), (1, 0, 2))
    obT, lse_b = _attn_fwd(qbT, kb, _v_with_ones(vb3), tile=min(ATTN_TILE_FWD, S), hb=2, name="mla_fwd")
    cat = jnp.concatenate([_from_T(oaT), _from_T(obT)], axis=1)
    out = _mm_w128(cat, W["B"][0], MIX_OUT_BLK, res=h, name="ev_out")
    return out, (hn, proj, (qa, qaT, ka, kaT, va, oaT, lse_a), prm, cqn, ckvn, (qb, qbT, kb, kbT, vb, obT, lse_b), cat)


def _even_bwd(dout, saved, W, GB):
    hn, proj, (qa, qaT, ka, kaT, va, oaT, lse_a), prm, cqn, ckvn, (qb, qbT, kb, kbT, vb, obT, lse_b), cat = saved
    S = hn.shape[0]
    G = {}
    dcat = _mm_w128(dout, W["B"][0], MIX_OUT_BLK, tb=True, out=BF16, name="ev_out_dx")
    GB["B"][0] = _mm_w128_dw(cat, dout, MIX_OUT_BLK, GB["B"][0], "ev_out_dw")
    doa, doaT = _rows_and_cols(dcat[:, :512].reshape(S, A_HEADS, A_HEAD_DIM))
    dqaT, dka, dva, dsink = _attn_bwd(qa, qaT, ka, kaT, va, oaT, doa, doaT, lse_a, tile=SWA_TILE, hb=2, window=WINDOW, sink=prm, real=A_HEAD_DIM,
                                       name="swa_bwd")
    G["ev_sinks"] = dsink[:, 0, 0]
    dqa = _from_T(dqaT) * A_HEAD_DIM ** -0.5
    dka = dka.reshape(A_KV_HEADS, A_GROUP, S, A_HEAD_DIM).sum(axis=1)
    dva = dva.reshape(A_KV_HEADS, A_GROUP, S, A_HEAD_DIM).sum(axis=1)
    dob, dobT = _rows_and_cols(dcat[:, 512:].reshape(S, B_HEADS, B_V))
    dqbT, dkb, dvb = _attn_bwd(qb, qbT, kb, kbT, vb, obT, dob, dobT, lse_b, tile=ATTN_TILE, hb=2, name="mla_bwd")
    half = B_ROPE // 2
    dqb = jnp.transpose(dqbT, (2, 0, 1)) * (B_NOPE + B_ROPE) ** -0.5
    dkb = jnp.transpose(dkb, (1, 0, 2))
    cos, sin = _rope_tables(S)
    cos8, sin8 = jnp.tile(cos, (1, B_HEADS)), jnp.tile(sin, (1, B_HEADS))
    dq1, dq2 = _rope(dqb[:, :, B_NOPE:B_NOPE + half].reshape(S, -1), dqb[:, :, B_NOPE + half:].reshape(S, -1), cos8, -sin8, "ev_rope_q_bwd")
    dq_all = jnp.concatenate([dqb[:, :, :B_NOPE].reshape(S, -1), dq1, dq2], axis=1).astype(BF16)
    dkr = dkb[:, :, B_NOPE:].sum(axis=1)
    dk1, dk2 = _rope(dkr[:, :half], dkr[:, half:], cos, -sin, "ev_rope_k_bwd")
    dkv_all = jnp.concatenate([dkb[:, :, :B_NOPE].reshape(S, -1), _unheads(dvb)], axis=1).astype(BF16)
    G["ev_w_uq"] = _mm(cqn, dq_all, ta=True, name="ev_uq_dw")
    dcqn = _mm(dq_all, W["ev_w_uq"], tb=True, name="ev_uq_dx")
    dc_q, G["ev_cq_norm"] = _rms_bwd(dcqn, proj[:, 768:1024], W["ev_cq_norm"], None, "ev_cq_norm_bwd")
    G["ev_w_ukv"] = _mm(ckvn, dkv_all, ta=True, name="ev_ukv_dw")
    dckvn = _mm(dkv_all, W["ev_w_ukv"], tb=True, name="ev_ukv_dx")
    dc_kv, G["ev_ckv_norm"] = _rms_bwd(dckvn, proj[:, 1024:1152], W["ev_ckv_norm"], None, "ev_ckv_norm_bwd")
    dproj = jnp.concatenate([dqa, _unheads(dka), _unheads(dva), dc_q, dc_kv, dk1, dk2,
                             jnp.zeros((S, EVEN_IN_PAD - EVEN_IN), F32)], axis=1).astype(BF16)
    G["ev_w_in"] = _mm(hn, dproj, ta=True, name="ev_in_dw")
    dhn = _mm(dproj, W["ev_w_in"], tb=True, name="ev_in_dx")
    return dhn, G


def _odd_fwd(hn, h, W):
    S = hn.shape[0]
    w = C_HEADS * C_HEAD_DIM
    proj = _mm(hn, W["od_w_in"], name="od_in")
    f_logit = proj[:, 3 * w: 3 * w + C_HEADS]
    logf = _logsig_fwd(f_logit, W["od_b_f"], "od_logsig")
    logc = _cumsum(logf, False, "od_cumsum")
    parts = [p[:, :, None] for p in _exact3(logc)]
    ones = [jnp.ones((S, C_HEADS, 1), F32)] * 3
    pad = [jnp.zeros((S, C_HEADS, QK_PAD - C_HEAD_DIM - 6), F32)]
    q3 = (proj[:, :w] * C_HEAD_DIM ** -0.5).reshape(S, C_HEADS, C_HEAD_DIM)
    k3 = proj[:, w:2 * w].reshape(S, C_HEADS, C_HEAD_DIM)
    q, qT = _rows_and_cols(jnp.concatenate([q3, jnp.concatenate(parts + ones + pad, axis=-1)], axis=-1))
    k, kT = _rows_and_cols(jnp.concatenate([k3, jnp.concatenate(ones + [-p for p in parts] + pad, axis=-1)], axis=-1))
    v3 = proj[:, 2 * w:3 * w].reshape(S, C_HEADS, C_HEAD_DIM)
    v = jnp.transpose(v3.astype(BF16), (1, 0, 2))
    oT, lse = _attn_fwd(qT, k, _v_with_ones(v3), tile=min(ATTN_TILE_FWD, S), hb=2, name="fox_fwd")
    cat = _from_T(oT)
    out = _mm_w128(cat, W["B"][1], MIX_OUT_BLK, res=h, name="od_out")
    return out, (hn, q, qT, k, kT, v, f_logit, oT, lse, cat)


def _odd_bwd(dout, saved, W, GB):
    hn, q, qT, k, kT, v, f_logit, oT, lse, cat = saved
    S = hn.shape[0]
    G = {}
    dcat = _mm_w128(dout, W["B"][1], MIX_OUT_BLK, tb=True, out=BF16, name="od_out_dx")
    GB["B"][1] = _mm_w128_dw(cat, dout, MIX_OUT_BLK, GB["B"][1], "od_out_dw")
    do, doT = _rows_and_cols(dcat.reshape(S, C_HEADS, C_HEAD_DIM))
    dqT, dk, dv, dqxT, dkx = _attn_bwd(q, qT, k, kT, v, oT, do, doT, lse, tile=ATTN_TILE, hb=2, real=C_HEAD_DIM, extra=True, name="fox_bwd")
    dlogc = jnp.transpose(dqxT[:, 0, :] - dkx[:, :, 3])
    dlogf = _cumsum(dlogc, True, "od_cumsum_bwd")
    df, db = _logsig_bwd(dlogf, f_logit, W["od_b_f"], "od_logsig_bwd")
    G["od_b_f"] = db
    dproj = jnp.concatenate([_from_T(dqT) * C_HEAD_DIM ** -0.5, _unheads(dk), _unheads(dv), df,
                             jnp.zeros((S, ODD_IN_PAD - ODD_IN), F32)], axis=1).astype(BF16)
    G["od_w_in"] = _mm(hn, dproj, ta=True, name="od_in_dw")
    dhn = _mm(dproj, W["od_w_in"], tb=True, name="od_in_dx")
    return dhn, G


class _Rider:
    def __init__(self, steps, tag):
        self.steps, self.tag, self.count, self.result = steps, tag, 0, None
        self.job = next(steps)

    def __call__(self, got=None):
        if got is None:
            return self.job
        try:
            self.job = self.steps.send(list(got))
        except StopIteration as done:
            self.job, self.result = None, done.value
        return None

    def finish(self):
        while self.job is not None:
            self.count += 1
            self(_comm_call(self.job, f"{self.tag}_{self.count}"))
        return self.result


def _gather_later_steps(a, b, c):
    (c,) = yield _gather_job([c], rows=(0, D_MODEL))
    (a,) = yield _gather_job([a])
    (c,) = yield _gather_job([c], rows=(D_MODEL, D_MODEL))
    (b,) = yield _gather_job([b])
    return a, b, c


def _local_step(x, p, target, W, later):
    h = x
    saved = []
    gather = _Rider(_gather_later_steps(*later), "all_gather_later")
    for i in range(DEPTH):
        t = f"l{i}"
        ride = gather if i == 0 else None
        h1, s_a = _ffn_fwd(h, W["ffa_norm"][i:i + 1], W, 0, i, f"{t}_ffa", ride)
        nm = _rms_fwd(h1, W["mix_norm"][i:i + 1], f"{t}_mix_norm")
        h2, s_m = (_even_fwd if i % 2 == 0 else _odd_fwd)(nm, h1, W)
        h3, s_b = _ffn_fwd(h2, W["ffb_norm"][i:i + 1], W, 1, i, f"{t}_ffb", ride)
        npl = _rms_fwd(h3, W["ple_norm"][i:i + 1], f"{t}_ple_norm")
        gpre = _mm_w128(npl, W["B"][i], PLE_GATE_BLK, name=f"{t}_ple_gate")
        pp = _mm(p[i], W["ple_w_proj"][i], name=f"{t}_ple_proj")
        h4 = _ple_fwd(h3, gpre, pp, f"{t}_ple")
        saved.append((s_a, h1, s_m, s_b, h3, npl, gpre, pp))
        h = h4
        if i == 0:
            for key, val in zip("ABC", gather.finish()):
                W[key].append(val)
    dh, g_final, loss_cols = _final_fwd_bwd(h, W["final_norm"], target, "final")
    G = {"final_norm": g_final}
    GB = {"A": [lax.empty((N_DEV, A_ROWS, D_MODEL), BF16) for _ in range(DEPTH)],
          "B": [lax.empty((N_DEV, B_ROWS, D_MODEL), BF16) for _ in range(DEPTH)],
          "C": [lax.empty((N_DEV, C_ROWS, FF_BLK), BF16) for _ in range(DEPTH)]}
    per_layer = {n: [None] * DEPTH for n in ("ffa_norm", "mix_norm", "ffb_norm", "ple_norm", "ple_w_proj")}
    scatter = None
    for i in reversed(range(DEPTH)):
        t = f"l{i}"
        s_a, h1, s_m, s_b, h3, npl, gpre, pp = saved[i]
        dgpre, dpp = _ple_bwd(dh, gpre, pp, f"{t}_ple_bwd")
        per_layer["ple_w_proj"][i] = _mm(p[i], dpp, ta=True, name=f"{t}_ple_proj_dw")
        GB["B"][i] = _mm_w128_dw(npl, dgpre, PLE_GATE_BLK, GB["B"][i], f"{t}_ple_gate_dw")
        dnpl = _mm_w128(dgpre, W["B"][i], PLE_GATE_BLK, tb=True, name=f"{t}_ple_gate_dx")
        dh, per_layer["ple_norm"][i] = _rms_bwd(dnpl, h3, W["ple_norm"][i:i + 1], dh, f"{t}_ple_norm_bwd")
        dh, per_layer["ffb_norm"][i] = _ffn_bwd(dh, s_b, W["ffb_norm"][i:i + 1], W, GB, 1, i, f"{t}_ffb", scatter)
        dnm, g_mix = (_even_bwd if i % 2 == 0 else _odd_bwd)(dh, s_m, W, GB)
        G.update(g_mix)
        dh, per_layer["mix_norm"][i] = _rms_bwd(dnm, h1, W["mix_norm"][i:i + 1], dh, f"{t}_mix_norm_bwd")
        dh, per_layer["ffa_norm"][i] = _ffn_bwd(dh, s_a, W["ffa_norm"][i:i + 1], W, GB, 0, i, f"{t}_ffa", scatter)
        if i == DEPTH - 1:
            scatter = _Rider(_reduce_scatter_steps([GB[key][i] for key in "ABC"], "later"), "rs_later")
    for n in ("ffa_norm", "mix_norm", "ffb_norm", "ple_norm"):
        G[n] = jnp.concatenate(per_layer[n], axis=0)
    G["ple_w_proj"] = per_layer["ple_w_proj"]
    return loss_cols, dh, tuple(scatter.finish()), {key: GB[key][0] for key in "ABC"}, G


def kernel(x, p, ffa_norm, ffa_w_gate_up, ffa_w_down, mix_norm, ffb_norm, ffb_w_gate_up, ffb_w_down, ple_norm, ple_w_gate, ple_w_proj, ev_w_in, ev_sinks, ev_cq_norm, ev_w_uq, ev_ckv_norm, ev_w_ukv, ev_w_out, od_w_in, od_b_f, od_w_out, final_norm, loss_target, m_ffa_norm, m_ffa_w_gate_up, m_ffa_w_down, m_mix_norm, m_ffb_norm, m_ffb_w_gate_up, m_ffb_w_down, m_ple_norm, m_ple_w_gate, m_ple_w_proj, m_ev_w_in, m_ev_sinks, m_ev_cq_norm, m_ev_w_uq, m_ev_ckv_norm, m_ev_w_ukv, m_ev_w_out, m_od_w_in, m_od_b_f, m_od_w_out, m_final_norm, v_ffa_norm, v_ffa_w_gate_up, v_ffa_w_down, v_mix_norm, v_ffb_norm, v_ffb_w_gate_up, v_ffb_w_down, v_ple_norm, v_ple_w_gate, v_ple_w_proj, v_ev_w_in, v_ev_sinks, v_ev_cq_norm, v_ev_w_uq, v_ev_ckv_norm, v_ev_w_ukv, v_ev_w_out, v_od_w_in, v_od_b_f, v_od_w_out, v_final_norm):
    given = dict(locals())
    w_in = {n: given[n] for n in WEIGHTS}

    layers, misc = _local_groups(w_in, BF16)
    a0, b0, c0, m = _comm_call(_gather_job([_in_slot(g) for g in (*layers[0], misc)]), "all_gather_first")
    W = {n: w_in[n] for n in SMALL}
    W["final_norm"] = final_norm.reshape(1, -1)
    W.update(_misc_weights(m))
    W.update(A=[a0], B=[b0], C=[c0])

    loss_cols, dx, r_later, GB, G = _local_step(x[0], p[:, 0], loss_target[0], W, [_in_slot(g) for g in layers[1]])

    *r_first, r_misc = _reduce_scatter([GB["A"], GB["B"], GB["C"], _misc_grads(G).astype(BF16)], "first")
    grads = _ungroup_local([tuple(r_first), r_later], r_misc)
    layout = [(n, int(np.prod(w_in[n].shape))) for n in SMALL]
    vec = jnp.concatenate([G[n].astype(F32).reshape(-1) for n, _ in layout] + [jnp.sum(loss_cols).reshape(1)])
    vec = jnp.pad(vec, (0, N_DEV * SMALL_COLS - vec.shape[0])).reshape(N_DEV, SMALL_COLS)
    vec = _all_reduce_small(vec).reshape(-1)
    off = 0
    for n, size in layout:
        grads[n] = vec[off: off + size].reshape(w_in[n].shape)
        off += size
    loss = vec[off]

    delta, new_m, new_v = {}, {}, {}
    for n in WEIGHTS:
        shp = w_in[n].shape
        as2d = (lambda a: a.reshape(1, -1)) if len(shp) == 1 else (lambda a: a)
        d, nm, nv = _adamw(as2d(w_in[n]), as2d(grads[n]), as2d(given["m_" + n]), as2d(given["v_" + n]), f"adamw_{n}")
        delta[n], new_m[n], new_v[n] = d.reshape(shp), nm.reshape(shp), nv.reshape(shp)
    return (loss, dx[None], *[grads[n] for n in WEIGHTS], *[delta[n] for n in WEIGHTS],
            *[new_m[n] for n in WEIGHTS], *[new_v[n] for n in WEIGHTS])
```

```python
import functools

import numpy as np
import jax
import jax.numpy as jnp
from jax import lax
from jax.experimental import pallas as pl
from jax.experimental.pallas import tpu as pltpu

F32 = jnp.float32
BF16 = jnp.bfloat16
MESH = pl.DeviceIdType.MESH

D_MODEL = 1024
D_FF = 2816
RMS_EPS = 1e-6
PLE_DIM = 256
A_HEADS, A_KV_HEADS, A_HEAD_DIM, WINDOW = 8, 2, 64, 128
A_GROUP = A_HEADS // A_KV_HEADS
B_HEADS, B_Q_LORA, B_KV_LORA, B_NOPE, B_ROPE, B_V = 8, 256, 128, 64, 32, 64
ROPE_THETA = 10000.0
C_HEADS, C_HEAD_DIM = 16, 64
EVEN_IN = 1184
EVEN_IN_PAD = 1280
ODD_IN = 3088
ODD_IN_PAD = 3200
DEPTH = 2
ADAM_LR, ADAM_B1, ADAM_B2, ADAM_EPS, ADAM_WD, ADAM_STEP = 0.001, 0.9, 0.999, 1e-08, 0.01, 10

N_DEV = 8
LANES = 128
SUBLANES = 8
EW_TILE_BYTES = 3 << 20
MM_VMEM_BYTES = 26 << 20
NEG = -1e30
ATTN_TILE = 1024
ATTN_TILE_FWD = 1024
SWA_TILE = 256
QK_PAD = 80

FF_BLK = D_FF // 4
DOWN_ROWS = D_FF // N_DEV
A_ROWS, B_ROWS, C_ROWS, G3_ROWS, G3_COLS = 2 * DOWN_ROWS, 256, 2 * D_MODEL, 1024, 768
PLE_GATE_BLK, MIX_OUT_BLK = 0, 1
OD_C, EV_C, STRIP_C = 386, 148, 128
STRIP0 = OD_C + EV_C

SMALL = ["ffa_norm", "mix_norm", "ffb_norm", "ple_norm", "ev_sinks", "ev_cq_norm", "ev_ckv_norm", "od_b_f", "final_norm"]
WEIGHTS = ["ffa_norm", "ffa_w_gate_up", "ffa_w_down", "mix_norm", "ffb_norm", "ffb_w_gate_up", "ffb_w_down", "ple_norm",
           "ple_w_gate", "ple_w_proj", "ev_w_in", "ev_sinks", "ev_cq_norm", "ev_w_uq", "ev_ckv_norm", "ev_w_ukv", "ev_w_out",
           "od_w_in", "od_b_f", "od_w_out", "final_norm"]
SMALL_COLS = 1280


def _divisor(n, cap, mult):
    if n <= cap:
        return n
    for t in range(cap - cap % mult, 0, -mult):
        if n % t == 0:
            return t
    raise ValueError(f"no tile for {n} under {cap} in steps of {mult}")


def _lanes(c):
    return -(-c // LANES) * LANES


def _ew(fn, rows, vecs, outs, reds=(), *, name):
    R = rows[0].shape[0]
    per_row = sum(_lanes(a.shape[1]) * a.dtype.itemsize for a in rows) + sum(_lanes(c) * jnp.dtype(d).itemsize for c, d in outs)
    tm = _divisor(R, max(16, EW_TILE_BYTES // per_row // 16 * 16), 16) if R % 16 == 0 else R
    n_r, n_v, n_o = len(rows), len(vecs), len(outs)

    def body(*refs):
        ins = [r[...] for r in refs[: n_r + n_v]]
        res = fn(*ins)
        if not isinstance(res, (tuple, list)):
            res = (res,)
        o_refs = refs[n_r + n_v: n_r + n_v + n_o]
        r_refs = refs[n_r + n_v + n_o:]
        for ref, val in zip(o_refs, res[:n_o]):
            ref[...] = val.astype(ref.dtype)
        if r_refs:
            @pl.when(pl.program_id(0) == 0)
            def _():
                for ref in r_refs:
                    ref[...] = jnp.zeros_like(ref)
            for ref, val in zip(r_refs, res[n_o:]):
                ref[...] += val

    in_specs = [pl.BlockSpec((tm, a.shape[1]), lambda i: (i, 0)) for a in rows]
    in_specs += [pl.BlockSpec((1, a.shape[1]), lambda i: (0, 0)) for a in vecs]
    out_specs = [pl.BlockSpec((tm, c), lambda i: (i, 0)) for c, _ in outs]
    out_specs += [pl.BlockSpec((1, c), lambda i: (0, 0)) for c in reds]
    out_shape = [jax.ShapeDtypeStruct((R, c), d) for c, d in outs] + [jax.ShapeDtypeStruct((1, c), F32) for c in reds]
    res = pl.pallas_call(body, name=name, grid=(R // tm,), in_specs=in_specs, out_specs=out_specs, out_shape=out_shape)(*rows, *vecs)
    return res[0] if len(res) == 1 else res


def _rms_fwd(x, w, name):
    def fn(x, w):
        y = x * lax.rsqrt(jnp.mean(x * x, axis=-1, keepdims=True) + RMS_EPS)
        return y * w
    return _ew(fn, [x], [w], [(x.shape[1], BF16)], name=name)


def _rms_bwd(dn, x, w, dres, name):
    def fn(dn, x, *rest):
        w = rest[-1]
        r = lax.rsqrt(jnp.mean(x * x, axis=-1, keepdims=True) + RMS_EPS)
        xh = x * r
        gw = dn * w
        dx = r * (gw - xh * jnp.mean(gw * xh, axis=-1, keepdims=True))
        if len(rest) == 2:
            dx = dx + rest[0]
        return dx, jnp.sum(dn * xh, axis=0, keepdims=True)
    rows = [dn, x] + ([dres] if dres is not None else [])
    return _ew(fn, rows, [w], [(x.shape[1], F32)], [x.shape[1]], name=name)


def _ple_fwd(h, gpre, pp, name):
    return _ew(lambda h, g, q: h + jax.nn.sigmoid(g) * q, [h, gpre, pp], [], [(h.shape[1], F32)], name=name)


def _ple_bwd(dh, gpre, pp, name):
    def fn(dh, g, q):
        sg = jax.nn.sigmoid(g)
        return dh * q * (sg * (1.0 - sg)), dh * sg
    return _ew(fn, [dh, gpre, pp], [], [(dh.shape[1], BF16), (dh.shape[1], BF16)], name=name)


def _rope(x1, x2, cos, sin, name):
    c = x1.shape[1]
    return _ew(lambda a, b, co, si: (a * co - b * si, a * si + b * co), [x1, x2, cos, sin], [], [(c, F32), (c, F32)], name=name)


def _logsig_fwd(f, b, name):
    def fn(f, b):
        z = f + b
        return jnp.minimum(z, 0.0) - jnp.log(1.0 + jnp.exp(-jnp.abs(z)))
    return _ew(fn, [f], [b], [(f.shape[1], F32)], name=name)


def _logsig_bwd(dlogf, f, b, name):
    def fn(d, f, b):
        df = d * jax.nn.sigmoid(-(f + b))
        return df, jnp.sum(df, axis=0, keepdims=True)
    return _ew(fn, [dlogf, f], [b], [(f.shape[1], F32)], [f.shape[1]], name=name)


def _final_fwd_bwd(h, w, target, name):
    d = h.shape[1]

    def fn(h, t, w):
        r = lax.rsqrt(jnp.mean(h * h, axis=-1, keepdims=True) + RMS_EPS)
        xh = h * r
        y = xh * w
        err = y - t
        dy = err * (1.0 / d)
        gw = dy * w
        dx = r * (gw - xh * jnp.mean(gw * xh, axis=-1, keepdims=True))
        return dx, jnp.sum(dy * xh, axis=0, keepdims=True), jnp.sum(err * err, axis=0, keepdims=True) * (0.5 / d)
    return _ew(fn, [h, target], [w], [(d, F32)], [d, d], name=name)


def _adamw(w, g, m, v, name):
    shape = w.shape
    c = shape[-1]
    w2, g2, m2, v2 = (a.reshape(-1, c) for a in (w, g, m, v))

    def fn(w, g, m, v):
        m = ADAM_B1 * m + (1.0 - ADAM_B1) * g
        v = ADAM_B2 * v + (1.0 - ADAM_B2) * jnp.square(g)
        m_hat = m / (1.0 - ADAM_B1 ** ADAM_STEP)
        v_hat = v / (1.0 - ADAM_B2 ** ADAM_STEP)
        delta = -ADAM_LR * (m_hat / (jnp.sqrt(v_hat) + ADAM_EPS) + ADAM_WD * w)
        return delta, m, v
    d, nm, nv = _ew(fn, [w2, g2, m2, v2], [], [(c, F32)] * 3, name=name)
    return d.reshape(shape), nm.reshape(shape), nv.reshape(shape)


def _split3(v):
    hi = v.astype(BF16)
    r1 = v - hi.astype(F32)
    mid = r1.astype(BF16)
    lo = (r1 - mid.astype(F32)).astype(BF16)
    return hi, mid, lo


def _cumsum(x, reverse, name):
    S, C = x.shape
    tm = _divisor(S, 512, 16)
    nt = S // tm

    def body(x_ref, o_ref, carry):
        @pl.when(pl.program_id(0) == 0)
        def _():
            carry[...] = jnp.zeros_like(carry)
        r = lax.broadcasted_iota(jnp.int32, (tm, tm), 0)
        c = lax.broadcasted_iota(jnp.int32, (tm, tm), 1)
        tri = jnp.where((c >= r) if reverse else (c <= r), 1.0, 0.0).astype(BF16)
        xv = x_ref[...]
        acc = jnp.zeros((tm, C), F32)
        for part in _split3(xv):
            acc = acc + jnp.dot(tri, part, preferred_element_type=F32)
        o_ref[...] = acc + carry[...]
        carry[...] += jnp.sum(xv, axis=0, keepdims=True)

    idx = (lambda i: (nt - 1 - i, 0)) if reverse else (lambda i: (i, 0))
    return pl.pallas_call(
        body, name=name, grid=(nt,), in_specs=[pl.BlockSpec((tm, C), idx)], out_specs=pl.BlockSpec((tm, C), idx),
        out_shape=jax.ShapeDtypeStruct((S, C), F32), scratch_shapes=[pltpu.VMEM((1, C), F32)],
    )(x)


NN = (((1,), (0,)), ((), ()))
NT = (((1,), (1,)), ((), ()))
TN = (((0,), (0,)), ((), ()))

HBM_SPEC = pl.BlockSpec(memory_space=pl.ANY)


def _job_in_body(job, refs, n_in, n_out, n_scr, grid):
    if job is None:
        return refs[n_in:], lambda: None
    ji, jo = len(job["ins"]), len(job["outs"])
    j_in = refs[n_in: n_in + ji]
    pos = n_in + ji
    own = list(refs[pos: pos + n_out])
    pos += n_out
    j_out = refs[pos: pos + jo]
    pos += jo
    own += list(refs[pos: pos + n_scr])
    ss, rs = refs[-2], refs[-1]
    first = functools.reduce(jnp.logical_and, [pl.program_id(d) == 0 for d in range(len(grid))])
    last = functools.reduce(jnp.logical_and, [pl.program_id(d) == n - 1 for d, n in enumerate(grid)])

    @pl.when(first)
    def _():
        job["start"](j_in, j_out, ss, rs)

    def finish():
        @pl.when(last)
        def _():
            job["finish"](j_in, j_out, ss, rs)

    return own, finish


def _job_call(job, body, *, name, grid, in_specs, out_specs, out_shape, args, scratch_shapes, aliases, dimension_semantics):
    in_specs, out_specs, out_shape, args, scratch_shapes = list(in_specs), list(out_specs), list(out_shape), list(args), list(scratch_shapes)
    aliases = dict(aliases)
    if job is not None:
        for i, o in job["aliases"].items():
            aliases[len(args) + i] = len(out_shape) + o
        in_specs += [HBM_SPEC] * len(job["ins"])
        args += list(job["ins"])
        out_specs += [HBM_SPEC] * len(job["outs"])
        out_shape += list(job["outs"])
        scratch_shapes += [pltpu.SemaphoreType.DMA((job["n_sems"],)), pltpu.SemaphoreType.DMA((job["n_sems"],))]
    return pl.pallas_call(
        body, name=name, grid=grid, in_specs=in_specs, out_specs=out_specs, out_shape=out_shape,
        scratch_shapes=scratch_shapes, input_output_aliases=aliases,
        compiler_params=pltpu.CompilerParams(dimension_semantics=dimension_semantics),
    )(*args)


def _comm_call(job, name):
    def body(*refs):
        ji, jo = len(job["ins"]), len(job["outs"])
        job["start"](refs[:ji], refs[ji: ji + jo], refs[-2], refs[-1])
        job["finish"](refs[:ji], refs[ji: ji + jo], refs[-2], refs[-1])

    return pl.pallas_call(
        body, name=name, in_specs=[HBM_SPEC] * len(job["ins"]), out_specs=[HBM_SPEC] * len(job["outs"]), out_shape=list(job["outs"]),
        input_output_aliases=dict(job["aliases"]),
        scratch_shapes=[pltpu.SemaphoreType.DMA((job["n_sems"],)), pltpu.SemaphoreType.DMA((job["n_sems"],))],
    )(*job["ins"])


def _mm_call(name, grid, k_axis, a, a_spec, a2d, b, b_spec, b2d, dims, out_sds, out_spec, o2d, *,
             alpha=1.0, res=None, res_spec=None, into=None, job=None):
    nk = grid[k_axis]
    n_in = 2 + (res is not None) + (into is not None)

    def body(*refs):
        a_ref, b_ref = refs[0], refs[1]
        res_ref = refs[2] if res is not None else None
        (o_ref, acc_ref), finish_job = _job_in_body(job, refs, n_in, 1, 1, grid)
        k = pl.program_id(k_axis)

        @pl.when(k == 0)
        def _():
            acc_ref[...] = jnp.zeros_like(acc_ref)

        av = a_ref[...].reshape(a2d).astype(BF16)
        bv = b_ref[...].reshape(b2d).astype(BF16)
        acc_ref[...] += lax.dot_general(av, bv, dims, preferred_element_type=F32)

        @pl.when(k == nk - 1)
        def _():
            r = acc_ref[...]
            if alpha != 1.0:
                r = r * alpha
            if res_ref is not None:
                r = res_ref[...].reshape(o2d) + r
            o_ref[...] = r.reshape(o_ref.shape).astype(o_ref.dtype)

        finish_job()

    in_specs, args = [a_spec, b_spec], [a, b]
    if res is not None:
        in_specs.append(res_spec)
        args.append(res)
    aliases = {}
    if into is not None:
        aliases = {len(args): 0}
        in_specs.append(pl.BlockSpec(memory_space=pl.ANY))
        args.append(into)
        out_sds = jax.ShapeDtypeStruct(into.shape, into.dtype)
    sem = tuple("arbitrary" if d == k_axis or job is not None else "parallel" for d in range(len(grid)))
    res_all = _job_call(
        job, body, name=name, grid=grid, in_specs=in_specs, out_specs=[out_spec], out_shape=[out_sds], args=args,
        scratch_shapes=[pltpu.VMEM(o2d, F32)], aliases=aliases, dimension_semantics=sem)
    return res_all[0] if job is None else (res_all[0], res_all[1:])


def _mm(a, b, *, ta=False, tb=False, out=F32, res=None, alpha=1.0, name):
    K, M = a.shape if ta else a.shape[::-1]
    N = b.shape[0] if tb else b.shape[1]
    assert (b.shape[1] if tb else b.shape[0]) == K, (a.shape, b.shape, ta, tb)
    tk = _divisor(K, 1024, LANES)
    tn = _divisor(N, 1408, LANES)
    for cap in (1024, 512, 256, 128):
        tm = _divisor(M, cap, LANES if ta else 16)
        est = 2 * (tm * tk * a.dtype.itemsize + tk * tn * b.dtype.itemsize + tm * tn * jnp.dtype(out).itemsize)
        est += tm * tn * 4 + (2 * tm * tn * 4 if res is not None else 0)
        if est <= MM_VMEM_BYTES:
            break
    a_spec = pl.BlockSpec((tk, tm), lambda i, j, k: (k, i)) if ta else pl.BlockSpec((tm, tk), lambda i, j, k: (i, k))
    b_spec = pl.BlockSpec((tn, tk), lambda i, j, k: (j, k)) if tb else pl.BlockSpec((tk, tn), lambda i, j, k: (k, j))
    o_spec = pl.BlockSpec((tm, tn), lambda i, j, k: (i, j))
    dims = (((0 if ta else 1,), (1 if tb else 0,)), ((), ()))
    return _mm_call(name, (M // tm, N // tn, K // tk), 2, a, a_spec, (tk, tm) if ta else (tm, tk), b, b_spec,
                    (tn, tk) if tb else (tk, tn), dims, jax.ShapeDtypeStruct((M, N), out), o_spec, (tm, tn),
                    alpha=alpha, res=res, res_spec=o_spec)


def _w128_spec(blk):
    return pl.BlockSpec((N_DEV, 128, D_MODEL), lambda *_: (0, blk, 0))


def _mm_w128(a, G1, blk, *, tb=False, res=None, out=F32, name):
    S = a.shape[0]
    tm = _divisor(S, 512, 16)
    row = pl.BlockSpec((tm, D_MODEL), lambda i, k: (i, 0))
    return _mm_call(name, (S // tm, 1), 1, a, row, (tm, D_MODEL), G1, _w128_spec(blk), (D_MODEL, D_MODEL), NT if tb else NN,
                    jax.ShapeDtypeStruct((S, D_MODEL), out), row, (tm, D_MODEL), res=res, res_spec=row)


def _mm_w128_dw(a, b, blk, into, name):
    S = a.shape[0]
    tk = _divisor(S, 1024, 16)
    row = pl.BlockSpec((tk, D_MODEL), lambda i, k: (k, 0))
    return _mm_call(name, (1, S // tk), 1, a, row, (tk, D_MODEL), b, row, (tk, D_MODEL), TN, None, _w128_spec(blk),
                    (D_MODEL, D_MODEL), into=into)


def _ffn_gate_up(h, norm_w, G2v, rb, name, job=None):
    S = h.shape[0]
    tm = _divisor(S, 1024, 16)
    grid = (S // tm, 4)

    def body(*refs):
        h_ref, nw_ref, w_ref = refs[:3]
        (n_ref, gu_ref, act_ref, n_scr), finish_job = _job_in_body(job, refs, 3, 3, 1, grid)

        @pl.when(pl.program_id(1) == 0)
        def _():
            x = h_ref[...]
            y = x * lax.rsqrt(jnp.mean(x * x, axis=-1, keepdims=True) + RMS_EPS)
            n_scr[...] = (y * nw_ref[...]).astype(BF16)
            n_ref[...] = n_scr[...]

        nv = n_scr[...]
        g = jnp.dot(nv, w_ref[0, 0], preferred_element_type=F32)
        u = jnp.dot(nv, w_ref[1, 0], preferred_element_type=F32)
        gu_ref[0, 0] = g.astype(BF16)
        gu_ref[1, 0] = u.astype(BF16)
        act_ref[0] = (g * jax.nn.sigmoid(g) * u).astype(BF16)
        finish_job()

    row = pl.BlockSpec((tm, D_MODEL), lambda i, j: (i, 0))
    return _job_call(
        job, body, name=name, grid=grid,
        in_specs=[row, pl.BlockSpec((1, D_MODEL), lambda i, j: (0, 0)), pl.BlockSpec((2, 1, D_MODEL, FF_BLK), lambda i, j: (0, j, rb, 0))],
        out_specs=[row, pl.BlockSpec((2, 1, tm, FF_BLK), lambda i, j: (0, j, i, 0)), pl.BlockSpec((1, tm, FF_BLK), lambda i, j: (j, i, 0))],
        out_shape=[jax.ShapeDtypeStruct((S, D_MODEL), BF16), jax.ShapeDtypeStruct((2, 4, S, FF_BLK), BF16), jax.ShapeDtypeStruct((4, S, FF_BLK), BF16)],
        args=[h, norm_w, G2v], scratch_shapes=[pltpu.VMEM((tm, D_MODEL), BF16)], aliases={},
        dimension_semantics=("arbitrary" if job is not None else "parallel", "arbitrary"))


def _ffn_down(act, G1, ob, h, name, job=None):
    S = h.shape[0]
    tm = _divisor(S, 512, 16)
    row = pl.BlockSpec((tm, D_MODEL), lambda i, k: (i, 0))
    return _mm_call(name, (S // tm, 4), 1, act, pl.BlockSpec((1, tm, FF_BLK), lambda i, k: (k, i, 0)), (tm, FF_BLK),
                    G1, pl.BlockSpec((2, DOWN_ROWS, D_MODEL), lambda i, k: (k, ob, 0)), (FF_BLK, D_MODEL), NN,
                    jax.ShapeDtypeStruct((S, D_MODEL), F32), row, (tm, D_MODEL), alpha=0.5, res=h, res_spec=row, job=job)


def _ffn_down_dx(dh, G1, ob, gu, name):
    S = dh.shape[0]
    tm = _divisor(S, 512, 16)

    def body(dh_ref, w_ref, gu_ref, o_ref):
        w = w_ref[...].reshape(FF_BLK, D_MODEL)
        dact = lax.dot_general(dh_ref[...].astype(BF16), w, NT, preferred_element_type=F32) * 0.5
        g = gu_ref[0, 0].astype(F32)
        u = gu_ref[1, 0].astype(F32)
        sg = jax.nn.sigmoid(g)
        o_ref[0, 0] = (dact * u * (sg * (1.0 + g * (1.0 - sg)))).astype(BF16)
        o_ref[1, 0] = (dact * (g * sg)).astype(BF16)

    blk = pl.BlockSpec((2, 1, tm, FF_BLK), lambda j, i: (0, j, i, 0))
    return pl.pallas_call(
        body, name=name, grid=(4, S // tm),
        in_specs=[pl.BlockSpec((tm, D_MODEL), lambda j, i: (i, 0)), pl.BlockSpec((2, DOWN_ROWS, D_MODEL), lambda j, i: (j, ob, 0)), blk],
        out_specs=blk, out_shape=jax.ShapeDtypeStruct((2, 4, S, FF_BLK), BF16),
    )(dh, G1, gu)


def _ffn_down_dw(act, dh, ob, into, name, job=None):
    S = dh.shape[0]
    tk = _divisor(S, 1024, 16)
    return _mm_call(name, (4, S // tk), 1, act, pl.BlockSpec((1, tk, FF_BLK), lambda j, k: (j, k, 0)), (tk, FF_BLK),
                    dh, pl.BlockSpec((tk, D_MODEL), lambda j, k: (k, 0)), (tk, D_MODEL), TN, None,
                    pl.BlockSpec((2, DOWN_ROWS, D_MODEL), lambda j, k: (j, ob, 0)), (FF_BLK, D_MODEL), alpha=0.5, into=into, job=job)


def _ffn_gate_up_dw(n, dgu8, rb, into, name, job=None):
    S = n.shape[0]
    tk = _divisor(S, 1024, 16)
    return _mm_call(name, (N_DEV, S // tk), 1, n, pl.BlockSpec((tk, D_MODEL), lambda b, k: (k, 0)), (tk, D_MODEL),
                    dgu8, pl.BlockSpec((1, tk, FF_BLK), lambda b, k: (b, k, 0)), (tk, FF_BLK), TN, None,
                    pl.BlockSpec((1, D_MODEL, FF_BLK), lambda b, k: (b, rb, 0)), (D_MODEL, FF_BLK), into=into, job=job)


def _ffn_gate_up_dx(dgu8, G2, rb, h, norm_w, dres, name):
    S = h.shape[0]
    tm = _divisor(S, 512, 16)

    def body(a_ref, w_ref, h_ref, nw_ref, r_ref, o_ref, dw_ref, acc_ref):
        i, k = pl.program_id(0), pl.program_id(1)

        @pl.when(k == 0)
        def _():
            acc_ref[...] = jnp.zeros_like(acc_ref)

        @pl.when((k == 0) & (i == 0))
        def _():
            dw_ref[...] = jnp.zeros_like(dw_ref)

        acc_ref[...] += lax.dot_general(a_ref[0], w_ref[0], NT, preferred_element_type=F32)

        @pl.when(k == N_DEV - 1)
        def _():
            dn = acc_ref[...]
            x = h_ref[...]
            r = lax.rsqrt(jnp.mean(x * x, axis=-1, keepdims=True) + RMS_EPS)
            xh = x * r
            gw = dn * nw_ref[...]
            o_ref[...] = r_ref[...] + r * (gw - xh * jnp.mean(gw * xh, axis=-1, keepdims=True))
            dw_ref[...] += jnp.sum(dn * xh, axis=0, keepdims=True)

    row = pl.BlockSpec((tm, D_MODEL), lambda i, k: (i, 0))
    vec = pl.BlockSpec((1, D_MODEL), lambda i, k: (0, 0))
    return pl.pallas_call(
        body, name=name, grid=(S // tm, N_DEV),
        in_specs=[pl.BlockSpec((1, tm, FF_BLK), lambda i, k: (k, i, 0)), pl.BlockSpec((1, D_MODEL, FF_BLK), lambda i, k: (k, rb, 0)), row, vec, row],
        out_specs=[row, vec], out_shape=[jax.ShapeDtypeStruct((S, D_MODEL), F32), jax.ShapeDtypeStruct((1, D_MODEL), F32)],
        scratch_shapes=[pltpu.VMEM((tm, D_MODEL), F32)],
        compiler_params=pltpu.CompilerParams(dimension_semantics=("arbitrary", "arbitrary")),
    )(dgu8, G2, h, norm_w, dres)


def _unheads(x):
    h, S, d = x.shape
    return jnp.transpose(x, (1, 0, 2)).reshape(S, h * d)


def _exact3(v):
    rnd = lambda a: lax.reduce_precision(a, exponent_bits=8, mantissa_bits=7)
    hi = rnd(v)
    mid = rnd(v - hi)
    return hi, mid, rnd(v - hi - mid)


def _causal_mask(st, i, j, tq, tk, window):
    dist = (i * tq + lax.broadcasted_iota(jnp.int32, (tk, tq), 1)) - (j * tk + lax.broadcasted_iota(jnp.int32, (tk, tq), 0))
    mask = dist >= 0
    if window is not None:
        mask = mask & (dist < window)
    return jnp.where(mask, st, NEG)


def _attn_fwd(qT, k, vT1, *, tile, hb, window=None, sink=None, name, job=None):
    H, dqk, S = qT.shape
    G = H // k.shape[0]
    dvp = vT1.shape[1]
    dv = dvp - 16
    tq = tk = tile
    assert H % hb == 0 and (G == 1 or G % hb == 0)
    kvb = hb if G == 1 else 1
    grid = (H // hb, S // tq)
    n_in = 3 + (sink is not None)

    def body(*refs):
        q_ref, k_ref, v_ref = refs[:3]
        (o_ref, lse_ref), finish_job = _job_in_body(job, refs, n_in, 2, 0, grid)
        i = pl.program_id(1)
        carry = []
        for a in range(hb):
            if sink is not None:
                carry.append(jnp.zeros((1, tq), F32) + refs[3][a, :, 0:1])
                carry.append(jnp.where(lax.broadcasted_iota(jnp.int32, (dvp, tq), 0) == dv, 1.0, 0.0))
            else:
                carry.append(jnp.full((1, tq), NEG, F32))
                carry.append(jnp.zeros((dvp, tq), F32))

        def step(j, carry, masked):
            off = pl.multiple_of(j * tk, tk)
            out = []
            for a in range(hb):
                m, acc = carry[2 * a], carry[2 * a + 1]
                kv = a if kvb > 1 else 0
                st = jnp.dot(k_ref[kv, pl.ds(off, tk), :], q_ref[a], preferred_element_type=F32)
                if masked:
                    st = _causal_mask(st, i, j, tq, tk, window)
                m_new = jnp.maximum(m, jnp.max(st, axis=0, keepdims=True))
                pt = jnp.exp(st - m_new).astype(BF16)
                acc = jnp.exp(m - m_new) * acc + jnp.dot(v_ref[kv, :, pl.ds(off, tk)], pt, preferred_element_type=F32)
                out += [m_new, acc]
            return tuple(out)

        carry = tuple(carry)
        if window is None:
            carry = lax.fori_loop(0, i, functools.partial(step, masked=False), carry)
            carry = step(i, carry, True)
        else:
            lo = jnp.maximum((i * tq - (window - 1)) // tk, 0)
            carry = lax.fori_loop(lo, i + 1, functools.partial(step, masked=True), carry)
        for a in range(hb):
            m, acc = carry[2 * a], carry[2 * a + 1]
            l = acc[dv:dv + 1, :]
            o_ref[a] = acc[:dv, :] / l
            lse_ref[a] = m + jnp.log(l)
        finish_job()

    kv_idx = (lambda b: b) if G == 1 else (lambda b: (b * hb) // G)
    in_specs = [
        pl.BlockSpec((hb, dqk, tq), lambda b, i: (b, 0, i)),
        pl.BlockSpec((kvb, S, dqk), lambda b, i: (kv_idx(b), 0, 0)),
        pl.BlockSpec((kvb, dvp, S), lambda b, i: (kv_idx(b), 0, 0)),
    ]
    args = [qT, k, vT1]
    if sink is not None:
        in_specs += [pl.BlockSpec((hb, 1, LANES), lambda b, i: (b, 0, 0))]
        args += [sink]
    return _job_call(
        job, body, name=name, grid=grid, in_specs=in_specs,
        out_specs=[pl.BlockSpec((hb, dv, tq), lambda b, i: (b, 0, i)), pl.BlockSpec((hb, 1, tq), lambda b, i: (b, 0, i))],
        out_shape=[jax.ShapeDtypeStruct((H, dv, S), F32), jax.ShapeDtypeStruct((H, 1, S), F32)],
        args=args, scratch_shapes=[], aliases={}, dimension_semantics=("arbitrary", "arbitrary") if job is not None else ("parallel", "parallel"))


def _attn_bwd(q, qT, k, kT, v, oT, do, doT, lse, *, tile, hb, window=None, sink=None, real=None, extra=False, name):
    H, S, dqk = q.shape
    G = H // k.shape[0]
    dv = v.shape[2]
    tq = tk = tile
    nq = S // tq
    has_p = sink is not None
    real = dqk if real is None else real
    assert H % hb == 0 and (G == 1 or G % hb == 0) and not (extra and real == dqk)
    kvb = hb if G == 1 else 1

    def body(*refs):
        q_ref, qT_ref, k_ref, kT_ref, v_ref, oT_ref, do_ref, doT_ref, lse_ref = refs[:9]
        p_ref = refs[9] if has_p else None
        pos = 10 if has_p else 9
        dq_ref, dk_ref, dv_ref = refs[pos: pos + 3]
        pos += 3
        ds_ref = refs[pos] if has_p else None
        pos += has_p
        dqx_ref, dkx_ref = (refs[pos], refs[pos + 1]) if extra else (None, None)
        delta = refs[-1]
        j = pl.program_id(1)

        @pl.when(j == 0)
        def _():
            dq_ref[...] = jnp.zeros_like(dq_ref)
            if extra:
                dqx_ref[...] = jnp.zeros_like(dqx_ref)
            for a in range(hb):
                drow = jnp.sum(doT_ref[a].astype(F32) * oT_ref[a], axis=0, keepdims=True)
                delta[a] = drow
                if has_p:
                    w = jnp.exp(p_ref[a, :, 0:1] - lse_ref[a])
                    ds_ref[a] = jnp.zeros((1, LANES), F32) - jnp.sum(w * drow, axis=1, keepdims=True)

        def step(i, carry, masked):
            off = pl.multiple_of(i * tq, tq)
            out = []
            for a in range(hb):
                dk, dvv = carry[2 * a], carry[2 * a + 1]
                kv = a if kvb > 1 else 0
                st = jnp.dot(k_ref[kv], qT_ref[a, :, pl.ds(off, tq)], preferred_element_type=F32)
                if masked:
                    st = _causal_mask(st, i, j, tq, tk, window)
                pt = jnp.exp(st - lse_ref[a, :, pl.ds(off, tq)])
                dvv = dvv + jnp.dot(pt.astype(BF16), do_ref[a, pl.ds(off, tq), :], preferred_element_type=F32)
                dpt = jnp.dot(v_ref[kv], doT_ref[a, :, pl.ds(off, tq)], preferred_element_type=F32)
                dsb = (pt * (dpt - delta[a, :, pl.ds(off, tq)])).astype(BF16)
                dk = dk + jnp.dot(dsb, q_ref[a, pl.ds(off, tq), :], preferred_element_type=F32)
                dqt = jnp.dot(kT_ref[kv], dsb, preferred_element_type=F32)
                dq_ref[a, :, pl.ds(off, tq)] += dqt[:real]
                if extra:
                    dqx_ref[a, :, pl.ds(off, tq)] += dqt[real:]
                out += [dk, dvv]
            return tuple(out)

        carry = (jnp.zeros((tk, dqk), F32), jnp.zeros((tk, dv), F32)) * hb
        if window is None:
            carry = step(j, carry, True)
            carry = lax.fori_loop(j + 1, nq, functools.partial(step, masked=False), carry)
        else:
            hi = jnp.minimum(nq - 1, ((j + 1) * tk + window - 2) // tq)
            carry = lax.fori_loop(j, hi + 1, functools.partial(step, masked=True), carry)
        for a in range(hb):
            dk_ref[a] = carry[2 * a][:, :real]
            if extra:
                dkx_ref[a] = carry[2 * a][:, real:]
            dv_ref[a] = carry[2 * a + 1]

    kv_idx = (lambda b: b) if G == 1 else (lambda b: (b * hb) // G)
    rows = lambda d: pl.BlockSpec((hb, S, d), lambda b, j: (b, 0, 0))
    colsT = lambda d: pl.BlockSpec((hb, d, S), lambda b, j: (b, 0, 0))
    in_specs = [
        rows(dqk), colsT(dqk),
        pl.BlockSpec((kvb, tk, dqk), lambda b, j: (kv_idx(b), j, 0)),
        pl.BlockSpec((kvb, dqk, tk), lambda b, j: (kv_idx(b), 0, j)),
        pl.BlockSpec((kvb, tk, dv), lambda b, j: (kv_idx(b), j, 0)),
        colsT(dv), rows(dv), colsT(dv),
        pl.BlockSpec((hb, 1, S), lambda b, j: (b, 0, 0)),
    ]
    args = [q, qT, k, kT, v, oT, do, doT, lse]
    if has_p:
        in_specs += [pl.BlockSpec((hb, 1, LANES), lambda b, j: (b, 0, 0))]
        args += [sink]
    out_specs = [colsT(real), pl.BlockSpec((hb, tk, real), lambda b, j: (b, j, 0)), pl.BlockSpec((hb, tk, dv), lambda b, j: (b, j, 0))]
    out_shape = [jax.ShapeDtypeStruct((H, real, S), F32), jax.ShapeDtypeStruct((H, S, real), F32), jax.ShapeDtypeStruct((H, S, dv), F32)]
    if has_p:
        out_specs += [pl.BlockSpec((hb, 1, LANES), lambda b, j: (b, 0, 0))]
        out_shape += [jax.ShapeDtypeStruct((H, 1, LANES), F32)]
    if extra:
        out_specs += [colsT(dqk - real), pl.BlockSpec((hb, tk, dqk - real), lambda b, j: (b, j, 0))]
        out_shape += [jax.ShapeDtypeStruct((H, dqk - real, S), F32), jax.ShapeDtypeStruct((H, S, dqk - real), F32)]
    return pl.pallas_call(
        body, name=name, grid=(H // hb, S // tk), in_specs=in_specs, out_specs=out_specs, out_shape=out_shape,
        scratch_shapes=[pltpu.VMEM((hb, 1, S), F32)],
        compiler_params=pltpu.CompilerParams(dimension_semantics=("parallel", "arbitrary")),
    )(*args)


def _rows_and_cols(x3):
    xb = x3.astype(BF16)
    return jnp.transpose(xb, (1, 0, 2)), jnp.transpose(xb, (1, 2, 0))


def _v_with_ones(v3):
    S, h, _ = v3.shape
    vT = jnp.transpose(v3.astype(BF16), (1, 2, 0))
    return jnp.concatenate([vT, jnp.ones((h, 1, S), BF16), jnp.zeros((h, 15, S), BF16)], axis=1)


def _from_T(oT):
    h, d, S = oT.shape
    return jnp.transpose(oT, (2, 0, 1)).reshape(S, h * d)


def _coords():
    return lax.axis_index("x"), lax.axis_index("y"), lax.axis_index("c")


def _peer(axis):
    x, y, c = _coords()
    return {"x": (1 - x, y, c), "y": (x, 1 - y, c), "c": (x, y, 1 - c)}[axis]


def _gather_job(bufs, rows=None):
    n = len(bufs)

    def copies(outs, send_sems, recv_sems):
        x, y, c = _coords()
        me, sibling = (x, y, c), (x, y, 1 - c)
        chips = [(1 - x, y), (x, 1 - y), (1 - x, 1 - y)]

        def copy(t, k, block, to):
            px, py, pc = block
            ref = outs[t].at[4 * px + 2 * py + pc]
            if rows is not None and rows[t] is not None:
                ref = ref.at[pl.ds(rows[t][0], rows[t][1])]
            return pltpu.make_async_remote_copy(ref, ref, send_sems.at[7 * t + k], recv_sems.at[7 * t + k], device_id=to, device_id_type=MESH)

        return copy, me, sibling, chips, c

    def start(ins, outs, send_sems, recv_sems):
        copy, me, sibling, chips, c = copies(outs, send_sems, recv_sems)
        for t in range(n):
            copy(t, 0, me, sibling).start()
            for j, chip in enumerate(chips):
                copy(t, 1 + j, me, (*chip, c)).start()

    def finish(ins, outs, send_sems, recv_sems):
        copy, me, sibling, chips, c = copies(outs, send_sems, recv_sems)
        for j, chip in enumerate(chips):
            for t in range(n):
                copy(t, 1 + j, (*chip, c), me).wait_recv()
                copy(t, 4 + j, (*chip, c), sibling).start()
        for t in range(n):
            copy(t, 0, sibling, me).wait_recv()
            for j, chip in enumerate(chips):
                copy(t, 4 + j, (*chip, 1 - c), me).wait_recv()
        for t in range(n):
            copy(t, 0, me, sibling).wait_send()
            for j, chip in enumerate(chips):
                copy(t, 1 + j, me, (*chip, c)).wait_send()
                copy(t, 4 + j, (*chip, c), sibling).wait_send()

    return dict(ins=list(bufs), outs=[jax.ShapeDtypeStruct(b.shape, b.dtype) for b in bufs], aliases={t: t for t in range(n)},
                n_sems=7 * n, start=start, finish=finish)


def _in_slot(local):
    x, y, c = _coords()
    buf = jnp.zeros((N_DEV,) + local.shape, local.dtype)
    return lax.dynamic_update_slice(buf, local[None], (4 * x + 2 * y + c, 0, 0))


def _pair_job(vs, axes):
    n = len(vs)
    axes = [axes] * n if isinstance(axes, str) else axes

    def copies(ins, outs, send_sems, recv_sems):
        out = []
        for t in range(n):
            me = lax.axis_index(axes[t])
            src = ins[t].at[1 - me] if len(ins[t].shape) == 3 else ins[t].at[:, 1 - me]
            out.append(pltpu.make_async_remote_copy(src, outs[t], send_sems.at[t], recv_sems.at[t], device_id=_peer(axes[t]), device_id_type=MESH))
        return out

    def start(*refs):
        for cp in copies(*refs):
            cp.start()

    def finish(*refs):
        for cp in copies(*refs):
            cp.wait()

    return dict(ins=list(vs), outs=[jax.ShapeDtypeStruct(v.shape[:-3] + v.shape[-2:], v.dtype) for v in vs], aliases={}, n_sems=n,
                start=start, finish=finish)


def _add_kept(v, got, axis, out, name):
    R, C = v.shape[-2:]
    lead = v.shape[0] if v.ndim == 4 else 1
    tm = _divisor(R, max(16, EW_TILE_BYTES // (_lanes(C) * (v.dtype.itemsize + got.dtype.itemsize + jnp.dtype(out).itemsize)) // 16 * 16), 16)
    me = lax.axis_index(axis).astype(jnp.int32).reshape(1)
    v4 = v.reshape(lead, 2, R, C)
    g3 = got.reshape(lead, R, C)

    def body(me_ref, v_ref, g_ref, o_ref):
        o_ref[...] = (v_ref[0].astype(F32) + g_ref[...].astype(F32)).astype(o_ref.dtype)

    res = pl.pallas_call(
        body, name=name, out_shape=jax.ShapeDtypeStruct((lead, R, C), out),
        grid_spec=pltpu.PrefetchScalarGridSpec(
            num_scalar_prefetch=1, grid=(lead, R // tm),
            in_specs=[pl.BlockSpec((1, 1, tm, C), lambda b, i, me: (b, me[0], i, 0)), pl.BlockSpec((1, tm, C), lambda b, i, me: (b, i, 0))],
            out_specs=pl.BlockSpec((1, tm, C), lambda b, i, me: (b, i, 0))),
    )(me, v4, g3)
    return res


def _cross_job(vs):
    n = len(vs)

    def copies(ins, outs, send_sems, recv_sems):
        x, y, _ = _coords()
        out = []
        for t in range(n):
            h = ins[t].shape[2] // 2
            out.append(pltpu.make_async_remote_copy(ins[t].at[1 - x, :, pl.ds(0, h)], outs[2 * t], send_sems.at[2 * t], recv_sems.at[2 * t],
                                                    device_id=_peer("x"), device_id_type=MESH))
            out.append(pltpu.make_async_remote_copy(ins[t].at[:, 1 - y, pl.ds(h, h)], outs[2 * t + 1], send_sems.at[2 * t + 1], recv_sems.at[2 * t + 1],
                                                    device_id=_peer("y"), device_id_type=MESH))
        return out

    def start(*refs):
        for cp in copies(*refs):
            cp.start()

    def finish(*refs):
        for cp in copies(*refs):
            cp.wait()

    outs = []
    for v in vs:
        outs += [jax.ShapeDtypeStruct((2, v.shape[2] // 2, v.shape[3]), v.dtype)] * 2
    return dict(ins=list(vs), outs=outs, aliases={}, n_sems=2 * n, start=start, finish=finish)


def _add_picked(v, got, axis, out, name):
    _, _, R, C = v.shape
    h = R // 2
    tm = _divisor(h, max(16, EW_TILE_BYTES // (_lanes(C) * (v.dtype.itemsize + got.dtype.itemsize + jnp.dtype(out).itemsize)) // 16 * 16), 16)
    me = lax.axis_index(axis).astype(jnp.int32).reshape(1)
    if axis == "x":
        v_map = lambda b, i, me: (me[0], b, i, 0)
    else:
        v_map = lambda b, i, me: (b, me[0], i + h // tm, 0)

    def body(me_ref, v_ref, g_ref, o_ref):
        o_ref[...] = (v_ref[0].astype(F32) + g_ref[...].astype(F32)).astype(o_ref.dtype)

    return pl.pallas_call(
        body, name=name, out_shape=jax.ShapeDtypeStruct((2, h, C), out),
        grid_spec=pltpu.PrefetchScalarGridSpec(
            num_scalar_prefetch=1, grid=(2, h // tm),
            in_specs=[pl.BlockSpec((1, 1, tm, C), v_map), pl.BlockSpec((1, tm, C), lambda b, i, me: (b, i, 0))],
            out_specs=pl.BlockSpec((1, tm, C), lambda b, i, me: (b, i, 0))),
    )(me, v, got)


def _reduce_scatter_steps(gs, tag):
    n = len(gs)
    vs = [g.reshape(4, 2, *g.shape[1:]) for g in gs]
    got = yield _pair_job(vs, "c")
    vs = [_add_kept(v, r, "c", BF16, f"rs_{tag}_add_c{t}") for t, (v, r) in enumerate(zip(vs, got))]
    vs = [v.reshape(2, 2, v.shape[1], v.shape[2]) for v in vs]
    got = yield _cross_job(vs)
    up = [_add_picked(v, r, "x", BF16, f"rs_{tag}_add_x{t}") for t, (v, r) in enumerate(zip(vs, got[0::2]))]
    lo = [_add_picked(v, r, "y", BF16, f"rs_{tag}_add_y{t}") for t, (v, r) in enumerate(zip(vs, got[1::2]))]
    got = yield _pair_job(up + lo, ["y"] * n + ["x"] * n)
    out = []
    for t in range(n):
        a = _add_kept(up[t], got[t], "y", F32, f"rs_{tag}_add_y2{t}")[0]
        b = _add_kept(lo[t], got[n + t], "x", F32, f"rs_{tag}_add_x2{t}")[0]
        out.append(jnp.concatenate([a, b], axis=0))
    return out


def _reduce_scatter(gs, tag):
    steps = _reduce_scatter_steps(gs, tag)
    job = next(steps)
    for stage in ("c", "xy", "yx"):
        got = _comm_call(job, f"rs_{tag}_{stage}")
        try:
            job = steps.send(got)
        except StopIteration as done:
            return done.value


def _all_reduce_small(v):
    def body(v_ref, o_ref, buf, send_sems, recv_sems):
        x, y, c = _coords()
        me = 4 * x + 2 * y + c
        buf[me] = v_ref[...]
        copies = []
        for k in range(1, N_DEV):
            peer = tuple((1 - a) if (k >> s) & 1 else a for a, s in ((x, 2), (y, 1), (c, 0)))
            cp = pltpu.make_async_remote_copy(v_ref, buf.at[me], send_sems.at[k - 1], recv_sems.at[k - 1], device_id=peer, device_id_type=MESH)
            cp.start()
            copies.append(cp)
        for cp in copies:
            cp.wait()
        acc = buf[0]
        for d in range(1, N_DEV):
            acc = acc + buf[d]
        o_ref[...] = acc

    vm = pl.BlockSpec(memory_space=pltpu.VMEM)
    return pl.pallas_call(
        body, name="all_reduce_small", in_specs=[vm], out_specs=vm, out_shape=jax.ShapeDtypeStruct(v.shape, F32),
        scratch_shapes=[pltpu.VMEM((N_DEV,) + v.shape, F32), pltpu.SemaphoreType.DMA((N_DEV - 1,)), pltpu.SemaphoreType.DMA((N_DEV - 1,))],
    )(v)


def _local_groups(w, dtype):
    mix_out = [w["ev_w_out"][0], w["od_w_out"][0]]
    layers = []
    for l in range(DEPTH):
        a = jnp.concatenate([w["ffa_w_down"][l], w["ffb_w_down"][l]], axis=0).astype(dtype)
        b = jnp.concatenate([w["ple_w_gate"][l], mix_out[l]], axis=0).astype(dtype)
        c = jnp.concatenate([w["ffa_w_gate_up"][l], w["ffb_w_gate_up"][l]], axis=0).astype(dtype)
        layers.append((a, b, c))
    strip = jnp.concatenate([w["ple_w_proj"].reshape(-1, STRIP_C), w["ev_w_ukv"][0], jnp.pad(w["ev_w_uq"][0], ((0, 0), (0, STRIP_C - 96))),
                             jnp.zeros((G3_ROWS - 896, STRIP_C), F32)], axis=0)
    m = jnp.concatenate([w["od_w_in"][0], w["ev_w_in"][0], strip, jnp.zeros((G3_ROWS, G3_COLS - STRIP0 - STRIP_C), F32)], axis=1).astype(dtype)
    return layers, m


def _ungroup_local(layers, r3):
    a, b, c = zip(*layers)
    out = {
        "ffa_w_down": jnp.stack([x[:DOWN_ROWS] for x in a]), "ffb_w_down": jnp.stack([x[DOWN_ROWS:] for x in a]),
        "ple_w_gate": jnp.stack([x[:128] for x in b]), "ev_w_out": b[0][128:][None], "od_w_out": b[1][128:][None],
        "ffa_w_gate_up": jnp.stack([x[:D_MODEL] for x in c]), "ffb_w_gate_up": jnp.stack([x[D_MODEL:] for x in c]),
        "od_w_in": r3[:, :OD_C][None], "ev_w_in": r3[:, OD_C:STRIP0][None],
    }
    strip = r3[:, STRIP0:STRIP0 + STRIP_C]
    out["ple_w_proj"] = strip[:512].reshape(2, PLE_DIM, STRIP_C)
    out["ev_w_ukv"] = strip[512:640][None]
    out["ev_w_uq"] = strip[640:896, :96][None]
    return out


def _cols(a):
    return jnp.transpose(a, (1, 0, 2)).reshape(a.shape[1], -1)


def _blocks(g, c):
    return jnp.transpose(g.reshape(g.shape[0], N_DEV, c), (1, 0, 2))


def _uq_permute(w):
    r = w.shape[0]
    w3 = w.reshape(r, B_HEADS, B_NOPE + B_ROPE)
    half = B_ROPE // 2
    return jnp.concatenate([w3[:, :, :B_NOPE].reshape(r, -1), w3[:, :, B_NOPE:B_NOPE + half].reshape(r, -1), w3[:, :, B_NOPE + half:].reshape(r, -1)], axis=1)


def _uq_unpermute(g):
    r = g.shape[0]
    half = B_ROPE // 2
    n = B_HEADS * B_NOPE
    parts = [g[:, :n].reshape(r, B_HEADS, B_NOPE), g[:, n:n + B_HEADS * half].reshape(r, B_HEADS, half), g[:, n + B_HEADS * half:].reshape(r, B_HEADS, half)]
    return jnp.concatenate(parts, axis=2).reshape(r, -1)


def _ukv_permute(w):
    r = w.shape[0]
    return jnp.transpose(w.reshape(r, B_HEADS, 2, B_NOPE), (0, 2, 1, 3)).reshape(r, -1)


def _ukv_unpermute(g):
    r = g.shape[0]
    return jnp.transpose(g.reshape(r, 2, B_HEADS, B_NOPE), (0, 2, 1, 3)).reshape(r, -1)


def _misc_weights(G3):
    strip = G3[:, :, STRIP0:STRIP0 + STRIP_C]
    return {
        "od_w_in": jnp.pad(_cols(G3[:, :, :OD_C]), ((0, 0), (0, ODD_IN_PAD - ODD_IN))),
        "ev_w_in": jnp.pad(_cols(G3[:, :, OD_C:STRIP0]), ((0, 0), (0, EVEN_IN_PAD - EVEN_IN))),
        "ple_w_proj": [_cols(strip[:, i * PLE_DIM:(i + 1) * PLE_DIM]) for i in range(DEPTH)],
        "ev_w_ukv": _ukv_permute(_cols(strip[:, 512:640])),
        "ev_w_uq": _uq_permute(_cols(strip[:, 640:896, :96])),
    }


def _misc_grads(G):
    strip = jnp.concatenate([
        _blocks(G["ple_w_proj"][0], STRIP_C), _blocks(G["ple_w_proj"][1], STRIP_C), _blocks(_ukv_unpermute(G["ev_w_ukv"]), STRIP_C),
        jnp.pad(_blocks(_uq_unpermute(G["ev_w_uq"]), 96), ((0, 0), (0, 0), (0, STRIP_C - 96))),
        jnp.zeros((N_DEV, G3_ROWS - 896, STRIP_C), F32)], axis=1)
    return jnp.concatenate([_blocks(G["od_w_in"][:, :ODD_IN], OD_C), _blocks(G["ev_w_in"][:, :EVEN_IN], EV_C), strip,
                            jnp.zeros((N_DEV, G3_ROWS, G3_COLS - STRIP0 - STRIP_C), F32)], axis=2)


def _ffn_fwd(h, norm_w, W, f, i, tag, ride=None):
    job = ride() if ride else None
    res = _ffn_gate_up(h, norm_w, W["C"][i].reshape(2, 4, C_ROWS, FF_BLK), f, f"{tag}_gate_up", job=job)
    n, gu, act = res[:3]
    if job is not None:
        ride(res[3:])
    job = ride() if ride else None
    out = _ffn_down(act, W["A"][i], f, h, f"{tag}_down", job=job)
    if job is not None:
        out, got = out
        ride(got)
    return out, (h, n, gu, act)


def _ffn_bwd(dout, saved, norm_w, W, GB, f, i, tag, ride=None):
    h, n, gu, act = saved
    S = h.shape[0]
    job = ride() if ride else None
    res = _ffn_down_dw(act, dout, f, GB["A"][i], f"{tag}_down_dw", job=job)
    if job is not None:
        res, got = res
        ride(got)
    GB["A"][i] = res
    dgu = _ffn_down_dx(dout, W["A"][i], f, gu, f"{tag}_down_dx").reshape(N_DEV, S, FF_BLK)
    job = ride() if ride else None
    res = _ffn_gate_up_dw(n, dgu, f, GB["C"][i], f"{tag}_gate_up_dw", job=job)
    if job is not None:
        res, got = res
        ride(got)
    GB["C"][i] = res
    return _ffn_gate_up_dx(dgu, W["C"][i], f, h, norm_w, dout, f"{tag}_gate_up_dx")


def _rope_tables(S):
    inv = ROPE_THETA ** (-jnp.arange(0, B_ROPE, 2, dtype=F32) / B_ROPE)
    ang = jnp.arange(S, dtype=F32)[:, None] * inv[None, :]
    return jnp.cos(ang), jnp.sin(ang)


def _alibi_columns(S):
    t = jnp.arange(S, dtype=jnp.int32)
    hi = ((t // 16) * 16).astype(F32)
    lo = (t % 16).astype(F32)
    slopes = 2.0 ** (-8.0 * jnp.arange(1, A_HEADS + 1, dtype=F32) / A_HEADS)
    zq = jnp.zeros((S, A_HEADS), F32)
    rest = QK_PAD - A_HEAD_DIM - 4
    qc = jnp.stack([-slopes[None, :] * hi[:, None], -slopes[None, :] * lo[:, None], zq + slopes[None, :], zq + slopes[None, :]] + [zq] * rest, axis=-1)
    one = jnp.ones((S, A_KV_HEADS), F32)
    zk = jnp.zeros((S, A_KV_HEADS), F32)
    kc = jnp.stack([one, one, zk + hi[:, None], zk + lo[:, None]] + [zk] * rest, axis=-1)
    return qc, kc


def _sink_prm(sinks):
    return jnp.zeros((A_HEADS, 1, LANES), F32).at[:, 0, 0].set(sinks.astype(F32))


def _with_ride(ride, call):
    job = ride() if ride else None
    res = call(job)
    if job is None:
        return res
    n_own = len(res) - len(job["outs"])
    ride(res[n_own:])
    return res[:n_own]


def _even_fwd(hn, h, W, ride=None):
    S = hn.shape[0]
    proj = _mm(hn, W["ev_w_in"], name="ev_in")
    a_q, a_k, a_v = proj[:, :512], proj[:, 512:640], proj[:, 640:768]
    c_q, c_kv = proj[:, 768:1024], proj[:, 1024:1152]
    kr1, kr2 = proj[:, 1152:1168], proj[:, 1168:1184]
    qc, kc = _alibi_columns(S)
    qa, qaT = _rows_and_cols(jnp.concatenate([(a_q * A_HEAD_DIM ** -0.5).reshape(S, A_HEADS, A_HEAD_DIM), qc], axis=-1))
    ka, kaT = _rows_and_cols(jnp.concatenate([a_k.reshape(S, A_KV_HEADS, A_HEAD_DIM), kc], axis=-1))
    va3 = a_v.reshape(S, A_KV_HEADS, A_HEAD_DIM)
    va = jnp.transpose(va3.astype(BF16), (1, 0, 2))
    prm = _sink_prm(W["ev_sinks"][0])
    oaT, lse_a = _with_ride(ride, lambda job: _attn_fwd(qaT, ka, _v_with_ones(va3), tile=SWA_TILE, hb=2, window=WINDOW, sink=prm,
                                                        name="swa_fwd", job=job))
    cqn = _rms_fwd(c_q, W["ev_cq_norm"], "ev_cq_norm")
    q_all = _mm(cqn, W["ev_w_uq"], name="ev_uq")
    ckvn = _rms_fwd(c_kv, W["ev_ckv_norm"], "ev_ckv_norm")
    kv_all = _mm(ckvn, W["ev_w_ukv"], name="ev_ukv")
    cos, sin = _rope_tables(S)
    cos8, sin8 = jnp.tile(cos, (1, B_HEADS)), jnp.tile(sin, (1, B_HEADS))
    q1, q2 = _rope(q_all[:, 512:640], q_all[:, 640:768], cos8, sin8, "ev_rope_q")
    k1, k2 = _rope(kr1, kr2, cos, sin, "ev_rope_k")
    half = B_ROPE // 2
    scale = (B_NOPE + B_ROPE) ** -0.5
    qb, qbT = _rows_and_cols(jnp.concatenate([q_all[:, :512].reshape(S, B_HEADS, B_NOPE), q1.reshape(S, B_HEADS, half), q2.reshape(S, B_HEADS, half)], axis=-1) * scale)
    kro = jnp.broadcast_to(jnp.concatenate([k1, k2], axis=1)[:, None, :], (S, B_HEADS, B_ROPE))
    kb, kbT = _rows_and_cols(jnp.concatenate([kv_all[:, :512].reshape(S, B_HEADS, B_NOPE), kro], axis=-1))
    vb3 = kv_all[:, 512:].reshape(S, B_HEADS, B_V)
    vb = jnp.transpose(vb3.astype(BF16), (1, 0, 2))
    obT, lse_b = _with_ride(ride, lambda job: _attn_fwd(qbT, kb, _v_with_ones(vb3), tile=min(ATTN_TILE_FWD, S), hb=2, name="mla_fwd", job=job))
    cat = jnp.concatenate([_from_T(oaT), _from_T(obT)], axis=1)
    out = _mm_w128(cat, W["B"][0], MIX_OUT_BLK, res=h, name="ev_out")
    return out, (hn, proj, (qa, qaT, ka, kaT, va, oaT, lse_a), prm, cqn, ckvn, (qb, qbT, kb, kbT, vb, obT, lse_b), cat)


def _even_bwd(dout, saved, W, GB):
    hn, proj, (qa, qaT, ka, kaT, va, oaT, lse_a), prm, cqn, ckvn, (qb, qbT, kb, kbT, vb, obT, lse_b), cat = saved
    S = hn.shape[0]
    G = {}
    dcat = _mm_w128(dout, W["B"][0], MIX_OUT_BLK, tb=True, out=BF16, name="ev_out_dx")
    GB["B"][0] = _mm_w128_dw(cat, dout, MIX_OUT_BLK, GB["B"][0], "ev_out_dw")
    doa, doaT = _rows_and_cols(dcat[:, :512].reshape(S, A_HEADS, A_HEAD_DIM))
    dqaT, dka, dva, dsink = _attn_bwd(qa, qaT, ka, kaT, va, oaT, doa, doaT, lse_a, tile=SWA_TILE, hb=2, window=WINDOW, sink=prm, real=A_HEAD_DIM,
                                       name="swa_bwd")
    G["ev_sinks"] = dsink[:, 0, 0]
    dqa = _from_T(dqaT) * A_HEAD_DIM ** -0.5
    dka = dka.reshape(A_KV_HEADS, A_GROUP, S, A_HEAD_DIM).sum(axis=1)
    dva = dva.reshape(A_KV_HEADS, A_GROUP, S, A_HEAD_DIM).sum(axis=1)
    dob, dobT = _rows_and_cols(dcat[:, 512:].reshape(S, B_HEADS, B_V))
    dqbT, dkb, dvb = _attn_bwd(qb, qbT, kb, kbT, vb, obT, dob, dobT, lse_b, tile=min(ATTN_TILE, S), hb=1, name="mla_bwd")
    half = B_ROPE // 2
    dqb = jnp.transpose(dqbT, (2, 0, 1)) * (B_NOPE + B_ROPE) ** -0.5
    dkb = jnp.transpose(dkb, (1, 0, 2))
    cos, sin = _rope_tables(S)
    cos8, sin8 = jnp.tile(cos, (1, B_HEADS)), jnp.tile(sin, (1, B_HEADS))
    dq1, dq2 = _rope(dqb[:, :, B_NOPE:B_NOPE + half].reshape(S, -1), dqb[:, :, B_NOPE + half:].reshape(S, -1), cos8, -sin8, "ev_rope_q_bwd")
    dq_all = jnp.concatenate([dqb[:, :, :B_NOPE].reshape(S, -1), dq1, dq2], axis=1).astype(BF16)
    dkr = dkb[:, :, B_NOPE:].sum(axis=1)
    dk1, dk2 = _rope(dkr[:, :half], dkr[:, half:], cos, -sin, "ev_rope_k_bwd")
    dkv_all = jnp.concatenate([dkb[:, :, :B_NOPE].reshape(S, -1), _unheads(dvb)], axis=1).astype(BF16)
    G["ev_w_uq"] = _mm(cqn, dq_all, ta=True, name="ev_uq_dw")
    dcqn = _mm(dq_all, W["ev_w_uq"], tb=True, name="ev_uq_dx")
    dc_q, G["ev_cq_norm"] = _rms_bwd(dcqn, proj[:, 768:1024], W["ev_cq_norm"], None, "ev_cq_norm_bwd")
    G["ev_w_ukv"] = _mm(ckvn, dkv_all, ta=True, name="ev_ukv_dw")
    dckvn = _mm(dkv_all, W["ev_w_ukv"], tb=True, name="ev_ukv_dx")
    dc_kv, G["ev_ckv_norm"] = _rms_bwd(dckvn, proj[:, 1024:1152], W["ev_ckv_norm"], None, "ev_ckv_norm_bwd")
    dproj = jnp.concatenate([dqa, _unheads(dka), _unheads(dva), dc_q, dc_kv, dk1, dk2,
                             jnp.zeros((S, EVEN_IN_PAD - EVEN_IN), F32)], axis=1).astype(BF16)
    G["ev_w_in"] = _mm(hn, dproj, ta=True, name="ev_in_dw")
    dhn = _mm(dproj, W["ev_w_in"], tb=True, name="ev_in_dx")
    return dhn, G


def _odd_fwd(hn, h, W):
    S = hn.shape[0]
    w = C_HEADS * C_HEAD_DIM
    proj = _mm(hn, W["od_w_in"], name="od_in")
    f_logit = proj[:, 3 * w: 3 * w + C_HEADS]
    logf = _logsig_fwd(f_logit, W["od_b_f"], "od_logsig")
    logc = _cumsum(logf, False, "od_cumsum")
    parts = [p[:, :, None] for p in _exact3(logc)]
    ones = [jnp.ones((S, C_HEADS, 1), F32)] * 3
    pad = [jnp.zeros((S, C_HEADS, QK_PAD - C_HEAD_DIM - 6), F32)]
    q3 = (proj[:, :w] * C_HEAD_DIM ** -0.5).reshape(S, C_HEADS, C_HEAD_DIM)
    k3 = proj[:, w:2 * w].reshape(S, C_HEADS, C_HEAD_DIM)
    q, qT = _rows_and_cols(jnp.concatenate([q3, jnp.concatenate(parts + ones + pad, axis=-1)], axis=-1))
    k, kT = _rows_and_cols(jnp.concatenate([k3, jnp.concatenate(ones + [-p for p in parts] + pad, axis=-1)], axis=-1))
    v3 = proj[:, 2 * w:3 * w].reshape(S, C_HEADS, C_HEAD_DIM)
    v = jnp.transpose(v3.astype(BF16), (1, 0, 2))
    oT, lse = _attn_fwd(qT, k, _v_with_ones(v3), tile=min(ATTN_TILE_FWD, S), hb=2, name="fox_fwd")
    cat = _from_T(oT)
    out = _mm_w128(cat, W["B"][1], MIX_OUT_BLK, res=h, name="od_out")
    return out, (hn, q, qT, k, kT, v, f_logit, oT, lse, cat)


def _odd_bwd(dout, saved, W, GB):
    hn, q, qT, k, kT, v, f_logit, oT, lse, cat = saved
    S = hn.shape[0]
    G = {}
    dcat = _mm_w128(dout, W["B"][1], MIX_OUT_BLK, tb=True, out=BF16, name="od_out_dx")
    GB["B"][1] = _mm_w128_dw(cat, dout, MIX_OUT_BLK, GB["B"][1], "od_out_dw")
    do, doT = _rows_and_cols(dcat.reshape(S, C_HEADS, C_HEAD_DIM))
    dqT, dk, dv, dqxT, dkx = _attn_bwd(q, qT, k, kT, v, oT, do, doT, lse, tile=min(ATTN_TILE, S), hb=1, real=C_HEAD_DIM, extra=True, name="fox_bwd")
    dlogc = jnp.transpose(dqxT[:, 0, :] - dkx[:, :, 3])
    dlogf = _cumsum(dlogc, True, "od_cumsum_bwd")
    df, db = _logsig_bwd(dlogf, f_logit, W["od_b_f"], "od_logsig_bwd")
    G["od_b_f"] = db
    dproj = jnp.concatenate([_from_T(dqT) * C_HEAD_DIM ** -0.5, _unheads(dk), _unheads(dv), df,
                             jnp.zeros((S, ODD_IN_PAD - ODD_IN), F32)], axis=1).astype(BF16)
    G["od_w_in"] = _mm(hn, dproj, ta=True, name="od_in_dw")
    dhn = _mm(dproj, W["od_w_in"], tb=True, name="od_in_dx")
    return dhn, G


class _Rider:
    def __init__(self, steps, tag):
        self.steps, self.tag, self.count, self.result = steps, tag, 0, None
        self.job = next(steps)

    def __call__(self, got=None):
        if got is None:
            return self.job
        try:
            self.job = self.steps.send(list(got))
        except StopIteration as done:
            self.job, self.result = None, done.value
        return None

    def finish(self):
        while self.job is not None:
            self.count += 1
            self(_comm_call(self.job, f"{self.tag}_{self.count}"))
        return self.result


def _gather_plan(W, slots):
    a0, b0, c0, m, a1, b1, c1 = (slots[key] for key in ("a0", "b0", "c0", "m", "a1", "b1", "c1"))
    (m,) = yield _gather_job([m])
    W.update(_misc_weights(m))
    (b0,) = yield _gather_job([b0])
    W["B"] = [b0]
    (c0,) = yield _gather_job([c0], rows=[(D_MODEL, D_MODEL)])
    W["C"] = [c0]
    a0, c1 = yield _gather_job([a0, c1], rows=[(DOWN_ROWS, DOWN_ROWS), (0, D_MODEL)])
    W["A"] = [a0]
    a1, b1 = yield _gather_job([a1, b1])
    (c1,) = yield _gather_job([c1], rows=[(D_MODEL, D_MODEL)])
    W["A"].append(a1)
    W["B"].append(b1)
    W["C"].append(c1)


def _local_step(x, p, target, W, slots):
    h = x
    saved = []
    gather = _Rider(_gather_plan(W, slots), "all_gather_rest")
    for i in range(DEPTH):
        t = f"l{i}"
        ride = gather if i == 0 else None
        h1, s_a = _ffn_fwd(h, W["ffa_norm"][i:i + 1], W, 0, i, f"{t}_ffa", ride)
        nm = _rms_fwd(h1, W["mix_norm"][i:i + 1], f"{t}_mix_norm")
        h2, s_m = _even_fwd(nm, h1, W, ride) if i % 2 == 0 else _odd_fwd(nm, h1, W)
        h3, s_b = _ffn_fwd(h2, W["ffb_norm"][i:i + 1], W, 1, i, f"{t}_ffb", ride)
        npl = _rms_fwd(h3, W["ple_norm"][i:i + 1], f"{t}_ple_norm")
        gpre = _mm_w128(npl, W["B"][i], PLE_GATE_BLK, name=f"{t}_ple_gate")
        pp = _mm(p[i], W["ple_w_proj"][i], name=f"{t}_ple_proj")
        h4 = _ple_fwd(h3, gpre, pp, f"{t}_ple")
        saved.append((s_a, h1, s_m, s_b, h3, npl, gpre, pp))
        h = h4
        if i == 0:
            gather.finish()
    dh, g_final, loss_cols = _final_fwd_bwd(h, W["final_norm"], target, "final")
    G = {"final_norm": g_final}
    GB = {"A": [lax.empty((N_DEV, A_ROWS, D_MODEL), BF16) for _ in range(DEPTH)],
          "B": [lax.empty((N_DEV, B_ROWS, D_MODEL), BF16) for _ in range(DEPTH)],
          "C": [lax.empty((N_DEV, C_ROWS, FF_BLK), BF16) for _ in range(DEPTH)]}
    per_layer = {n: [None] * DEPTH for n in ("ffa_norm", "mix_norm", "ffb_norm", "ple_norm", "ple_w_proj")}
    scatter = None
    for i in reversed(range(DEPTH)):
        t = f"l{i}"
        s_a, h1, s_m, s_b, h3, npl, gpre, pp = saved[i]
        dgpre, dpp = _ple_bwd(dh, gpre, pp, f"{t}_ple_bwd")
        per_layer["ple_w_proj"][i] = _mm(p[i], dpp, ta=True, name=f"{t}_ple_proj_dw")
        GB["B"][i] = _mm_w128_dw(npl, dgpre, PLE_GATE_BLK, GB["B"][i], f"{t}_ple_gate_dw")
        dnpl = _mm_w128(dgpre, W["B"][i], PLE_GATE_BLK, tb=True, name=f"{t}_ple_gate_dx")
        dh, per_layer["ple_norm"][i] = _rms_bwd(dnpl, h3, W["ple_norm"][i:i + 1], dh, f"{t}_ple_norm_bwd")
        dh, per_layer["ffb_norm"][i] = _ffn_bwd(dh, s_b, W["ffb_norm"][i:i + 1], W, GB, 1, i, f"{t}_ffb", scatter)
        dnm, g_mix = (_even_bwd if i % 2 == 0 else _odd_bwd)(dh, s_m, W, GB)
        G.update(g_mix)
        dh, per_layer["mix_norm"][i] = _rms_bwd(dnm, h1, W["mix_norm"][i:i + 1], dh, f"{t}_mix_norm_bwd")
        dh, per_layer["ffa_norm"][i] = _ffn_bwd(dh, s_a, W["ffa_norm"][i:i + 1], W, GB, 0, i, f"{t}_ffa", scatter)
        if i == DEPTH - 1:
            scatter = _Rider(_reduce_scatter_steps([GB[key][i] for key in "ABC"], "later"), "rs_later")
    for n in ("ffa_norm", "mix_norm", "ffb_norm", "ple_norm"):
        G[n] = jnp.concatenate(per_layer[n], axis=0)
    G["ple_w_proj"] = per_layer["ple_w_proj"]
    return loss_cols, dh, tuple(scatter.finish()), {key: GB[key][0] for key in "ABC"}, G


def kernel(x, p, ffa_norm, ffa_w_gate_up, ffa_w_down, mix_norm, ffb_norm, ffb_w_gate_up, ffb_w_down, ple_norm, ple_w_gate, ple_w_proj, ev_w_in, ev_sinks, ev_cq_norm, ev_w_uq, ev_ckv_norm, ev_w_ukv, ev_w_out, od_w_in, od_b_f, od_w_out, final_norm, loss_target, m_ffa_norm, m_ffa_w_gate_up, m_ffa_w_down, m_mix_norm, m_ffb_norm, m_ffb_w_gate_up, m_ffb_w_down, m_ple_norm, m_ple_w_gate, m_ple_w_proj, m_ev_w_in, m_ev_sinks, m_ev_cq_norm, m_ev_w_uq, m_ev_ckv_norm, m_ev_w_ukv, m_ev_w_out, m_od_w_in, m_od_b_f, m_od_w_out, m_final_norm, v_ffa_norm, v_ffa_w_gate_up, v_ffa_w_down, v_mix_norm, v_ffb_norm, v_ffb_w_gate_up, v_ffb_w_down, v_ple_norm, v_ple_w_gate, v_ple_w_proj, v_ev_w_in, v_ev_sinks, v_ev_cq_norm, v_ev_w_uq, v_ev_ckv_norm, v_ev_w_ukv, v_ev_w_out, v_od_w_in, v_od_b_f, v_od_w_out, v_final_norm):
    given = dict(locals())
    w_in = {n: given[n] for n in WEIGHTS}

    layers, misc = _local_groups(w_in, BF16)
    (a0, b0, c0), (a1, b1, c1) = [[_in_slot(g) for g in layer] for layer in layers]
    a0, c0 = _comm_call(_gather_job([a0, c0], rows=[(0, DOWN_ROWS), (0, D_MODEL)]), "all_gather_first")
    W = {n: w_in[n] for n in SMALL}
    W["final_norm"] = final_norm.reshape(1, -1)
    W.update(A=[a0], C=[c0])
    slots = dict(a0=a0, b0=b0, c0=c0, m=_in_slot(misc), a1=a1, b1=b1, c1=c1)

    loss_cols, dx, r_later, GB, G = _local_step(x[0], p[:, 0], loss_target[0], W, slots)

    *r_first, r_misc = _reduce_scatter([GB["A"], GB["B"], GB["C"], _misc_grads(G).astype(BF16)], "first")
    grads = _ungroup_local([tuple(r_first), r_later], r_misc)
    layout = [(n, int(np.prod(w_in[n].shape))) for n in SMALL]
    vec = jnp.concatenate([G[n].astype(F32).reshape(-1) for n, _ in layout] + [jnp.sum(loss_cols).reshape(1)])
    vec = jnp.pad(vec, (0, N_DEV * SMALL_COLS - vec.shape[0])).reshape(N_DEV, SMALL_COLS)
    vec = _all_reduce_small(vec).reshape(-1)
    off = 0
    for n, size in layout:
        grads[n] = vec[off: off + size].reshape(w_in[n].shape)
        off += size
    loss = vec[off]

    delta, new_m, new_v = {}, {}, {}
    for n in WEIGHTS:
        shp = w_in[n].shape
        as2d = (lambda a: a.reshape(1, -1)) if len(shp) == 1 else (lambda a: a)
        d, nm, nv = _adamw(as2d(w_in[n]), as2d(grads[n]), as2d(given["m_" + n]), as2d(given["v_" + n]), f"adamw_{n}")
        delta[n], new_m[n], new_v[n] = d.reshape(shp), nm.reshape(shp), nv.reshape(shp)
    return (loss, dx[None], *[grads[n] for n in WEIGHTS], *[delta[n] for n in WEIGHTS],
            *[new_m[n] for n in WEIGHTS], *[new_v[n] for n in WEIGHTS])
```

```python
import functools

import numpy as np
import jax
import jax.numpy as jnp
from jax import lax
from jax.experimental import pallas as pl
from jax.experimental.pallas import tpu as pltpu

F32 = jnp.float32
BF16 = jnp.bfloat16
MESH = pl.DeviceIdType.MESH

D_MODEL = 1024
D_FF = 2816
RMS_EPS = 1e-6
PLE_DIM = 256
A_HEADS, A_KV_HEADS, A_HEAD_DIM, WINDOW = 8, 2, 64, 128
A_GROUP = A_HEADS // A_KV_HEADS
B_HEADS, B_Q_LORA, B_KV_LORA, B_NOPE, B_ROPE, B_V = 8, 256, 128, 64, 32, 64
ROPE_THETA = 10000.0
C_HEADS, C_HEAD_DIM = 16, 64
EVEN_IN = 1184
EVEN_IN_PAD = 1280
ODD_IN = 3088
ODD_IN_AUG = 2 * 16 * 80 + 1024 + 16
ODD_IN_PAD = 3840
DEPTH = 2
ADAM_LR, ADAM_B1, ADAM_B2, ADAM_EPS, ADAM_WD, ADAM_STEP = 0.001, 0.9, 0.999, 1e-08, 0.01, 10

N_DEV = 8
LANES = 128
SUBLANES = 8
EW_TILE_BYTES = 3 << 20
MM_VMEM_BYTES = 26 << 20
NEG = -1e30
ATTN_TILE = 1024
ATTN_TILE_FWD = 1024
SWA_TILE = 256
QK_PAD = 80

FF_BLK = D_FF // 4
DOWN_ROWS = D_FF // N_DEV
A_ROWS, B_ROWS, C_ROWS, G3_ROWS, G3_COLS = 2 * DOWN_ROWS, 256, 2 * D_MODEL, 1024, 768
PLE_GATE_BLK, MIX_OUT_BLK = 0, 1
OD_C, EV_C, STRIP_C = 386, 148, 128
STRIP0 = OD_C + EV_C

SMALL = ["ffa_norm", "mix_norm", "ffb_norm", "ple_norm", "ev_sinks", "ev_cq_norm", "ev_ckv_norm", "od_b_f", "final_norm"]
WEIGHTS = ["ffa_norm", "ffa_w_gate_up", "ffa_w_down", "mix_norm", "ffb_norm", "ffb_w_gate_up", "ffb_w_down", "ple_norm",
           "ple_w_gate", "ple_w_proj", "ev_w_in", "ev_sinks", "ev_cq_norm", "ev_w_uq", "ev_ckv_norm", "ev_w_ukv", "ev_w_out",
           "od_w_in", "od_b_f", "od_w_out", "final_norm"]
SMALL_COLS = 1280


def _divisor(n, cap, mult):
    if n <= cap:
        return n
    for t in range(cap - cap % mult, 0, -mult):
        if n % t == 0:
            return t
    raise ValueError(f"no tile for {n} under {cap} in steps of {mult}")


def _lanes(c):
    return -(-c // LANES) * LANES


def _ew(fn, rows, vecs, outs, reds=(), *, name):
    R = rows[0].shape[0]
    per_row = sum(_lanes(a.shape[1]) * a.dtype.itemsize for a in rows) + sum(_lanes(c) * jnp.dtype(d).itemsize for c, d in outs)
    tm = _divisor(R, max(16, EW_TILE_BYTES // per_row // 16 * 16), 16) if R % 16 == 0 else R
    n_r, n_v, n_o = len(rows), len(vecs), len(outs)

    def body(*refs):
        ins = [r[...] for r in refs[: n_r + n_v]]
        res = fn(*ins)
        if not isinstance(res, (tuple, list)):
            res = (res,)
        o_refs = refs[n_r + n_v: n_r + n_v + n_o]
        r_refs = refs[n_r + n_v + n_o:]
        for ref, val in zip(o_refs, res[:n_o]):
            ref[...] = val.astype(ref.dtype)
        if r_refs:
            @pl.when(pl.program_id(0) == 0)
            def _():
                for ref in r_refs:
                    ref[...] = jnp.zeros_like(ref)
            for ref, val in zip(r_refs, res[n_o:]):
                ref[...] += val

    in_specs = [pl.BlockSpec((tm, a.shape[1]), lambda i: (i, 0)) for a in rows]
    in_specs += [pl.BlockSpec((1, a.shape[1]), lambda i: (0, 0)) for a in vecs]
    out_specs = [pl.BlockSpec((tm, c), lambda i: (i, 0)) for c, _ in outs]
    out_specs += [pl.BlockSpec((1, c), lambda i: (0, 0)) for c in reds]
    out_shape = [jax.ShapeDtypeStruct((R, c), d) for c, d in outs] + [jax.ShapeDtypeStruct((1, c), F32) for c in reds]
    res = pl.pallas_call(body, name=name, grid=(R // tm,), in_specs=in_specs, out_specs=out_specs, out_shape=out_shape)(*rows, *vecs)
    return res[0] if len(res) == 1 else res


def _rms_fwd(x, w, name):
    def fn(x, w):
        y = x * lax.rsqrt(jnp.mean(x * x, axis=-1, keepdims=True) + RMS_EPS)
        return y * w
    return _ew(fn, [x], [w], [(x.shape[1], BF16)], name=name)


def _rms_bwd(dn, x, w, dres, name):
    def fn(dn, x, *rest):
        w = rest[-1]
        r = lax.rsqrt(jnp.mean(x * x, axis=-1, keepdims=True) + RMS_EPS)
        xh = x * r
        gw = dn * w
        dx = r * (gw - xh * jnp.mean(gw * xh, axis=-1, keepdims=True))
        if len(rest) == 2:
            dx = dx + rest[0]
        return dx, jnp.sum(dn * xh, axis=0, keepdims=True)
    rows = [dn, x] + ([dres] if dres is not None else [])
    return _ew(fn, rows, [w], [(x.shape[1], F32)], [x.shape[1]], name=name)


def _ple_fwd(h, gpre, pp, name):
    return _ew(lambda h, g, q: h + jax.nn.sigmoid(g) * q, [h, gpre, pp], [], [(h.shape[1], F32)], name=name)


def _ple_bwd(dh, gpre, pp, name):
    def fn(dh, g, q):
        sg = jax.nn.sigmoid(g)
        return dh * q * (sg * (1.0 - sg)), dh * sg
    return _ew(fn, [dh, gpre, pp], [], [(dh.shape[1], BF16), (dh.shape[1], BF16)], name=name)


def _rope(x1, x2, cos, sin, name):
    c = x1.shape[1]
    return _ew(lambda a, b, co, si: (a * co - b * si, a * si + b * co), [x1, x2, cos, sin], [], [(c, F32), (c, F32)], name=name)


def _logsig_fwd(f, b, name):
    def fn(f, b):
        z = f + b
        return jnp.minimum(z, 0.0) - jnp.log(1.0 + jnp.exp(-jnp.abs(z)))
    return _ew(fn, [f], [b], [(f.shape[1], F32)], name=name)


def _logsig_bwd(dlogf, f, b, name):
    def fn(d, f, b):
        df = d * jax.nn.sigmoid(-(f + b))
        return df, jnp.sum(df, axis=0, keepdims=True)
    return _ew(fn, [dlogf, f], [b], [(f.shape[1], F32)], [f.shape[1]], name=name)


def _final_fwd_bwd(h, w, target, name):
    d = h.shape[1]

    def fn(h, t, w):
        r = lax.rsqrt(jnp.mean(h * h, axis=-1, keepdims=True) + RMS_EPS)
        xh = h * r
        y = xh * w
        err = y - t
        dy = err * (1.0 / d)
        gw = dy * w
        dx = r * (gw - xh * jnp.mean(gw * xh, axis=-1, keepdims=True))
        return dx, jnp.sum(dy * xh, axis=0, keepdims=True), jnp.sum(err * err, axis=0, keepdims=True) * (0.5 / d)
    return _ew(fn, [h, target], [w], [(d, F32)], [d, d], name=name)


def _adamw(w, g, m, v, name):
    shape = w.shape
    c = shape[-1]
    w2, g2, m2, v2 = (a.reshape(-1, c) for a in (w, g, m, v))

    def fn(w, g, m, v):
        m = ADAM_B1 * m + (1.0 - ADAM_B1) * g
        v = ADAM_B2 * v + (1.0 - ADAM_B2) * jnp.square(g)
        m_hat = m / (1.0 - ADAM_B1 ** ADAM_STEP)
        v_hat = v / (1.0 - ADAM_B2 ** ADAM_STEP)
        delta = -ADAM_LR * (m_hat / (jnp.sqrt(v_hat) + ADAM_EPS) + ADAM_WD * w)
        return delta, m, v
    d, nm, nv = _ew(fn, [w2, g2, m2, v2], [], [(c, F32)] * 3, name=name)
    return d.reshape(shape), nm.reshape(shape), nv.reshape(shape)


def _split3(v):
    hi = v.astype(BF16)
    r1 = v - hi.astype(F32)
    mid = r1.astype(BF16)
    lo = (r1 - mid.astype(F32)).astype(BF16)
    return hi, mid, lo


def _cumsum(x, reverse, name):
    S, C = x.shape
    tm = _divisor(S, 512, 16)
    nt = S // tm

    def body(x_ref, o_ref, carry):
        @pl.when(pl.program_id(0) == 0)
        def _():
            carry[...] = jnp.zeros_like(carry)
        r = lax.broadcasted_iota(jnp.int32, (tm, tm), 0)
        c = lax.broadcasted_iota(jnp.int32, (tm, tm), 1)
        tri = jnp.where((c >= r) if reverse else (c <= r), 1.0, 0.0).astype(BF16)
        xv = x_ref[...]
        acc = jnp.zeros((tm, C), F32)
        for part in _split3(xv):
            acc = acc + jnp.dot(tri, part, preferred_element_type=F32)
        o_ref[...] = acc + carry[...]
        carry[...] += jnp.sum(xv, axis=0, keepdims=True)

    idx = (lambda i: (nt - 1 - i, 0)) if reverse else (lambda i: (i, 0))
    return pl.pallas_call(
        body, name=name, grid=(nt,), in_specs=[pl.BlockSpec((tm, C), idx)], out_specs=pl.BlockSpec((tm, C), idx),
        out_shape=jax.ShapeDtypeStruct((S, C), F32), scratch_shapes=[pltpu.VMEM((1, C), F32)],
    )(x)


NN = (((1,), (0,)), ((), ()))
NT = (((1,), (1,)), ((), ()))
TN = (((0,), (0,)), ((), ()))

HBM_SPEC = pl.BlockSpec(memory_space=pl.ANY)


def _job_in_body(job, refs, n_in, n_out, n_scr, grid):
    if job is None:
        return refs[n_in:], lambda: None
    ji, jo = len(job["ins"]), len(job["outs"])
    j_in = refs[n_in: n_in + ji]
    pos = n_in + ji
    own = list(refs[pos: pos + n_out])
    pos += n_out
    j_out = refs[pos: pos + jo]
    pos += jo
    own += list(refs[pos: pos + n_scr])
    ss, rs = refs[-2], refs[-1]
    first = functools.reduce(jnp.logical_and, [pl.program_id(d) == 0 for d in range(len(grid))])
    last = functools.reduce(jnp.logical_and, [pl.program_id(d) == n - 1 for d, n in enumerate(grid)])

    @pl.when(first)
    def _():
        job["start"](j_in, j_out, ss, rs)

    def finish():
        @pl.when(last)
        def _():
            job["finish"](j_in, j_out, ss, rs)

    return own, finish


def _job_call(job, body, *, name, grid, in_specs, out_specs, out_shape, args, scratch_shapes, aliases, dimension_semantics):
    in_specs, out_specs, out_shape, args, scratch_shapes = list(in_specs), list(out_specs), list(out_shape), list(args), list(scratch_shapes)
    aliases = dict(aliases)
    if job is not None:
        for i, o in job["aliases"].items():
            aliases[len(args) + i] = len(out_shape) + o
        in_specs += [HBM_SPEC] * len(job["ins"])
        args += list(job["ins"])
        out_specs += [HBM_SPEC] * len(job["outs"])
        out_shape += list(job["outs"])
        scratch_shapes += [pltpu.SemaphoreType.DMA((job["n_sems"],)), pltpu.SemaphoreType.DMA((job["n_sems"],))]
    return pl.pallas_call(
        body, name=name, grid=grid, in_specs=in_specs, out_specs=out_specs, out_shape=out_shape,
        scratch_shapes=scratch_shapes, input_output_aliases=aliases,
        compiler_params=pltpu.CompilerParams(dimension_semantics=dimension_semantics),
    )(*args)


def _comm_call(job, name):
    def body(*refs):
        ji, jo = len(job["ins"]), len(job["outs"])
        job["start"](refs[:ji], refs[ji: ji + jo], refs[-2], refs[-1])
        job["finish"](refs[:ji], refs[ji: ji + jo], refs[-2], refs[-1])

    return pl.pallas_call(
        body, name=name, in_specs=[HBM_SPEC] * len(job["ins"]), out_specs=[HBM_SPEC] * len(job["outs"]), out_shape=list(job["outs"]),
        input_output_aliases=dict(job["aliases"]),
        scratch_shapes=[pltpu.SemaphoreType.DMA((job["n_sems"],)), pltpu.SemaphoreType.DMA((job["n_sems"],))],
    )(*job["ins"])


def _mm_call(name, grid, k_axis, a, a_spec, a2d, b, b_spec, b2d, dims, out_sds, out_spec, o2d, *,
             alpha=1.0, res=None, res_spec=None, into=None, job=None, norm_bwd=None):
    nk = grid[k_axis]
    n_in = 2 + (res is not None) + (into is not None) + (3 if norm_bwd is not None else 0)
    n_out = 2 if norm_bwd is not None else 1

    def body(*refs):
        a_ref, b_ref = refs[0], refs[1]
        res_ref = refs[2] if res is not None else None
        own, finish_job = _job_in_body(job, refs, n_in, n_out, 1, grid)
        o_ref, acc_ref = own[0], own[-1]
        k = pl.program_id(k_axis)

        @pl.when(k == 0)
        def _():
            acc_ref[...] = jnp.zeros_like(acc_ref)

        if norm_bwd is not None:
            x_ref, w_ref, dres_ref = refs[n_in - 3: n_in]
            dw_ref = own[1]

            @pl.when(functools.reduce(jnp.logical_and, [pl.program_id(d) == 0 for d in range(len(grid))]))
            def _():
                dw_ref[...] = jnp.zeros_like(dw_ref)

        av = a_ref[...].reshape(a2d).astype(BF16)
        bv = b_ref[...].reshape(b2d).astype(BF16)
        acc_ref[...] += lax.dot_general(av, bv, dims, preferred_element_type=F32)

        @pl.when(k == nk - 1)
        def _():
            r = acc_ref[...]
            if alpha != 1.0:
                r = r * alpha
            if res_ref is not None:
                r = res_ref[...].reshape(o2d) + r
            if norm_bwd is not None:
                x = x_ref[...]
                rs = lax.rsqrt(jnp.mean(x * x, axis=-1, keepdims=True) + RMS_EPS)
                xh = x * rs
                gw = r * w_ref[...]
                dw_ref[...] += jnp.sum(r * xh, axis=0, keepdims=True)
                r = dres_ref[...] + rs * (gw - xh * jnp.mean(gw * xh, axis=-1, keepdims=True))
            o_ref[...] = r.reshape(o_ref.shape).astype(o_ref.dtype)

        finish_job()

    in_specs, args = [a_spec, b_spec], [a, b]
    if res is not None:
        in_specs.append(res_spec)
        args.append(res)
    aliases = {}
    if into is not None:
        aliases = {len(args): 0}
        in_specs.append(pl.BlockSpec(memory_space=pl.ANY))
        args.append(into)
        out_sds = jax.ShapeDtypeStruct(into.shape, into.dtype)
    out_specs, out_shape = [out_spec], [out_sds]
    if norm_bwd is not None:
        vec = pl.BlockSpec((1, o2d[1]), lambda *_: (0, 0))
        in_specs += [out_spec, vec, out_spec]
        args += list(norm_bwd)
        out_specs.append(vec)
        out_shape.append(jax.ShapeDtypeStruct((1, o2d[1]), F32))
    serial = job is not None or norm_bwd is not None
    sem = tuple("arbitrary" if d == k_axis or serial else "parallel" for d in range(len(grid)))
    res_all = _job_call(
        job, body, name=name, grid=grid, in_specs=in_specs, out_specs=out_specs, out_shape=out_shape, args=args,
        scratch_shapes=[pltpu.VMEM(o2d, F32)], aliases=aliases, dimension_semantics=sem)
    own = res_all[0] if n_out == 1 else tuple(res_all[:n_out])
    return own if job is None else (own, res_all[n_out:])


def _mm(a, b, *, ta=False, tb=False, out=F32, res=None, alpha=1.0, norm_bwd=None, name):
    K, M = a.shape if ta else a.shape[::-1]
    N = b.shape[0] if tb else b.shape[1]
    assert (b.shape[1] if tb else b.shape[0]) == K, (a.shape, b.shape, ta, tb)
    tk = _divisor(K, 1024, LANES)
    tn = _divisor(N, 1408, LANES)
    assert norm_bwd is None or tn == N
    for cap in (1024, 512, 256, 128):
        tm = _divisor(M, cap, LANES if ta else 16)
        est = 2 * (tm * tk * a.dtype.itemsize + tk * tn * b.dtype.itemsize + tm * tn * jnp.dtype(out).itemsize)
        est += tm * tn * 4 + (2 * tm * tn * 4 if res is not None else 0) + (4 * tm * tn * 4 if norm_bwd is not None else 0)
        if est <= MM_VMEM_BYTES:
            break
    a_spec = pl.BlockSpec((tk, tm), lambda i, j, k: (k, i)) if ta else pl.BlockSpec((tm, tk), lambda i, j, k: (i, k))
    b_spec = pl.BlockSpec((tn, tk), lambda i, j, k: (j, k)) if tb else pl.BlockSpec((tk, tn), lambda i, j, k: (k, j))
    o_spec = pl.BlockSpec((tm, tn), lambda i, j, k: (i, j))
    dims = (((0 if ta else 1,), (1 if tb else 0,)), ((), ()))
    return _mm_call(name, (M // tm, N // tn, K // tk), 2, a, a_spec, (tk, tm) if ta else (tm, tk), b, b_spec,
                    (tn, tk) if tb else (tk, tn), dims, jax.ShapeDtypeStruct((M, N), out), o_spec, (tm, tn),
                    alpha=alpha, res=res, res_spec=o_spec, norm_bwd=norm_bwd)


def _w128_spec(blk):
    return pl.BlockSpec((N_DEV, 128, D_MODEL), lambda *_: (0, blk, 0))


def _mm_w128(a, G1, blk, *, tb=False, res=None, out=F32, norm_bwd=None, name):
    S = a.shape[0]
    tm = _divisor(S, 512, 16)
    row = pl.BlockSpec((tm, D_MODEL), lambda i, k: (i, 0))
    return _mm_call(name, (S // tm, 1), 1, a, row, (tm, D_MODEL), G1, _w128_spec(blk), (D_MODEL, D_MODEL), NT if tb else NN,
                    jax.ShapeDtypeStruct((S, D_MODEL), out), row, (tm, D_MODEL), res=res, res_spec=row, norm_bwd=norm_bwd)


def _mm_w128_dw(a, b, blk, into, name):
    S = a.shape[0]
    tk = _divisor(S, 1024, 16)
    row = pl.BlockSpec((tk, D_MODEL), lambda i, k: (k, 0))
    return _mm_call(name, (1, S // tk), 1, a, row, (tk, D_MODEL), b, row, (tk, D_MODEL), TN, None, _w128_spec(blk),
                    (D_MODEL, D_MODEL), into=into)


def _ffn_gate_up(h, norm_w, G2v, rb, name, job=None):
    S = h.shape[0]
    tm = _divisor(S, 1024, 16)
    grid = (S // tm, 4)

    def body(*refs):
        h_ref, nw_ref, w_ref = refs[:3]
        (n_ref, gu_ref, act_ref, n_scr), finish_job = _job_in_body(job, refs, 3, 3, 1, grid)

        @pl.when(pl.program_id(1) == 0)
        def _():
            x = h_ref[...]
            y = x * lax.rsqrt(jnp.mean(x * x, axis=-1, keepdims=True) + RMS_EPS)
            n_scr[...] = (y * nw_ref[...]).astype(BF16)
            n_ref[...] = n_scr[...]

        nv = n_scr[...]
        g = jnp.dot(nv, w_ref[0, 0], preferred_element_type=F32)
        u = jnp.dot(nv, w_ref[1, 0], preferred_element_type=F32)
        gu_ref[0, 0] = g.astype(BF16)
        gu_ref[1, 0] = u.astype(BF16)
        act_ref[0] = (g * jax.nn.sigmoid(g) * u).astype(BF16)
        finish_job()

    row = pl.BlockSpec((tm, D_MODEL), lambda i, j: (i, 0))
    return _job_call(
        job, body, name=name, grid=grid,
        in_specs=[row, pl.BlockSpec((1, D_MODEL), lambda i, j: (0, 0)), pl.BlockSpec((2, 1, D_MODEL, FF_BLK), lambda i, j: (0, j, rb, 0))],
        out_specs=[row, pl.BlockSpec((2, 1, tm, FF_BLK), lambda i, j: (0, j, i, 0)), pl.BlockSpec((1, tm, FF_BLK), lambda i, j: (j, i, 0))],
        out_shape=[jax.ShapeDtypeStruct((S, D_MODEL), BF16), jax.ShapeDtypeStruct((2, 4, S, FF_BLK), BF16), jax.ShapeDtypeStruct((4, S, FF_BLK), BF16)],
        args=[h, norm_w, G2v], scratch_shapes=[pltpu.VMEM((tm, D_MODEL), BF16)], aliases={},
        dimension_semantics=("arbitrary" if job is not None else "parallel", "arbitrary"))


def _ffn_down(act, G1, ob, h, name, job=None):
    S = h.shape[0]
    tm = _divisor(S, 512, 16)
    row = pl.BlockSpec((tm, D_MODEL), lambda i, k: (i, 0))
    return _mm_call(name, (S // tm, 4), 1, act, pl.BlockSpec((1, tm, FF_BLK), lambda i, k: (k, i, 0)), (tm, FF_BLK),
                    G1, pl.BlockSpec((2, DOWN_ROWS, D_MODEL), lambda i, k: (k, ob, 0)), (FF_BLK, D_MODEL), NN,
                    jax.ShapeDtypeStruct((S, D_MODEL), F32), row, (tm, D_MODEL), alpha=0.5, res=h, res_spec=row, job=job)


def _ffn_down_dx(dh, G1, ob, gu, name):
    S = dh.shape[0]
    tm = _divisor(S, 512, 16)

    def body(dh_ref, w_ref, gu_ref, o_ref):
        w = w_ref[...].reshape(FF_BLK, D_MODEL)
        dact = lax.dot_general(dh_ref[...].astype(BF16), w, NT, preferred_element_type=F32) * 0.5
        g = gu_ref[0, 0].astype(F32)
        u = gu_ref[1, 0].astype(F32)
        sg = jax.nn.sigmoid(g)
        o_ref[0, 0] = (dact * u * (sg * (1.0 + g * (1.0 - sg)))).astype(BF16)
        o_ref[1, 0] = (dact * (g * sg)).astype(BF16)

    blk = pl.BlockSpec((2, 1, tm, FF_BLK), lambda j, i: (0, j, i, 0))
    return pl.pallas_call(
        body, name=name, grid=(4, S // tm),
        in_specs=[pl.BlockSpec((tm, D_MODEL), lambda j, i: (i, 0)), pl.BlockSpec((2, DOWN_ROWS, D_MODEL), lambda j, i: (j, ob, 0)), blk],
        out_specs=blk, out_shape=jax.ShapeDtypeStruct((2, 4, S, FF_BLK), BF16),
    )(dh, G1, gu)


def _ffn_down_dw(act, dh, ob, into, name, job=None):
    S = dh.shape[0]
    tk = _divisor(S, 1024, 16)
    return _mm_call(name, (4, S // tk), 1, act, pl.BlockSpec((1, tk, FF_BLK), lambda j, k: (j, k, 0)), (tk, FF_BLK),
                    dh, pl.BlockSpec((tk, D_MODEL), lambda j, k: (k, 0)), (tk, D_MODEL), TN, None,
                    pl.BlockSpec((2, DOWN_ROWS, D_MODEL), lambda j, k: (j, ob, 0)), (FF_BLK, D_MODEL), alpha=0.5, into=into, job=job)


def _ffn_gate_up_dw(n, dgu8, rb, into, name, job=None):
    S = n.shape[0]
    tk = _divisor(S, 1024, 16)
    return _mm_call(name, (N_DEV, S // tk), 1, n, pl.BlockSpec((tk, D_MODEL), lambda b, k: (k, 0)), (tk, D_MODEL),
                    dgu8, pl.BlockSpec((1, tk, FF_BLK), lambda b, k: (b, k, 0)), (tk, FF_BLK), TN, None,
                    pl.BlockSpec((1, D_MODEL, FF_BLK), lambda b, k: (b, rb, 0)), (D_MODEL, FF_BLK), into=into, job=job)


def _ffn_gate_up_dx(dgu8, G2, rb, h, norm_w, dres, name):
    S = h.shape[0]
    tm = _divisor(S, 512, 16)

    def body(a_ref, w_ref, h_ref, nw_ref, r_ref, o_ref, dw_ref, acc_ref):
        i, k = pl.program_id(0), pl.program_id(1)

        @pl.when(k == 0)
        def _():
            acc_ref[...] = jnp.zeros_like(acc_ref)

        @pl.when((k == 0) & (i == 0))
        def _():
            dw_ref[...] = jnp.zeros_like(dw_ref)

        acc_ref[...] += lax.dot_general(a_ref[0], w_ref[0], NT, preferred_element_type=F32)

        @pl.when(k == N_DEV - 1)
        def _():
            dn = acc_ref[...]
            x = h_ref[...]
            r = lax.rsqrt(jnp.mean(x * x, axis=-1, keepdims=True) + RMS_EPS)
            xh = x * r
            gw = dn * nw_ref[...]
            o_ref[...] = r_ref[...] + r * (gw - xh * jnp.mean(gw * xh, axis=-1, keepdims=True))
            dw_ref[...] += jnp.sum(dn * xh, axis=0, keepdims=True)

    row = pl.BlockSpec((tm, D_MODEL), lambda i, k: (i, 0))
    vec = pl.BlockSpec((1, D_MODEL), lambda i, k: (0, 0))
    return pl.pallas_call(
        body, name=name, grid=(S // tm, N_DEV),
        in_specs=[pl.BlockSpec((1, tm, FF_BLK), lambda i, k: (k, i, 0)), pl.BlockSpec((1, D_MODEL, FF_BLK), lambda i, k: (k, rb, 0)), row, vec, row],
        out_specs=[row, vec], out_shape=[jax.ShapeDtypeStruct((S, D_MODEL), F32), jax.ShapeDtypeStruct((1, D_MODEL), F32)],
        scratch_shapes=[pltpu.VMEM((tm, D_MODEL), F32)],
        compiler_params=pltpu.CompilerParams(dimension_semantics=("arbitrary", "arbitrary")),
    )(dgu8, G2, h, norm_w, dres)


def _unheads(x):
    h, S, d = x.shape
    return jnp.transpose(x, (1, 0, 2)).reshape(S, h * d)


def _exact3(v):
    rnd = lambda a: lax.reduce_precision(a, exponent_bits=8, mantissa_bits=7)
    hi = rnd(v)
    mid = rnd(v - hi)
    return hi, mid, rnd(v - hi - mid)


def _causal_mask(st, i, j, tq, tk, window):
    dist = (i * tq + lax.broadcasted_iota(jnp.int32, (tk, tq), 1)) - (j * tk + lax.broadcasted_iota(jnp.int32, (tk, tq), 0))
    mask = dist >= 0
    if window is not None:
        mask = mask & (dist < window)
    return jnp.where(mask, st, NEG)


def _attn_fwd(qT, k, vT1, *, tile, hb, window=None, sink=None, name, job=None):
    H, dqk, S = qT.shape
    G = H // k.shape[0]
    dvp = vT1.shape[1]
    dv = dvp - 16
    tq = tk = tile
    assert H % hb == 0 and (G == 1 or G % hb == 0)
    kvb = hb if G == 1 else 1
    grid = (H // hb, S // tq)
    n_in = 3 + (sink is not None)

    def body(*refs):
        q_ref, k_ref, v_ref = refs[:3]
        (o_ref, lse_ref), finish_job = _job_in_body(job, refs, n_in, 2, 0, grid)
        i = pl.program_id(1)
        carry = []
        for a in range(hb):
            if sink is not None:
                carry.append(jnp.zeros((1, tq), F32) + refs[3][a, :, 0:1])
                carry.append(jnp.where(lax.broadcasted_iota(jnp.int32, (dvp, tq), 0) == dv, 1.0, 0.0))
            else:
                carry.append(jnp.full((1, tq), NEG, F32))
                carry.append(jnp.zeros((dvp, tq), F32))

        def step(j, carry, masked):
            off = pl.multiple_of(j * tk, tk)
            out = []
            for a in range(hb):
                m, acc = carry[2 * a], carry[2 * a + 1]
                kv = a if kvb > 1 else 0
                st = jnp.dot(k_ref[kv, pl.ds(off, tk), :], q_ref[a], preferred_element_type=F32)
                if masked:
                    st = _causal_mask(st, i, j, tq, tk, window)
                m_new = jnp.maximum(m, jnp.max(st, axis=0, keepdims=True))
                pt = jnp.exp(st - m_new).astype(BF16)
                acc = jnp.exp(m - m_new) * acc + jnp.dot(v_ref[kv, :, pl.ds(off, tk)], pt, preferred_element_type=F32)
                out += [m_new, acc]
            return tuple(out)

        carry = tuple(carry)
        if window is None:
            carry = lax.fori_loop(0, i, functools.partial(step, masked=False), carry)
            carry = step(i, carry, True)
        else:
            lo = jnp.maximum((i * tq - (window - 1)) // tk, 0)
            carry = lax.fori_loop(lo, i + 1, functools.partial(step, masked=True), carry)
        for a in range(hb):
            m, acc = carry[2 * a], carry[2 * a + 1]
            l = acc[dv:dv + 1, :]
            o_ref[a] = acc[:dv, :] / l
            lse_ref[a] = m + jnp.log(l)
        finish_job()

    kv_idx = (lambda b: b) if G == 1 else (lambda b: (b * hb) // G)
    in_specs = [
        pl.BlockSpec((hb, dqk, tq), lambda b, i: (b, 0, i)),
        pl.BlockSpec((kvb, S, dqk), lambda b, i: (kv_idx(b), 0, 0)),
        pl.BlockSpec((kvb, dvp, S), lambda b, i: (kv_idx(b), 0, 0)),
    ]
    args = [qT, k, vT1]
    if sink is not None:
        in_specs += [pl.BlockSpec((hb, 1, LANES), lambda b, i: (b, 0, 0))]
        args += [sink]
    return _job_call(
        job, body, name=name, grid=grid, in_specs=in_specs,
        out_specs=[pl.BlockSpec((hb, dv, tq), lambda b, i: (b, 0, i)), pl.BlockSpec((hb, 1, tq), lambda b, i: (b, 0, i))],
        out_shape=[jax.ShapeDtypeStruct((H, dv, S), F32), jax.ShapeDtypeStruct((H, 1, S), F32)],
        args=args, scratch_shapes=[], aliases={}, dimension_semantics=("arbitrary", "arbitrary") if job is not None else ("parallel", "parallel"))


def _attn_bwd(q, qT, k, kT, v, oT, do, doT, lse, *, tile, hb, window=None, sink=None, real=None, extra=False, full=False, name):
    H, S, dqk = q.shape
    G = H // k.shape[0]
    dv = v.shape[2]
    tq = tk = tile
    nq = S // tq
    has_p = sink is not None
    real = dqk if real is None else real
    main = dqk if full else real
    assert H % hb == 0 and (G == 1 or G % hb == 0) and not (extra and real == dqk)
    kvb = hb if G == 1 else 1

    def body(*refs):
        q_ref, qT_ref, k_ref, kT_ref, v_ref, oT_ref, do_ref, doT_ref, lse_ref = refs[:9]
        p_ref = refs[9] if has_p else None
        pos = 10 if has_p else 9
        dq_ref, dk_ref, dv_ref = refs[pos: pos + 3]
        pos += 3
        ds_ref = refs[pos] if has_p else None
        pos += has_p
        dqx_ref, dkx_ref = (refs[pos], refs[pos + 1]) if extra else (None, None)
        delta = refs[-1]
        j = pl.program_id(1)

        @pl.when(j == 0)
        def _():
            dq_ref[...] = jnp.zeros_like(dq_ref)
            if extra:
                dqx_ref[...] = jnp.zeros_like(dqx_ref)
            for a in range(hb):
                drow = jnp.sum(doT_ref[a].astype(F32) * oT_ref[a], axis=0, keepdims=True)
                delta[a] = drow
                if has_p:
                    w = jnp.exp(p_ref[a, :, 0:1] - lse_ref[a])
                    ds_ref[a] = jnp.zeros((1, LANES), F32) - jnp.sum(w * drow, axis=1, keepdims=True)

        def step(i, carry, masked):
            off = pl.multiple_of(i * tq, tq)
            out = []
            for a in range(hb):
                dk, dvv = carry[2 * a], carry[2 * a + 1]
                kv = a if kvb > 1 else 0
                st = jnp.dot(k_ref[kv], qT_ref[a, :, pl.ds(off, tq)], preferred_element_type=F32)
                if masked:
                    st = _causal_mask(st, i, j, tq, tk, window)
                pt = jnp.exp(st - lse_ref[a, :, pl.ds(off, tq)])
                dvv = dvv + jnp.dot(pt.astype(BF16), do_ref[a, pl.ds(off, tq), :], preferred_element_type=F32)
                dpt = jnp.dot(v_ref[kv], doT_ref[a, :, pl.ds(off, tq)], preferred_element_type=F32)
                dsb = (pt * (dpt - delta[a, :, pl.ds(off, tq)])).astype(BF16)
                dk = dk + jnp.dot(dsb, q_ref[a, pl.ds(off, tq), :], preferred_element_type=F32)
                dqt = jnp.dot(kT_ref[kv], dsb, preferred_element_type=F32)
                dq_ref[a, :, pl.ds(off, tq)] += dqt[:main]
                if extra:
                    dqx_ref[a, :, pl.ds(off, tq)] += dqt[real:]
                out += [dk, dvv]
            return tuple(out)

        carry = (jnp.zeros((tk, dqk), F32), jnp.zeros((tk, dv), F32)) * hb
        if window is None:
            carry = step(j, carry, True)
            carry = lax.fori_loop(j + 1, nq, functools.partial(step, masked=False), carry)
        else:
            hi = jnp.minimum(nq - 1, ((j + 1) * tk + window - 2) // tq)
            carry = lax.fori_loop(j, hi + 1, functools.partial(step, masked=True), carry)
        for a in range(hb):
            dk_ref[a] = carry[2 * a][:, :main]
            if extra:
                dkx_ref[a] = carry[2 * a][:, real:]
            dv_ref[a] = carry[2 * a + 1]

    kv_idx = (lambda b: b) if G == 1 else (lambda b: (b * hb) // G)
    rows = lambda d: pl.BlockSpec((hb, S, d), lambda b, j: (b, 0, 0))
    colsT = lambda d: pl.BlockSpec((hb, d, S), lambda b, j: (b, 0, 0))
    in_specs = [
        rows(dqk), colsT(dqk),
        pl.BlockSpec((kvb, tk, dqk), lambda b, j: (kv_idx(b), j, 0)),
        pl.BlockSpec((kvb, dqk, tk), lambda b, j: (kv_idx(b), 0, j)),
        pl.BlockSpec((kvb, tk, dv), lambda b, j: (kv_idx(b), j, 0)),
        colsT(dv), rows(dv), colsT(dv),
        pl.BlockSpec((hb, 1, S), lambda b, j: (b, 0, 0)),
    ]
    args = [q, qT, k, kT, v, oT, do, doT, lse]
    if has_p:
        in_specs += [pl.BlockSpec((hb, 1, LANES), lambda b, j: (b, 0, 0))]
        args += [sink]
    out_specs = [colsT(main), pl.BlockSpec((hb, tk, main), lambda b, j: (b, j, 0)), pl.BlockSpec((hb, tk, dv), lambda b, j: (b, j, 0))]
    out_shape = [jax.ShapeDtypeStruct((H, main, S), F32), jax.ShapeDtypeStruct((H, S, main), F32), jax.ShapeDtypeStruct((H, S, dv), F32)]
    if has_p:
        out_specs += [pl.BlockSpec((hb, 1, LANES), lambda b, j: (b, 0, 0))]
        out_shape += [jax.ShapeDtypeStruct((H, 1, LANES), F32)]
    if extra:
        out_specs += [colsT(dqk - real), pl.BlockSpec((hb, tk, dqk - real), lambda b, j: (b, j, 0))]
        out_shape += [jax.ShapeDtypeStruct((H, dqk - real, S), F32), jax.ShapeDtypeStruct((H, S, dqk - real), F32)]
    return pl.pallas_call(
        body, name=name, grid=(H // hb, S // tk), in_specs=in_specs, out_specs=out_specs, out_shape=out_shape,
        scratch_shapes=[pltpu.VMEM((hb, 1, S), F32)],
        compiler_params=pltpu.CompilerParams(dimension_semantics=("parallel", "arbitrary")),
    )(*args)


def _rows_and_cols(x3):
    xb = x3.astype(BF16)
    return jnp.transpose(xb, (1, 0, 2)), jnp.transpose(xb, (1, 2, 0))


def _v_with_ones(v3):
    S, h, _ = v3.shape
    vT = jnp.transpose(v3.astype(BF16), (1, 2, 0))
    return jnp.concatenate([vT, jnp.ones((h, 1, S), BF16), jnp.zeros((h, 15, S), BF16)], axis=1)


def _from_T(oT):
    h, d, S = oT.shape
    return jnp.transpose(oT, (2, 0, 1)).reshape(S, h * d)


def _coords():
    return lax.axis_index("x"), lax.axis_index("y"), lax.axis_index("c")


def _peer(axis):
    x, y, c = _coords()
    return {"x": (1 - x, y, c), "y": (x, 1 - y, c), "c": (x, y, 1 - c)}[axis]


def _gather_job(bufs, rows=None):
    n = len(bufs)

    def copies(outs, send_sems, recv_sems):
        x, y, c = _coords()
        me, sibling = (x, y, c), (x, y, 1 - c)
        chips = [(1 - x, y), (x, 1 - y), (1 - x, 1 - y)]

        def copy(t, k, block, to):
            px, py, pc = block
            ref = outs[t].at[4 * px + 2 * py + pc]
            if rows is not None and rows[t] is not None:
                ref = ref.at[pl.ds(rows[t][0], rows[t][1])]
            return pltpu.make_async_remote_copy(ref, ref, send_sems.at[7 * t + k], recv_sems.at[7 * t + k], device_id=to, device_id_type=MESH)

        return copy, me, sibling, chips, c

    def start(ins, outs, send_sems, recv_sems):
        copy, me, sibling, chips, c = copies(outs, send_sems, recv_sems)
        for t in range(n):
            copy(t, 0, me, sibling).start()
            for j, chip in enumerate(chips):
                copy(t, 1 + j, me, (*chip, c)).start()

    def finish(ins, outs, send_sems, recv_sems):
        copy, me, sibling, chips, c = copies(outs, send_sems, recv_sems)
        for j, chip in enumerate(chips):
            for t in range(n):
                copy(t, 1 + j, (*chip, c), me).wait_recv()
                copy(t, 4 + j, (*chip, c), sibling).start()
        for t in range(n):
            copy(t, 0, sibling, me).wait_recv()
            for j, chip in enumerate(chips):
                copy(t, 4 + j, (*chip, 1 - c), me).wait_recv()
        for t in range(n):
            copy(t, 0, me, sibling).wait_send()
            for j, chip in enumerate(chips):
                copy(t, 1 + j, me, (*chip, c)).wait_send()
                copy(t, 4 + j, (*chip, c), sibling).wait_send()

    return dict(ins=list(bufs), outs=[jax.ShapeDtypeStruct(b.shape, b.dtype) for b in bufs], aliases={t: t for t in range(n)},
                n_sems=7 * n, start=start, finish=finish)


def _in_slot(local):
    x, y, c = _coords()
    buf = jnp.zeros((N_DEV,) + local.shape, local.dtype)
    return lax.dynamic_update_slice(buf, local[None], (4 * x + 2 * y + c, 0, 0))


def _pair_job(vs, axes):
    n = len(vs)
    axes = [axes] * n if isinstance(axes, str) else axes

    def copies(ins, outs, send_sems, recv_sems):
        out = []
        for t in range(n):
            me = lax.axis_index(axes[t])
            src = ins[t].at[1 - me] if len(ins[t].shape) == 3 else ins[t].at[:, 1 - me]
            out.append(pltpu.make_async_remote_copy(src, outs[t], send_sems.at[t], recv_sems.at[t], device_id=_peer(axes[t]), device_id_type=MESH))
        return out

    def start(*refs):
        for cp in copies(*refs):
            cp.start()

    def finish(*refs):
        for cp in copies(*refs):
            cp.wait()

    return dict(ins=list(vs), outs=[jax.ShapeDtypeStruct(v.shape[:-3] + v.shape[-2:], v.dtype) for v in vs], aliases={}, n_sems=n,
                start=start, finish=finish)


def _add_kept(v, got, axis, out, name):
    R, C = v.shape[-2:]
    lead = v.shape[0] if v.ndim == 4 else 1
    tm = _divisor(R, max(16, EW_TILE_BYTES // (_lanes(C) * (v.dtype.itemsize + got.dtype.itemsize + jnp.dtype(out).itemsize)) // 16 * 16), 16)
    me = lax.axis_index(axis).astype(jnp.int32).reshape(1)
    v4 = v.reshape(lead, 2, R, C)
    g3 = got.reshape(lead, R, C)

    def body(me_ref, v_ref, g_ref, o_ref):
        o_ref[...] = (v_ref[0].astype(F32) + g_ref[...].astype(F32)).astype(o_ref.dtype)

    res = pl.pallas_call(
        body, name=name, out_shape=jax.ShapeDtypeStruct((lead, R, C), out),
        grid_spec=pltpu.PrefetchScalarGridSpec(
            num_scalar_prefetch=1, grid=(lead, R // tm),
            in_specs=[pl.BlockSpec((1, 1, tm, C), lambda b, i, me: (b, me[0], i, 0)), pl.BlockSpec((1, tm, C), lambda b, i, me: (b, i, 0))],
            out_specs=pl.BlockSpec((1, tm, C), lambda b, i, me: (b, i, 0))),
    )(me, v4, g3)
    return res


def _cross_job(vs):
    n = len(vs)

    def copies(ins, outs, send_sems, recv_sems):
        x, y, _ = _coords()
        out = []
        for t in range(n):
            h = ins[t].shape[2] // 2
            out.append(pltpu.make_async_remote_copy(ins[t].at[1 - x, :, pl.ds(0, h)], outs[2 * t], send_sems.at[2 * t], recv_sems.at[2 * t],
                                                    device_id=_peer("x"), device_id_type=MESH))
            out.append(pltpu.make_async_remote_copy(ins[t].at[:, 1 - y, pl.ds(h, h)], outs[2 * t + 1], send_sems.at[2 * t + 1], recv_sems.at[2 * t + 1],
                                                    device_id=_peer("y"), device_id_type=MESH))
        return out

    def start(*refs):
        for cp in copies(*refs):
            cp.start()

    def finish(*refs):
        for cp in copies(*refs):
            cp.wait()

    outs = []
    for v in vs:
        outs += [jax.ShapeDtypeStruct((2, v.shape[2] // 2, v.shape[3]), v.dtype)] * 2
    return dict(ins=list(vs), outs=outs, aliases={}, n_sems=2 * n, start=start, finish=finish)


def _add_picked(v, got, axis, out, name):
    _, _, R, C = v.shape
    h = R // 2
    tm = _divisor(h, max(16, EW_TILE_BYTES // (_lanes(C) * (v.dtype.itemsize + got.dtype.itemsize + jnp.dtype(out).itemsize)) // 16 * 16), 16)
    me = lax.axis_index(axis).astype(jnp.int32).reshape(1)
    if axis == "x":
        v_map = lambda b, i, me: (me[0], b, i, 0)
    else:
        v_map = lambda b, i, me: (b, me[0], i + h // tm, 0)

    def body(me_ref, v_ref, g_ref, o_ref):
        o_ref[...] = (v_ref[0].astype(F32) + g_ref[...].astype(F32)).astype(o_ref.dtype)

    return pl.pallas_call(
        body, name=name, out_shape=jax.ShapeDtypeStruct((2, h, C), out),
        grid_spec=pltpu.PrefetchScalarGridSpec(
            num_scalar_prefetch=1, grid=(2, h // tm),
            in_specs=[pl.BlockSpec((1, 1, tm, C), v_map), pl.BlockSpec((1, tm, C), lambda b, i, me: (b, i, 0))],
            out_specs=pl.BlockSpec((1, tm, C), lambda b, i, me: (b, i, 0))),
    )(me, v, got)


def _reduce_scatter_steps(gs, tag):
    n = len(gs)
    vs = [g.reshape(4, 2, *g.shape[1:]) for g in gs]
    got = yield _pair_job(vs, "c")
    vs = [_add_kept(v, r, "c", BF16, f"rs_{tag}_add_c{t}") for t, (v, r) in enumerate(zip(vs, got))]
    vs = [v.reshape(2, 2, v.shape[1], v.shape[2]) for v in vs]
    got = yield _cross_job(vs)
    up = [_add_picked(v, r, "x", BF16, f"rs_{tag}_add_x{t}") for t, (v, r) in enumerate(zip(vs, got[0::2]))]
    lo = [_add_picked(v, r, "y", BF16, f"rs_{tag}_add_y{t}") for t, (v, r) in enumerate(zip(vs, got[1::2]))]
    got = yield _pair_job(up + lo, ["y"] * n + ["x"] * n)
    out = []
    for t in range(n):
        a = _add_kept(up[t], got[t], "y", F32, f"rs_{tag}_add_y2{t}")[0]
        b = _add_kept(lo[t], got[n + t], "x", F32, f"rs_{tag}_add_x2{t}")[0]
        out.append(jnp.concatenate([a, b], axis=0))
    return out


def _reduce_scatter(gs, tag):
    steps = _reduce_scatter_steps(gs, tag)
    job = next(steps)
    for stage in ("c", "xy", "yx"):
        got = _comm_call(job, f"rs_{tag}_{stage}")
        try:
            job = steps.send(got)
        except StopIteration as done:
            return done.value


def _all_reduce_small(v):
    def body(v_ref, o_ref, buf, send_sems, recv_sems):
        x, y, c = _coords()
        me = 4 * x + 2 * y + c
        buf[me] = v_ref[...]
        copies = []
        for k in range(1, N_DEV):
            peer = tuple((1 - a) if (k >> s) & 1 else a for a, s in ((x, 2), (y, 1), (c, 0)))
            cp = pltpu.make_async_remote_copy(v_ref, buf.at[me], send_sems.at[k - 1], recv_sems.at[k - 1], device_id=peer, device_id_type=MESH)
            cp.start()
            copies.append(cp)
        for cp in copies:
            cp.wait()
        acc = buf[0]
        for d in range(1, N_DEV):
            acc = acc + buf[d]
        o_ref[...] = acc

    vm = pl.BlockSpec(memory_space=pltpu.VMEM)
    return pl.pallas_call(
        body, name="all_reduce_small", in_specs=[vm], out_specs=vm, out_shape=jax.ShapeDtypeStruct(v.shape, F32),
        scratch_shapes=[pltpu.VMEM((N_DEV,) + v.shape, F32), pltpu.SemaphoreType.DMA((N_DEV - 1,)), pltpu.SemaphoreType.DMA((N_DEV - 1,))],
    )(v)


def _local_groups(w, dtype):
    mix_out = [w["ev_w_out"][0], w["od_w_out"][0]]
    layers = []
    for l in range(DEPTH):
        a = jnp.concatenate([w["ffa_w_down"][l], w["ffb_w_down"][l]], axis=0).astype(dtype)
        b = jnp.concatenate([w["ple_w_gate"][l], mix_out[l]], axis=0).astype(dtype)
        c = jnp.concatenate([w["ffa_w_gate_up"][l], w["ffb_w_gate_up"][l]], axis=0).astype(dtype)
        layers.append((a, b, c))
    strip = jnp.concatenate([w["ple_w_proj"].reshape(-1, STRIP_C), w["ev_w_ukv"][0], jnp.pad(w["ev_w_uq"][0], ((0, 0), (0, STRIP_C - 96))),
                             jnp.zeros((G3_ROWS - 896, STRIP_C), F32)], axis=0)
    m = jnp.concatenate([w["od_w_in"][0], w["ev_w_in"][0], strip, jnp.zeros((G3_ROWS, G3_COLS - STRIP0 - STRIP_C), F32)], axis=1).astype(dtype)
    return layers, m


def _ungroup_local(layers, r3):
    a, b, c = zip(*layers)
    out = {
        "ffa_w_down": jnp.stack([x[:DOWN_ROWS] for x in a]), "ffb_w_down": jnp.stack([x[DOWN_ROWS:] for x in a]),
        "ple_w_gate": jnp.stack([x[:128] for x in b]), "ev_w_out": b[0][128:][None], "od_w_out": b[1][128:][None],
        "ffa_w_gate_up": jnp.stack([x[:D_MODEL] for x in c]), "ffb_w_gate_up": jnp.stack([x[D_MODEL:] for x in c]),
        "od_w_in": r3[:, :OD_C][None], "ev_w_in": r3[:, OD_C:STRIP0][None],
    }
    strip = r3[:, STRIP0:STRIP0 + STRIP_C]
    out["ple_w_proj"] = strip[:512].reshape(2, PLE_DIM, STRIP_C)
    out["ev_w_ukv"] = strip[512:640][None]
    out["ev_w_uq"] = strip[640:896, :96][None]
    return out


def _cols(a):
    return jnp.transpose(a, (1, 0, 2)).reshape(a.shape[1], -1)


def _blocks(g, c):
    return jnp.transpose(g.reshape(g.shape[0], N_DEV, c), (1, 0, 2))


def _uq_permute(w):
    r = w.shape[0]
    w3 = w.reshape(r, B_HEADS, B_NOPE + B_ROPE)
    half = B_ROPE // 2
    return jnp.concatenate([w3[:, :, :B_NOPE].reshape(r, -1), w3[:, :, B_NOPE:B_NOPE + half].reshape(r, -1), w3[:, :, B_NOPE + half:].reshape(r, -1)], axis=1)


def _uq_unpermute(g):
    r = g.shape[0]
    half = B_ROPE // 2
    n = B_HEADS * B_NOPE
    parts = [g[:, :n].reshape(r, B_HEADS, B_NOPE), g[:, n:n + B_HEADS * half].reshape(r, B_HEADS, half), g[:, n + B_HEADS * half:].reshape(r, B_HEADS, half)]
    return jnp.concatenate(parts, axis=2).reshape(r, -1)


def _ukv_permute(w):
    r = w.shape[0]
    return jnp.transpose(w.reshape(r, B_HEADS, 2, B_NOPE), (0, 2, 1, 3)).reshape(r, -1)


def _ukv_unpermute(g):
    r = g.shape[0]
    return jnp.transpose(g.reshape(r, 2, B_HEADS, B_NOPE), (0, 2, 1, 3)).reshape(r, -1)


def _od_in_widen(w):
    n = C_HEADS * C_HEAD_DIM
    wide = lambda m: jnp.pad(m.reshape(-1, C_HEADS, C_HEAD_DIM), ((0, 0), (0, 0), (0, QK_PAD - C_HEAD_DIM))).reshape(m.shape[0], -1)
    return jnp.concatenate([wide(w[:, :n] * C_HEAD_DIM ** -0.5), wide(w[:, n:2 * n]), w[:, 2 * n:],
                            jnp.zeros((w.shape[0], ODD_IN_PAD - ODD_IN_AUG), w.dtype)], axis=1)


def _od_in_narrow(g):
    wp = C_HEADS * QK_PAD
    narrow = lambda m: m.reshape(-1, C_HEADS, QK_PAD)[:, :, :C_HEAD_DIM].reshape(m.shape[0], -1)
    return jnp.concatenate([narrow(g[:, :wp]) * C_HEAD_DIM ** -0.5, narrow(g[:, wp:2 * wp]), g[:, 2 * wp:ODD_IN_AUG]], axis=1)


def _misc_weights(G3):
    strip = G3[:, :, STRIP0:STRIP0 + STRIP_C]
    return {
        "od_w_in": _od_in_widen(_cols(G3[:, :, :OD_C])),
        "ev_w_in": jnp.pad(_cols(G3[:, :, OD_C:STRIP0]), ((0, 0), (0, EVEN_IN_PAD - EVEN_IN))),
        "ple_w_proj": [_cols(strip[:, i * PLE_DIM:(i + 1) * PLE_DIM]) for i in range(DEPTH)],
        "ev_w_ukv": _ukv_permute(_cols(strip[:, 512:640])),
        "ev_w_uq": _uq_permute(_cols(strip[:, 640:896, :96])),
    }


def _misc_grads(G):
    strip = jnp.concatenate([
        _blocks(G["ple_w_proj"][0], STRIP_C), _blocks(G["ple_w_proj"][1], STRIP_C), _blocks(_ukv_unpermute(G["ev_w_ukv"]), STRIP_C),
        jnp.pad(_blocks(_uq_unpermute(G["ev_w_uq"]), 96), ((0, 0), (0, 0), (0, STRIP_C - 96))),
        jnp.zeros((N_DEV, G3_ROWS - 896, STRIP_C), F32)], axis=1)
    return jnp.concatenate([_blocks(_od_in_narrow(G["od_w_in"]), OD_C), _blocks(G["ev_w_in"][:, :EVEN_IN], EV_C), strip,
                            jnp.zeros((N_DEV, G3_ROWS, G3_COLS - STRIP0 - STRIP_C), F32)], axis=2)


def _ffn_fwd(h, norm_w, W, f, i, tag, ride=None):
    job = ride() if ride else None
    res = _ffn_gate_up(h, norm_w, W["C"][i].reshape(2, 4, C_ROWS, FF_BLK), f, f"{tag}_gate_up", job=job)
    n, gu, act = res[:3]
    if job is not None:
        ride(res[3:])
    job = ride() if ride else None
    out = _ffn_down(act, W["A"][i], f, h, f"{tag}_down", job=job)
    if job is not None:
        out, got = out
        ride(got)
    return out, (h, n, gu, act)


def _ffn_bwd(dout, saved, norm_w, W, GB, f, i, tag, ride=None):
    h, n, gu, act = saved
    S = h.shape[0]
    job = ride() if ride else None
    res = _ffn_down_dw(act, dout, f, GB["A"][i], f"{tag}_down_dw", job=job)
    if job is not None:
        res, got = res
        ride(got)
    GB["A"][i] = res
    dgu = _ffn_down_dx(dout, W["A"][i], f, gu, f"{tag}_down_dx").reshape(N_DEV, S, FF_BLK)
    job = ride() if ride else None
    res = _ffn_gate_up_dw(n, dgu, f, GB["C"][i], f"{tag}_gate_up_dw", job=job)
    if job is not None:
        res, got = res
        ride(got)
    GB["C"][i] = res
    return _ffn_gate_up_dx(dgu, W["C"][i], f, h, norm_w, dout, f"{tag}_gate_up_dx")


def _rope_tables(S):
    inv = ROPE_THETA ** (-jnp.arange(0, B_ROPE, 2, dtype=F32) / B_ROPE)
    ang = jnp.arange(S, dtype=F32)[:, None] * inv[None, :]
    return jnp.cos(ang), jnp.sin(ang)


def _alibi_columns(S):
    t = jnp.arange(S, dtype=jnp.int32)
    hi = ((t // 16) * 16).astype(F32)
    lo = (t % 16).astype(F32)
    slopes = 2.0 ** (-8.0 * jnp.arange(1, A_HEADS + 1, dtype=F32) / A_HEADS)
    zq = jnp.zeros((S, A_HEADS), F32)
    rest = QK_PAD - A_HEAD_DIM - 4
    qc = jnp.stack([-slopes[None, :] * hi[:, None], -slopes[None, :] * lo[:, None], zq + slopes[None, :], zq + slopes[None, :]] + [zq] * rest, axis=-1)
    one = jnp.ones((S, A_KV_HEADS), F32)
    zk = jnp.zeros((S, A_KV_HEADS), F32)
    kc = jnp.stack([one, one, zk + hi[:, None], zk + lo[:, None]] + [zk] * rest, axis=-1)
    return qc, kc


def _sink_prm(sinks):
    return jnp.zeros((A_HEADS, 1, LANES), F32).at[:, 0, 0].set(sinks.astype(F32))


def _with_ride(ride, call):
    job = ride() if ride else None
    res = call(job)
    if job is None:
        return res
    n_own = len(res) - len(job["outs"])
    ride(res[n_own:])
    return res[:n_own]


def _even_fwd(hn, h, W, ride=None):
    S = hn.shape[0]
    proj = _mm(hn, W["ev_w_in"], name="ev_in")
    a_q, a_k, a_v = proj[:, :512], proj[:, 512:640], proj[:, 640:768]
    c_q, c_kv = proj[:, 768:1024], proj[:, 1024:1152]
    kr1, kr2 = proj[:, 1152:1168], proj[:, 1168:1184]
    qc, kc = _alibi_columns(S)
    qa, qaT = _rows_and_cols(jnp.concatenate([(a_q * A_HEAD_DIM ** -0.5).reshape(S, A_HEADS, A_HEAD_DIM), qc], axis=-1))
    ka, kaT = _rows_and_cols(jnp.concatenate([a_k.reshape(S, A_KV_HEADS, A_HEAD_DIM), kc], axis=-1))
    va3 = a_v.reshape(S, A_KV_HEADS, A_HEAD_DIM)
    va = jnp.transpose(va3.astype(BF16), (1, 0, 2))
    prm = _sink_prm(W["ev_sinks"][0])
    oaT, lse_a = _with_ride(ride, lambda job: _attn_fwd(qaT, ka, _v_with_ones(va3), tile=SWA_TILE, hb=2, window=WINDOW, sink=prm,
                                                        name="swa_fwd", job=job))
    cqn = _rms_fwd(c_q, W["ev_cq_norm"], "ev_cq_norm")
    q_all = _mm(cqn, W["ev_w_uq"], name="ev_uq")
    ckvn = _rms_fwd(c_kv, W["ev_ckv_norm"], "ev_ckv_norm")
    kv_all = _mm(ckvn, W["ev_w_ukv"], name="ev_ukv")
    cos, sin = _rope_tables(S)
    cos8, sin8 = jnp.tile(cos, (1, B_HEADS)), jnp.tile(sin, (1, B_HEADS))
    q1, q2 = _rope(q_all[:, 512:640], q_all[:, 640:768], cos8, sin8, "ev_rope_q")
    k1, k2 = _rope(kr1, kr2, cos, sin, "ev_rope_k")
    half = B_ROPE // 2
    scale = (B_NOPE + B_ROPE) ** -0.5
    qb, qbT = _rows_and_cols(jnp.concatenate([q_all[:, :512].reshape(S, B_HEADS, B_NOPE), q1.reshape(S, B_HEADS, half), q2.reshape(S, B_HEADS, half)], axis=-1) * scale)
    kro = jnp.broadcast_to(jnp.concatenate([k1, k2], axis=1)[:, None, :], (S, B_HEADS, B_ROPE))
    kb, kbT = _rows_and_cols(jnp.concatenate([kv_all[:, :512].reshape(S, B_HEADS, B_NOPE), kro], axis=-1))
    vb3 = kv_all[:, 512:].reshape(S, B_HEADS, B_V)
    vb = jnp.transpose(vb3.astype(BF16), (1, 0, 2))
    obT, lse_b = _with_ride(ride, lambda job: _attn_fwd(qbT, kb, _v_with_ones(vb3), tile=min(ATTN_TILE_FWD, S), hb=2, name="mla_fwd", job=job))
    cat = jnp.concatenate([_from_T(oaT), _from_T(obT)], axis=1)
    out = _mm_w128(cat, W["B"][0], MIX_OUT_BLK, res=h, name="ev_out")
    return out, (hn, proj, (qa, qaT, ka, kaT, va, oaT, lse_a), prm, cqn, ckvn, (qb, qbT, kb, kbT, vb, obT, lse_b), cat)


def _even_bwd(dout, saved, W, GB, norm):
    hn, proj, (qa, qaT, ka, kaT, va, oaT, lse_a), prm, cqn, ckvn, (qb, qbT, kb, kbT, vb, obT, lse_b), cat = saved
    S = hn.shape[0]
    G = {}
    dcat = _mm_w128(dout, W["B"][0], MIX_OUT_BLK, tb=True, out=BF16, name="ev_out_dx")
    GB["B"][0] = _mm_w128_dw(cat, dout, MIX_OUT_BLK, GB["B"][0], "ev_out_dw")
    doa, doaT = _rows_and_cols(dcat[:, :512].reshape(S, A_HEADS, A_HEAD_DIM))
    dqaT, dka, dva, dsink = _attn_bwd(qa, qaT, ka, kaT, va, oaT, doa, doaT, lse_a, tile=SWA_TILE, hb=2, window=WINDOW, sink=prm, real=A_HEAD_DIM,
                                       name="swa_bwd")
    G["ev_sinks"] = dsink[:, 0, 0]
    dqa = _from_T(dqaT) * A_HEAD_DIM ** -0.5
    dka = dka.reshape(A_KV_HEADS, A_GROUP, S, A_HEAD_DIM).sum(axis=1)
    dva = dva.reshape(A_KV_HEADS, A_GROUP, S, A_HEAD_DIM).sum(axis=1)
    dob, dobT = _rows_and_cols(dcat[:, 512:].reshape(S, B_HEADS, B_V))
    dqbT, dkb, dvb = _attn_bwd(qb, qbT, kb, kbT, vb, obT, dob, dobT, lse_b, tile=min(ATTN_TILE, S), hb=1, name="mla_bwd")
    half = B_ROPE // 2
    dqb = jnp.transpose(dqbT, (2, 0, 1)) * (B_NOPE + B_ROPE) ** -0.5
    dkb = jnp.transpose(dkb, (1, 0, 2))
    cos, sin = _rope_tables(S)
    cos8, sin8 = jnp.tile(cos, (1, B_HEADS)), jnp.tile(sin, (1, B_HEADS))
    dq1, dq2 = _rope(dqb[:, :, B_NOPE:B_NOPE + half].reshape(S, -1), dqb[:, :, B_NOPE + half:].reshape(S, -1), cos8, -sin8, "ev_rope_q_bwd")
    dq_all = jnp.concatenate([dqb[:, :, :B_NOPE].reshape(S, -1), dq1, dq2], axis=1).astype(BF16)
    dkr = dkb[:, :, B_NOPE:].sum(axis=1)
    dk1, dk2 = _rope(dkr[:, :half], dkr[:, half:], cos, -sin, "ev_rope_k_bwd")
    dkv_all = jnp.concatenate([dkb[:, :, :B_NOPE].reshape(S, -1), _unheads(dvb)], axis=1).astype(BF16)
    G["ev_w_uq"] = _mm(cqn, dq_all, ta=True, name="ev_uq_dw")
    dcqn = _mm(dq_all, W["ev_w_uq"], tb=True, name="ev_uq_dx")
    dc_q, G["ev_cq_norm"] = _rms_bwd(dcqn, proj[:, 768:1024], W["ev_cq_norm"], None, "ev_cq_norm_bwd")
    G["ev_w_ukv"] = _mm(ckvn, dkv_all, ta=True, name="ev_ukv_dw")
    dckvn = _mm(dkv_all, W["ev_w_ukv"], tb=True, name="ev_ukv_dx")
    dc_kv, G["ev_ckv_norm"] = _rms_bwd(dckvn, proj[:, 1024:1152], W["ev_ckv_norm"], None, "ev_ckv_norm_bwd")
    dproj = jnp.concatenate([dqa, _unheads(dka), _unheads(dva), dc_q, dc_kv, dk1, dk2,
                             jnp.zeros((S, EVEN_IN_PAD - EVEN_IN), F32)], axis=1).astype(BF16)
    G["ev_w_in"] = _mm(hn, dproj, ta=True, name="ev_in_dw")
    dh, dnorm = _mm(dproj, W["ev_w_in"], tb=True, norm_bwd=(*norm, dout), name="ev_in_dx")
    return dh, dnorm, G


def _odd_fwd(hn, h, W):
    S = hn.shape[0]
    w = C_HEADS * C_HEAD_DIM
    wp = C_HEADS * QK_PAD
    proj = _mm(hn, W["od_w_in"], name="od_in")
    f_logit = proj[:, 2 * wp + w: 2 * wp + w + C_HEADS]
    logf = _logsig_fwd(f_logit, W["od_b_f"], "od_logsig")
    logc = _cumsum(logf, False, "od_cumsum")
    parts = list(_exact3(logc))
    ones = [jnp.ones((S, C_HEADS), F32)] * 3
    pad = [jnp.zeros((S, C_HEADS), F32)] * (QK_PAD - C_HEAD_DIM - 6)
    lead = ((0, 0), (0, 0), (C_HEAD_DIM, 0))
    q3 = proj[:, :wp].reshape(S, C_HEADS, QK_PAD) + jnp.pad(jnp.stack(parts + ones + pad, axis=-1), lead)
    k3 = proj[:, wp:2 * wp].reshape(S, C_HEADS, QK_PAD) + jnp.pad(jnp.stack(ones + [-p for p in parts] + pad, axis=-1), lead)
    q, qT = _rows_and_cols(q3)
    k, kT = _rows_and_cols(k3)
    v3 = proj[:, 2 * wp:2 * wp + w].reshape(S, C_HEADS, C_HEAD_DIM)
    v = jnp.transpose(v3.astype(BF16), (1, 0, 2))
    oT, lse = _attn_fwd(qT, k, _v_with_ones(v3), tile=min(ATTN_TILE_FWD, S), hb=2, name="fox_fwd")
    cat = _from_T(oT)
    out = _mm_w128(cat, W["B"][1], MIX_OUT_BLK, res=h, name="od_out")
    return out, (hn, q, qT, k, kT, v, f_logit, oT, lse, cat)


def _odd_bwd(dout, saved, W, GB, norm):
    hn, q, qT, k, kT, v, f_logit, oT, lse, cat = saved
    S = hn.shape[0]
    G = {}
    dcat = _mm_w128(dout, W["B"][1], MIX_OUT_BLK, tb=True, out=BF16, name="od_out_dx")
    GB["B"][1] = _mm_w128_dw(cat, dout, MIX_OUT_BLK, GB["B"][1], "od_out_dw")
    do, doT = _rows_and_cols(dcat.reshape(S, C_HEADS, C_HEAD_DIM))
    dqT, dk, dv, dqxT, dkx = _attn_bwd(q, qT, k, kT, v, oT, do, doT, lse, tile=min(ATTN_TILE, S), hb=1, real=C_HEAD_DIM, extra=True,
                                       full=True, name="fox_bwd")
    dlogc = jnp.transpose(dqxT[:, 0, :] - dkx[:, :, 3])
    dlogf = _cumsum(dlogc, True, "od_cumsum_bwd")
    df, db = _logsig_bwd(dlogf, f_logit, W["od_b_f"], "od_logsig_bwd")
    G["od_b_f"] = db
    dproj = jnp.concatenate([_from_T(dqT), _unheads(dk), _unheads(dv), df, jnp.zeros((S, ODD_IN_PAD - ODD_IN_AUG), F32)], axis=1).astype(BF16)
    G["od_w_in"] = _mm(hn, dproj, ta=True, name="od_in_dw")
    dh, dnorm = _mm(dproj, W["od_w_in"], tb=True, norm_bwd=(*norm, dout), name="od_in_dx")
    return dh, dnorm, G


class _Rider:
    def __init__(self, steps, tag):
        self.steps, self.tag, self.count, self.result = steps, tag, 0, None
        self.job = next(steps)

    def __call__(self, got=None):
        if got is None:
            return self.job
        try:
            self.job = self.steps.send(list(got))
        except StopIteration as done:
            self.job, self.result = None, done.value
        return None

    def finish(self):
        while self.job is not None:
            self.count += 1
            self(_comm_call(self.job, f"{self.tag}_{self.count}"))
        return self.result


def _gather_plan(W, slots):
    a0, b0, c0, m, a1, b1, c1 = (slots[key] for key in ("a0", "b0", "c0", "m", "a1", "b1", "c1"))
    (m,) = yield _gather_job([m])
    W.update(_misc_weights(m))
    (b0,) = yield _gather_job([b0])
    W["B"] = [b0]
    (c0,) = yield _gather_job([c0], rows=[(D_MODEL, D_MODEL)])
    W["C"] = [c0]
    a0, c1 = yield _gather_job([a0, c1], rows=[(DOWN_ROWS, DOWN_ROWS), (0, D_MODEL)])
    W["A"] = [a0]
    a1, b1 = yield _gather_job([a1, b1])
    (c1,) = yield _gather_job([c1], rows=[(D_MODEL, D_MODEL)])
    W["A"].append(a1)
    W["B"].append(b1)
    W["C"].append(c1)


def _local_step(x, p, target, W, slots):
    h = x
    saved = []
    gather = _Rider(_gather_plan(W, slots), "all_gather_rest")
    for i in range(DEPTH):
        t = f"l{i}"
        ride = gather if i == 0 else None
        h1, s_a = _ffn_fwd(h, W["ffa_norm"][i:i + 1], W, 0, i, f"{t}_ffa", ride)
        nm = _rms_fwd(h1, W["mix_norm"][i:i + 1], f"{t}_mix_norm")
        h2, s_m = _even_fwd(nm, h1, W, ride) if i % 2 == 0 else _odd_fwd(nm, h1, W)
        h3, s_b = _ffn_fwd(h2, W["ffb_norm"][i:i + 1], W, 1, i, f"{t}_ffb", ride)
        npl = _rms_fwd(h3, W["ple_norm"][i:i + 1], f"{t}_ple_norm")
        gpre = _mm_w128(npl, W["B"][i], PLE_GATE_BLK, name=f"{t}_ple_gate")
        pp = _mm(p[i], W["ple_w_proj"][i], name=f"{t}_ple_proj")
        h4 = _ple_fwd(h3, gpre, pp, f"{t}_ple")
        saved.append((s_a, h1, s_m, s_b, h3, npl, gpre, pp))
        h = h4
        if i == 0:
            gather.finish()
    dh, g_final, loss_cols = _final_fwd_bwd(h, W["final_norm"], target, "final")
    G = {"final_norm": g_final}
    GB = {"A": [lax.empty((N_DEV, A_ROWS, D_MODEL), BF16) for _ in range(DEPTH)],
          "B": [lax.empty((N_DEV, B_ROWS, D_MODEL), BF16) for _ in range(DEPTH)],
          "C": [lax.empty((N_DEV, C_ROWS, FF_BLK), BF16) for _ in range(DEPTH)]}
    per_layer = {n: [None] * DEPTH for n in ("ffa_norm", "mix_norm", "ffb_norm", "ple_norm", "ple_w_proj")}
    scatter = None
    for i in reversed(range(DEPTH)):
        t = f"l{i}"
        s_a, h1, s_m, s_b, h3, npl, gpre, pp = saved[i]
        dgpre, dpp = _ple_bwd(dh, gpre, pp, f"{t}_ple_bwd")
        per_layer["ple_w_proj"][i] = _mm(p[i], dpp, ta=True, name=f"{t}_ple_proj_dw")
        GB["B"][i] = _mm_w128_dw(npl, dgpre, PLE_GATE_BLK, GB["B"][i], f"{t}_ple_gate_dw")
        dh, per_layer["ple_norm"][i] = _mm_w128(dgpre, W["B"][i], PLE_GATE_BLK, tb=True, norm_bwd=(h3, W["ple_norm"][i:i + 1], dh),
                                                name=f"{t}_ple_gate_dx")
        dh, per_layer["ffb_norm"][i] = _ffn_bwd(dh, s_b, W["ffb_norm"][i:i + 1], W, GB, 1, i, f"{t}_ffb", scatter)
        dh, per_layer["mix_norm"][i], g_mix = (_even_bwd if i % 2 == 0 else _odd_bwd)(dh, s_m, W, GB, (h1, W["mix_norm"][i:i + 1]))
        G.update(g_mix)
        dh, per_layer["ffa_norm"][i] = _ffn_bwd(dh, s_a, W["ffa_norm"][i:i + 1], W, GB, 0, i, f"{t}_ffa", scatter)
        if i == DEPTH - 1:
            scatter = _Rider(_reduce_scatter_steps([GB[key][i] for key in "ABC"], "later"), "rs_later")
    for n in ("ffa_norm", "mix_norm", "ffb_norm", "ple_norm"):
        G[n] = jnp.concatenate(per_layer[n], axis=0)
    G["ple_w_proj"] = per_layer["ple_w_proj"]
    return loss_cols, dh, tuple(scatter.finish()), {key: GB[key][0] for key in "ABC"}, G


def kernel(x, p, ffa_norm, ffa_w_gate_up, ffa_w_down, mix_norm, ffb_norm, ffb_w_gate_up, ffb_w_down, ple_norm, ple_w_gate, ple_w_proj, ev_w_in, ev_sinks, ev_cq_norm, ev_w_uq, ev_ckv_norm, ev_w_ukv, ev_w_out, od_w_in, od_b_f, od_w_out, final_norm, loss_target, m_ffa_norm, m_ffa_w_gate_up, m_ffa_w_down, m_mix_norm, m_ffb_norm, m_ffb_w_gate_up, m_ffb_w_down, m_ple_norm, m_ple_w_gate, m_ple_w_proj, m_ev_w_in, m_ev_sinks, m_ev_cq_norm, m_ev_w_uq, m_ev_ckv_norm, m_ev_w_ukv, m_ev_w_out, m_od_w_in, m_od_b_f, m_od_w_out, m_final_norm, v_ffa_norm, v_ffa_w_gate_up, v_ffa_w_down, v_mix_norm, v_ffb_norm, v_ffb_w_gate_up, v_ffb_w_down, v_ple_norm, v_ple_w_gate, v_ple_w_proj, v_ev_w_in, v_ev_sinks, v_ev_cq_norm, v_ev_w_uq, v_ev_ckv_norm, v_ev_w_ukv, v_ev_w_out, v_od_w_in, v_od_b_f, v_od_w_out, v_final_norm):
    given = dict(locals())
    w_in = {n: given[n] for n in WEIGHTS}

    layers, misc = _local_groups(w_in, BF16)
    (a0, b0, c0), (a1, b1, c1) = [[_in_slot(g) for g in layer] for layer in layers]
    a0, c0 = _comm_call(_gather_job([a0, c0], rows=[(0, DOWN_ROWS), (0, D_MODEL)]), "all_gather_first")
    W = {n: w_in[n] for n in SMALL}
    W["final_norm"] = final_norm.reshape(1, -1)
    W.update(A=[a0], C=[c0])
    slots = dict(a0=a0, b0=b0, c0=c0, m=_in_slot(misc), a1=a1, b1=b1, c1=c1)

    loss_cols, dx, r_later, GB, G = _local_step(x[0], p[:, 0], loss_target[0], W, slots)

    *r_first, r_misc = _reduce_scatter([GB["A"], GB["B"], GB["C"], _misc_grads(G).astype(BF16)], "first")
    grads = _ungroup_local([tuple(r_first), r_later], r_misc)
    layout = [(n, int(np.prod(w_in[n].shape))) for n in SMALL]
    vec = jnp.concatenate([G[n].astype(F32).reshape(-1) for n, _ in layout] + [jnp.sum(loss_cols).reshape(1)])
    vec = jnp.pad(vec, (0, N_DEV * SMALL_COLS - vec.shape[0])).reshape(N_DEV, SMALL_COLS)
    vec = _all_reduce_small(vec).reshape(-1)
    off = 0
    for n, size in layout:
        grads[n] = vec[off: off + size].reshape(w_in[n].shape)
        off += size
    loss = vec[off]

    delta, new_m, new_v = {}, {}, {}
    for n in WEIGHTS:
        shp = w_in[n].shape
        as2d = (lambda a: a.reshape(1, -1)) if len(shp) == 1 else (lambda a: a)
        d, nm, nv = _adamw(as2d(w_in[n]), as2d(grads[n]), as2d(given["m_" + n]), as2d(given["v_" + n]), f"adamw_{n}")
        delta[n], new_m[n], new_v[n] = d.reshape(shp), nm.reshape(shp), nv.reshape(shp)
    return (loss, dx[None], *[grads[n] for n in WEIGHTS], *[delta[n] for n in WEIGHTS],
            *[new_m[n] for n in WEIGHTS], *[new_v[n] for n in WEIGHTS])
```

```python
import functools

import numpy as np
import jax
import jax.numpy as jnp
from jax import lax
from jax.experimental import pallas as pl
from jax.experimental.pallas import tpu as pltpu

F32 = jnp.float32
BF16 = jnp.bfloat16
MESH = pl.DeviceIdType.MESH

D_MODEL = 1024
D_FF = 2816
RMS_EPS = 1e-6
PLE_DIM = 256
A_HEADS, A_KV_HEADS, A_HEAD_DIM, WINDOW = 8, 2, 64, 128
A_GROUP = A_HEADS // A_KV_HEADS
B_HEADS, B_Q_LORA, B_KV_LORA, B_NOPE, B_ROPE, B_V = 8, 256, 128, 64, 32, 64
ROPE_THETA = 10000.0
C_HEADS, C_HEAD_DIM = 16, 64
EVEN_IN = 1184
EVEN_IN_PAD = 1280
ODD_IN = 3088
ODD_IN_AUG = 2 * 16 * 80 + 1024 + 16
ODD_IN_PAD = 3840
DEPTH = 2
ADAM_LR, ADAM_B1, ADAM_B2, ADAM_EPS, ADAM_WD, ADAM_STEP = 0.001, 0.9, 0.999, 1e-08, 0.01, 10

N_DEV = 8
LANES = 128
SUBLANES = 8
EW_TILE_BYTES = 3 << 20
MM_VMEM_BYTES = 26 << 20
NEG = -1e30
ATTN_TILE = 1024
ATTN_TILE_FWD = 1024
SWA_TILE = 256
QK_PAD = 80

FF_BLK = D_FF // 4
DOWN_ROWS = D_FF // N_DEV
A_ROWS, B_ROWS, C_ROWS, G3_ROWS, G3_COLS = 2 * DOWN_ROWS, 256, 2 * D_MODEL, 1024, 768
PLE_GATE_BLK, MIX_OUT_BLK = 0, 1
OD_C, EV_C, STRIP_C = 386, 148, 128
STRIP0 = OD_C + EV_C

SMALL = ["ffa_norm", "mix_norm", "ffb_norm", "ple_norm", "ev_sinks", "ev_cq_norm", "ev_ckv_norm", "od_b_f", "final_norm"]
WEIGHTS = ["ffa_norm", "ffa_w_gate_up", "ffa_w_down", "mix_norm", "ffb_norm", "ffb_w_gate_up", "ffb_w_down", "ple_norm",
           "ple_w_gate", "ple_w_proj", "ev_w_in", "ev_sinks", "ev_cq_norm", "ev_w_uq", "ev_ckv_norm", "ev_w_ukv", "ev_w_out",
           "od_w_in", "od_b_f", "od_w_out", "final_norm"]
SMALL_COLS = 1280


def _divisor(n, cap, mult):
    if n <= cap:
        return n
    for t in range(cap - cap % mult, 0, -mult):
        if n % t == 0:
            return t
    raise ValueError(f"no tile for {n} under {cap} in steps of {mult}")


def _lanes(c):
    return -(-c // LANES) * LANES


def _ew(fn, rows, vecs, outs, reds=(), *, name):
    R = rows[0].shape[0]
    per_row = sum(_lanes(a.shape[1]) * a.dtype.itemsize for a in rows) + sum(_lanes(c) * jnp.dtype(d).itemsize for c, d in outs)
    tm = _divisor(R, max(16, EW_TILE_BYTES // per_row // 16 * 16), 16) if R % 16 == 0 else R
    n_r, n_v, n_o = len(rows), len(vecs), len(outs)

    def body(*refs):
        ins = [r[...] for r in refs[: n_r + n_v]]
        res = fn(*ins)
        if not isinstance(res, (tuple, list)):
            res = (res,)
        o_refs = refs[n_r + n_v: n_r + n_v + n_o]
        r_refs = refs[n_r + n_v + n_o:]
        for ref, val in zip(o_refs, res[:n_o]):
            ref[...] = val.astype(ref.dtype)
        if r_refs:
            @pl.when(pl.program_id(0) == 0)
            def _():
                for ref in r_refs:
                    ref[...] = jnp.zeros_like(ref)
            for ref, val in zip(r_refs, res[n_o:]):
                ref[...] += val

    in_specs = [pl.BlockSpec((tm, a.shape[1]), lambda i: (i, 0)) for a in rows]
    in_specs += [pl.BlockSpec((1, a.shape[1]), lambda i: (0, 0)) for a in vecs]
    out_specs = [pl.BlockSpec((tm, c), lambda i: (i, 0)) for c, _ in outs]
    out_specs += [pl.BlockSpec((1, c), lambda i: (0, 0)) for c in reds]
    out_shape = [jax.ShapeDtypeStruct((R, c), d) for c, d in outs] + [jax.ShapeDtypeStruct((1, c), F32) for c in reds]
    res = pl.pallas_call(body, name=name, grid=(R // tm,), in_specs=in_specs, out_specs=out_specs, out_shape=out_shape)(*rows, *vecs)
    return res[0] if len(res) == 1 else res


def _rms_fwd(x, w, name):
    def fn(x, w):
        y = x * lax.rsqrt(jnp.mean(x * x, axis=-1, keepdims=True) + RMS_EPS)
        return y * w
    return _ew(fn, [x], [w], [(x.shape[1], BF16)], name=name)


def _rms_bwd(dn, x, w, dres, name):
    def fn(dn, x, *rest):
        w = rest[-1]
        r = lax.rsqrt(jnp.mean(x * x, axis=-1, keepdims=True) + RMS_EPS)
        xh = x * r
        gw = dn * w
        dx = r * (gw - xh * jnp.mean(gw * xh, axis=-1, keepdims=True))
        if len(rest) == 2:
            dx = dx + rest[0]
        return dx, jnp.sum(dn * xh, axis=0, keepdims=True)
    rows = [dn, x] + ([dres] if dres is not None else [])
    return _ew(fn, rows, [w], [(x.shape[1], F32)], [x.shape[1]], name=name)


def _ple_fwd(h, gpre, pp, name):
    return _ew(lambda h, g, q: h + jax.nn.sigmoid(g) * q, [h, gpre, pp], [], [(h.shape[1], F32)], name=name)


def _ple_bwd(dh, gpre, pp, name):
    def fn(dh, g, q):
        sg = jax.nn.sigmoid(g)
        return dh * q * (sg * (1.0 - sg)), dh * sg
    return _ew(fn, [dh, gpre, pp], [], [(dh.shape[1], BF16), (dh.shape[1], BF16)], name=name)


def _rope(x1, x2, cos, sin, name):
    c = x1.shape[1]
    return _ew(lambda a, b, co, si: (a * co - b * si, a * si + b * co), [x1, x2, cos, sin], [], [(c, F32), (c, F32)], name=name)


def _logsig_fwd(f, b, name):
    def fn(f, b):
        z = f + b
        return jnp.minimum(z, 0.0) - jnp.log(1.0 + jnp.exp(-jnp.abs(z)))
    return _ew(fn, [f], [b], [(f.shape[1], F32)], name=name)


def _logsig_bwd(dlogf, f, b, name):
    def fn(d, f, b):
        df = d * jax.nn.sigmoid(-(f + b))
        return df, jnp.sum(df, axis=0, keepdims=True)
    return _ew(fn, [dlogf, f], [b], [(f.shape[1], F32)], [f.shape[1]], name=name)


def _final_fwd_bwd(h, w, target, name):
    d = h.shape[1]

    def fn(h, t, w):
        r = lax.rsqrt(jnp.mean(h * h, axis=-1, keepdims=True) + RMS_EPS)
        xh = h * r
        y = xh * w
        err = y - t
        dy = err * (1.0 / d)
        gw = dy * w
        dx = r * (gw - xh * jnp.mean(gw * xh, axis=-1, keepdims=True))
        return dx, jnp.sum(dy * xh, axis=0, keepdims=True), jnp.sum(err * err, axis=0, keepdims=True) * (0.5 / d)
    return _ew(fn, [h, target], [w], [(d, F32)], [d, d], name=name)


def _adamw(w, g, m, v, name):
    shape = w.shape
    c = shape[-1]
    w2, g2, m2, v2 = (a.reshape(-1, c) for a in (w, g, m, v))

    def fn(w, g, m, v):
        m = ADAM_B1 * m + (1.0 - ADAM_B1) * g
        v = ADAM_B2 * v + (1.0 - ADAM_B2) * jnp.square(g)
        m_hat = m / (1.0 - ADAM_B1 ** ADAM_STEP)
        v_hat = v / (1.0 - ADAM_B2 ** ADAM_STEP)
        delta = -ADAM_LR * (m_hat / (jnp.sqrt(v_hat) + ADAM_EPS) + ADAM_WD * w)
        return delta, m, v
    d, nm, nv = _ew(fn, [w2, g2, m2, v2], [], [(c, F32)] * 3, name=name)
    return d.reshape(shape), nm.reshape(shape), nv.reshape(shape)


def _split3(v):
    hi = v.astype(BF16)
    r1 = v - hi.astype(F32)
    mid = r1.astype(BF16)
    lo = (r1 - mid.astype(F32)).astype(BF16)
    return hi, mid, lo


def _cumsum(x, reverse, name):
    S, C = x.shape
    tm = _divisor(S, 512, 16)
    nt = S // tm

    def body(x_ref, o_ref, carry):
        @pl.when(pl.program_id(0) == 0)
        def _():
            carry[...] = jnp.zeros_like(carry)
        r = lax.broadcasted_iota(jnp.int32, (tm, tm), 0)
        c = lax.broadcasted_iota(jnp.int32, (tm, tm), 1)
        tri = jnp.where((c >= r) if reverse else (c <= r), 1.0, 0.0).astype(BF16)
        xv = x_ref[...]
        acc = jnp.zeros((tm, C), F32)
        for part in _split3(xv):
            acc = acc + jnp.dot(tri, part, preferred_element_type=F32)
        o_ref[...] = acc + carry[...]
        carry[...] += jnp.sum(xv, axis=0, keepdims=True)

    idx = (lambda i: (nt - 1 - i, 0)) if reverse else (lambda i: (i, 0))
    return pl.pallas_call(
        body, name=name, grid=(nt,), in_specs=[pl.BlockSpec((tm, C), idx)], out_specs=pl.BlockSpec((tm, C), idx),
        out_shape=jax.ShapeDtypeStruct((S, C), F32), scratch_shapes=[pltpu.VMEM((1, C), F32)],
    )(x)


NN = (((1,), (0,)), ((), ()))
NT = (((1,), (1,)), ((), ()))
TN = (((0,), (0,)), ((), ()))

HBM_SPEC = pl.BlockSpec(memory_space=pl.ANY)


def _job_in_body(job, refs, n_in, n_out, n_scr, grid):
    if job is None:
        return refs[n_in:], lambda: None
    ji, jo = len(job["ins"]), len(job["outs"])
    j_in = refs[n_in: n_in + ji]
    pos = n_in + ji
    own = list(refs[pos: pos + n_out])
    pos += n_out
    j_out = refs[pos: pos + jo]
    pos += jo
    own += list(refs[pos: pos + n_scr])
    ss, rs = refs[-2], refs[-1]
    first = functools.reduce(jnp.logical_and, [pl.program_id(d) == 0 for d in range(len(grid))])
    last = functools.reduce(jnp.logical_and, [pl.program_id(d) == n - 1 for d, n in enumerate(grid)])

    @pl.when(first)
    def _():
        job["start"](j_in, j_out, ss, rs)

    def finish():
        @pl.when(last)
        def _():
            job["finish"](j_in, j_out, ss, rs)

    return own, finish


def _job_call(job, body, *, name, grid, in_specs, out_specs, out_shape, args, scratch_shapes, aliases, dimension_semantics):
    in_specs, out_specs, out_shape, args, scratch_shapes = list(in_specs), list(out_specs), list(out_shape), list(args), list(scratch_shapes)
    aliases = dict(aliases)
    if job is not None:
        for i, o in job["aliases"].items():
            aliases[len(args) + i] = len(out_shape) + o
        in_specs += [HBM_SPEC] * len(job["ins"])
        args += list(job["ins"])
        out_specs += [HBM_SPEC] * len(job["outs"])
        out_shape += list(job["outs"])
        scratch_shapes += [pltpu.SemaphoreType.DMA((job["n_sems"],)), pltpu.SemaphoreType.DMA((job["n_sems"],))]
    return pl.pallas_call(
        body, name=name, grid=grid, in_specs=in_specs, out_specs=out_specs, out_shape=out_shape,
        scratch_shapes=scratch_shapes, input_output_aliases=aliases,
        compiler_params=pltpu.CompilerParams(dimension_semantics=dimension_semantics),
    )(*args)


def _comm_call(job, name):
    def body(*refs):
        ji, jo = len(job["ins"]), len(job["outs"])
        job["start"](refs[:ji], refs[ji: ji + jo], refs[-2], refs[-1])
        job["finish"](refs[:ji], refs[ji: ji + jo], refs[-2], refs[-1])

    return pl.pallas_call(
        body, name=name, in_specs=[HBM_SPEC] * len(job["ins"]), out_specs=[HBM_SPEC] * len(job["outs"]), out_shape=list(job["outs"]),
        input_output_aliases=dict(job["aliases"]),
        scratch_shapes=[pltpu.SemaphoreType.DMA((job["n_sems"],)), pltpu.SemaphoreType.DMA((job["n_sems"],))],
    )(*job["ins"])


def _mm_call(name, grid, k_axis, a, a_spec, a2d, b, b_spec, b2d, dims, out_sds, out_spec, o2d, *,
             alpha=1.0, res=None, res_spec=None, into=None, job=None, norm_bwd=None):
    nk = grid[k_axis]
    n_in = 2 + (res is not None) + (into is not None) + (3 if norm_bwd is not None else 0)
    n_out = 2 if norm_bwd is not None else 1

    def body(*refs):
        a_ref, b_ref = refs[0], refs[1]
        res_ref = refs[2] if res is not None else None
        own, finish_job = _job_in_body(job, refs, n_in, n_out, 1, grid)
        o_ref, acc_ref = own[0], own[-1]
        k = pl.program_id(k_axis)

        @pl.when(k == 0)
        def _():
            acc_ref[...] = jnp.zeros_like(acc_ref)

        if norm_bwd is not None:
            x_ref, w_ref, dres_ref = refs[n_in - 3: n_in]
            dw_ref = own[1]

            @pl.when(functools.reduce(jnp.logical_and, [pl.program_id(d) == 0 for d in range(len(grid))]))
            def _():
                dw_ref[...] = jnp.zeros_like(dw_ref)

        av = a_ref[...].reshape(a2d).astype(BF16)
        bv = b_ref[...].reshape(b2d).astype(BF16)
        acc_ref[...] += lax.dot_general(av, bv, dims, preferred_element_type=F32)

        @pl.when(k == nk - 1)
        def _():
            r = acc_ref[...]
            if alpha != 1.0:
                r = r * alpha
            if res_ref is not None:
                r = res_ref[...].reshape(o2d) + r
            if norm_bwd is not None:
                x = x_ref[...]
                rs = lax.rsqrt(jnp.mean(x * x, axis=-1, keepdims=True) + RMS_EPS)
                xh = x * rs
                gw = r * w_ref[...]
                dw_ref[...] += jnp.sum(r * xh, axis=0, keepdims=True)
                r = dres_ref[...] + rs * (gw - xh * jnp.mean(gw * xh, axis=-1, keepdims=True))
            o_ref[...] = r.reshape(o_ref.shape).astype(o_ref.dtype)

        finish_job()

    in_specs, args = [a_spec, b_spec], [a, b]
    if res is not None:
        in_specs.append(res_spec)
        args.append(res)
    aliases = {}
    if into is not None:
        aliases = {len(args): 0}
        in_specs.append(pl.BlockSpec(memory_space=pl.ANY))
        args.append(into)
        out_sds = jax.ShapeDtypeStruct(into.shape, into.dtype)
    out_specs, out_shape = [out_spec], [out_sds]
    if norm_bwd is not None:
        vec = pl.BlockSpec((1, o2d[1]), lambda *_: (0, 0))
        in_specs += [out_spec, vec, out_spec]
        args += list(norm_bwd)
        out_specs.append(vec)
        out_shape.append(jax.ShapeDtypeStruct((1, o2d[1]), F32))
    serial = job is not None or norm_bwd is not None
    sem = tuple("arbitrary" if d == k_axis or serial else "parallel" for d in range(len(grid)))
    res_all = _job_call(
        job, body, name=name, grid=grid, in_specs=in_specs, out_specs=out_specs, out_shape=out_shape, args=args,
        scratch_shapes=[pltpu.VMEM(o2d, F32)], aliases=aliases, dimension_semantics=sem)
    own = res_all[0] if n_out == 1 else tuple(res_all[:n_out])
    return own if job is None else (own, res_all[n_out:])


def _mm(a, b, *, ta=False, tb=False, out=F32, res=None, alpha=1.0, norm_bwd=None, name):
    K, M = a.shape if ta else a.shape[::-1]
    N = b.shape[0] if tb else b.shape[1]
    assert (b.shape[1] if tb else b.shape[0]) == K, (a.shape, b.shape, ta, tb)
    tk = _divisor(K, 1024, LANES)
    tn = _divisor(N, 1408, LANES)
    assert norm_bwd is None or tn == N
    for cap in (1024, 512, 256, 128):
        tm = _divisor(M, cap, LANES if ta else 16)
        est = 2 * (tm * tk * a.dtype.itemsize + tk * tn * b.dtype.itemsize + tm * tn * jnp.dtype(out).itemsize)
        est += tm * tn * 4 + (2 * tm * tn * 4 if res is not None else 0) + (4 * tm * tn * 4 if norm_bwd is not None else 0)
        if est <= MM_VMEM_BYTES:
            break
    a_spec = pl.BlockSpec((tk, tm), lambda i, j, k: (k, i)) if ta else pl.BlockSpec((tm, tk), lambda i, j, k: (i, k))
    b_spec = pl.BlockSpec((tn, tk), lambda i, j, k: (j, k)) if tb else pl.BlockSpec((tk, tn), lambda i, j, k: (k, j))
    o_spec = pl.BlockSpec((tm, tn), lambda i, j, k: (i, j))
    dims = (((0 if ta else 1,), (1 if tb else 0,)), ((), ()))
    return _mm_call(name, (M // tm, N // tn, K // tk), 2, a, a_spec, (tk, tm) if ta else (tm, tk), b, b_spec,
                    (tn, tk) if tb else (tk, tn), dims, jax.ShapeDtypeStruct((M, N), out), o_spec, (tm, tn),
                    alpha=alpha, res=res, res_spec=o_spec, norm_bwd=norm_bwd)


def _w128_spec(blk):
    return pl.BlockSpec((N_DEV, 128, D_MODEL), lambda *_: (0, blk, 0))


def _mm_w128(a, G1, blk, *, tb=False, res=None, out=F32, norm_bwd=None, name):
    S = a.shape[0]
    tm = _divisor(S, 512, 16)
    row = pl.BlockSpec((tm, D_MODEL), lambda i, k: (i, 0))
    return _mm_call(name, (S // tm, 1), 1, a, row, (tm, D_MODEL), G1, _w128_spec(blk), (D_MODEL, D_MODEL), NT if tb else NN,
                    jax.ShapeDtypeStruct((S, D_MODEL), out), row, (tm, D_MODEL), res=res, res_spec=row, norm_bwd=norm_bwd)


def _mm_w128_dw(a, b, blk, into, name):
    S = a.shape[0]
    tk = _divisor(S, 1024, 16)
    row = pl.BlockSpec((tk, D_MODEL), lambda i, k: (k, 0))
    return _mm_call(name, (1, S // tk), 1, a, row, (tk, D_MODEL), b, row, (tk, D_MODEL), TN, None, _w128_spec(blk),
                    (D_MODEL, D_MODEL), into=into)


def _ffn_gate_up(h, norm_w, G2v, rb, name, job=None):
    S = h.shape[0]
    tm = _divisor(S, 1024, 16)
    grid = (S // tm, 4)

    def body(*refs):
        h_ref, nw_ref, w_ref = refs[:3]
        (n_ref, gu_ref, act_ref, n_scr), finish_job = _job_in_body(job, refs, 3, 3, 1, grid)

        @pl.when(pl.program_id(1) == 0)
        def _():
            x = h_ref[...]
            y = x * lax.rsqrt(jnp.mean(x * x, axis=-1, keepdims=True) + RMS_EPS)
            n_scr[...] = (y * nw_ref[...]).astype(BF16)
            n_ref[...] = n_scr[...]

        nv = n_scr[...]
        g = jnp.dot(nv, w_ref[0, 0], preferred_element_type=F32)
        u = jnp.dot(nv, w_ref[1, 0], preferred_element_type=F32)
        gu_ref[0, 0] = g.astype(BF16)
        gu_ref[1, 0] = u.astype(BF16)
        act_ref[0] = (g * jax.nn.sigmoid(g) * u).astype(BF16)
        finish_job()

    row = pl.BlockSpec((tm, D_MODEL), lambda i, j: (i, 0))
    return _job_call(
        job, body, name=name, grid=grid,
        in_specs=[row, pl.BlockSpec((1, D_MODEL), lambda i, j: (0, 0)), pl.BlockSpec((2, 1, D_MODEL, FF_BLK), lambda i, j: (0, j, rb, 0))],
        out_specs=[row, pl.BlockSpec((2, 1, tm, FF_BLK), lambda i, j: (0, j, i, 0)), pl.BlockSpec((1, tm, FF_BLK), lambda i, j: (j, i, 0))],
        out_shape=[jax.ShapeDtypeStruct((S, D_MODEL), BF16), jax.ShapeDtypeStruct((2, 4, S, FF_BLK), BF16), jax.ShapeDtypeStruct((4, S, FF_BLK), BF16)],
        args=[h, norm_w, G2v], scratch_shapes=[pltpu.VMEM((tm, D_MODEL), BF16)], aliases={},
        dimension_semantics=("arbitrary" if job is not None else "parallel", "arbitrary"))


def _ffn_down(act, G1, ob, h, name, job=None):
    S = h.shape[0]
    tm = _divisor(S, 512, 16)
    row = pl.BlockSpec((tm, D_MODEL), lambda i, k: (i, 0))
    return _mm_call(name, (S // tm, 4), 1, act, pl.BlockSpec((1, tm, FF_BLK), lambda i, k: (k, i, 0)), (tm, FF_BLK),
                    G1, pl.BlockSpec((2, DOWN_ROWS, D_MODEL), lambda i, k: (k, ob, 0)), (FF_BLK, D_MODEL), NN,
                    jax.ShapeDtypeStruct((S, D_MODEL), F32), row, (tm, D_MODEL), alpha=0.5, res=h, res_spec=row, job=job)


def _ffn_down_dx(dh, G1, ob, gu, name):
    S = dh.shape[0]
    tm = _divisor(S, 512, 16)

    def body(dh_ref, w_ref, gu_ref, o_ref):
        w = w_ref[...].reshape(FF_BLK, D_MODEL)
        dact = lax.dot_general(dh_ref[...].astype(BF16), w, NT, preferred_element_type=F32) * 0.5
        g = gu_ref[0, 0].astype(F32)
        u = gu_ref[1, 0].astype(F32)
        sg = jax.nn.sigmoid(g)
        o_ref[0, 0] = (dact * u * (sg * (1.0 + g * (1.0 - sg)))).astype(BF16)
        o_ref[1, 0] = (dact * (g * sg)).astype(BF16)

    blk = pl.BlockSpec((2, 1, tm, FF_BLK), lambda j, i: (0, j, i, 0))
    return pl.pallas_call(
        body, name=name, grid=(4, S // tm),
        in_specs=[pl.BlockSpec((tm, D_MODEL), lambda j, i: (i, 0)), pl.BlockSpec((2, DOWN_ROWS, D_MODEL), lambda j, i: (j, ob, 0)), blk],
        out_specs=blk, out_shape=jax.ShapeDtypeStruct((2, 4, S, FF_BLK), BF16),
    )(dh, G1, gu)


def _ffn_down_dw(act, dh, name, job=None):
    S = dh.shape[0]
    tk = _divisor(S, 1024, 16)
    return _mm_call(name, (4, S // tk), 1, act, pl.BlockSpec((1, tk, FF_BLK), lambda j, k: (j, k, 0)), (tk, FF_BLK),
                    dh, pl.BlockSpec((tk, D_MODEL), lambda j, k: (k, 0)), (tk, D_MODEL), TN,
                    jax.ShapeDtypeStruct((N_DEV, DOWN_ROWS, D_MODEL), BF16),
                    pl.BlockSpec((2, DOWN_ROWS, D_MODEL), lambda j, k: (j, 0, 0)), (FF_BLK, D_MODEL), alpha=0.5, job=job)


def _ffn_gate_up_dw(n, dgu8, name, job=None):
    S = n.shape[0]
    tk = _divisor(S, 1024, 16)
    return _mm_call(name, (N_DEV, S // tk), 1, n, pl.BlockSpec((tk, D_MODEL), lambda b, k: (k, 0)), (tk, D_MODEL),
                    dgu8, pl.BlockSpec((1, tk, FF_BLK), lambda b, k: (b, k, 0)), (tk, FF_BLK), TN,
                    jax.ShapeDtypeStruct((N_DEV, D_MODEL, FF_BLK), BF16),
                    pl.BlockSpec((1, D_MODEL, FF_BLK), lambda b, k: (b, 0, 0)), (D_MODEL, FF_BLK), job=job)


def _ffn_gate_up_dx(dgu8, G2, rb, h, norm_w, dres, name, job=None):
    S = h.shape[0]
    tm = _divisor(S, 1024, 16)
    row = pl.BlockSpec((tm, D_MODEL), lambda i, k: (i, 0))
    return _mm_call(name, (S // tm, N_DEV), 1, dgu8, pl.BlockSpec((1, tm, FF_BLK), lambda i, k: (k, i, 0)), (tm, FF_BLK),
                    G2, pl.BlockSpec((1, D_MODEL, FF_BLK), lambda i, k: (k, rb, 0)), (D_MODEL, FF_BLK), NT,
                    jax.ShapeDtypeStruct((S, D_MODEL), F32), row, (tm, D_MODEL), norm_bwd=(h, norm_w, dres), job=job)


def _unheads(x):
    h, S, d = x.shape
    return jnp.transpose(x, (1, 0, 2)).reshape(S, h * d)


def _exact3(v):
    rnd = lambda a: lax.reduce_precision(a, exponent_bits=8, mantissa_bits=7)
    hi = rnd(v)
    mid = rnd(v - hi)
    return hi, mid, rnd(v - hi - mid)


def _causal_mask(st, i, j, tq, tk, window):
    dist = (i * tq + lax.broadcasted_iota(jnp.int32, (tk, tq), 1)) - (j * tk + lax.broadcasted_iota(jnp.int32, (tk, tq), 0))
    mask = dist >= 0
    if window is not None:
        mask = mask & (dist < window)
    return jnp.where(mask, st, NEG)


def _attn_fwd(qT, k, vT1, *, tile, hb, window=None, sink=None, name, job=None):
    H, dqk, S = qT.shape
    G = H // k.shape[0]
    dvp = vT1.shape[1]
    dv = dvp - 16
    tq = tk = tile
    assert H % hb == 0 and (G == 1 or G % hb == 0)
    kvb = hb if G == 1 else 1
    grid = (H // hb, S // tq)
    n_in = 3 + (sink is not None)

    def body(*refs):
        q_ref, k_ref, v_ref = refs[:3]
        (o_ref, lse_ref), finish_job = _job_in_body(job, refs, n_in, 2, 0, grid)
        i = pl.program_id(1)
        carry = []
        for a in range(hb):
            if sink is not None:
                carry.append(jnp.zeros((1, tq), F32) + refs[3][a, :, 0:1])
                carry.append(jnp.where(lax.broadcasted_iota(jnp.int32, (dvp, tq), 0) == dv, 1.0, 0.0))
            else:
                carry.append(jnp.full((1, tq), NEG, F32))
                carry.append(jnp.zeros((dvp, tq), F32))

        def step(j, carry, masked):
            off = pl.multiple_of(j * tk, tk)
            out = []
            for a in range(hb):
                m, acc = carry[2 * a], carry[2 * a + 1]
                kv = a if kvb > 1 else 0
                st = jnp.dot(k_ref[kv, pl.ds(off, tk), :], q_ref[a], preferred_element_type=F32)
                if masked:
                    st = _causal_mask(st, i, j, tq, tk, window)
                m_new = jnp.maximum(m, jnp.max(st, axis=0, keepdims=True))
                pt = jnp.exp(st - m_new).astype(BF16)
                acc = jnp.exp(m - m_new) * acc + jnp.dot(v_ref[kv, :, pl.ds(off, tk)], pt, preferred_element_type=F32)
                out += [m_new, acc]
            return tuple(out)

        carry = tuple(carry)
        if window is None:
            carry = lax.fori_loop(0, i, functools.partial(step, masked=False), carry)
            carry = step(i, carry, True)
        else:
            lo = jnp.maximum((i * tq - (window - 1)) // tk, 0)
            carry = lax.fori_loop(lo, i + 1, functools.partial(step, masked=True), carry)
        for a in range(hb):
            m, acc = carry[2 * a], carry[2 * a + 1]
            l = acc[dv:dv + 1, :]
            o_ref[a] = acc[:dv, :] / l
            lse_ref[a] = m + jnp.log(l)
        finish_job()

    kv_idx = (lambda b: b) if G == 1 else (lambda b: (b * hb) // G)
    in_specs = [
        pl.BlockSpec((hb, dqk, tq), lambda b, i: (b, 0, i)),
        pl.BlockSpec((kvb, S, dqk), lambda b, i: (kv_idx(b), 0, 0)),
        pl.BlockSpec((kvb, dvp, S), lambda b, i: (kv_idx(b), 0, 0)),
    ]
    args = [qT, k, vT1]
    if sink is not None:
        in_specs += [pl.BlockSpec((hb, 1, LANES), lambda b, i: (b, 0, 0))]
        args += [sink]
    return _job_call(
        job, body, name=name, grid=grid, in_specs=in_specs,
        out_specs=[pl.BlockSpec((hb, dv, tq), lambda b, i: (b, 0, i)), pl.BlockSpec((hb, 1, tq), lambda b, i: (b, 0, i))],
        out_shape=[jax.ShapeDtypeStruct((H, dv, S), F32), jax.ShapeDtypeStruct((H, 1, S), F32)],
        args=args, scratch_shapes=[], aliases={}, dimension_semantics=("arbitrary", "arbitrary") if job is not None else ("parallel", "parallel"))


def _attn_bwd(q, qT, k, kT, v, oT, do, doT, lse, *, tile, hb, window=None, sink=None, real=None, extra=False, full=False, name):
    H, S, dqk = q.shape
    G = H // k.shape[0]
    dv = v.shape[2]
    tq = tk = tile
    nq = S // tq
    has_p = sink is not None
    real = dqk if real is None else real
    main = dqk if full else real
    assert H % hb == 0 and (G == 1 or G % hb == 0) and not (extra and real == dqk)
    kvb = hb if G == 1 else 1

    def body(*refs):
        q_ref, qT_ref, k_ref, kT_ref, v_ref, oT_ref, do_ref, doT_ref, lse_ref = refs[:9]
        p_ref = refs[9] if has_p else None
        pos = 10 if has_p else 9
        dq_ref, dk_ref, dv_ref = refs[pos: pos + 3]
        pos += 3
        ds_ref = refs[pos] if has_p else None
        pos += has_p
        dqx_ref, dkx_ref = (refs[pos], refs[pos + 1]) if extra else (None, None)
        delta = refs[-1]
        j = pl.program_id(1)

        @pl.when(j == 0)
        def _():
            dq_ref[...] = jnp.zeros_like(dq_ref)
            if extra:
                dqx_ref[...] = jnp.zeros_like(dqx_ref)
            for a in range(hb):
                drow = jnp.sum(doT_ref[a].astype(F32) * oT_ref[a], axis=0, keepdims=True)
                delta[a] = drow
                if has_p:
                    w = jnp.exp(p_ref[a, :, 0:1] - lse_ref[a])
                    ds_ref[a] = jnp.zeros((1, LANES), F32) - jnp.sum(w * drow, axis=1, keepdims=True)

        def step(i, carry, masked):
            off = pl.multiple_of(i * tq, tq)
            out = []
            for a in range(hb):
                dk, dvv = carry[2 * a], carry[2 * a + 1]
                kv = a if kvb > 1 else 0
                st = jnp.dot(k_ref[kv], qT_ref[a, :, pl.ds(off, tq)], preferred_element_type=F32)
                if masked:
                    st = _causal_mask(st, i, j, tq, tk, window)
                pt = jnp.exp(st - lse_ref[a, :, pl.ds(off, tq)])
                dvv = dvv + jnp.dot(pt.astype(BF16), do_ref[a, pl.ds(off, tq), :], preferred_element_type=F32)
                dpt = jnp.dot(v_ref[kv], doT_ref[a, :, pl.ds(off, tq)], preferred_element_type=F32)
                dsb = (pt * (dpt - delta[a, :, pl.ds(off, tq)])).astype(BF16)
                dk = dk + jnp.dot(dsb, q_ref[a, pl.ds(off, tq), :], preferred_element_type=F32)
                dqt = jnp.dot(kT_ref[kv], dsb, preferred_element_type=F32)
                dq_ref[a, :, pl.ds(off, tq)] += dqt[:main]
                if extra:
                    dqx_ref[a, :, pl.ds(off, tq)] += dqt[real:]
                out += [dk, dvv]
            return tuple(out)

        carry = (jnp.zeros((tk, dqk), F32), jnp.zeros((tk, dv), F32)) * hb
        if window is None:
            carry = step(j, carry, True)
            carry = lax.fori_loop(j + 1, nq, functools.partial(step, masked=False), carry)
        else:
            hi = jnp.minimum(nq - 1, ((j + 1) * tk + window - 2) // tq)
            carry = lax.fori_loop(j, hi + 1, functools.partial(step, masked=True), carry)
        for a in range(hb):
            dk_ref[a] = carry[2 * a][:, :main]
            if extra:
                dkx_ref[a] = carry[2 * a][:, real:]
            dv_ref[a] = carry[2 * a + 1]

    kv_idx = (lambda b: b) if G == 1 else (lambda b: (b * hb) // G)
    rows = lambda d: pl.BlockSpec((hb, S, d), lambda b, j: (b, 0, 0))
    colsT = lambda d: pl.BlockSpec((hb, d, S), lambda b, j: (b, 0, 0))
    in_specs = [
        rows(dqk), colsT(dqk),
        pl.BlockSpec((kvb, tk, dqk), lambda b, j: (kv_idx(b), j, 0)),
        pl.BlockSpec((kvb, dqk, tk), lambda b, j: (kv_idx(b), 0, j)),
        pl.BlockSpec((kvb, tk, dv), lambda b, j: (kv_idx(b), j, 0)),
        colsT(dv), rows(dv), colsT(dv),
        pl.BlockSpec((hb, 1, S), lambda b, j: (b, 0, 0)),
    ]
    args = [q, qT, k, kT, v, oT, do, doT, lse]
    if has_p:
        in_specs += [pl.BlockSpec((hb, 1, LANES), lambda b, j: (b, 0, 0))]
        args += [sink]
    out_specs = [colsT(main), pl.BlockSpec((hb, tk, main), lambda b, j: (b, j, 0)), pl.BlockSpec((hb, tk, dv), lambda b, j: (b, j, 0))]
    out_shape = [jax.ShapeDtypeStruct((H, main, S), F32), jax.ShapeDtypeStruct((H, S, main), F32), jax.ShapeDtypeStruct((H, S, dv), F32)]
    if has_p:
        out_specs += [pl.BlockSpec((hb, 1, LANES), lambda b, j: (b, 0, 0))]
        out_shape += [jax.ShapeDtypeStruct((H, 1, LANES), F32)]
    if extra:
        out_specs += [colsT(dqk - real), pl.BlockSpec((hb, tk, dqk - real), lambda b, j: (b, j, 0))]
        out_shape += [jax.ShapeDtypeStruct((H, dqk - real, S), F32), jax.ShapeDtypeStruct((H, S, dqk - real), F32)]
    return pl.pallas_call(
        body, name=name, grid=(H // hb, S // tk), in_specs=in_specs, out_specs=out_specs, out_shape=out_shape,
        scratch_shapes=[pltpu.VMEM((hb, 1, S), F32)],
        compiler_params=pltpu.CompilerParams(dimension_semantics=("parallel", "arbitrary")),
    )(*args)


def _rows_and_cols(x3):
    xb = x3.astype(BF16)
    return jnp.transpose(xb, (1, 0, 2)), jnp.transpose(xb, (1, 2, 0))


def _v_with_ones(v3):
    S, h, _ = v3.shape
    vT = jnp.transpose(v3.astype(BF16), (1, 2, 0))
    return jnp.concatenate([vT, jnp.ones((h, 1, S), BF16), jnp.zeros((h, 15, S), BF16)], axis=1)


def _from_T(oT):
    h, d, S = oT.shape
    return jnp.transpose(oT, (2, 0, 1)).reshape(S, h * d)


def _coords():
    return lax.axis_index("x"), lax.axis_index("y"), lax.axis_index("c")


def _peer(axis):
    x, y, c = _coords()
    return {"x": (1 - x, y, c), "y": (x, 1 - y, c), "c": (x, y, 1 - c)}[axis]


def _gather_job(bufs, rows=None):
    n = len(bufs)

    def copies(outs, send_sems, recv_sems):
        x, y, c = _coords()
        me, sibling = (x, y, c), (x, y, 1 - c)
        chips = [(1 - x, y), (x, 1 - y), (1 - x, 1 - y)]

        def copy(t, k, block, to):
            px, py, pc = block
            ref = outs[t].at[4 * px + 2 * py + pc]
            if rows is not None and rows[t] is not None:
                ref = ref.at[pl.ds(rows[t][0], rows[t][1])]
            return pltpu.make_async_remote_copy(ref, ref, send_sems.at[7 * t + k], recv_sems.at[7 * t + k], device_id=to, device_id_type=MESH)

        return copy, me, sibling, chips, c

    def start(ins, outs, send_sems, recv_sems):
        copy, me, sibling, chips, c = copies(outs, send_sems, recv_sems)
        for t in range(n):
            copy(t, 0, me, sibling).start()
            for j, chip in enumerate(chips):
                copy(t, 1 + j, me, (*chip, c)).start()

    def finish(ins, outs, send_sems, recv_sems):
        copy, me, sibling, chips, c = copies(outs, send_sems, recv_sems)
        for j, chip in enumerate(chips):
            for t in range(n):
                copy(t, 1 + j, (*chip, c), me).wait_recv()
                copy(t, 4 + j, (*chip, c), sibling).start()
        for t in range(n):
            copy(t, 0, sibling, me).wait_recv()
            for j, chip in enumerate(chips):
                copy(t, 4 + j, (*chip, 1 - c), me).wait_recv()
        for t in range(n):
            copy(t, 0, me, sibling).wait_send()
            for j, chip in enumerate(chips):
                copy(t, 1 + j, me, (*chip, c)).wait_send()
                copy(t, 4 + j, (*chip, c), sibling).wait_send()

    return dict(ins=list(bufs), outs=[jax.ShapeDtypeStruct(b.shape, b.dtype) for b in bufs], aliases={t: t for t in range(n)},
                n_sems=7 * n, start=start, finish=finish)


def _in_slot(local):
    x, y, c = _coords()
    buf = jnp.zeros((N_DEV,) + local.shape, local.dtype)
    return lax.dynamic_update_slice(buf, local[None], (4 * x + 2 * y + c, 0, 0))


def _pair_job(vs, axes):
    n = len(vs)
    axes = [axes] * n if isinstance(axes, str) else axes

    def copies(ins, outs, send_sems, recv_sems):
        out = []
        for t in range(n):
            me = lax.axis_index(axes[t])
            src = ins[t].at[1 - me] if len(ins[t].shape) == 3 else ins[t].at[:, 1 - me]
            out.append(pltpu.make_async_remote_copy(src, outs[t], send_sems.at[t], recv_sems.at[t], device_id=_peer(axes[t]), device_id_type=MESH))
        return out

    def start(*refs):
        for cp in copies(*refs):
            cp.start()

    def finish(*refs):
        for cp in copies(*refs):
            cp.wait()

    return dict(ins=list(vs), outs=[jax.ShapeDtypeStruct(v.shape[:-3] + v.shape[-2:], v.dtype) for v in vs], aliases={}, n_sems=n,
                start=start, finish=finish)


def _add_kept(v, got, axis, out, name):
    R, C = v.shape[-2:]
    lead = v.shape[0] if v.ndim == 4 else 1
    tm = _divisor(R, max(16, EW_TILE_BYTES // (_lanes(C) * (v.dtype.itemsize + got.dtype.itemsize + jnp.dtype(out).itemsize)) // 16 * 16), 16)
    me = lax.axis_index(axis).astype(jnp.int32).reshape(1)
    v4 = v.reshape(lead, 2, R, C)
    g3 = got.reshape(lead, R, C)

    def body(me_ref, v_ref, g_ref, o_ref):
        o_ref[...] = (v_ref[0].astype(F32) + g_ref[...].astype(F32)).astype(o_ref.dtype)

    res = pl.pallas_call(
        body, name=name, out_shape=jax.ShapeDtypeStruct((lead, R, C), out),
        grid_spec=pltpu.PrefetchScalarGridSpec(
            num_scalar_prefetch=1, grid=(lead, R // tm),
            in_specs=[pl.BlockSpec((1, 1, tm, C), lambda b, i, me: (b, me[0], i, 0)), pl.BlockSpec((1, tm, C), lambda b, i, me: (b, i, 0))],
            out_specs=pl.BlockSpec((1, tm, C), lambda b, i, me: (b, i, 0))),
    )(me, v4, g3)
    return res


def _cross_job(vs):
    n = len(vs)

    def copies(ins, outs, send_sems, recv_sems):
        x, y, _ = _coords()
        out = []
        for t in range(n):
            h = ins[t].shape[2] // 2
            out.append(pltpu.make_async_remote_copy(ins[t].at[1 - x, :, pl.ds(0, h)], outs[2 * t], send_sems.at[2 * t], recv_sems.at[2 * t],
                                                    device_id=_peer("x"), device_id_type=MESH))
            out.append(pltpu.make_async_remote_copy(ins[t].at[:, 1 - y, pl.ds(h, h)], outs[2 * t + 1], send_sems.at[2 * t + 1], recv_sems.at[2 * t + 1],
                                                    device_id=_peer("y"), device_id_type=MESH))
        return out

    def start(*refs):
        for cp in copies(*refs):
            cp.start()

    def finish(*refs):
        for cp in copies(*refs):
            cp.wait()

    outs = []
    for v in vs:
        outs += [jax.ShapeDtypeStruct((2, v.shape[2] // 2, v.shape[3]), v.dtype)] * 2
    return dict(ins=list(vs), outs=outs, aliases={}, n_sems=2 * n, start=start, finish=finish)


def _add_picked(v, got, axis, out, name):
    _, _, R, C = v.shape
    h = R // 2
    tm = _divisor(h, max(16, EW_TILE_BYTES // (_lanes(C) * (v.dtype.itemsize + got.dtype.itemsize + jnp.dtype(out).itemsize)) // 16 * 16), 16)
    me = lax.axis_index(axis).astype(jnp.int32).reshape(1)
    if axis == "x":
        v_map = lambda b, i, me: (me[0], b, i, 0)
    else:
        v_map = lambda b, i, me: (b, me[0], i + h // tm, 0)

    def body(me_ref, v_ref, g_ref, o_ref):
        o_ref[...] = (v_ref[0].astype(F32) + g_ref[...].astype(F32)).astype(o_ref.dtype)

    return pl.pallas_call(
        body, name=name, out_shape=jax.ShapeDtypeStruct((2, h, C), out),
        grid_spec=pltpu.PrefetchScalarGridSpec(
            num_scalar_prefetch=1, grid=(2, h // tm),
            in_specs=[pl.BlockSpec((1, 1, tm, C), v_map), pl.BlockSpec((1, tm, C), lambda b, i, me: (b, i, 0))],
            out_specs=pl.BlockSpec((1, tm, C), lambda b, i, me: (b, i, 0))),
    )(me, v, got)


def _reduce_scatter_steps(gs, tag):
    n = len(gs)
    vs = [g.reshape(4, 2, *g.shape[1:]) for g in gs]
    got = yield _pair_job(vs, "c")
    vs = [_add_kept(v, r, "c", BF16, f"rs_{tag}_add_c{t}") for t, (v, r) in enumerate(zip(vs, got))]
    vs = [v.reshape(2, 2, v.shape[1], v.shape[2]) for v in vs]
    got = yield _cross_job(vs)
    up = [_add_picked(v, r, "x", BF16, f"rs_{tag}_add_x{t}") for t, (v, r) in enumerate(zip(vs, got[0::2]))]
    lo = [_add_picked(v, r, "y", BF16, f"rs_{tag}_add_y{t}") for t, (v, r) in enumerate(zip(vs, got[1::2]))]
    got = yield _pair_job(up + lo, ["y"] * n + ["x"] * n)
    out = []
    for t in range(n):
        a = _add_kept(up[t], got[t], "y", F32, f"rs_{tag}_add_y2{t}")[0]
        b = _add_kept(lo[t], got[n + t], "x", F32, f"rs_{tag}_add_x2{t}")[0]
        out.append(jnp.concatenate([a, b], axis=0))
    return out


def _reduce_scatter(gs, tag):
    steps = _reduce_scatter_steps(gs, tag)
    job = next(steps)
    for stage in ("c", "xy", "yx"):
        got = _comm_call(job, f"rs_{tag}_{stage}")
        try:
            job = steps.send(got)
        except StopIteration as done:
            return done.value


def _all_reduce_small(v):
    def body(v_ref, o_ref, buf, send_sems, recv_sems):
        x, y, c = _coords()
        me = 4 * x + 2 * y + c
        buf[me] = v_ref[...]
        copies = []
        for k in range(1, N_DEV):
            peer = tuple((1 - a) if (k >> s) & 1 else a for a, s in ((x, 2), (y, 1), (c, 0)))
            cp = pltpu.make_async_remote_copy(v_ref, buf.at[me], send_sems.at[k - 1], recv_sems.at[k - 1], device_id=peer, device_id_type=MESH)
            cp.start()
            copies.append(cp)
        for cp in copies:
            cp.wait()
        acc = buf[0]
        for d in range(1, N_DEV):
            acc = acc + buf[d]
        o_ref[...] = acc

    vm = pl.BlockSpec(memory_space=pltpu.VMEM)
    return pl.pallas_call(
        body, name="all_reduce_small", in_specs=[vm], out_specs=vm, out_shape=jax.ShapeDtypeStruct(v.shape, F32),
        scratch_shapes=[pltpu.VMEM((N_DEV,) + v.shape, F32), pltpu.SemaphoreType.DMA((N_DEV - 1,)), pltpu.SemaphoreType.DMA((N_DEV - 1,))],
    )(v)


def _local_groups(w, dtype):
    mix_out = [w["ev_w_out"][0], w["od_w_out"][0]]
    layers = []
    for l in range(DEPTH):
        a = jnp.concatenate([w["ffa_w_down"][l], w["ffb_w_down"][l]], axis=0).astype(dtype)
        b = jnp.concatenate([w["ple_w_gate"][l], mix_out[l]], axis=0).astype(dtype)
        c = jnp.concatenate([w["ffa_w_gate_up"][l], w["ffb_w_gate_up"][l]], axis=0).astype(dtype)
        layers.append((a, b, c))
    strip = jnp.concatenate([w["ple_w_proj"].reshape(-1, STRIP_C), w["ev_w_ukv"][0], jnp.pad(w["ev_w_uq"][0], ((0, 0), (0, STRIP_C - 96))),
                             jnp.zeros((G3_ROWS - 896, STRIP_C), F32)], axis=0)
    m = jnp.concatenate([w["od_w_in"][0], w["ev_w_in"][0], strip, jnp.zeros((G3_ROWS, G3_COLS - STRIP0 - STRIP_C), F32)], axis=1).astype(dtype)
    return layers, m


def _ungroup_local(a, b, c, r3):
    out = {
        "ffa_w_down": jnp.stack([x[0] for x in a]), "ffb_w_down": jnp.stack([x[1] for x in a]),
        "ple_w_gate": jnp.stack([x[:128] for x in b]), "ev_w_out": b[0][128:][None], "od_w_out": b[1][128:][None],
        "ffa_w_gate_up": jnp.stack([x[0] for x in c]), "ffb_w_gate_up": jnp.stack([x[1] for x in c]),
        "od_w_in": r3[:, :OD_C][None], "ev_w_in": r3[:, OD_C:STRIP0][None],
    }
    strip = r3[:, STRIP0:STRIP0 + STRIP_C]
    out["ple_w_proj"] = strip[:512].reshape(2, PLE_DIM, STRIP_C)
    out["ev_w_ukv"] = strip[512:640][None]
    out["ev_w_uq"] = strip[640:896, :96][None]
    return out


def _cols(a):
    return jnp.transpose(a, (1, 0, 2)).reshape(a.shape[1], -1)


def _blocks(g, c):
    return jnp.transpose(g.reshape(g.shape[0], N_DEV, c), (1, 0, 2))


def _uq_permute(w):
    r = w.shape[0]
    w3 = w.reshape(r, B_HEADS, B_NOPE + B_ROPE)
    half = B_ROPE // 2
    return jnp.concatenate([w3[:, :, :B_NOPE].reshape(r, -1), w3[:, :, B_NOPE:B_NOPE + half].reshape(r, -1), w3[:, :, B_NOPE + half:].reshape(r, -1)], axis=1)


def _uq_unpermute(g):
    r = g.shape[0]
    half = B_ROPE // 2
    n = B_HEADS * B_NOPE
    parts = [g[:, :n].reshape(r, B_HEADS, B_NOPE), g[:, n:n + B_HEADS * half].reshape(r, B_HEADS, half), g[:, n + B_HEADS * half:].reshape(r, B_HEADS, half)]
    return jnp.concatenate(parts, axis=2).reshape(r, -1)


def _ukv_permute(w):
    r = w.shape[0]
    return jnp.transpose(w.reshape(r, B_HEADS, 2, B_NOPE), (0, 2, 1, 3)).reshape(r, -1)


def _ukv_unpermute(g):
    r = g.shape[0]
    return jnp.transpose(g.reshape(r, 2, B_HEADS, B_NOPE), (0, 2, 1, 3)).reshape(r, -1)


def _od_in_widen(w):
    n = C_HEADS * C_HEAD_DIM
    wide = lambda m: jnp.pad(m.reshape(-1, C_HEADS, C_HEAD_DIM), ((0, 0), (0, 0), (0, QK_PAD - C_HEAD_DIM))).reshape(m.shape[0], -1)
    return jnp.concatenate([wide(w[:, :n] * C_HEAD_DIM ** -0.5), wide(w[:, n:2 * n]), w[:, 2 * n:],
                            jnp.zeros((w.shape[0], ODD_IN_PAD - ODD_IN_AUG), w.dtype)], axis=1)


def _od_in_narrow(g):
    wp = C_HEADS * QK_PAD
    narrow = lambda m: m.reshape(-1, C_HEADS, QK_PAD)[:, :, :C_HEAD_DIM].reshape(m.shape[0], -1)
    return jnp.concatenate([narrow(g[:, :wp]) * C_HEAD_DIM ** -0.5, narrow(g[:, wp:2 * wp]), g[:, 2 * wp:ODD_IN_AUG]], axis=1)


def _misc_weights(G3):
    strip = G3[:, :, STRIP0:STRIP0 + STRIP_C]
    return {
        "od_w_in": _od_in_widen(_cols(G3[:, :, :OD_C])),
        "ev_w_in": jnp.pad(_cols(G3[:, :, OD_C:STRIP0]), ((0, 0), (0, EVEN_IN_PAD - EVEN_IN))),
        "ple_w_proj": [_cols(strip[:, i * PLE_DIM:(i + 1) * PLE_DIM]) for i in range(DEPTH)],
        "ev_w_ukv": _ukv_permute(_cols(strip[:, 512:640])),
        "ev_w_uq": _uq_permute(_cols(strip[:, 640:896, :96])),
    }


def _misc_grads(G):
    strip = jnp.concatenate([
        _blocks(G["ple_w_proj"][0], STRIP_C), _blocks(G["ple_w_proj"][1], STRIP_C), _blocks(_ukv_unpermute(G["ev_w_ukv"]), STRIP_C),
        jnp.pad(_blocks(_uq_unpermute(G["ev_w_uq"]), 96), ((0, 0), (0, 0), (0, STRIP_C - 96))),
        jnp.zeros((N_DEV, G3_ROWS - 896, STRIP_C), F32)], axis=1)
    return jnp.concatenate([_blocks(_od_in_narrow(G["od_w_in"]), OD_C), _blocks(G["ev_w_in"][:, :EVEN_IN], EV_C), strip,
                            jnp.zeros((N_DEV, G3_ROWS, G3_COLS - STRIP0 - STRIP_C), F32)], axis=2)


def _ffn_fwd(h, norm_w, W, f, i, tag, ride=None):
    job = ride() if ride else None
    res = _ffn_gate_up(h, norm_w, W["C"][i].reshape(2, 4, C_ROWS, FF_BLK), f, f"{tag}_gate_up", job=job)
    n, gu, act = res[:3]
    if job is not None:
        ride(res[3:])
    job = ride() if ride else None
    out = _ffn_down(act, W["A"][i], f, h, f"{tag}_down", job=job)
    if job is not None:
        out, got = out
        ride(got)
    return out, (h, n, gu, act)


def _ffn_bwd(dout, saved, norm_w, W, GB, f, i, tag, ride=None):
    h, n, gu, act = saved
    S = h.shape[0]
    def carried(call):
        job = ride() if ride else None
        res = call(job)
        if job is None:
            return res
        ride(res[1])
        return res[0]

    GB["A"][i][f] = carried(lambda job: _ffn_down_dw(act, dout, f"{tag}_down_dw", job=job))
    dgu = _ffn_down_dx(dout, W["A"][i], f, gu, f"{tag}_down_dx").reshape(N_DEV, S, FF_BLK)
    GB["C"][i][f] = carried(lambda job: _ffn_gate_up_dw(n, dgu, f"{tag}_gate_up_dw", job=job))
    return carried(lambda job: _ffn_gate_up_dx(dgu, W["C"][i], f, h, norm_w, dout, f"{tag}_gate_up_dx", job=job))


def _rope_tables(S):
    inv = ROPE_THETA ** (-jnp.arange(0, B_ROPE, 2, dtype=F32) / B_ROPE)
    ang = jnp.arange(S, dtype=F32)[:, None] * inv[None, :]
    return jnp.cos(ang), jnp.sin(ang)


def _alibi_columns(S):
    t = jnp.arange(S, dtype=jnp.int32)
    hi = ((t // 16) * 16).astype(F32)
    lo = (t % 16).astype(F32)
    slopes = 2.0 ** (-8.0 * jnp.arange(1, A_HEADS + 1, dtype=F32) / A_HEADS)
    zq = jnp.zeros((S, A_HEADS), F32)
    rest = QK_PAD - A_HEAD_DIM - 4
    qc = jnp.stack([-slopes[None, :] * hi[:, None], -slopes[None, :] * lo[:, None], zq + slopes[None, :], zq + slopes[None, :]] + [zq] * rest, axis=-1)
    one = jnp.ones((S, A_KV_HEADS), F32)
    zk = jnp.zeros((S, A_KV_HEADS), F32)
    kc = jnp.stack([one, one, zk + hi[:, None], zk + lo[:, None]] + [zk] * rest, axis=-1)
    return qc, kc


def _sink_prm(sinks):
    return jnp.zeros((A_HEADS, 1, LANES), F32).at[:, 0, 0].set(sinks.astype(F32))


def _with_ride(ride, call):
    job = ride() if ride else None
    res = call(job)
    if job is None:
        return res
    n_own = len(res) - len(job["outs"])
    ride(res[n_own:])
    return res[:n_own]


def _even_fwd(hn, h, W, ride=None):
    S = hn.shape[0]
    proj = _mm(hn, W["ev_w_in"], name="ev_in")
    a_q, a_k, a_v = proj[:, :512], proj[:, 512:640], proj[:, 640:768]
    c_q, c_kv = proj[:, 768:1024], proj[:, 1024:1152]
    kr1, kr2 = proj[:, 1152:1168], proj[:, 1168:1184]
    qc, kc = _alibi_columns(S)
    qa, qaT = _rows_and_cols(jnp.concatenate([(a_q * A_HEAD_DIM ** -0.5).reshape(S, A_HEADS, A_HEAD_DIM), qc], axis=-1))
    ka, kaT = _rows_and_cols(jnp.concatenate([a_k.reshape(S, A_KV_HEADS, A_HEAD_DIM), kc], axis=-1))
    va3 = a_v.reshape(S, A_KV_HEADS, A_HEAD_DIM)
    va = jnp.transpose(va3.astype(BF16), (1, 0, 2))
    prm = _sink_prm(W["ev_sinks"][0])
    oaT, lse_a = _with_ride(ride, lambda job: _attn_fwd(qaT, ka, _v_with_ones(va3), tile=SWA_TILE, hb=2, window=WINDOW, sink=prm,
                                                        name="swa_fwd", job=job))
    cqn = _rms_fwd(c_q, W["ev_cq_norm"], "ev_cq_norm")
    q_all = _mm(cqn, W["ev_w_uq"], name="ev_uq")
    ckvn = _rms_fwd(c_kv, W["ev_ckv_norm"], "ev_ckv_norm")
    kv_all = _mm(ckvn, W["ev_w_ukv"], name="ev_ukv")
    cos, sin = _rope_tables(S)
    cos8, sin8 = jnp.tile(cos, (1, B_HEADS)), jnp.tile(sin, (1, B_HEADS))
    q1, q2 = _rope(q_all[:, 512:640], q_all[:, 640:768], cos8, sin8, "ev_rope_q")
    k1, k2 = _rope(kr1, kr2, cos, sin, "ev_rope_k")
    half = B_ROPE // 2
    scale = (B_NOPE + B_ROPE) ** -0.5
    qb, qbT = _rows_and_cols(jnp.concatenate([q_all[:, :512].reshape(S, B_HEADS, B_NOPE), q1.reshape(S, B_HEADS, half), q2.reshape(S, B_HEADS, half)], axis=-1) * scale)
    kro = jnp.broadcast_to(jnp.concatenate([k1, k2], axis=1)[:, None, :], (S, B_HEADS, B_ROPE))
    kb, kbT = _rows_and_cols(jnp.concatenate([kv_all[:, :512].reshape(S, B_HEADS, B_NOPE), kro], axis=-1))
    vb3 = kv_all[:, 512:].reshape(S, B_HEADS, B_V)
    vb = jnp.transpose(vb3.astype(BF16), (1, 0, 2))
    obT, lse_b = _with_ride(ride, lambda job: _attn_fwd(qbT, kb, _v_with_ones(vb3), tile=min(ATTN_TILE_FWD, S), hb=2, name="mla_fwd", job=job))
    cat = jnp.concatenate([_from_T(oaT), _from_T(obT)], axis=1)
    out = _mm_w128(cat, W["B"][0], MIX_OUT_BLK, res=h, name="ev_out")
    return out, (hn, proj, (qa, qaT, ka, kaT, va, oaT, lse_a), prm, cqn, ckvn, (qb, qbT, kb, kbT, vb, obT, lse_b), cat)


def _even_bwd(dout, saved, W, GB, norm):
    hn, proj, (qa, qaT, ka, kaT, va, oaT, lse_a), prm, cqn, ckvn, (qb, qbT, kb, kbT, vb, obT, lse_b), cat = saved
    S = hn.shape[0]
    G = {}
    dcat = _mm_w128(dout, W["B"][0], MIX_OUT_BLK, tb=True, out=BF16, name="ev_out_dx")
    GB["B"][0] = _mm_w128_dw(cat, dout, MIX_OUT_BLK, GB["B"][0], "ev_out_dw")
    doa, doaT = _rows_and_cols(dcat[:, :512].reshape(S, A_HEADS, A_HEAD_DIM))
    dqaT, dka, dva, dsink = _attn_bwd(qa, qaT, ka, kaT, va, oaT, doa, doaT, lse_a, tile=SWA_TILE, hb=2, window=WINDOW, sink=prm, real=A_HEAD_DIM,
                                       name="swa_bwd")
    G["ev_sinks"] = dsink[:, 0, 0]
    dqa = _from_T(dqaT) * A_HEAD_DIM ** -0.5
    dka = dka.reshape(A_KV_HEADS, A_GROUP, S, A_HEAD_DIM).sum(axis=1)
    dva = dva.reshape(A_KV_HEADS, A_GROUP, S, A_HEAD_DIM).sum(axis=1)
    dob, dobT = _rows_and_cols(dcat[:, 512:].reshape(S, B_HEADS, B_V))
    dqbT, dkb, dvb = _attn_bwd(qb, qbT, kb, kbT, vb, obT, dob, dobT, lse_b, tile=min(ATTN_TILE, S), hb=1, name="mla_bwd")
    half = B_ROPE // 2
    dqb = jnp.transpose(dqbT, (2, 0, 1)) * (B_NOPE + B_ROPE) ** -0.5
    dkb = jnp.transpose(dkb, (1, 0, 2))
    cos, sin = _rope_tables(S)
    cos8, sin8 = jnp.tile(cos, (1, B_HEADS)), jnp.tile(sin, (1, B_HEADS))
    dq1, dq2 = _rope(dqb[:, :, B_NOPE:B_NOPE + half].reshape(S, -1), dqb[:, :, B_NOPE + half:].reshape(S, -1), cos8, -sin8, "ev_rope_q_bwd")
    dq_all = jnp.concatenate([dqb[:, :, :B_NOPE].reshape(S, -1), dq1, dq2], axis=1).astype(BF16)
    dkr = dkb[:, :, B_NOPE:].sum(axis=1)
    dk1, dk2 = _rope(dkr[:, :half], dkr[:, half:], cos, -sin, "ev_rope_k_bwd")
    dkv_all = jnp.concatenate([dkb[:, :, :B_NOPE].reshape(S, -1), _unheads(dvb)], axis=1).astype(BF16)
    G["ev_w_uq"] = _mm(cqn, dq_all, ta=True, name="ev_uq_dw")
    dcqn = _mm(dq_all, W["ev_w_uq"], tb=True, name="ev_uq_dx")
    dc_q, G["ev_cq_norm"] = _rms_bwd(dcqn, proj[:, 768:1024], W["ev_cq_norm"], None, "ev_cq_norm_bwd")
    G["ev_w_ukv"] = _mm(ckvn, dkv_all, ta=True, name="ev_ukv_dw")
    dckvn = _mm(dkv_all, W["ev_w_ukv"], tb=True, name="ev_ukv_dx")
    dc_kv, G["ev_ckv_norm"] = _rms_bwd(dckvn, proj[:, 1024:1152], W["ev_ckv_norm"], None, "ev_ckv_norm_bwd")
    dproj = jnp.concatenate([dqa, _unheads(dka), _unheads(dva), dc_q, dc_kv, dk1, dk2,
                             jnp.zeros((S, EVEN_IN_PAD - EVEN_IN), F32)], axis=1).astype(BF16)
    G["ev_w_in"] = _mm(hn, dproj, ta=True, name="ev_in_dw")
    dh, dnorm = _mm(dproj, W["ev_w_in"], tb=True, norm_bwd=(*norm, dout), name="ev_in_dx")
    return dh, dnorm, G


def _odd_fwd(hn, h, W):
    S = hn.shape[0]
    w = C_HEADS * C_HEAD_DIM
    wp = C_HEADS * QK_PAD
    proj = _mm(hn, W["od_w_in"], name="od_in")
    f_logit = proj[:, 2 * wp + w: 2 * wp + w + C_HEADS]
    logf = _logsig_fwd(f_logit, W["od_b_f"], "od_logsig")
    logc = _cumsum(logf, False, "od_cumsum")
    parts = list(_exact3(logc))
    ones = [jnp.ones((S, C_HEADS), F32)] * 3
    pad = [jnp.zeros((S, C_HEADS), F32)] * (QK_PAD - C_HEAD_DIM - 6)
    lead = ((0, 0), (0, 0), (C_HEAD_DIM, 0))
    q3 = proj[:, :wp].reshape(S, C_HEADS, QK_PAD) + jnp.pad(jnp.stack(parts + ones + pad, axis=-1), lead)
    k3 = proj[:, wp:2 * wp].reshape(S, C_HEADS, QK_PAD) + jnp.pad(jnp.stack(ones + [-p for p in parts] + pad, axis=-1), lead)
    q, qT = _rows_and_cols(q3)
    k, kT = _rows_and_cols(k3)
    v3 = proj[:, 2 * wp:2 * wp + w].reshape(S, C_HEADS, C_HEAD_DIM)
    v = jnp.transpose(v3.astype(BF16), (1, 0, 2))
    oT, lse = _attn_fwd(qT, k, _v_with_ones(v3), tile=min(ATTN_TILE_FWD, S), hb=2, name="fox_fwd")
    cat = _from_T(oT)
    out = _mm_w128(cat, W["B"][1], MIX_OUT_BLK, res=h, name="od_out")
    return out, (hn, q, qT, k, kT, v, f_logit, oT, lse, cat)


def _odd_bwd(dout, saved, W, GB, norm):
    hn, q, qT, k, kT, v, f_logit, oT, lse, cat = saved
    S = hn.shape[0]
    G = {}
    dcat = _mm_w128(dout, W["B"][1], MIX_OUT_BLK, tb=True, out=BF16, name="od_out_dx")
    GB["B"][1] = _mm_w128_dw(cat, dout, MIX_OUT_BLK, GB["B"][1], "od_out_dw")
    do, doT = _rows_and_cols(dcat.reshape(S, C_HEADS, C_HEAD_DIM))
    dqT, dk, dv, dqxT, dkx = _attn_bwd(q, qT, k, kT, v, oT, do, doT, lse, tile=min(ATTN_TILE, S), hb=1, real=C_HEAD_DIM, extra=True,
                                       full=True, name="fox_bwd")
    dlogc = jnp.transpose(dqxT[:, 0, :] - dkx[:, :, 3])
    dlogf = _cumsum(dlogc, True, "od_cumsum_bwd")
    df, db = _logsig_bwd(dlogf, f_logit, W["od_b_f"], "od_logsig_bwd")
    G["od_b_f"] = db
    dproj = jnp.concatenate([_from_T(dqT), _unheads(dk), _unheads(dv), df, jnp.zeros((S, ODD_IN_PAD - ODD_IN_AUG), F32)], axis=1).astype(BF16)
    G["od_w_in"] = _mm(hn, dproj, ta=True, name="od_in_dw")
    dh, dnorm = _mm(dproj, W["od_w_in"], tb=True, norm_bwd=(*norm, dout), name="od_in_dx")
    return dh, dnorm, G


class _Rider:
    def __init__(self, steps, tag):
        self.steps, self.tag, self.count, self.result = steps, tag, 0, None
        self.job = next(steps)

    def __call__(self, got=None):
        if got is None:
            return self.job
        try:
            self.job = self.steps.send(list(got))
        except StopIteration as done:
            self.job, self.result = None, done.value
        return None

    def finish(self):
        while self.job is not None:
            self.count += 1
            self(_comm_call(self.job, f"{self.tag}_{self.count}"))
        return self.result


def _gather_plan(W, slots):
    a0, b0, c0, m, a1, b1, c1 = (slots[key] for key in ("a0", "b0", "c0", "m", "a1", "b1", "c1"))
    (m,) = yield _gather_job([m])
    W.update(_misc_weights(m))
    (b0,) = yield _gather_job([b0])
    W["B"] = [b0]
    (c0,) = yield _gather_job([c0], rows=[(D_MODEL, D_MODEL)])
    W["C"] = [c0]
    a0, c1 = yield _gather_job([a0, c1], rows=[(DOWN_ROWS, DOWN_ROWS), (0, D_MODEL)])
    W["A"] = [a0]
    a1, b1 = yield _gather_job([a1, b1])
    (c1,) = yield _gather_job([c1], rows=[(D_MODEL, D_MODEL)])
    W["A"].append(a1)
    W["B"].append(b1)
    W["C"].append(c1)


def _local_step(x, p, target, W, slots):
    h = x
    saved = []
    gather = _Rider(_gather_plan(W, slots), "all_gather_rest")
    for i in range(DEPTH):
        t = f"l{i}"
        ride = gather if i == 0 else None
        h1, s_a = _ffn_fwd(h, W["ffa_norm"][i:i + 1], W, 0, i, f"{t}_ffa", ride)
        nm = _rms_fwd(h1, W["mix_norm"][i:i + 1], f"{t}_mix_norm")
        h2, s_m = _even_fwd(nm, h1, W, ride) if i % 2 == 0 else _odd_fwd(nm, h1, W)
        h3, s_b = _ffn_fwd(h2, W["ffb_norm"][i:i + 1], W, 1, i, f"{t}_ffb", ride)
        npl = _rms_fwd(h3, W["ple_norm"][i:i + 1], f"{t}_ple_norm")
        gpre = _mm_w128(npl, W["B"][i], PLE_GATE_BLK, name=f"{t}_ple_gate")
        pp = _mm(p[i], W["ple_w_proj"][i], name=f"{t}_ple_proj")
        h4 = _ple_fwd(h3, gpre, pp, f"{t}_ple")
        saved.append((s_a, h1, s_m, s_b, h3, npl, gpre, pp))
        h = h4
        if i == 0:
            gather.finish()
    dh, g_final, loss_cols = _final_fwd_bwd(h, W["final_norm"], target, "final")
    G = {"final_norm": g_final}
    GB = {"A": [[None, None] for _ in range(DEPTH)], "C": [[None, None] for _ in range(DEPTH)],
          "B": [lax.empty((N_DEV, B_ROWS, D_MODEL), BF16) for _ in range(DEPTH)]}
    per_layer = {n: [None] * DEPTH for n in ("ffa_norm", "mix_norm", "ffb_norm", "ple_norm", "ple_w_proj")}
    scatter = scatter_mid = None
    for i in reversed(range(DEPTH)):
        t = f"l{i}"
        s_a, h1, s_m, s_b, h3, npl, gpre, pp = saved[i]
        dgpre, dpp = _ple_bwd(dh, gpre, pp, f"{t}_ple_bwd")
        per_layer["ple_w_proj"][i] = _mm(p[i], dpp, ta=True, name=f"{t}_ple_proj_dw")
        GB["B"][i] = _mm_w128_dw(npl, dgpre, PLE_GATE_BLK, GB["B"][i], f"{t}_ple_gate_dw")
        dh, per_layer["ple_norm"][i] = _mm_w128(dgpre, W["B"][i], PLE_GATE_BLK, tb=True, norm_bwd=(h3, W["ple_norm"][i:i + 1], dh),
                                                name=f"{t}_ple_gate_dx")
        dh, per_layer["ffb_norm"][i] = _ffn_bwd(dh, s_b, W["ffb_norm"][i:i + 1], W, GB, 1, i, f"{t}_ffb", scatter)
        dh, per_layer["mix_norm"][i], g_mix = (_even_bwd if i % 2 == 0 else _odd_bwd)(dh, s_m, W, GB, (h1, W["mix_norm"][i:i + 1]))
        G.update(g_mix)
        if i == 0:
            G["ple_w_proj"] = per_layer["ple_w_proj"]
            mid = [GB["A"][0][1], GB["C"][0][1], GB["B"][0], _misc_grads(G).astype(BF16)]
            scatter_mid = _Rider(_reduce_scatter_steps(mid, "mid"), "rs_mid")
        dh, per_layer["ffa_norm"][i] = _ffn_bwd(dh, s_a, W["ffa_norm"][i:i + 1], W, GB, 0, i, f"{t}_ffa", scatter_mid)
        if i == DEPTH - 1:
            later = [GB["A"][i][0], GB["A"][i][1], GB["C"][i][0], GB["C"][i][1], GB["B"][i]]
            scatter = _Rider(_reduce_scatter_steps(later, "later"), "rs_later")
    for n in ("ffa_norm", "mix_norm", "ffb_norm", "ple_norm"):
        G[n] = jnp.concatenate(per_layer[n], axis=0)
    return loss_cols, dh, scatter.finish(), scatter_mid.finish(), [GB["A"][0][0], GB["C"][0][0]], G


def kernel(x, p, ffa_norm, ffa_w_gate_up, ffa_w_down, mix_norm, ffb_norm, ffb_w_gate_up, ffb_w_down, ple_norm, ple_w_gate, ple_w_proj, ev_w_in, ev_sinks, ev_cq_norm, ev_w_uq, ev_ckv_norm, ev_w_ukv, ev_w_out, od_w_in, od_b_f, od_w_out, final_norm, loss_target, m_ffa_norm, m_ffa_w_gate_up, m_ffa_w_down, m_mix_norm, m_ffb_norm, m_ffb_w_gate_up, m_ffb_w_down, m_ple_norm, m_ple_w_gate, m_ple_w_proj, m_ev_w_in, m_ev_sinks, m_ev_cq_norm, m_ev_w_uq, m_ev_ckv_norm, m_ev_w_ukv, m_ev_w_out, m_od_w_in, m_od_b_f, m_od_w_out, m_final_norm, v_ffa_norm, v_ffa_w_gate_up, v_ffa_w_down, v_mix_norm, v_ffb_norm, v_ffb_w_gate_up, v_ffb_w_down, v_ple_norm, v_ple_w_gate, v_ple_w_proj, v_ev_w_in, v_ev_sinks, v_ev_cq_norm, v_ev_w_uq, v_ev_ckv_norm, v_ev_w_ukv, v_ev_w_out, v_od_w_in, v_od_b_f, v_od_w_out, v_final_norm):
    given = dict(locals())
    w_in = {n: given[n] for n in WEIGHTS}

    layers, misc = _local_groups(w_in, BF16)
    (a0, b0, c0), (a1, b1, c1) = [[_in_slot(g) for g in layer] for layer in layers]
    a0, c0 = _comm_call(_gather_job([a0, c0], rows=[(0, DOWN_ROWS), (0, D_MODEL)]), "all_gather_first")
    W = {n: w_in[n] for n in SMALL}
    W["final_norm"] = final_norm.reshape(1, -1)
    W.update(A=[a0], C=[c0])
    slots = dict(a0=a0, b0=b0, c0=c0, m=_in_slot(misc), a1=a1, b1=b1, c1=c1)

    loss_cols, dx, r_later, r_mid, last, G = _local_step(x[0], p[:, 0], loss_target[0], W, slots)

    a1f, a1b, c1f, c1b, b1 = r_later
    a0b, c0b, b0, r_misc = r_mid
    a0f, c0f = _reduce_scatter(last, "last")
    grads = _ungroup_local([[a0f, a0b], [a1f, a1b]], [b0, b1], [[c0f, c0b], [c1f, c1b]], r_misc)
    layout = [(n, int(np.prod(w_in[n].shape))) for n in SMALL]
    vec = jnp.concatenate([G[n].astype(F32).reshape(-1) for n, _ in layout] + [jnp.sum(loss_cols).reshape(1)])
    vec = jnp.pad(vec, (0, N_DEV * SMALL_COLS - vec.shape[0])).reshape(N_DEV, SMALL_COLS)
    vec = _all_reduce_small(vec).reshape(-1)
    off = 0
    for n, size in layout:
        grads[n] = vec[off: off + size].reshape(w_in[n].shape)
        off += size
    loss = vec[off]

    delta, new_m, new_v = {}, {}, {}
    for n in WEIGHTS:
        shp = w_in[n].shape
        as2d = (lambda a: a.reshape(1, -1)) if len(shp) == 1 else (lambda a: a)
        d, nm, nv = _adamw(as2d(w_in[n]), as2d(grads[n]), as2d(given["m_" + n]), as2d(given["v_" + n]), f"adamw_{n}")
        delta[n], new_m[n], new_v[n] = d.reshape(shp), nm.reshape(shp), nv.reshape(shp)
    return (loss, dx[None], *[grads[n] for n in WEIGHTS], *[delta[n] for n in WEIGHTS],
            *[new_m[n] for n in WEIGHTS], *[new_v[n] for n in WEIGHTS])
```

```python
import functools

import numpy as np
import jax
import jax.numpy as jnp
from jax import lax
from jax.experimental import pallas as pl
from jax.experimental.pallas import tpu as pltpu

F32 = jnp.float32
BF16 = jnp.bfloat16
MESH = pl.DeviceIdType.MESH

D_MODEL = 1024
D_FF = 2816
RMS_EPS = 1e-6
PLE_DIM = 256
A_HEADS, A_KV_HEADS, A_HEAD_DIM, WINDOW = 8, 2, 64, 128
A_GROUP = A_HEADS // A_KV_HEADS
B_HEADS, B_Q_LORA, B_KV_LORA, B_NOPE, B_ROPE, B_V = 8, 256, 128, 64, 32, 64
ROPE_THETA = 10000.0
C_HEADS, C_HEAD_DIM = 16, 64
EVEN_IN = 1184
EVEN_IN_PAD = 1280
ODD_IN = 3088
ODD_IN_AUG = 2 * 16 * 80 + 1024 + 16
ODD_IN_PAD = 3840
DEPTH = 2
ADAM_LR, ADAM_B1, ADAM_B2, ADAM_EPS, ADAM_WD, ADAM_STEP = 0.001, 0.9, 0.999, 1e-08, 0.01, 10

N_DEV = 8
LANES = 128
SUBLANES = 8
EW_TILE_BYTES = 3 << 20
MM_VMEM_BYTES = 26 << 20
NEG = -1e30
ATTN_TILE = 1024
ATTN_TILE_FWD = 1024
SWA_TILE = 256
QK_PAD = 80

FF_BLK = D_FF // 4
DOWN_ROWS = D_FF // N_DEV
A_ROWS, B_ROWS, C_ROWS, G3_ROWS, G3_COLS = 2 * DOWN_ROWS, 256, 2 * D_MODEL, 1024, 768
PLE_GATE_BLK, MIX_OUT_BLK = 0, 1
OD_C, EV_C, STRIP_C = 386, 148, 128
STRIP0 = OD_C + EV_C

SMALL = ["ffa_norm", "mix_norm", "ffb_norm", "ple_norm", "ev_sinks", "ev_cq_norm", "ev_ckv_norm", "od_b_f", "final_norm"]
WEIGHTS = ["ffa_norm", "ffa_w_gate_up", "ffa_w_down", "mix_norm", "ffb_norm", "ffb_w_gate_up", "ffb_w_down", "ple_norm",
           "ple_w_gate", "ple_w_proj", "ev_w_in", "ev_sinks", "ev_cq_norm", "ev_w_uq", "ev_ckv_norm", "ev_w_ukv", "ev_w_out",
           "od_w_in", "od_b_f", "od_w_out", "final_norm"]
SMALL_COLS = 1280


def _divisor(n, cap, mult):
    if n <= cap:
        return n
    for t in range(cap - cap % mult, 0, -mult):
        if n % t == 0:
            return t
    raise ValueError(f"no tile for {n} under {cap} in steps of {mult}")


def _lanes(c):
    return -(-c // LANES) * LANES


def _ew(fn, rows, vecs, outs, reds=(), *, name):
    R = rows[0].shape[0]
    per_row = sum(_lanes(a.shape[1]) * a.dtype.itemsize for a in rows) + sum(_lanes(c) * jnp.dtype(d).itemsize for c, d in outs)
    tm = _divisor(R, max(16, EW_TILE_BYTES // per_row // 16 * 16), 16) if R % 16 == 0 else R
    n_r, n_v, n_o = len(rows), len(vecs), len(outs)

    def body(*refs):
        ins = [r[...] for r in refs[: n_r + n_v]]
        res = fn(*ins)
        if not isinstance(res, (tuple, list)):
            res = (res,)
        o_refs = refs[n_r + n_v: n_r + n_v + n_o]
        r_refs = refs[n_r + n_v + n_o:]
        for ref, val in zip(o_refs, res[:n_o]):
            ref[...] = val.astype(ref.dtype)
        if r_refs:
            @pl.when(pl.program_id(0) == 0)
            def _():
                for ref in r_refs:
                    ref[...] = jnp.zeros_like(ref)
            for ref, val in zip(r_refs, res[n_o:]):
                ref[...] += val

    in_specs = [pl.BlockSpec((tm, a.shape[1]), lambda i: (i, 0)) for a in rows]
    in_specs += [pl.BlockSpec((1, a.shape[1]), lambda i: (0, 0)) for a in vecs]
    out_specs = [pl.BlockSpec((tm, c), lambda i: (i, 0)) for c, _ in outs]
    out_specs += [pl.BlockSpec((1, c), lambda i: (0, 0)) for c in reds]
    out_shape = [jax.ShapeDtypeStruct((R, c), d) for c, d in outs] + [jax.ShapeDtypeStruct((1, c), F32) for c in reds]
    res = pl.pallas_call(body, name=name, grid=(R // tm,), in_specs=in_specs, out_specs=out_specs, out_shape=out_shape)(*rows, *vecs)
    return res[0] if len(res) == 1 else res


def _rms_fwd(x, w, name):
    def fn(x, w):
        y = x * lax.rsqrt(jnp.mean(x * x, axis=-1, keepdims=True) + RMS_EPS)
        return y * w
    return _ew(fn, [x], [w], [(x.shape[1], BF16)], name=name)


def _rms_bwd(dn, x, w, dres, name):
    def fn(dn, x, *rest):
        w = rest[-1]
        r = lax.rsqrt(jnp.mean(x * x, axis=-1, keepdims=True) + RMS_EPS)
        xh = x * r
        gw = dn * w
        dx = r * (gw - xh * jnp.mean(gw * xh, axis=-1, keepdims=True))
        if len(rest) == 2:
            dx = dx + rest[0]
        return dx, jnp.sum(dn * xh, axis=0, keepdims=True)
    rows = [dn, x] + ([dres] if dres is not None else [])
    return _ew(fn, rows, [w], [(x.shape[1], F32)], [x.shape[1]], name=name)


def _ple_fwd(h, gpre, pp, name):
    return _ew(lambda h, g, q: h + jax.nn.sigmoid(g) * q, [h, gpre, pp], [], [(h.shape[1], F32)], name=name)


def _ple_bwd(dh, gpre, pp, name):
    def fn(dh, g, q):
        sg = jax.nn.sigmoid(g)
        return dh * q * (sg * (1.0 - sg)), dh * sg
    return _ew(fn, [dh, gpre, pp], [], [(dh.shape[1], BF16), (dh.shape[1], BF16)], name=name)


def _rope(x1, x2, cos, sin, name):
    c = x1.shape[1]
    return _ew(lambda a, b, co, si: (a * co - b * si, a * si + b * co), [x1, x2, cos, sin], [], [(c, F32), (c, F32)], name=name)


def _logsig_fwd(f, b, name):
    def fn(f, b):
        z = f + b
        return jnp.minimum(z, 0.0) - jnp.log(1.0 + jnp.exp(-jnp.abs(z)))
    return _ew(fn, [f], [b], [(f.shape[1], F32)], name=name)


def _logsig_bwd(dlogf, f, b, name):
    def fn(d, f, b):
        df = d * jax.nn.sigmoid(-(f + b))
        return df, jnp.sum(df, axis=0, keepdims=True)
    return _ew(fn, [dlogf, f], [b], [(f.shape[1], F32)], [f.shape[1]], name=name)


def _final_fwd_bwd(h, w, target, name):
    d = h.shape[1]

    def fn(h, t, w):
        r = lax.rsqrt(jnp.mean(h * h, axis=-1, keepdims=True) + RMS_EPS)
        xh = h * r
        y = xh * w
        err = y - t
        dy = err * (1.0 / d)
        gw = dy * w
        dx = r * (gw - xh * jnp.mean(gw * xh, axis=-1, keepdims=True))
        return dx, jnp.sum(dy * xh, axis=0, keepdims=True), jnp.sum(err * err, axis=0, keepdims=True) * (0.5 / d)
    return _ew(fn, [h, target], [w], [(d, F32)], [d, d], name=name)


def _adamw(w, g, m, v, name):
    shape = w.shape
    c = shape[-1]
    w2, g2, m2, v2 = (a.reshape(-1, c) for a in (w, g, m, v))

    def fn(w, g, m, v):
        m = ADAM_B1 * m + (1.0 - ADAM_B1) * g
        v = ADAM_B2 * v + (1.0 - ADAM_B2) * jnp.square(g)
        m_hat = m / (1.0 - ADAM_B1 ** ADAM_STEP)
        v_hat = v / (1.0 - ADAM_B2 ** ADAM_STEP)
        delta = -ADAM_LR * (m_hat / (jnp.sqrt(v_hat) + ADAM_EPS) + ADAM_WD * w)
        return delta, m, v
    d, nm, nv = _ew(fn, [w2, g2, m2, v2], [], [(c, F32)] * 3, name=name)
    return d.reshape(shape), nm.reshape(shape), nv.reshape(shape)


def _split3(v):
    hi = v.astype(BF16)
    r1 = v - hi.astype(F32)
    mid = r1.astype(BF16)
    lo = (r1 - mid.astype(F32)).astype(BF16)
    return hi, mid, lo


def _cumsum(x, reverse, name):
    S, C = x.shape
    tm = _divisor(S, 512, 16)
    nt = S // tm

    def body(x_ref, o_ref, carry):
        @pl.when(pl.program_id(0) == 0)
        def _():
            carry[...] = jnp.zeros_like(carry)
        r = lax.broadcasted_iota(jnp.int32, (tm, tm), 0)
        c = lax.broadcasted_iota(jnp.int32, (tm, tm), 1)
        tri = jnp.where((c >= r) if reverse else (c <= r), 1.0, 0.0).astype(BF16)
        xv = x_ref[...]
        acc = jnp.zeros((tm, C), F32)
        for part in _split3(xv):
            acc = acc + jnp.dot(tri, part, preferred_element_type=F32)
        o_ref[...] = acc + carry[...]
        carry[...] += jnp.sum(xv, axis=0, keepdims=True)

    idx = (lambda i: (nt - 1 - i, 0)) if reverse else (lambda i: (i, 0))
    return pl.pallas_call(
        body, name=name, grid=(nt,), in_specs=[pl.BlockSpec((tm, C), idx)], out_specs=pl.BlockSpec((tm, C), idx),
        out_shape=jax.ShapeDtypeStruct((S, C), F32), scratch_shapes=[pltpu.VMEM((1, C), F32)],
    )(x)


NN = (((1,), (0,)), ((), ()))
NT = (((1,), (1,)), ((), ()))
TN = (((0,), (0,)), ((), ()))

HBM_SPEC = pl.BlockSpec(memory_space=pl.ANY)


def _job_in_body(job, refs, n_in, n_out, n_scr, grid):
    if job is None:
        return refs[n_in:], lambda: None
    ji, jo = len(job["ins"]), len(job["outs"])
    j_in = refs[n_in: n_in + ji]
    pos = n_in + ji
    own = list(refs[pos: pos + n_out])
    pos += n_out
    j_out = refs[pos: pos + jo]
    pos += jo
    own += list(refs[pos: pos + n_scr])
    ss, rs = refs[-2], refs[-1]
    first = functools.reduce(jnp.logical_and, [pl.program_id(d) == 0 for d in range(len(grid))])
    last = functools.reduce(jnp.logical_and, [pl.program_id(d) == n - 1 for d, n in enumerate(grid)])

    @pl.when(first)
    def _():
        job["start"](j_in, j_out, ss, rs)

    def finish():
        @pl.when(last)
        def _():
            job["finish"](j_in, j_out, ss, rs)

    return own, finish


def _job_call(job, body, *, name, grid, in_specs, out_specs, out_shape, args, scratch_shapes, aliases, dimension_semantics):
    in_specs, out_specs, out_shape, args, scratch_shapes = list(in_specs), list(out_specs), list(out_shape), list(args), list(scratch_shapes)
    aliases = dict(aliases)
    if job is not None:
        for i, o in job["aliases"].items():
            aliases[len(args) + i] = len(out_shape) + o
        in_specs += [HBM_SPEC] * len(job["ins"])
        args += list(job["ins"])
        out_specs += [HBM_SPEC] * len(job["outs"])
        out_shape += list(job["outs"])
        scratch_shapes += [pltpu.SemaphoreType.DMA((job["n_sems"],)), pltpu.SemaphoreType.DMA((job["n_sems"],))]
    return pl.pallas_call(
        body, name=name, grid=grid, in_specs=in_specs, out_specs=out_specs, out_shape=out_shape,
        scratch_shapes=scratch_shapes, input_output_aliases=aliases,
        compiler_params=pltpu.CompilerParams(dimension_semantics=dimension_semantics),
    )(*args)


def _comm_call(job, name):
    def body(*refs):
        ji, jo = len(job["ins"]), len(job["outs"])
        job["start"](refs[:ji], refs[ji: ji + jo], refs[-2], refs[-1])
        job["finish"](refs[:ji], refs[ji: ji + jo], refs[-2], refs[-1])

    return pl.pallas_call(
        body, name=name, in_specs=[HBM_SPEC] * len(job["ins"]), out_specs=[HBM_SPEC] * len(job["outs"]), out_shape=list(job["outs"]),
        input_output_aliases=dict(job["aliases"]),
        scratch_shapes=[pltpu.SemaphoreType.DMA((job["n_sems"],)), pltpu.SemaphoreType.DMA((job["n_sems"],))],
    )(*job["ins"])


def _mm_call(name, grid, k_axis, a, a_spec, a2d, b, b_spec, b2d, dims, out_sds, out_spec, o2d, *,
             alpha=1.0, res=None, res_spec=None, into=None, job=None, norm_bwd=None):
    nk = grid[k_axis]
    n_in = 2 + (res is not None) + (into is not None) + (3 if norm_bwd is not None else 0)
    n_out = 2 if norm_bwd is not None else 1

    def body(*refs):
        a_ref, b_ref = refs[0], refs[1]
        res_ref = refs[2] if res is not None else None
        own, finish_job = _job_in_body(job, refs, n_in, n_out, 1, grid)
        o_ref, acc_ref = own[0], own[-1]
        k = pl.program_id(k_axis)

        @pl.when(k == 0)
        def _():
            acc_ref[...] = jnp.zeros_like(acc_ref)

        if norm_bwd is not None:
            x_ref, w_ref, dres_ref = refs[n_in - 3: n_in]
            dw_ref = own[1]

            @pl.when(functools.reduce(jnp.logical_and, [pl.program_id(d) == 0 for d in range(len(grid))]))
            def _():
                dw_ref[...] = jnp.zeros_like(dw_ref)

        av = a_ref[...].reshape(a2d).astype(BF16)
        bv = b_ref[...].reshape(b2d).astype(BF16)
        acc_ref[...] += lax.dot_general(av, bv, dims, preferred_element_type=F32)

        @pl.when(k == nk - 1)
        def _():
            r = acc_ref[...]
            if alpha != 1.0:
                r = r * alpha
            if res_ref is not None:
                r = res_ref[...].reshape(o2d) + r
            if norm_bwd is not None:
                x = x_ref[...]
                rs = lax.rsqrt(jnp.mean(x * x, axis=-1, keepdims=True) + RMS_EPS)
                xh = x * rs
                gw = r * w_ref[...]
                dw_ref[...] += jnp.sum(r * xh, axis=0, keepdims=True)
                r = dres_ref[...] + rs * (gw - xh * jnp.mean(gw * xh, axis=-1, keepdims=True))
            o_ref[...] = r.reshape(o_ref.shape).astype(o_ref.dtype)

        finish_job()

    in_specs, args = [a_spec, b_spec], [a, b]
    if res is not None:
        in_specs.append(res_spec)
        args.append(res)
    aliases = {}
    if into is not None:
        aliases = {len(args): 0}
        in_specs.append(pl.BlockSpec(memory_space=pl.ANY))
        args.append(into)
        out_sds = jax.ShapeDtypeStruct(into.shape, into.dtype)
    out_specs, out_shape = [out_spec], [out_sds]
    if norm_bwd is not None:
        vec = pl.BlockSpec((1, o2d[1]), lambda *_: (0, 0))
        in_specs += [out_spec, vec, out_spec]
        args += list(norm_bwd)
        out_specs.append(vec)
        out_shape.append(jax.ShapeDtypeStruct((1, o2d[1]), F32))
    serial = job is not None or norm_bwd is not None
    sem = tuple("arbitrary" if d == k_axis or serial else "parallel" for d in range(len(grid)))
    res_all = _job_call(
        job, body, name=name, grid=grid, in_specs=in_specs, out_specs=out_specs, out_shape=out_shape, args=args,
        scratch_shapes=[pltpu.VMEM(o2d, F32)], aliases=aliases, dimension_semantics=sem)
    own = res_all[0] if n_out == 1 else tuple(res_all[:n_out])
    return own if job is None else (own, res_all[n_out:])


def _mm(a, b, *, ta=False, tb=False, out=F32, res=None, alpha=1.0, norm_bwd=None, name):
    K, M = a.shape if ta else a.shape[::-1]
    N = b.shape[0] if tb else b.shape[1]
    assert (b.shape[1] if tb else b.shape[0]) == K, (a.shape, b.shape, ta, tb)
    tk = _divisor(K, 1024, LANES)
    tn = _divisor(N, 1408, LANES)
    assert norm_bwd is None or tn == N
    for cap in (1024, 512, 256, 128):
        tm = _divisor(M, cap, LANES if ta else 16)
        est = 2 * (tm * tk * a.dtype.itemsize + tk * tn * b.dtype.itemsize + tm * tn * jnp.dtype(out).itemsize)
        est += tm * tn * 4 + (2 * tm * tn * 4 if res is not None else 0) + (4 * tm * tn * 4 if norm_bwd is not None else 0)
        if est <= MM_VMEM_BYTES:
            break
    a_spec = pl.BlockSpec((tk, tm), lambda i, j, k: (k, i)) if ta else pl.BlockSpec((tm, tk), lambda i, j, k: (i, k))
    b_spec = pl.BlockSpec((tn, tk), lambda i, j, k: (j, k)) if tb else pl.BlockSpec((tk, tn), lambda i, j, k: (k, j))
    o_spec = pl.BlockSpec((tm, tn), lambda i, j, k: (i, j))
    dims = (((0 if ta else 1,), (1 if tb else 0,)), ((), ()))
    return _mm_call(name, (M // tm, N // tn, K // tk), 2, a, a_spec, (tk, tm) if ta else (tm, tk), b, b_spec,
                    (tn, tk) if tb else (tk, tn), dims, jax.ShapeDtypeStruct((M, N), out), o_spec, (tm, tn),
                    alpha=alpha, res=res, res_spec=o_spec, norm_bwd=norm_bwd)


def _w128_spec(blk):
    return pl.BlockSpec((N_DEV, 128, D_MODEL), lambda *_: (0, blk, 0))


def _mm_w128(a, G1, blk, *, tb=False, res=None, out=F32, norm_bwd=None, name):
    S = a.shape[0]
    tm = _divisor(S, 512, 16)
    row = pl.BlockSpec((tm, D_MODEL), lambda i, k: (i, 0))
    return _mm_call(name, (S // tm, 1), 1, a, row, (tm, D_MODEL), G1, _w128_spec(blk), (D_MODEL, D_MODEL), NT if tb else NN,
                    jax.ShapeDtypeStruct((S, D_MODEL), out), row, (tm, D_MODEL), res=res, res_spec=row, norm_bwd=norm_bwd)


def _mm_w128_dw(a, b, blk, into, name):
    S = a.shape[0]
    tk = _divisor(S, 1024, 16)
    row = pl.BlockSpec((tk, D_MODEL), lambda i, k: (k, 0))
    return _mm_call(name, (1, S // tk), 1, a, row, (tk, D_MODEL), b, row, (tk, D_MODEL), TN, None, _w128_spec(blk),
                    (D_MODEL, D_MODEL), into=into)


def _ffn_gate_up(h, norm_w, G2v, rb, name, job=None):
    S = h.shape[0]
    tm = _divisor(S, 1024, 16)
    grid = (S // tm, 4)

    def body(*refs):
        h_ref, nw_ref, w_ref = refs[:3]
        (n_ref, gu_ref, act_ref, n_scr), finish_job = _job_in_body(job, refs, 3, 3, 1, grid)

        @pl.when(pl.program_id(1) == 0)
        def _():
            x = h_ref[...]
            y = x * lax.rsqrt(jnp.mean(x * x, axis=-1, keepdims=True) + RMS_EPS)
            n_scr[...] = (y * nw_ref[...]).astype(BF16)
            n_ref[...] = n_scr[...]

        nv = n_scr[...]
        g = jnp.dot(nv, w_ref[0, 0], preferred_element_type=F32)
        u = jnp.dot(nv, w_ref[1, 0], preferred_element_type=F32)
        gu_ref[0, 0] = g.astype(BF16)
        gu_ref[1, 0] = u.astype(BF16)
        act_ref[0] = (g * jax.nn.sigmoid(g) * u).astype(BF16)
        finish_job()

    row = pl.BlockSpec((tm, D_MODEL), lambda i, j: (i, 0))
    return _job_call(
        job, body, name=name, grid=grid,
        in_specs=[row, pl.BlockSpec((1, D_MODEL), lambda i, j: (0, 0)), pl.BlockSpec((2, 1, D_MODEL, FF_BLK), lambda i, j: (0, j, rb, 0))],
        out_specs=[row, pl.BlockSpec((2, 1, tm, FF_BLK), lambda i, j: (0, j, i, 0)), pl.BlockSpec((1, tm, FF_BLK), lambda i, j: (j, i, 0))],
        out_shape=[jax.ShapeDtypeStruct((S, D_MODEL), BF16), jax.ShapeDtypeStruct((2, 4, S, FF_BLK), BF16), jax.ShapeDtypeStruct((4, S, FF_BLK), BF16)],
        args=[h, norm_w, G2v], scratch_shapes=[pltpu.VMEM((tm, D_MODEL), BF16)], aliases={},
        dimension_semantics=("arbitrary" if job is not None else "parallel", "arbitrary"))


def _ffn_down(act, G1, ob, h, name, job=None):
    S = h.shape[0]
    tm = _divisor(S, 1024, 16)
    row = pl.BlockSpec((tm, D_MODEL), lambda i, k: (i, 0))
    return _mm_call(name, (S // tm, 4), 1, act, pl.BlockSpec((1, tm, FF_BLK), lambda i, k: (k, i, 0)), (tm, FF_BLK),
                    G1, pl.BlockSpec((2, DOWN_ROWS, D_MODEL), lambda i, k: (k, ob, 0)), (FF_BLK, D_MODEL), NN,
                    jax.ShapeDtypeStruct((S, D_MODEL), F32), row, (tm, D_MODEL), alpha=0.5, res=h, res_spec=row, job=job)


def _ffn_down_dx(dh, G1, ob, gu, name):
    S = dh.shape[0]
    tm = _divisor(S, 512, 16)

    def body(dh_ref, w_ref, gu_ref, o_ref):
        w = w_ref[...].reshape(FF_BLK, D_MODEL)
        dact = lax.dot_general(dh_ref[...].astype(BF16), w, NT, preferred_element_type=F32) * 0.5
        g = gu_ref[0, 0].astype(F32)
        u = gu_ref[1, 0].astype(F32)
        sg = jax.nn.sigmoid(g)
        o_ref[0, 0] = (dact * u * (sg * (1.0 + g * (1.0 - sg)))).astype(BF16)
        o_ref[1, 0] = (dact * (g * sg)).astype(BF16)

    blk = pl.BlockSpec((2, 1, tm, FF_BLK), lambda j, i: (0, j, i, 0))
    return pl.pallas_call(
        body, name=name, grid=(4, S // tm),
        in_specs=[pl.BlockSpec((tm, D_MODEL), lambda j, i: (i, 0)), pl.BlockSpec((2, DOWN_ROWS, D_MODEL), lambda j, i: (j, ob, 0)), blk],
        out_specs=blk, out_shape=jax.ShapeDtypeStruct((2, 4, S, FF_BLK), BF16),
    )(dh, G1, gu)


def _ffn_down_dw(act, dh, name, job=None):
    S = dh.shape[0]
    tk = _divisor(S, 1024, 16)
    return _mm_call(name, (4, S // tk), 1, act, pl.BlockSpec((1, tk, FF_BLK), lambda j, k: (j, k, 0)), (tk, FF_BLK),
                    dh, pl.BlockSpec((tk, D_MODEL), lambda j, k: (k, 0)), (tk, D_MODEL), TN,
                    jax.ShapeDtypeStruct((N_DEV, DOWN_ROWS, D_MODEL), BF16),
                    pl.BlockSpec((2, DOWN_ROWS, D_MODEL), lambda j, k: (j, 0, 0)), (FF_BLK, D_MODEL), alpha=0.5, job=job)


def _ffn_gate_up_dw(n, dgu8, name, job=None):
    S = n.shape[0]
    tk = _divisor(S, 1024, 16)
    return _mm_call(name, (N_DEV, S // tk), 1, n, pl.BlockSpec((tk, D_MODEL), lambda b, k: (k, 0)), (tk, D_MODEL),
                    dgu8, pl.BlockSpec((1, tk, FF_BLK), lambda b, k: (b, k, 0)), (tk, FF_BLK), TN,
                    jax.ShapeDtypeStruct((N_DEV, D_MODEL, FF_BLK), BF16),
                    pl.BlockSpec((1, D_MODEL, FF_BLK), lambda b, k: (b, 0, 0)), (D_MODEL, FF_BLK), job=job)


def _ffn_gate_up_dx(dgu8, G2, rb, h, norm_w, dres, name, job=None):
    S = h.shape[0]
    tm = _divisor(S, 1024, 16)
    row = pl.BlockSpec((tm, D_MODEL), lambda i, k: (i, 0))
    return _mm_call(name, (S // tm, N_DEV), 1, dgu8, pl.BlockSpec((1, tm, FF_BLK), lambda i, k: (k, i, 0)), (tm, FF_BLK),
                    G2, pl.BlockSpec((1, D_MODEL, FF_BLK), lambda i, k: (k, rb, 0)), (D_MODEL, FF_BLK), NT,
                    jax.ShapeDtypeStruct((S, D_MODEL), F32), row, (tm, D_MODEL), norm_bwd=(h, norm_w, dres), job=job)


def _unheads(x):
    h, S, d = x.shape
    return jnp.transpose(x, (1, 0, 2)).reshape(S, h * d)


def _exact3(v):
    rnd = lambda a: lax.reduce_precision(a, exponent_bits=8, mantissa_bits=7)
    hi = rnd(v)
    mid = rnd(v - hi)
    return hi, mid, rnd(v - hi - mid)


def _causal_mask(st, i, j, tq, tk, window):
    dist = (i * tq + lax.broadcasted_iota(jnp.int32, (tk, tq), 1)) - (j * tk + lax.broadcasted_iota(jnp.int32, (tk, tq), 0))
    mask = dist >= 0
    if window is not None:
        mask = mask & (dist < window)
    return jnp.where(mask, st, NEG)


def _attn_fwd(qT, k, vT1, *, tile, hb, window=None, sink=None, name, job=None):
    H, dqk, S = qT.shape
    G = H // k.shape[0]
    dvp = vT1.shape[1]
    dv = dvp - 16
    tq = tk = tile
    assert H % hb == 0 and (G == 1 or G % hb == 0)
    kvb = hb if G == 1 else 1
    grid = (H // hb, S // tq)
    n_in = 3 + (sink is not None)

    def body(*refs):
        q_ref, k_ref, v_ref = refs[:3]
        (o_ref, lse_ref), finish_job = _job_in_body(job, refs, n_in, 2, 0, grid)
        i = pl.program_id(1)
        carry = []
        for a in range(hb):
            if sink is not None:
                carry.append(jnp.zeros((1, tq), F32) + refs[3][a, :, 0:1])
                carry.append(jnp.where(lax.broadcasted_iota(jnp.int32, (dvp, tq), 0) == dv, 1.0, 0.0))
            else:
                carry.append(jnp.full((1, tq), NEG, F32))
                carry.append(jnp.zeros((dvp, tq), F32))

        def step(j, carry, masked):
            off = pl.multiple_of(j * tk, tk)
            out = []
            for a in range(hb):
                m, acc = carry[2 * a], carry[2 * a + 1]
                kv = a if kvb > 1 else 0
                st = jnp.dot(k_ref[kv, pl.ds(off, tk), :], q_ref[a], preferred_element_type=F32)
                if masked:
                    st = _causal_mask(st, i, j, tq, tk, window)
                m_new = jnp.maximum(m, jnp.max(st, axis=0, keepdims=True))
                pt = jnp.exp(st - m_new).astype(BF16)
                acc = jnp.exp(m - m_new) * acc + jnp.dot(v_ref[kv, :, pl.ds(off, tk)], pt, preferred_element_type=F32)
                out += [m_new, acc]
            return tuple(out)

        carry = tuple(carry)
        if window is None:
            carry = lax.fori_loop(0, i, functools.partial(step, masked=False), carry)
            carry = step(i, carry, True)
        else:
            lo = jnp.maximum((i * tq - (window - 1)) // tk, 0)
            carry = lax.fori_loop(lo, i + 1, functools.partial(step, masked=True), carry)
        for a in range(hb):
            m, acc = carry[2 * a], carry[2 * a + 1]
            l = acc[dv:dv + 1, :]
            o_ref[a] = acc[:dv, :] / l
            lse_ref[a] = m + jnp.log(l)
        finish_job()

    kv_idx = (lambda b: b) if G == 1 else (lambda b: (b * hb) // G)
    in_specs = [
        pl.BlockSpec((hb, dqk, tq), lambda b, i: (b, 0, i)),
        pl.BlockSpec((kvb, S, dqk), lambda b, i: (kv_idx(b), 0, 0)),
        pl.BlockSpec((kvb, dvp, S), lambda b, i: (kv_idx(b), 0, 0)),
    ]
    args = [qT, k, vT1]
    if sink is not None:
        in_specs += [pl.BlockSpec((hb, 1, LANES), lambda b, i: (b, 0, 0))]
        args += [sink]
    return _job_call(
        job, body, name=name, grid=grid, in_specs=in_specs,
        out_specs=[pl.BlockSpec((hb, dv, tq), lambda b, i: (b, 0, i)), pl.BlockSpec((hb, 1, tq), lambda b, i: (b, 0, i))],
        out_shape=[jax.ShapeDtypeStruct((H, dv, S), F32), jax.ShapeDtypeStruct((H, 1, S), F32)],
        args=args, scratch_shapes=[], aliases={}, dimension_semantics=("arbitrary", "arbitrary") if job is not None else ("parallel", "parallel"))


def _attn_bwd(q, qT, k, kT, v, oT, do, doT, lse, *, tile, hb, window=None, sink=None, real=None, extra=False, full=False, name):
    H, S, dqk = q.shape
    G = H // k.shape[0]
    dv = v.shape[2]
    tq = tk = tile
    nq = S // tq
    has_p = sink is not None
    real = dqk if real is None else real
    main = dqk if full else real
    assert H % hb == 0 and (G == 1 or G % hb == 0) and not (extra and real == dqk)
    kvb = hb if G == 1 else 1

    def body(*refs):
        q_ref, qT_ref, k_ref, kT_ref, v_ref, oT_ref, do_ref, doT_ref, lse_ref = refs[:9]
        p_ref = refs[9] if has_p else None
        pos = 10 if has_p else 9
        dq_ref, dk_ref, dv_ref = refs[pos: pos + 3]
        pos += 3
        ds_ref = refs[pos] if has_p else None
        pos += has_p
        dqx_ref, dkx_ref = (refs[pos], refs[pos + 1]) if extra else (None, None)
        delta = refs[-1]
        j = pl.program_id(1)

        @pl.when(j == 0)
        def _():
            dq_ref[...] = jnp.zeros_like(dq_ref)
            if extra:
                dqx_ref[...] = jnp.zeros_like(dqx_ref)
            for a in range(hb):
                drow = jnp.sum(doT_ref[a].astype(F32) * oT_ref[a], axis=0, keepdims=True)
                delta[a] = drow
                if has_p:
                    w = jnp.exp(p_ref[a, :, 0:1] - lse_ref[a])
                    ds_ref[a] = jnp.zeros((1, LANES), F32) - jnp.sum(w * drow, axis=1, keepdims=True)

        def step(i, carry, masked):
            off = pl.multiple_of(i * tq, tq)
            out = []
            for a in range(hb):
                dk, dvv = carry[2 * a], carry[2 * a + 1]
                kv = a if kvb > 1 else 0
                st = jnp.dot(k_ref[kv], qT_ref[a, :, pl.ds(off, tq)], preferred_element_type=F32)
                if masked:
                    st = _causal_mask(st, i, j, tq, tk, window)
                pt = jnp.exp(st - lse_ref[a, :, pl.ds(off, tq)])
                dvv = dvv + jnp.dot(pt.astype(BF16), do_ref[a, pl.ds(off, tq), :], preferred_element_type=F32)
                dpt = jnp.dot(v_ref[kv], doT_ref[a, :, pl.ds(off, tq)], preferred_element_type=F32)
                dsb = (pt * (dpt - delta[a, :, pl.ds(off, tq)])).astype(BF16)
                dk = dk + jnp.dot(dsb, q_ref[a, pl.ds(off, tq), :], preferred_element_type=F32)
                dqt = jnp.dot(kT_ref[kv], dsb, preferred_element_type=F32)
                dq_ref[a, :, pl.ds(off, tq)] += dqt[:main]
                if extra:
                    dqx_ref[a, :, pl.ds(off, tq)] += dqt[real:]
                out += [dk, dvv]
            return tuple(out)

        carry = (jnp.zeros((tk, dqk), F32), jnp.zeros((tk, dv), F32)) * hb
        if window is None:
            carry = step(j, carry, True)
            carry = lax.fori_loop(j + 1, nq, functools.partial(step, masked=False), carry)
        else:
            hi = jnp.minimum(nq - 1, ((j + 1) * tk + window - 2) // tq)
            carry = lax.fori_loop(j, hi + 1, functools.partial(step, masked=True), carry)
        for a in range(hb):
            dk_ref[a] = carry[2 * a][:, :main]
            if extra:
                dkx_ref[a] = carry[2 * a][:, real:]
            dv_ref[a] = carry[2 * a + 1]

    kv_idx = (lambda b: b) if G == 1 else (lambda b: (b * hb) // G)
    rows = lambda d: pl.BlockSpec((hb, S, d), lambda b, j: (b, 0, 0))
    colsT = lambda d: pl.BlockSpec((hb, d, S), lambda b, j: (b, 0, 0))
    in_specs = [
        rows(dqk), colsT(dqk),
        pl.BlockSpec((kvb, tk, dqk), lambda b, j: (kv_idx(b), j, 0)),
        pl.BlockSpec((kvb, dqk, tk), lambda b, j: (kv_idx(b), 0, j)),
        pl.BlockSpec((kvb, tk, dv), lambda b, j: (kv_idx(b), j, 0)),
        colsT(dv), rows(dv), colsT(dv),
        pl.BlockSpec((hb, 1, S), lambda b, j: (b, 0, 0)),
    ]
    args = [q, qT, k, kT, v, oT, do, doT, lse]
    if has_p:
        in_specs += [pl.BlockSpec((hb, 1, LANES), lambda b, j: (b, 0, 0))]
        args += [sink]
    out_specs = [colsT(main), pl.BlockSpec((hb, tk, main), lambda b, j: (b, j, 0)), pl.BlockSpec((hb, tk, dv), lambda b, j: (b, j, 0))]
    out_shape = [jax.ShapeDtypeStruct((H, main, S), F32), jax.ShapeDtypeStruct((H, S, main), F32), jax.ShapeDtypeStruct((H, S, dv), F32)]
    if has_p:
        out_specs += [pl.BlockSpec((hb, 1, LANES), lambda b, j: (b, 0, 0))]
        out_shape += [jax.ShapeDtypeStruct((H, 1, LANES), F32)]
    if extra:
        out_specs += [colsT(dqk - real), pl.BlockSpec((hb, tk, dqk - real), lambda b, j: (b, j, 0))]
        out_shape += [jax.ShapeDtypeStruct((H, dqk - real, S), F32), jax.ShapeDtypeStruct((H, S, dqk - real), F32)]
    return pl.pallas_call(
        body, name=name, grid=(H // hb, S // tk), in_specs=in_specs, out_specs=out_specs, out_shape=out_shape,
        scratch_shapes=[pltpu.VMEM((hb, 1, S), F32)],
        compiler_params=pltpu.CompilerParams(dimension_semantics=("parallel", "arbitrary")),
    )(*args)


def _rows_and_cols(x3):
    xb = x3.astype(BF16)
    return jnp.transpose(xb, (1, 0, 2)), jnp.transpose(xb, (1, 2, 0))


def _v_with_ones(v3):
    S, h, _ = v3.shape
    vT = jnp.transpose(v3.astype(BF16), (1, 2, 0))
    return jnp.concatenate([vT, jnp.ones((h, 1, S), BF16), jnp.zeros((h, 15, S), BF16)], axis=1)


def _from_T(oT):
    h, d, S = oT.shape
    return jnp.transpose(oT, (2, 0, 1)).reshape(S, h * d)


def _coords():
    return lax.axis_index("x"), lax.axis_index("y"), lax.axis_index("c")


def _peer(axis):
    x, y, c = _coords()
    return {"x": (1 - x, y, c), "y": (x, 1 - y, c), "c": (x, y, 1 - c)}[axis]


def _gather_job(bufs, rows=None):
    n = len(bufs)

    def copies(outs, send_sems, recv_sems):
        x, y, c = _coords()
        me, sibling = (x, y, c), (x, y, 1 - c)
        chips = [(1 - x, y), (x, 1 - y), (1 - x, 1 - y)]

        def copy(t, k, block, to):
            px, py, pc = block
            ref = outs[t].at[4 * px + 2 * py + pc]
            if rows is not None and rows[t] is not None:
                ref = ref.at[pl.ds(rows[t][0], rows[t][1])]
            return pltpu.make_async_remote_copy(ref, ref, send_sems.at[7 * t + k], recv_sems.at[7 * t + k], device_id=to, device_id_type=MESH)

        return copy, me, sibling, chips, c

    def start(ins, outs, send_sems, recv_sems):
        copy, me, sibling, chips, c = copies(outs, send_sems, recv_sems)
        for t in range(n):
            copy(t, 0, me, sibling).start()
            for j, chip in enumerate(chips):
                copy(t, 1 + j, me, (*chip, c)).start()

    def finish(ins, outs, send_sems, recv_sems):
        copy, me, sibling, chips, c = copies(outs, send_sems, recv_sems)
        for j, chip in enumerate(chips):
            for t in range(n):
                copy(t, 1 + j, (*chip, c), me).wait_recv()
                copy(t, 4 + j, (*chip, c), sibling).start()
        for t in range(n):
            copy(t, 0, sibling, me).wait_recv()
            for j, chip in enumerate(chips):
                copy(t, 4 + j, (*chip, 1 - c), me).wait_recv()
        for t in range(n):
            copy(t, 0, me, sibling).wait_send()
            for j, chip in enumerate(chips):
                copy(t, 1 + j, me, (*chip, c)).wait_send()
                copy(t, 4 + j, (*chip, c), sibling).wait_send()

    return dict(ins=list(bufs), outs=[jax.ShapeDtypeStruct(b.shape, b.dtype) for b in bufs], aliases={t: t for t in range(n)},
                n_sems=7 * n, start=start, finish=finish)


def _in_slot(local):
    x, y, c = _coords()
    buf = lax.empty((N_DEV,) + local.shape, local.dtype)
    return lax.dynamic_update_slice(buf, local[None], (4 * x + 2 * y + c, 0, 0))


def _pair_job(vs, axes):
    n = len(vs)
    axes = [axes] * n if isinstance(axes, str) else axes

    def copies(ins, outs, send_sems, recv_sems):
        out = []
        for t in range(n):
            me = lax.axis_index(axes[t])
            src = ins[t].at[1 - me] if len(ins[t].shape) == 3 else ins[t].at[:, 1 - me]
            out.append(pltpu.make_async_remote_copy(src, outs[t], send_sems.at[t], recv_sems.at[t], device_id=_peer(axes[t]), device_id_type=MESH))
        return out

    def start(*refs):
        for cp in copies(*refs):
            cp.start()

    def finish(*refs):
        for cp in copies(*refs):
            cp.wait()

    return dict(ins=list(vs), outs=[jax.ShapeDtypeStruct(v.shape[:-3] + v.shape[-2:], v.dtype) for v in vs], aliases={}, n_sems=n,
                start=start, finish=finish)


def _add_kept(v, got, axis, out, name):
    R, C = v.shape[-2:]
    lead = v.shape[0] if v.ndim == 4 else 1
    tm = _divisor(R, max(16, EW_TILE_BYTES // (_lanes(C) * (v.dtype.itemsize + got.dtype.itemsize + jnp.dtype(out).itemsize)) // 16 * 16), 16)
    me = lax.axis_index(axis).astype(jnp.int32).reshape(1)
    v4 = v.reshape(lead, 2, R, C)
    g3 = got.reshape(lead, R, C)

    def body(me_ref, v_ref, g_ref, o_ref):
        o_ref[...] = (v_ref[0].astype(F32) + g_ref[...].astype(F32)).astype(o_ref.dtype)

    res = pl.pallas_call(
        body, name=name, out_shape=jax.ShapeDtypeStruct((lead, R, C), out),
        grid_spec=pltpu.PrefetchScalarGridSpec(
            num_scalar_prefetch=1, grid=(lead, R // tm),
            in_specs=[pl.BlockSpec((1, 1, tm, C), lambda b, i, me: (b, me[0], i, 0)), pl.BlockSpec((1, tm, C), lambda b, i, me: (b, i, 0))],
            out_specs=pl.BlockSpec((1, tm, C), lambda b, i, me: (b, i, 0))),
    )(me, v4, g3)
    return res


def _cross_job(vs):
    n = len(vs)

    def copies(ins, outs, send_sems, recv_sems):
        x, y, _ = _coords()
        out = []
        for t in range(n):
            h = ins[t].shape[2] // 2
            out.append(pltpu.make_async_remote_copy(ins[t].at[1 - x, :, pl.ds(0, h)], outs[2 * t], send_sems.at[2 * t], recv_sems.at[2 * t],
                                                    device_id=_peer("x"), device_id_type=MESH))
            out.append(pltpu.make_async_remote_copy(ins[t].at[:, 1 - y, pl.ds(h, h)], outs[2 * t + 1], send_sems.at[2 * t + 1], recv_sems.at[2 * t + 1],
                                                    device_id=_peer("y"), device_id_type=MESH))
        return out

    def start(*refs):
        for cp in copies(*refs):
            cp.start()

    def finish(*refs):
        for cp in copies(*refs):
            cp.wait()

    outs = []
    for v in vs:
        outs += [jax.ShapeDtypeStruct((2, v.shape[2] // 2, v.shape[3]), v.dtype)] * 2
    return dict(ins=list(vs), outs=outs, aliases={}, n_sems=2 * n, start=start, finish=finish)


def _add_picked(v, got, axis, out, name):
    _, _, R, C = v.shape
    h = R // 2
    tm = _divisor(h, max(16, EW_TILE_BYTES // (_lanes(C) * (v.dtype.itemsize + got.dtype.itemsize + jnp.dtype(out).itemsize)) // 16 * 16), 16)
    me = lax.axis_index(axis).astype(jnp.int32).reshape(1)
    if axis == "x":
        v_map = lambda b, i, me: (me[0], b, i, 0)
    else:
        v_map = lambda b, i, me: (b, me[0], i + h // tm, 0)

    def body(me_ref, v_ref, g_ref, o_ref):
        o_ref[...] = (v_ref[0].astype(F32) + g_ref[...].astype(F32)).astype(o_ref.dtype)

    return pl.pallas_call(
        body, name=name, out_shape=jax.ShapeDtypeStruct((2, h, C), out),
        grid_spec=pltpu.PrefetchScalarGridSpec(
            num_scalar_prefetch=1, grid=(2, h // tm),
            in_specs=[pl.BlockSpec((1, 1, tm, C), v_map), pl.BlockSpec((1, tm, C), lambda b, i, me: (b, i, 0))],
            out_specs=pl.BlockSpec((1, tm, C), lambda b, i, me: (b, i, 0))),
    )(me, v, got)


def _reduce_scatter_steps(gs, tag):
    n = len(gs)
    vs = [g.reshape(4, 2, *g.shape[1:]) for g in gs]
    got = yield _pair_job(vs, "c")
    vs = [_add_kept(v, r, "c", BF16, f"rs_{tag}_add_c{t}") for t, (v, r) in enumerate(zip(vs, got))]
    vs = [v.reshape(2, 2, v.shape[1], v.shape[2]) for v in vs]
    got = yield _cross_job(vs)
    up = [_add_picked(v, r, "x", BF16, f"rs_{tag}_add_x{t}") for t, (v, r) in enumerate(zip(vs, got[0::2]))]
    lo = [_add_picked(v, r, "y", BF16, f"rs_{tag}_add_y{t}") for t, (v, r) in enumerate(zip(vs, got[1::2]))]
    got = yield _pair_job(up + lo, ["y"] * n + ["x"] * n)
    out = []
    for t in range(n):
        a = _add_kept(up[t], got[t], "y", F32, f"rs_{tag}_add_y2{t}")[0]
        b = _add_kept(lo[t], got[n + t], "x", F32, f"rs_{tag}_add_x2{t}")[0]
        out.append(jnp.concatenate([a, b], axis=0))
    return out


def _reduce_scatter(gs, tag):
    steps = _reduce_scatter_steps(gs, tag)
    job = next(steps)
    for stage in ("c", "xy", "yx"):
        got = _comm_call(job, f"rs_{tag}_{stage}")
        try:
            job = steps.send(got)
        except StopIteration as done:
            return done.value


def _all_reduce_small(v):
    def body(v_ref, o_ref, buf, send_sems, recv_sems):
        x, y, c = _coords()
        me = 4 * x + 2 * y + c
        buf[me] = v_ref[...]
        copies = []
        for k in range(1, N_DEV):
            peer = tuple((1 - a) if (k >> s) & 1 else a for a, s in ((x, 2), (y, 1), (c, 0)))
            cp = pltpu.make_async_remote_copy(v_ref, buf.at[me], send_sems.at[k - 1], recv_sems.at[k - 1], device_id=peer, device_id_type=MESH)
            cp.start()
            copies.append(cp)
        for cp in copies:
            cp.wait()
        acc = buf[0]
        for d in range(1, N_DEV):
            acc = acc + buf[d]
        o_ref[...] = acc

    vm = pl.BlockSpec(memory_space=pltpu.VMEM)
    return pl.pallas_call(
        body, name="all_reduce_small", in_specs=[vm], out_specs=vm, out_shape=jax.ShapeDtypeStruct(v.shape, F32),
        scratch_shapes=[pltpu.VMEM((N_DEV,) + v.shape, F32), pltpu.SemaphoreType.DMA((N_DEV - 1,)), pltpu.SemaphoreType.DMA((N_DEV - 1,))],
    )(v)


def _local_groups(w, dtype):
    mix_out = [w["ev_w_out"][0], w["od_w_out"][0]]
    layers = []
    for l in range(DEPTH):
        a = jnp.concatenate([w["ffa_w_down"][l], w["ffb_w_down"][l]], axis=0).astype(dtype)
        b = jnp.concatenate([w["ple_w_gate"][l], mix_out[l]], axis=0).astype(dtype)
        c = jnp.concatenate([w["ffa_w_gate_up"][l], w["ffb_w_gate_up"][l]], axis=0).astype(dtype)
        layers.append((a, b, c))
    strip = jnp.concatenate([w["ple_w_proj"].reshape(-1, STRIP_C), w["ev_w_ukv"][0], jnp.pad(w["ev_w_uq"][0], ((0, 0), (0, STRIP_C - 96))),
                             jnp.zeros((G3_ROWS - 896, STRIP_C), F32)], axis=0)
    m = jnp.concatenate([w["od_w_in"][0], w["ev_w_in"][0], strip, jnp.zeros((G3_ROWS, G3_COLS - STRIP0 - STRIP_C), F32)], axis=1).astype(dtype)
    return layers, m


def _ungroup_local(a, b, c, r3):
    out = {
        "ffa_w_down": jnp.stack([x[0] for x in a]), "ffb_w_down": jnp.stack([x[1] for x in a]),
        "ple_w_gate": jnp.stack([x[:128] for x in b]), "ev_w_out": b[0][128:][None], "od_w_out": b[1][128:][None],
        "ffa_w_gate_up": jnp.stack([x[0] for x in c]), "ffb_w_gate_up": jnp.stack([x[1] for x in c]),
        "od_w_in": r3[:, :OD_C][None], "ev_w_in": r3[:, OD_C:STRIP0][None],
    }
    strip = r3[:, STRIP0:STRIP0 + STRIP_C]
    out["ple_w_proj"] = strip[:512].reshape(2, PLE_DIM, STRIP_C)
    out["ev_w_ukv"] = strip[512:640][None]
    out["ev_w_uq"] = strip[640:896, :96][None]
    return out


def _cols(a):
    return jnp.transpose(a, (1, 0, 2)).reshape(a.shape[1], -1)


def _blocks(g, c):
    return jnp.transpose(g.reshape(g.shape[0], N_DEV, c), (1, 0, 2))


def _uq_permute(w):
    r = w.shape[0]
    w3 = w.reshape(r, B_HEADS, B_NOPE + B_ROPE)
    half = B_ROPE // 2
    return jnp.concatenate([w3[:, :, :B_NOPE].reshape(r, -1), w3[:, :, B_NOPE:B_NOPE + half].reshape(r, -1), w3[:, :, B_NOPE + half:].reshape(r, -1)], axis=1)


def _uq_unpermute(g):
    r = g.shape[0]
    half = B_ROPE // 2
    n = B_HEADS * B_NOPE
    parts = [g[:, :n].reshape(r, B_HEADS, B_NOPE), g[:, n:n + B_HEADS * half].reshape(r, B_HEADS, half), g[:, n + B_HEADS * half:].reshape(r, B_HEADS, half)]
    return jnp.concatenate(parts, axis=2).reshape(r, -1)


def _ukv_permute(w):
    r = w.shape[0]
    return jnp.transpose(w.reshape(r, B_HEADS, 2, B_NOPE), (0, 2, 1, 3)).reshape(r, -1)


def _ukv_unpermute(g):
    r = g.shape[0]
    return jnp.transpose(g.reshape(r, 2, B_HEADS, B_NOPE), (0, 2, 1, 3)).reshape(r, -1)


def _od_in_widen(w):
    n = C_HEADS * C_HEAD_DIM
    wide = lambda m: jnp.pad(m.reshape(-1, C_HEADS, C_HEAD_DIM), ((0, 0), (0, 0), (0, QK_PAD - C_HEAD_DIM))).reshape(m.shape[0], -1)
    return jnp.concatenate([wide(w[:, :n] * C_HEAD_DIM ** -0.5), wide(w[:, n:2 * n]), w[:, 2 * n:],
                            jnp.zeros((w.shape[0], ODD_IN_PAD - ODD_IN_AUG), w.dtype)], axis=1)


def _od_in_narrow(g):
    wp = C_HEADS * QK_PAD
    narrow = lambda m: m.reshape(-1, C_HEADS, QK_PAD)[:, :, :C_HEAD_DIM].reshape(m.shape[0], -1)
    return jnp.concatenate([narrow(g[:, :wp]) * C_HEAD_DIM ** -0.5, narrow(g[:, wp:2 * wp]), g[:, 2 * wp:ODD_IN_AUG]], axis=1)


def _misc_weights(G3):
    strip = G3[:, :, STRIP0:STRIP0 + STRIP_C]
    return {
        "od_w_in": _od_in_widen(_cols(G3[:, :, :OD_C])),
        "ev_w_in": jnp.pad(_cols(G3[:, :, OD_C:STRIP0]), ((0, 0), (0, EVEN_IN_PAD - EVEN_IN))),
        "ple_w_proj": [_cols(strip[:, i * PLE_DIM:(i + 1) * PLE_DIM]) for i in range(DEPTH)],
        "ev_w_ukv": _ukv_permute(_cols(strip[:, 512:640])),
        "ev_w_uq": _uq_permute(_cols(strip[:, 640:896, :96])),
    }


def _misc_grads(G):
    strip = jnp.concatenate([
        _blocks(G["ple_w_proj"][0], STRIP_C), _blocks(G["ple_w_proj"][1], STRIP_C), _blocks(_ukv_unpermute(G["ev_w_ukv"]), STRIP_C),
        jnp.pad(_blocks(_uq_unpermute(G["ev_w_uq"]), 96), ((0, 0), (0, 0), (0, STRIP_C - 96))),
        jnp.zeros((N_DEV, G3_ROWS - 896, STRIP_C), F32)], axis=1)
    return jnp.concatenate([_blocks(_od_in_narrow(G["od_w_in"]), OD_C), _blocks(G["ev_w_in"][:, :EVEN_IN], EV_C), strip,
                            jnp.zeros((N_DEV, G3_ROWS, G3_COLS - STRIP0 - STRIP_C), F32)], axis=2)


def _ffn_fwd(h, norm_w, W, f, i, tag, ride=None):
    job = ride() if ride else None
    res = _ffn_gate_up(h, norm_w, W["C"][i].reshape(2, 4, C_ROWS, FF_BLK), f, f"{tag}_gate_up", job=job)
    n, gu, act = res[:3]
    if job is not None:
        ride(res[3:])
    job = ride() if ride else None
    out = _ffn_down(act, W["A"][i], f, h, f"{tag}_down", job=job)
    if job is not None:
        out, got = out
        ride(got)
    return out, (h, n, gu, act)


def _ffn_bwd(dout, saved, norm_w, W, GB, f, i, tag, ride=None):
    h, n, gu, act = saved
    S = h.shape[0]
    def carried(call):
        job = ride() if ride else None
        res = call(job)
        if job is None:
            return res
        ride(res[1])
        return res[0]

    GB["A"][i][f] = carried(lambda job: _ffn_down_dw(act, dout, f"{tag}_down_dw", job=job))
    dgu = _ffn_down_dx(dout, W["A"][i], f, gu, f"{tag}_down_dx").reshape(N_DEV, S, FF_BLK)
    res = carried(lambda job: _ffn_gate_up_dx(dgu, W["C"][i], f, h, norm_w, dout, f"{tag}_gate_up_dx", job=job))
    GB["C"][i][f] = carried(lambda job: _ffn_gate_up_dw(n, dgu, f"{tag}_gate_up_dw", job=job))
    return res


def _rope_tables(S):
    inv = ROPE_THETA ** (-jnp.arange(0, B_ROPE, 2, dtype=F32) / B_ROPE)
    ang = jnp.arange(S, dtype=F32)[:, None] * inv[None, :]
    return jnp.cos(ang), jnp.sin(ang)


def _alibi_columns(S):
    t = jnp.arange(S, dtype=jnp.int32)
    hi = ((t // 16) * 16).astype(F32)
    lo = (t % 16).astype(F32)
    slopes = 2.0 ** (-8.0 * jnp.arange(1, A_HEADS + 1, dtype=F32) / A_HEADS)
    zq = jnp.zeros((S, A_HEADS), F32)
    rest = QK_PAD - A_HEAD_DIM - 4
    qc = jnp.stack([-slopes[None, :] * hi[:, None], -slopes[None, :] * lo[:, None], zq + slopes[None, :], zq + slopes[None, :]] + [zq] * rest, axis=-1)
    one = jnp.ones((S, A_KV_HEADS), F32)
    zk = jnp.zeros((S, A_KV_HEADS), F32)
    kc = jnp.stack([one, one, zk + hi[:, None], zk + lo[:, None]] + [zk] * rest, axis=-1)
    return qc, kc


def _sink_prm(sinks):
    return jnp.zeros((A_HEADS, 1, LANES), F32).at[:, 0, 0].set(sinks.astype(F32))


def _with_ride(ride, call):
    job = ride() if ride else None
    res = call(job)
    if job is None:
        return res
    n_own = len(res) - len(job["outs"])
    ride(res[n_own:])
    return res[:n_own]


def _even_fwd(hn, h, W, ride=None):
    S = hn.shape[0]
    proj = _mm(hn, W["ev_w_in"], name="ev_in")
    a_q, a_k, a_v = proj[:, :512], proj[:, 512:640], proj[:, 640:768]
    c_q, c_kv = proj[:, 768:1024], proj[:, 1024:1152]
    kr1, kr2 = proj[:, 1152:1168], proj[:, 1168:1184]
    qc, kc = _alibi_columns(S)
    qa, qaT = _rows_and_cols(jnp.concatenate([(a_q * A_HEAD_DIM ** -0.5).reshape(S, A_HEADS, A_HEAD_DIM), qc], axis=-1))
    ka, kaT = _rows_and_cols(jnp.concatenate([a_k.reshape(S, A_KV_HEADS, A_HEAD_DIM), kc], axis=-1))
    va3 = a_v.reshape(S, A_KV_HEADS, A_HEAD_DIM)
    va = jnp.transpose(va3.astype(BF16), (1, 0, 2))
    prm = _sink_prm(W["ev_sinks"][0])
    oaT, lse_a = _with_ride(ride, lambda job: _attn_fwd(qaT, ka, _v_with_ones(va3), tile=SWA_TILE, hb=2, window=WINDOW, sink=prm,
                                                        name="swa_fwd", job=job))
    cqn = _rms_fwd(c_q, W["ev_cq_norm"], "ev_cq_norm")
    q_all = _mm(cqn, W["ev_w_uq"], name="ev_uq")
    ckvn = _rms_fwd(c_kv, W["ev_ckv_norm"], "ev_ckv_norm")
    kv_all = _mm(ckvn, W["ev_w_ukv"], name="ev_ukv")
    cos, sin = _rope_tables(S)
    cos8, sin8 = jnp.tile(cos, (1, B_HEADS)), jnp.tile(sin, (1, B_HEADS))
    q1, q2 = _rope(q_all[:, 512:640], q_all[:, 640:768], cos8, sin8, "ev_rope_q")
    k1, k2 = _rope(kr1, kr2, cos, sin, "ev_rope_k")
    half = B_ROPE // 2
    scale = (B_NOPE + B_ROPE) ** -0.5
    qb, qbT = _rows_and_cols(jnp.concatenate([q_all[:, :512].reshape(S, B_HEADS, B_NOPE), q1.reshape(S, B_HEADS, half), q2.reshape(S, B_HEADS, half)], axis=-1) * scale)
    kro = jnp.broadcast_to(jnp.concatenate([k1, k2], axis=1)[:, None, :], (S, B_HEADS, B_ROPE))
    kb, kbT = _rows_and_cols(jnp.concatenate([kv_all[:, :512].reshape(S, B_HEADS, B_NOPE), kro], axis=-1))
    vb3 = kv_all[:, 512:].reshape(S, B_HEADS, B_V)
    vb = jnp.transpose(vb3.astype(BF16), (1, 0, 2))
    obT, lse_b = _with_ride(ride, lambda job: _attn_fwd(qbT, kb, _v_with_ones(vb3), tile=min(ATTN_TILE_FWD, S), hb=2, name="mla_fwd", job=job))
    cat = jnp.concatenate([_from_T(oaT), _from_T(obT)], axis=1)
    out = _mm_w128(cat, W["B"][0], MIX_OUT_BLK, res=h, name="ev_out")
    return out, (hn, proj, (qa, qaT, ka, kaT, va, oaT, lse_a), prm, cqn, ckvn, (qb, qbT, kb, kbT, vb, obT, lse_b), cat)


def _even_bwd(dout, saved, W, GB, norm):
    hn, proj, (qa, qaT, ka, kaT, va, oaT, lse_a), prm, cqn, ckvn, (qb, qbT, kb, kbT, vb, obT, lse_b), cat = saved
    S = hn.shape[0]
    G = {}
    dcat = _mm_w128(dout, W["B"][0], MIX_OUT_BLK, tb=True, out=BF16, name="ev_out_dx")
    GB["B"][0] = _mm_w128_dw(cat, dout, MIX_OUT_BLK, GB["B"][0], "ev_out_dw")
    doa, doaT = _rows_and_cols(dcat[:, :512].reshape(S, A_HEADS, A_HEAD_DIM))
    dqaT, dka, dva, dsink = _attn_bwd(qa, qaT, ka, kaT, va, oaT, doa, doaT, lse_a, tile=SWA_TILE, hb=2, window=WINDOW, sink=prm, real=A_HEAD_DIM,
                                       name="swa_bwd")
    G["ev_sinks"] = dsink[:, 0, 0]
    dqa = _from_T(dqaT) * A_HEAD_DIM ** -0.5
    dka = dka.reshape(A_KV_HEADS, A_GROUP, S, A_HEAD_DIM).sum(axis=1)
    dva = dva.reshape(A_KV_HEADS, A_GROUP, S, A_HEAD_DIM).sum(axis=1)
    dob, dobT = _rows_and_cols(dcat[:, 512:].reshape(S, B_HEADS, B_V))
    dqbT, dkb, dvb = _attn_bwd(qb, qbT, kb, kbT, vb, obT, dob, dobT, lse_b, tile=min(ATTN_TILE, S), hb=1, name="mla_bwd")
    half = B_ROPE // 2
    dqb = jnp.transpose(dqbT, (2, 0, 1)) * (B_NOPE + B_ROPE) ** -0.5
    dkb = jnp.transpose(dkb, (1, 0, 2))
    cos, sin = _rope_tables(S)
    cos8, sin8 = jnp.tile(cos, (1, B_HEADS)), jnp.tile(sin, (1, B_HEADS))
    dq1, dq2 = _rope(dqb[:, :, B_NOPE:B_NOPE + half].reshape(S, -1), dqb[:, :, B_NOPE + half:].reshape(S, -1), cos8, -sin8, "ev_rope_q_bwd")
    dq_all = jnp.concatenate([dqb[:, :, :B_NOPE].reshape(S, -1), dq1, dq2], axis=1).astype(BF16)
    dkr = dkb[:, :, B_NOPE:].sum(axis=1)
    dk1, dk2 = _rope(dkr[:, :half], dkr[:, half:], cos, -sin, "ev_rope_k_bwd")
    dkv_all = jnp.concatenate([dkb[:, :, :B_NOPE].reshape(S, -1), _unheads(dvb)], axis=1).astype(BF16)
    G["ev_w_uq"] = _mm(cqn, dq_all, ta=True, name="ev_uq_dw")
    dcqn = _mm(dq_all, W["ev_w_uq"], tb=True, name="ev_uq_dx")
    dc_q, G["ev_cq_norm"] = _rms_bwd(dcqn, proj[:, 768:1024], W["ev_cq_norm"], None, "ev_cq_norm_bwd")
    G["ev_w_ukv"] = _mm(ckvn, dkv_all, ta=True, name="ev_ukv_dw")
    dckvn = _mm(dkv_all, W["ev_w_ukv"], tb=True, name="ev_ukv_dx")
    dc_kv, G["ev_ckv_norm"] = _rms_bwd(dckvn, proj[:, 1024:1152], W["ev_ckv_norm"], None, "ev_ckv_norm_bwd")
    dproj = jnp.concatenate([dqa, _unheads(dka), _unheads(dva), dc_q, dc_kv, dk1, dk2,
                             jnp.zeros((S, EVEN_IN_PAD - EVEN_IN), F32)], axis=1).astype(BF16)
    G["ev_w_in"] = _mm(hn, dproj, ta=True, name="ev_in_dw")
    dh, dnorm = _mm(dproj, W["ev_w_in"], tb=True, norm_bwd=(*norm, dout), name="ev_in_dx")
    return dh, dnorm, G


def _odd_fwd(hn, h, W):
    S = hn.shape[0]
    w = C_HEADS * C_HEAD_DIM
    wp = C_HEADS * QK_PAD
    proj = _mm(hn, W["od_w_in"], name="od_in")
    f_logit = proj[:, 2 * wp + w: 2 * wp + w + C_HEADS]
    logf = _logsig_fwd(f_logit, W["od_b_f"], "od_logsig")
    logc = _cumsum(logf, False, "od_cumsum")
    parts = list(_exact3(logc))
    ones = [jnp.ones((S, C_HEADS), F32)] * 3
    pad = [jnp.zeros((S, C_HEADS), F32)] * (QK_PAD - C_HEAD_DIM - 6)
    lead = ((0, 0), (0, 0), (C_HEAD_DIM, 0))
    q3 = proj[:, :wp].reshape(S, C_HEADS, QK_PAD) + jnp.pad(jnp.stack(parts + ones + pad, axis=-1), lead)
    k3 = proj[:, wp:2 * wp].reshape(S, C_HEADS, QK_PAD) + jnp.pad(jnp.stack(ones + [-p for p in parts] + pad, axis=-1), lead)
    q, qT = _rows_and_cols(q3)
    k, kT = _rows_and_cols(k3)
    v3 = proj[:, 2 * wp:2 * wp + w].reshape(S, C_HEADS, C_HEAD_DIM)
    v = jnp.transpose(v3.astype(BF16), (1, 0, 2))
    oT, lse = _attn_fwd(qT, k, _v_with_ones(v3), tile=min(ATTN_TILE_FWD, S), hb=2, name="fox_fwd")
    cat = _from_T(oT)
    out = _mm_w128(cat, W["B"][1], MIX_OUT_BLK, res=h, name="od_out")
    return out, (hn, q, qT, k, kT, v, f_logit, oT, lse, cat)


def _odd_bwd(dout, saved, W, GB, norm):
    hn, q, qT, k, kT, v, f_logit, oT, lse, cat = saved
    S = hn.shape[0]
    G = {}
    dcat = _mm_w128(dout, W["B"][1], MIX_OUT_BLK, tb=True, out=BF16, name="od_out_dx")
    GB["B"][1] = _mm_w128_dw(cat, dout, MIX_OUT_BLK, GB["B"][1], "od_out_dw")
    do, doT = _rows_and_cols(dcat.reshape(S, C_HEADS, C_HEAD_DIM))
    dqT, dk, dv, dqxT, dkx = _attn_bwd(q, qT, k, kT, v, oT, do, doT, lse, tile=min(ATTN_TILE, S), hb=1, real=C_HEAD_DIM, extra=True,
                                       full=True, name="fox_bwd")
    dlogc = jnp.transpose(dqxT[:, 0, :] - dkx[:, :, 3])
    dlogf = _cumsum(dlogc, True, "od_cumsum_bwd")
    df, db = _logsig_bwd(dlogf, f_logit, W["od_b_f"], "od_logsig_bwd")
    G["od_b_f"] = db
    dproj = jnp.concatenate([_from_T(dqT), _unheads(dk), _unheads(dv), df, jnp.zeros((S, ODD_IN_PAD - ODD_IN_AUG), F32)], axis=1).astype(BF16)
    G["od_w_in"] = _mm(hn, dproj, ta=True, name="od_in_dw")
    dh, dnorm = _mm(dproj, W["od_w_in"], tb=True, norm_bwd=(*norm, dout), name="od_in_dx")
    return dh, dnorm, G


class _Rider:
    def __init__(self, steps, tag):
        self.steps, self.tag, self.count, self.result = steps, tag, 0, None
        self.job = next(steps)

    def __call__(self, got=None):
        if got is None:
            return self.job
        try:
            self.job = self.steps.send(list(got))
        except StopIteration as done:
            self.job, self.result = None, done.value
        return None

    def finish(self):
        while self.job is not None:
            self.count += 1
            self(_comm_call(self.job, f"{self.tag}_{self.count}"))
        return self.result


def _gather_plan(W, slots):
    a0, b0, c0, m, a1, b1, c1 = (slots[key] for key in ("a0", "b0", "c0", "m", "a1", "b1", "c1"))
    (m,) = yield _gather_job([m])
    W.update(_misc_weights(m))
    (b0,) = yield _gather_job([b0])
    W["B"] = [b0]
    (c0,) = yield _gather_job([c0], rows=[(D_MODEL, D_MODEL)])
    W["C"] = [c0]
    a0, c1 = yield _gather_job([a0, c1], rows=[(DOWN_ROWS, DOWN_ROWS), (0, D_MODEL)])
    W["A"] = [a0]
    a1, b1 = yield _gather_job([a1, b1])
    (c1,) = yield _gather_job([c1], rows=[(D_MODEL, D_MODEL)])
    W["A"].append(a1)
    W["B"].append(b1)
    W["C"].append(c1)


def _local_step(x, p, target, W, slots):
    h = x
    saved = []
    gather = _Rider(_gather_plan(W, slots), "all_gather_rest")
    for i in range(DEPTH):
        t = f"l{i}"
        ride = gather if i == 0 else None
        h1, s_a = _ffn_fwd(h, W["ffa_norm"][i:i + 1], W, 0, i, f"{t}_ffa", ride)
        nm = _rms_fwd(h1, W["mix_norm"][i:i + 1], f"{t}_mix_norm")
        h2, s_m = _even_fwd(nm, h1, W, ride) if i % 2 == 0 else _odd_fwd(nm, h1, W)
        h3, s_b = _ffn_fwd(h2, W["ffb_norm"][i:i + 1], W, 1, i, f"{t}_ffb", ride)
        npl = _rms_fwd(h3, W["ple_norm"][i:i + 1], f"{t}_ple_norm")
        gpre = _mm_w128(npl, W["B"][i], PLE_GATE_BLK, name=f"{t}_ple_gate")
        pp = _mm(p[i], W["ple_w_proj"][i], name=f"{t}_ple_proj")
        h4 = _ple_fwd(h3, gpre, pp, f"{t}_ple")
        saved.append((s_a, h1, s_m, s_b, h3, npl, gpre, pp))
        h = h4
        if i == 0:
            gather.finish()
    dh, g_final, loss_cols = _final_fwd_bwd(h, W["final_norm"], target, "final")
    G = {"final_norm": g_final}
    GB = {"A": [[None, None] for _ in range(DEPTH)], "C": [[None, None] for _ in range(DEPTH)],
          "B": [lax.empty((N_DEV, B_ROWS, D_MODEL), BF16) for _ in range(DEPTH)]}
    per_layer = {n: [None] * DEPTH for n in ("ffa_norm", "mix_norm", "ffb_norm", "ple_norm", "ple_w_proj")}
    scatter = scatter_mid = None
    for i in reversed(range(DEPTH)):
        t = f"l{i}"
        s_a, h1, s_m, s_b, h3, npl, gpre, pp = saved[i]
        dgpre, dpp = _ple_bwd(dh, gpre, pp, f"{t}_ple_bwd")
        per_layer["ple_w_proj"][i] = _mm(p[i], dpp, ta=True, name=f"{t}_ple_proj_dw")
        GB["B"][i] = _mm_w128_dw(npl, dgpre, PLE_GATE_BLK, GB["B"][i], f"{t}_ple_gate_dw")
        dh, per_layer["ple_norm"][i] = _mm_w128(dgpre, W["B"][i], PLE_GATE_BLK, tb=True, norm_bwd=(h3, W["ple_norm"][i:i + 1], dh),
                                                name=f"{t}_ple_gate_dx")
        dh, per_layer["ffb_norm"][i] = _ffn_bwd(dh, s_b, W["ffb_norm"][i:i + 1], W, GB, 1, i, f"{t}_ffb", scatter)
        dh, per_layer["mix_norm"][i], g_mix = (_even_bwd if i % 2 == 0 else _odd_bwd)(dh, s_m, W, GB, (h1, W["mix_norm"][i:i + 1]))
        G.update(g_mix)
        if i == 0:
            G["ple_w_proj"] = per_layer["ple_w_proj"]
            mid = [GB["A"][0][1], GB["C"][0][1], GB["B"][0], _misc_grads(G).astype(BF16)]
            scatter_mid = _Rider(_reduce_scatter_steps(mid, "mid"), "rs_mid")
        dh, per_layer["ffa_norm"][i] = _ffn_bwd(dh, s_a, W["ffa_norm"][i:i + 1], W, GB, 0, i, f"{t}_ffa", scatter_mid)
        if i == DEPTH - 1:
            later = [GB["A"][i][0], GB["A"][i][1], GB["C"][i][0], GB["C"][i][1], GB["B"][i]]
            scatter = _Rider(_reduce_scatter_steps(later, "later"), "rs_later")
    for n in ("ffa_norm", "mix_norm", "ffb_norm", "ple_norm"):
        G[n] = jnp.concatenate(per_layer[n], axis=0)
    return loss_cols, dh, scatter.finish(), scatter_mid.finish(), [GB["A"][0][0], GB["C"][0][0]], G


def kernel(x, p, ffa_norm, ffa_w_gate_up, ffa_w_down, mix_norm, ffb_norm, ffb_w_gate_up, ffb_w_down, ple_norm, ple_w_gate, ple_w_proj, ev_w_in, ev_sinks, ev_cq_norm, ev_w_uq, ev_ckv_norm, ev_w_ukv, ev_w_out, od_w_in, od_b_f, od_w_out, final_norm, loss_target, m_ffa_norm, m_ffa_w_gate_up, m_ffa_w_down, m_mix_norm, m_ffb_norm, m_ffb_w_gate_up, m_ffb_w_down, m_ple_norm, m_ple_w_gate, m_ple_w_proj, m_ev_w_in, m_ev_sinks, m_ev_cq_norm, m_ev_w_uq, m_ev_ckv_norm, m_ev_w_ukv, m_ev_w_out, m_od_w_in, m_od_b_f, m_od_w_out, m_final_norm, v_ffa_norm, v_ffa_w_gate_up, v_ffa_w_down, v_mix_norm, v_ffb_norm, v_ffb_w_gate_up, v_ffb_w_down, v_ple_norm, v_ple_w_gate, v_ple_w_proj, v_ev_w_in, v_ev_sinks, v_ev_cq_norm, v_ev_w_uq, v_ev_ckv_norm, v_ev_w_ukv, v_ev_w_out, v_od_w_in, v_od_b_f, v_od_w_out, v_final_norm):
    given = dict(locals())
    w_in = {n: given[n] for n in WEIGHTS}

    layers, misc = _local_groups(w_in, BF16)
    (a0, b0, c0), (a1, b1, c1) = [[_in_slot(g) for g in layer] for layer in layers]
    a0, c0 = _comm_call(_gather_job([a0, c0], rows=[(0, DOWN_ROWS), (0, D_MODEL)]), "all_gather_first")
    W = {n: w_in[n] for n in SMALL}
    W["final_norm"] = final_norm.reshape(1, -1)
    W.update(A=[a0], C=[c0])
    slots = dict(a0=a0, b0=b0, c0=c0, m=_in_slot(misc), a1=a1, b1=b1, c1=c1)

    loss_cols, dx, r_later, r_mid, last, G = _local_step(x[0], p[:, 0], loss_target[0], W, slots)

    a1f, a1b, c1f, c1b, b1 = r_later
    a0b, c0b, b0, r_misc = r_mid
    a0f, c0f = _reduce_scatter(last, "last")
    grads = _ungroup_local([[a0f, a0b], [a1f, a1b]], [b0, b1], [[c0f, c0b], [c1f, c1b]], r_misc)
    layout = [(n, int(np.prod(w_in[n].shape))) for n in SMALL]
    vec = jnp.concatenate([G[n].astype(F32).reshape(-1) for n, _ in layout] + [jnp.sum(loss_cols).reshape(1)])
    vec = jnp.pad(vec, (0, N_DEV * SMALL_COLS - vec.shape[0])).reshape(N_DEV, SMALL_COLS)
    vec = _all_reduce_small(vec).reshape(-1)
    off = 0
    for n, size in layout:
        grads[n] = vec[off: off + size].reshape(w_in[n].shape)
        off += size
    loss = vec[off]

    delta, new_m, new_v = {}, {}, {}
    for n in WEIGHTS:
        shp = w_in[n].shape
        as2d = (lambda a: a.reshape(1, -1)) if len(shp) == 1 else (lambda a: a)
        d, nm, nv = _adamw(as2d(w_in[n]), as2d(grads[n]), as2d(given["m_" + n]), as2d(given["v_" + n]), f"adamw_{n}")
        delta[n], new_m[n], new_v[n] = d.reshape(shp), nm.reshape(shp), nv.reshape(shp)
    return (loss, dx[None], *[grads[n] for n in WEIGHTS], *[delta[n] for n in WEIGHTS],
            *[new_m[n] for n in WEIGHTS], *[new_v[n] for n in WEIGHTS])
```

```python
import functools

import numpy as np
import jax
import jax.numpy as jnp
from jax import lax
from jax.experimental import pallas as pl
from jax.experimental.pallas import tpu as pltpu

F32 = jnp.float32
BF16 = jnp.bfloat16
MESH = pl.DeviceIdType.MESH

D_MODEL = 1024
D_FF = 2816
RMS_EPS = 1e-6
PLE_DIM = 256
A_HEADS, A_KV_HEADS, A_HEAD_DIM, WINDOW = 8, 2, 64, 128
A_GROUP = A_HEADS // A_KV_HEADS
B_HEADS, B_Q_LORA, B_KV_LORA, B_NOPE, B_ROPE, B_V = 8, 256, 128, 64, 32, 64
ROPE_THETA = 10000.0
C_HEADS, C_HEAD_DIM = 16, 64
EVEN_IN = 1184
EVEN_IN_PAD = 1280
ODD_IN = 3088
ODD_IN_AUG = 2 * 16 * 80 + 1024 + 16
ODD_IN_PAD = 3840
DEPTH = 2
ADAM_LR, ADAM_B1, ADAM_B2, ADAM_EPS, ADAM_WD, ADAM_STEP = 0.001, 0.9, 0.999, 1e-08, 0.01, 10

N_DEV = 8
LANES = 128
SUBLANES = 8
EW_TILE_BYTES = 3 << 20
MM_VMEM_BYTES = 26 << 20
NEG = -1e30
ATTN_TILE = 1024
ATTN_TILE_FWD = 1024
SWA_TILE = 256
QK_PAD = 80

FF_BLK = D_FF // 4
DOWN_ROWS = D_FF // N_DEV
A_ROWS, B_ROWS, C_ROWS, G3_ROWS, G3_COLS = 2 * DOWN_ROWS, 256, 2 * D_MODEL, 1024, 768
PLE_GATE_BLK, MIX_OUT_BLK = 0, 1
OD_C, EV_C, STRIP_C = 386, 148, 128
STRIP0 = OD_C + EV_C

SMALL = ["ffa_norm", "mix_norm", "ffb_norm", "ple_norm", "ev_sinks", "ev_cq_norm", "ev_ckv_norm", "od_b_f", "final_norm"]
WEIGHTS = ["ffa_norm", "ffa_w_gate_up", "ffa_w_down", "mix_norm", "ffb_norm", "ffb_w_gate_up", "ffb_w_down", "ple_norm",
           "ple_w_gate", "ple_w_proj", "ev_w_in", "ev_sinks", "ev_cq_norm", "ev_w_uq", "ev_ckv_norm", "ev_w_ukv", "ev_w_out",
           "od_w_in", "od_b_f", "od_w_out", "final_norm"]
SMALL_COLS = 1280


def _divisor(n, cap, mult):
    if n <= cap:
        return n
    for t in range(cap - cap % mult, 0, -mult):
        if n % t == 0:
            return t
    raise ValueError(f"no tile for {n} under {cap} in steps of {mult}")


def _lanes(c):
    return -(-c // LANES) * LANES


def _ew(fn, rows, vecs, outs, reds=(), *, name):
    R = rows[0].shape[0]
    per_row = sum(_lanes(a.shape[1]) * a.dtype.itemsize for a in rows) + sum(_lanes(c) * jnp.dtype(d).itemsize for c, d in outs)
    tm = _divisor(R, max(16, EW_TILE_BYTES // per_row // 16 * 16), 16) if R % 16 == 0 else R
    n_r, n_v, n_o = len(rows), len(vecs), len(outs)

    def body(*refs):
        ins = [r[...] for r in refs[: n_r + n_v]]
        res = fn(*ins)
        if not isinstance(res, (tuple, list)):
            res = (res,)
        o_refs = refs[n_r + n_v: n_r + n_v + n_o]
        r_refs = refs[n_r + n_v + n_o:]
        for ref, val in zip(o_refs, res[:n_o]):
            ref[...] = val.astype(ref.dtype)
        if r_refs:
            @pl.when(pl.program_id(0) == 0)
            def _():
                for ref in r_refs:
                    ref[...] = jnp.zeros_like(ref)
            for ref, val in zip(r_refs, res[n_o:]):
                ref[...] += val

    in_specs = [pl.BlockSpec((tm, a.shape[1]), lambda i: (i, 0)) for a in rows]
    in_specs += [pl.BlockSpec((1, a.shape[1]), lambda i: (0, 0)) for a in vecs]
    out_specs = [pl.BlockSpec((tm, c), lambda i: (i, 0)) for c, _ in outs]
    out_specs += [pl.BlockSpec((1, c), lambda i: (0, 0)) for c in reds]
    out_shape = [jax.ShapeDtypeStruct((R, c), d) for c, d in outs] + [jax.ShapeDtypeStruct((1, c), F32) for c in reds]
    res = pl.pallas_call(body, name=name, grid=(R // tm,), in_specs=in_specs, out_specs=out_specs, out_shape=out_shape)(*rows, *vecs)
    return res[0] if len(res) == 1 else res


def _rms_fwd(x, w, name):
    def fn(x, w):
        y = x * lax.rsqrt(jnp.mean(x * x, axis=-1, keepdims=True) + RMS_EPS)
        return y * w
    return _ew(fn, [x], [w], [(x.shape[1], BF16)], name=name)


def _rms_bwd(dn, x, w, dres, name):
    def fn(dn, x, *rest):
        w = rest[-1]
        r = lax.rsqrt(jnp.mean(x * x, axis=-1, keepdims=True) + RMS_EPS)
        xh = x * r
        gw = dn * w
        dx = r * (gw - xh * jnp.mean(gw * xh, axis=-1, keepdims=True))
        if len(rest) == 2:
            dx = dx + rest[0]
        return dx, jnp.sum(dn * xh, axis=0, keepdims=True)
    rows = [dn, x] + ([dres] if dres is not None else [])
    return _ew(fn, rows, [w], [(x.shape[1], F32)], [x.shape[1]], name=name)


def _ple_fwd(h, gpre, pp, name):
    return _ew(lambda h, g, q: h + jax.nn.sigmoid(g) * q, [h, gpre, pp], [], [(h.shape[1], F32)], name=name)


def _ple_bwd(dh, gpre, pp, name):
    def fn(dh, g, q):
        sg = jax.nn.sigmoid(g)
        return dh * q * (sg * (1.0 - sg)), dh * sg
    return _ew(fn, [dh, gpre, pp], [], [(dh.shape[1], BF16), (dh.shape[1], BF16)], name=name)


def _rope(x1, x2, cos, sin, name):
    c = x1.shape[1]
    return _ew(lambda a, b, co, si: (a * co - b * si, a * si + b * co), [x1, x2, cos, sin], [], [(c, F32), (c, F32)], name=name)


def _logsig_fwd(f, b, name):
    def fn(f, b):
        z = f + b
        return jnp.minimum(z, 0.0) - jnp.log(1.0 + jnp.exp(-jnp.abs(z)))
    return _ew(fn, [f], [b], [(f.shape[1], F32)], name=name)


def _logsig_bwd(dlogf, f, b, name):
    def fn(d, f, b):
        df = d * jax.nn.sigmoid(-(f + b))
        return df, jnp.sum(df, axis=0, keepdims=True)
    return _ew(fn, [dlogf, f], [b], [(f.shape[1], F32)], [f.shape[1]], name=name)


def _final_fwd_bwd(h, w, target, name):
    d = h.shape[1]

    def fn(h, t, w):
        r = lax.rsqrt(jnp.mean(h * h, axis=-1, keepdims=True) + RMS_EPS)
        xh = h * r
        y = xh * w
        err = y - t
        dy = err * (1.0 / d)
        gw = dy * w
        dx = r * (gw - xh * jnp.mean(gw * xh, axis=-1, keepdims=True))
        return dx, jnp.sum(dy * xh, axis=0, keepdims=True), jnp.sum(err * err, axis=0, keepdims=True) * (0.5 / d)
    return _ew(fn, [h, target], [w], [(d, F32)], [d, d], name=name)


def _adamw(w, g, m, v, name):
    shape = w.shape
    c = shape[-1]
    w2, g2, m2, v2 = (a.reshape(-1, c) for a in (w, g, m, v))

    def fn(w, g, m, v):
        m = ADAM_B1 * m + (1.0 - ADAM_B1) * g
        v = ADAM_B2 * v + (1.0 - ADAM_B2) * jnp.square(g)
        m_hat = m / (1.0 - ADAM_B1 ** ADAM_STEP)
        v_hat = v / (1.0 - ADAM_B2 ** ADAM_STEP)
        delta = -ADAM_LR * (m_hat / (jnp.sqrt(v_hat) + ADAM_EPS) + ADAM_WD * w)
        return delta, m, v
    d, nm, nv = _ew(fn, [w2, g2, m2, v2], [], [(c, F32)] * 3, name=name)
    return d.reshape(shape), nm.reshape(shape), nv.reshape(shape)


def _split3(v):
    hi = v.astype(BF16)
    r1 = v - hi.astype(F32)
    mid = r1.astype(BF16)
    lo = (r1 - mid.astype(F32)).astype(BF16)
    return hi, mid, lo


def _cumsum(x, reverse, name):
    S, C = x.shape
    tm = _divisor(S, 512, 16)
    nt = S // tm

    def body(x_ref, o_ref, carry):
        @pl.when(pl.program_id(0) == 0)
        def _():
            carry[...] = jnp.zeros_like(carry)
        r = lax.broadcasted_iota(jnp.int32, (tm, tm), 0)
        c = lax.broadcasted_iota(jnp.int32, (tm, tm), 1)
        tri = jnp.where((c >= r) if reverse else (c <= r), 1.0, 0.0).astype(BF16)
        xv = x_ref[...]
        acc = jnp.zeros((tm, C), F32)
        for part in _split3(xv):
            acc = acc + jnp.dot(tri, part, preferred_element_type=F32)
        o_ref[...] = acc + carry[...]
        carry[...] += jnp.sum(xv, axis=0, keepdims=True)

    idx = (lambda i: (nt - 1 - i, 0)) if reverse else (lambda i: (i, 0))
    return pl.pallas_call(
        body, name=name, grid=(nt,), in_specs=[pl.BlockSpec((tm, C), idx)], out_specs=pl.BlockSpec((tm, C), idx),
        out_shape=jax.ShapeDtypeStruct((S, C), F32), scratch_shapes=[pltpu.VMEM((1, C), F32)],
    )(x)


NN = (((1,), (0,)), ((), ()))
NT = (((1,), (1,)), ((), ()))
TN = (((0,), (0,)), ((), ()))

HBM_SPEC = pl.BlockSpec(memory_space=pl.ANY)


def _job_in_body(job, refs, n_in, n_out, n_scr, grid):
    if job is None:
        return refs[n_in:], lambda: None
    ji, jo = len(job["ins"]), len(job["outs"])
    j_in = refs[n_in: n_in + ji]
    pos = n_in + ji
    own = list(refs[pos: pos + n_out])
    pos += n_out
    j_out = refs[pos: pos + jo]
    pos += jo
    own += list(refs[pos: pos + n_scr])
    ss, rs = refs[-2], refs[-1]
    first = functools.reduce(jnp.logical_and, [pl.program_id(d) == 0 for d in range(len(grid))])
    last = functools.reduce(jnp.logical_and, [pl.program_id(d) == n - 1 for d, n in enumerate(grid)])

    @pl.when(first)
    def _():
        job["start"](j_in, j_out, ss, rs)

    def finish():
        @pl.when(last)
        def _():
            job["finish"](j_in, j_out, ss, rs)

    return own, finish


def _job_call(job, body, *, name, grid, in_specs, out_specs, out_shape, args, scratch_shapes, aliases, dimension_semantics):
    in_specs, out_specs, out_shape, args, scratch_shapes = list(in_specs), list(out_specs), list(out_shape), list(args), list(scratch_shapes)
    aliases = dict(aliases)
    if job is not None:
        for i, o in job["aliases"].items():
            aliases[len(args) + i] = len(out_shape) + o
        in_specs += [HBM_SPEC] * len(job["ins"])
        args += list(job["ins"])
        out_specs += [HBM_SPEC] * len(job["outs"])
        out_shape += list(job["outs"])
        scratch_shapes += [pltpu.SemaphoreType.DMA((job["n_sems"],)), pltpu.SemaphoreType.DMA((job["n_sems"],))]
    return pl.pallas_call(
        body, name=name, grid=grid, in_specs=in_specs, out_specs=out_specs, out_shape=out_shape,
        scratch_shapes=scratch_shapes, input_output_aliases=aliases,
        compiler_params=pltpu.CompilerParams(dimension_semantics=dimension_semantics),
    )(*args)


def _comm_call(job, name):
    def body(*refs):
        ji, jo = len(job["ins"]), len(job["outs"])
        job["start"](refs[:ji], refs[ji: ji + jo], refs[-2], refs[-1])
        job["finish"](refs[:ji], refs[ji: ji + jo], refs[-2], refs[-1])

    return pl.pallas_call(
        body, name=name, in_specs=[HBM_SPEC] * len(job["ins"]), out_specs=[HBM_SPEC] * len(job["outs"]), out_shape=list(job["outs"]),
        input_output_aliases=dict(job["aliases"]),
        scratch_shapes=[pltpu.SemaphoreType.DMA((job["n_sems"],)), pltpu.SemaphoreType.DMA((job["n_sems"],))],
    )(*job["ins"])


def _mm_call(name, grid, k_axis, a, a_spec, a2d, b, b_spec, b2d, dims, out_sds, out_spec, o2d, *,
             alpha=1.0, res=None, res_spec=None, into=None, job=None, norm_bwd=None):
    nk = grid[k_axis]
    n_in = 2 + (res is not None) + (into is not None) + (3 if norm_bwd is not None else 0)
    n_out = 2 if norm_bwd is not None else 1

    def body(*refs):
        a_ref, b_ref = refs[0], refs[1]
        res_ref = refs[2] if res is not None else None
        own, finish_job = _job_in_body(job, refs, n_in, n_out, 1, grid)
        o_ref, acc_ref = own[0], own[-1]
        k = pl.program_id(k_axis)

        @pl.when(k == 0)
        def _():
            acc_ref[...] = jnp.zeros_like(acc_ref)

        if norm_bwd is not None:
            x_ref, w_ref, dres_ref = refs[n_in - 3: n_in]
            dw_ref = own[1]

            @pl.when(functools.reduce(jnp.logical_and, [pl.program_id(d) == 0 for d in range(len(grid))]))
            def _():
                dw_ref[...] = jnp.zeros_like(dw_ref)

        av = a_ref[...].reshape(a2d).astype(BF16)
        bv = b_ref[...].reshape(b2d).astype(BF16)
        acc_ref[...] += lax.dot_general(av, bv, dims, preferred_element_type=F32)

        @pl.when(k == nk - 1)
        def _():
            r = acc_ref[...]
            if alpha != 1.0:
                r = r * alpha
            if res_ref is not None:
                r = res_ref[...].reshape(o2d) + r
            if norm_bwd is not None:
                x = x_ref[...]
                rs = lax.rsqrt(jnp.mean(x * x, axis=-1, keepdims=True) + RMS_EPS)
                xh = x * rs
                gw = r * w_ref[...]
                dw_ref[...] += jnp.sum(r * xh, axis=0, keepdims=True)
                r = dres_ref[...] + rs * (gw - xh * jnp.mean(gw * xh, axis=-1, keepdims=True))
            o_ref[...] = r.reshape(o_ref.shape).astype(o_ref.dtype)

        finish_job()

    in_specs, args = [a_spec, b_spec], [a, b]
    if res is not None:
        in_specs.append(res_spec)
        args.append(res)
    aliases = {}
    if into is not None:
        aliases = {len(args): 0}
        in_specs.append(pl.BlockSpec(memory_space=pl.ANY))
        args.append(into)
        out_sds = jax.ShapeDtypeStruct(into.shape, into.dtype)
    out_specs, out_shape = [out_spec], [out_sds]
    if norm_bwd is not None:
        vec = pl.BlockSpec((1, o2d[1]), lambda *_: (0, 0))
        in_specs += [out_spec, vec, out_spec]
        args += list(norm_bwd)
        out_specs.append(vec)
        out_shape.append(jax.ShapeDtypeStruct((1, o2d[1]), F32))
    serial = job is not None or norm_bwd is not None
    sem = tuple("arbitrary" if d == k_axis or serial else "parallel" for d in range(len(grid)))
    res_all = _job_call(
        job, body, name=name, grid=grid, in_specs=in_specs, out_specs=out_specs, out_shape=out_shape, args=args,
        scratch_shapes=[pltpu.VMEM(o2d, F32)], aliases=aliases, dimension_semantics=sem)
    own = res_all[0] if n_out == 1 else tuple(res_all[:n_out])
    return own if job is None else (own, res_all[n_out:])


def _mm(a, b, *, ta=False, tb=False, out=F32, res=None, alpha=1.0, norm_bwd=None, name):
    K, M = a.shape if ta else a.shape[::-1]
    N = b.shape[0] if tb else b.shape[1]
    assert (b.shape[1] if tb else b.shape[0]) == K, (a.shape, b.shape, ta, tb)
    tk = _divisor(K, 1024, LANES)
    tn = _divisor(N, 1408, LANES)
    assert norm_bwd is None or tn == N
    for cap in (1024, 512, 256, 128):
        tm = _divisor(M, cap, LANES if ta else 16)
        est = 2 * (tm * tk * a.dtype.itemsize + tk * tn * b.dtype.itemsize + tm * tn * jnp.dtype(out).itemsize)
        est += tm * tn * 4 + (2 * tm * tn * 4 if res is not None else 0) + (4 * tm * tn * 4 if norm_bwd is not None else 0)
        if est <= MM_VMEM_BYTES:
            break
    a_spec = pl.BlockSpec((tk, tm), lambda i, j, k: (k, i)) if ta else pl.BlockSpec((tm, tk), lambda i, j, k: (i, k))
    b_spec = pl.BlockSpec((tn, tk), lambda i, j, k: (j, k)) if tb else pl.BlockSpec((tk, tn), lambda i, j, k: (k, j))
    o_spec = pl.BlockSpec((tm, tn), lambda i, j, k: (i, j))
    dims = (((0 if ta else 1,), (1 if tb else 0,)), ((), ()))
    return _mm_call(name, (M // tm, N // tn, K // tk), 2, a, a_spec, (tk, tm) if ta else (tm, tk), b, b_spec,
                    (tn, tk) if tb else (tk, tn), dims, jax.ShapeDtypeStruct((M, N), out), o_spec, (tm, tn),
                    alpha=alpha, res=res, res_spec=o_spec, norm_bwd=norm_bwd)


def _w128_spec(blk):
    return pl.BlockSpec((N_DEV, 128, D_MODEL), lambda *_: (0, blk, 0))


def _mm_w128(a, G1, blk, *, tb=False, res=None, out=F32, norm_bwd=None, name):
    S = a.shape[0]
    tm = _divisor(S, 512, 16)
    row = pl.BlockSpec((tm, D_MODEL), lambda i, k: (i, 0))
    return _mm_call(name, (S // tm, 1), 1, a, row, (tm, D_MODEL), G1, _w128_spec(blk), (D_MODEL, D_MODEL), NT if tb else NN,
                    jax.ShapeDtypeStruct((S, D_MODEL), out), row, (tm, D_MODEL), res=res, res_spec=row, norm_bwd=norm_bwd)


def _mm_w128_dw(a, b, blk, into, name):
    S = a.shape[0]
    tk = _divisor(S, 1024, 16)
    row = pl.BlockSpec((tk, D_MODEL), lambda i, k: (k, 0))
    return _mm_call(name, (1, S // tk), 1, a, row, (tk, D_MODEL), b, row, (tk, D_MODEL), TN, None, _w128_spec(blk),
                    (D_MODEL, D_MODEL), into=into)


def _ffn_gate_up(h, norm_w, G2v, rb, name, job=None):
    S = h.shape[0]
    tm = _divisor(S, 1024, 16)
    grid = (S // tm, 4)

    def body(*refs):
        h_ref, nw_ref, w_ref = refs[:3]
        (n_ref, gu_ref, act_ref, n_scr), finish_job = _job_in_body(job, refs, 3, 3, 1, grid)

        @pl.when(pl.program_id(1) == 0)
        def _():
            x = h_ref[...]
            y = x * lax.rsqrt(jnp.mean(x * x, axis=-1, keepdims=True) + RMS_EPS)
            n_scr[...] = (y * nw_ref[...]).astype(BF16)
            n_ref[...] = n_scr[...]

        nv = n_scr[...]
        g = jnp.dot(nv, w_ref[0, 0], preferred_element_type=F32)
        u = jnp.dot(nv, w_ref[1, 0], preferred_element_type=F32)
        sg = jax.nn.sigmoid(g)
        silu = g * sg
        gu_ref[0, 0] = (u * (sg * (1.0 + g * (1.0 - sg)))).astype(BF16)
        gu_ref[1, 0] = silu.astype(BF16)
        act_ref[0] = (silu * u).astype(BF16)
        finish_job()

    row = pl.BlockSpec((tm, D_MODEL), lambda i, j: (i, 0))
    return _job_call(
        job, body, name=name, grid=grid,
        in_specs=[row, pl.BlockSpec((1, D_MODEL), lambda i, j: (0, 0)), pl.BlockSpec((2, 1, D_MODEL, FF_BLK), lambda i, j: (0, j, rb, 0))],
        out_specs=[row, pl.BlockSpec((2, 1, tm, FF_BLK), lambda i, j: (0, j, i, 0)), pl.BlockSpec((1, tm, FF_BLK), lambda i, j: (j, i, 0))],
        out_shape=[jax.ShapeDtypeStruct((S, D_MODEL), BF16), jax.ShapeDtypeStruct((2, 4, S, FF_BLK), BF16), jax.ShapeDtypeStruct((4, S, FF_BLK), BF16)],
        args=[h, norm_w, G2v], scratch_shapes=[pltpu.VMEM((tm, D_MODEL), BF16)], aliases={},
        dimension_semantics=("arbitrary" if job is not None else "parallel", "arbitrary"))


def _ffn_down(act, G1, ob, h, name, job=None):
    S = h.shape[0]
    tm = _divisor(S, 1024, 16)
    row = pl.BlockSpec((tm, D_MODEL), lambda i, k: (i, 0))
    return _mm_call(name, (S // tm, 4), 1, act, pl.BlockSpec((1, tm, FF_BLK), lambda i, k: (k, i, 0)), (tm, FF_BLK),
                    G1, pl.BlockSpec((2, DOWN_ROWS, D_MODEL), lambda i, k: (k, ob, 0)), (FF_BLK, D_MODEL), NN,
                    jax.ShapeDtypeStruct((S, D_MODEL), F32), row, (tm, D_MODEL), alpha=0.5, res=h, res_spec=row, job=job)


def _ffn_down_dx(dh, G1, ob, gu, name):
    S = dh.shape[0]
    tm = _divisor(S, 1024, 16)

    def body(dh_ref, w_ref, gu_ref, o_ref):
        w = w_ref[...].reshape(FF_BLK, D_MODEL)
        dact = lax.dot_general(dh_ref[...].astype(BF16), w, NT, preferred_element_type=F32) * 0.5
        o_ref[0, 0] = (dact * gu_ref[0, 0].astype(F32)).astype(BF16)
        o_ref[1, 0] = (dact * gu_ref[1, 0].astype(F32)).astype(BF16)

    blk = pl.BlockSpec((2, 1, tm, FF_BLK), lambda i, j: (0, j, i, 0))
    return pl.pallas_call(
        body, name=name, grid=(S // tm, 4),
        in_specs=[pl.BlockSpec((tm, D_MODEL), lambda i, j: (i, 0)), pl.BlockSpec((2, DOWN_ROWS, D_MODEL), lambda i, j: (j, ob, 0)), blk],
        out_specs=blk, out_shape=jax.ShapeDtypeStruct((2, 4, S, FF_BLK), BF16),
    )(dh, G1, gu)


def _ffn_down_dw(act, dh, name, job=None):
    S = dh.shape[0]
    tk = _divisor(S, 1024, 16)
    return _mm_call(name, (4, S // tk), 1, act, pl.BlockSpec((1, tk, FF_BLK), lambda j, k: (j, k, 0)), (tk, FF_BLK),
                    dh, pl.BlockSpec((tk, D_MODEL), lambda j, k: (k, 0)), (tk, D_MODEL), TN,
                    jax.ShapeDtypeStruct((N_DEV, DOWN_ROWS, D_MODEL), BF16),
                    pl.BlockSpec((2, DOWN_ROWS, D_MODEL), lambda j, k: (j, 0, 0)), (FF_BLK, D_MODEL), alpha=0.5, job=job)


def _ffn_gate_up_dw(n, dgu8, name, job=None):
    S = n.shape[0]
    tk = _divisor(S, 1024, 16)
    return _mm_call(name, (N_DEV, S // tk), 1, n, pl.BlockSpec((tk, D_MODEL), lambda b, k: (k, 0)), (tk, D_MODEL),
                    dgu8, pl.BlockSpec((1, tk, FF_BLK), lambda b, k: (b, k, 0)), (tk, FF_BLK), TN,
                    jax.ShapeDtypeStruct((N_DEV, D_MODEL, FF_BLK), BF16),
                    pl.BlockSpec((1, D_MODEL, FF_BLK), lambda b, k: (b, 0, 0)), (D_MODEL, FF_BLK), job=job)


def _ffn_gate_up_dx(dgu8, G2, rb, h, norm_w, dres, name, job=None):
    S = h.shape[0]
    tm = _divisor(S, 1024, 16)
    row = pl.BlockSpec((tm, D_MODEL), lambda i, k: (i, 0))
    return _mm_call(name, (S // tm, N_DEV), 1, dgu8, pl.BlockSpec((1, tm, FF_BLK), lambda i, k: (k, i, 0)), (tm, FF_BLK),
                    G2, pl.BlockSpec((1, D_MODEL, FF_BLK), lambda i, k: (k, rb, 0)), (D_MODEL, FF_BLK), NT,
                    jax.ShapeDtypeStruct((S, D_MODEL), F32), row, (tm, D_MODEL), norm_bwd=(h, norm_w, dres), job=job)


def _unheads(x):
    h, S, d = x.shape
    return jnp.transpose(x, (1, 0, 2)).reshape(S, h * d)


def _exact3(v):
    rnd = lambda a: lax.reduce_precision(a, exponent_bits=8, mantissa_bits=7)
    hi = rnd(v)
    mid = rnd(v - hi)
    return hi, mid, rnd(v - hi - mid)


def _causal_mask(st, i, j, tq, tk, window):
    dist = (i * tq + lax.broadcasted_iota(jnp.int32, (tk, tq), 1)) - (j * tk + lax.broadcasted_iota(jnp.int32, (tk, tq), 0))
    mask = dist >= 0
    if window is not None:
        mask = mask & (dist < window)
    return jnp.where(mask, st, NEG)


def _attn_fwd(qT, k, vT1, *, tile, hb, window=None, sink=None, name, job=None):
    H, dqk, S = qT.shape
    G = H // k.shape[0]
    dvp = vT1.shape[1]
    dv = dvp - 16
    tq = tk = tile
    assert H % hb == 0 and (G == 1 or G % hb == 0)
    kvb = hb if G == 1 else 1
    grid = (H // hb, S // tq)
    n_in = 3 + (sink is not None)

    def body(*refs):
        q_ref, k_ref, v_ref = refs[:3]
        (o_ref, lse_ref), finish_job = _job_in_body(job, refs, n_in, 2, 0, grid)
        i = pl.program_id(1)
        carry = []
        for a in range(hb):
            if sink is not None:
                carry.append(jnp.zeros((1, tq), F32) + refs[3][a, :, 0:1])
                carry.append(jnp.where(lax.broadcasted_iota(jnp.int32, (dvp, tq), 0) == dv, 1.0, 0.0))
            else:
                carry.append(jnp.full((1, tq), NEG, F32))
                carry.append(jnp.zeros((dvp, tq), F32))

        def step(j, carry, masked):
            off = pl.multiple_of(j * tk, tk)
            out = []
            for a in range(hb):
                m, acc = carry[2 * a], carry[2 * a + 1]
                kv = a if kvb > 1 else 0
                st = jnp.dot(k_ref[kv, pl.ds(off, tk), :], q_ref[a], preferred_element_type=F32)
                if masked:
                    st = _causal_mask(st, i, j, tq, tk, window)
                m_new = jnp.maximum(m, jnp.max(st, axis=0, keepdims=True))
                pt = jnp.exp(st - m_new).astype(BF16)
                acc = jnp.exp(m - m_new) * acc + jnp.dot(v_ref[kv, :, pl.ds(off, tk)], pt, preferred_element_type=F32)
                out += [m_new, acc]
            return tuple(out)

        carry = tuple(carry)
        if window is None:
            carry = lax.fori_loop(0, i, functools.partial(step, masked=False), carry)
            carry = step(i, carry, True)
        else:
            lo = jnp.maximum((i * tq - (window - 1)) // tk, 0)
            carry = lax.fori_loop(lo, i + 1, functools.partial(step, masked=True), carry)
        for a in range(hb):
            m, acc = carry[2 * a], carry[2 * a + 1]
            l = acc[dv:dv + 1, :]
            o_ref[a] = acc[:dv, :] / l
            lse_ref[a] = m + jnp.log(l)
        finish_job()

    kv_idx = (lambda b: b) if G == 1 else (lambda b: (b * hb) // G)
    in_specs = [
        pl.BlockSpec((hb, dqk, tq), lambda b, i: (b, 0, i)),
        pl.BlockSpec((kvb, S, dqk), lambda b, i: (kv_idx(b), 0, 0)),
        pl.BlockSpec((kvb, dvp, S), lambda b, i: (kv_idx(b), 0, 0)),
    ]
    args = [qT, k, vT1]
    if sink is not None:
        in_specs += [pl.BlockSpec((hb, 1, LANES), lambda b, i: (b, 0, 0))]
        args += [sink]
    return _job_call(
        job, body, name=name, grid=grid, in_specs=in_specs,
        out_specs=[pl.BlockSpec((hb, dv, tq), lambda b, i: (b, 0, i)), pl.BlockSpec((hb, 1, tq), lambda b, i: (b, 0, i))],
        out_shape=[jax.ShapeDtypeStruct((H, dv, S), F32), jax.ShapeDtypeStruct((H, 1, S), F32)],
        args=args, scratch_shapes=[], aliases={}, dimension_semantics=("arbitrary", "arbitrary") if job is not None else ("parallel", "parallel"))


def _attn_bwd(q, qT, k, kT, v, oT, do, doT, lse, *, tile, hb, window=None, sink=None, real=None, extra=False, full=False, name):
    H, S, dqk = q.shape
    G = H // k.shape[0]
    dv = v.shape[2]
    tq = tk = tile
    nq = S // tq
    has_p = sink is not None
    real = dqk if real is None else real
    main = dqk if full else real
    assert H % hb == 0 and (G == 1 or G % hb == 0) and not (extra and real == dqk)
    kvb = hb if G == 1 else 1

    def body(*refs):
        q_ref, qT_ref, k_ref, kT_ref, v_ref, oT_ref, do_ref, doT_ref, lse_ref = refs[:9]
        p_ref = refs[9] if has_p else None
        pos = 10 if has_p else 9
        dq_ref, dk_ref, dv_ref = refs[pos: pos + 3]
        pos += 3
        ds_ref = refs[pos] if has_p else None
        pos += has_p
        dqx_ref, dkx_ref = (refs[pos], refs[pos + 1]) if extra else (None, None)
        delta = refs[-1]
        j = pl.program_id(1)

        @pl.when(j == 0)
        def _():
            dq_ref[...] = jnp.zeros_like(dq_ref)
            if extra:
                dqx_ref[...] = jnp.zeros_like(dqx_ref)
            for a in range(hb):
                drow = jnp.sum(doT_ref[a].astype(F32) * oT_ref[a], axis=0, keepdims=True)
                delta[a] = drow
                if has_p:
                    w = jnp.exp(p_ref[a, :, 0:1] - lse_ref[a])
                    ds_ref[a] = jnp.zeros((1, LANES), F32) - jnp.sum(w * drow, axis=1, keepdims=True)

        def step(i, carry, masked):
            off = pl.multiple_of(i * tq, tq)
            out = []
            for a in range(hb):
                dk, dvv = carry[2 * a], carry[2 * a + 1]
                kv = a if kvb > 1 else 0
                st = jnp.dot(k_ref[kv], qT_ref[a, :, pl.ds(off, tq)], preferred_element_type=F32)
                if masked:
                    st = _causal_mask(st, i, j, tq, tk, window)
                pt = jnp.exp(st - lse_ref[a, :, pl.ds(off, tq)])
                dvv = dvv + jnp.dot(pt.astype(BF16), do_ref[a, pl.ds(off, tq), :], preferred_element_type=F32)
                dpt = jnp.dot(v_ref[kv], doT_ref[a, :, pl.ds(off, tq)], preferred_element_type=F32)
                dsb = (pt * (dpt - delta[a, :, pl.ds(off, tq)])).astype(BF16)
                dk = dk + jnp.dot(dsb, q_ref[a, pl.ds(off, tq), :], preferred_element_type=F32)
                dqt = jnp.dot(kT_ref[kv], dsb, preferred_element_type=F32)
                dq_ref[a, :, pl.ds(off, tq)] += dqt[:main]
                if extra:
                    dqx_ref[a, :, pl.ds(off, tq)] += dqt[real:]
                out += [dk, dvv]
            return tuple(out)

        carry = (jnp.zeros((tk, dqk), F32), jnp.zeros((tk, dv), F32)) * hb
        if window is None:
            carry = step(j, carry, True)
            carry = lax.fori_loop(j + 1, nq, functools.partial(step, masked=False), carry)
        else:
            hi = jnp.minimum(nq - 1, ((j + 1) * tk + window - 2) // tq)
            carry = lax.fori_loop(j, hi + 1, functools.partial(step, masked=True), carry)
        for a in range(hb):
            dk_ref[a] = carry[2 * a][:, :main]
            if extra:
                dkx_ref[a] = carry[2 * a][:, real:]
            dv_ref[a] = carry[2 * a + 1]

    kv_idx = (lambda b: b) if G == 1 else (lambda b: (b * hb) // G)
    rows = lambda d: pl.BlockSpec((hb, S, d), lambda b, j: (b, 0, 0))
    colsT = lambda d: pl.BlockSpec((hb, d, S), lambda b, j: (b, 0, 0))
    in_specs = [
        rows(dqk), colsT(dqk),
        pl.BlockSpec((kvb, tk, dqk), lambda b, j: (kv_idx(b), j, 0)),
        pl.BlockSpec((kvb, dqk, tk), lambda b, j: (kv_idx(b), 0, j)),
        pl.BlockSpec((kvb, tk, dv), lambda b, j: (kv_idx(b), j, 0)),
        colsT(dv), rows(dv), colsT(dv),
        pl.BlockSpec((hb, 1, S), lambda b, j: (b, 0, 0)),
    ]
    args = [q, qT, k, kT, v, oT, do, doT, lse]
    if has_p:
        in_specs += [pl.BlockSpec((hb, 1, LANES), lambda b, j: (b, 0, 0))]
        args += [sink]
    out_specs = [colsT(main), pl.BlockSpec((hb, tk, main), lambda b, j: (b, j, 0)), pl.BlockSpec((hb, tk, dv), lambda b, j: (b, j, 0))]
    out_shape = [jax.ShapeDtypeStruct((H, main, S), F32), jax.ShapeDtypeStruct((H, S, main), F32), jax.ShapeDtypeStruct((H, S, dv), F32)]
    if has_p:
        out_specs += [pl.BlockSpec((hb, 1, LANES), lambda b, j: (b, 0, 0))]
        out_shape += [jax.ShapeDtypeStruct((H, 1, LANES), F32)]
    if extra:
        out_specs += [colsT(dqk - real), pl.BlockSpec((hb, tk, dqk - real), lambda b, j: (b, j, 0))]
        out_shape += [jax.ShapeDtypeStruct((H, dqk - real, S), F32), jax.ShapeDtypeStruct((H, S, dqk - real), F32)]
    return pl.pallas_call(
        body, name=name, grid=(H // hb, S // tk), in_specs=in_specs, out_specs=out_specs, out_shape=out_shape,
        scratch_shapes=[pltpu.VMEM((hb, 1, S), F32)],
        compiler_params=pltpu.CompilerParams(dimension_semantics=("parallel", "arbitrary")),
    )(*args)


def _rows_and_cols(x3):
    xb = x3.astype(BF16)
    return jnp.transpose(xb, (1, 0, 2)), jnp.transpose(xb, (1, 2, 0))


def _v_with_ones(v3):
    S, h, _ = v3.shape
    vT = jnp.transpose(v3.astype(BF16), (1, 2, 0))
    return jnp.concatenate([vT, jnp.ones((h, 1, S), BF16), jnp.zeros((h, 15, S), BF16)], axis=1)


def _from_T(oT):
    h, d, S = oT.shape
    return jnp.transpose(oT, (2, 0, 1)).reshape(S, h * d)


def _coords():
    return lax.axis_index("x"), lax.axis_index("y"), lax.axis_index("c")


def _peer(axis):
    x, y, c = _coords()
    return {"x": (1 - x, y, c), "y": (x, 1 - y, c), "c": (x, y, 1 - c)}[axis]


def _gather_job(bufs, rows=None):
    n = len(bufs)

    def copies(outs, send_sems, recv_sems):
        x, y, c = _coords()
        me, sibling = (x, y, c), (x, y, 1 - c)
        chips = [(1 - x, y), (x, 1 - y), (1 - x, 1 - y)]

        def copy(t, k, block, to):
            px, py, pc = block
            ref = outs[t].at[4 * px + 2 * py + pc]
            if rows is not None and rows[t] is not None:
                ref = ref.at[pl.ds(rows[t][0], rows[t][1])]
            return pltpu.make_async_remote_copy(ref, ref, send_sems.at[7 * t + k], recv_sems.at[7 * t + k], device_id=to, device_id_type=MESH)

        return copy, me, sibling, chips, c

    def start(ins, outs, send_sems, recv_sems):
        copy, me, sibling, chips, c = copies(outs, send_sems, recv_sems)
        for t in range(n):
            copy(t, 0, me, sibling).start()
            for j, chip in enumerate(chips):
                copy(t, 1 + j, me, (*chip, c)).start()

    def finish(ins, outs, send_sems, recv_sems):
        copy, me, sibling, chips, c = copies(outs, send_sems, recv_sems)
        for j, chip in enumerate(chips):
            for t in range(n):
                copy(t, 1 + j, (*chip, c), me).wait_recv()
                copy(t, 4 + j, (*chip, c), sibling).start()
        for t in range(n):
            copy(t, 0, sibling, me).wait_recv()
            for j, chip in enumerate(chips):
                copy(t, 4 + j, (*chip, 1 - c), me).wait_recv()
        for t in range(n):
            copy(t, 0, me, sibling).wait_send()
            for j, chip in enumerate(chips):
                copy(t, 1 + j, me, (*chip, c)).wait_send()
                copy(t, 4 + j, (*chip, c), sibling).wait_send()

    return dict(ins=list(bufs), outs=[jax.ShapeDtypeStruct(b.shape, b.dtype) for b in bufs], aliases={t: t for t in range(n)},
                n_sems=7 * n, start=start, finish=finish)


def _in_slot(local):
    x, y, c = _coords()
    buf = lax.empty((N_DEV,) + local.shape, local.dtype)
    return lax.dynamic_update_slice(buf, local[None], (4 * x + 2 * y + c, 0, 0))


def _pair_job(vs, axes):
    n = len(vs)
    axes = [axes] * n if isinstance(axes, str) else axes

    def copies(ins, outs, send_sems, recv_sems):
        out = []
        for t in range(n):
            me = lax.axis_index(axes[t])
            src = ins[t].at[1 - me] if len(ins[t].shape) == 3 else ins[t].at[:, 1 - me]
            out.append(pltpu.make_async_remote_copy(src, outs[t], send_sems.at[t], recv_sems.at[t], device_id=_peer(axes[t]), device_id_type=MESH))
        return out

    def start(*refs):
        for cp in copies(*refs):
            cp.start()

    def finish(*refs):
        for cp in copies(*refs):
            cp.wait()

    return dict(ins=list(vs), outs=[jax.ShapeDtypeStruct(v.shape[:-3] + v.shape[-2:], v.dtype) for v in vs], aliases={}, n_sems=n,
                start=start, finish=finish)


def _add_kept(v, got, axis, out, name):
    R, C = v.shape[-2:]
    lead = v.shape[0] if v.ndim == 4 else 1
    tm = _divisor(R, max(16, EW_TILE_BYTES // (_lanes(C) * (v.dtype.itemsize + got.dtype.itemsize + jnp.dtype(out).itemsize)) // 16 * 16), 16)
    me = lax.axis_index(axis).astype(jnp.int32).reshape(1)
    v4 = v.reshape(lead, 2, R, C)
    g3 = got.reshape(lead, R, C)

    def body(me_ref, v_ref, g_ref, o_ref):
        o_ref[...] = (v_ref[0].astype(F32) + g_ref[...].astype(F32)).astype(o_ref.dtype)

    res = pl.pallas_call(
        body, name=name, out_shape=jax.ShapeDtypeStruct((lead, R, C), out),
        grid_spec=pltpu.PrefetchScalarGridSpec(
            num_scalar_prefetch=1, grid=(lead, R // tm),
            in_specs=[pl.BlockSpec((1, 1, tm, C), lambda b, i, me: (b, me[0], i, 0)), pl.BlockSpec((1, tm, C), lambda b, i, me: (b, i, 0))],
            out_specs=pl.BlockSpec((1, tm, C), lambda b, i, me: (b, i, 0))),
    )(me, v4, g3)
    return res


def _cross_job(vs):
    n = len(vs)

    def copies(ins, outs, send_sems, recv_sems):
        x, y, _ = _coords()
        out = []
        for t in range(n):
            h = ins[t].shape[2] // 2
            out.append(pltpu.make_async_remote_copy(ins[t].at[1 - x, :, pl.ds(0, h)], outs[2 * t], send_sems.at[2 * t], recv_sems.at[2 * t],
                                                    device_id=_peer("x"), device_id_type=MESH))
            out.append(pltpu.make_async_remote_copy(ins[t].at[:, 1 - y, pl.ds(h, h)], outs[2 * t + 1], send_sems.at[2 * t + 1], recv_sems.at[2 * t + 1],
                                                    device_id=_peer("y"), device_id_type=MESH))
        return out

    def start(*refs):
        for cp in copies(*refs):
            cp.start()

    def finish(*refs):
        for cp in copies(*refs):
            cp.wait()

    outs = []
    for v in vs:
        outs += [jax.ShapeDtypeStruct((2, v.shape[2] // 2, v.shape[3]), v.dtype)] * 2
    return dict(ins=list(vs), outs=outs, aliases={}, n_sems=2 * n, start=start, finish=finish)


def _add_picked(v, got, axis, out, name):
    _, _, R, C = v.shape
    h = R // 2
    tm = _divisor(h, max(16, EW_TILE_BYTES // (_lanes(C) * (v.dtype.itemsize + got.dtype.itemsize + jnp.dtype(out).itemsize)) // 16 * 16), 16)
    me = lax.axis_index(axis).astype(jnp.int32).reshape(1)
    if axis == "x":
        v_map = lambda b, i, me: (me[0], b, i, 0)
    else:
        v_map = lambda b, i, me: (b, me[0], i + h // tm, 0)

    def body(me_ref, v_ref, g_ref, o_ref):
        o_ref[...] = (v_ref[0].astype(F32) + g_ref[...].astype(F32)).astype(o_ref.dtype)

    return pl.pallas_call(
        body, name=name, out_shape=jax.ShapeDtypeStruct((2, h, C), out),
        grid_spec=pltpu.PrefetchScalarGridSpec(
            num_scalar_prefetch=1, grid=(2, h // tm),
            in_specs=[pl.BlockSpec((1, 1, tm, C), v_map), pl.BlockSpec((1, tm, C), lambda b, i, me: (b, i, 0))],
            out_specs=pl.BlockSpec((1, tm, C), lambda b, i, me: (b, i, 0))),
    )(me, v, got)


def _reduce_scatter_steps(gs, tag):
    n = len(gs)
    vs = [g.reshape(4, 2, *g.shape[1:]) for g in gs]
    got = yield _pair_job(vs, "c")
    vs = [_add_kept(v, r, "c", BF16, f"rs_{tag}_add_c{t}") for t, (v, r) in enumerate(zip(vs, got))]
    vs = [v.reshape(2, 2, v.shape[1], v.shape[2]) for v in vs]
    got = yield _cross_job(vs)
    up = [_add_picked(v, r, "x", BF16, f"rs_{tag}_add_x{t}") for t, (v, r) in enumerate(zip(vs, got[0::2]))]
    lo = [_add_picked(v, r, "y", BF16, f"rs_{tag}_add_y{t}") for t, (v, r) in enumerate(zip(vs, got[1::2]))]
    got = yield _pair_job(up + lo, ["y"] * n + ["x"] * n)
    out = []
    for t in range(n):
        a = _add_kept(up[t], got[t], "y", F32, f"rs_{tag}_add_y2{t}")[0]
        b = _add_kept(lo[t], got[n + t], "x", F32, f"rs_{tag}_add_x2{t}")[0]
        out.append(jnp.concatenate([a, b], axis=0))
    return out


def _reduce_scatter(gs, tag):
    steps = _reduce_scatter_steps(gs, tag)
    job = next(steps)
    for stage in ("c", "xy", "yx"):
        got = _comm_call(job, f"rs_{tag}_{stage}")
        try:
            job = steps.send(got)
        except StopIteration as done:
            return done.value


def _all_reduce_small(v):
    def body(v_ref, o_ref, buf, send_sems, recv_sems):
        x, y, c = _coords()
        me = 4 * x + 2 * y + c
        buf[me] = v_ref[...]
        copies = []
        for k in range(1, N_DEV):
            peer = tuple((1 - a) if (k >> s) & 1 else a for a, s in ((x, 2), (y, 1), (c, 0)))
            cp = pltpu.make_async_remote_copy(v_ref, buf.at[me], send_sems.at[k - 1], recv_sems.at[k - 1], device_id=peer, device_id_type=MESH)
            cp.start()
            copies.append(cp)
        for cp in copies:
            cp.wait()
        acc = buf[0]
        for d in range(1, N_DEV):
            acc = acc + buf[d]
        o_ref[...] = acc

    vm = pl.BlockSpec(memory_space=pltpu.VMEM)
    return pl.pallas_call(
        body, name="all_reduce_small", in_specs=[vm], out_specs=vm, out_shape=jax.ShapeDtypeStruct(v.shape, F32),
        scratch_shapes=[pltpu.VMEM((N_DEV,) + v.shape, F32), pltpu.SemaphoreType.DMA((N_DEV - 1,)), pltpu.SemaphoreType.DMA((N_DEV - 1,))],
    )(v)


def _local_groups(w, dtype):
    mix_out = [w["ev_w_out"][0], w["od_w_out"][0]]
    layers = []
    for l in range(DEPTH):
        a = jnp.concatenate([w["ffa_w_down"][l], w["ffb_w_down"][l]], axis=0).astype(dtype)
        b = jnp.concatenate([w["ple_w_gate"][l], mix_out[l]], axis=0).astype(dtype)
        c = jnp.concatenate([w["ffa_w_gate_up"][l], w["ffb_w_gate_up"][l]], axis=0).astype(dtype)
        layers.append((a, b, c))
    strip = jnp.concatenate([w["ple_w_proj"].reshape(-1, STRIP_C), w["ev_w_ukv"][0], jnp.pad(w["ev_w_uq"][0], ((0, 0), (0, STRIP_C - 96))),
                             jnp.zeros((G3_ROWS - 896, STRIP_C), F32)], axis=0)
    m = jnp.concatenate([w["od_w_in"][0], w["ev_w_in"][0], strip, jnp.zeros((G3_ROWS, G3_COLS - STRIP0 - STRIP_C), F32)], axis=1).astype(dtype)
    return layers, m


def _ungroup_local(a, b, c, r3):
    out = {
        "ffa_w_down": jnp.stack([x[0] for x in a]), "ffb_w_down": jnp.stack([x[1] for x in a]),
        "ple_w_gate": jnp.stack([x[:128] for x in b]), "ev_w_out": b[0][128:][None], "od_w_out": b[1][128:][None],
        "ffa_w_gate_up": jnp.stack([x[0] for x in c]), "ffb_w_gate_up": jnp.stack([x[1] for x in c]),
        "od_w_in": r3[:, :OD_C][None], "ev_w_in": r3[:, OD_C:STRIP0][None],
    }
    strip = r3[:, STRIP0:STRIP0 + STRIP_C]
    out["ple_w_proj"] = strip[:512].reshape(2, PLE_DIM, STRIP_C)
    out["ev_w_ukv"] = strip[512:640][None]
    out["ev_w_uq"] = strip[640:896, :96][None]
    return out


def _cols(a):
    return jnp.transpose(a, (1, 0, 2)).reshape(a.shape[1], -1)


def _blocks(g, c):
    return jnp.transpose(g.reshape(g.shape[0], N_DEV, c), (1, 0, 2))


def _uq_permute(w):
    r = w.shape[0]
    w3 = w.reshape(r, B_HEADS, B_NOPE + B_ROPE)
    half = B_ROPE // 2
    return jnp.concatenate([w3[:, :, :B_NOPE].reshape(r, -1), w3[:, :, B_NOPE:B_NOPE + half].reshape(r, -1), w3[:, :, B_NOPE + half:].reshape(r, -1)], axis=1)


def _uq_unpermute(g):
    r = g.shape[0]
    half = B_ROPE // 2
    n = B_HEADS * B_NOPE
    parts = [g[:, :n].reshape(r, B_HEADS, B_NOPE), g[:, n:n + B_HEADS * half].reshape(r, B_HEADS, half), g[:, n + B_HEADS * half:].reshape(r, B_HEADS, half)]
    return jnp.concatenate(parts, axis=2).reshape(r, -1)


def _ukv_permute(w):
    r = w.shape[0]
    return jnp.transpose(w.reshape(r, B_HEADS, 2, B_NOPE), (0, 2, 1, 3)).reshape(r, -1)


def _ukv_unpermute(g):
    r = g.shape[0]
    return jnp.transpose(g.reshape(r, 2, B_HEADS, B_NOPE), (0, 2, 1, 3)).reshape(r, -1)


def _od_in_widen(w):
    n = C_HEADS * C_HEAD_DIM
    wide = lambda m: jnp.pad(m.reshape(-1, C_HEADS, C_HEAD_DIM), ((0, 0), (0, 0), (0, QK_PAD - C_HEAD_DIM))).reshape(m.shape[0], -1)
    return jnp.concatenate([wide(w[:, :n] * C_HEAD_DIM ** -0.5), wide(w[:, n:2 * n]), w[:, 2 * n:],
                            jnp.zeros((w.shape[0], ODD_IN_PAD - ODD_IN_AUG), w.dtype)], axis=1)


def _od_in_narrow(g):
    wp = C_HEADS * QK_PAD
    narrow = lambda m: m.reshape(-1, C_HEADS, QK_PAD)[:, :, :C_HEAD_DIM].reshape(m.shape[0], -1)
    return jnp.concatenate([narrow(g[:, :wp]) * C_HEAD_DIM ** -0.5, narrow(g[:, wp:2 * wp]), g[:, 2 * wp:ODD_IN_AUG]], axis=1)


def _misc_weights(G3):
    strip = G3[:, :, STRIP0:STRIP0 + STRIP_C]
    return {
        "od_w_in": _od_in_widen(_cols(G3[:, :, :OD_C])),
        "ev_w_in": jnp.pad(_cols(G3[:, :, OD_C:STRIP0]), ((0, 0), (0, EVEN_IN_PAD - EVEN_IN))),
        "ple_w_proj": [_cols(strip[:, i * PLE_DIM:(i + 1) * PLE_DIM]) for i in range(DEPTH)],
        "ev_w_ukv": _ukv_permute(_cols(strip[:, 512:640])),
        "ev_w_uq": _uq_permute(_cols(strip[:, 640:896, :96])),
    }


def _misc_grads(G):
    strip = jnp.concatenate([
        _blocks(G["ple_w_proj"][0], STRIP_C), _blocks(G["ple_w_proj"][1], STRIP_C), _blocks(_ukv_unpermute(G["ev_w_ukv"]), STRIP_C),
        jnp.pad(_blocks(_uq_unpermute(G["ev_w_uq"]), 96), ((0, 0), (0, 0), (0, STRIP_C - 96))),
        jnp.zeros((N_DEV, G3_ROWS - 896, STRIP_C), F32)], axis=1)
    return jnp.concatenate([_blocks(_od_in_narrow(G["od_w_in"]), OD_C), _blocks(G["ev_w_in"][:, :EVEN_IN], EV_C), strip,
                            jnp.zeros((N_DEV, G3_ROWS, G3_COLS - STRIP0 - STRIP_C), F32)], axis=2)


def _ffn_fwd(h, norm_w, W, f, i, tag, ride=None):
    job = ride() if ride else None
    res = _ffn_gate_up(h, norm_w, W["C"][i].reshape(2, 4, C_ROWS, FF_BLK), f, f"{tag}_gate_up", job=job)
    n, gu, act = res[:3]
    if job is not None:
        ride(res[3:])
    job = ride() if ride else None
    out = _ffn_down(act, W["A"][i], f, h, f"{tag}_down", job=job)
    if job is not None:
        out, got = out
        ride(got)
    return out, (h, n, gu, act)


def _ffn_bwd(dout, saved, norm_w, W, GB, f, i, tag, ride=None):
    h, n, gu, act = saved
    S = h.shape[0]
    def carried(call):
        job = ride() if ride else None
        res = call(job)
        if job is None:
            return res
        ride(res[1])
        return res[0]

    GB["A"][i][f] = carried(lambda job: _ffn_down_dw(act, dout, f"{tag}_down_dw", job=job))
    dgu = _ffn_down_dx(dout, W["A"][i], f, gu, f"{tag}_down_dx").reshape(N_DEV, S, FF_BLK)
    res = carried(lambda job: _ffn_gate_up_dx(dgu, W["C"][i], f, h, norm_w, dout, f"{tag}_gate_up_dx", job=job))
    GB["C"][i][f] = carried(lambda job: _ffn_gate_up_dw(n, dgu, f"{tag}_gate_up_dw", job=job))
    return res


def _rope_tables(S):
    inv = ROPE_THETA ** (-jnp.arange(0, B_ROPE, 2, dtype=F32) / B_ROPE)
    ang = jnp.arange(S, dtype=F32)[:, None] * inv[None, :]
    return jnp.cos(ang), jnp.sin(ang)


def _alibi_columns(S):
    t = jnp.arange(S, dtype=jnp.int32)
    hi = ((t // 16) * 16).astype(F32)
    lo = (t % 16).astype(F32)
    slopes = 2.0 ** (-8.0 * jnp.arange(1, A_HEADS + 1, dtype=F32) / A_HEADS)
    zq = jnp.zeros((S, A_HEADS), F32)
    rest = QK_PAD - A_HEAD_DIM - 4
    qc = jnp.stack([-slopes[None, :] * hi[:, None], -slopes[None, :] * lo[:, None], zq + slopes[None, :], zq + slopes[None, :]] + [zq] * rest, axis=-1)
    one = jnp.ones((S, A_KV_HEADS), F32)
    zk = jnp.zeros((S, A_KV_HEADS), F32)
    kc = jnp.stack([one, one, zk + hi[:, None], zk + lo[:, None]] + [zk] * rest, axis=-1)
    return qc, kc


def _sink_prm(sinks):
    return jnp.zeros((A_HEADS, 1, LANES), F32).at[:, 0, 0].set(sinks.astype(F32))


def _with_ride(ride, call):
    job = ride() if ride else None
    res = call(job)
    if job is None:
        return res
    n_own = len(res) - len(job["outs"])
    ride(res[n_own:])
    return res[:n_own]


def _even_fwd(hn, h, W, ride=None):
    S = hn.shape[0]
    proj = _mm(hn, W["ev_w_in"], name="ev_in")
    a_q, a_k, a_v = proj[:, :512], proj[:, 512:640], proj[:, 640:768]
    c_q, c_kv = proj[:, 768:1024], proj[:, 1024:1152]
    kr1, kr2 = proj[:, 1152:1168], proj[:, 1168:1184]
    qc, kc = _alibi_columns(S)
    qa, qaT = _rows_and_cols(jnp.concatenate([(a_q * A_HEAD_DIM ** -0.5).reshape(S, A_HEADS, A_HEAD_DIM), qc], axis=-1))
    ka, kaT = _rows_and_cols(jnp.concatenate([a_k.reshape(S, A_KV_HEADS, A_HEAD_DIM), kc], axis=-1))
    va3 = a_v.reshape(S, A_KV_HEADS, A_HEAD_DIM)
    va = jnp.transpose(va3.astype(BF16), (1, 0, 2))
    prm = _sink_prm(W["ev_sinks"][0])
    oaT, lse_a = _with_ride(ride, lambda job: _attn_fwd(qaT, ka, _v_with_ones(va3), tile=SWA_TILE, hb=2, window=WINDOW, sink=prm,
                                                        name="swa_fwd", job=job))
    cqn = _rms_fwd(c_q, W["ev_cq_norm"], "ev_cq_norm")
    q_all = _mm(cqn, W["ev_w_uq"], name="ev_uq")
    ckvn = _rms_fwd(c_kv, W["ev_ckv_norm"], "ev_ckv_norm")
    kv_all = _mm(ckvn, W["ev_w_ukv"], name="ev_ukv")
    cos, sin = _rope_tables(S)
    cos8, sin8 = jnp.tile(cos, (1, B_HEADS)), jnp.tile(sin, (1, B_HEADS))
    q1, q2 = _rope(q_all[:, 512:640], q_all[:, 640:768], cos8, sin8, "ev_rope_q")
    k1, k2 = _rope(kr1, kr2, cos, sin, "ev_rope_k")
    half = B_ROPE // 2
    scale = (B_NOPE + B_ROPE) ** -0.5
    qb, qbT = _rows_and_cols(jnp.concatenate([q_all[:, :512].reshape(S, B_HEADS, B_NOPE), q1.reshape(S, B_HEADS, half), q2.reshape(S, B_HEADS, half)], axis=-1) * scale)
    kro = jnp.broadcast_to(jnp.concatenate([k1, k2], axis=1)[:, None, :], (S, B_HEADS, B_ROPE))
    kb, kbT = _rows_and_cols(jnp.concatenate([kv_all[:, :512].reshape(S, B_HEADS, B_NOPE), kro], axis=-1))
    vb3 = kv_all[:, 512:].reshape(S, B_HEADS, B_V)
    vb = jnp.transpose(vb3.astype(BF16), (1, 0, 2))
    obT, lse_b = _with_ride(ride, lambda job: _attn_fwd(qbT, kb, _v_with_ones(vb3), tile=min(ATTN_TILE_FWD, S), hb=2, name="mla_fwd", job=job))
    cat = jnp.concatenate([_from_T(oaT), _from_T(obT)], axis=1)
    out = _mm_w128(cat, W["B"][0], MIX_OUT_BLK, res=h, name="ev_out")
    return out, (hn, proj, (qa, qaT, ka, kaT, va, oaT, lse_a), prm, cqn, ckvn, (qb, qbT, kb, kbT, vb, obT, lse_b), cat)


def _even_bwd(dout, saved, W, GB, norm):
    hn, proj, (qa, qaT, ka, kaT, va, oaT, lse_a), prm, cqn, ckvn, (qb, qbT, kb, kbT, vb, obT, lse_b), cat = saved
    S = hn.shape[0]
    G = {}
    dcat = _mm_w128(dout, W["B"][0], MIX_OUT_BLK, tb=True, out=BF16, name="ev_out_dx")
    GB["B"][0] = _mm_w128_dw(cat, dout, MIX_OUT_BLK, GB["B"][0], "ev_out_dw")
    doa, doaT = _rows_and_cols(dcat[:, :512].reshape(S, A_HEADS, A_HEAD_DIM))
    dqaT, dka, dva, dsink = _attn_bwd(qa, qaT, ka, kaT, va, oaT, doa, doaT, lse_a, tile=SWA_TILE, hb=2, window=WINDOW, sink=prm, real=A_HEAD_DIM,
                                       name="swa_bwd")
    G["ev_sinks"] = dsink[:, 0, 0]
    dqa = _from_T(dqaT) * A_HEAD_DIM ** -0.5
    dka = dka.reshape(A_KV_HEADS, A_GROUP, S, A_HEAD_DIM).sum(axis=1)
    dva = dva.reshape(A_KV_HEADS, A_GROUP, S, A_HEAD_DIM).sum(axis=1)
    dob, dobT = _rows_and_cols(dcat[:, 512:].reshape(S, B_HEADS, B_V))
    dqbT, dkb, dvb = _attn_bwd(qb, qbT, kb, kbT, vb, obT, dob, dobT, lse_b, tile=min(ATTN_TILE, S), hb=1, name="mla_bwd")
    half = B_ROPE // 2
    dqb = jnp.transpose(dqbT, (2, 0, 1)) * (B_NOPE + B_ROPE) ** -0.5
    dkb = jnp.transpose(dkb, (1, 0, 2))
    cos, sin = _rope_tables(S)
    cos8, sin8 = jnp.tile(cos, (1, B_HEADS)), jnp.tile(sin, (1, B_HEADS))
    dq1, dq2 = _rope(dqb[:, :, B_NOPE:B_NOPE + half].reshape(S, -1), dqb[:, :, B_NOPE + half:].reshape(S, -1), cos8, -sin8, "ev_rope_q_bwd")
    dq_all = jnp.concatenate([dqb[:, :, :B_NOPE].reshape(S, -1), dq1, dq2], axis=1).astype(BF16)
    dkr = dkb[:, :, B_NOPE:].sum(axis=1)
    dk1, dk2 = _rope(dkr[:, :half], dkr[:, half:], cos, -sin, "ev_rope_k_bwd")
    dkv_all = jnp.concatenate([dkb[:, :, :B_NOPE].reshape(S, -1), _unheads(dvb)], axis=1).astype(BF16)
    G["ev_w_uq"] = _mm(cqn, dq_all, ta=True, name="ev_uq_dw")
    dcqn = _mm(dq_all, W["ev_w_uq"], tb=True, name="ev_uq_dx")
    dc_q, G["ev_cq_norm"] = _rms_bwd(dcqn, proj[:, 768:1024], W["ev_cq_norm"], None, "ev_cq_norm_bwd")
    G["ev_w_ukv"] = _mm(ckvn, dkv_all, ta=True, name="ev_ukv_dw")
    dckvn = _mm(dkv_all, W["ev_w_ukv"], tb=True, name="ev_ukv_dx")
    dc_kv, G["ev_ckv_norm"] = _rms_bwd(dckvn, proj[:, 1024:1152], W["ev_ckv_norm"], None, "ev_ckv_norm_bwd")
    dproj = jnp.concatenate([dqa, _unheads(dka), _unheads(dva), dc_q, dc_kv, dk1, dk2,
                             jnp.zeros((S, EVEN_IN_PAD - EVEN_IN), F32)], axis=1).astype(BF16)
    G["ev_w_in"] = _mm(hn, dproj, ta=True, name="ev_in_dw")
    dh, dnorm = _mm(dproj, W["ev_w_in"], tb=True, norm_bwd=(*norm, dout), name="ev_in_dx")
    return dh, dnorm, G


def _odd_fwd(hn, h, W, ride=None):
    S = hn.shape[0]
    w = C_HEADS * C_HEAD_DIM
    wp = C_HEADS * QK_PAD
    proj = _mm(hn, W["od_w_in"], name="od_in")
    f_logit = proj[:, 2 * wp + w: 2 * wp + w + C_HEADS]
    logf = _logsig_fwd(f_logit, W["od_b_f"], "od_logsig")
    logc = _cumsum(logf, False, "od_cumsum")
    parts = list(_exact3(logc))
    ones = [jnp.ones((S, C_HEADS), F32)] * 3
    pad = [jnp.zeros((S, C_HEADS), F32)] * (QK_PAD - C_HEAD_DIM - 6)
    lead = ((0, 0), (0, 0), (C_HEAD_DIM, 0))
    q3 = proj[:, :wp].reshape(S, C_HEADS, QK_PAD) + jnp.pad(jnp.stack(parts + ones + pad, axis=-1), lead)
    k3 = proj[:, wp:2 * wp].reshape(S, C_HEADS, QK_PAD) + jnp.pad(jnp.stack(ones + [-p for p in parts] + pad, axis=-1), lead)
    q, qT = _rows_and_cols(q3)
    k, kT = _rows_and_cols(k3)
    v3 = proj[:, 2 * wp:2 * wp + w].reshape(S, C_HEADS, C_HEAD_DIM)
    v = jnp.transpose(v3.astype(BF16), (1, 0, 2))
    oT, lse = _with_ride(ride, lambda job: _attn_fwd(qT, k, _v_with_ones(v3), tile=min(ATTN_TILE_FWD, S), hb=2, name="fox_fwd", job=job))
    cat = _from_T(oT)
    out = _mm_w128(cat, W["B"][1], MIX_OUT_BLK, res=h, name="od_out")
    return out, (hn, q, qT, k, kT, v, f_logit, oT, lse, cat)


def _odd_bwd(dout, saved, W, GB, norm):
    hn, q, qT, k, kT, v, f_logit, oT, lse, cat = saved
    S = hn.shape[0]
    G = {}
    dcat = _mm_w128(dout, W["B"][1], MIX_OUT_BLK, tb=True, out=BF16, name="od_out_dx")
    GB["B"][1] = _mm_w128_dw(cat, dout, MIX_OUT_BLK, GB["B"][1], "od_out_dw")
    do, doT = _rows_and_cols(dcat.reshape(S, C_HEADS, C_HEAD_DIM))
    dqT, dk, dv, dqxT, dkx = _attn_bwd(q, qT, k, kT, v, oT, do, doT, lse, tile=min(ATTN_TILE, S), hb=1, real=C_HEAD_DIM, extra=True,
                                       full=True, name="fox_bwd")
    dlogc = jnp.transpose(dqxT[:, 0, :] - dkx[:, :, 3])
    dlogf = _cumsum(dlogc, True, "od_cumsum_bwd")
    df, db = _logsig_bwd(dlogf, f_logit, W["od_b_f"], "od_logsig_bwd")
    G["od_b_f"] = db
    dproj = jnp.concatenate([_from_T(dqT), _unheads(dk), _unheads(dv), df, jnp.zeros((S, ODD_IN_PAD - ODD_IN_AUG), F32)], axis=1).astype(BF16)
    G["od_w_in"] = _mm(hn, dproj, ta=True, name="od_in_dw")
    dh, dnorm = _mm(dproj, W["od_w_in"], tb=True, norm_bwd=(*norm, dout), name="od_in_dx")
    return dh, dnorm, G


class _Rider:
    def __init__(self, steps, tag):
        self.steps, self.tag, self.count, self.result = steps, tag, 0, None
        self.job = next(steps)

    def __call__(self, got=None):
        if got is not None:
            return self._advance(list(got))
        job = self.job
        if isinstance(job, str):
            self._advance(None)
            return None
        return job

    def _advance(self, value):
        try:
            self.job = self.steps.send(value)
        except StopIteration as done:
            self.job, self.result = None, done.value

    def finish(self):
        while self.job is not None:
            if isinstance(self.job, str):
                self._advance(None)
                continue
            self.count += 1
            self(_comm_call(self.job, f"{self.tag}_{self.count}"))
        return self.result


def _gather_plan(W, slots):
    a0, b0, c0, m, a1, b1, c1 = (slots[key] for key in ("a0", "b0", "c0", "m", "a1", "b1", "c1"))
    (m,) = yield _gather_job([m])
    W.update(_misc_weights(m))
    (b0,) = yield _gather_job([b0])
    W["B"] = [b0]
    (c0,) = yield _gather_job([c0], rows=[(D_MODEL, D_MODEL)])
    W["C"] = [c0]
    a0, c1 = yield _gather_job([a0, c1], rows=[(DOWN_ROWS, DOWN_ROWS), (0, D_MODEL)])
    W["A"] = [a0]
    W["C"].append(c1)
    (a1,) = yield _gather_job([a1], rows=[(0, DOWN_ROWS)])
    W["A"].append(a1)
    for _ in range(3):
        yield "skip"
    a1, b1, c1 = yield _gather_job([a1, b1, c1], rows=[(DOWN_ROWS, DOWN_ROWS), None, (D_MODEL, D_MODEL)])
    W["A"][1], W["C"][1] = a1, c1
    W["B"].append(b1)


def _local_step(x, p, target, W, slots):
    h = x
    saved = []
    gather = _Rider(_gather_plan(W, slots), "all_gather_rest")
    for i in range(DEPTH):
        t = f"l{i}"
        h1, s_a = _ffn_fwd(h, W["ffa_norm"][i:i + 1], W, 0, i, f"{t}_ffa", gather)
        nm = _rms_fwd(h1, W["mix_norm"][i:i + 1], f"{t}_mix_norm")
        h2, s_m = (_even_fwd if i % 2 == 0 else _odd_fwd)(nm, h1, W, gather)
        h3, s_b = _ffn_fwd(h2, W["ffb_norm"][i:i + 1], W, 1, i, f"{t}_ffb", gather)
        npl = _rms_fwd(h3, W["ple_norm"][i:i + 1], f"{t}_ple_norm")
        gpre = _mm_w128(npl, W["B"][i], PLE_GATE_BLK, name=f"{t}_ple_gate")
        pp = _mm(p[i], W["ple_w_proj"][i], name=f"{t}_ple_proj")
        h4 = _ple_fwd(h3, gpre, pp, f"{t}_ple")
        saved.append((s_a, h1, s_m, s_b, h3, npl, gpre, pp))
        h = h4
    gather.finish()
    dh, g_final, loss_cols = _final_fwd_bwd(h, W["final_norm"], target, "final")
    G = {"final_norm": g_final}
    GB = {"A": [[None, None] for _ in range(DEPTH)], "C": [[None, None] for _ in range(DEPTH)],
          "B": [lax.empty((N_DEV, B_ROWS, D_MODEL), BF16) for _ in range(DEPTH)]}
    per_layer = {n: [None] * DEPTH for n in ("ffa_norm", "mix_norm", "ffb_norm", "ple_norm", "ple_w_proj")}
    scatter = scatter_mid = None
    for i in reversed(range(DEPTH)):
        t = f"l{i}"
        s_a, h1, s_m, s_b, h3, npl, gpre, pp = saved[i]
        dgpre, dpp = _ple_bwd(dh, gpre, pp, f"{t}_ple_bwd")
        per_layer["ple_w_proj"][i] = _mm(p[i], dpp, ta=True, name=f"{t}_ple_proj_dw")
        GB["B"][i] = _mm_w128_dw(npl, dgpre, PLE_GATE_BLK, GB["B"][i], f"{t}_ple_gate_dw")
        dh, per_layer["ple_norm"][i] = _mm_w128(dgpre, W["B"][i], PLE_GATE_BLK, tb=True, norm_bwd=(h3, W["ple_norm"][i:i + 1], dh),
                                                name=f"{t}_ple_gate_dx")
        dh, per_layer["ffb_norm"][i] = _ffn_bwd(dh, s_b, W["ffb_norm"][i:i + 1], W, GB, 1, i, f"{t}_ffb", scatter)
        dh, per_layer["mix_norm"][i], g_mix = (_even_bwd if i % 2 == 0 else _odd_bwd)(dh, s_m, W, GB, (h1, W["mix_norm"][i:i + 1]))
        G.update(g_mix)
        if i == 0:
            G["ple_w_proj"] = per_layer["ple_w_proj"]
            mid = [GB["A"][0][1], GB["C"][0][1], GB["B"][0], _misc_grads(G).astype(BF16)]
            scatter_mid = _Rider(_reduce_scatter_steps(mid, "mid"), "rs_mid")
        dh, per_layer["ffa_norm"][i] = _ffn_bwd(dh, s_a, W["ffa_norm"][i:i + 1], W, GB, 0, i, f"{t}_ffa", scatter_mid)
        if i == DEPTH - 1:
            later = [GB["A"][i][0], GB["A"][i][1], GB["C"][i][0], GB["C"][i][1], GB["B"][i]]
            scatter = _Rider(_reduce_scatter_steps(later, "later"), "rs_later")
    for n in ("ffa_norm", "mix_norm", "ffb_norm", "ple_norm"):
        G[n] = jnp.concatenate(per_layer[n], axis=0)
    return loss_cols, dh, scatter.finish(), scatter_mid.finish(), [GB["A"][0][0], GB["C"][0][0]], G


def kernel(x, p, ffa_norm, ffa_w_gate_up, ffa_w_down, mix_norm, ffb_norm, ffb_w_gate_up, ffb_w_down, ple_norm, ple_w_gate, ple_w_proj, ev_w_in, ev_sinks, ev_cq_norm, ev_w_uq, ev_ckv_norm, ev_w_ukv, ev_w_out, od_w_in, od_b_f, od_w_out, final_norm, loss_target, m_ffa_norm, m_ffa_w_gate_up, m_ffa_w_down, m_mix_norm, m_ffb_norm, m_ffb_w_gate_up, m_ffb_w_down, m_ple_norm, m_ple_w_gate, m_ple_w_proj, m_ev_w_in, m_ev_sinks, m_ev_cq_norm, m_ev_w_uq, m_ev_ckv_norm, m_ev_w_ukv, m_ev_w_out, m_od_w_in, m_od_b_f, m_od_w_out, m_final_norm, v_ffa_norm, v_ffa_w_gate_up, v_ffa_w_down, v_mix_norm, v_ffb_norm, v_ffb_w_gate_up, v_ffb_w_down, v_ple_norm, v_ple_w_gate, v_ple_w_proj, v_ev_w_in, v_ev_sinks, v_ev_cq_norm, v_ev_w_uq, v_ev_ckv_norm, v_ev_w_ukv, v_ev_w_out, v_od_w_in, v_od_b_f, v_od_w_out, v_final_norm):
    given = dict(locals())
    w_in = {n: given[n] for n in WEIGHTS}

    layers, misc = _local_groups(w_in, BF16)
    (a0, b0, c0), (a1, b1, c1) = [[_in_slot(g) for g in layer] for layer in layers]
    a0, c0 = _comm_call(_gather_job([a0, c0], rows=[(0, DOWN_ROWS), (0, D_MODEL)]), "all_gather_first")
    W = {n: w_in[n] for n in SMALL}
    W["final_norm"] = final_norm.reshape(1, -1)
    W.update(A=[a0], C=[c0])
    slots = dict(a0=a0, b0=b0, c0=c0, m=_in_slot(misc), a1=a1, b1=b1, c1=c1)

    loss_cols, dx, r_later, r_mid, last, G = _local_step(x[0], p[:, 0], loss_target[0], W, slots)

    a1f, a1b, c1f, c1b, b1 = r_later
    a0b, c0b, b0, r_misc = r_mid
    a0f, c0f = _reduce_scatter(last, "last")
    grads = _ungroup_local([[a0f, a0b], [a1f, a1b]], [b0, b1], [[c0f, c0b], [c1f, c1b]], r_misc)
    layout = [(n, int(np.prod(w_in[n].shape))) for n in SMALL]
    vec = jnp.concatenate([G[n].astype(F32).reshape(-1) for n, _ in layout] + [jnp.sum(loss_cols).reshape(1)])
    vec = jnp.pad(vec, (0, N_DEV * SMALL_COLS - vec.shape[0])).reshape(N_DEV, SMALL_COLS)
    vec = _all_reduce_small(vec).reshape(-1)
    off = 0
    for n, size in layout:
        grads[n] = vec[off: off + size].reshape(w_in[n].shape)
        off += size
    loss = vec[off]

    delta, new_m, new_v = {}, {}, {}
    for n in WEIGHTS:
        shp = w_in[n].shape
        as2d = (lambda a: a.reshape(1, -1)) if len(shp) == 1 else (lambda a: a)
        d, nm, nv = _adamw(as2d(w_in[n]), as2d(grads[n]), as2d(given["m_" + n]), as2d(given["v_" + n]), f"adamw_{n}")
        delta[n], new_m[n], new_v[n] = d.reshape(shp), nm.reshape(shp), nv.reshape(shp)
    return (loss, dx[None], *[grads[n] for n in WEIGHTS], *[delta[n] for n in WEIGHTS],
            *[new_m[n] for n in WEIGHTS], *[new_v[n] for n in WEIGHTS])
```

```python
import functools

import numpy as np
import jax
import jax.numpy as jnp
from jax import lax
from jax.experimental import pallas as pl
from jax.experimental.pallas import tpu as pltpu

F32 = jnp.float32
BF16 = jnp.bfloat16
MESH = pl.DeviceIdType.MESH

D_MODEL = 1024
D_FF = 2816
RMS_EPS = 1e-6
PLE_DIM = 256
A_HEADS, A_KV_HEADS, A_HEAD_DIM, WINDOW = 8, 2, 64, 128
A_GROUP = A_HEADS // A_KV_HEADS
B_HEADS, B_Q_LORA, B_KV_LORA, B_NOPE, B_ROPE, B_V = 8, 256, 128, 64, 32, 64
ROPE_THETA = 10000.0
C_HEADS, C_HEAD_DIM = 16, 64
EVEN_IN = 1184
EVEN_IN_PAD = 1280
ODD_IN = 3088
ODD_IN_AUG = 2 * 16 * 80 + 1024 + 16
ODD_IN_PAD = 3840
DEPTH = 2
ADAM_LR, ADAM_B1, ADAM_B2, ADAM_EPS, ADAM_WD, ADAM_STEP = 0.001, 0.9, 0.999, 1e-08, 0.01, 10

N_DEV = 8
LANES = 128
SUBLANES = 8
EW_TILE_BYTES = 3 << 20
MM_VMEM_BYTES = 26 << 20
NEG = -1e30
ATTN_TILE = 1024
ATTN_TILE_FWD = 1024
SWA_TILE = 256
QK_PAD = 80

FF_BLK = D_FF // 4
DOWN_ROWS = D_FF // N_DEV
A_ROWS, B_ROWS, C_ROWS, G3_ROWS, G3_COLS = 2 * DOWN_ROWS, 256, 2 * D_MODEL, 1024, 768
PLE_GATE_BLK, MIX_OUT_BLK = 0, 1
OD_C, EV_C, STRIP_C = 386, 148, 128
STRIP0 = OD_C + EV_C

SMALL = ["ffa_norm", "mix_norm", "ffb_norm", "ple_norm", "ev_sinks", "ev_cq_norm", "ev_ckv_norm", "od_b_f", "final_norm"]
WEIGHTS = ["ffa_norm", "ffa_w_gate_up", "ffa_w_down", "mix_norm", "ffb_norm", "ffb_w_gate_up", "ffb_w_down", "ple_norm",
           "ple_w_gate", "ple_w_proj", "ev_w_in", "ev_sinks", "ev_cq_norm", "ev_w_uq", "ev_ckv_norm", "ev_w_ukv", "ev_w_out",
           "od_w_in", "od_b_f", "od_w_out", "final_norm"]
SMALL_COLS = 1280


def _divisor(n, cap, mult):
    if n <= cap:
        return n
    for t in range(cap - cap % mult, 0, -mult):
        if n % t == 0:
            return t
    raise ValueError(f"no tile for {n} under {cap} in steps of {mult}")


def _lanes(c):
    return -(-c // LANES) * LANES


def _ew(fn, rows, vecs, outs, reds=(), *, name):
    R = rows[0].shape[0]
    per_row = sum(_lanes(a.shape[1]) * a.dtype.itemsize for a in rows) + sum(_lanes(c) * jnp.dtype(d).itemsize for c, d in outs)
    tm = _divisor(R, max(16, EW_TILE_BYTES // per_row // 16 * 16), 16) if R % 16 == 0 else R
    n_r, n_v, n_o = len(rows), len(vecs), len(outs)

    def body(*refs):
        ins = [r[...] for r in refs[: n_r + n_v]]
        res = fn(*ins)
        if not isinstance(res, (tuple, list)):
            res = (res,)
        o_refs = refs[n_r + n_v: n_r + n_v + n_o]
        r_refs = refs[n_r + n_v + n_o:]
        for ref, val in zip(o_refs, res[:n_o]):
            ref[...] = val.astype(ref.dtype)
        if r_refs:
            @pl.when(pl.program_id(0) == 0)
            def _():
                for ref in r_refs:
                    ref[...] = jnp.zeros_like(ref)
            for ref, val in zip(r_refs, res[n_o:]):
                ref[...] += val

    in_specs = [pl.BlockSpec((tm, a.shape[1]), lambda i: (i, 0)) for a in rows]
    in_specs += [pl.BlockSpec((1, a.shape[1]), lambda i: (0, 0)) for a in vecs]
    out_specs = [pl.BlockSpec((tm, c), lambda i: (i, 0)) for c, _ in outs]
    out_specs += [pl.BlockSpec((1, c), lambda i: (0, 0)) for c in reds]
    out_shape = [jax.ShapeDtypeStruct((R, c), d) for c, d in outs] + [jax.ShapeDtypeStruct((1, c), F32) for c in reds]
    res = pl.pallas_call(body, name=name, grid=(R // tm,), in_specs=in_specs, out_specs=out_specs, out_shape=out_shape)(*rows, *vecs)
    return res[0] if len(res) == 1 else res


def _rms_fwd(x, w, name):
    def fn(x, w):
        y = x * lax.rsqrt(jnp.mean(x * x, axis=-1, keepdims=True) + RMS_EPS)
        return y * w
    return _ew(fn, [x], [w], [(x.shape[1], BF16)], name=name)


def _rms_bwd(dn, x, w, dres, name):
    def fn(dn, x, *rest):
        w = rest[-1]
        r = lax.rsqrt(jnp.mean(x * x, axis=-1, keepdims=True) + RMS_EPS)
        xh = x * r
        gw = dn * w
        dx = r * (gw - xh * jnp.mean(gw * xh, axis=-1, keepdims=True))
        if len(rest) == 2:
            dx = dx + rest[0]
        return dx, jnp.sum(dn * xh, axis=0, keepdims=True)
    rows = [dn, x] + ([dres] if dres is not None else [])
    return _ew(fn, rows, [w], [(x.shape[1], F32)], [x.shape[1]], name=name)


def _ple_fwd(h, gpre, pp, name):
    return _ew(lambda h, g, q: h + jax.nn.sigmoid(g) * q, [h, gpre, pp], [], [(h.shape[1], F32)], name=name)


def _ple_bwd(dh, gpre, pp, name):
    def fn(dh, g, q):
        sg = jax.nn.sigmoid(g)
        return dh * q * (sg * (1.0 - sg)), dh * sg
    return _ew(fn, [dh, gpre, pp], [], [(dh.shape[1], BF16), (dh.shape[1], BF16)], name=name)


def _rope(x1, x2, cos, sin, name):
    c = x1.shape[1]
    return _ew(lambda a, b, co, si: (a * co - b * si, a * si + b * co), [x1, x2, cos, sin], [], [(c, F32), (c, F32)], name=name)


def _logsig_fwd(f, b, name):
    def fn(f, b):
        z = f + b
        return jnp.minimum(z, 0.0) - jnp.log(1.0 + jnp.exp(-jnp.abs(z)))
    return _ew(fn, [f], [b], [(f.shape[1], F32)], name=name)


def _logsig_bwd(dlogf, f, b, name):
    def fn(d, f, b):
        df = d * jax.nn.sigmoid(-(f + b))
        return df, jnp.sum(df, axis=0, keepdims=True)
    return _ew(fn, [dlogf, f], [b], [(f.shape[1], F32)], [f.shape[1]], name=name)


def _final_fwd_bwd(h, w, target, name):
    d = h.shape[1]

    def fn(h, t, w):
        r = lax.rsqrt(jnp.mean(h * h, axis=-1, keepdims=True) + RMS_EPS)
        xh = h * r
        y = xh * w
        err = y - t
        dy = err * (1.0 / d)
        gw = dy * w
        dx = r * (gw - xh * jnp.mean(gw * xh, axis=-1, keepdims=True))
        return dx, jnp.sum(dy * xh, axis=0, keepdims=True), jnp.sum(err * err, axis=0, keepdims=True) * (0.5 / d)
    return _ew(fn, [h, target], [w], [(d, F32)], [d, d], name=name)


def _adamw(w, g, m, v, name):
    shape = w.shape
    c = shape[-1]
    w2, g2, m2, v2 = (a.reshape(-1, c) for a in (w, g, m, v))

    def fn(w, g, m, v):
        m = ADAM_B1 * m + (1.0 - ADAM_B1) * g
        v = ADAM_B2 * v + (1.0 - ADAM_B2) * jnp.square(g)
        m_hat = m / (1.0 - ADAM_B1 ** ADAM_STEP)
        v_hat = v / (1.0 - ADAM_B2 ** ADAM_STEP)
        delta = -ADAM_LR * (m_hat / (jnp.sqrt(v_hat) + ADAM_EPS) + ADAM_WD * w)
        return delta, m, v
    d, nm, nv = _ew(fn, [w2, g2, m2, v2], [], [(c, F32)] * 3, name=name)
    return d.reshape(shape), nm.reshape(shape), nv.reshape(shape)


def _split3(v):
    hi = v.astype(BF16)
    r1 = v - hi.astype(F32)
    mid = r1.astype(BF16)
    lo = (r1 - mid.astype(F32)).astype(BF16)
    return hi, mid, lo


def _cumsum(x, reverse, name):
    S, C = x.shape
    tm = _divisor(S, 512, 16)
    nt = S // tm

    def body(x_ref, o_ref, carry):
        @pl.when(pl.program_id(0) == 0)
        def _():
            carry[...] = jnp.zeros_like(carry)
        r = lax.broadcasted_iota(jnp.int32, (tm, tm), 0)
        c = lax.broadcasted_iota(jnp.int32, (tm, tm), 1)
        tri = jnp.where((c >= r) if reverse else (c <= r), 1.0, 0.0).astype(BF16)
        xv = x_ref[...]
        acc = jnp.zeros((tm, C), F32)
        for part in _split3(xv):
            acc = acc + jnp.dot(tri, part, preferred_element_type=F32)
        o_ref[...] = acc + carry[...]
        carry[...] += jnp.sum(xv, axis=0, keepdims=True)

    idx = (lambda i: (nt - 1 - i, 0)) if reverse else (lambda i: (i, 0))
    return pl.pallas_call(
        body, name=name, grid=(nt,), in_specs=[pl.BlockSpec((tm, C), idx)], out_specs=pl.BlockSpec((tm, C), idx),
        out_shape=jax.ShapeDtypeStruct((S, C), F32), scratch_shapes=[pltpu.VMEM((1, C), F32)],
    )(x)


NN = (((1,), (0,)), ((), ()))
NT = (((1,), (1,)), ((), ()))
TN = (((0,), (0,)), ((), ()))

HBM_SPEC = pl.BlockSpec(memory_space=pl.ANY)


def _job_in_body(job, refs, n_in, n_out, n_scr, grid):
    if job is None:
        return refs[n_in:], lambda: None
    ji, jo = len(job["ins"]), len(job["outs"])
    j_in = refs[n_in: n_in + ji]
    pos = n_in + ji
    own = list(refs[pos: pos + n_out])
    pos += n_out
    j_out = refs[pos: pos + jo]
    pos += jo
    own += list(refs[pos: pos + n_scr])
    ss, rs = refs[-2], refs[-1]
    first = functools.reduce(jnp.logical_and, [pl.program_id(d) == 0 for d in range(len(grid))])
    last = functools.reduce(jnp.logical_and, [pl.program_id(d) == n - 1 for d, n in enumerate(grid)])

    @pl.when(first)
    def _():
        job["start"](j_in, j_out, ss, rs)

    def finish():
        @pl.when(last)
        def _():
            job["finish"](j_in, j_out, ss, rs)

    return own, finish


def _job_call(job, body, *, name, grid, in_specs, out_specs, out_shape, args, scratch_shapes, aliases, dimension_semantics):
    in_specs, out_specs, out_shape, args, scratch_shapes = list(in_specs), list(out_specs), list(out_shape), list(args), list(scratch_shapes)
    aliases = dict(aliases)
    if job is not None:
        for i, o in job["aliases"].items():
            aliases[len(args) + i] = len(out_shape) + o
        in_specs += [HBM_SPEC] * len(job["ins"])
        args += list(job["ins"])
        out_specs += [HBM_SPEC] * len(job["outs"])
        out_shape += list(job["outs"])
        scratch_shapes += [pltpu.SemaphoreType.DMA((job["n_sems"],)), pltpu.SemaphoreType.DMA((job["n_sems"],))]
    return pl.pallas_call(
        body, name=name, grid=grid, in_specs=in_specs, out_specs=out_specs, out_shape=out_shape,
        scratch_shapes=scratch_shapes, input_output_aliases=aliases,
        compiler_params=pltpu.CompilerParams(dimension_semantics=dimension_semantics),
    )(*args)


def _comm_call(job, name):
    def body(*refs):
        ji, jo = len(job["ins"]), len(job["outs"])
        job["start"](refs[:ji], refs[ji: ji + jo], refs[-2], refs[-1])
        job["finish"](refs[:ji], refs[ji: ji + jo], refs[-2], refs[-1])

    return pl.pallas_call(
        body, name=name, in_specs=[HBM_SPEC] * len(job["ins"]), out_specs=[HBM_SPEC] * len(job["outs"]), out_shape=list(job["outs"]),
        input_output_aliases=dict(job["aliases"]),
        scratch_shapes=[pltpu.SemaphoreType.DMA((job["n_sems"],)), pltpu.SemaphoreType.DMA((job["n_sems"],))],
    )(*job["ins"])


def _mm_call(name, grid, k_axis, a, a_spec, a2d, b, b_spec, b2d, dims, out_sds, out_spec, o2d, *,
             alpha=1.0, res=None, res_spec=None, into=None, job=None, norm_bwd=None):
    nk = grid[k_axis]
    n_in = 2 + (res is not None) + (into is not None) + (3 if norm_bwd is not None else 0)
    n_out = 2 if norm_bwd is not None else 1

    def body(*refs):
        a_ref, b_ref = refs[0], refs[1]
        res_ref = refs[2] if res is not None else None
        own, finish_job = _job_in_body(job, refs, n_in, n_out, 1, grid)
        o_ref, acc_ref = own[0], own[-1]
        k = pl.program_id(k_axis)

        @pl.when(k == 0)
        def _():
            acc_ref[...] = jnp.zeros_like(acc_ref)

        if norm_bwd is not None:
            x_ref, w_ref, dres_ref = refs[n_in - 3: n_in]
            dw_ref = own[1]

            @pl.when(functools.reduce(jnp.logical_and, [pl.program_id(d) == 0 for d in range(len(grid))]))
            def _():
                dw_ref[...] = jnp.zeros_like(dw_ref)

        av = a_ref[...].reshape(a2d).astype(BF16)
        bv = b_ref[...].reshape(b2d).astype(BF16)
        acc_ref[...] += lax.dot_general(av, bv, dims, preferred_element_type=F32)

        @pl.when(k == nk - 1)
        def _():
            r = acc_ref[...]
            if alpha != 1.0:
                r = r * alpha
            if res_ref is not None:
                r = res_ref[...].reshape(o2d) + r
            if norm_bwd is not None:
                x = x_ref[...]
                rs = lax.rsqrt(jnp.mean(x * x, axis=-1, keepdims=True) + RMS_EPS)
                xh = x * rs
                gw = r * w_ref[...]
                dw_ref[...] += jnp.sum(r * xh, axis=0, keepdims=True)
                r = dres_ref[...] + rs * (gw - xh * jnp.mean(gw * xh, axis=-1, keepdims=True))
            o_ref[...] = r.reshape(o_ref.shape).astype(o_ref.dtype)

        finish_job()

    in_specs, args = [a_spec, b_spec], [a, b]
    if res is not None:
        in_specs.append(res_spec)
        args.append(res)
    aliases = {}
    if into is not None:
        aliases = {len(args): 0}
        in_specs.append(pl.BlockSpec(memory_space=pl.ANY))
        args.append(into)
        out_sds = jax.ShapeDtypeStruct(into.shape, into.dtype)
    out_specs, out_shape = [out_spec], [out_sds]
    if norm_bwd is not None:
        vec = pl.BlockSpec((1, o2d[1]), lambda *_: (0, 0))
        in_specs += [out_spec, vec, out_spec]
        args += list(norm_bwd)
        out_specs.append(vec)
        out_shape.append(jax.ShapeDtypeStruct((1, o2d[1]), F32))
    serial = job is not None or norm_bwd is not None
    sem = tuple("arbitrary" if d == k_axis or serial else "parallel" for d in range(len(grid)))
    res_all = _job_call(
        job, body, name=name, grid=grid, in_specs=in_specs, out_specs=out_specs, out_shape=out_shape, args=args,
        scratch_shapes=[pltpu.VMEM(o2d, F32)], aliases=aliases, dimension_semantics=sem)
    own = res_all[0] if n_out == 1 else tuple(res_all[:n_out])
    return own if job is None else (own, res_all[n_out:])


def _mm(a, b, *, ta=False, tb=False, out=F32, res=None, alpha=1.0, norm_bwd=None, name):
    K, M = a.shape if ta else a.shape[::-1]
    N = b.shape[0] if tb else b.shape[1]
    assert (b.shape[1] if tb else b.shape[0]) == K, (a.shape, b.shape, ta, tb)
    tk = _divisor(K, 1024, LANES)
    tn = _divisor(N, 1408, LANES)
    assert norm_bwd is None or tn == N
    for cap in (1024, 512, 256, 128):
        tm = _divisor(M, cap, LANES if ta else 16)
        est = 2 * (tm * tk * a.dtype.itemsize + tk * tn * b.dtype.itemsize + tm * tn * jnp.dtype(out).itemsize)
        est += tm * tn * 4 + (2 * tm * tn * 4 if res is not None else 0) + (4 * tm * tn * 4 if norm_bwd is not None else 0)
        if est <= MM_VMEM_BYTES:
            break
    a_spec = pl.BlockSpec((tk, tm), lambda i, j, k: (k, i)) if ta else pl.BlockSpec((tm, tk), lambda i, j, k: (i, k))
    b_spec = pl.BlockSpec((tn, tk), lambda i, j, k: (j, k)) if tb else pl.BlockSpec((tk, tn), lambda i, j, k: (k, j))
    o_spec = pl.BlockSpec((tm, tn), lambda i, j, k: (i, j))
    dims = (((0 if ta else 1,), (1 if tb else 0,)), ((), ()))
    return _mm_call(name, (M // tm, N // tn, K // tk), 2, a, a_spec, (tk, tm) if ta else (tm, tk), b, b_spec,
                    (tn, tk) if tb else (tk, tn), dims, jax.ShapeDtypeStruct((M, N), out), o_spec, (tm, tn),
                    alpha=alpha, res=res, res_spec=o_spec, norm_bwd=norm_bwd)


def _w128_spec(blk):
    return pl.BlockSpec((N_DEV, 128, D_MODEL), lambda *_: (0, blk, 0))


def _mm_w128(a, G1, blk, *, tb=False, res=None, out=F32, norm_bwd=None, name):
    S = a.shape[0]
    tm = _divisor(S, 1024 if norm_bwd is None else 512, 16)
    row = pl.BlockSpec((tm, D_MODEL), lambda i, k: (i, 0))
    return _mm_call(name, (S // tm, 1), 1, a, row, (tm, D_MODEL), G1, _w128_spec(blk), (D_MODEL, D_MODEL), NT if tb else NN,
                    jax.ShapeDtypeStruct((S, D_MODEL), out), row, (tm, D_MODEL), res=res, res_spec=row, norm_bwd=norm_bwd)


def _mm_w128_dw(a, b, blk, into, name):
    S = a.shape[0]
    tk = _divisor(S, 1024, 16)
    row = pl.BlockSpec((tk, D_MODEL), lambda i, k: (k, 0))
    return _mm_call(name, (1, S // tk), 1, a, row, (tk, D_MODEL), b, row, (tk, D_MODEL), TN, None, _w128_spec(blk),
                    (D_MODEL, D_MODEL), into=into)


def _ffn_gate_up(h, norm_w, G2v, rb, name, job=None):
    S = h.shape[0]
    tm = _divisor(S, 1024, 16)
    grid = (S // tm, 4)

    def body(*refs):
        h_ref, nw_ref, w_ref = refs[:3]
        (n_ref, gu_ref, act_ref, n_scr), finish_job = _job_in_body(job, refs, 3, 3, 1, grid)

        @pl.when(pl.program_id(1) == 0)
        def _():
            x = h_ref[...]
            y = x * lax.rsqrt(jnp.mean(x * x, axis=-1, keepdims=True) + RMS_EPS)
            n_scr[...] = (y * nw_ref[...]).astype(BF16)
            n_ref[...] = n_scr[...]

        nv = n_scr[...]
        g = jnp.dot(nv, w_ref[0, 0], preferred_element_type=F32)
        u = jnp.dot(nv, w_ref[1, 0], preferred_element_type=F32)
        sg = jax.nn.sigmoid(g)
        silu = g * sg
        gu_ref[0, 0] = (u * (sg * (1.0 + g * (1.0 - sg)))).astype(BF16)
        gu_ref[1, 0] = silu.astype(BF16)
        act_ref[0] = (silu * u).astype(BF16)
        finish_job()

    row = pl.BlockSpec((tm, D_MODEL), lambda i, j: (i, 0))
    return _job_call(
        job, body, name=name, grid=grid,
        in_specs=[row, pl.BlockSpec((1, D_MODEL), lambda i, j: (0, 0)), pl.BlockSpec((2, 1, D_MODEL, FF_BLK), lambda i, j: (0, j, rb, 0))],
        out_specs=[row, pl.BlockSpec((2, 1, tm, FF_BLK), lambda i, j: (0, j, i, 0)), pl.BlockSpec((1, tm, FF_BLK), lambda i, j: (j, i, 0))],
        out_shape=[jax.ShapeDtypeStruct((S, D_MODEL), BF16), jax.ShapeDtypeStruct((2, 4, S, FF_BLK), BF16), jax.ShapeDtypeStruct((4, S, FF_BLK), BF16)],
        args=[h, norm_w, G2v], scratch_shapes=[pltpu.VMEM((tm, D_MODEL), BF16)], aliases={},
        dimension_semantics=("arbitrary" if job is not None else "parallel", "arbitrary"))


def _ffn_down(act, G1, ob, h, name, job=None):
    S = h.shape[0]
    tm = _divisor(S, 1024, 16)
    row = pl.BlockSpec((tm, D_MODEL), lambda i, k: (i, 0))
    return _mm_call(name, (S // tm, 4), 1, act, pl.BlockSpec((1, tm, FF_BLK), lambda i, k: (k, i, 0)), (tm, FF_BLK),
                    G1, pl.BlockSpec((2, DOWN_ROWS, D_MODEL), lambda i, k: (k, ob, 0)), (FF_BLK, D_MODEL), NN,
                    jax.ShapeDtypeStruct((S, D_MODEL), F32), row, (tm, D_MODEL), alpha=0.5, res=h, res_spec=row, job=job)


def _ffn_down_dx(dh, G1, ob, gu, name):
    S = dh.shape[0]
    tm = _divisor(S, 1024, 16)

    def body(dh_ref, w_ref, gu_ref, o_ref):
        w = w_ref[...].reshape(FF_BLK, D_MODEL)
        dact = lax.dot_general(dh_ref[...].astype(BF16), w, NT, preferred_element_type=F32) * 0.5
        o_ref[0, 0] = (dact * gu_ref[0, 0].astype(F32)).astype(BF16)
        o_ref[1, 0] = (dact * gu_ref[1, 0].astype(F32)).astype(BF16)

    blk = pl.BlockSpec((2, 1, tm, FF_BLK), lambda i, j: (0, j, i, 0))
    return pl.pallas_call(
        body, name=name, grid=(S // tm, 4),
        in_specs=[pl.BlockSpec((tm, D_MODEL), lambda i, j: (i, 0)), pl.BlockSpec((2, DOWN_ROWS, D_MODEL), lambda i, j: (j, ob, 0)), blk],
        out_specs=blk, out_shape=jax.ShapeDtypeStruct((2, 4, S, FF_BLK), BF16),
    )(dh, G1, gu)


def _ffn_down_dw(act, dh, name, job=None):
    S = dh.shape[0]
    tk = _divisor(S, 1024, 16)
    return _mm_call(name, (4, S // tk), 1, act, pl.BlockSpec((1, tk, FF_BLK), lambda j, k: (j, k, 0)), (tk, FF_BLK),
                    dh, pl.BlockSpec((tk, D_MODEL), lambda j, k: (k, 0)), (tk, D_MODEL), TN,
                    jax.ShapeDtypeStruct((N_DEV, DOWN_ROWS, D_MODEL), BF16),
                    pl.BlockSpec((2, DOWN_ROWS, D_MODEL), lambda j, k: (j, 0, 0)), (FF_BLK, D_MODEL), alpha=0.5, job=job)


def _ffn_gate_up_dw(n, dgu8, name, job=None):
    S = n.shape[0]
    tk = _divisor(S, 1024, 16)
    return _mm_call(name, (N_DEV, S // tk), 1, n, pl.BlockSpec((tk, D_MODEL), lambda b, k: (k, 0)), (tk, D_MODEL),
                    dgu8, pl.BlockSpec((1, tk, FF_BLK), lambda b, k: (b, k, 0)), (tk, FF_BLK), TN,
                    jax.ShapeDtypeStruct((N_DEV, D_MODEL, FF_BLK), BF16),
                    pl.BlockSpec((1, D_MODEL, FF_BLK), lambda b, k: (b, 0, 0)), (D_MODEL, FF_BLK), job=job)


def _ffn_gate_up_dx(dgu8, G2, rb, h, norm_w, dres, name, job=None):
    S = h.shape[0]
    tm = _divisor(S, 1024, 16)
    row = pl.BlockSpec((tm, D_MODEL), lambda i, k: (i, 0))
    return _mm_call(name, (S // tm, N_DEV), 1, dgu8, pl.BlockSpec((1, tm, FF_BLK), lambda i, k: (k, i, 0)), (tm, FF_BLK),
                    G2, pl.BlockSpec((1, D_MODEL, FF_BLK), lambda i, k: (k, rb, 0)), (D_MODEL, FF_BLK), NT,
                    jax.ShapeDtypeStruct((S, D_MODEL), F32), row, (tm, D_MODEL), norm_bwd=(h, norm_w, dres), job=job)


def _unheads(x):
    h, S, d = x.shape
    return jnp.transpose(x, (1, 0, 2)).reshape(S, h * d)


def _exact3(v):
    rnd = lambda a: lax.reduce_precision(a, exponent_bits=8, mantissa_bits=7)
    hi = rnd(v)
    mid = rnd(v - hi)
    return hi, mid, rnd(v - hi - mid)


def _causal_mask(st, q0, k0, window):
    dist = (q0 + lax.broadcasted_iota(jnp.int32, st.shape, 1)) - (k0 + lax.broadcasted_iota(jnp.int32, st.shape, 0))
    mask = dist >= 0
    if window is not None:
        mask = mask & (dist < window)
    return jnp.where(mask, st, NEG)


def _attn_fwd(qT, k, vT1, *, tile, hb, window=None, sink=None, name, job=None):
    H, dqk, S = qT.shape
    G = H // k.shape[0]
    dvp = vT1.shape[1]
    dv = dvp - 16
    tq = tk = tile
    assert H % hb == 0 and (G == 1 or G % hb == 0)
    kvb = hb if G == 1 else 1
    grid = (H // hb, S // tq)
    n_in = 3 + (sink is not None)

    def body(*refs):
        q_ref, k_ref, v_ref = refs[:3]
        (o_ref, lse_ref), finish_job = _job_in_body(job, refs, n_in, 2, 0, grid)
        i = pl.program_id(1)
        carry = []
        for a in range(hb):
            if sink is not None:
                carry.append(jnp.zeros((1, tq), F32) + refs[3][a, :, 0:1])
                carry.append(jnp.where(lax.broadcasted_iota(jnp.int32, (dvp, tq), 0) == dv, 1.0, 0.0))
            else:
                carry.append(jnp.full((1, tq), NEG, F32))
                carry.append(jnp.zeros((dvp, tq), F32))

        def step(j, carry, masked, off=None, keys=tk):
            off = pl.multiple_of(j * tk, tk) if off is None else off
            out = []
            for a in range(hb):
                m, acc = carry[2 * a], carry[2 * a + 1]
                kv = a if kvb > 1 else 0
                st = jnp.dot(k_ref[kv, pl.ds(off, keys), :], q_ref[a], preferred_element_type=F32)
                if masked:
                    st = _causal_mask(st, i * tq, off, window)
                m_new = jnp.maximum(m, jnp.max(st, axis=0, keepdims=True))
                pt = jnp.exp(st - m_new).astype(BF16)
                acc = jnp.exp(m - m_new) * acc + jnp.dot(v_ref[kv, :, pl.ds(off, keys)], pt, preferred_element_type=F32)
                out += [m_new, acc]
            return tuple(out)

        carry = tuple(carry)
        if window is None:
            carry = lax.fori_loop(0, i, functools.partial(step, masked=False), carry)
            carry = step(i, carry, True)
        else:
            assert window % LANES == 0 and tq + window <= S
            carry = step(None, carry, True, off=pl.multiple_of(jnp.maximum(i * tq - window, 0), LANES), keys=tq + window)
        for a in range(hb):
            m, acc = carry[2 * a], carry[2 * a + 1]
            l = acc[dv:dv + 1, :]
            o_ref[a] = acc[:dv, :] / l
            lse_ref[a] = m + jnp.log(l)
        finish_job()

    kv_idx = (lambda b: b) if G == 1 else (lambda b: (b * hb) // G)
    in_specs = [
        pl.BlockSpec((hb, dqk, tq), lambda b, i: (b, 0, i)),
        pl.BlockSpec((kvb, S, dqk), lambda b, i: (kv_idx(b), 0, 0)),
        pl.BlockSpec((kvb, dvp, S), lambda b, i: (kv_idx(b), 0, 0)),
    ]
    args = [qT, k, vT1]
    if sink is not None:
        in_specs += [pl.BlockSpec((hb, 1, LANES), lambda b, i: (b, 0, 0))]
        args += [sink]
    return _job_call(
        job, body, name=name, grid=grid, in_specs=in_specs,
        out_specs=[pl.BlockSpec((hb, dv, tq), lambda b, i: (b, 0, i)), pl.BlockSpec((hb, 1, tq), lambda b, i: (b, 0, i))],
        out_shape=[jax.ShapeDtypeStruct((H, dv, S), F32), jax.ShapeDtypeStruct((H, 1, S), F32)],
        args=args, scratch_shapes=[], aliases={}, dimension_semantics=("arbitrary", "arbitrary") if job is not None else ("parallel", "parallel"))


def _attn_bwd(q, qT, k, kT, v, oT, do, doT, lse, *, tile, hb, window=None, sink=None, real=None, extra=False, full=False, name):
    H, S, dqk = q.shape
    G = H // k.shape[0]
    dv = v.shape[2]
    tq = tk = tile
    nq = S // tq
    has_p = sink is not None
    real = dqk if real is None else real
    main = dqk if full else real
    assert H % hb == 0 and (G == 1 or G % hb == 0) and not (extra and real == dqk)
    kvb = hb if G == 1 else 1

    def body(*refs):
        q_ref, qT_ref, k_ref, kT_ref, v_ref, oT_ref, do_ref, doT_ref, lse_ref = refs[:9]
        p_ref = refs[9] if has_p else None
        pos = 10 if has_p else 9
        dq_ref, dk_ref, dv_ref = refs[pos: pos + 3]
        pos += 3
        ds_ref = refs[pos] if has_p else None
        pos += has_p
        dqx_ref, dkx_ref = (refs[pos], refs[pos + 1]) if extra else (None, None)
        delta = refs[-1]
        j = pl.program_id(1)

        @pl.when(j == 0)
        def _():
            dq_ref[...] = jnp.zeros_like(dq_ref)
            if extra:
                dqx_ref[...] = jnp.zeros_like(dqx_ref)
            for a in range(hb):
                drow = jnp.sum(doT_ref[a].astype(F32) * oT_ref[a], axis=0, keepdims=True)
                delta[a] = drow
                if has_p:
                    w = jnp.exp(p_ref[a, :, 0:1] - lse_ref[a])
                    ds_ref[a] = jnp.zeros((1, LANES), F32) - jnp.sum(w * drow, axis=1, keepdims=True)

        def step(i, carry, masked, off=None, qs=tq):
            off = pl.multiple_of(i * tq, tq) if off is None else off
            out = []
            for a in range(hb):
                dk, dvv = carry[2 * a], carry[2 * a + 1]
                kv = a if kvb > 1 else 0
                st = jnp.dot(k_ref[kv], qT_ref[a, :, pl.ds(off, qs)], preferred_element_type=F32)
                if masked:
                    st = _causal_mask(st, off, j * tk, window)
                pt = jnp.exp(st - lse_ref[a, :, pl.ds(off, qs)])
                dvv = dvv + jnp.dot(pt.astype(BF16), do_ref[a, pl.ds(off, qs), :], preferred_element_type=F32)
                dpt = jnp.dot(v_ref[kv], doT_ref[a, :, pl.ds(off, qs)], preferred_element_type=F32)
                dsb = (pt * (dpt - delta[a, :, pl.ds(off, qs)])).astype(BF16)
                dk = dk + jnp.dot(dsb, q_ref[a, pl.ds(off, qs), :], preferred_element_type=F32)
                dqt = jnp.dot(kT_ref[kv], dsb, preferred_element_type=F32)
                dq_ref[a, :, pl.ds(off, qs)] += dqt[:main]
                if extra:
                    dqx_ref[a, :, pl.ds(off, qs)] += dqt[real:]
                out += [dk, dvv]
            return tuple(out)

        carry = (jnp.zeros((tk, dqk), F32), jnp.zeros((tk, dv), F32)) * hb
        if window is None:
            carry = step(j, carry, True)
            carry = lax.fori_loop(j + 1, nq, functools.partial(step, masked=False), carry)
        else:
            assert window % LANES == 0 and tk + window <= S
            carry = step(None, carry, True, off=pl.multiple_of(jnp.minimum(j * tk, S - (tk + window)), LANES), qs=tk + window)
        for a in range(hb):
            dk_ref[a] = carry[2 * a][:, :main]
            if extra:
                dkx_ref[a] = carry[2 * a][:, real:]
            dv_ref[a] = carry[2 * a + 1]

    kv_idx = (lambda b: b) if G == 1 else (lambda b: (b * hb) // G)
    rows = lambda d: pl.BlockSpec((hb, S, d), lambda b, j: (b, 0, 0))
    colsT = lambda d: pl.BlockSpec((hb, d, S), lambda b, j: (b, 0, 0))
    in_specs = [
        rows(dqk), colsT(dqk),
        pl.BlockSpec((kvb, tk, dqk), lambda b, j: (kv_idx(b), j, 0)),
        pl.BlockSpec((kvb, dqk, tk), lambda b, j: (kv_idx(b), 0, j)),
        pl.BlockSpec((kvb, tk, dv), lambda b, j: (kv_idx(b), j, 0)),
        colsT(dv), rows(dv), colsT(dv),
        pl.BlockSpec((hb, 1, S), lambda b, j: (b, 0, 0)),
    ]
    args = [q, qT, k, kT, v, oT, do, doT, lse]
    if has_p:
        in_specs += [pl.BlockSpec((hb, 1, LANES), lambda b, j: (b, 0, 0))]
        args += [sink]
    out_specs = [colsT(main), pl.BlockSpec((hb, tk, main), lambda b, j: (b, j, 0)), pl.BlockSpec((hb, tk, dv), lambda b, j: (b, j, 0))]
    out_shape = [jax.ShapeDtypeStruct((H, main, S), F32), jax.ShapeDtypeStruct((H, S, main), F32), jax.ShapeDtypeStruct((H, S, dv), F32)]
    if has_p:
        out_specs += [pl.BlockSpec((hb, 1, LANES), lambda b, j: (b, 0, 0))]
        out_shape += [jax.ShapeDtypeStruct((H, 1, LANES), F32)]
    if extra:
        out_specs += [colsT(dqk - real), pl.BlockSpec((hb, tk, dqk - real), lambda b, j: (b, j, 0))]
        out_shape += [jax.ShapeDtypeStruct((H, dqk - real, S), F32), jax.ShapeDtypeStruct((H, S, dqk - real), F32)]
    return pl.pallas_call(
        body, name=name, grid=(H // hb, S // tk), in_specs=in_specs, out_specs=out_specs, out_shape=out_shape,
        scratch_shapes=[pltpu.VMEM((hb, 1, S), F32)],
        compiler_params=pltpu.CompilerParams(dimension_semantics=("parallel", "arbitrary")),
    )(*args)


def _rows_and_cols(x3):
    xb = x3.astype(BF16)
    return jnp.transpose(xb, (1, 0, 2)), jnp.transpose(xb, (1, 2, 0))


def _v_with_ones(v3):
    S, h, _ = v3.shape
    vT = jnp.transpose(v3.astype(BF16), (1, 2, 0))
    return jnp.concatenate([vT, jnp.ones((h, 1, S), BF16), jnp.zeros((h, 15, S), BF16)], axis=1)


def _from_T(oT):
    h, d, S = oT.shape
    return jnp.transpose(oT, (2, 0, 1)).reshape(S, h * d)


def _coords():
    return lax.axis_index("x"), lax.axis_index("y"), lax.axis_index("c")


def _peer(axis):
    x, y, c = _coords()
    return {"x": (1 - x, y, c), "y": (x, 1 - y, c), "c": (x, y, 1 - c)}[axis]


def _gather_job(bufs, rows=None):
    n = len(bufs)

    def copies(outs, send_sems, recv_sems):
        x, y, c = _coords()
        me, sibling = (x, y, c), (x, y, 1 - c)
        chips = [(1 - x, y), (x, 1 - y), (1 - x, 1 - y)]

        def copy(t, k, block, to):
            px, py, pc = block
            ref = outs[t].at[4 * px + 2 * py + pc]
            if rows is not None and rows[t] is not None:
                ref = ref.at[pl.ds(rows[t][0], rows[t][1])]
            return pltpu.make_async_remote_copy(ref, ref, send_sems.at[7 * t + k], recv_sems.at[7 * t + k], device_id=to, device_id_type=MESH)

        return copy, me, sibling, chips, c

    def start(ins, outs, send_sems, recv_sems):
        copy, me, sibling, chips, c = copies(outs, send_sems, recv_sems)
        for t in range(n):
            copy(t, 0, me, sibling).start()
            for j, chip in enumerate(chips):
                copy(t, 1 + j, me, (*chip, c)).start()

    def finish(ins, outs, send_sems, recv_sems):
        copy, me, sibling, chips, c = copies(outs, send_sems, recv_sems)
        for j, chip in enumerate(chips):
            for t in range(n):
                copy(t, 1 + j, (*chip, c), me).wait_recv()
                copy(t, 4 + j, (*chip, c), sibling).start()
        for t in range(n):
            copy(t, 0, sibling, me).wait_recv()
            for j, chip in enumerate(chips):
                copy(t, 4 + j, (*chip, 1 - c), me).wait_recv()
        for t in range(n):
            copy(t, 0, me, sibling).wait_send()
            for j, chip in enumerate(chips):
                copy(t, 1 + j, me, (*chip, c)).wait_send()
                copy(t, 4 + j, (*chip, c), sibling).wait_send()

    return dict(ins=list(bufs), outs=[jax.ShapeDtypeStruct(b.shape, b.dtype) for b in bufs], aliases={t: t for t in range(n)},
                n_sems=7 * n, start=start, finish=finish)


def _in_slot(local):
    x, y, c = _coords()
    buf = lax.empty((N_DEV,) + local.shape, local.dtype)
    return lax.dynamic_update_slice(buf, local[None], (4 * x + 2 * y + c, 0, 0))


def _pair_job(vs, axes):
    n = len(vs)
    axes = [axes] * n if isinstance(axes, str) else axes

    def copies(ins, outs, send_sems, recv_sems):
        out = []
        for t in range(n):
            me = lax.axis_index(axes[t])
            src = ins[t].at[1 - me] if len(ins[t].shape) == 3 else ins[t].at[:, 1 - me]
            out.append(pltpu.make_async_remote_copy(src, outs[t], send_sems.at[t], recv_sems.at[t], device_id=_peer(axes[t]), device_id_type=MESH))
        return out

    def start(*refs):
        for cp in copies(*refs):
            cp.start()

    def finish(*refs):
        for cp in copies(*refs):
            cp.wait()

    return dict(ins=list(vs), outs=[jax.ShapeDtypeStruct(v.shape[:-3] + v.shape[-2:], v.dtype) for v in vs], aliases={}, n_sems=n,
                start=start, finish=finish)


def _add_kept(v, got, axis, out, name):
    R, C = v.shape[-2:]
    lead = v.shape[0] if v.ndim == 4 else 1
    tm = _divisor(R, max(16, EW_TILE_BYTES // (_lanes(C) * (v.dtype.itemsize + got.dtype.itemsize + jnp.dtype(out).itemsize)) // 16 * 16), 16)
    me = lax.axis_index(axis).astype(jnp.int32).reshape(1)
    v4 = v.reshape(lead, 2, R, C)
    g3 = got.reshape(lead, R, C)

    def body(me_ref, v_ref, g_ref, o_ref):
        o_ref[...] = (v_ref[0].astype(F32) + g_ref[...].astype(F32)).astype(o_ref.dtype)

    res = pl.pallas_call(
        body, name=name, out_shape=jax.ShapeDtypeStruct((lead, R, C), out),
        grid_spec=pltpu.PrefetchScalarGridSpec(
            num_scalar_prefetch=1, grid=(lead, R // tm),
            in_specs=[pl.BlockSpec((1, 1, tm, C), lambda b, i, me: (b, me[0], i, 0)), pl.BlockSpec((1, tm, C), lambda b, i, me: (b, i, 0))],
            out_specs=pl.BlockSpec((1, tm, C), lambda b, i, me: (b, i, 0))),
    )(me, v4, g3)
    return res


def _cross_job(vs):
    n = len(vs)

    def copies(ins, outs, send_sems, recv_sems):
        x, y, _ = _coords()
        out = []
        for t in range(n):
            h = ins[t].shape[2] // 2
            out.append(pltpu.make_async_remote_copy(ins[t].at[1 - x, :, pl.ds(0, h)], outs[2 * t], send_sems.at[2 * t], recv_sems.at[2 * t],
                                                    device_id=_peer("x"), device_id_type=MESH))
            out.append(pltpu.make_async_remote_copy(ins[t].at[:, 1 - y, pl.ds(h, h)], outs[2 * t + 1], send_sems.at[2 * t + 1], recv_sems.at[2 * t + 1],
                                                    device_id=_peer("y"), device_id_type=MESH))
        return out

    def start(*refs):
        for cp in copies(*refs):
            cp.start()

    def finish(*refs):
        for cp in copies(*refs):
            cp.wait()

    outs = []
    for v in vs:
        outs += [jax.ShapeDtypeStruct((2, v.shape[2] // 2, v.shape[3]), v.dtype)] * 2
    return dict(ins=list(vs), outs=outs, aliases={}, n_sems=2 * n, start=start, finish=finish)


def _add_picked(v, got, axis, out, name):
    _, _, R, C = v.shape
    h = R // 2
    tm = _divisor(h, max(16, EW_TILE_BYTES // (_lanes(C) * (v.dtype.itemsize + got.dtype.itemsize + jnp.dtype(out).itemsize)) // 16 * 16), 16)
    me = lax.axis_index(axis).astype(jnp.int32).reshape(1)
    if axis == "x":
        v_map = lambda b, i, me: (me[0], b, i, 0)
    else:
        v_map = lambda b, i, me: (b, me[0], i + h // tm, 0)

    def body(me_ref, v_ref, g_ref, o_ref):
        o_ref[...] = (v_ref[0].astype(F32) + g_ref[...].astype(F32)).astype(o_ref.dtype)

    return pl.pallas_call(
        body, name=name, out_shape=jax.ShapeDtypeStruct((2, h, C), out),
        grid_spec=pltpu.PrefetchScalarGridSpec(
            num_scalar_prefetch=1, grid=(2, h // tm),
            in_specs=[pl.BlockSpec((1, 1, tm, C), v_map), pl.BlockSpec((1, tm, C), lambda b, i, me: (b, i, 0))],
            out_specs=pl.BlockSpec((1, tm, C), lambda b, i, me: (b, i, 0))),
    )(me, v, got)


def _reduce_scatter_steps(gs, tag):
    n = len(gs)
    vs = [g.reshape(4, 2, *g.shape[1:]) for g in gs]
    got = yield _pair_job(vs, "c")
    vs = [_add_kept(v, r, "c", BF16, f"rs_{tag}_add_c{t}") for t, (v, r) in enumerate(zip(vs, got))]
    vs = [v.reshape(2, 2, v.shape[1], v.shape[2]) for v in vs]
    got = yield _cross_job(vs)
    up = [_add_picked(v, r, "x", BF16, f"rs_{tag}_add_x{t}") for t, (v, r) in enumerate(zip(vs, got[0::2]))]
    lo = [_add_picked(v, r, "y", BF16, f"rs_{tag}_add_y{t}") for t, (v, r) in enumerate(zip(vs, got[1::2]))]
    got = yield _pair_job(up + lo, ["y"] * n + ["x"] * n)
    out = []
    for t in range(n):
        a = _add_kept(up[t], got[t], "y", F32, f"rs_{tag}_add_y2{t}")[0]
        b = _add_kept(lo[t], got[n + t], "x", F32, f"rs_{tag}_add_x2{t}")[0]
        out.append(jnp.concatenate([a, b], axis=0))
    return out


def _reduce_scatter(gs, tag):
    steps = _reduce_scatter_steps(gs, tag)
    job = next(steps)
    for stage in ("c", "xy", "yx"):
        got = _comm_call(job, f"rs_{tag}_{stage}")
        try:
            job = steps.send(got)
        except StopIteration as done:
            return done.value


def _all_reduce_small(v):
    def body(v_ref, o_ref, buf, send_sems, recv_sems):
        x, y, c = _coords()
        me = 4 * x + 2 * y + c
        buf[me] = v_ref[...]
        copies = []
        for k in range(1, N_DEV):
            peer = tuple((1 - a) if (k >> s) & 1 else a for a, s in ((x, 2), (y, 1), (c, 0)))
            cp = pltpu.make_async_remote_copy(v_ref, buf.at[me], send_sems.at[k - 1], recv_sems.at[k - 1], device_id=peer, device_id_type=MESH)
            cp.start()
            copies.append(cp)
        for cp in copies:
            cp.wait()
        acc = buf[0]
        for d in range(1, N_DEV):
            acc = acc + buf[d]
        o_ref[...] = acc

    vm = pl.BlockSpec(memory_space=pltpu.VMEM)
    return pl.pallas_call(
        body, name="all_reduce_small", in_specs=[vm], out_specs=vm, out_shape=jax.ShapeDtypeStruct(v.shape, F32),
        scratch_shapes=[pltpu.VMEM((N_DEV,) + v.shape, F32), pltpu.SemaphoreType.DMA((N_DEV - 1,)), pltpu.SemaphoreType.DMA((N_DEV - 1,))],
    )(v)


def _local_groups(w, dtype):
    mix_out = [w["ev_w_out"][0], w["od_w_out"][0]]
    layers = []
    for l in range(DEPTH):
        a = jnp.concatenate([w["ffa_w_down"][l], w["ffb_w_down"][l]], axis=0).astype(dtype)
        b = jnp.concatenate([w["ple_w_gate"][l], mix_out[l]], axis=0).astype(dtype)
        c = jnp.concatenate([w["ffa_w_gate_up"][l], w["ffb_w_gate_up"][l]], axis=0).astype(dtype)
        layers.append((a, b, c))
    strip = jnp.concatenate([w["ple_w_proj"].reshape(-1, STRIP_C), w["ev_w_ukv"][0], jnp.pad(w["ev_w_uq"][0], ((0, 0), (0, STRIP_C - 96))),
                             jnp.zeros((G3_ROWS - 896, STRIP_C), F32)], axis=0)
    m = jnp.concatenate([w["od_w_in"][0], w["ev_w_in"][0], strip, jnp.zeros((G3_ROWS, G3_COLS - STRIP0 - STRIP_C), F32)], axis=1).astype(dtype)
    return layers, m


def _ungroup_local(a, b, c, r3):
    out = {
        "ffa_w_down": jnp.stack([x[0] for x in a]), "ffb_w_down": jnp.stack([x[1] for x in a]),
        "ple_w_gate": jnp.stack([x[:128] for x in b]), "ev_w_out": b[0][128:][None], "od_w_out": b[1][128:][None],
        "ffa_w_gate_up": jnp.stack([x[0] for x in c]), "ffb_w_gate_up": jnp.stack([x[1] for x in c]),
        "od_w_in": r3[:, :OD_C][None], "ev_w_in": r3[:, OD_C:STRIP0][None],
    }
    strip = r3[:, STRIP0:STRIP0 + STRIP_C]
    out["ple_w_proj"] = strip[:512].reshape(2, PLE_DIM, STRIP_C)
    out["ev_w_ukv"] = strip[512:640][None]
    out["ev_w_uq"] = strip[640:896, :96][None]
    return out


def _cols(a):
    return jnp.transpose(a, (1, 0, 2)).reshape(a.shape[1], -1)


def _blocks(g, c):
    return jnp.transpose(g.reshape(g.shape[0], N_DEV, c), (1, 0, 2))


def _uq_permute(w):
    r = w.shape[0]
    w3 = w.reshape(r, B_HEADS, B_NOPE + B_ROPE)
    half = B_ROPE // 2
    return jnp.concatenate([w3[:, :, :B_NOPE].reshape(r, -1), w3[:, :, B_NOPE:B_NOPE + half].reshape(r, -1), w3[:, :, B_NOPE + half:].reshape(r, -1)], axis=1)


def _uq_unpermute(g):
    r = g.shape[0]
    half = B_ROPE // 2
    n = B_HEADS * B_NOPE
    parts = [g[:, :n].reshape(r, B_HEADS, B_NOPE), g[:, n:n + B_HEADS * half].reshape(r, B_HEADS, half), g[:, n + B_HEADS * half:].reshape(r, B_HEADS, half)]
    return jnp.concatenate(parts, axis=2).reshape(r, -1)


def _ukv_permute(w):
    r = w.shape[0]
    return jnp.transpose(w.reshape(r, B_HEADS, 2, B_NOPE), (0, 2, 1, 3)).reshape(r, -1)


def _ukv_unpermute(g):
    r = g.shape[0]
    return jnp.transpose(g.reshape(r, 2, B_HEADS, B_NOPE), (0, 2, 1, 3)).reshape(r, -1)


def _od_in_widen(w):
    n = C_HEADS * C_HEAD_DIM
    wide = lambda m: jnp.pad(m.reshape(-1, C_HEADS, C_HEAD_DIM), ((0, 0), (0, 0), (0, QK_PAD - C_HEAD_DIM))).reshape(m.shape[0], -1)
    return jnp.concatenate([wide(w[:, :n] * C_HEAD_DIM ** -0.5), wide(w[:, n:2 * n]), w[:, 2 * n:],
                            jnp.zeros((w.shape[0], ODD_IN_PAD - ODD_IN_AUG), w.dtype)], axis=1)


def _od_in_narrow(g):
    wp = C_HEADS * QK_PAD
    narrow = lambda m: m.reshape(-1, C_HEADS, QK_PAD)[:, :, :C_HEAD_DIM].reshape(m.shape[0], -1)
    return jnp.concatenate([narrow(g[:, :wp]) * C_HEAD_DIM ** -0.5, narrow(g[:, wp:2 * wp]), g[:, 2 * wp:ODD_IN_AUG]], axis=1)


def _misc_weights(G3):
    strip = G3[:, :, STRIP0:STRIP0 + STRIP_C]
    return {
        "od_w_in": _od_in_widen(_cols(G3[:, :, :OD_C])),
        "ev_w_in": jnp.pad(_cols(G3[:, :, OD_C:STRIP0]), ((0, 0), (0, EVEN_IN_PAD - EVEN_IN))),
        "ple_w_proj": [_cols(strip[:, i * PLE_DIM:(i + 1) * PLE_DIM]) for i in range(DEPTH)],
        "ev_w_ukv": _ukv_permute(_cols(strip[:, 512:640])),
        "ev_w_uq": _uq_permute(_cols(strip[:, 640:896, :96])),
    }


def _misc_grads(G):
    strip = jnp.concatenate([
        _blocks(G["ple_w_proj"][0], STRIP_C), _blocks(G["ple_w_proj"][1], STRIP_C), _blocks(_ukv_unpermute(G["ev_w_ukv"]), STRIP_C),
        jnp.pad(_blocks(_uq_unpermute(G["ev_w_uq"]), 96), ((0, 0), (0, 0), (0, STRIP_C - 96))),
        jnp.zeros((N_DEV, G3_ROWS - 896, STRIP_C), F32)], axis=1)
    return jnp.concatenate([_blocks(_od_in_narrow(G["od_w_in"]), OD_C), _blocks(G["ev_w_in"][:, :EVEN_IN], EV_C), strip,
                            jnp.zeros((N_DEV, G3_ROWS, G3_COLS - STRIP0 - STRIP_C), F32)], axis=2)


def _ffn_fwd(h, norm_w, W, f, i, tag, ride=None):
    job = ride() if ride else None
    res = _ffn_gate_up(h, norm_w, W["C"][i].reshape(2, 4, C_ROWS, FF_BLK), f, f"{tag}_gate_up", job=job)
    n, gu, act = res[:3]
    if job is not None:
        ride(res[3:])
    job = ride() if ride else None
    out = _ffn_down(act, W["A"][i], f, h, f"{tag}_down", job=job)
    if job is not None:
        out, got = out
        ride(got)
    return out, (h, n, gu, act)


def _ffn_bwd(dout, saved, norm_w, W, GB, f, i, tag, ride=None):
    h, n, gu, act = saved
    S = h.shape[0]
    def carried(call):
        job = ride() if ride else None
        res = call(job)
        if job is None:
            return res
        ride(res[1])
        return res[0]

    GB["A"][i][f] = carried(lambda job: _ffn_down_dw(act, dout, f"{tag}_down_dw", job=job))
    dgu = _ffn_down_dx(dout, W["A"][i], f, gu, f"{tag}_down_dx").reshape(N_DEV, S, FF_BLK)
    res = carried(lambda job: _ffn_gate_up_dx(dgu, W["C"][i], f, h, norm_w, dout, f"{tag}_gate_up_dx", job=job))
    GB["C"][i][f] = carried(lambda job: _ffn_gate_up_dw(n, dgu, f"{tag}_gate_up_dw", job=job))
    return res


def _rope_tables(S):
    inv = ROPE_THETA ** (-jnp.arange(0, B_ROPE, 2, dtype=F32) / B_ROPE)
    ang = jnp.arange(S, dtype=F32)[:, None] * inv[None, :]
    return jnp.cos(ang), jnp.sin(ang)


def _alibi_columns(S):
    t = jnp.arange(S, dtype=jnp.int32)
    hi = ((t // 16) * 16).astype(F32)
    lo = (t % 16).astype(F32)
    slopes = 2.0 ** (-8.0 * jnp.arange(1, A_HEADS + 1, dtype=F32) / A_HEADS)
    zq = jnp.zeros((S, A_HEADS), F32)
    rest = QK_PAD - A_HEAD_DIM - 4
    qc = jnp.stack([-slopes[None, :] * hi[:, None], -slopes[None, :] * lo[:, None], zq + slopes[None, :], zq + slopes[None, :]] + [zq] * rest, axis=-1)
    one = jnp.ones((S, A_KV_HEADS), F32)
    zk = jnp.zeros((S, A_KV_HEADS), F32)
    kc = jnp.stack([one, one, zk + hi[:, None], zk + lo[:, None]] + [zk] * rest, axis=-1)
    return qc, kc


def _sink_prm(sinks):
    return jnp.zeros((A_HEADS, 1, LANES), F32).at[:, 0, 0].set(sinks.astype(F32))


def _with_ride(ride, call):
    job = ride() if ride else None
    res = call(job)
    if job is None:
        return res
    n_own = len(res) - len(job["outs"])
    ride(res[n_own:])
    return res[:n_own]


def _even_fwd(hn, h, W, ride=None):
    S = hn.shape[0]
    proj = _mm(hn, W["ev_w_in"], name="ev_in")
    a_q, a_k, a_v = proj[:, :512], proj[:, 512:640], proj[:, 640:768]
    c_q, c_kv = proj[:, 768:1024], proj[:, 1024:1152]
    kr1, kr2 = proj[:, 1152:1168], proj[:, 1168:1184]
    qc, kc = _alibi_columns(S)
    qa, qaT = _rows_and_cols(jnp.concatenate([(a_q * A_HEAD_DIM ** -0.5).reshape(S, A_HEADS, A_HEAD_DIM), qc], axis=-1))
    ka, kaT = _rows_and_cols(jnp.concatenate([a_k.reshape(S, A_KV_HEADS, A_HEAD_DIM), kc], axis=-1))
    va3 = a_v.reshape(S, A_KV_HEADS, A_HEAD_DIM)
    va = jnp.transpose(va3.astype(BF16), (1, 0, 2))
    prm = _sink_prm(W["ev_sinks"][0])
    oaT, lse_a = _with_ride(ride, lambda job: _attn_fwd(qaT, ka, _v_with_ones(va3), tile=SWA_TILE, hb=2, window=WINDOW, sink=prm,
                                                        name="swa_fwd", job=job))
    cqn = _rms_fwd(c_q, W["ev_cq_norm"], "ev_cq_norm")
    q_all = _mm(cqn, W["ev_w_uq"], name="ev_uq")
    ckvn = _rms_fwd(c_kv, W["ev_ckv_norm"], "ev_ckv_norm")
    kv_all = _mm(ckvn, W["ev_w_ukv"], name="ev_ukv")
    cos, sin = _rope_tables(S)
    cos8, sin8 = jnp.tile(cos, (1, B_HEADS)), jnp.tile(sin, (1, B_HEADS))
    q1, q2 = _rope(q_all[:, 512:640], q_all[:, 640:768], cos8, sin8, "ev_rope_q")
    k1, k2 = _rope(kr1, kr2, cos, sin, "ev_rope_k")
    half = B_ROPE // 2
    scale = (B_NOPE + B_ROPE) ** -0.5
    qb, qbT = _rows_and_cols(jnp.concatenate([q_all[:, :512].reshape(S, B_HEADS, B_NOPE), q1.reshape(S, B_HEADS, half), q2.reshape(S, B_HEADS, half)], axis=-1) * scale)
    kro = jnp.broadcast_to(jnp.concatenate([k1, k2], axis=1)[:, None, :], (S, B_HEADS, B_ROPE))
    kb, kbT = _rows_and_cols(jnp.concatenate([kv_all[:, :512].reshape(S, B_HEADS, B_NOPE), kro], axis=-1))
    vb3 = kv_all[:, 512:].reshape(S, B_HEADS, B_V)
    vb = jnp.transpose(vb3.astype(BF16), (1, 0, 2))
    obT, lse_b = _with_ride(ride, lambda job: _attn_fwd(qbT, kb, _v_with_ones(vb3), tile=min(ATTN_TILE_FWD, S), hb=2, name="mla_fwd", job=job))
    cat = jnp.concatenate([_from_T(oaT), _from_T(obT)], axis=1)
    out = _mm_w128(cat, W["B"][0], MIX_OUT_BLK, res=h, name="ev_out")
    return out, (hn, proj, (qa, qaT, ka, kaT, va, oaT, lse_a), prm, cqn, ckvn, (qb, qbT, kb, kbT, vb, obT, lse_b), cat)


def _even_bwd(dout, saved, W, GB, norm):
    hn, proj, (qa, qaT, ka, kaT, va, oaT, lse_a), prm, cqn, ckvn, (qb, qbT, kb, kbT, vb, obT, lse_b), cat = saved
    S = hn.shape[0]
    G = {}
    dcat = _mm_w128(dout, W["B"][0], MIX_OUT_BLK, tb=True, out=BF16, name="ev_out_dx")
    GB["B"][0] = _mm_w128_dw(cat, dout, MIX_OUT_BLK, GB["B"][0], "ev_out_dw")
    doa, doaT = _rows_and_cols(dcat[:, :512].reshape(S, A_HEADS, A_HEAD_DIM))
    dqaT, dka, dva, dsink = _attn_bwd(qa, qaT, ka, kaT, va, oaT, doa, doaT, lse_a, tile=SWA_TILE, hb=2, window=WINDOW, sink=prm, real=A_HEAD_DIM,
                                       name="swa_bwd")
    G["ev_sinks"] = dsink[:, 0, 0]
    dqa = _from_T(dqaT) * A_HEAD_DIM ** -0.5
    dka = dka.reshape(A_KV_HEADS, A_GROUP, S, A_HEAD_DIM).sum(axis=1)
    dva = dva.reshape(A_KV_HEADS, A_GROUP, S, A_HEAD_DIM).sum(axis=1)
    dob, dobT = _rows_and_cols(dcat[:, 512:].reshape(S, B_HEADS, B_V))
    dqbT, dkb, dvb = _attn_bwd(qb, qbT, kb, kbT, vb, obT, dob, dobT, lse_b, tile=min(ATTN_TILE, S), hb=1, name="mla_bwd")
    half = B_ROPE // 2
    dqb = jnp.transpose(dqbT, (2, 0, 1)) * (B_NOPE + B_ROPE) ** -0.5
    dkb = jnp.transpose(dkb, (1, 0, 2))
    cos, sin = _rope_tables(S)
    cos8, sin8 = jnp.tile(cos, (1, B_HEADS)), jnp.tile(sin, (1, B_HEADS))
    dq1, dq2 = _rope(dqb[:, :, B_NOPE:B_NOPE + half].reshape(S, -1), dqb[:, :, B_NOPE + half:].reshape(S, -1), cos8, -sin8, "ev_rope_q_bwd")
    dq_all = jnp.concatenate([dqb[:, :, :B_NOPE].reshape(S, -1), dq1, dq2], axis=1).astype(BF16)
    dkr = dkb[:, :, B_NOPE:].sum(axis=1)
    dk1, dk2 = _rope(dkr[:, :half], dkr[:, half:], cos, -sin, "ev_rope_k_bwd")
    dkv_all = jnp.concatenate([dkb[:, :, :B_NOPE].reshape(S, -1), _unheads(dvb)], axis=1).astype(BF16)
    G["ev_w_uq"] = _mm(cqn, dq_all, ta=True, name="ev_uq_dw")
    dcqn = _mm(dq_all, W["ev_w_uq"], tb=True, name="ev_uq_dx")
    dc_q, G["ev_cq_norm"] = _rms_bwd(dcqn, proj[:, 768:1024], W["ev_cq_norm"], None, "ev_cq_norm_bwd")
    G["ev_w_ukv"] = _mm(ckvn, dkv_all, ta=True, name="ev_ukv_dw")
    dckvn = _mm(dkv_all, W["ev_w_ukv"], tb=True, name="ev_ukv_dx")
    dc_kv, G["ev_ckv_norm"] = _rms_bwd(dckvn, proj[:, 1024:1152], W["ev_ckv_norm"], None, "ev_ckv_norm_bwd")
    dproj = jnp.concatenate([dqa, _unheads(dka), _unheads(dva), dc_q, dc_kv, dk1, dk2,
                             jnp.zeros((S, EVEN_IN_PAD - EVEN_IN), F32)], axis=1).astype(BF16)
    G["ev_w_in"] = _mm(hn, dproj, ta=True, name="ev_in_dw")
    dh, dnorm = _mm(dproj, W["ev_w_in"], tb=True, norm_bwd=(*norm, dout), name="ev_in_dx")
    return dh, dnorm, G


def _odd_fwd(hn, h, W, ride=None):
    S = hn.shape[0]
    w = C_HEADS * C_HEAD_DIM
    wp = C_HEADS * QK_PAD
    proj = _mm(hn, W["od_w_in"], name="od_in")
    f_logit = proj[:, 2 * wp + w: 2 * wp + w + C_HEADS]
    logf = _logsig_fwd(f_logit, W["od_b_f"], "od_logsig")
    logc = _cumsum(logf, False, "od_cumsum")
    parts = list(_exact3(logc))
    ones = [jnp.ones((S, C_HEADS), F32)] * 3
    pad = [jnp.zeros((S, C_HEADS), F32)] * (QK_PAD - C_HEAD_DIM - 6)
    lead = ((0, 0), (0, 0), (C_HEAD_DIM, 0))
    q3 = proj[:, :wp].reshape(S, C_HEADS, QK_PAD) + jnp.pad(jnp.stack(parts + ones + pad, axis=-1), lead)
    k3 = proj[:, wp:2 * wp].reshape(S, C_HEADS, QK_PAD) + jnp.pad(jnp.stack(ones + [-p for p in parts] + pad, axis=-1), lead)
    q, qT = _rows_and_cols(q3)
    k, kT = _rows_and_cols(k3)
    v3 = proj[:, 2 * wp:2 * wp + w].reshape(S, C_HEADS, C_HEAD_DIM)
    v = jnp.transpose(v3.astype(BF16), (1, 0, 2))
    oT, lse = _with_ride(ride, lambda job: _attn_fwd(qT, k, _v_with_ones(v3), tile=min(ATTN_TILE_FWD, S), hb=2, name="fox_fwd", job=job))
    cat = _from_T(oT)
    out = _mm_w128(cat, W["B"][1], MIX_OUT_BLK, res=h, name="od_out")
    return out, (hn, q, qT, k, kT, v, f_logit, oT, lse, cat)


def _odd_bwd(dout, saved, W, GB, norm):
    hn, q, qT, k, kT, v, f_logit, oT, lse, cat = saved
    S = hn.shape[0]
    G = {}
    dcat = _mm_w128(dout, W["B"][1], MIX_OUT_BLK, tb=True, out=BF16, name="od_out_dx")
    GB["B"][1] = _mm_w128_dw(cat, dout, MIX_OUT_BLK, GB["B"][1], "od_out_dw")
    do, doT = _rows_and_cols(dcat.reshape(S, C_HEADS, C_HEAD_DIM))
    dqT, dk, dv, dqxT, dkx = _attn_bwd(q, qT, k, kT, v, oT, do, doT, lse, tile=min(ATTN_TILE, S), hb=1, real=C_HEAD_DIM, extra=True,
                                       full=True, name="fox_bwd")
    dlogc = jnp.transpose(dqxT[:, 0, :] - dkx[:, :, 3])
    dlogf = _cumsum(dlogc, True, "od_cumsum_bwd")
    df, db = _logsig_bwd(dlogf, f_logit, W["od_b_f"], "od_logsig_bwd")
    G["od_b_f"] = db
    dproj = jnp.concatenate([_from_T(dqT), _unheads(dk), _unheads(dv), df, jnp.zeros((S, ODD_IN_PAD - ODD_IN_AUG), F32)], axis=1).astype(BF16)
    G["od_w_in"] = _mm(hn, dproj, ta=True, name="od_in_dw")
    dh, dnorm = _mm(dproj, W["od_w_in"], tb=True, norm_bwd=(*norm, dout), name="od_in_dx")
    return dh, dnorm, G


class _Rider:
    def __init__(self, steps, tag):
        self.steps, self.tag, self.count, self.result = steps, tag, 0, None
        self.job = next(steps)

    def __call__(self, got=None):
        if got is not None:
            return self._advance(list(got))
        job = self.job
        if isinstance(job, str):
            self._advance(None)
            return None
        return job

    def _advance(self, value):
        try:
            self.job = self.steps.send(value)
        except StopIteration as done:
            self.job, self.result = None, done.value

    def finish(self):
        while self.job is not None:
            if isinstance(self.job, str):
                self._advance(None)
                continue
            self.count += 1
            self(_comm_call(self.job, f"{self.tag}_{self.count}"))
        return self.result


def _gather_plan(W, slots):
    a0, b0, c0, m, a1, b1, c1 = (slots[key] for key in ("a0", "b0", "c0", "m", "a1", "b1", "c1"))
    (m,) = yield _gather_job([m])
    W.update(_misc_weights(m))
    (b0,) = yield _gather_job([b0])
    W["B"] = [b0]
    (c0,) = yield _gather_job([c0], rows=[(D_MODEL, D_MODEL)])
    W["C"] = [c0]
    a0, c1 = yield _gather_job([a0, c1], rows=[(DOWN_ROWS, DOWN_ROWS), (0, D_MODEL)])
    W["A"] = [a0]
    W["C"].append(c1)
    (a1,) = yield _gather_job([a1], rows=[(0, DOWN_ROWS)])
    W["A"].append(a1)
    for _ in range(3):
        yield "skip"
    a1, b1, c1 = yield _gather_job([a1, b1, c1], rows=[(DOWN_ROWS, DOWN_ROWS), None, (D_MODEL, D_MODEL)])
    W["A"][1], W["C"][1] = a1, c1
    W["B"].append(b1)


def _local_step(x, p, target, W, slots):
    h = x
    saved = []
    gather = _Rider(_gather_plan(W, slots), "all_gather_rest")
    for i in range(DEPTH):
        t = f"l{i}"
        h1, s_a = _ffn_fwd(h, W["ffa_norm"][i:i + 1], W, 0, i, f"{t}_ffa", gather)
        nm = _rms_fwd(h1, W["mix_norm"][i:i + 1], f"{t}_mix_norm")
        h2, s_m = (_even_fwd if i % 2 == 0 else _odd_fwd)(nm, h1, W, gather)
        h3, s_b = _ffn_fwd(h2, W["ffb_norm"][i:i + 1], W, 1, i, f"{t}_ffb", gather)
        npl = _rms_fwd(h3, W["ple_norm"][i:i + 1], f"{t}_ple_norm")
        gpre = _mm_w128(npl, W["B"][i], PLE_GATE_BLK, name=f"{t}_ple_gate")
        pp = _mm(p[i], W["ple_w_proj"][i], name=f"{t}_ple_proj")
        h4 = _ple_fwd(h3, gpre, pp, f"{t}_ple")
        saved.append((s_a, h1, s_m, s_b, h3, npl, gpre, pp))
        h = h4
    gather.finish()
    dh, g_final, loss_cols = _final_fwd_bwd(h, W["final_norm"], target, "final")
    G = {"final_norm": g_final}
    GB = {"A": [[None, None] for _ in range(DEPTH)], "C": [[None, None] for _ in range(DEPTH)],
          "B": [lax.empty((N_DEV, B_ROWS, D_MODEL), BF16) for _ in range(DEPTH)]}
    per_layer = {n: [None] * DEPTH for n in ("ffa_norm", "mix_norm", "ffb_norm", "ple_norm", "ple_w_proj")}
    scatter = scatter_mid = None
    for i in reversed(range(DEPTH)):
        t = f"l{i}"
        s_a, h1, s_m, s_b, h3, npl, gpre, pp = saved[i]
        dgpre, dpp = _ple_bwd(dh, gpre, pp, f"{t}_ple_bwd")
        per_layer["ple_w_proj"][i] = _mm(p[i], dpp, ta=True, name=f"{t}_ple_proj_dw")
        GB["B"][i] = _mm_w128_dw(npl, dgpre, PLE_GATE_BLK, GB["B"][i], f"{t}_ple_gate_dw")
        dh, per_layer["ple_norm"][i] = _mm_w128(dgpre, W["B"][i], PLE_GATE_BLK, tb=True, norm_bwd=(h3, W["ple_norm"][i:i + 1], dh),
                                                name=f"{t}_ple_gate_dx")
        dh, per_layer["ffb_norm"][i] = _ffn_bwd(dh, s_b, W["ffb_norm"][i:i + 1], W, GB, 1, i, f"{t}_ffb", scatter)
        dh, per_layer["mix_norm"][i], g_mix = (_even_bwd if i % 2 == 0 else _odd_bwd)(dh, s_m, W, GB, (h1, W["mix_norm"][i:i + 1]))
        G.update(g_mix)
        if i == 0:
            G["ple_w_proj"] = per_layer["ple_w_proj"]
            mid = [GB["A"][0][1], GB["C"][0][1], GB["B"][0], _misc_grads(G).astype(BF16)]
            scatter_mid = _Rider(_reduce_scatter_steps(mid, "mid"), "rs_mid")
        dh, per_layer["ffa_norm"][i] = _ffn_bwd(dh, s_a, W["ffa_norm"][i:i + 1], W, GB, 0, i, f"{t}_ffa", scatter_mid)
        if i == DEPTH - 1:
            later = [GB["A"][i][0], GB["A"][i][1], GB["C"][i][0], GB["C"][i][1], GB["B"][i]]
            scatter = _Rider(_reduce_scatter_steps(later, "later"), "rs_later")
    for n in ("ffa_norm", "mix_norm", "ffb_norm", "ple_norm"):
        G[n] = jnp.concatenate(per_layer[n], axis=0)
    return loss_cols, dh, scatter.finish(), scatter_mid.finish(), [GB["A"][0][0], GB["C"][0][0]], G


def kernel(x, p, ffa_norm, ffa_w_gate_up, ffa_w_down, mix_norm, ffb_norm, ffb_w_gate_up, ffb_w_down, ple_norm, ple_w_gate, ple_w_proj, ev_w_in, ev_sinks, ev_cq_norm, ev_w_uq, ev_ckv_norm, ev_w_ukv, ev_w_out, od_w_in, od_b_f, od_w_out, final_norm, loss_target, m_ffa_norm, m_ffa_w_gate_up, m_ffa_w_down, m_mix_norm, m_ffb_norm, m_ffb_w_gate_up, m_ffb_w_down, m_ple_norm, m_ple_w_gate, m_ple_w_proj, m_ev_w_in, m_ev_sinks, m_ev_cq_norm, m_ev_w_uq, m_ev_ckv_norm, m_ev_w_ukv, m_ev_w_out, m_od_w_in, m_od_b_f, m_od_w_out, m_final_norm, v_ffa_norm, v_ffa_w_gate_up, v_ffa_w_down, v_mix_norm, v_ffb_norm, v_ffb_w_gate_up, v_ffb_w_down, v_ple_norm, v_ple_w_gate, v_ple_w_proj, v_ev_w_in, v_ev_sinks, v_ev_cq_norm, v_ev_w_uq, v_ev_ckv_norm, v_ev_w_ukv, v_ev_w_out, v_od_w_in, v_od_b_f, v_od_w_out, v_final_norm):
    given = dict(locals())
    w_in = {n: given[n] for n in WEIGHTS}

    layers, misc = _local_groups(w_in, BF16)
    (a0, b0, c0), (a1, b1, c1) = [[_in_slot(g) for g in layer] for layer in layers]
    a0, c0 = _comm_call(_gather_job([a0, c0], rows=[(0, DOWN_ROWS), (0, D_MODEL)]), "all_gather_first")
    W = {n: w_in[n] for n in SMALL}
    W["final_norm"] = final_norm.reshape(1, -1)
    W.update(A=[a0], C=[c0])
    slots = dict(a0=a0, b0=b0, c0=c0, m=_in_slot(misc), a1=a1, b1=b1, c1=c1)

    loss_cols, dx, r_later, r_mid, last, G = _local_step(x[0], p[:, 0], loss_target[0], W, slots)

    a1f, a1b, c1f, c1b, b1 = r_later
    a0b, c0b, b0, r_misc = r_mid
    a0f, c0f = _reduce_scatter(last, "last")
    grads = _ungroup_local([[a0f, a0b], [a1f, a1b]], [b0, b1], [[c0f, c0b], [c1f, c1b]], r_misc)
    layout = [(n, int(np.prod(w_in[n].shape))) for n in SMALL]
    vec = jnp.concatenate([G[n].astype(F32).reshape(-1) for n, _ in layout] + [jnp.sum(loss_cols).reshape(1)])
    vec = jnp.pad(vec, (0, N_DEV * SMALL_COLS - vec.shape[0])).reshape(N_DEV, SMALL_COLS)
    vec = _all_reduce_small(vec).reshape(-1)
    off = 0
    for n, size in layout:
        grads[n] = vec[off: off + size].reshape(w_in[n].shape)
        off += size
    loss = vec[off]

    delta, new_m, new_v = {}, {}, {}
    for n in WEIGHTS:
        shp = w_in[n].shape
        as2d = (lambda a: a.reshape(1, -1)) if len(shp) == 1 else (lambda a: a)
        d, nm, nv = _adamw(as2d(w_in[n]), as2d(grads[n]), as2d(given["m_" + n]), as2d(given["v_" + n]), f"adamw_{n}")
        delta[n], new_m[n], new_v[n] = d.reshape(shp), nm.reshape(shp), nv.reshape(shp)
    return (loss, dx[None], *[grads[n] for n in WEIGHTS], *[delta[n] for n in WEIGHTS],
            *[new_m[n] for n in WEIGHTS], *[new_v[n] for n in WEIGHTS])
```

```python
import functools

import numpy as np
import jax
import jax.numpy as jnp
from jax import lax
from jax.experimental import pallas as pl
from jax.experimental.pallas import tpu as pltpu

F32 = jnp.float32
BF16 = jnp.bfloat16
MESH = pl.DeviceIdType.MESH

D_MODEL = 1024
D_FF = 2816
RMS_EPS = 1e-6
PLE_DIM = 256
A_HEADS, A_KV_HEADS, A_HEAD_DIM, WINDOW = 8, 2, 64, 128
A_GROUP = A_HEADS // A_KV_HEADS
B_HEADS, B_Q_LORA, B_KV_LORA, B_NOPE, B_ROPE, B_V = 8, 256, 128, 64, 32, 64
ROPE_THETA = 10000.0
C_HEADS, C_HEAD_DIM = 16, 64
EVEN_IN = 1184
EVEN_IN_PAD = 1280
ODD_IN = 3088
ODD_IN_AUG = 2 * 16 * 80 + 1024 + 16
ODD_IN_PAD = 3840
DEPTH = 2
ADAM_LR, ADAM_B1, ADAM_B2, ADAM_EPS, ADAM_WD, ADAM_STEP = 0.001, 0.9, 0.999, 1e-08, 0.01, 10

N_DEV = 8
LANES = 128
SUBLANES = 8
EW_TILE_BYTES = 3 << 20
MM_VMEM_BYTES = 26 << 20
NEG = -1e30
ATTN_TILE = 1024
ATTN_TILE_FWD = 1024
SWA_TILE = 512
QK_PAD = 80

FF_BLK = D_FF // 4
DOWN_ROWS = D_FF // N_DEV
A_ROWS, B_ROWS, C_ROWS, G3_ROWS, G3_COLS = 2 * DOWN_ROWS, 256, 2 * D_MODEL, 1024, 768
PLE_GATE_BLK, MIX_OUT_BLK = 0, 1
OD_C, EV_C, STRIP_C = 386, 148, 128
STRIP0 = OD_C + EV_C

SMALL = ["ffa_norm", "mix_norm", "ffb_norm", "ple_norm", "ev_sinks", "ev_cq_norm", "ev_ckv_norm", "od_b_f", "final_norm"]
WEIGHTS = ["ffa_norm", "ffa_w_gate_up", "ffa_w_down", "mix_norm", "ffb_norm", "ffb_w_gate_up", "ffb_w_down", "ple_norm",
           "ple_w_gate", "ple_w_proj", "ev_w_in", "ev_sinks", "ev_cq_norm", "ev_w_uq", "ev_ckv_norm", "ev_w_ukv", "ev_w_out",
           "od_w_in", "od_b_f", "od_w_out", "final_norm"]
SMALL_COLS = 1280


def _divisor(n, cap, mult):
    if n <= cap:
        return n
    for t in range(cap - cap % mult, 0, -mult):
        if n % t == 0:
            return t
    raise ValueError(f"no tile for {n} under {cap} in steps of {mult}")


def _lanes(c):
    return -(-c // LANES) * LANES


def _ew(fn, rows, vecs, outs, reds=(), *, name):
    R = rows[0].shape[0]
    per_row = sum(_lanes(a.shape[1]) * a.dtype.itemsize for a in rows) + sum(_lanes(c) * jnp.dtype(d).itemsize for c, d in outs)
    tm = _divisor(R, max(16, EW_TILE_BYTES // per_row // 16 * 16), 16) if R % 16 == 0 else R
    n_r, n_v, n_o = len(rows), len(vecs), len(outs)

    def body(*refs):
        ins = [r[...] for r in refs[: n_r + n_v]]
        res = fn(*ins)
        if not isinstance(res, (tuple, list)):
            res = (res,)
        o_refs = refs[n_r + n_v: n_r + n_v + n_o]
        r_refs = refs[n_r + n_v + n_o:]
        for ref, val in zip(o_refs, res[:n_o]):
            ref[...] = val.astype(ref.dtype)
        if r_refs:
            @pl.when(pl.program_id(0) == 0)
            def _():
                for ref in r_refs:
                    ref[...] = jnp.zeros_like(ref)
            for ref, val in zip(r_refs, res[n_o:]):
                ref[...] += val

    in_specs = [pl.BlockSpec((tm, a.shape[1]), lambda i: (i, 0)) for a in rows]
    in_specs += [pl.BlockSpec((1, a.shape[1]), lambda i: (0, 0)) for a in vecs]
    out_specs = [pl.BlockSpec((tm, c), lambda i: (i, 0)) for c, _ in outs]
    out_specs += [pl.BlockSpec((1, c), lambda i: (0, 0)) for c in reds]
    out_shape = [jax.ShapeDtypeStruct((R, c), d) for c, d in outs] + [jax.ShapeDtypeStruct((1, c), F32) for c in reds]
    res = pl.pallas_call(body, name=name, grid=(R // tm,), in_specs=in_specs, out_specs=out_specs, out_shape=out_shape)(*rows, *vecs)
    return res[0] if len(res) == 1 else res


def _rms_fwd(x, w, name):
    def fn(x, w):
        y = x * lax.rsqrt(jnp.mean(x * x, axis=-1, keepdims=True) + RMS_EPS)
        return y * w
    return _ew(fn, [x], [w], [(x.shape[1], BF16)], name=name)


def _rms_bwd(dn, x, w, dres, name):
    def fn(dn, x, *rest):
        w = rest[-1]
        r = lax.rsqrt(jnp.mean(x * x, axis=-1, keepdims=True) + RMS_EPS)
        xh = x * r
        gw = dn * w
        dx = r * (gw - xh * jnp.mean(gw * xh, axis=-1, keepdims=True))
        if len(rest) == 2:
            dx = dx + rest[0]
        return dx, jnp.sum(dn * xh, axis=0, keepdims=True)
    rows = [dn, x] + ([dres] if dres is not None else [])
    return _ew(fn, rows, [w], [(x.shape[1], F32)], [x.shape[1]], name=name)


def _ple_fwd(h, gpre, pp, name):
    return _ew(lambda h, g, q: h + jax.nn.sigmoid(g) * q, [h, gpre, pp], [], [(h.shape[1], F32)], name=name)


def _ple_bwd(dh, gpre, pp, name):
    def fn(dh, g, q):
        sg = jax.nn.sigmoid(g)
        return dh * q * (sg * (1.0 - sg)), dh * sg
    return _ew(fn, [dh, gpre, pp], [], [(dh.shape[1], BF16), (dh.shape[1], BF16)], name=name)


def _rope(x1, x2, cos, sin, name):
    c = x1.shape[1]
    return _ew(lambda a, b, co, si: (a * co - b * si, a * si + b * co), [x1, x2, cos, sin], [], [(c, F32), (c, F32)], name=name)


def _logsig_fwd(f, b, name):
    def fn(f, b):
        z = f + b
        return jnp.minimum(z, 0.0) - jnp.log(1.0 + jnp.exp(-jnp.abs(z)))
    return _ew(fn, [f], [b], [(f.shape[1], F32)], name=name)


def _logsig_bwd(dlogf, f, b, name):
    def fn(d, f, b):
        df = d * jax.nn.sigmoid(-(f + b))
        return df, jnp.sum(df, axis=0, keepdims=True)
    return _ew(fn, [dlogf, f], [b], [(f.shape[1], F32)], [f.shape[1]], name=name)


def _final_fwd_bwd(h, w, target, name):
    d = h.shape[1]

    def fn(h, t, w):
        r = lax.rsqrt(jnp.mean(h * h, axis=-1, keepdims=True) + RMS_EPS)
        xh = h * r
        y = xh * w
        err = y - t
        dy = err * (1.0 / d)
        gw = dy * w
        dx = r * (gw - xh * jnp.mean(gw * xh, axis=-1, keepdims=True))
        return dx, jnp.sum(dy * xh, axis=0, keepdims=True), jnp.sum(err * err, axis=0, keepdims=True) * (0.5 / d)
    return _ew(fn, [h, target], [w], [(d, F32)], [d, d], name=name)


def _adamw(w, g, m, v, name):
    shape = w.shape
    c = shape[-1]
    w2, g2, m2, v2 = (a.reshape(-1, c) for a in (w, g, m, v))

    def fn(w, g, m, v):
        m = ADAM_B1 * m + (1.0 - ADAM_B1) * g
        v = ADAM_B2 * v + (1.0 - ADAM_B2) * jnp.square(g)
        m_hat = m / (1.0 - ADAM_B1 ** ADAM_STEP)
        v_hat = v / (1.0 - ADAM_B2 ** ADAM_STEP)
        delta = -ADAM_LR * (m_hat / (jnp.sqrt(v_hat) + ADAM_EPS) + ADAM_WD * w)
        return delta, m, v
    d, nm, nv = _ew(fn, [w2, g2, m2, v2], [], [(c, F32)] * 3, name=name)
    return d.reshape(shape), nm.reshape(shape), nv.reshape(shape)


def _split3(v):
    hi = v.astype(BF16)
    r1 = v - hi.astype(F32)
    mid = r1.astype(BF16)
    lo = (r1 - mid.astype(F32)).astype(BF16)
    return hi, mid, lo


def _cumsum(x, reverse, name):
    S, C = x.shape
    tm = _divisor(S, 512, 16)
    nt = S // tm

    def body(x_ref, o_ref, carry):
        @pl.when(pl.program_id(0) == 0)
        def _():
            carry[...] = jnp.zeros_like(carry)
        r = lax.broadcasted_iota(jnp.int32, (tm, tm), 0)
        c = lax.broadcasted_iota(jnp.int32, (tm, tm), 1)
        tri = jnp.where((c >= r) if reverse else (c <= r), 1.0, 0.0).astype(BF16)
        xv = x_ref[...]
        acc = jnp.zeros((tm, C), F32)
        for part in _split3(xv):
            acc = acc + jnp.dot(tri, part, preferred_element_type=F32)
        o_ref[...] = acc + carry[...]
        carry[...] += jnp.sum(xv, axis=0, keepdims=True)

    idx = (lambda i: (nt - 1 - i, 0)) if reverse else (lambda i: (i, 0))
    return pl.pallas_call(
        body, name=name, grid=(nt,), in_specs=[pl.BlockSpec((tm, C), idx)], out_specs=pl.BlockSpec((tm, C), idx),
        out_shape=jax.ShapeDtypeStruct((S, C), F32), scratch_shapes=[pltpu.VMEM((1, C), F32)],
    )(x)


NN = (((1,), (0,)), ((), ()))
NT = (((1,), (1,)), ((), ()))
TN = (((0,), (0,)), ((), ()))

HBM_SPEC = pl.BlockSpec(memory_space=pl.ANY)


def _job_in_body(job, refs, n_in, n_out, n_scr, grid):
    if job is None:
        return refs[n_in:], lambda: None
    ji, jo = len(job["ins"]), len(job["outs"])
    j_in = refs[n_in: n_in + ji]
    pos = n_in + ji
    own = list(refs[pos: pos + n_out])
    pos += n_out
    j_out = refs[pos: pos + jo]
    pos += jo
    own += list(refs[pos: pos + n_scr])
    ss, rs = refs[-2], refs[-1]
    first = functools.reduce(jnp.logical_and, [pl.program_id(d) == 0 for d in range(len(grid))])
    last = functools.reduce(jnp.logical_and, [pl.program_id(d) == n - 1 for d, n in enumerate(grid)])

    @pl.when(first)
    def _():
        job["start"](j_in, j_out, ss, rs)

    def finish():
        @pl.when(last)
        def _():
            job["finish"](j_in, j_out, ss, rs)

    return own, finish


def _job_call(job, body, *, name, grid, in_specs, out_specs, out_shape, args, scratch_shapes, aliases, dimension_semantics):
    in_specs, out_specs, out_shape, args, scratch_shapes = list(in_specs), list(out_specs), list(out_shape), list(args), list(scratch_shapes)
    aliases = dict(aliases)
    if job is not None:
        for i, o in job["aliases"].items():
            aliases[len(args) + i] = len(out_shape) + o
        in_specs += [HBM_SPEC] * len(job["ins"])
        args += list(job["ins"])
        out_specs += [HBM_SPEC] * len(job["outs"])
        out_shape += list(job["outs"])
        scratch_shapes += [pltpu.SemaphoreType.DMA((job["n_sems"],)), pltpu.SemaphoreType.DMA((job["n_sems"],))]
    return pl.pallas_call(
        body, name=name, grid=grid, in_specs=in_specs, out_specs=out_specs, out_shape=out_shape,
        scratch_shapes=scratch_shapes, input_output_aliases=aliases,
        compiler_params=pltpu.CompilerParams(dimension_semantics=dimension_semantics),
    )(*args)


def _comm_call(job, name):
    def body(*refs):
        ji, jo = len(job["ins"]), len(job["outs"])
        job["start"](refs[:ji], refs[ji: ji + jo], refs[-2], refs[-1])
        job["finish"](refs[:ji], refs[ji: ji + jo], refs[-2], refs[-1])

    return pl.pallas_call(
        body, name=name, in_specs=[HBM_SPEC] * len(job["ins"]), out_specs=[HBM_SPEC] * len(job["outs"]), out_shape=list(job["outs"]),
        input_output_aliases=dict(job["aliases"]),
        scratch_shapes=[pltpu.SemaphoreType.DMA((job["n_sems"],)), pltpu.SemaphoreType.DMA((job["n_sems"],))],
    )(*job["ins"])


def _mm_call(name, grid, k_axis, a, a_spec, a2d, b, b_spec, b2d, dims, out_sds, out_spec, o2d, *,
             alpha=1.0, res=None, res_spec=None, into=None, job=None, norm_bwd=None):
    nk = grid[k_axis]
    n_in = 2 + (res is not None) + (into is not None) + (3 if norm_bwd is not None else 0)
    n_out = 2 if norm_bwd is not None else 1

    def body(*refs):
        a_ref, b_ref = refs[0], refs[1]
        res_ref = refs[2] if res is not None else None
        own, finish_job = _job_in_body(job, refs, n_in, n_out, 1, grid)
        o_ref, acc_ref = own[0], own[-1]
        k = pl.program_id(k_axis)

        @pl.when(k == 0)
        def _():
            acc_ref[...] = jnp.zeros_like(acc_ref)

        if norm_bwd is not None:
            x_ref, w_ref, dres_ref = refs[n_in - 3: n_in]
            dw_ref = own[1]

            @pl.when(functools.reduce(jnp.logical_and, [pl.program_id(d) == 0 for d in range(len(grid))]))
            def _():
                dw_ref[...] = jnp.zeros_like(dw_ref)

        av = a_ref[...].reshape(a2d).astype(BF16)
        bv = b_ref[...].reshape(b2d).astype(BF16)
        acc_ref[...] += lax.dot_general(av, bv, dims, preferred_element_type=F32)

        @pl.when(k == nk - 1)
        def _():
            r = acc_ref[...]
            if alpha != 1.0:
                r = r * alpha
            if res_ref is not None:
                r = res_ref[...].reshape(o2d) + r
            if norm_bwd is not None:
                x = x_ref[...]
                rs = lax.rsqrt(jnp.mean(x * x, axis=-1, keepdims=True) + RMS_EPS)
                xh = x * rs
                gw = r * w_ref[...]
                dw_ref[...] += jnp.sum(r * xh, axis=0, keepdims=True)
                r = dres_ref[...] + rs * (gw - xh * jnp.mean(gw * xh, axis=-1, keepdims=True))
            o_ref[...] = r.reshape(o_ref.shape).astype(o_ref.dtype)

        finish_job()

    in_specs, args = [a_spec, b_spec], [a, b]
    if res is not None:
        in_specs.append(res_spec)
        args.append(res)
    aliases = {}
    if into is not None:
        aliases = {len(args): 0}
        in_specs.append(pl.BlockSpec(memory_space=pl.ANY))
        args.append(into)
        out_sds = jax.ShapeDtypeStruct(into.shape, into.dtype)
    out_specs, out_shape = [out_spec], [out_sds]
    if norm_bwd is not None:
        vec = pl.BlockSpec((1, o2d[1]), lambda *_: (0, 0))
        in_specs += [out_spec, vec, out_spec]
        args += list(norm_bwd)
        out_specs.append(vec)
        out_shape.append(jax.ShapeDtypeStruct((1, o2d[1]), F32))
    serial = job is not None or norm_bwd is not None
    sem = tuple("arbitrary" if d == k_axis or serial else "parallel" for d in range(len(grid)))
    res_all = _job_call(
        job, body, name=name, grid=grid, in_specs=in_specs, out_specs=out_specs, out_shape=out_shape, args=args,
        scratch_shapes=[pltpu.VMEM(o2d, F32)], aliases=aliases, dimension_semantics=sem)
    own = res_all[0] if n_out == 1 else tuple(res_all[:n_out])
    return own if job is None else (own, res_all[n_out:])


def _mm(a, b, *, ta=False, tb=False, out=F32, res=None, alpha=1.0, norm_bwd=None, name):
    K, M = a.shape if ta else a.shape[::-1]
    N = b.shape[0] if tb else b.shape[1]
    assert (b.shape[1] if tb else b.shape[0]) == K, (a.shape, b.shape, ta, tb)
    tk = _divisor(K, 1024, LANES)
    tn = _divisor(N, 1408, LANES)
    assert norm_bwd is None or tn == N
    for cap in (1024, 512, 256, 128):
        tm = _divisor(M, cap, LANES if ta else 16)
        est = 2 * (tm * tk * a.dtype.itemsize + tk * tn * b.dtype.itemsize + tm * tn * jnp.dtype(out).itemsize)
        est += tm * tn * 4 + (2 * tm * tn * 4 if res is not None else 0) + (4 * tm * tn * 4 if norm_bwd is not None else 0)
        if est <= MM_VMEM_BYTES:
            break
    a_spec = pl.BlockSpec((tk, tm), lambda i, j, k: (k, i)) if ta else pl.BlockSpec((tm, tk), lambda i, j, k: (i, k))
    b_spec = pl.BlockSpec((tn, tk), lambda i, j, k: (j, k)) if tb else pl.BlockSpec((tk, tn), lambda i, j, k: (k, j))
    o_spec = pl.BlockSpec((tm, tn), lambda i, j, k: (i, j))
    dims = (((0 if ta else 1,), (1 if tb else 0,)), ((), ()))
    return _mm_call(name, (M // tm, N // tn, K // tk), 2, a, a_spec, (tk, tm) if ta else (tm, tk), b, b_spec,
                    (tn, tk) if tb else (tk, tn), dims, jax.ShapeDtypeStruct((M, N), out), o_spec, (tm, tn),
                    alpha=alpha, res=res, res_spec=o_spec, norm_bwd=norm_bwd)


def _w128_spec(blk):
    return pl.BlockSpec((N_DEV, 128, D_MODEL), lambda *_: (0, blk, 0))


def _mm_w128(a, G1, blk, *, tb=False, res=None, out=F32, norm_bwd=None, name):
    S = a.shape[0]
    tm = _divisor(S, 1024 if norm_bwd is None else 512, 16)
    row = pl.BlockSpec((tm, D_MODEL), lambda i, k: (i, 0))
    return _mm_call(name, (S // tm, 1), 1, a, row, (tm, D_MODEL), G1, _w128_spec(blk), (D_MODEL, D_MODEL), NT if tb else NN,
                    jax.ShapeDtypeStruct((S, D_MODEL), out), row, (tm, D_MODEL), res=res, res_spec=row, norm_bwd=norm_bwd)


def _mm_w128_dw(a, b, blk, into, name):
    S = a.shape[0]
    tk = _divisor(S, 1024, 16)
    row = pl.BlockSpec((tk, D_MODEL), lambda i, k: (k, 0))
    return _mm_call(name, (1, S // tk), 1, a, row, (tk, D_MODEL), b, row, (tk, D_MODEL), TN, None, _w128_spec(blk),
                    (D_MODEL, D_MODEL), into=into)


def _ffn_gate_up(h, norm_w, G2v, rb, name, job=None):
    S = h.shape[0]
    tm = _divisor(S, 1024, 16)
    grid = (S // tm, 4)

    def body(*refs):
        h_ref, nw_ref, w_ref = refs[:3]
        (n_ref, gu_ref, act_ref, n_scr), finish_job = _job_in_body(job, refs, 3, 3, 1, grid)

        @pl.when(pl.program_id(1) == 0)
        def _():
            x = h_ref[...]
            y = x * lax.rsqrt(jnp.mean(x * x, axis=-1, keepdims=True) + RMS_EPS)
            n_scr[...] = (y * nw_ref[...]).astype(BF16)
            n_ref[...] = n_scr[...]

        nv = n_scr[...]
        g = jnp.dot(nv, w_ref[0, 0], preferred_element_type=F32)
        u = jnp.dot(nv, w_ref[1, 0], preferred_element_type=F32)
        sg = jax.nn.sigmoid(g)
        silu = g * sg
        gu_ref[0, 0] = (u * (sg * (1.0 + g * (1.0 - sg)))).astype(BF16)
        gu_ref[1, 0] = silu.astype(BF16)
        act_ref[0] = (silu * u).astype(BF16)
        finish_job()

    row = pl.BlockSpec((tm, D_MODEL), lambda i, j: (i, 0))
    return _job_call(
        job, body, name=name, grid=grid,
        in_specs=[row, pl.BlockSpec((1, D_MODEL), lambda i, j: (0, 0)), pl.BlockSpec((2, 1, D_MODEL, FF_BLK), lambda i, j: (0, j, rb, 0))],
        out_specs=[row, pl.BlockSpec((2, 1, tm, FF_BLK), lambda i, j: (0, j, i, 0)), pl.BlockSpec((1, tm, FF_BLK), lambda i, j: (j, i, 0))],
        out_shape=[jax.ShapeDtypeStruct((S, D_MODEL), BF16), jax.ShapeDtypeStruct((2, 4, S, FF_BLK), BF16), jax.ShapeDtypeStruct((4, S, FF_BLK), BF16)],
        args=[h, norm_w, G2v], scratch_shapes=[pltpu.VMEM((tm, D_MODEL), BF16)], aliases={},
        dimension_semantics=("arbitrary" if job is not None else "parallel", "arbitrary"))


def _ffn_down(act, G1, ob, h, name, job=None):
    S = h.shape[0]
    tm = _divisor(S, 1024, 16)
    row = pl.BlockSpec((tm, D_MODEL), lambda i, k: (i, 0))
    return _mm_call(name, (S // tm, 4), 1, act, pl.BlockSpec((1, tm, FF_BLK), lambda i, k: (k, i, 0)), (tm, FF_BLK),
                    G1, pl.BlockSpec((2, DOWN_ROWS, D_MODEL), lambda i, k: (k, ob, 0)), (FF_BLK, D_MODEL), NN,
                    jax.ShapeDtypeStruct((S, D_MODEL), F32), row, (tm, D_MODEL), alpha=0.5, res=h, res_spec=row, job=job)


def _ffn_down_dx(dh, G1, ob, gu, name):
    S = dh.shape[0]
    tm = _divisor(S, 1024, 16)

    def body(dh_ref, w_ref, gu_ref, o_ref):
        w = w_ref[...].reshape(FF_BLK, D_MODEL)
        dact = lax.dot_general(dh_ref[...].astype(BF16), w, NT, preferred_element_type=F32) * 0.5
        o_ref[0, 0] = (dact * gu_ref[0, 0].astype(F32)).astype(BF16)
        o_ref[1, 0] = (dact * gu_ref[1, 0].astype(F32)).astype(BF16)

    blk = pl.BlockSpec((2, 1, tm, FF_BLK), lambda i, j: (0, j, i, 0))
    return pl.pallas_call(
        body, name=name, grid=(S // tm, 4),
        in_specs=[pl.BlockSpec((tm, D_MODEL), lambda i, j: (i, 0)), pl.BlockSpec((2, DOWN_ROWS, D_MODEL), lambda i, j: (j, ob, 0)), blk],
        out_specs=blk, out_shape=jax.ShapeDtypeStruct((2, 4, S, FF_BLK), BF16),
    )(dh, G1, gu)


def _ffn_down_dw(act, dh, name, job=None):
    S = dh.shape[0]
    tk = _divisor(S, 1024, 16)
    return _mm_call(name, (4, S // tk), 1, act, pl.BlockSpec((1, tk, FF_BLK), lambda j, k: (j, k, 0)), (tk, FF_BLK),
                    dh, pl.BlockSpec((tk, D_MODEL), lambda j, k: (k, 0)), (tk, D_MODEL), TN,
                    jax.ShapeDtypeStruct((N_DEV, DOWN_ROWS, D_MODEL), BF16),
                    pl.BlockSpec((2, DOWN_ROWS, D_MODEL), lambda j, k: (j, 0, 0)), (FF_BLK, D_MODEL), alpha=0.5, job=job)


def _ffn_gate_up_dw(n, dgu8, name, job=None):
    S = n.shape[0]
    tk = _divisor(S, 1024, 16)
    return _mm_call(name, (N_DEV, S // tk), 1, n, pl.BlockSpec((tk, D_MODEL), lambda b, k: (k, 0)), (tk, D_MODEL),
                    dgu8, pl.BlockSpec((1, tk, FF_BLK), lambda b, k: (b, k, 0)), (tk, FF_BLK), TN,
                    jax.ShapeDtypeStruct((N_DEV, D_MODEL, FF_BLK), BF16),
                    pl.BlockSpec((1, D_MODEL, FF_BLK), lambda b, k: (b, 0, 0)), (D_MODEL, FF_BLK), job=job)


def _ffn_gate_up_dx(dgu8, G2, rb, h, norm_w, dres, name, job=None):
    S = h.shape[0]
    tm = _divisor(S, 1024, 16)
    row = pl.BlockSpec((tm, D_MODEL), lambda i, k: (i, 0))
    return _mm_call(name, (S // tm, N_DEV), 1, dgu8, pl.BlockSpec((1, tm, FF_BLK), lambda i, k: (k, i, 0)), (tm, FF_BLK),
                    G2, pl.BlockSpec((1, D_MODEL, FF_BLK), lambda i, k: (k, rb, 0)), (D_MODEL, FF_BLK), NT,
                    jax.ShapeDtypeStruct((S, D_MODEL), F32), row, (tm, D_MODEL), norm_bwd=(h, norm_w, dres), job=job)


def _unheads(x):
    h, S, d = x.shape
    return jnp.transpose(x, (1, 0, 2)).reshape(S, h * d)


def _exact3(v):
    rnd = lambda a: lax.reduce_precision(a, exponent_bits=8, mantissa_bits=7)
    hi = rnd(v)
    mid = rnd(v - hi)
    return hi, mid, rnd(v - hi - mid)


def _causal_mask(st, q0, k0, window):
    dist = (q0 + lax.broadcasted_iota(jnp.int32, st.shape, 1)) - (k0 + lax.broadcasted_iota(jnp.int32, st.shape, 0))
    mask = dist >= 0
    if window is not None:
        mask = mask & (dist < window)
    return jnp.where(mask, st, NEG)


def _attn_fwd(qT, k, vT1, *, tile, hb, window=None, sink=None, name, job=None):
    H, dqk, S = qT.shape
    G = H // k.shape[0]
    dvp = vT1.shape[1]
    dv = dvp - 16
    tq = tk = tile
    assert H % hb == 0 and (G == 1 or G % hb == 0)
    kvb = hb if G == 1 else 1
    grid = (H // hb, S // tq)
    n_in = 3 + (sink is not None)

    def body(*refs):
        q_ref, k_ref, v_ref = refs[:3]
        (o_ref, lse_ref), finish_job = _job_in_body(job, refs, n_in, 2, 0, grid)
        i = pl.program_id(1)
        carry = []
        for a in range(hb):
            if sink is not None:
                carry.append(jnp.zeros((1, tq), F32) + refs[3][a, :, 0:1])
                carry.append(jnp.where(lax.broadcasted_iota(jnp.int32, (dvp, tq), 0) == dv, 1.0, 0.0))
            else:
                carry.append(jnp.full((1, tq), NEG, F32))
                carry.append(jnp.zeros((dvp, tq), F32))

        def step(j, carry, masked, off=None, keys=tk):
            off = pl.multiple_of(j * tk, tk) if off is None else off
            out = []
            for a in range(hb):
                m, acc = carry[2 * a], carry[2 * a + 1]
                kv = a if kvb > 1 else 0
                st = jnp.dot(k_ref[kv, pl.ds(off, keys), :], q_ref[a], preferred_element_type=F32)
                if masked:
                    st = _causal_mask(st, i * tq, off, window)
                m_new = jnp.maximum(m, jnp.max(st, axis=0, keepdims=True))
                pt = jnp.exp(st - m_new).astype(BF16)
                acc = jnp.exp(m - m_new) * acc + jnp.dot(v_ref[kv, :, pl.ds(off, keys)], pt, preferred_element_type=F32)
                out += [m_new, acc]
            return tuple(out)

        carry = tuple(carry)
        if window is None:
            carry = lax.fori_loop(0, i, functools.partial(step, masked=False), carry)
            carry = step(i, carry, True)
        else:
            assert window % LANES == 0 and tq + window <= S
            carry = step(None, carry, True, off=pl.multiple_of(jnp.maximum(i * tq - window, 0), LANES), keys=tq + window)
        for a in range(hb):
            m, acc = carry[2 * a], carry[2 * a + 1]
            l = acc[dv:dv + 1, :]
            o_ref[a] = acc[:dv, :] / l
            lse_ref[a] = m + jnp.log(l)
        finish_job()

    kv_idx = (lambda b: b) if G == 1 else (lambda b: (b * hb) // G)
    in_specs = [
        pl.BlockSpec((hb, dqk, tq), lambda b, i: (b, 0, i)),
        pl.BlockSpec((kvb, S, dqk), lambda b, i: (kv_idx(b), 0, 0)),
        pl.BlockSpec((kvb, dvp, S), lambda b, i: (kv_idx(b), 0, 0)),
    ]
    args = [qT, k, vT1]
    if sink is not None:
        in_specs += [pl.BlockSpec((hb, 1, LANES), lambda b, i: (b, 0, 0))]
        args += [sink]
    return _job_call(
        job, body, name=name, grid=grid, in_specs=in_specs,
        out_specs=[pl.BlockSpec((hb, dv, tq), lambda b, i: (b, 0, i)), pl.BlockSpec((hb, 1, tq), lambda b, i: (b, 0, i))],
        out_shape=[jax.ShapeDtypeStruct((H, dv, S), F32), jax.ShapeDtypeStruct((H, 1, S), F32)],
        args=args, scratch_shapes=[], aliases={}, dimension_semantics=("arbitrary", "arbitrary") if job is not None else ("parallel", "parallel"))


def _attn_bwd(qT, k, kT, v, oT, doT, lse, *, tile, hb, window=None, sink=None, real=None, extra=False, full=False, name):
    H, dqk, S = qT.shape
    G = H // k.shape[0]
    dv = v.shape[2]
    tq = tk = tile
    nq = S // tq
    has_p = sink is not None
    real = dqk if real is None else real
    main = dqk if full else real
    assert H % hb == 0 and (G == 1 or G % hb == 0) and not (extra and real == dqk)
    kvb = hb if G == 1 else 1

    def body(*refs):
        qT_ref, k_ref, kT_ref, v_ref, oT_ref, doT_ref, lse_ref = refs[:7]
        p_ref = refs[7] if has_p else None
        pos = 8 if has_p else 7
        dq_ref, dk_ref, dv_ref = refs[pos: pos + 3]
        pos += 3
        ds_ref = refs[pos] if has_p else None
        pos += has_p
        dqx_ref, dkx_ref = (refs[pos], refs[pos + 1]) if extra else (None, None)
        delta = refs[-1]
        j = pl.program_id(1)

        @pl.when(j == 0)
        def _():
            dq_ref[...] = jnp.zeros_like(dq_ref)
            if extra:
                dqx_ref[...] = jnp.zeros_like(dqx_ref)
            for a in range(hb):
                drow = jnp.sum(doT_ref[a].astype(F32) * oT_ref[a], axis=0, keepdims=True)
                delta[a] = drow
                if has_p:
                    w = jnp.exp(p_ref[a, :, 0:1] - lse_ref[a])
                    ds_ref[a] = jnp.zeros((1, LANES), F32) - jnp.sum(w * drow, axis=1, keepdims=True)

        def step(i, carry, masked, off=None, qs=tq):
            off = pl.multiple_of(i * tq, tq) if off is None else off
            out = []
            for a in range(hb):
                dk, dvv = carry[2 * a], carry[2 * a + 1]
                kv = a if kvb > 1 else 0
                qTi = qT_ref[a, :, pl.ds(off, qs)]
                doTi = doT_ref[a, :, pl.ds(off, qs)]
                st = jnp.dot(k_ref[kv], qTi, preferred_element_type=F32)
                if masked:
                    st = _causal_mask(st, off, j * tk, window)
                pt = jnp.exp(st - lse_ref[a, :, pl.ds(off, qs)])
                dvv = dvv + lax.dot_general(pt.astype(BF16), doTi, NT, preferred_element_type=F32)
                dpt = jnp.dot(v_ref[kv], doTi, preferred_element_type=F32)
                dsb = (pt * (dpt - delta[a, :, pl.ds(off, qs)])).astype(BF16)
                dk = dk + lax.dot_general(dsb, qTi, NT, preferred_element_type=F32)
                dqt = jnp.dot(kT_ref[kv], dsb, preferred_element_type=F32)
                dq_ref[a, :, pl.ds(off, qs)] += dqt[:main]
                if extra:
                    dqx_ref[a, :, pl.ds(off, qs)] += dqt[real:]
                out += [dk, dvv]
            return tuple(out)

        carry = (jnp.zeros((tk, dqk), F32), jnp.zeros((tk, dv), F32)) * hb
        if window is None:
            carry = step(j, carry, True)
            carry = lax.fori_loop(j + 1, nq, functools.partial(step, masked=False), carry)
        else:
            assert window % LANES == 0 and tk + window <= S
            carry = step(None, carry, True, off=pl.multiple_of(jnp.minimum(j * tk, S - (tk + window)), LANES), qs=tk + window)
        for a in range(hb):
            dk_ref[a] = carry[2 * a][:, :main]
            if extra:
                dkx_ref[a] = carry[2 * a][:, real:]
            dv_ref[a] = carry[2 * a + 1]

    kv_idx = (lambda b: b) if G == 1 else (lambda b: (b * hb) // G)
    colsT = lambda d: pl.BlockSpec((hb, d, S), lambda b, j: (b, 0, 0))
    in_specs = [
        colsT(dqk),
        pl.BlockSpec((kvb, tk, dqk), lambda b, j: (kv_idx(b), j, 0)),
        pl.BlockSpec((kvb, dqk, tk), lambda b, j: (kv_idx(b), 0, j)),
        pl.BlockSpec((kvb, tk, dv), lambda b, j: (kv_idx(b), j, 0)),
        colsT(dv), colsT(dv),
        pl.BlockSpec((hb, 1, S), lambda b, j: (b, 0, 0)),
    ]
    args = [qT, k, kT, v, oT, doT, lse]
    if has_p:
        in_specs += [pl.BlockSpec((hb, 1, LANES), lambda b, j: (b, 0, 0))]
        args += [sink]
    out_specs = [colsT(main), pl.BlockSpec((hb, tk, main), lambda b, j: (b, j, 0)), pl.BlockSpec((hb, tk, dv), lambda b, j: (b, j, 0))]
    out_shape = [jax.ShapeDtypeStruct((H, main, S), F32), jax.ShapeDtypeStruct((H, S, main), F32), jax.ShapeDtypeStruct((H, S, dv), F32)]
    if has_p:
        out_specs += [pl.BlockSpec((hb, 1, LANES), lambda b, j: (b, 0, 0))]
        out_shape += [jax.ShapeDtypeStruct((H, 1, LANES), F32)]
    if extra:
        out_specs += [colsT(dqk - real), pl.BlockSpec((hb, tk, dqk - real), lambda b, j: (b, j, 0))]
        out_shape += [jax.ShapeDtypeStruct((H, dqk - real, S), F32), jax.ShapeDtypeStruct((H, S, dqk - real), F32)]
    return pl.pallas_call(
        body, name=name, grid=(H // hb, S // tk), in_specs=in_specs, out_specs=out_specs, out_shape=out_shape,
        scratch_shapes=[pltpu.VMEM((hb, 1, S), F32)],
        compiler_params=pltpu.CompilerParams(dimension_semantics=("parallel", "arbitrary")),
    )(*args)


def _rows_and_cols(x3):
    xb = x3.astype(BF16)
    return jnp.transpose(xb, (1, 0, 2)), jnp.transpose(xb, (1, 2, 0))


def _cols_only(x3):
    return jnp.transpose(x3.astype(BF16), (1, 2, 0))


def _v_with_ones(v3):
    S, h, _ = v3.shape
    vT = jnp.transpose(v3.astype(BF16), (1, 2, 0))
    return jnp.concatenate([vT, jnp.ones((h, 1, S), BF16), jnp.zeros((h, 15, S), BF16)], axis=1)


def _from_T(oT):
    h, d, S = oT.shape
    return jnp.transpose(oT, (2, 0, 1)).reshape(S, h * d)


def _coords():
    return lax.axis_index("x"), lax.axis_index("y"), lax.axis_index("c")


def _peer(axis):
    x, y, c = _coords()
    return {"x": (1 - x, y, c), "y": (x, 1 - y, c), "c": (x, y, 1 - c)}[axis]


def _gather_job(bufs, rows=None):
    n = len(bufs)

    def copies(outs, send_sems, recv_sems):
        x, y, c = _coords()
        me, sibling = (x, y, c), (x, y, 1 - c)
        chips = [(1 - x, y), (x, 1 - y), (1 - x, 1 - y)]

        def copy(t, k, block, to):
            px, py, pc = block
            ref = outs[t].at[4 * px + 2 * py + pc]
            if rows is not None and rows[t] is not None:
                ref = ref.at[pl.ds(rows[t][0], rows[t][1])]
            return pltpu.make_async_remote_copy(ref, ref, send_sems.at[7 * t + k], recv_sems.at[7 * t + k], device_id=to, device_id_type=MESH)

        return copy, me, sibling, chips, c

    def start(ins, outs, send_sems, recv_sems):
        copy, me, sibling, chips, c = copies(outs, send_sems, recv_sems)
        for t in range(n):
            copy(t, 0, me, sibling).start()
            for j, chip in enumerate(chips):
                copy(t, 1 + j, me, (*chip, c)).start()

    def finish(ins, outs, send_sems, recv_sems):
        copy, me, sibling, chips, c = copies(outs, send_sems, recv_sems)
        for j, chip in enumerate(chips):
            for t in range(n):
                copy(t, 1 + j, (*chip, c), me).wait_recv()
                copy(t, 4 + j, (*chip, c), sibling).start()
        for t in range(n):
            copy(t, 0, sibling, me).wait_recv()
            for j, chip in enumerate(chips):
                copy(t, 4 + j, (*chip, 1 - c), me).wait_recv()
        for t in range(n):
            copy(t, 0, me, sibling).wait_send()
            for j, chip in enumerate(chips):
                copy(t, 1 + j, me, (*chip, c)).wait_send()
                copy(t, 4 + j, (*chip, c), sibling).wait_send()

    return dict(ins=list(bufs), outs=[jax.ShapeDtypeStruct(b.shape, b.dtype) for b in bufs], aliases={t: t for t in range(n)},
                n_sems=7 * n, start=start, finish=finish)


def _in_slot(local):
    x, y, c = _coords()
    buf = lax.empty((N_DEV,) + local.shape, local.dtype)
    return lax.dynamic_update_slice(buf, local[None], (4 * x + 2 * y + c, 0, 0))


def _pair_job(vs, axes):
    n = len(vs)
    axes = [axes] * n if isinstance(axes, str) else axes

    def copies(ins, outs, send_sems, recv_sems):
        out = []
        for t in range(n):
            me = lax.axis_index(axes[t])
            src = ins[t].at[1 - me] if len(ins[t].shape) == 3 else ins[t].at[:, 1 - me]
            out.append(pltpu.make_async_remote_copy(src, outs[t], send_sems.at[t], recv_sems.at[t], device_id=_peer(axes[t]), device_id_type=MESH))
        return out

    def start(*refs):
        for cp in copies(*refs):
            cp.start()

    def finish(*refs):
        for cp in copies(*refs):
            cp.wait()

    return dict(ins=list(vs), outs=[jax.ShapeDtypeStruct(v.shape[:-3] + v.shape[-2:], v.dtype) for v in vs], aliases={}, n_sems=n,
                start=start, finish=finish)


def _add_kept(v, got, axis, out, name):
    R, C = v.shape[-2:]
    lead = v.shape[0] if v.ndim == 4 else 1
    tm = _divisor(R, max(16, EW_TILE_BYTES // (_lanes(C) * (v.dtype.itemsize + got.dtype.itemsize + jnp.dtype(out).itemsize)) // 16 * 16), 16)
    me = lax.axis_index(axis).astype(jnp.int32).reshape(1)
    v4 = v.reshape(lead, 2, R, C)
    g3 = got.reshape(lead, R, C)

    def body(me_ref, v_ref, g_ref, o_ref):
        o_ref[...] = (v_ref[0].astype(F32) + g_ref[...].astype(F32)).astype(o_ref.dtype)

    res = pl.pallas_call(
        body, name=name, out_shape=jax.ShapeDtypeStruct((lead, R, C), out),
        grid_spec=pltpu.PrefetchScalarGridSpec(
            num_scalar_prefetch=1, grid=(lead, R // tm),
            in_specs=[pl.BlockSpec((1, 1, tm, C), lambda b, i, me: (b, me[0], i, 0)), pl.BlockSpec((1, tm, C), lambda b, i, me: (b, i, 0))],
            out_specs=pl.BlockSpec((1, tm, C), lambda b, i, me: (b, i, 0))),
    )(me, v4, g3)
    return res


def _cross_job(vs):
    n = len(vs)

    def copies(ins, outs, send_sems, recv_sems):
        x, y, _ = _coords()
        out = []
        for t in range(n):
            h = ins[t].shape[2] // 2
            out.append(pltpu.make_async_remote_copy(ins[t].at[1 - x, :, pl.ds(0, h)], outs[2 * t], send_sems.at[2 * t], recv_sems.at[2 * t],
                                                    device_id=_peer("x"), device_id_type=MESH))
            out.append(pltpu.make_async_remote_copy(ins[t].at[:, 1 - y, pl.ds(h, h)], outs[2 * t + 1], send_sems.at[2 * t + 1], recv_sems.at[2 * t + 1],
                                                    device_id=_peer("y"), device_id_type=MESH))
        return out

    def start(*refs):
        for cp in copies(*refs):
            cp.start()

    def finish(*refs):
        for cp in copies(*refs):
            cp.wait()

    outs = []
    for v in vs:
        outs += [jax.ShapeDtypeStruct((2, v.shape[2] // 2, v.shape[3]), v.dtype)] * 2
    return dict(ins=list(vs), outs=outs, aliases={}, n_sems=2 * n, start=start, finish=finish)


def _add_picked(v, got, axis, out, name):
    _, _, R, C = v.shape
    h = R // 2
    tm = _divisor(h, max(16, EW_TILE_BYTES // (_lanes(C) * (v.dtype.itemsize + got.dtype.itemsize + jnp.dtype(out).itemsize)) // 16 * 16), 16)
    me = lax.axis_index(axis).astype(jnp.int32).reshape(1)
    if axis == "x":
        v_map = lambda b, i, me: (me[0], b, i, 0)
    else:
        v_map = lambda b, i, me: (b, me[0], i + h // tm, 0)

    def body(me_ref, v_ref, g_ref, o_ref):
        o_ref[...] = (v_ref[0].astype(F32) + g_ref[...].astype(F32)).astype(o_ref.dtype)

    return pl.pallas_call(
        body, name=name, out_shape=jax.ShapeDtypeStruct((2, h, C), out),
        grid_spec=pltpu.PrefetchScalarGridSpec(
            num_scalar_prefetch=1, grid=(2, h // tm),
            in_specs=[pl.BlockSpec((1, 1, tm, C), v_map), pl.BlockSpec((1, tm, C), lambda b, i, me: (b, i, 0))],
            out_specs=pl.BlockSpec((1, tm, C), lambda b, i, me: (b, i, 0))),
    )(me, v, got)


def _reduce_scatter_steps(gs, tag):
    n = len(gs)
    vs = [g.reshape(4, 2, *g.shape[1:]) for g in gs]
    got = yield _pair_job(vs, "c")
    vs = [_add_kept(v, r, "c", BF16, f"rs_{tag}_add_c{t}") for t, (v, r) in enumerate(zip(vs, got))]
    vs = [v.reshape(2, 2, v.shape[1], v.shape[2]) for v in vs]
    got = yield _cross_job(vs)
    up = [_add_picked(v, r, "x", BF16, f"rs_{tag}_add_x{t}") for t, (v, r) in enumerate(zip(vs, got[0::2]))]
    lo = [_add_picked(v, r, "y", BF16, f"rs_{tag}_add_y{t}") for t, (v, r) in enumerate(zip(vs, got[1::2]))]
    got = yield _pair_job(up + lo, ["y"] * n + ["x"] * n)
    out = []
    for t in range(n):
        a = _add_kept(up[t], got[t], "y", F32, f"rs_{tag}_add_y2{t}")[0]
        b = _add_kept(lo[t], got[n + t], "x", F32, f"rs_{tag}_add_x2{t}")[0]
        out.append(jnp.concatenate([a, b], axis=0))
    return out


def _reduce_scatter(gs, tag):
    steps = _reduce_scatter_steps(gs, tag)
    job = next(steps)
    for stage in ("c", "xy", "yx"):
        got = _comm_call(job, f"rs_{tag}_{stage}")
        try:
            job = steps.send(got)
        except StopIteration as done:
            return done.value


def _all_reduce_small(v):
    def body(v_ref, o_ref, buf, send_sems, recv_sems):
        x, y, c = _coords()
        me = 4 * x + 2 * y + c
        buf[me] = v_ref[...]
        copies = []
        for k in range(1, N_DEV):
            peer = tuple((1 - a) if (k >> s) & 1 else a for a, s in ((x, 2), (y, 1), (c, 0)))
            cp = pltpu.make_async_remote_copy(v_ref, buf.at[me], send_sems.at[k - 1], recv_sems.at[k - 1], device_id=peer, device_id_type=MESH)
            cp.start()
            copies.append(cp)
        for cp in copies:
            cp.wait()
        acc = buf[0]
        for d in range(1, N_DEV):
            acc = acc + buf[d]
        o_ref[...] = acc

    vm = pl.BlockSpec(memory_space=pltpu.VMEM)
    return pl.pallas_call(
        body, name="all_reduce_small", in_specs=[vm], out_specs=vm, out_shape=jax.ShapeDtypeStruct(v.shape, F32),
        scratch_shapes=[pltpu.VMEM((N_DEV,) + v.shape, F32), pltpu.SemaphoreType.DMA((N_DEV - 1,)), pltpu.SemaphoreType.DMA((N_DEV - 1,))],
    )(v)


def _local_groups(w, dtype):
    mix_out = [w["ev_w_out"][0], w["od_w_out"][0]]
    layers = []
    for l in range(DEPTH):
        a = jnp.concatenate([w["ffa_w_down"][l], w["ffb_w_down"][l]], axis=0).astype(dtype)
        b = jnp.concatenate([w["ple_w_gate"][l], mix_out[l]], axis=0).astype(dtype)
        c = jnp.concatenate([w["ffa_w_gate_up"][l], w["ffb_w_gate_up"][l]], axis=0).astype(dtype)
        layers.append((a, b, c))
    strip = jnp.concatenate([w["ple_w_proj"].reshape(-1, STRIP_C), w["ev_w_ukv"][0], jnp.pad(w["ev_w_uq"][0], ((0, 0), (0, STRIP_C - 96))),
                             jnp.zeros((G3_ROWS - 896, STRIP_C), F32)], axis=0)
    m = jnp.concatenate([w["od_w_in"][0], w["ev_w_in"][0], strip, jnp.zeros((G3_ROWS, G3_COLS - STRIP0 - STRIP_C), F32)], axis=1).astype(dtype)
    return layers, m


def _ungroup_local(a, b, c, r3):
    out = {
        "ffa_w_down": jnp.stack([x[0] for x in a]), "ffb_w_down": jnp.stack([x[1] for x in a]),
        "ple_w_gate": jnp.stack([x[:128] for x in b]), "ev_w_out": b[0][128:][None], "od_w_out": b[1][128:][None],
        "ffa_w_gate_up": jnp.stack([x[0] for x in c]), "ffb_w_gate_up": jnp.stack([x[1] for x in c]),
        "od_w_in": r3[:, :OD_C][None], "ev_w_in": r3[:, OD_C:STRIP0][None],
    }
    strip = r3[:, STRIP0:STRIP0 + STRIP_C]
    out["ple_w_proj"] = strip[:512].reshape(2, PLE_DIM, STRIP_C)
    out["ev_w_ukv"] = strip[512:640][None]
    out["ev_w_uq"] = strip[640:896, :96][None]
    return out


def _cols(a):
    return jnp.transpose(a, (1, 0, 2)).reshape(a.shape[1], -1)


def _blocks(g, c):
    return jnp.transpose(g.reshape(g.shape[0], N_DEV, c), (1, 0, 2))


def _uq_permute(w):
    r = w.shape[0]
    w3 = w.reshape(r, B_HEADS, B_NOPE + B_ROPE)
    half = B_ROPE // 2
    return jnp.concatenate([w3[:, :, :B_NOPE].reshape(r, -1), w3[:, :, B_NOPE:B_NOPE + half].reshape(r, -1), w3[:, :, B_NOPE + half:].reshape(r, -1)], axis=1)


def _uq_unpermute(g):
    r = g.shape[0]
    half = B_ROPE // 2
    n = B_HEADS * B_NOPE
    parts = [g[:, :n].reshape(r, B_HEADS, B_NOPE), g[:, n:n + B_HEADS * half].reshape(r, B_HEADS, half), g[:, n + B_HEADS * half:].reshape(r, B_HEADS, half)]
    return jnp.concatenate(parts, axis=2).reshape(r, -1)


def _ukv_permute(w):
    r = w.shape[0]
    return jnp.transpose(w.reshape(r, B_HEADS, 2, B_NOPE), (0, 2, 1, 3)).reshape(r, -1)


def _ukv_unpermute(g):
    r = g.shape[0]
    return jnp.transpose(g.reshape(r, 2, B_HEADS, B_NOPE), (0, 2, 1, 3)).reshape(r, -1)


def _od_in_widen(w):
    n = C_HEADS * C_HEAD_DIM
    wide = lambda m: jnp.pad(m.reshape(-1, C_HEADS, C_HEAD_DIM), ((0, 0), (0, 0), (0, QK_PAD - C_HEAD_DIM))).reshape(m.shape[0], -1)
    return jnp.concatenate([wide(w[:, :n] * C_HEAD_DIM ** -0.5), wide(w[:, n:2 * n]), w[:, 2 * n:],
                            jnp.zeros((w.shape[0], ODD_IN_PAD - ODD_IN_AUG), w.dtype)], axis=1)


def _od_in_narrow(g):
    wp = C_HEADS * QK_PAD
    narrow = lambda m: m.reshape(-1, C_HEADS, QK_PAD)[:, :, :C_HEAD_DIM].reshape(m.shape[0], -1)
    return jnp.concatenate([narrow(g[:, :wp]) * C_HEAD_DIM ** -0.5, narrow(g[:, wp:2 * wp]), g[:, 2 * wp:ODD_IN_AUG]], axis=1)


def _misc_weights(G3):
    strip = G3[:, :, STRIP0:STRIP0 + STRIP_C]
    return {
        "od_w_in": _od_in_widen(_cols(G3[:, :, :OD_C])),
        "ev_w_in": jnp.pad(_cols(G3[:, :, OD_C:STRIP0]), ((0, 0), (0, EVEN_IN_PAD - EVEN_IN))),
        "ple_w_proj": [_cols(strip[:, i * PLE_DIM:(i + 1) * PLE_DIM]) for i in range(DEPTH)],
        "ev_w_ukv": _ukv_permute(_cols(strip[:, 512:640])),
        "ev_w_uq": _uq_permute(_cols(strip[:, 640:896, :96])),
    }


def _misc_grads(G):
    strip = jnp.concatenate([
        _blocks(G["ple_w_proj"][0], STRIP_C), _blocks(G["ple_w_proj"][1], STRIP_C), _blocks(_ukv_unpermute(G["ev_w_ukv"]), STRIP_C),
        jnp.pad(_blocks(_uq_unpermute(G["ev_w_uq"]), 96), ((0, 0), (0, 0), (0, STRIP_C - 96))),
        jnp.zeros((N_DEV, G3_ROWS - 896, STRIP_C), F32)], axis=1)
    return jnp.concatenate([_blocks(_od_in_narrow(G["od_w_in"]), OD_C), _blocks(G["ev_w_in"][:, :EVEN_IN], EV_C), strip,
                            jnp.zeros((N_DEV, G3_ROWS, G3_COLS - STRIP0 - STRIP_C), F32)], axis=2)


def _ffn_fwd(h, norm_w, W, f, i, tag, ride=None):
    job = ride() if ride else None
    res = _ffn_gate_up(h, norm_w, W["C"][i].reshape(2, 4, C_ROWS, FF_BLK), f, f"{tag}_gate_up", job=job)
    n, gu, act = res[:3]
    if job is not None:
        ride(res[3:])
    job = ride() if ride else None
    out = _ffn_down(act, W["A"][i], f, h, f"{tag}_down", job=job)
    if job is not None:
        out, got = out
        ride(got)
    return out, (h, n, gu, act)


def _ffn_bwd(dout, saved, norm_w, W, GB, f, i, tag, ride=None):
    h, n, gu, act = saved
    S = h.shape[0]
    def carried(call):
        job = ride() if ride else None
        res = call(job)
        if job is None:
            return res
        ride(res[1])
        return res[0]

    GB["A"][i][f] = carried(lambda job: _ffn_down_dw(act, dout, f"{tag}_down_dw", job=job))
    dgu = _ffn_down_dx(dout, W["A"][i], f, gu, f"{tag}_down_dx").reshape(N_DEV, S, FF_BLK)
    res = carried(lambda job: _ffn_gate_up_dx(dgu, W["C"][i], f, h, norm_w, dout, f"{tag}_gate_up_dx", job=job))
    GB["C"][i][f] = carried(lambda job: _ffn_gate_up_dw(n, dgu, f"{tag}_gate_up_dw", job=job))
    return res


def _rope_tables(S):
    inv = ROPE_THETA ** (-jnp.arange(0, B_ROPE, 2, dtype=F32) / B_ROPE)
    ang = jnp.arange(S, dtype=F32)[:, None] * inv[None, :]
    return jnp.cos(ang), jnp.sin(ang)


def _alibi_columns(S):
    t = jnp.arange(S, dtype=jnp.int32)
    hi = ((t // 16) * 16).astype(F32)
    lo = (t % 16).astype(F32)
    slopes = 2.0 ** (-8.0 * jnp.arange(1, A_HEADS + 1, dtype=F32) / A_HEADS)
    zq = jnp.zeros((S, A_HEADS), F32)
    rest = QK_PAD - A_HEAD_DIM - 4
    qc = jnp.stack([-slopes[None, :] * hi[:, None], -slopes[None, :] * lo[:, None], zq + slopes[None, :], zq + slopes[None, :]] + [zq] * rest, axis=-1)
    one = jnp.ones((S, A_KV_HEADS), F32)
    zk = jnp.zeros((S, A_KV_HEADS), F32)
    kc = jnp.stack([one, one, zk + hi[:, None], zk + lo[:, None]] + [zk] * rest, axis=-1)
    return qc, kc


def _sink_prm(sinks):
    return jnp.zeros((A_HEADS, 1, LANES), F32).at[:, 0, 0].set(sinks.astype(F32))


def _with_ride(ride, call):
    job = ride() if ride else None
    res = call(job)
    if job is None:
        return res
    n_own = len(res) - len(job["outs"])
    ride(res[n_own:])
    return res[:n_own]


def _even_fwd(hn, h, W, ride=None):
    S = hn.shape[0]
    proj = _mm(hn, W["ev_w_in"], name="ev_in")
    a_q, a_k, a_v = proj[:, :512], proj[:, 512:640], proj[:, 640:768]
    c_q, c_kv = proj[:, 768:1024], proj[:, 1024:1152]
    kr1, kr2 = proj[:, 1152:1168], proj[:, 1168:1184]
    qc, kc = _alibi_columns(S)
    qaT = _cols_only(jnp.concatenate([(a_q * A_HEAD_DIM ** -0.5).reshape(S, A_HEADS, A_HEAD_DIM), qc], axis=-1))
    ka, kaT = _rows_and_cols(jnp.concatenate([a_k.reshape(S, A_KV_HEADS, A_HEAD_DIM), kc], axis=-1))
    va3 = a_v.reshape(S, A_KV_HEADS, A_HEAD_DIM)
    va = jnp.transpose(va3.astype(BF16), (1, 0, 2))
    prm = _sink_prm(W["ev_sinks"][0])
    oaT, lse_a = _with_ride(ride, lambda job: _attn_fwd(qaT, ka, _v_with_ones(va3), tile=min(SWA_TILE, S // 2), hb=2, window=WINDOW, sink=prm,
                                                        name="swa_fwd", job=job))
    cqn = _rms_fwd(c_q, W["ev_cq_norm"], "ev_cq_norm")
    q_all = _mm(cqn, W["ev_w_uq"], name="ev_uq")
    ckvn = _rms_fwd(c_kv, W["ev_ckv_norm"], "ev_ckv_norm")
    kv_all = _mm(ckvn, W["ev_w_ukv"], name="ev_ukv")
    cos, sin = _rope_tables(S)
    cos8, sin8 = jnp.tile(cos, (1, B_HEADS)), jnp.tile(sin, (1, B_HEADS))
    q1, q2 = _rope(q_all[:, 512:640], q_all[:, 640:768], cos8, sin8, "ev_rope_q")
    k1, k2 = _rope(kr1, kr2, cos, sin, "ev_rope_k")
    half = B_ROPE // 2
    scale = (B_NOPE + B_ROPE) ** -0.5
    qbT = _cols_only(jnp.concatenate([q_all[:, :512].reshape(S, B_HEADS, B_NOPE), q1.reshape(S, B_HEADS, half), q2.reshape(S, B_HEADS, half)], axis=-1) * scale)
    kro = jnp.broadcast_to(jnp.concatenate([k1, k2], axis=1)[:, None, :], (S, B_HEADS, B_ROPE))
    kb, kbT = _rows_and_cols(jnp.concatenate([kv_all[:, :512].reshape(S, B_HEADS, B_NOPE), kro], axis=-1))
    vb3 = kv_all[:, 512:].reshape(S, B_HEADS, B_V)
    vb = jnp.transpose(vb3.astype(BF16), (1, 0, 2))
    obT, lse_b = _with_ride(ride, lambda job: _attn_fwd(qbT, kb, _v_with_ones(vb3), tile=min(ATTN_TILE_FWD, S), hb=2, name="mla_fwd", job=job))
    cat = jnp.concatenate([_from_T(oaT), _from_T(obT)], axis=1)
    out = _mm_w128(cat, W["B"][0], MIX_OUT_BLK, res=h, name="ev_out")
    return out, (hn, proj, (qaT, ka, kaT, va, oaT, lse_a), prm, cqn, ckvn, (qbT, kb, kbT, vb, obT, lse_b), cat)


def _even_bwd(dout, saved, W, GB, norm):
    hn, proj, (qaT, ka, kaT, va, oaT, lse_a), prm, cqn, ckvn, (qbT, kb, kbT, vb, obT, lse_b), cat = saved
    S = hn.shape[0]
    G = {}
    dcat = _mm_w128(dout, W["B"][0], MIX_OUT_BLK, tb=True, out=BF16, name="ev_out_dx")
    GB["B"][0] = _mm_w128_dw(cat, dout, MIX_OUT_BLK, GB["B"][0], "ev_out_dw")
    doaT = _cols_only(dcat[:, :512].reshape(S, A_HEADS, A_HEAD_DIM))
    dqaT, dka, dva, dsink = _attn_bwd(qaT, ka, kaT, va, oaT, doaT, lse_a, tile=min(SWA_TILE, S // 2), hb=2, window=WINDOW, sink=prm, real=A_HEAD_DIM,
                                       name="swa_bwd")
    G["ev_sinks"] = dsink[:, 0, 0]
    dqa = _from_T(dqaT) * A_HEAD_DIM ** -0.5
    dka = dka.reshape(A_KV_HEADS, A_GROUP, S, A_HEAD_DIM).sum(axis=1)
    dva = dva.reshape(A_KV_HEADS, A_GROUP, S, A_HEAD_DIM).sum(axis=1)
    dobT = _cols_only(dcat[:, 512:].reshape(S, B_HEADS, B_V))
    dqbT, dkb, dvb = _attn_bwd(qbT, kb, kbT, vb, obT, dobT, lse_b, tile=min(ATTN_TILE, S), hb=1, name="mla_bwd")
    half = B_ROPE // 2
    dqb = jnp.transpose(dqbT, (2, 0, 1)) * (B_NOPE + B_ROPE) ** -0.5
    dkb = jnp.transpose(dkb, (1, 0, 2))
    cos, sin = _rope_tables(S)
    cos8, sin8 = jnp.tile(cos, (1, B_HEADS)), jnp.tile(sin, (1, B_HEADS))
    dq1, dq2 = _rope(dqb[:, :, B_NOPE:B_NOPE + half].reshape(S, -1), dqb[:, :, B_NOPE + half:].reshape(S, -1), cos8, -sin8, "ev_rope_q_bwd")
    dq_all = jnp.concatenate([dqb[:, :, :B_NOPE].reshape(S, -1), dq1, dq2], axis=1).astype(BF16)
    dkr = dkb[:, :, B_NOPE:].sum(axis=1)
    dk1, dk2 = _rope(dkr[:, :half], dkr[:, half:], cos, -sin, "ev_rope_k_bwd")
    dkv_all = jnp.concatenate([dkb[:, :, :B_NOPE].reshape(S, -1), _unheads(dvb)], axis=1).astype(BF16)
    G["ev_w_uq"] = _mm(cqn, dq_all, ta=True, name="ev_uq_dw")
    dcqn = _mm(dq_all, W["ev_w_uq"], tb=True, name="ev_uq_dx")
    dc_q, G["ev_cq_norm"] = _rms_bwd(dcqn, proj[:, 768:1024], W["ev_cq_norm"], None, "ev_cq_norm_bwd")
    G["ev_w_ukv"] = _mm(ckvn, dkv_all, ta=True, name="ev_ukv_dw")
    dckvn = _mm(dkv_all, W["ev_w_ukv"], tb=True, name="ev_ukv_dx")
    dc_kv, G["ev_ckv_norm"] = _rms_bwd(dckvn, proj[:, 1024:1152], W["ev_ckv_norm"], None, "ev_ckv_norm_bwd")
    dproj = jnp.concatenate([dqa, _unheads(dka), _unheads(dva), dc_q, dc_kv, dk1, dk2,
                             jnp.zeros((S, EVEN_IN_PAD - EVEN_IN), F32)], axis=1).astype(BF16)
    G["ev_w_in"] = _mm(hn, dproj, ta=True, name="ev_in_dw")
    dh, dnorm = _mm(dproj, W["ev_w_in"], tb=True, norm_bwd=(*norm, dout), name="ev_in_dx")
    return dh, dnorm, G


def _odd_fwd(hn, h, W, ride=None):
    S = hn.shape[0]
    w = C_HEADS * C_HEAD_DIM
    wp = C_HEADS * QK_PAD
    proj = _mm(hn, W["od_w_in"], name="od_in")
    f_logit = proj[:, 2 * wp + w: 2 * wp + w + C_HEADS]
    logf = _logsig_fwd(f_logit, W["od_b_f"], "od_logsig")
    logc = _cumsum(logf, False, "od_cumsum")
    parts = list(_exact3(logc))
    ones = [jnp.ones((S, C_HEADS), F32)] * 3
    pad = [jnp.zeros((S, C_HEADS), F32)] * (QK_PAD - C_HEAD_DIM - 6)
    lead = ((0, 0), (0, 0), (C_HEAD_DIM, 0))
    q3 = proj[:, :wp].reshape(S, C_HEADS, QK_PAD) + jnp.pad(jnp.stack(parts + ones + pad, axis=-1), lead)
    k3 = proj[:, wp:2 * wp].reshape(S, C_HEADS, QK_PAD) + jnp.pad(jnp.stack(ones + [-p for p in parts] + pad, axis=-1), lead)
    qT = _cols_only(q3)
    k, kT = _rows_and_cols(k3)
    v3 = proj[:, 2 * wp:2 * wp + w].reshape(S, C_HEADS, C_HEAD_DIM)
    v = jnp.transpose(v3.astype(BF16), (1, 0, 2))
    oT, lse = _with_ride(ride, lambda job: _attn_fwd(qT, k, _v_with_ones(v3), tile=min(ATTN_TILE_FWD, S), hb=2, name="fox_fwd", job=job))
    cat = _from_T(oT)
    out = _mm_w128(cat, W["B"][1], MIX_OUT_BLK, res=h, name="od_out")
    return out, (hn, qT, k, kT, v, f_logit, oT, lse, cat)


def _odd_bwd(dout, saved, W, GB, norm):
    hn, qT, k, kT, v, f_logit, oT, lse, cat = saved
    S = hn.shape[0]
    G = {}
    dcat = _mm_w128(dout, W["B"][1], MIX_OUT_BLK, tb=True, out=BF16, name="od_out_dx")
    GB["B"][1] = _mm_w128_dw(cat, dout, MIX_OUT_BLK, GB["B"][1], "od_out_dw")
    doT = _cols_only(dcat.reshape(S, C_HEADS, C_HEAD_DIM))
    dqT, dk, dv, dqxT, dkx = _attn_bwd(qT, k, kT, v, oT, doT, lse, tile=min(ATTN_TILE, S), hb=1, real=C_HEAD_DIM, extra=True,
                                       full=True, name="fox_bwd")
    dlogc = jnp.transpose(dqxT[:, 0, :] - dkx[:, :, 3])
    dlogf = _cumsum(dlogc, True, "od_cumsum_bwd")
    df, db = _logsig_bwd(dlogf, f_logit, W["od_b_f"], "od_logsig_bwd")
    G["od_b_f"] = db
    dproj = jnp.concatenate([_from_T(dqT), _unheads(dk), _unheads(dv), df, jnp.zeros((S, ODD_IN_PAD - ODD_IN_AUG), F32)], axis=1).astype(BF16)
    G["od_w_in"] = _mm(hn, dproj, ta=True, name="od_in_dw")
    dh, dnorm = _mm(dproj, W["od_w_in"], tb=True, norm_bwd=(*norm, dout), name="od_in_dx")
    return dh, dnorm, G


class _Rider:
    def __init__(self, steps, tag):
        self.steps, self.tag, self.count, self.result = steps, tag, 0, None
        self.job = next(steps)

    def __call__(self, got=None):
        if got is not None:
            return self._advance(list(got))
        job = self.job
        if isinstance(job, str):
            self._advance(None)
            return None
        return job

    def _advance(self, value):
        try:
            self.job = self.steps.send(value)
        except StopIteration as done:
            self.job, self.result = None, done.value

    def finish(self):
        while self.job is not None:
            if isinstance(self.job, str):
                self._advance(None)
                continue
            self.count += 1
            self(_comm_call(self.job, f"{self.tag}_{self.count}"))
        return self.result


def _gather_plan(W, slots):
    a0, b0, c0, m, a1, b1, c1 = (slots[key] for key in ("a0", "b0", "c0", "m", "a1", "b1", "c1"))
    (m,) = yield _gather_job([m])
    W.update(_misc_weights(m))
    (b0,) = yield _gather_job([b0])
    W["B"] = [b0]
    (c0,) = yield _gather_job([c0], rows=[(D_MODEL, D_MODEL)])
    W["C"] = [c0]
    a0, c1 = yield _gather_job([a0, c1], rows=[(DOWN_ROWS, DOWN_ROWS), (0, D_MODEL)])
    W["A"] = [a0]
    W["C"].append(c1)
    (a1,) = yield _gather_job([a1], rows=[(0, DOWN_ROWS)])
    W["A"].append(a1)
    for _ in range(3):
        yield "skip"
    a1, b1, c1 = yield _gather_job([a1, b1, c1], rows=[(DOWN_ROWS, DOWN_ROWS), None, (D_MODEL, D_MODEL)])
    W["A"][1], W["C"][1] = a1, c1
    W["B"].append(b1)


def _local_step(x, p, target, W, slots):
    h = x
    saved = []
    gather = _Rider(_gather_plan(W, slots), "all_gather_rest")
    for i in range(DEPTH):
        t = f"l{i}"
        h1, s_a = _ffn_fwd(h, W["ffa_norm"][i:i + 1], W, 0, i, f"{t}_ffa", gather)
        nm = _rms_fwd(h1, W["mix_norm"][i:i + 1], f"{t}_mix_norm")
        h2, s_m = (_even_fwd if i % 2 == 0 else _odd_fwd)(nm, h1, W, gather)
        h3, s_b = _ffn_fwd(h2, W["ffb_norm"][i:i + 1], W, 1, i, f"{t}_ffb", gather)
        npl = _rms_fwd(h3, W["ple_norm"][i:i + 1], f"{t}_ple_norm")
        gpre = _mm_w128(npl, W["B"][i], PLE_GATE_BLK, name=f"{t}_ple_gate")
        pp = _mm(p[i], W["ple_w_proj"][i], name=f"{t}_ple_proj")
        h4 = _ple_fwd(h3, gpre, pp, f"{t}_ple")
        saved.append((s_a, h1, s_m, s_b, h3, npl, gpre, pp))
        h = h4
    gather.finish()
    dh, g_final, loss_cols = _final_fwd_bwd(h, W["final_norm"], target, "final")
    G = {"final_norm": g_final}
    GB = {"A": [[None, None] for _ in range(DEPTH)], "C": [[None, None] for _ in range(DEPTH)],
          "B": [lax.empty((N_DEV, B_ROWS, D_MODEL), BF16) for _ in range(DEPTH)]}
    per_layer = {n: [None] * DEPTH for n in ("ffa_norm", "mix_norm", "ffb_norm", "ple_norm", "ple_w_proj")}
    scatter = scatter_mid = None
    for i in reversed(range(DEPTH)):
        t = f"l{i}"
        s_a, h1, s_m, s_b, h3, npl, gpre, pp = saved[i]
        dgpre, dpp = _ple_bwd(dh, gpre, pp, f"{t}_ple_bwd")
        per_layer["ple_w_proj"][i] = _mm(p[i], dpp, ta=True, name=f"{t}_ple_proj_dw")
        GB["B"][i] = _mm_w128_dw(npl, dgpre, PLE_GATE_BLK, GB["B"][i], f"{t}_ple_gate_dw")
        dh, per_layer["ple_norm"][i] = _mm_w128(dgpre, W["B"][i], PLE_GATE_BLK, tb=True, norm_bwd=(h3, W["ple_norm"][i:i + 1], dh),
                                                name=f"{t}_ple_gate_dx")
        dh, per_layer["ffb_norm"][i] = _ffn_bwd(dh, s_b, W["ffb_norm"][i:i + 1], W, GB, 1, i, f"{t}_ffb", scatter)
        dh, per_layer["mix_norm"][i], g_mix = (_even_bwd if i % 2 == 0 else _odd_bwd)(dh, s_m, W, GB, (h1, W["mix_norm"][i:i + 1]))
        G.update(g_mix)
        if i == 0:
            G["ple_w_proj"] = per_layer["ple_w_proj"]
            mid = [GB["A"][0][1], GB["C"][0][1], GB["B"][0], _misc_grads(G).astype(BF16)]
            scatter_mid = _Rider(_reduce_scatter_steps(mid, "mid"), "rs_mid")
        dh, per_layer["ffa_norm"][i] = _ffn_bwd(dh, s_a, W["ffa_norm"][i:i + 1], W, GB, 0, i, f"{t}_ffa", scatter_mid)
        if i == DEPTH - 1:
            later = [GB["A"][i][0], GB["A"][i][1], GB["C"][i][0], GB["C"][i][1], GB["B"][i]]
            scatter = _Rider(_reduce_scatter_steps(later, "later"), "rs_later")
    for n in ("ffa_norm", "mix_norm", "ffb_norm", "ple_norm"):
        G[n] = jnp.concatenate(per_layer[n], axis=0)
    return loss_cols, dh, scatter.finish(), scatter_mid.finish(), [GB["A"][0][0], GB["C"][0][0]], G


def kernel(x, p, ffa_norm, ffa_w_gate_up, ffa_w_down, mix_norm, ffb_norm, ffb_w_gate_up, ffb_w_down, ple_norm, ple_w_gate, ple_w_proj, ev_w_in, ev_sinks, ev_cq_norm, ev_w_uq, ev_ckv_norm, ev_w_ukv, ev_w_out, od_w_in, od_b_f, od_w_out, final_norm, loss_target, m_ffa_norm, m_ffa_w_gate_up, m_ffa_w_down, m_mix_norm, m_ffb_norm, m_ffb_w_gate_up, m_ffb_w_down, m_ple_norm, m_ple_w_gate, m_ple_w_proj, m_ev_w_in, m_ev_sinks, m_ev_cq_norm, m_ev_w_uq, m_ev_ckv_norm, m_ev_w_ukv, m_ev_w_out, m_od_w_in, m_od_b_f, m_od_w_out, m_final_norm, v_ffa_norm, v_ffa_w_gate_up, v_ffa_w_down, v_mix_norm, v_ffb_norm, v_ffb_w_gate_up, v_ffb_w_down, v_ple_norm, v_ple_w_gate, v_ple_w_proj, v_ev_w_in, v_ev_sinks, v_ev_cq_norm, v_ev_w_uq, v_ev_ckv_norm, v_ev_w_ukv, v_ev_w_out, v_od_w_in, v_od_b_f, v_od_w_out, v_final_norm):
    given = dict(locals())
    w_in = {n: given[n] for n in WEIGHTS}

    layers, misc = _local_groups(w_in, BF16)
    (a0, b0, c0), (a1, b1, c1) = [[_in_slot(g) for g in layer] for layer in layers]
    a0, c0 = _comm_call(_gather_job([a0, c0], rows=[(0, DOWN_ROWS), (0, D_MODEL)]), "all_gather_first")
    W = {n: w_in[n] for n in SMALL}
    W["final_norm"] = final_norm.reshape(1, -1)
    W.update(A=[a0], C=[c0])
    slots = dict(a0=a0, b0=b0, c0=c0, m=_in_slot(misc), a1=a1, b1=b1, c1=c1)

    loss_cols, dx, r_later, r_mid, last, G = _local_step(x[0], p[:, 0], loss_target[0], W, slots)

    a1f, a1b, c1f, c1b, b1 = r_later
    a0b, c0b, b0, r_misc = r_mid
    a0f, c0f = _reduce_scatter(last, "last")
    grads = _ungroup_local([[a0f, a0b], [a1f, a1b]], [b0, b1], [[c0f, c0b], [c1f, c1b]], r_misc)
    layout = [(n, int(np.prod(w_in[n].shape))) for n in SMALL]
    vec = jnp.concatenate([G[n].astype(F32).reshape(-1) for n, _ in layout] + [jnp.sum(loss_cols).reshape(1)])
    vec = jnp.pad(vec, (0, N_DEV * SMALL_COLS - vec.shape[0])).reshape(N_DEV, SMALL_COLS)
    vec = _all_reduce_small(vec).reshape(-1)
    off = 0
    for n, size in layout:
        grads[n] = vec[off: off + size].reshape(w_in[n].shape)
        off += size
    loss = vec[off]

    delta, new_m, new_v = {}, {}, {}
    for n in WEIGHTS:
        shp = w_in[n].shape
        as2d = (lambda a: a.reshape(1, -1)) if len(shp) == 1 else (lambda a: a)
        d, nm, nv = _adamw(as2d(w_in[n]), as2d(grads[n]), as2d(given["m_" + n]), as2d(given["v_" + n]), f"adamw_{n}")
        delta[n], new_m[n], new_v[n] = d.reshape(shp), nm.reshape(shp), nv.reshape(shp)
    return (loss, dx[None], *[grads[n] for n in WEIGHTS], *[delta[n] for n in WEIGHTS],
            *[new_m[n] for n in WEIGHTS], *[new_v[n] for n in WEIGHTS])
```

```python
import functools

import numpy as np
import jax
import jax.numpy as jnp
from jax import lax
from jax.experimental import pallas as pl
from jax.experimental.pallas import tpu as pltpu

F32 = jnp.float32
BF16 = jnp.bfloat16
MESH = pl.DeviceIdType.MESH

D_MODEL = 1024
D_FF = 2816
RMS_EPS = 1e-6
PLE_DIM = 256
A_HEADS, A_KV_HEADS, A_HEAD_DIM, WINDOW = 8, 2, 64, 128
A_GROUP = A_HEADS // A_KV_HEADS
B_HEADS, B_Q_LORA, B_KV_LORA, B_NOPE, B_ROPE, B_V = 8, 256, 128, 64, 32, 64
ROPE_THETA = 10000.0
C_HEADS, C_HEAD_DIM = 16, 64
EVEN_IN = 1184
EVEN_IN_PAD = 1280
ODD_IN = 3088
ODD_IN_AUG = 2 * 16 * 80 + 1024 + 16
ODD_IN_PAD = 3840
DEPTH = 2
ADAM_LR, ADAM_B1, ADAM_B2, ADAM_EPS, ADAM_WD, ADAM_STEP = 0.001, 0.9, 0.999, 1e-08, 0.01, 10

N_DEV = 8
LANES = 128
SUBLANES = 8
EW_TILE_BYTES = 3 << 20
MM_VMEM_BYTES = 26 << 20
NEG = -1e30
ATTN_TILE = 1024
ATTN_TILE_FWD = 1024
SWA_TILE = 512
QK_PAD = 80

FF_BLK = D_FF // 4
DOWN_ROWS = D_FF // N_DEV
A_ROWS, B_ROWS, C_ROWS, G3_ROWS, G3_COLS = 2 * DOWN_ROWS, 256, 2 * D_MODEL, 1024, 768
PLE_GATE_BLK, MIX_OUT_BLK = 0, 1
OD_C, EV_C, STRIP_C = 386, 148, 128
STRIP0 = OD_C + EV_C

SMALL = ["ffa_norm", "mix_norm", "ffb_norm", "ple_norm", "ev_sinks", "ev_cq_norm", "ev_ckv_norm", "od_b_f", "final_norm"]
WEIGHTS = ["ffa_norm", "ffa_w_gate_up", "ffa_w_down", "mix_norm", "ffb_norm", "ffb_w_gate_up", "ffb_w_down", "ple_norm",
           "ple_w_gate", "ple_w_proj", "ev_w_in", "ev_sinks", "ev_cq_norm", "ev_w_uq", "ev_ckv_norm", "ev_w_ukv", "ev_w_out",
           "od_w_in", "od_b_f", "od_w_out", "final_norm"]
SMALL_COLS = 1280


def _divisor(n, cap, mult):
    if n <= cap:
        return n
    for t in range(cap - cap % mult, 0, -mult):
        if n % t == 0:
            return t
    raise ValueError(f"no tile for {n} under {cap} in steps of {mult}")


def _lanes(c):
    return -(-c // LANES) * LANES


def _ew(fn, rows, vecs, outs, reds=(), *, name):
    R = rows[0].shape[0]
    per_row = sum(_lanes(a.shape[1]) * a.dtype.itemsize for a in rows) + sum(_lanes(c) * jnp.dtype(d).itemsize for c, d in outs)
    tm = _divisor(R, max(16, EW_TILE_BYTES // per_row // 16 * 16), 16) if R % 16 == 0 else R
    n_r, n_v, n_o = len(rows), len(vecs), len(outs)

    def body(*refs):
        ins = [r[...] for r in refs[: n_r + n_v]]
        res = fn(*ins)
        if not isinstance(res, (tuple, list)):
            res = (res,)
        o_refs = refs[n_r + n_v: n_r + n_v + n_o]
        r_refs = refs[n_r + n_v + n_o:]
        for ref, val in zip(o_refs, res[:n_o]):
            ref[...] = val.astype(ref.dtype)
        if r_refs:
            @pl.when(pl.program_id(0) == 0)
            def _():
                for ref in r_refs:
                    ref[...] = jnp.zeros_like(ref)
            for ref, val in zip(r_refs, res[n_o:]):
                ref[...] += val

    in_specs = [pl.BlockSpec((tm, a.shape[1]), lambda i: (i, 0)) for a in rows]
    in_specs += [pl.BlockSpec((1, a.shape[1]), lambda i: (0, 0)) for a in vecs]
    out_specs = [pl.BlockSpec((tm, c), lambda i: (i, 0)) for c, _ in outs]
    out_specs += [pl.BlockSpec((1, c), lambda i: (0, 0)) for c in reds]
    out_shape = [jax.ShapeDtypeStruct((R, c), d) for c, d in outs] + [jax.ShapeDtypeStruct((1, c), F32) for c in reds]
    res = pl.pallas_call(body, name=name, grid=(R // tm,), in_specs=in_specs, out_specs=out_specs, out_shape=out_shape)(*rows, *vecs)
    return res[0] if len(res) == 1 else res


def _rms_fwd(x, w, name):
    def fn(x, w):
        y = x * lax.rsqrt(jnp.mean(x * x, axis=-1, keepdims=True) + RMS_EPS)
        return y * w
    return _ew(fn, [x], [w], [(x.shape[1], BF16)], name=name)


def _rms_bwd(dn, x, w, dres, name):
    def fn(dn, x, *rest):
        w = rest[-1]
        r = lax.rsqrt(jnp.mean(x * x, axis=-1, keepdims=True) + RMS_EPS)
        xh = x * r
        gw = dn * w
        dx = r * (gw - xh * jnp.mean(gw * xh, axis=-1, keepdims=True))
        if len(rest) == 2:
            dx = dx + rest[0]
        return dx, jnp.sum(dn * xh, axis=0, keepdims=True)
    rows = [dn, x] + ([dres] if dres is not None else [])
    return _ew(fn, rows, [w], [(x.shape[1], F32)], [x.shape[1]], name=name)


def _ple_fwd(h, gpre, pp, name):
    return _ew(lambda h, g, q: h + jax.nn.sigmoid(g) * q, [h, gpre, pp], [], [(h.shape[1], F32)], name=name)


def _ple_bwd(dh, gpre, pp, name):
    def fn(dh, g, q):
        sg = jax.nn.sigmoid(g)
        return dh * q * (sg * (1.0 - sg)), dh * sg
    return _ew(fn, [dh, gpre, pp], [], [(dh.shape[1], BF16), (dh.shape[1], BF16)], name=name)


def _rope(x1, x2, cos, sin, name):
    c = x1.shape[1]
    return _ew(lambda a, b, co, si: (a * co - b * si, a * si + b * co), [x1, x2, cos, sin], [], [(c, F32), (c, F32)], name=name)


def _logsig_fwd(f, b, name):
    def fn(f, b):
        z = f + b
        return jnp.minimum(z, 0.0) - jnp.log(1.0 + jnp.exp(-jnp.abs(z)))
    return _ew(fn, [f], [b], [(f.shape[1], F32)], name=name)


def _logsig_bwd(dlogf, f, b, name):
    def fn(d, f, b):
        df = d * jax.nn.sigmoid(-(f + b))
        return df, jnp.sum(df, axis=0, keepdims=True)
    return _ew(fn, [dlogf, f], [b], [(f.shape[1], F32)], [f.shape[1]], name=name)


def _final_fwd_bwd(h, w, target, name):
    d = h.shape[1]

    def fn(h, t, w):
        r = lax.rsqrt(jnp.mean(h * h, axis=-1, keepdims=True) + RMS_EPS)
        xh = h * r
        y = xh * w
        err = y - t
        dy = err * (1.0 / d)
        gw = dy * w
        dx = r * (gw - xh * jnp.mean(gw * xh, axis=-1, keepdims=True))
        return dx, jnp.sum(dy * xh, axis=0, keepdims=True), jnp.sum(err * err, axis=0, keepdims=True) * (0.5 / d)
    return _ew(fn, [h, target], [w], [(d, F32)], [d, d], name=name)


def _adamw(w, g, m, v, name):
    shape = w.shape
    c = shape[-1]
    w2, g2, m2, v2 = (a.reshape(-1, c) for a in (w, g, m, v))

    def fn(w, g, m, v):
        m = ADAM_B1 * m + (1.0 - ADAM_B1) * g
        v = ADAM_B2 * v + (1.0 - ADAM_B2) * jnp.square(g)
        m_hat = m / (1.0 - ADAM_B1 ** ADAM_STEP)
        v_hat = v / (1.0 - ADAM_B2 ** ADAM_STEP)
        delta = -ADAM_LR * (m_hat / (jnp.sqrt(v_hat) + ADAM_EPS) + ADAM_WD * w)
        return delta, m, v
    d, nm, nv = _ew(fn, [w2, g2, m2, v2], [], [(c, F32)] * 3, name=name)
    return d.reshape(shape), nm.reshape(shape), nv.reshape(shape)


def _split3(v):
    hi = v.astype(BF16)
    r1 = v - hi.astype(F32)
    mid = r1.astype(BF16)
    lo = (r1 - mid.astype(F32)).astype(BF16)
    return hi, mid, lo


def _cumsum(x, reverse, name):
    S, C = x.shape
    tm = _divisor(S, 512, 16)
    nt = S // tm

    def body(x_ref, o_ref, carry):
        @pl.when(pl.program_id(0) == 0)
        def _():
            carry[...] = jnp.zeros_like(carry)
        r = lax.broadcasted_iota(jnp.int32, (tm, tm), 0)
        c = lax.broadcasted_iota(jnp.int32, (tm, tm), 1)
        tri = jnp.where((c >= r) if reverse else (c <= r), 1.0, 0.0).astype(BF16)
        xv = x_ref[...]
        acc = jnp.zeros((tm, C), F32)
        for part in _split3(xv):
            acc = acc + jnp.dot(tri, part, preferred_element_type=F32)
        o_ref[...] = acc + carry[...]
        carry[...] += jnp.sum(xv, axis=0, keepdims=True)

    idx = (lambda i: (nt - 1 - i, 0)) if reverse else (lambda i: (i, 0))
    return pl.pallas_call(
        body, name=name, grid=(nt,), in_specs=[pl.BlockSpec((tm, C), idx)], out_specs=pl.BlockSpec((tm, C), idx),
        out_shape=jax.ShapeDtypeStruct((S, C), F32), scratch_shapes=[pltpu.VMEM((1, C), F32)],
    )(x)


NN = (((1,), (0,)), ((), ()))
NT = (((1,), (1,)), ((), ()))
TN = (((0,), (0,)), ((), ()))

HBM_SPEC = pl.BlockSpec(memory_space=pl.ANY)


def _job_in_body(job, refs, n_in, n_out, n_scr, grid):
    if job is None:
        return refs[n_in:], lambda: None
    ji, jo = len(job["ins"]), len(job["outs"])
    j_in = refs[n_in: n_in + ji]
    pos = n_in + ji
    own = list(refs[pos: pos + n_out])
    pos += n_out
    j_out = refs[pos: pos + jo]
    pos += jo
    own += list(refs[pos: pos + n_scr])
    ss, rs = refs[-2], refs[-1]
    first = functools.reduce(jnp.logical_and, [pl.program_id(d) == 0 for d in range(len(grid))])
    last = functools.reduce(jnp.logical_and, [pl.program_id(d) == n - 1 for d, n in enumerate(grid)])

    @pl.when(first)
    def _():
        job["start"](j_in, j_out, ss, rs)

    def finish():
        @pl.when(last)
        def _():
            job["finish"](j_in, j_out, ss, rs)

    return own, finish


def _job_call(job, body, *, name, grid, in_specs, out_specs, out_shape, args, scratch_shapes, aliases, dimension_semantics):
    in_specs, out_specs, out_shape, args, scratch_shapes = list(in_specs), list(out_specs), list(out_shape), list(args), list(scratch_shapes)
    aliases = dict(aliases)
    if job is not None:
        for i, o in job["aliases"].items():
            aliases[len(args) + i] = len(out_shape) + o
        in_specs += [HBM_SPEC] * len(job["ins"])
        args += list(job["ins"])
        out_specs += [HBM_SPEC] * len(job["outs"])
        out_shape += list(job["outs"])
        scratch_shapes += [pltpu.SemaphoreType.DMA((job["n_sems"],)), pltpu.SemaphoreType.DMA((job["n_sems"],))]
    return pl.pallas_call(
        body, name=name, grid=grid, in_specs=in_specs, out_specs=out_specs, out_shape=out_shape,
        scratch_shapes=scratch_shapes, input_output_aliases=aliases,
        compiler_params=pltpu.CompilerParams(dimension_semantics=dimension_semantics),
    )(*args)


def _comm_call(job, name):
    def body(*refs):
        ji, jo = len(job["ins"]), len(job["outs"])
        job["start"](refs[:ji], refs[ji: ji + jo], refs[-2], refs[-1])
        job["finish"](refs[:ji], refs[ji: ji + jo], refs[-2], refs[-1])

    return pl.pallas_call(
        body, name=name, in_specs=[HBM_SPEC] * len(job["ins"]), out_specs=[HBM_SPEC] * len(job["outs"]), out_shape=list(job["outs"]),
        input_output_aliases=dict(job["aliases"]),
        scratch_shapes=[pltpu.SemaphoreType.DMA((job["n_sems"],)), pltpu.SemaphoreType.DMA((job["n_sems"],))],
    )(*job["ins"])


def _mm_call(name, grid, k_axis, a, a_spec, a2d, b, b_spec, b2d, dims, out_sds, out_spec, o2d, *,
             alpha=1.0, res=None, res_spec=None, into=None, job=None, norm_bwd=None):
    nk = grid[k_axis]
    n_in = 2 + (res is not None) + (into is not None) + (3 if norm_bwd is not None else 0)
    n_out = 2 if norm_bwd is not None else 1

    def body(*refs):
        a_ref, b_ref = refs[0], refs[1]
        res_ref = refs[2] if res is not None else None
        own, finish_job = _job_in_body(job, refs, n_in, n_out, 1, grid)
        o_ref, acc_ref = own[0], own[-1]
        k = pl.program_id(k_axis)

        @pl.when(k == 0)
        def _():
            acc_ref[...] = jnp.zeros_like(acc_ref)

        if norm_bwd is not None:
            x_ref, w_ref, dres_ref = refs[n_in - 3: n_in]
            dw_ref = own[1]

            @pl.when(functools.reduce(jnp.logical_and, [pl.program_id(d) == 0 for d in range(len(grid))]))
            def _():
                dw_ref[...] = jnp.zeros_like(dw_ref)

        av = a_ref[...].reshape(a2d).astype(BF16)
        bv = b_ref[...].reshape(b2d).astype(BF16)
        acc_ref[...] += lax.dot_general(av, bv, dims, preferred_element_type=F32)

        @pl.when(k == nk - 1)
        def _():
            r = acc_ref[...]
            if alpha != 1.0:
                r = r * alpha
            if res_ref is not None:
                r = res_ref[...].reshape(o2d) + r
            if norm_bwd is not None:
                x = x_ref[...]
                rs = lax.rsqrt(jnp.mean(x * x, axis=-1, keepdims=True) + RMS_EPS)
                xh = x * rs
                gw = r * w_ref[...]
                dw_ref[...] += jnp.sum(r * xh, axis=0, keepdims=True)
                r = dres_ref[...] + rs * (gw - xh * jnp.mean(gw * xh, axis=-1, keepdims=True))
            o_ref[...] = r.reshape(o_ref.shape).astype(o_ref.dtype)

        finish_job()

    in_specs, args = [a_spec, b_spec], [a, b]
    if res is not None:
        in_specs.append(res_spec)
        args.append(res)
    aliases = {}
    if into is not None:
        aliases = {len(args): 0}
        in_specs.append(pl.BlockSpec(memory_space=pl.ANY))
        args.append(into)
        out_sds = jax.ShapeDtypeStruct(into.shape, into.dtype)
    out_specs, out_shape = [out_spec], [out_sds]
    if norm_bwd is not None:
        vec = pl.BlockSpec((1, o2d[1]), lambda *_: (0, 0))
        in_specs += [out_spec, vec, out_spec]
        args += list(norm_bwd)
        out_specs.append(vec)
        out_shape.append(jax.ShapeDtypeStruct((1, o2d[1]), F32))
    serial = job is not None or norm_bwd is not None
    sem = tuple("arbitrary" if d == k_axis or serial else "parallel" for d in range(len(grid)))
    res_all = _job_call(
        job, body, name=name, grid=grid, in_specs=in_specs, out_specs=out_specs, out_shape=out_shape, args=args,
        scratch_shapes=[pltpu.VMEM(o2d, F32)], aliases=aliases, dimension_semantics=sem)
    own = res_all[0] if n_out == 1 else tuple(res_all[:n_out])
    return own if job is None else (own, res_all[n_out:])


def _mm(a, b, *, ta=False, tb=False, out=F32, res=None, alpha=1.0, norm_bwd=None, name):
    K, M = a.shape if ta else a.shape[::-1]
    N = b.shape[0] if tb else b.shape[1]
    assert (b.shape[1] if tb else b.shape[0]) == K, (a.shape, b.shape, ta, tb)
    tk = _divisor(K, 1024, LANES)
    tn = _divisor(N, 1408, LANES)
    assert norm_bwd is None or tn == N
    for cap in (1024, 512, 256, 128):
        tm = _divisor(M, cap, LANES if ta else 16)
        est = 2 * (tm * tk * a.dtype.itemsize + tk * tn * b.dtype.itemsize + tm * tn * jnp.dtype(out).itemsize)
        est += tm * tn * 4 + (2 * tm * tn * 4 if res is not None else 0) + (4 * tm * tn * 4 if norm_bwd is not None else 0)
        if est <= MM_VMEM_BYTES:
            break
    a_spec = pl.BlockSpec((tk, tm), lambda i, j, k: (k, i)) if ta else pl.BlockSpec((tm, tk), lambda i, j, k: (i, k))
    b_spec = pl.BlockSpec((tn, tk), lambda i, j, k: (j, k)) if tb else pl.BlockSpec((tk, tn), lambda i, j, k: (k, j))
    o_spec = pl.BlockSpec((tm, tn), lambda i, j, k: (i, j))
    dims = (((0 if ta else 1,), (1 if tb else 0,)), ((), ()))
    return _mm_call(name, (M // tm, N // tn, K // tk), 2, a, a_spec, (tk, tm) if ta else (tm, tk), b, b_spec,
                    (tn, tk) if tb else (tk, tn), dims, jax.ShapeDtypeStruct((M, N), out), o_spec, (tm, tn),
                    alpha=alpha, res=res, res_spec=o_spec, norm_bwd=norm_bwd)


def _w128_spec(blk):
    return pl.BlockSpec((N_DEV, 128, D_MODEL), lambda *_: (0, blk, 0))


def _mm_w128(a, G1, blk, *, tb=False, res=None, out=F32, norm_bwd=None, name):
    S = a.shape[0]
    tm = _divisor(S, 1024 if norm_bwd is None else 512, 16)
    row = pl.BlockSpec((tm, D_MODEL), lambda i, k: (i, 0))
    return _mm_call(name, (S // tm, 1), 1, a, row, (tm, D_MODEL), G1, _w128_spec(blk), (D_MODEL, D_MODEL), NT if tb else NN,
                    jax.ShapeDtypeStruct((S, D_MODEL), out), row, (tm, D_MODEL), res=res, res_spec=row, norm_bwd=norm_bwd)


def _mm_w128_dw(a, b, blk, into, name):
    S = a.shape[0]
    tk = _divisor(S, 1024, 16)
    row = pl.BlockSpec((tk, D_MODEL), lambda i, k: (k, 0))
    return _mm_call(name, (1, S // tk), 1, a, row, (tk, D_MODEL), b, row, (tk, D_MODEL), TN, None, _w128_spec(blk),
                    (D_MODEL, D_MODEL), into=into)


def _ffn_gate_up(h, norm_w, G2v, rb, name, job=None):
    S = h.shape[0]
    tm = _divisor(S, 1024, 16)
    grid = (S // tm, 4)

    def body(*refs):
        h_ref, nw_ref, w_ref = refs[:3]
        (n_ref, gu_ref, act_ref, n_scr), finish_job = _job_in_body(job, refs, 3, 3, 1, grid)

        @pl.when(pl.program_id(1) == 0)
        def _():
            x = h_ref[...]
            y = x * lax.rsqrt(jnp.mean(x * x, axis=-1, keepdims=True) + RMS_EPS)
            n_scr[...] = (y * nw_ref[...]).astype(BF16)
            n_ref[...] = n_scr[...]

        nv = n_scr[...]
        g = jnp.dot(nv, w_ref[0, 0], preferred_element_type=F32)
        u = jnp.dot(nv, w_ref[1, 0], preferred_element_type=F32)
        sg = jax.nn.sigmoid(g)
        silu = g * sg
        gu_ref[0, 0] = (u * (sg * (1.0 + g * (1.0 - sg)))).astype(BF16)
        gu_ref[1, 0] = silu.astype(BF16)
        act_ref[0] = (silu * u).astype(BF16)
        finish_job()

    row = pl.BlockSpec((tm, D_MODEL), lambda i, j: (i, 0))
    return _job_call(
        job, body, name=name, grid=grid,
        in_specs=[row, pl.BlockSpec((1, D_MODEL), lambda i, j: (0, 0)), pl.BlockSpec((2, 1, D_MODEL, FF_BLK), lambda i, j: (0, j, rb, 0))],
        out_specs=[row, pl.BlockSpec((2, 1, tm, FF_BLK), lambda i, j: (0, j, i, 0)), pl.BlockSpec((1, tm, FF_BLK), lambda i, j: (j, i, 0))],
        out_shape=[jax.ShapeDtypeStruct((S, D_MODEL), BF16), jax.ShapeDtypeStruct((2, 4, S, FF_BLK), BF16), jax.ShapeDtypeStruct((4, S, FF_BLK), BF16)],
        args=[h, norm_w, G2v], scratch_shapes=[pltpu.VMEM((tm, D_MODEL), BF16)], aliases={},
        dimension_semantics=("arbitrary" if job is not None else "parallel", "arbitrary"))


def _ffn_down(act, G1, ob, h, name, job=None):
    S = h.shape[0]
    tm = _divisor(S, 1024, 16)
    row = pl.BlockSpec((tm, D_MODEL), lambda i, k: (i, 0))
    return _mm_call(name, (S // tm, 4), 1, act, pl.BlockSpec((1, tm, FF_BLK), lambda i, k: (k, i, 0)), (tm, FF_BLK),
                    G1, pl.BlockSpec((2, DOWN_ROWS, D_MODEL), lambda i, k: (k, ob, 0)), (FF_BLK, D_MODEL), NN,
                    jax.ShapeDtypeStruct((S, D_MODEL), F32), row, (tm, D_MODEL), alpha=0.5, res=h, res_spec=row, job=job)


def _ffn_down_dx(dh, G1, ob, gu, name):
    S = dh.shape[0]
    tm = _divisor(S, 1024, 16)

    def body(dh_ref, w_ref, gu_ref, o_ref):
        w = w_ref[...].reshape(FF_BLK, D_MODEL)
        dact = lax.dot_general(dh_ref[...].astype(BF16), w, NT, preferred_element_type=F32) * 0.5
        o_ref[0, 0] = (dact * gu_ref[0, 0].astype(F32)).astype(BF16)
        o_ref[1, 0] = (dact * gu_ref[1, 0].astype(F32)).astype(BF16)

    blk = pl.BlockSpec((2, 1, tm, FF_BLK), lambda i, j: (0, j, i, 0))
    return pl.pallas_call(
        body, name=name, grid=(S // tm, 4),
        in_specs=[pl.BlockSpec((tm, D_MODEL), lambda i, j: (i, 0)), pl.BlockSpec((2, DOWN_ROWS, D_MODEL), lambda i, j: (j, ob, 0)), blk],
        out_specs=blk, out_shape=jax.ShapeDtypeStruct((2, 4, S, FF_BLK), BF16),
    )(dh, G1, gu)


def _ffn_down_dw(act, dh, name, job=None):
    S = dh.shape[0]
    tk = _divisor(S, 1024, 16)
    return _mm_call(name, (4, S // tk), 1, act, pl.BlockSpec((1, tk, FF_BLK), lambda j, k: (j, k, 0)), (tk, FF_BLK),
                    dh, pl.BlockSpec((tk, D_MODEL), lambda j, k: (k, 0)), (tk, D_MODEL), TN,
                    jax.ShapeDtypeStruct((N_DEV, DOWN_ROWS, D_MODEL), BF16),
                    pl.BlockSpec((2, DOWN_ROWS, D_MODEL), lambda j, k: (j, 0, 0)), (FF_BLK, D_MODEL), alpha=0.5, job=job)


def _ffn_gate_up_dw(n, dgu8, name, job=None):
    S = n.shape[0]
    tk = _divisor(S, 1024, 16)
    return _mm_call(name, (N_DEV, S // tk), 1, n, pl.BlockSpec((tk, D_MODEL), lambda b, k: (k, 0)), (tk, D_MODEL),
                    dgu8, pl.BlockSpec((1, tk, FF_BLK), lambda b, k: (b, k, 0)), (tk, FF_BLK), TN,
                    jax.ShapeDtypeStruct((N_DEV, D_MODEL, FF_BLK), BF16),
                    pl.BlockSpec((1, D_MODEL, FF_BLK), lambda b, k: (b, 0, 0)), (D_MODEL, FF_BLK), job=job)


def _ffn_gate_up_dx(dgu8, G2, rb, h, norm_w, dres, name, job=None):
    S = h.shape[0]
    tm = _divisor(S, 1024, 16)
    row = pl.BlockSpec((tm, D_MODEL), lambda i, k: (i, 0))
    return _mm_call(name, (S // tm, N_DEV), 1, dgu8, pl.BlockSpec((1, tm, FF_BLK), lambda i, k: (k, i, 0)), (tm, FF_BLK),
                    G2, pl.BlockSpec((1, D_MODEL, FF_BLK), lambda i, k: (k, rb, 0)), (D_MODEL, FF_BLK), NT,
                    jax.ShapeDtypeStruct((S, D_MODEL), F32), row, (tm, D_MODEL), norm_bwd=(h, norm_w, dres), job=job)


def _unheads(x):
    h, S, d = x.shape
    return jnp.transpose(x, (1, 0, 2)).reshape(S, h * d)


def _exact3(v):
    rnd = lambda a: lax.reduce_precision(a, exponent_bits=8, mantissa_bits=7)
    hi = rnd(v)
    mid = rnd(v - hi)
    return hi, mid, rnd(v - hi - mid)


def _causal_mask(st, q0, k0, window):
    dist = (q0 + lax.broadcasted_iota(jnp.int32, st.shape, 1)) - (k0 + lax.broadcasted_iota(jnp.int32, st.shape, 0))
    mask = dist >= 0
    if window is not None:
        mask = mask & (dist < window)
    return jnp.where(mask, st, NEG)


def _attn_fwd(qT, k, vT1, *, tile, hb, window=None, sink=None, name, job=None):
    H, dqk, S = qT.shape
    G = H // k.shape[0]
    dvp = vT1.shape[1]
    dv = dvp - 16
    tq = tk = tile
    assert H % hb == 0 and (G == 1 or G % hb == 0)
    kvb = hb if G == 1 else 1
    grid = (H // hb, S // tq)
    n_in = 3 + (sink is not None)

    def body(*refs):
        q_ref, k_ref, v_ref = refs[:3]
        (o_ref, lse_ref), finish_job = _job_in_body(job, refs, n_in, 2, 0, grid)
        i = pl.program_id(1)
        carry = []
        for a in range(hb):
            if sink is not None:
                carry.append(jnp.zeros((1, tq), F32) + refs[3][a, :, 0:1])
                carry.append(jnp.where(lax.broadcasted_iota(jnp.int32, (dvp, tq), 0) == dv, 1.0, 0.0))
            else:
                carry.append(jnp.full((1, tq), NEG, F32))
                carry.append(jnp.zeros((dvp, tq), F32))

        def step(j, carry, masked, off=None, keys=tk, q_from=0):
            off = pl.multiple_of(j * tk, tk) if off is None else off
            out = []
            for a in range(hb):
                m, acc = carry[2 * a], carry[2 * a + 1]
                kv = a if kvb > 1 else 0
                st = jnp.dot(k_ref[kv, pl.ds(off, keys), :], q_ref[a][:, q_from:], preferred_element_type=F32)
                if masked:
                    st = _causal_mask(st, i * tq + q_from, off, window)
                m_old, acc_old = m[:, q_from:], acc[:, q_from:]
                m_new = jnp.maximum(m_old, jnp.max(st, axis=0, keepdims=True))
                pt = jnp.exp(st - m_new).astype(BF16)
                acc_new = jnp.exp(m_old - m_new) * acc_old + jnp.dot(v_ref[kv, :, pl.ds(off, keys)], pt, preferred_element_type=F32)
                if q_from:
                    m_new = jnp.concatenate([m[:, :q_from], m_new], axis=1)
                    acc_new = jnp.concatenate([acc[:, :q_from], acc_new], axis=1)
                out += [m_new, acc_new]
            return tuple(out)

        carry = tuple(carry)
        if window is None:
            carry = lax.fori_loop(0, i, functools.partial(step, masked=False), carry)
            if tq % (2 * LANES) == 0:
                half = tq // 2
                carry = step(None, carry, True, off=pl.multiple_of(i * tq, tq), keys=half)
                carry = step(None, carry, True, off=pl.multiple_of(i * tq + half, half), keys=half, q_from=half)
            else:
                carry = step(i, carry, True)
        else:
            assert window % LANES == 0 and tq + window <= S
            carry = step(None, carry, True, off=pl.multiple_of(jnp.maximum(i * tq - window, 0), LANES), keys=tq + window)
        for a in range(hb):
            m, acc = carry[2 * a], carry[2 * a + 1]
            l = acc[dv:dv + 1, :]
            o_ref[a] = acc[:dv, :] / l
            lse_ref[a] = m + jnp.log(l)
        finish_job()

    kv_idx = (lambda b: b) if G == 1 else (lambda b: (b * hb) // G)
    in_specs = [
        pl.BlockSpec((hb, dqk, tq), lambda b, i: (b, 0, i)),
        pl.BlockSpec((kvb, S, dqk), lambda b, i: (kv_idx(b), 0, 0)),
        pl.BlockSpec((kvb, dvp, S), lambda b, i: (kv_idx(b), 0, 0)),
    ]
    args = [qT, k, vT1]
    if sink is not None:
        in_specs += [pl.BlockSpec((hb, 1, LANES), lambda b, i: (b, 0, 0))]
        args += [sink]
    return _job_call(
        job, body, name=name, grid=grid, in_specs=in_specs,
        out_specs=[pl.BlockSpec((hb, dv, tq), lambda b, i: (b, 0, i)), pl.BlockSpec((hb, 1, tq), lambda b, i: (b, 0, i))],
        out_shape=[jax.ShapeDtypeStruct((H, dv, S), F32), jax.ShapeDtypeStruct((H, 1, S), F32)],
        args=args, scratch_shapes=[], aliases={}, dimension_semantics=("arbitrary", "arbitrary") if job is not None else ("parallel", "parallel"))


def _attn_bwd(qT, k, kT, v, oT, doT, lse, *, tile, hb, window=None, sink=None, real=None, extra=False, full=False, name):
    H, dqk, S = qT.shape
    G = H // k.shape[0]
    dv = v.shape[2]
    tq = tk = tile
    nq = S // tq
    has_p = sink is not None
    real = dqk if real is None else real
    main = dqk if full else real
    assert H % hb == 0 and (G == 1 or G % hb == 0) and not (extra and real == dqk)
    kvb = hb if G == 1 else 1

    def body(*refs):
        qT_ref, k_ref, kT_ref, v_ref, oT_ref, doT_ref, lse_ref = refs[:7]
        p_ref = refs[7] if has_p else None
        pos = 8 if has_p else 7
        dq_ref, dk_ref, dv_ref = refs[pos: pos + 3]
        pos += 3
        ds_ref = refs[pos] if has_p else None
        pos += has_p
        dqx_ref, dkx_ref = (refs[pos], refs[pos + 1]) if extra else (None, None)
        delta = refs[-1]
        j = pl.program_id(1)

        @pl.when(j == 0)
        def _():
            dq_ref[...] = jnp.zeros_like(dq_ref)
            if extra:
                dqx_ref[...] = jnp.zeros_like(dqx_ref)
            for a in range(hb):
                drow = jnp.sum(doT_ref[a].astype(F32) * oT_ref[a], axis=0, keepdims=True)
                delta[a] = drow
                if has_p:
                    w = jnp.exp(p_ref[a, :, 0:1] - lse_ref[a])
                    ds_ref[a] = jnp.zeros((1, LANES), F32) - jnp.sum(w * drow, axis=1, keepdims=True)

        def step(i, carry, masked, off=None, qs=tq, keys=tk):
            off = pl.multiple_of(i * tq, tq) if off is None else off
            out = []
            for a in range(hb):
                dk, dvv = carry[2 * a], carry[2 * a + 1]
                kv = a if kvb > 1 else 0
                qTi = qT_ref[a, :, pl.ds(off, qs)]
                doTi = doT_ref[a, :, pl.ds(off, qs)]
                st = jnp.dot(k_ref[kv, pl.ds(0, keys), :], qTi, preferred_element_type=F32)
                if masked:
                    st = _causal_mask(st, off, j * tk, window)
                pt = jnp.exp(st - lse_ref[a, :, pl.ds(off, qs)])
                dv_new = lax.dot_general(pt.astype(BF16), doTi, NT, preferred_element_type=F32)
                dpt = jnp.dot(v_ref[kv, pl.ds(0, keys), :], doTi, preferred_element_type=F32)
                dsb = (pt * (dpt - delta[a, :, pl.ds(off, qs)])).astype(BF16)
                dk_new = lax.dot_general(dsb, qTi, NT, preferred_element_type=F32)
                if keys < tk:
                    dk = jnp.concatenate([dk[:keys] + dk_new, dk[keys:]], axis=0)
                    dvv = jnp.concatenate([dvv[:keys] + dv_new, dvv[keys:]], axis=0)
                else:
                    dk, dvv = dk + dk_new, dvv + dv_new
                dqt = jnp.dot(kT_ref[kv, :, pl.ds(0, keys)], dsb, preferred_element_type=F32)
                dq_ref[a, :, pl.ds(off, qs)] += dqt[:main]
                if extra:
                    dqx_ref[a, :, pl.ds(off, qs)] += dqt[real:]
                out += [dk, dvv]
            return tuple(out)

        carry = (jnp.zeros((tk, dqk), F32), jnp.zeros((tk, dv), F32)) * hb
        if window is None:
            if tk % (2 * LANES) == 0:
                half = tk // 2
                carry = step(None, carry, True, off=pl.multiple_of(j * tk + half, half), qs=half)
                carry = step(None, carry, True, off=pl.multiple_of(j * tk, tk), qs=half, keys=half)
            else:
                carry = step(j, carry, True)
            carry = lax.fori_loop(j + 1, nq, functools.partial(step, masked=False), carry)
        else:
            assert window % LANES == 0 and tk + window <= S
            carry = step(None, carry, True, off=pl.multiple_of(jnp.minimum(j * tk, S - (tk + window)), LANES), qs=tk + window)
        for a in range(hb):
            dk_ref[a] = carry[2 * a][:, :main]
            if extra:
                dkx_ref[a] = carry[2 * a][:, real:]
            dv_ref[a] = carry[2 * a + 1]

    kv_idx = (lambda b: b) if G == 1 else (lambda b: (b * hb) // G)
    colsT = lambda d: pl.BlockSpec((hb, d, S), lambda b, j: (b, 0, 0))
    in_specs = [
        colsT(dqk),
        pl.BlockSpec((kvb, tk, dqk), lambda b, j: (kv_idx(b), j, 0)),
        pl.BlockSpec((kvb, dqk, tk), lambda b, j: (kv_idx(b), 0, j)),
        pl.BlockSpec((kvb, tk, dv), lambda b, j: (kv_idx(b), j, 0)),
        colsT(dv), colsT(dv),
        pl.BlockSpec((hb, 1, S), lambda b, j: (b, 0, 0)),
    ]
    args = [qT, k, kT, v, oT, doT, lse]
    if has_p:
        in_specs += [pl.BlockSpec((hb, 1, LANES), lambda b, j: (b, 0, 0))]
        args += [sink]
    out_specs = [colsT(main), pl.BlockSpec((hb, tk, main), lambda b, j: (b, j, 0)), pl.BlockSpec((hb, tk, dv), lambda b, j: (b, j, 0))]
    out_shape = [jax.ShapeDtypeStruct((H, main, S), F32), jax.ShapeDtypeStruct((H, S, main), F32), jax.ShapeDtypeStruct((H, S, dv), F32)]
    if has_p:
        out_specs += [pl.BlockSpec((hb, 1, LANES), lambda b, j: (b, 0, 0))]
        out_shape += [jax.ShapeDtypeStruct((H, 1, LANES), F32)]
    if extra:
        out_specs += [colsT(dqk - real), pl.BlockSpec((hb, tk, dqk - real), lambda b, j: (b, j, 0))]
        out_shape += [jax.ShapeDtypeStruct((H, dqk - real, S), F32), jax.ShapeDtypeStruct((H, S, dqk - real), F32)]
    return pl.pallas_call(
        body, name=name, grid=(H // hb, S // tk), in_specs=in_specs, out_specs=out_specs, out_shape=out_shape,
        scratch_shapes=[pltpu.VMEM((hb, 1, S), F32)],
        compiler_params=pltpu.CompilerParams(dimension_semantics=("parallel", "arbitrary")),
    )(*args)


def _rows_and_cols(x3):
    xb = x3.astype(BF16)
    return jnp.transpose(xb, (1, 0, 2)), jnp.transpose(xb, (1, 2, 0))


def _cols_only(x3):
    return jnp.transpose(x3.astype(BF16), (1, 2, 0))


def _v_with_ones(v3):
    S, h, _ = v3.shape
    vT = jnp.transpose(v3.astype(BF16), (1, 2, 0))
    return jnp.concatenate([vT, jnp.ones((h, 1, S), BF16), jnp.zeros((h, 15, S), BF16)], axis=1)


def _from_T(oT):
    h, d, S = oT.shape
    return jnp.transpose(oT, (2, 0, 1)).reshape(S, h * d)


def _coords():
    return lax.axis_index("x"), lax.axis_index("y"), lax.axis_index("c")


def _peer(axis):
    x, y, c = _coords()
    return {"x": (1 - x, y, c), "y": (x, 1 - y, c), "c": (x, y, 1 - c)}[axis]


def _gather_job(bufs, rows=None):
    n = len(bufs)

    def copies(outs, send_sems, recv_sems):
        x, y, c = _coords()
        me, sibling = (x, y, c), (x, y, 1 - c)
        chips = [(1 - x, y), (x, 1 - y), (1 - x, 1 - y)]

        def copy(t, k, block, to):
            px, py, pc = block
            ref = outs[t].at[4 * px + 2 * py + pc]
            if rows is not None and rows[t] is not None:
                ref = ref.at[pl.ds(rows[t][0], rows[t][1])]
            return pltpu.make_async_remote_copy(ref, ref, send_sems.at[7 * t + k], recv_sems.at[7 * t + k], device_id=to, device_id_type=MESH)

        return copy, me, sibling, chips, c

    def start(ins, outs, send_sems, recv_sems):
        copy, me, sibling, chips, c = copies(outs, send_sems, recv_sems)
        for t in range(n):
            copy(t, 0, me, sibling).start()
            for j, chip in enumerate(chips):
                copy(t, 1 + j, me, (*chip, c)).start()

    def finish(ins, outs, send_sems, recv_sems):
        copy, me, sibling, chips, c = copies(outs, send_sems, recv_sems)
        for j, chip in enumerate(chips):
            for t in range(n):
                copy(t, 1 + j, (*chip, c), me).wait_recv()
                copy(t, 4 + j, (*chip, c), sibling).start()
        for t in range(n):
            copy(t, 0, sibling, me).wait_recv()
            for j, chip in enumerate(chips):
                copy(t, 4 + j, (*chip, 1 - c), me).wait_recv()
        for t in range(n):
            copy(t, 0, me, sibling).wait_send()
            for j, chip in enumerate(chips):
                copy(t, 1 + j, me, (*chip, c)).wait_send()
                copy(t, 4 + j, (*chip, c), sibling).wait_send()

    return dict(ins=list(bufs), outs=[jax.ShapeDtypeStruct(b.shape, b.dtype) for b in bufs], aliases={t: t for t in range(n)},
                n_sems=7 * n, start=start, finish=finish)


def _in_slot(local):
    x, y, c = _coords()
    buf = lax.empty((N_DEV,) + local.shape, local.dtype)
    return lax.dynamic_update_slice(buf, local[None], (4 * x + 2 * y + c, 0, 0))


def _pair_job(vs, axes):
    n = len(vs)
    axes = [axes] * n if isinstance(axes, str) else axes

    def copies(ins, outs, send_sems, recv_sems):
        out = []
        for t in range(n):
            me = lax.axis_index(axes[t])
            src = ins[t].at[1 - me] if len(ins[t].shape) == 3 else ins[t].at[:, 1 - me]
            out.append(pltpu.make_async_remote_copy(src, outs[t], send_sems.at[t], recv_sems.at[t], device_id=_peer(axes[t]), device_id_type=MESH))
        return out

    def start(*refs):
        for cp in copies(*refs):
            cp.start()

    def finish(*refs):
        for cp in copies(*refs):
            cp.wait()

    return dict(ins=list(vs), outs=[jax.ShapeDtypeStruct(v.shape[:-3] + v.shape[-2:], v.dtype) for v in vs], aliases={}, n_sems=n,
                start=start, finish=finish)


def _add_kept(v, got, axis, out, name):
    R, C = v.shape[-2:]
    lead = v.shape[0] if v.ndim == 4 else 1
    tm = _divisor(R, max(16, EW_TILE_BYTES // (_lanes(C) * (v.dtype.itemsize + got.dtype.itemsize + jnp.dtype(out).itemsize)) // 16 * 16), 16)
    me = lax.axis_index(axis).astype(jnp.int32).reshape(1)
    v4 = v.reshape(lead, 2, R, C)
    g3 = got.reshape(lead, R, C)

    def body(me_ref, v_ref, g_ref, o_ref):
        o_ref[...] = (v_ref[0].astype(F32) + g_ref[...].astype(F32)).astype(o_ref.dtype)

    res = pl.pallas_call(
        body, name=name, out_shape=jax.ShapeDtypeStruct((lead, R, C), out),
        grid_spec=pltpu.PrefetchScalarGridSpec(
            num_scalar_prefetch=1, grid=(lead, R // tm),
            in_specs=[pl.BlockSpec((1, 1, tm, C), lambda b, i, me: (b, me[0], i, 0)), pl.BlockSpec((1, tm, C), lambda b, i, me: (b, i, 0))],
            out_specs=pl.BlockSpec((1, tm, C), lambda b, i, me: (b, i, 0))),
    )(me, v4, g3)
    return res


def _cross_job(vs):
    n = len(vs)

    def copies(ins, outs, send_sems, recv_sems):
        x, y, _ = _coords()
        out = []
        for t in range(n):
            h = ins[t].shape[2] // 2
            out.append(pltpu.make_async_remote_copy(ins[t].at[1 - x, :, pl.ds(0, h)], outs[2 * t], send_sems.at[2 * t], recv_sems.at[2 * t],
                                                    device_id=_peer("x"), device_id_type=MESH))
            out.append(pltpu.make_async_remote_copy(ins[t].at[:, 1 - y, pl.ds(h, h)], outs[2 * t + 1], send_sems.at[2 * t + 1], recv_sems.at[2 * t + 1],
                                                    device_id=_peer("y"), device_id_type=MESH))
        return out

    def start(*refs):
        for cp in copies(*refs):
            cp.start()

    def finish(*refs):
        for cp in copies(*refs):
            cp.wait()

    outs = []
    for v in vs:
        outs += [jax.ShapeDtypeStruct((2, v.shape[2] // 2, v.shape[3]), v.dtype)] * 2
    return dict(ins=list(vs), outs=outs, aliases={}, n_sems=2 * n, start=start, finish=finish)


def _add_picked(v, got, axis, out, name):
    _, _, R, C = v.shape
    h = R // 2
    tm = _divisor(h, max(16, EW_TILE_BYTES // (_lanes(C) * (v.dtype.itemsize + got.dtype.itemsize + jnp.dtype(out).itemsize)) // 16 * 16), 16)
    me = lax.axis_index(axis).astype(jnp.int32).reshape(1)
    if axis == "x":
        v_map = lambda b, i, me: (me[0], b, i, 0)
    else:
        v_map = lambda b, i, me: (b, me[0], i + h // tm, 0)

    def body(me_ref, v_ref, g_ref, o_ref):
        o_ref[...] = (v_ref[0].astype(F32) + g_ref[...].astype(F32)).astype(o_ref.dtype)

    return pl.pallas_call(
        body, name=name, out_shape=jax.ShapeDtypeStruct((2, h, C), out),
        grid_spec=pltpu.PrefetchScalarGridSpec(
            num_scalar_prefetch=1, grid=(2, h // tm),
            in_specs=[pl.BlockSpec((1, 1, tm, C), v_map), pl.BlockSpec((1, tm, C), lambda b, i, me: (b, i, 0))],
            out_specs=pl.BlockSpec((1, tm, C), lambda b, i, me: (b, i, 0))),
    )(me, v, got)


def _reduce_scatter_steps(gs, tag):
    n = len(gs)
    vs = [g.reshape(4, 2, *g.shape[1:]) for g in gs]
    got = yield _pair_job(vs, "c")
    vs = [_add_kept(v, r, "c", BF16, f"rs_{tag}_add_c{t}") for t, (v, r) in enumerate(zip(vs, got))]
    vs = [v.reshape(2, 2, v.shape[1], v.shape[2]) for v in vs]
    got = yield _cross_job(vs)
    up = [_add_picked(v, r, "x", BF16, f"rs_{tag}_add_x{t}") for t, (v, r) in enumerate(zip(vs, got[0::2]))]
    lo = [_add_picked(v, r, "y", BF16, f"rs_{tag}_add_y{t}") for t, (v, r) in enumerate(zip(vs, got[1::2]))]
    got = yield _pair_job(up + lo, ["y"] * n + ["x"] * n)
    out = []
    for t in range(n):
        a = _add_kept(up[t], got[t], "y", F32, f"rs_{tag}_add_y2{t}")[0]
        b = _add_kept(lo[t], got[n + t], "x", F32, f"rs_{tag}_add_x2{t}")[0]
        out.append(jnp.concatenate([a, b], axis=0))
    return out


def _reduce_scatter(gs, tag):
    steps = _reduce_scatter_steps(gs, tag)
    job = next(steps)
    for stage in ("c", "xy", "yx"):
        got = _comm_call(job, f"rs_{tag}_{stage}")
        try:
            job = steps.send(got)
        except StopIteration as done:
            return done.value


def _all_reduce_small(v):
    def body(v_ref, o_ref, buf, send_sems, recv_sems):
        x, y, c = _coords()
        me = 4 * x + 2 * y + c
        buf[me] = v_ref[...]
        copies = []
        for k in range(1, N_DEV):
            peer = tuple((1 - a) if (k >> s) & 1 else a for a, s in ((x, 2), (y, 1), (c, 0)))
            cp = pltpu.make_async_remote_copy(v_ref, buf.at[me], send_sems.at[k - 1], recv_sems.at[k - 1], device_id=peer, device_id_type=MESH)
            cp.start()
            copies.append(cp)
        for cp in copies:
            cp.wait()
        acc = buf[0]
        for d in range(1, N_DEV):
            acc = acc + buf[d]
        o_ref[...] = acc

    vm = pl.BlockSpec(memory_space=pltpu.VMEM)
    return pl.pallas_call(
        body, name="all_reduce_small", in_specs=[vm], out_specs=vm, out_shape=jax.ShapeDtypeStruct(v.shape, F32),
        scratch_shapes=[pltpu.VMEM((N_DEV,) + v.shape, F32), pltpu.SemaphoreType.DMA((N_DEV - 1,)), pltpu.SemaphoreType.DMA((N_DEV - 1,))],
    )(v)


def _local_groups(w, dtype):
    mix_out = [w["ev_w_out"][0], w["od_w_out"][0]]
    layers = []
    for l in range(DEPTH):
        a = jnp.concatenate([w["ffa_w_down"][l], w["ffb_w_down"][l]], axis=0).astype(dtype)
        b = jnp.concatenate([w["ple_w_gate"][l], mix_out[l]], axis=0).astype(dtype)
        c = jnp.concatenate([w["ffa_w_gate_up"][l], w["ffb_w_gate_up"][l]], axis=0).astype(dtype)
        layers.append((a, b, c))
    strip = jnp.concatenate([w["ple_w_proj"].reshape(-1, STRIP_C), w["ev_w_ukv"][0], jnp.pad(w["ev_w_uq"][0], ((0, 0), (0, STRIP_C - 96))),
                             jnp.zeros((G3_ROWS - 896, STRIP_C), F32)], axis=0)
    m = jnp.concatenate([w["od_w_in"][0], w["ev_w_in"][0], strip, jnp.zeros((G3_ROWS, G3_COLS - STRIP0 - STRIP_C), F32)], axis=1).astype(dtype)
    return layers, m


def _ungroup_local(a, b, c, r3):
    out = {
        "ffa_w_down": jnp.stack([x[0] for x in a]), "ffb_w_down": jnp.stack([x[1] for x in a]),
        "ple_w_gate": jnp.stack([x[:128] for x in b]), "ev_w_out": b[0][128:][None], "od_w_out": b[1][128:][None],
        "ffa_w_gate_up": jnp.stack([x[0] for x in c]), "ffb_w_gate_up": jnp.stack([x[1] for x in c]),
        "od_w_in": r3[:, :OD_C][None], "ev_w_in": r3[:, OD_C:STRIP0][None],
    }
    strip = r3[:, STRIP0:STRIP0 + STRIP_C]
    out["ple_w_proj"] = strip[:512].reshape(2, PLE_DIM, STRIP_C)
    out["ev_w_ukv"] = strip[512:640][None]
    out["ev_w_uq"] = strip[640:896, :96][None]
    return out


def _cols(a):
    return jnp.transpose(a, (1, 0, 2)).reshape(a.shape[1], -1)


def _blocks(g, c):
    return jnp.transpose(g.reshape(g.shape[0], N_DEV, c), (1, 0, 2))


def _uq_permute(w):
    r = w.shape[0]
    w3 = w.reshape(r, B_HEADS, B_NOPE + B_ROPE)
    half = B_ROPE // 2
    return jnp.concatenate([w3[:, :, :B_NOPE].reshape(r, -1), w3[:, :, B_NOPE:B_NOPE + half].reshape(r, -1), w3[:, :, B_NOPE + half:].reshape(r, -1)], axis=1)


def _uq_unpermute(g):
    r = g.shape[0]
    half = B_ROPE // 2
    n = B_HEADS * B_NOPE
    parts = [g[:, :n].reshape(r, B_HEADS, B_NOPE), g[:, n:n + B_HEADS * half].reshape(r, B_HEADS, half), g[:, n + B_HEADS * half:].reshape(r, B_HEADS, half)]
    return jnp.concatenate(parts, axis=2).reshape(r, -1)


def _ukv_permute(w):
    r = w.shape[0]
    return jnp.transpose(w.reshape(r, B_HEADS, 2, B_NOPE), (0, 2, 1, 3)).reshape(r, -1)


def _ukv_unpermute(g):
    r = g.shape[0]
    return jnp.transpose(g.reshape(r, 2, B_HEADS, B_NOPE), (0, 2, 1, 3)).reshape(r, -1)


def _od_in_widen(w):
    n = C_HEADS * C_HEAD_DIM
    wide = lambda m: jnp.pad(m.reshape(-1, C_HEADS, C_HEAD_DIM), ((0, 0), (0, 0), (0, QK_PAD - C_HEAD_DIM))).reshape(m.shape[0], -1)
    return jnp.concatenate([wide(w[:, :n] * C_HEAD_DIM ** -0.5), wide(w[:, n:2 * n]), w[:, 2 * n:],
                            jnp.zeros((w.shape[0], ODD_IN_PAD - ODD_IN_AUG), w.dtype)], axis=1)


def _od_in_narrow(g):
    wp = C_HEADS * QK_PAD
    narrow = lambda m: m.reshape(-1, C_HEADS, QK_PAD)[:, :, :C_HEAD_DIM].reshape(m.shape[0], -1)
    return jnp.concatenate([narrow(g[:, :wp]) * C_HEAD_DIM ** -0.5, narrow(g[:, wp:2 * wp]), g[:, 2 * wp:ODD_IN_AUG]], axis=1)


def _misc_weights(G3):
    strip = G3[:, :, STRIP0:STRIP0 + STRIP_C]
    return {
        "od_w_in": _od_in_widen(_cols(G3[:, :, :OD_C])),
        "ev_w_in": jnp.pad(_cols(G3[:, :, OD_C:STRIP0]), ((0, 0), (0, EVEN_IN_PAD - EVEN_IN))),
        "ple_w_proj": [_cols(strip[:, i * PLE_DIM:(i + 1) * PLE_DIM]) for i in range(DEPTH)],
        "ev_w_ukv": _ukv_permute(_cols(strip[:, 512:640])),
        "ev_w_uq": _uq_permute(_cols(strip[:, 640:896, :96])),
    }


def _misc_grads(G):
    strip = jnp.concatenate([
        _blocks(G["ple_w_proj"][0], STRIP_C), _blocks(G["ple_w_proj"][1], STRIP_C), _blocks(_ukv_unpermute(G["ev_w_ukv"]), STRIP_C),
        jnp.pad(_blocks(_uq_unpermute(G["ev_w_uq"]), 96), ((0, 0), (0, 0), (0, STRIP_C - 96))),
        jnp.zeros((N_DEV, G3_ROWS - 896, STRIP_C), F32)], axis=1)
    return jnp.concatenate([_blocks(_od_in_narrow(G["od_w_in"]), OD_C), _blocks(G["ev_w_in"][:, :EVEN_IN], EV_C), strip,
                            jnp.zeros((N_DEV, G3_ROWS, G3_COLS - STRIP0 - STRIP_C), F32)], axis=2)


def _ffn_fwd(h, norm_w, W, f, i, tag, ride=None):
    job = ride() if ride else None
    res = _ffn_gate_up(h, norm_w, W["C"][i].reshape(2, 4, C_ROWS, FF_BLK), f, f"{tag}_gate_up", job=job)
    n, gu, act = res[:3]
    if job is not None:
        ride(res[3:])
    job = ride() if ride else None
    out = _ffn_down(act, W["A"][i], f, h, f"{tag}_down", job=job)
    if job is not None:
        out, got = out
        ride(got)
    return out, (h, n, gu, act)


def _ffn_bwd(dout, saved, norm_w, W, GB, f, i, tag, ride=None):
    h, n, gu, act = saved
    S = h.shape[0]
    def carried(call):
        job = ride() if ride else None
        res = call(job)
        if job is None:
            return res
        ride(res[1])
        return res[0]

    GB["A"][i][f] = carried(lambda job: _ffn_down_dw(act, dout, f"{tag}_down_dw", job=job))
    dgu = _ffn_down_dx(dout, W["A"][i], f, gu, f"{tag}_down_dx").reshape(N_DEV, S, FF_BLK)
    res = carried(lambda job: _ffn_gate_up_dx(dgu, W["C"][i], f, h, norm_w, dout, f"{tag}_gate_up_dx", job=job))
    GB["C"][i][f] = carried(lambda job: _ffn_gate_up_dw(n, dgu, f"{tag}_gate_up_dw", job=job))
    return res


def _rope_tables(S):
    inv = ROPE_THETA ** (-jnp.arange(0, B_ROPE, 2, dtype=F32) / B_ROPE)
    ang = jnp.arange(S, dtype=F32)[:, None] * inv[None, :]
    return jnp.cos(ang), jnp.sin(ang)


def _alibi_columns(S):
    t = jnp.arange(S, dtype=jnp.int32)
    hi = ((t // 16) * 16).astype(F32)
    lo = (t % 16).astype(F32)
    slopes = 2.0 ** (-8.0 * jnp.arange(1, A_HEADS + 1, dtype=F32) / A_HEADS)
    zq = jnp.zeros((S, A_HEADS), F32)
    rest = QK_PAD - A_HEAD_DIM - 4
    qc = jnp.stack([-slopes[None, :] * hi[:, None], -slopes[None, :] * lo[:, None], zq + slopes[None, :], zq + slopes[None, :]] + [zq] * rest, axis=-1)
    one = jnp.ones((S, A_KV_HEADS), F32)
    zk = jnp.zeros((S, A_KV_HEADS), F32)
    kc = jnp.stack([one, one, zk + hi[:, None], zk + lo[:, None]] + [zk] * rest, axis=-1)
    return qc, kc


def _sink_prm(sinks):
    return jnp.zeros((A_HEADS, 1, LANES), F32).at[:, 0, 0].set(sinks.astype(F32))


def _with_ride(ride, call):
    job = ride() if ride else None
    res = call(job)
    if job is None:
        return res
    n_own = len(res) - len(job["outs"])
    ride(res[n_own:])
    return res[:n_own]


def _even_fwd(hn, h, W, ride=None):
    S = hn.shape[0]
    proj = _mm(hn, W["ev_w_in"], name="ev_in")
    a_q, a_k, a_v = proj[:, :512], proj[:, 512:640], proj[:, 640:768]
    c_q, c_kv = proj[:, 768:1024], proj[:, 1024:1152]
    kr1, kr2 = proj[:, 1152:1168], proj[:, 1168:1184]
    qc, kc = _alibi_columns(S)
    qaT = _cols_only(jnp.concatenate([(a_q * A_HEAD_DIM ** -0.5).reshape(S, A_HEADS, A_HEAD_DIM), qc], axis=-1))
    ka, kaT = _rows_and_cols(jnp.concatenate([a_k.reshape(S, A_KV_HEADS, A_HEAD_DIM), kc], axis=-1))
    va3 = a_v.reshape(S, A_KV_HEADS, A_HEAD_DIM)
    va = jnp.transpose(va3.astype(BF16), (1, 0, 2))
    prm = _sink_prm(W["ev_sinks"][0])
    oaT, lse_a = _with_ride(ride, lambda job: _attn_fwd(qaT, ka, _v_with_ones(va3), tile=min(SWA_TILE, S // 2), hb=2, window=WINDOW, sink=prm,
                                                        name="swa_fwd", job=job))
    cqn = _rms_fwd(c_q, W["ev_cq_norm"], "ev_cq_norm")
    q_all = _mm(cqn, W["ev_w_uq"], name="ev_uq")
    ckvn = _rms_fwd(c_kv, W["ev_ckv_norm"], "ev_ckv_norm")
    kv_all = _mm(ckvn, W["ev_w_ukv"], name="ev_ukv")
    cos, sin = _rope_tables(S)
    cos8, sin8 = jnp.tile(cos, (1, B_HEADS)), jnp.tile(sin, (1, B_HEADS))
    q1, q2 = _rope(q_all[:, 512:640], q_all[:, 640:768], cos8, sin8, "ev_rope_q")
    k1, k2 = _rope(kr1, kr2, cos, sin, "ev_rope_k")
    half = B_ROPE // 2
    scale = (B_NOPE + B_ROPE) ** -0.5
    qbT = _cols_only(jnp.concatenate([q_all[:, :512].reshape(S, B_HEADS, B_NOPE), q1.reshape(S, B_HEADS, half), q2.reshape(S, B_HEADS, half)], axis=-1) * scale)
    kro = jnp.broadcast_to(jnp.concatenate([k1, k2], axis=1)[:, None, :], (S, B_HEADS, B_ROPE))
    kb, kbT = _rows_and_cols(jnp.concatenate([kv_all[:, :512].reshape(S, B_HEADS, B_NOPE), kro], axis=-1))
    vb3 = kv_all[:, 512:].reshape(S, B_HEADS, B_V)
    vb = jnp.transpose(vb3.astype(BF16), (1, 0, 2))
    obT, lse_b = _with_ride(ride, lambda job: _attn_fwd(qbT, kb, _v_with_ones(vb3), tile=min(ATTN_TILE_FWD, S), hb=2, name="mla_fwd", job=job))
    cat = jnp.concatenate([_from_T(oaT), _from_T(obT)], axis=1)
    out = _mm_w128(cat, W["B"][0], MIX_OUT_BLK, res=h, name="ev_out")
    return out, (hn, proj, (qaT, ka, kaT, va, oaT, lse_a), prm, cqn, ckvn, (qbT, kb, kbT, vb, obT, lse_b), cat)


def _even_bwd(dout, saved, W, GB, norm):
    hn, proj, (qaT, ka, kaT, va, oaT, lse_a), prm, cqn, ckvn, (qbT, kb, kbT, vb, obT, lse_b), cat = saved
    S = hn.shape[0]
    G = {}
    dcat = _mm_w128(dout, W["B"][0], MIX_OUT_BLK, tb=True, out=BF16, name="ev_out_dx")
    GB["B"][0] = _mm_w128_dw(cat, dout, MIX_OUT_BLK, GB["B"][0], "ev_out_dw")
    doaT = _cols_only(dcat[:, :512].reshape(S, A_HEADS, A_HEAD_DIM))
    dqaT, dka, dva, dsink = _attn_bwd(qaT, ka, kaT, va, oaT, doaT, lse_a, tile=min(SWA_TILE, S // 2), hb=2, window=WINDOW, sink=prm, real=A_HEAD_DIM,
                                       name="swa_bwd")
    G["ev_sinks"] = dsink[:, 0, 0]
    dqa = _from_T(dqaT) * A_HEAD_DIM ** -0.5
    dka = dka.reshape(A_KV_HEADS, A_GROUP, S, A_HEAD_DIM).sum(axis=1)
    dva = dva.reshape(A_KV_HEADS, A_GROUP, S, A_HEAD_DIM).sum(axis=1)
    dobT = _cols_only(dcat[:, 512:].reshape(S, B_HEADS, B_V))
    dqbT, dkb, dvb = _attn_bwd(qbT, kb, kbT, vb, obT, dobT, lse_b, tile=min(ATTN_TILE, S), hb=1, name="mla_bwd")
    half = B_ROPE // 2
    dqb = jnp.transpose(dqbT, (2, 0, 1)) * (B_NOPE + B_ROPE) ** -0.5
    dkb = jnp.transpose(dkb, (1, 0, 2))
    cos, sin = _rope_tables(S)
    cos8, sin8 = jnp.tile(cos, (1, B_HEADS)), jnp.tile(sin, (1, B_HEADS))
    dq1, dq2 = _rope(dqb[:, :, B_NOPE:B_NOPE + half].reshape(S, -1), dqb[:, :, B_NOPE + half:].reshape(S, -1), cos8, -sin8, "ev_rope_q_bwd")
    dq_all = jnp.concatenate([dqb[:, :, :B_NOPE].reshape(S, -1), dq1, dq2], axis=1).astype(BF16)
    dkr = dkb[:, :, B_NOPE:].sum(axis=1)
    dk1, dk2 = _rope(dkr[:, :half], dkr[:, half:], cos, -sin, "ev_rope_k_bwd")
    dkv_all = jnp.concatenate([dkb[:, :, :B_NOPE].reshape(S, -1), _unheads(dvb)], axis=1).astype(BF16)
    G["ev_w_uq"] = _mm(cqn, dq_all, ta=True, name="ev_uq_dw")
    dcqn = _mm(dq_all, W["ev_w_uq"], tb=True, name="ev_uq_dx")
    dc_q, G["ev_cq_norm"] = _rms_bwd(dcqn, proj[:, 768:1024], W["ev_cq_norm"], None, "ev_cq_norm_bwd")
    G["ev_w_ukv"] = _mm(ckvn, dkv_all, ta=True, name="ev_ukv_dw")
    dckvn = _mm(dkv_all, W["ev_w_ukv"], tb=True, name="ev_ukv_dx")
    dc_kv, G["ev_ckv_norm"] = _rms_bwd(dckvn, proj[:, 1024:1152], W["ev_ckv_norm"], None, "ev_ckv_norm_bwd")
    dproj = jnp.concatenate([dqa, _unheads(dka), _unheads(dva), dc_q, dc_kv, dk1, dk2,
                             jnp.zeros((S, EVEN_IN_PAD - EVEN_IN), F32)], axis=1).astype(BF16)
    G["ev_w_in"] = _mm(hn, dproj, ta=True, name="ev_in_dw")
    dh, dnorm = _mm(dproj, W["ev_w_in"], tb=True, norm_bwd=(*norm, dout), name="ev_in_dx")
    return dh, dnorm, G


def _odd_fwd(hn, h, W, ride=None):
    S = hn.shape[0]
    w = C_HEADS * C_HEAD_DIM
    wp = C_HEADS * QK_PAD
    proj = _mm(hn, W["od_w_in"], name="od_in")
    f_logit = proj[:, 2 * wp + w: 2 * wp + w + C_HEADS]
    logf = _logsig_fwd(f_logit, W["od_b_f"], "od_logsig")
    logc = _cumsum(logf, False, "od_cumsum")
    parts = list(_exact3(logc))
    ones = [jnp.ones((S, C_HEADS), F32)] * 3
    pad = [jnp.zeros((S, C_HEADS), F32)] * (QK_PAD - C_HEAD_DIM - 6)
    lead = ((0, 0), (0, 0), (C_HEAD_DIM, 0))
    q3 = proj[:, :wp].reshape(S, C_HEADS, QK_PAD) + jnp.pad(jnp.stack(parts + ones + pad, axis=-1), lead)
    k3 = proj[:, wp:2 * wp].reshape(S, C_HEADS, QK_PAD) + jnp.pad(jnp.stack(ones + [-p for p in parts] + pad, axis=-1), lead)
    qT = _cols_only(q3)
    k, kT = _rows_and_cols(k3)
    v3 = proj[:, 2 * wp:2 * wp + w].reshape(S, C_HEADS, C_HEAD_DIM)
    v = jnp.transpose(v3.astype(BF16), (1, 0, 2))
    oT, lse = _with_ride(ride, lambda job: _attn_fwd(qT, k, _v_with_ones(v3), tile=min(ATTN_TILE_FWD, S), hb=2, name="fox_fwd", job=job))
    cat = _from_T(oT)
    out = _mm_w128(cat, W["B"][1], MIX_OUT_BLK, res=h, name="od_out")
    return out, (hn, qT, k, kT, v, f_logit, oT, lse, cat)


def _odd_bwd(dout, saved, W, GB, norm):
    hn, qT, k, kT, v, f_logit, oT, lse, cat = saved
    S = hn.shape[0]
    G = {}
    dcat = _mm_w128(dout, W["B"][1], MIX_OUT_BLK, tb=True, out=BF16, name="od_out_dx")
    GB["B"][1] = _mm_w128_dw(cat, dout, MIX_OUT_BLK, GB["B"][1], "od_out_dw")
    doT = _cols_only(dcat.reshape(S, C_HEADS, C_HEAD_DIM))
    dqT, dk, dv, dqxT, dkx = _attn_bwd(qT, k, kT, v, oT, doT, lse, tile=min(ATTN_TILE, S), hb=1, real=C_HEAD_DIM, extra=True,
                                       full=True, name="fox_bwd")
    dlogc = jnp.transpose(dqxT[:, 0, :] - dkx[:, :, 3])
    dlogf = _cumsum(dlogc, True, "od_cumsum_bwd")
    df, db = _logsig_bwd(dlogf, f_logit, W["od_b_f"], "od_logsig_bwd")
    G["od_b_f"] = db
    dproj = jnp.concatenate([_from_T(dqT), _unheads(dk), _unheads(dv), df, jnp.zeros((S, ODD_IN_PAD - ODD_IN_AUG), F32)], axis=1).astype(BF16)
    G["od_w_in"] = _mm(hn, dproj, ta=True, name="od_in_dw")
    dh, dnorm = _mm(dproj, W["od_w_in"], tb=True, norm_bwd=(*norm, dout), name="od_in_dx")
    return dh, dnorm, G


class _Rider:
    def __init__(self, steps, tag):
        self.steps, self.tag, self.count, self.result = steps, tag, 0, None
        self.job = next(steps)

    def __call__(self, got=None):
        if got is not None:
            return self._advance(list(got))
        job = self.job
        if isinstance(job, str):
            self._advance(None)
            return None
        return job

    def _advance(self, value):
        try:
            self.job = self.steps.send(value)
        except StopIteration as done:
            self.job, self.result = None, done.value

    def finish(self):
        while self.job is not None:
            if isinstance(self.job, str):
                self._advance(None)
                continue
            self.count += 1
            self(_comm_call(self.job, f"{self.tag}_{self.count}"))
        return self.result


def _gather_plan(W, slots):
    a0, b0, c0, m, a1, b1, c1 = (slots[key] for key in ("a0", "b0", "c0", "m", "a1", "b1", "c1"))
    (m,) = yield _gather_job([m])
    W.update(_misc_weights(m))
    (b0,) = yield _gather_job([b0])
    W["B"] = [b0]
    (c0,) = yield _gather_job([c0], rows=[(D_MODEL, D_MODEL)])
    W["C"] = [c0]
    a0, c1 = yield _gather_job([a0, c1], rows=[(DOWN_ROWS, DOWN_ROWS), (0, D_MODEL)])
    W["A"] = [a0]
    W["C"].append(c1)
    (a1,) = yield _gather_job([a1], rows=[(0, DOWN_ROWS)])
    W["A"].append(a1)
    for _ in range(3):
        yield "skip"
    a1, b1, c1 = yield _gather_job([a1, b1, c1], rows=[(DOWN_ROWS, DOWN_ROWS), None, (D_MODEL, D_MODEL)])
    W["A"][1], W["C"][1] = a1, c1
    W["B"].append(b1)


def _local_step(x, p, target, W, slots):
    h = x
    saved = []
    gather = _Rider(_gather_plan(W, slots), "all_gather_rest")
    for i in range(DEPTH):
        t = f"l{i}"
        h1, s_a = _ffn_fwd(h, W["ffa_norm"][i:i + 1], W, 0, i, f"{t}_ffa", gather)
        nm = _rms_fwd(h1, W["mix_norm"][i:i + 1], f"{t}_mix_norm")
        h2, s_m = (_even_fwd if i % 2 == 0 else _odd_fwd)(nm, h1, W, gather)
        h3, s_b = _ffn_fwd(h2, W["ffb_norm"][i:i + 1], W, 1, i, f"{t}_ffb", gather)
        npl = _rms_fwd(h3, W["ple_norm"][i:i + 1], f"{t}_ple_norm")
        gpre = _mm_w128(npl, W["B"][i], PLE_GATE_BLK, name=f"{t}_ple_gate")
        pp = _mm(p[i], W["ple_w_proj"][i], name=f"{t}_ple_proj")
        h4 = _ple_fwd(h3, gpre, pp, f"{t}_ple")
        saved.append((s_a, h1, s_m, s_b, h3, npl, gpre, pp))
        h = h4
    gather.finish()
    dh, g_final, loss_cols = _final_fwd_bwd(h, W["final_norm"], target, "final")
    G = {"final_norm": g_final}
    GB = {"A": [[None, None] for _ in range(DEPTH)], "C": [[None, None] for _ in range(DEPTH)],
          "B": [lax.empty((N_DEV, B_ROWS, D_MODEL), BF16) for _ in range(DEPTH)]}
    per_layer = {n: [None] * DEPTH for n in ("ffa_norm", "mix_norm", "ffb_norm", "ple_norm", "ple_w_proj")}
    scatter = scatter_mid = None
    for i in reversed(range(DEPTH)):
        t = f"l{i}"
        s_a, h1, s_m, s_b, h3, npl, gpre, pp = saved[i]
        dgpre, dpp = _ple_bwd(dh, gpre, pp, f"{t}_ple_bwd")
        per_layer["ple_w_proj"][i] = _mm(p[i], dpp, ta=True, name=f"{t}_ple_proj_dw")
        GB["B"][i] = _mm_w128_dw(npl, dgpre, PLE_GATE_BLK, GB["B"][i], f"{t}_ple_gate_dw")
        dh, per_layer["ple_norm"][i] = _mm_w128(dgpre, W["B"][i], PLE_GATE_BLK, tb=True, norm_bwd=(h3, W["ple_norm"][i:i + 1], dh),
                                                name=f"{t}_ple_gate_dx")
        dh, per_layer["ffb_norm"][i] = _ffn_bwd(dh, s_b, W["ffb_norm"][i:i + 1], W, GB, 1, i, f"{t}_ffb", scatter)
        dh, per_layer["mix_norm"][i], g_mix = (_even_bwd if i % 2 == 0 else _odd_bwd)(dh, s_m, W, GB, (h1, W["mix_norm"][i:i + 1]))
        G.update(g_mix)
        if i == 0:
            G["ple_w_proj"] = per_layer["ple_w_proj"]
            mid = [GB["A"][0][1], GB["C"][0][1], GB["B"][0], _misc_grads(G).astype(BF16)]
            scatter_mid = _Rider(_reduce_scatter_steps(mid, "mid"), "rs_mid")
        dh, per_layer["ffa_norm"][i] = _ffn_bwd(dh, s_a, W["ffa_norm"][i:i + 1], W, GB, 0, i, f"{t}_ffa", scatter_mid)
        if i == DEPTH - 1:
            later = [GB["A"][i][0], GB["A"][i][1], GB["C"][i][0], GB["C"][i][1], GB["B"][i]]
            scatter = _Rider(_reduce_scatter_steps(later, "later"), "rs_later")
    for n in ("ffa_norm", "mix_norm", "ffb_norm", "ple_norm"):
        G[n] = jnp.concatenate(per_layer[n], axis=0)
    return loss_cols, dh, scatter.finish(), scatter_mid.finish(), [GB["A"][0][0], GB["C"][0][0]], G


def kernel(x, p, ffa_norm, ffa_w_gate_up, ffa_w_down, mix_norm, ffb_norm, ffb_w_gate_up, ffb_w_down, ple_norm, ple_w_gate, ple_w_proj, ev_w_in, ev_sinks, ev_cq_norm, ev_w_uq, ev_ckv_norm, ev_w_ukv, ev_w_out, od_w_in, od_b_f, od_w_out, final_norm, loss_target, m_ffa_norm, m_ffa_w_gate_up, m_ffa_w_down, m_mix_norm, m_ffb_norm, m_ffb_w_gate_up, m_ffb_w_down, m_ple_norm, m_ple_w_gate, m_ple_w_proj, m_ev_w_in, m_ev_sinks, m_ev_cq_norm, m_ev_w_uq, m_ev_ckv_norm, m_ev_w_ukv, m_ev_w_out, m_od_w_in, m_od_b_f, m_od_w_out, m_final_norm, v_ffa_norm, v_ffa_w_gate_up, v_ffa_w_down, v_mix_norm, v_ffb_norm, v_ffb_w_gate_up, v_ffb_w_down, v_ple_norm, v_ple_w_gate, v_ple_w_proj, v_ev_w_in, v_ev_sinks, v_ev_cq_norm, v_ev_w_uq, v_ev_ckv_norm, v_ev_w_ukv, v_ev_w_out, v_od_w_in, v_od_b_f, v_od_w_out, v_final_norm):
    given = dict(locals())
    w_in = {n: given[n] for n in WEIGHTS}

    layers, misc = _local_groups(w_in, BF16)
    (a0, b0, c0), (a1, b1, c1) = [[_in_slot(g) for g in layer] for layer in layers]
    a0, c0 = _comm_call(_gather_job([a0, c0], rows=[(0, DOWN_ROWS), (0, D_MODEL)]), "all_gather_first")
    W = {n: w_in[n] for n in SMALL}
    W["final_norm"] = final_norm.reshape(1, -1)
    W.update(A=[a0], C=[c0])
    slots = dict(a0=a0, b0=b0, c0=c0, m=_in_slot(misc), a1=a1, b1=b1, c1=c1)

    loss_cols, dx, r_later, r_mid, last, G = _local_step(x[0], p[:, 0], loss_target[0], W, slots)

    a1f, a1b, c1f, c1b, b1 = r_later
    a0b, c0b, b0, r_misc = r_mid
    a0f, c0f = _reduce_scatter(last, "last")
    grads = _ungroup_local([[a0f, a0b], [a1f, a1b]], [b0, b1], [[c0f, c0b], [c1f, c1b]], r_misc)
    layout = [(n, int(np.prod(w_in[n].shape))) for n in SMALL]
    vec = jnp.concatenate([G[n].astype(F32).reshape(-1) for n, _ in layout] + [jnp.sum(loss_cols).reshape(1)])
    vec = jnp.pad(vec, (0, N_DEV * SMALL_COLS - vec.shape[0])).reshape(N_DEV, SMALL_COLS)
    vec = _all_reduce_small(vec).reshape(-1)
    off = 0
    for n, size in layout:
        grads[n] = vec[off: off + size].reshape(w_in[n].shape)
        off += size
    loss = vec[off]

    delta, new_m, new_v = {}, {}, {}
    for n in WEIGHTS:
        shp = w_in[n].shape
        as2d = (lambda a: a.reshape(1, -1)) if len(shp) == 1 else (lambda a: a)
        d, nm, nv = _adamw(as2d(w_in[n]), as2d(grads[n]), as2d(given["m_" + n]), as2d(given["v_" + n]), f"adamw_{n}")
        delta[n], new_m[n], new_v[n] = d.reshape(shp), nm.reshape(shp), nv.reshape(shp)
    return (loss, dx[None], *[grads[n] for n in WEIGHTS], *[delta[n] for n in WEIGHTS],
            *[new_m[n] for n in WEIGHTS], *[new_v[n] for n in WEIGHTS])
```

```python
import functools

import numpy as np
import jax
import jax.numpy as jnp
from jax import lax
from jax.experimental import pallas as pl
from jax.experimental.pallas import tpu as pltpu

F32 = jnp.float32
BF16 = jnp.bfloat16
MESH = pl.DeviceIdType.MESH

D_MODEL = 1024
D_FF = 2816
RMS_EPS = 1e-6
PLE_DIM = 256
A_HEADS, A_KV_HEADS, A_HEAD_DIM, WINDOW = 8, 2, 64, 128
A_GROUP = A_HEADS // A_KV_HEADS
B_HEADS, B_NOPE, B_ROPE, B_V = 8, 64, 32, 64
ROPE_THETA = 10000.0
C_HEADS, C_HEAD_DIM = 16, 64
EVEN_IN = 1184
EVEN_IN_PAD = 1280
ODD_IN_AUG = 2 * 16 * 80 + 1024 + 16
ODD_IN_PAD = 3840
DEPTH = 2
ADAM_LR, ADAM_B1, ADAM_B2, ADAM_EPS, ADAM_WD, ADAM_STEP = 0.001, 0.9, 0.999, 1e-08, 0.01, 10

N_DEV = 8
LANES = 128
EW_TILE_BYTES = 3 << 20
MM_VMEM_BYTES = 26 << 20
NEG = -1e30
ATTN_TILE = 1024
ATTN_TILE_FWD = 1024
SWA_TILE = 512
QK_PAD = 80

FF_BLK = D_FF // 4
DOWN_ROWS = D_FF // N_DEV
B_ROWS, C_ROWS, G3_ROWS, G3_COLS = 256, 2 * D_MODEL, 1024, 768
PLE_GATE_BLK, MIX_OUT_BLK = 0, 1
OD_C, EV_C, STRIP_C = 386, 148, 128
STRIP0 = OD_C + EV_C

SMALL = ["ffa_norm", "mix_norm", "ffb_norm", "ple_norm", "ev_sinks", "ev_cq_norm", "ev_ckv_norm", "od_b_f", "final_norm"]
WEIGHTS = ["ffa_norm", "ffa_w_gate_up", "ffa_w_down", "mix_norm", "ffb_norm", "ffb_w_gate_up", "ffb_w_down", "ple_norm",
           "ple_w_gate", "ple_w_proj", "ev_w_in", "ev_sinks", "ev_cq_norm", "ev_w_uq", "ev_ckv_norm", "ev_w_ukv", "ev_w_out",
           "od_w_in", "od_b_f", "od_w_out", "final_norm"]
SMALL_COLS = 1280


def _divisor(n, cap, mult):
    if n <= cap:
        return n
    for t in range(cap - cap % mult, 0, -mult):
        if n % t == 0:
            return t
    raise ValueError(f"no tile for {n} under {cap} in steps of {mult}")


def _lanes(c):
    return -(-c // LANES) * LANES


def _ew(fn, rows, vecs, outs, reds=(), *, name):
    R = rows[0].shape[0]
    per_row = sum(_lanes(a.shape[1]) * a.dtype.itemsize for a in rows) + sum(_lanes(c) * jnp.dtype(d).itemsize for c, d in outs)
    tm = _divisor(R, max(16, EW_TILE_BYTES // per_row // 16 * 16), 16) if R % 16 == 0 else R
    n_r, n_v, n_o = len(rows), len(vecs), len(outs)

    def body(*refs):
        ins = [r[...] for r in refs[: n_r + n_v]]
        res = fn(*ins)
        if not isinstance(res, (tuple, list)):
            res = (res,)
        o_refs = refs[n_r + n_v: n_r + n_v + n_o]
        r_refs = refs[n_r + n_v + n_o:]
        for ref, val in zip(o_refs, res[:n_o]):
            ref[...] = val.astype(ref.dtype)
        if r_refs:
            @pl.when(pl.program_id(0) == 0)
            def _():
                for ref in r_refs:
                    ref[...] = jnp.zeros_like(ref)
            for ref, val in zip(r_refs, res[n_o:]):
                ref[...] += val

    in_specs = [pl.BlockSpec((tm, a.shape[1]), lambda i: (i, 0)) for a in rows]
    in_specs += [pl.BlockSpec((1, a.shape[1]), lambda i: (0, 0)) for a in vecs]
    out_specs = [pl.BlockSpec((tm, c), lambda i: (i, 0)) for c, _ in outs]
    out_specs += [pl.BlockSpec((1, c), lambda i: (0, 0)) for c in reds]
    out_shape = [jax.ShapeDtypeStruct((R, c), d) for c, d in outs] + [jax.ShapeDtypeStruct((1, c), F32) for c in reds]
    res = pl.pallas_call(body, name=name, grid=(R // tm,), in_specs=in_specs, out_specs=out_specs, out_shape=out_shape)(*rows, *vecs)
    return res[0] if len(res) == 1 else res


def _rms_fwd(x, w, name):
    def fn(x, w):
        y = x * lax.rsqrt(jnp.mean(x * x, axis=-1, keepdims=True) + RMS_EPS)
        return y * w
    return _ew(fn, [x], [w], [(x.shape[1], BF16)], name=name)


def _rms_bwd(dn, x, w, dres, name):
    def fn(dn, x, *rest):
        w = rest[-1]
        r = lax.rsqrt(jnp.mean(x * x, axis=-1, keepdims=True) + RMS_EPS)
        xh = x * r
        gw = dn * w
        dx = r * (gw - xh * jnp.mean(gw * xh, axis=-1, keepdims=True))
        if len(rest) == 2:
            dx = dx + rest[0]
        return dx, jnp.sum(dn * xh, axis=0, keepdims=True)
    rows = [dn, x] + ([dres] if dres is not None else [])
    return _ew(fn, rows, [w], [(x.shape[1], F32)], [x.shape[1]], name=name)


def _ple_fwd(h, gpre, pp, name):
    return _ew(lambda h, g, q: h + jax.nn.sigmoid(g) * q, [h, gpre, pp], [], [(h.shape[1], F32)], name=name)


def _ple_bwd(dh, gpre, pp, name):
    def fn(dh, g, q):
        sg = jax.nn.sigmoid(g)
        return dh * q * (sg * (1.0 - sg)), dh * sg
    return _ew(fn, [dh, gpre, pp], [], [(dh.shape[1], BF16), (dh.shape[1], BF16)], name=name)


def _rope(x1, x2, cos, sin, name):
    c = x1.shape[1]
    return _ew(lambda a, b, co, si: (a * co - b * si, a * si + b * co), [x1, x2, cos, sin], [], [(c, F32), (c, F32)], name=name)


def _logsig_fwd(f, b, name):
    def fn(f, b):
        z = f + b
        return jnp.minimum(z, 0.0) - jnp.log(1.0 + jnp.exp(-jnp.abs(z)))
    return _ew(fn, [f], [b], [(f.shape[1], F32)], name=name)


def _logsig_bwd(dlogf, f, b, name):
    def fn(d, f, b):
        df = d * jax.nn.sigmoid(-(f + b))
        return df, jnp.sum(df, axis=0, keepdims=True)
    return _ew(fn, [dlogf, f], [b], [(f.shape[1], F32)], [f.shape[1]], name=name)


def _final_fwd_bwd(h, w, target, name):
    d = h.shape[1]

    def fn(h, t, w):
        r = lax.rsqrt(jnp.mean(h * h, axis=-1, keepdims=True) + RMS_EPS)
        xh = h * r
        y = xh * w
        err = y - t
        dy = err * (1.0 / d)
        gw = dy * w
        dx = r * (gw - xh * jnp.mean(gw * xh, axis=-1, keepdims=True))
        return dx, jnp.sum(dy * xh, axis=0, keepdims=True), jnp.sum(err * err, axis=0, keepdims=True) * (0.5 / d)
    return _ew(fn, [h, target], [w], [(d, F32)], [d, d], name=name)


def _adamw(w, g, m, v, name):
    shape = w.shape
    c = shape[-1]
    w2, g2, m2, v2 = (a.reshape(-1, c) for a in (w, g, m, v))

    def fn(w, g, m, v):
        m = ADAM_B1 * m + (1.0 - ADAM_B1) * g
        v = ADAM_B2 * v + (1.0 - ADAM_B2) * jnp.square(g)
        m_hat = m / (1.0 - ADAM_B1 ** ADAM_STEP)
        v_hat = v / (1.0 - ADAM_B2 ** ADAM_STEP)
        delta = -ADAM_LR * (m_hat / (jnp.sqrt(v_hat) + ADAM_EPS) + ADAM_WD * w)
        return delta, m, v
    d, nm, nv = _ew(fn, [w2, g2, m2, v2], [], [(c, F32)] * 3, name=name)
    return d.reshape(shape), nm.reshape(shape), nv.reshape(shape)


def _split3(v):
    hi = v.astype(BF16)
    r1 = v - hi.astype(F32)
    mid = r1.astype(BF16)
    lo = (r1 - mid.astype(F32)).astype(BF16)
    return hi, mid, lo


def _cumsum(x, reverse, name):
    S, C = x.shape
    tm = _divisor(S, 512, 16)
    nt = S // tm

    def body(x_ref, o_ref, carry):
        @pl.when(pl.program_id(0) == 0)
        def _():
            carry[...] = jnp.zeros_like(carry)
        r = lax.broadcasted_iota(jnp.int32, (tm, tm), 0)
        c = lax.broadcasted_iota(jnp.int32, (tm, tm), 1)
        tri = jnp.where((c >= r) if reverse else (c <= r), 1.0, 0.0).astype(BF16)
        xv = x_ref[...]
        acc = jnp.zeros((tm, C), F32)
        for part in _split3(xv):
            acc = acc + jnp.dot(tri, part, preferred_element_type=F32)
        o_ref[...] = acc + carry[...]
        carry[...] += jnp.sum(xv, axis=0, keepdims=True)

    idx = (lambda i: (nt - 1 - i, 0)) if reverse else (lambda i: (i, 0))
    return pl.pallas_call(
        body, name=name, grid=(nt,), in_specs=[pl.BlockSpec((tm, C), idx)], out_specs=pl.BlockSpec((tm, C), idx),
        out_shape=jax.ShapeDtypeStruct((S, C), F32), scratch_shapes=[pltpu.VMEM((1, C), F32)],
    )(x)


NN = (((1,), (0,)), ((), ()))
NT = (((1,), (1,)), ((), ()))
TN = (((0,), (0,)), ((), ()))

HBM_SPEC = pl.BlockSpec(memory_space=pl.ANY)


def _job_in_body(job, refs, n_in, n_out, n_scr, grid):
    if job is None:
        return refs[n_in:], lambda: None
    ji, jo = len(job["ins"]), len(job["outs"])
    j_in = refs[n_in: n_in + ji]
    pos = n_in + ji
    own = list(refs[pos: pos + n_out])
    pos += n_out
    j_out = refs[pos: pos + jo]
    pos += jo
    own += list(refs[pos: pos + n_scr])
    ss, rs = refs[-2], refs[-1]
    first = functools.reduce(jnp.logical_and, [pl.program_id(d) == 0 for d in range(len(grid))])
    last = functools.reduce(jnp.logical_and, [pl.program_id(d) == n - 1 for d, n in enumerate(grid)])

    @pl.when(first)
    def _():
        job["start"](j_in, j_out, ss, rs)

    def finish():
        @pl.when(last)
        def _():
            job["finish"](j_in, j_out, ss, rs)

    return own, finish


def _job_call(job, body, *, name, grid, in_specs, out_specs, out_shape, args, scratch_shapes, aliases, dimension_semantics):
    in_specs, out_specs, out_shape, args, scratch_shapes = list(in_specs), list(out_specs), list(out_shape), list(args), list(scratch_shapes)
    aliases = dict(aliases)
    if job is not None:
        for i, o in job["aliases"].items():
            aliases[len(args) + i] = len(out_shape) + o
        in_specs += [HBM_SPEC] * len(job["ins"])
        args += list(job["ins"])
        out_specs += [HBM_SPEC] * len(job["outs"])
        out_shape += list(job["outs"])
        scratch_shapes += [pltpu.SemaphoreType.DMA((job["n_sems"],)), pltpu.SemaphoreType.DMA((job["n_sems"],))]
    return pl.pallas_call(
        body, name=name, grid=grid, in_specs=in_specs, out_specs=out_specs, out_shape=out_shape,
        scratch_shapes=scratch_shapes, input_output_aliases=aliases,
        compiler_params=pltpu.CompilerParams(dimension_semantics=dimension_semantics),
    )(*args)


def _comm_call(job, name):
    def body(*refs):
        ji, jo = len(job["ins"]), len(job["outs"])
        job["start"](refs[:ji], refs[ji: ji + jo], refs[-2], refs[-1])
        job["finish"](refs[:ji], refs[ji: ji + jo], refs[-2], refs[-1])

    return pl.pallas_call(
        body, name=name, in_specs=[HBM_SPEC] * len(job["ins"]), out_specs=[HBM_SPEC] * len(job["outs"]), out_shape=list(job["outs"]),
        input_output_aliases=dict(job["aliases"]),
        scratch_shapes=[pltpu.SemaphoreType.DMA((job["n_sems"],)), pltpu.SemaphoreType.DMA((job["n_sems"],))],
    )(*job["ins"])


def _mm_call(name, grid, k_axis, a, a_spec, a2d, b, b_spec, b2d, dims, out_sds, out_spec, o2d, *,
             alpha=1.0, res=None, res_spec=None, into=None, job=None, norm_bwd=None):
    nk = grid[k_axis]
    n_in = 2 + (res is not None) + (into is not None) + (3 if norm_bwd is not None else 0)
    n_out = 2 if norm_bwd is not None else 1

    def body(*refs):
        a_ref, b_ref = refs[0], refs[1]
        res_ref = refs[2] if res is not None else None
        own, finish_job = _job_in_body(job, refs, n_in, n_out, 1, grid)
        o_ref, acc_ref = own[0], own[-1]
        k = pl.program_id(k_axis)

        @pl.when(k == 0)
        def _():
            acc_ref[...] = jnp.zeros_like(acc_ref)

        if norm_bwd is not None:
            x_ref, w_ref, dres_ref = refs[n_in - 3: n_in]
            dw_ref = own[1]

            @pl.when(functools.reduce(jnp.logical_and, [pl.program_id(d) == 0 for d in range(len(grid))]))
            def _():
                dw_ref[...] = jnp.zeros_like(dw_ref)

        av = a_ref[...].reshape(a2d).astype(BF16)
        bv = b_ref[...].reshape(b2d).astype(BF16)
        acc_ref[...] += lax.dot_general(av, bv, dims, preferred_element_type=F32)

        @pl.when(k == nk - 1)
        def _():
            r = acc_ref[...]
            if alpha != 1.0:
                r = r * alpha
            if res_ref is not None:
                r = res_ref[...].reshape(o2d) + r
            if norm_bwd is not None:
                x = x_ref[...]
                rs = lax.rsqrt(jnp.mean(x * x, axis=-1, keepdims=True) + RMS_EPS)
                xh = x * rs
                gw = r * w_ref[...]
                dw_ref[...] += jnp.sum(r * xh, axis=0, keepdims=True)
                r = dres_ref[...] + rs * (gw - xh * jnp.mean(gw * xh, axis=-1, keepdims=True))
            o_ref[...] = r.reshape(o_ref.shape).astype(o_ref.dtype)

        finish_job()

    in_specs, args = [a_spec, b_spec], [a, b]
    if res is not None:
        in_specs.append(res_spec)
        args.append(res)
    aliases = {}
    if into is not None:
        aliases = {len(args): 0}
        in_specs.append(pl.BlockSpec(memory_space=pl.ANY))
        args.append(into)
        out_sds = jax.ShapeDtypeStruct(into.shape, into.dtype)
    out_specs, out_shape = [out_spec], [out_sds]
    if norm_bwd is not None:
        vec = pl.BlockSpec((1, o2d[1]), lambda *_: (0, 0))
        in_specs += [out_spec, vec, out_spec]
        args += list(norm_bwd)
        out_specs.append(vec)
        out_shape.append(jax.ShapeDtypeStruct((1, o2d[1]), F32))
    serial = job is not None or norm_bwd is not None
    sem = tuple("arbitrary" if d == k_axis or serial else "parallel" for d in range(len(grid)))
    res_all = _job_call(
        job, body, name=name, grid=grid, in_specs=in_specs, out_specs=out_specs, out_shape=out_shape, args=args,
        scratch_shapes=[pltpu.VMEM(o2d, F32)], aliases=aliases, dimension_semantics=sem)
    own = res_all[0] if n_out == 1 else tuple(res_all[:n_out])
    return own if job is None else (own, res_all[n_out:])


def _mm(a, b, *, ta=False, tb=False, out=F32, res=None, alpha=1.0, norm_bwd=None, name):
    K, M = a.shape if ta else a.shape[::-1]
    N = b.shape[0] if tb else b.shape[1]
    assert (b.shape[1] if tb else b.shape[0]) == K, (a.shape, b.shape, ta, tb)
    tk = _divisor(K, 1024, LANES)
    tn = _divisor(N, 1408, LANES)
    assert norm_bwd is None or tn == N
    for cap in (1024, 512, 256, 128):
        tm = _divisor(M, cap, LANES if ta else 16)
        est = 2 * (tm * tk * a.dtype.itemsize + tk * tn * b.dtype.itemsize + tm * tn * jnp.dtype(out).itemsize)
        est += tm * tn * 4 + (2 * tm * tn * 4 if res is not None else 0) + (4 * tm * tn * 4 if norm_bwd is not None else 0)
        if est <= MM_VMEM_BYTES:
            break
    a_spec = pl.BlockSpec((tk, tm), lambda i, j, k: (k, i)) if ta else pl.BlockSpec((tm, tk), lambda i, j, k: (i, k))
    b_spec = pl.BlockSpec((tn, tk), lambda i, j, k: (j, k)) if tb else pl.BlockSpec((tk, tn), lambda i, j, k: (k, j))
    o_spec = pl.BlockSpec((tm, tn), lambda i, j, k: (i, j))
    dims = (((0 if ta else 1,), (1 if tb else 0,)), ((), ()))
    return _mm_call(name, (M // tm, N // tn, K // tk), 2, a, a_spec, (tk, tm) if ta else (tm, tk), b, b_spec,
                    (tn, tk) if tb else (tk, tn), dims, jax.ShapeDtypeStruct((M, N), out), o_spec, (tm, tn),
                    alpha=alpha, res=res, res_spec=o_spec, norm_bwd=norm_bwd)


def _w128_spec(blk):
    return pl.BlockSpec((N_DEV, 128, D_MODEL), lambda *_: (0, blk, 0))


def _mm_w128(a, G1, blk, *, tb=False, res=None, out=F32, norm_bwd=None, name):
    S = a.shape[0]
    tm = _divisor(S, 1024 if norm_bwd is None else 512, 16)
    row = pl.BlockSpec((tm, D_MODEL), lambda i, k: (i, 0))
    return _mm_call(name, (S // tm, 1), 1, a, row, (tm, D_MODEL), G1, _w128_spec(blk), (D_MODEL, D_MODEL), NT if tb else NN,
                    jax.ShapeDtypeStruct((S, D_MODEL), out), row, (tm, D_MODEL), res=res, res_spec=row, norm_bwd=norm_bwd)


def _mm_w128_dw(a, b, blk, into, name):
    S = a.shape[0]
    tk = _divisor(S, 1024, 16)
    row = pl.BlockSpec((tk, D_MODEL), lambda i, k: (k, 0))
    return _mm_call(name, (1, S // tk), 1, a, row, (tk, D_MODEL), b, row, (tk, D_MODEL), TN, None, _w128_spec(blk),
                    (D_MODEL, D_MODEL), into=into)


def _ffn_gate_up(h, norm_w, G2v, rb, name, job=None):
    S = h.shape[0]
    tm = _divisor(S, 1024, 16)
    grid = (S // tm, 4)

    def body(*refs):
        h_ref, nw_ref, w_ref = refs[:3]
        (n_ref, gu_ref, act_ref, n_scr), finish_job = _job_in_body(job, refs, 3, 3, 1, grid)

        @pl.when(pl.program_id(1) == 0)
        def _():
            x = h_ref[...]
            y = x * lax.rsqrt(jnp.mean(x * x, axis=-1, keepdims=True) + RMS_EPS)
            n_scr[...] = (y * nw_ref[...]).astype(BF16)
            n_ref[...] = n_scr[...]

        nv = n_scr[...]
        g = jnp.dot(nv, w_ref[0, 0], preferred_element_type=F32)
        u = jnp.dot(nv, w_ref[1, 0], preferred_element_type=F32)
        sg = jax.nn.sigmoid(g)
        silu = g * sg
        gu_ref[0, 0] = (u * (sg * (1.0 + g * (1.0 - sg)))).astype(BF16)
        gu_ref[1, 0] = silu.astype(BF16)
        act_ref[0] = (silu * u).astype(BF16)
        finish_job()

    row = pl.BlockSpec((tm, D_MODEL), lambda i, j: (i, 0))
    return _job_call(
        job, body, name=name, grid=grid,
        in_specs=[row, pl.BlockSpec((1, D_MODEL), lambda i, j: (0, 0)), pl.BlockSpec((2, 1, D_MODEL, FF_BLK), lambda i, j: (0, j, rb, 0))],
        out_specs=[row, pl.BlockSpec((2, 1, tm, FF_BLK), lambda i, j: (0, j, i, 0)), pl.BlockSpec((1, tm, FF_BLK), lambda i, j: (j, i, 0))],
        out_shape=[jax.ShapeDtypeStruct((S, D_MODEL), BF16), jax.ShapeDtypeStruct((2, 4, S, FF_BLK), BF16), jax.ShapeDtypeStruct((4, S, FF_BLK), BF16)],
        args=[h, norm_w, G2v], scratch_shapes=[pltpu.VMEM((tm, D_MODEL), BF16)], aliases={},
        dimension_semantics=("arbitrary" if job is not None else "parallel", "arbitrary"))


def _ffn_down(act, G1, ob, h, name, job=None):
    S = h.shape[0]
    tm = _divisor(S, 1024, 16)
    row = pl.BlockSpec((tm, D_MODEL), lambda i, k: (i, 0))
    return _mm_call(name, (S // tm, 4), 1, act, pl.BlockSpec((1, tm, FF_BLK), lambda i, k: (k, i, 0)), (tm, FF_BLK),
                    G1, pl.BlockSpec((2, DOWN_ROWS, D_MODEL), lambda i, k: (k, ob, 0)), (FF_BLK, D_MODEL), NN,
                    jax.ShapeDtypeStruct((S, D_MODEL), F32), row, (tm, D_MODEL), alpha=0.5, res=h, res_spec=row, job=job)


def _ffn_down_dx(dh, G1, ob, gu, name):
    S = dh.shape[0]
    tm = _divisor(S, 1024, 16)

    def body(dh_ref, w_ref, gu_ref, o_ref):
        w = w_ref[...].reshape(FF_BLK, D_MODEL)
        dact = lax.dot_general(dh_ref[...].astype(BF16), w, NT, preferred_element_type=F32) * 0.5
        o_ref[0, 0] = (dact * gu_ref[0, 0].astype(F32)).astype(BF16)
        o_ref[1, 0] = (dact * gu_ref[1, 0].astype(F32)).astype(BF16)

    blk = pl.BlockSpec((2, 1, tm, FF_BLK), lambda i, j: (0, j, i, 0))
    return pl.pallas_call(
        body, name=name, grid=(S // tm, 4),
        in_specs=[pl.BlockSpec((tm, D_MODEL), lambda i, j: (i, 0)), pl.BlockSpec((2, DOWN_ROWS, D_MODEL), lambda i, j: (j, ob, 0)), blk],
        out_specs=blk, out_shape=jax.ShapeDtypeStruct((2, 4, S, FF_BLK), BF16),
    )(dh, G1, gu)


def _ffn_down_dw(act, dh, name, job=None):
    S = dh.shape[0]
    tk = _divisor(S, 1024, 16)
    return _mm_call(name, (4, S // tk), 1, act, pl.BlockSpec((1, tk, FF_BLK), lambda j, k: (j, k, 0)), (tk, FF_BLK),
                    dh, pl.BlockSpec((tk, D_MODEL), lambda j, k: (k, 0)), (tk, D_MODEL), TN,
                    jax.ShapeDtypeStruct((N_DEV, DOWN_ROWS, D_MODEL), BF16),
                    pl.BlockSpec((2, DOWN_ROWS, D_MODEL), lambda j, k: (j, 0, 0)), (FF_BLK, D_MODEL), alpha=0.5, job=job)


def _ffn_gate_up_dw(n, dgu8, name, job=None):
    S = n.shape[0]
    tk = _divisor(S, 1024, 16)
    return _mm_call(name, (N_DEV, S // tk), 1, n, pl.BlockSpec((tk, D_MODEL), lambda b, k: (k, 0)), (tk, D_MODEL),
                    dgu8, pl.BlockSpec((1, tk, FF_BLK), lambda b, k: (b, k, 0)), (tk, FF_BLK), TN,
                    jax.ShapeDtypeStruct((N_DEV, D_MODEL, FF_BLK), BF16),
                    pl.BlockSpec((1, D_MODEL, FF_BLK), lambda b, k: (b, 0, 0)), (D_MODEL, FF_BLK), job=job)


def _ffn_gate_up_dx(dgu8, G2, rb, h, norm_w, dres, name, job=None):
    S = h.shape[0]
    tm = _divisor(S, 1024, 16)
    row = pl.BlockSpec((tm, D_MODEL), lambda i, k: (i, 0))
    return _mm_call(name, (S // tm, N_DEV), 1, dgu8, pl.BlockSpec((1, tm, FF_BLK), lambda i, k: (k, i, 0)), (tm, FF_BLK),
                    G2, pl.BlockSpec((1, D_MODEL, FF_BLK), lambda i, k: (k, rb, 0)), (D_MODEL, FF_BLK), NT,
                    jax.ShapeDtypeStruct((S, D_MODEL), F32), row, (tm, D_MODEL), norm_bwd=(h, norm_w, dres), job=job)


def _unheads(x):
    h, S, d = x.shape
    return jnp.transpose(x, (1, 0, 2)).reshape(S, h * d)


def _exact3(v):
    rnd = lambda a: lax.reduce_precision(a, exponent_bits=8, mantissa_bits=7)
    hi = rnd(v)
    mid = rnd(v - hi)
    return hi, mid, rnd(v - hi - mid)


def _causal_mask(st, q0, k0, window):
    dist = (q0 + lax.broadcasted_iota(jnp.int32, st.shape, 1)) - (k0 + lax.broadcasted_iota(jnp.int32, st.shape, 0))
    mask = dist >= 0
    if window is not None:
        mask = mask & (dist < window)
    return jnp.where(mask, st, NEG)


def _attn_fwd(qT, k, vT1, *, tile, hb, window=None, sink=None, name, job=None):
    H, dqk, S = qT.shape
    G = H // k.shape[0]
    dvp = vT1.shape[1]
    dv = dvp - 16
    tq = tk = tile
    assert H % hb == 0 and (G == 1 or G % hb == 0)
    kvb = hb if G == 1 else 1
    grid = (H // hb, S // tq)
    n_in = 3 + (sink is not None)

    def body(*refs):
        q_ref, k_ref, v_ref = refs[:3]
        (o_ref, lse_ref), finish_job = _job_in_body(job, refs, n_in, 2, 0, grid)
        i = pl.program_id(1)
        carry = []
        for a in range(hb):
            if sink is not None:
                carry.append(jnp.zeros((1, tq), F32) + refs[3][a, :, 0:1])
                carry.append(jnp.where(lax.broadcasted_iota(jnp.int32, (dvp, tq), 0) == dv, 1.0, 0.0))
            else:
                carry.append(jnp.full((1, tq), NEG, F32))
                carry.append(jnp.zeros((dvp, tq), F32))

        def step(j, carry, masked, off=None, keys=tk, q_from=0):
            off = pl.multiple_of(j * tk, tk) if off is None else off
            out = []
            for a in range(hb):
                m, acc = carry[2 * a], carry[2 * a + 1]
                kv = a if kvb > 1 else 0
                st = jnp.dot(k_ref[kv, pl.ds(off, keys), :], q_ref[a][:, q_from:], preferred_element_type=F32)
                if masked:
                    st = _causal_mask(st, i * tq + q_from, off, window)
                m_old, acc_old = m[:, q_from:], acc[:, q_from:]
                m_new = jnp.maximum(m_old, jnp.max(st, axis=0, keepdims=True))
                pt = jnp.exp(st - m_new).astype(BF16)
                acc_new = jnp.exp(m_old - m_new) * acc_old + jnp.dot(v_ref[kv, :, pl.ds(off, keys)], pt, preferred_element_type=F32)
                if q_from:
                    m_new = jnp.concatenate([m[:, :q_from], m_new], axis=1)
                    acc_new = jnp.concatenate([acc[:, :q_from], acc_new], axis=1)
                out += [m_new, acc_new]
            return tuple(out)

        carry = tuple(carry)
        if window is None:
            carry = lax.fori_loop(0, i, functools.partial(step, masked=False), carry)
            if tq % (2 * LANES) == 0:
                half = tq // 2
                carry = step(None, carry, True, off=pl.multiple_of(i * tq, tq), keys=half)
                carry = step(None, carry, True, off=pl.multiple_of(i * tq + half, half), keys=half, q_from=half)
            else:
                carry = step(i, carry, True)
        else:
            assert window % LANES == 0 and tq + window <= S
            carry = step(None, carry, True, off=pl.multiple_of(jnp.maximum(i * tq - window, 0), LANES), keys=tq + window)
        for a in range(hb):
            m, acc = carry[2 * a], carry[2 * a + 1]
            l = acc[dv:dv + 1, :]
            o_ref[a] = acc[:dv, :] / l
            lse_ref[a] = m + jnp.log(l)
        finish_job()

    kv_idx = (lambda b: b) if G == 1 else (lambda b: (b * hb) // G)
    in_specs = [
        pl.BlockSpec((hb, dqk, tq), lambda b, i: (b, 0, i)),
        pl.BlockSpec((kvb, S, dqk), lambda b, i: (kv_idx(b), 0, 0)),
        pl.BlockSpec((kvb, dvp, S), lambda b, i: (kv_idx(b), 0, 0)),
    ]
    args = [qT, k, vT1]
    if sink is not None:
        in_specs += [pl.BlockSpec((hb, 1, LANES), lambda b, i: (b, 0, 0))]
        args += [sink]
    return _job_call(
        job, body, name=name, grid=grid, in_specs=in_specs,
        out_specs=[pl.BlockSpec((hb, dv, tq), lambda b, i: (b, 0, i)), pl.BlockSpec((hb, 1, tq), lambda b, i: (b, 0, i))],
        out_shape=[jax.ShapeDtypeStruct((H, dv, S), F32), jax.ShapeDtypeStruct((H, 1, S), F32)],
        args=args, scratch_shapes=[], aliases={}, dimension_semantics=("arbitrary", "arbitrary") if job is not None else ("parallel", "parallel"))


def _attn_bwd(qT, k, kT, v, oT, doT, lse, *, tile, hb, window=None, sink=None, real=None, extra=False, full=False, name):
    H, dqk, S = qT.shape
    G = H // k.shape[0]
    dv = v.shape[2]
    tq = tk = tile
    nq = S // tq
    has_p = sink is not None
    real = dqk if real is None else real
    main = dqk if full else real
    assert H % hb == 0 and (G == 1 or G % hb == 0) and not (extra and real == dqk)
    kvb = hb if G == 1 else 1

    def body(*refs):
        qT_ref, k_ref, kT_ref, v_ref, oT_ref, doT_ref, lse_ref = refs[:7]
        p_ref = refs[7] if has_p else None
        pos = 8 if has_p else 7
        dq_ref, dk_ref, dv_ref = refs[pos: pos + 3]
        pos += 3
        ds_ref = refs[pos] if has_p else None
        pos += has_p
        dqx_ref, dkx_ref = (refs[pos], refs[pos + 1]) if extra else (None, None)
        delta = refs[-1]
        j = pl.program_id(1)

        @pl.when(j == 0)
        def _():
            dq_ref[...] = jnp.zeros_like(dq_ref)
            if extra:
                dqx_ref[...] = jnp.zeros_like(dqx_ref)
            for a in range(hb):
                drow = jnp.sum(doT_ref[a].astype(F32) * oT_ref[a], axis=0, keepdims=True)
                delta[a] = drow
                if has_p:
                    w = jnp.exp(p_ref[a, :, 0:1] - lse_ref[a])
                    ds_ref[a] = jnp.zeros((1, LANES), F32) - jnp.sum(w * drow, axis=1, keepdims=True)

        def step(i, carry, masked, off=None, qs=tq, keys=tk):
            off = pl.multiple_of(i * tq, tq) if off is None else off
            out = []
            for a in range(hb):
                dk, dvv = carry[2 * a], carry[2 * a + 1]
                kv = a if kvb > 1 else 0
                qTi = qT_ref[a, :, pl.ds(off, qs)]
                doTi = doT_ref[a, :, pl.ds(off, qs)]
                st = jnp.dot(k_ref[kv, pl.ds(0, keys), :], qTi, preferred_element_type=F32)
                if masked:
                    st = _causal_mask(st, off, j * tk, window)
                pt = jnp.exp(st - lse_ref[a, :, pl.ds(off, qs)])
                dv_new = lax.dot_general(pt.astype(BF16), doTi, NT, preferred_element_type=F32)
                dpt = jnp.dot(v_ref[kv, pl.ds(0, keys), :], doTi, preferred_element_type=F32)
                dsb = (pt * (dpt - delta[a, :, pl.ds(off, qs)])).astype(BF16)
                dk_new = lax.dot_general(dsb, qTi, NT, preferred_element_type=F32)
                if keys < tk:
                    dk = jnp.concatenate([dk[:keys] + dk_new, dk[keys:]], axis=0)
                    dvv = jnp.concatenate([dvv[:keys] + dv_new, dvv[keys:]], axis=0)
                else:
                    dk, dvv = dk + dk_new, dvv + dv_new
                dqt = jnp.dot(kT_ref[kv, :, pl.ds(0, keys)], dsb, preferred_element_type=F32)
                dq_ref[a, :, pl.ds(off, qs)] += dqt[:main]
                if extra:
                    dqx_ref[a, :, pl.ds(off, qs)] += dqt[real:]
                out += [dk, dvv]
            return tuple(out)

        carry = (jnp.zeros((tk, dqk), F32), jnp.zeros((tk, dv), F32)) * hb
        if window is None:
            if tk % (2 * LANES) == 0:
                half = tk // 2
                carry = step(None, carry, True, off=pl.multiple_of(j * tk + half, half), qs=half)
                carry = step(None, carry, True, off=pl.multiple_of(j * tk, tk), qs=half, keys=half)
            else:
                carry = step(j, carry, True)
            carry = lax.fori_loop(j + 1, nq, functools.partial(step, masked=False), carry)
        else:
            assert window % LANES == 0 and tk + window <= S
            carry = step(None, carry, True, off=pl.multiple_of(jnp.minimum(j * tk, S - (tk + window)), LANES), qs=tk + window)
        for a in range(hb):
            dk_ref[a] = carry[2 * a][:, :main]
            if extra:
                dkx_ref[a] = carry[2 * a][:, real:]
            dv_ref[a] = carry[2 * a + 1]

    kv_idx = (lambda b: b) if G == 1 else (lambda b: (b * hb) // G)
    colsT = lambda d: pl.BlockSpec((hb, d, S), lambda b, j: (b, 0, 0))
    in_specs = [
        colsT(dqk),
        pl.BlockSpec((kvb, tk, dqk), lambda b, j: (kv_idx(b), j, 0)),
        pl.BlockSpec((kvb, dqk, tk), lambda b, j: (kv_idx(b), 0, j)),
        pl.BlockSpec((kvb, tk, dv), lambda b, j: (kv_idx(b), j, 0)),
        colsT(dv), colsT(dv),
        pl.BlockSpec((hb, 1, S), lambda b, j: (b, 0, 0)),
    ]
    args = [qT, k, kT, v, oT, doT, lse]
    if has_p:
        in_specs += [pl.BlockSpec((hb, 1, LANES), lambda b, j: (b, 0, 0))]
        args += [sink]
    out_specs = [colsT(main), pl.BlockSpec((hb, tk, main), lambda b, j: (b, j, 0)), pl.BlockSpec((hb, tk, dv), lambda b, j: (b, j, 0))]
    out_shape = [jax.ShapeDtypeStruct((H, main, S), F32), jax.ShapeDtypeStruct((H, S, main), F32), jax.ShapeDtypeStruct((H, S, dv), F32)]
    if has_p:
        out_specs += [pl.BlockSpec((hb, 1, LANES), lambda b, j: (b, 0, 0))]
        out_shape += [jax.ShapeDtypeStruct((H, 1, LANES), F32)]
    if extra:
        out_specs += [colsT(dqk - real), pl.BlockSpec((hb, tk, dqk - real), lambda b, j: (b, j, 0))]
        out_shape += [jax.ShapeDtypeStruct((H, dqk - real, S), F32), jax.ShapeDtypeStruct((H, S, dqk - real), F32)]
    return pl.pallas_call(
        body, name=name, grid=(H // hb, S // tk), in_specs=in_specs, out_specs=out_specs, out_shape=out_shape,
        scratch_shapes=[pltpu.VMEM((hb, 1, S), F32)],
        compiler_params=pltpu.CompilerParams(dimension_semantics=("parallel", "arbitrary")),
    )(*args)


def _rows_and_cols(x3):
    xb = x3.astype(BF16)
    return jnp.transpose(xb, (1, 0, 2)), jnp.transpose(xb, (1, 2, 0))


def _cols_only(x3):
    return jnp.transpose(x3.astype(BF16), (1, 2, 0))


def _v_with_ones(v3):
    S, h, _ = v3.shape
    vT = jnp.transpose(v3.astype(BF16), (1, 2, 0))
    return jnp.concatenate([vT, jnp.ones((h, 1, S), BF16), jnp.zeros((h, 15, S), BF16)], axis=1)


def _from_T(oT):
    h, d, S = oT.shape
    return jnp.transpose(oT, (2, 0, 1)).reshape(S, h * d)


def _coords():
    return lax.axis_index("x"), lax.axis_index("y"), lax.axis_index("c")


def _peer(axis):
    x, y, c = _coords()
    return {"x": (1 - x, y, c), "y": (x, 1 - y, c), "c": (x, y, 1 - c)}[axis]


def _gather_job(bufs, rows=None):
    n = len(bufs)

    def copies(outs, send_sems, recv_sems):
        x, y, c = _coords()
        me, sibling = (x, y, c), (x, y, 1 - c)
        chips = [(1 - x, y), (x, 1 - y), (1 - x, 1 - y)]

        def copy(t, k, block, to):
            px, py, pc = block
            ref = outs[t].at[4 * px + 2 * py + pc]
            if rows is not None and rows[t] is not None:
                ref = ref.at[pl.ds(rows[t][0], rows[t][1])]
            return pltpu.make_async_remote_copy(ref, ref, send_sems.at[7 * t + k], recv_sems.at[7 * t + k], device_id=to, device_id_type=MESH)

        return copy, me, sibling, chips, c

    def start(ins, outs, send_sems, recv_sems):
        copy, me, sibling, chips, c = copies(outs, send_sems, recv_sems)
        for t in range(n):
            copy(t, 0, me, sibling).start()
            for j, chip in enumerate(chips):
                copy(t, 1 + j, me, (*chip, c)).start()

    def finish(ins, outs, send_sems, recv_sems):
        copy, me, sibling, chips, c = copies(outs, send_sems, recv_sems)
        for j, chip in enumerate(chips):
            for t in range(n):
                copy(t, 1 + j, (*chip, c), me).wait_recv()
                copy(t, 4 + j, (*chip, c), sibling).start()
        for t in range(n):
            copy(t, 0, sibling, me).wait_recv()
            for j, chip in enumerate(chips):
                copy(t, 4 + j, (*chip, 1 - c), me).wait_recv()
        for t in range(n):
            copy(t, 0, me, sibling).wait_send()
            for j, chip in enumerate(chips):
                copy(t, 1 + j, me, (*chip, c)).wait_send()
                copy(t, 4 + j, (*chip, c), sibling).wait_send()

    return dict(ins=list(bufs), outs=[jax.ShapeDtypeStruct(b.shape, b.dtype) for b in bufs], aliases={t: t for t in range(n)},
                n_sems=7 * n, start=start, finish=finish)


def _in_slot(local):
    x, y, c = _coords()
    buf = lax.empty((N_DEV,) + local.shape, local.dtype)
    return lax.dynamic_update_slice(buf, local[None], (4 * x + 2 * y + c, 0, 0))


def _pair_job(vs, axes):
    n = len(vs)
    axes = [axes] * n if isinstance(axes, str) else axes

    def copies(ins, outs, send_sems, recv_sems):
        out = []
        for t in range(n):
            me = lax.axis_index(axes[t])
            src = ins[t].at[1 - me] if len(ins[t].shape) == 3 else ins[t].at[:, 1 - me]
            out.append(pltpu.make_async_remote_copy(src, outs[t], send_sems.at[t], recv_sems.at[t], device_id=_peer(axes[t]), device_id_type=MESH))
        return out

    def start(*refs):
        for cp in copies(*refs):
            cp.start()

    def finish(*refs):
        for cp in copies(*refs):
            cp.wait()

    return dict(ins=list(vs), outs=[jax.ShapeDtypeStruct(v.shape[:-3] + v.shape[-2:], v.dtype) for v in vs], aliases={}, n_sems=n,
                start=start, finish=finish)


def _add_kept(v, got, axis, out, name):
    R, C = v.shape[-2:]
    lead = v.shape[0] if v.ndim == 4 else 1
    tm = _divisor(R, max(16, EW_TILE_BYTES // (_lanes(C) * (v.dtype.itemsize + got.dtype.itemsize + jnp.dtype(out).itemsize)) // 16 * 16), 16)
    me = lax.axis_index(axis).astype(jnp.int32).reshape(1)
    v4 = v.reshape(lead, 2, R, C)
    g3 = got.reshape(lead, R, C)

    def body(me_ref, v_ref, g_ref, o_ref):
        o_ref[...] = (v_ref[0].astype(F32) + g_ref[...].astype(F32)).astype(o_ref.dtype)

    res = pl.pallas_call(
        body, name=name, out_shape=jax.ShapeDtypeStruct((lead, R, C), out),
        grid_spec=pltpu.PrefetchScalarGridSpec(
            num_scalar_prefetch=1, grid=(lead, R // tm),
            in_specs=[pl.BlockSpec((1, 1, tm, C), lambda b, i, me: (b, me[0], i, 0)), pl.BlockSpec((1, tm, C), lambda b, i, me: (b, i, 0))],
            out_specs=pl.BlockSpec((1, tm, C), lambda b, i, me: (b, i, 0))),
    )(me, v4, g3)
    return res


def _cross_job(vs):
    n = len(vs)

    def copies(ins, outs, send_sems, recv_sems):
        x, y, _ = _coords()
        out = []
        for t in range(n):
            h = ins[t].shape[2] // 2
            out.append(pltpu.make_async_remote_copy(ins[t].at[1 - x, :, pl.ds(0, h)], outs[2 * t], send_sems.at[2 * t], recv_sems.at[2 * t],
                                                    device_id=_peer("x"), device_id_type=MESH))
            out.append(pltpu.make_async_remote_copy(ins[t].at[:, 1 - y, pl.ds(h, h)], outs[2 * t + 1], send_sems.at[2 * t + 1], recv_sems.at[2 * t + 1],
                                                    device_id=_peer("y"), device_id_type=MESH))
        return out

    def start(*refs):
        for cp in copies(*refs):
            cp.start()

    def finish(*refs):
        for cp in copies(*refs):
            cp.wait()

    outs = []
    for v in vs:
        outs += [jax.ShapeDtypeStruct((2, v.shape[2] // 2, v.shape[3]), v.dtype)] * 2
    return dict(ins=list(vs), outs=outs, aliases={}, n_sems=2 * n, start=start, finish=finish)


def _add_picked(v, got, axis, out, name):
    _, _, R, C = v.shape
    h = R // 2
    tm = _divisor(h, max(16, EW_TILE_BYTES // (_lanes(C) * (v.dtype.itemsize + got.dtype.itemsize + jnp.dtype(out).itemsize)) // 16 * 16), 16)
    me = lax.axis_index(axis).astype(jnp.int32).reshape(1)
    if axis == "x":
        v_map = lambda b, i, me: (me[0], b, i, 0)
    else:
        v_map = lambda b, i, me: (b, me[0], i + h // tm, 0)

    def body(me_ref, v_ref, g_ref, o_ref):
        o_ref[...] = (v_ref[0].astype(F32) + g_ref[...].astype(F32)).astype(o_ref.dtype)

    return pl.pallas_call(
        body, name=name, out_shape=jax.ShapeDtypeStruct((2, h, C), out),
        grid_spec=pltpu.PrefetchScalarGridSpec(
            num_scalar_prefetch=1, grid=(2, h // tm),
            in_specs=[pl.BlockSpec((1, 1, tm, C), v_map), pl.BlockSpec((1, tm, C), lambda b, i, me: (b, i, 0))],
            out_specs=pl.BlockSpec((1, tm, C), lambda b, i, me: (b, i, 0))),
    )(me, v, got)


def _reduce_scatter_steps(gs, tag):
    n = len(gs)
    vs = [g.reshape(4, 2, *g.shape[1:]) for g in gs]
    got = yield _pair_job(vs, "c")
    vs = [_add_kept(v, r, "c", BF16, f"rs_{tag}_add_c{t}") for t, (v, r) in enumerate(zip(vs, got))]
    vs = [v.reshape(2, 2, v.shape[1], v.shape[2]) for v in vs]
    got = yield _cross_job(vs)
    up = [_add_picked(v, r, "x", BF16, f"rs_{tag}_add_x{t}") for t, (v, r) in enumerate(zip(vs, got[0::2]))]
    lo = [_add_picked(v, r, "y", BF16, f"rs_{tag}_add_y{t}") for t, (v, r) in enumerate(zip(vs, got[1::2]))]
    got = yield _pair_job(up + lo, ["y"] * n + ["x"] * n)
    out = []
    for t in range(n):
        a = _add_kept(up[t], got[t], "y", F32, f"rs_{tag}_add_y2{t}")[0]
        b = _add_kept(lo[t], got[n + t], "x", F32, f"rs_{tag}_add_x2{t}")[0]
        out.append(jnp.concatenate([a, b], axis=0))
    return out


def _reduce_scatter(gs, tag):
    steps = _reduce_scatter_steps(gs, tag)
    job = next(steps)
    for stage in ("c", "xy", "yx"):
        got = _comm_call(job, f"rs_{tag}_{stage}")
        try:
            job = steps.send(got)
        except StopIteration as done:
            return done.value


def _all_reduce_small(v):
    def body(v_ref, o_ref, buf, send_sems, recv_sems):
        x, y, c = _coords()
        me = 4 * x + 2 * y + c
        buf[me] = v_ref[...]
        copies = []
        for k in range(1, N_DEV):
            peer = tuple((1 - a) if (k >> s) & 1 else a for a, s in ((x, 2), (y, 1), (c, 0)))
            cp = pltpu.make_async_remote_copy(v_ref, buf.at[me], send_sems.at[k - 1], recv_sems.at[k - 1], device_id=peer, device_id_type=MESH)
            cp.start()
            copies.append(cp)
        for cp in copies:
            cp.wait()
        acc = buf[0]
        for d in range(1, N_DEV):
            acc = acc + buf[d]
        o_ref[...] = acc

    vm = pl.BlockSpec(memory_space=pltpu.VMEM)
    return pl.pallas_call(
        body, name="all_reduce_small", in_specs=[vm], out_specs=vm, out_shape=jax.ShapeDtypeStruct(v.shape, F32),
        scratch_shapes=[pltpu.VMEM((N_DEV,) + v.shape, F32), pltpu.SemaphoreType.DMA((N_DEV - 1,)), pltpu.SemaphoreType.DMA((N_DEV - 1,))],
    )(v)


def _local_groups(w, dtype):
    mix_out = [w["ev_w_out"][0], w["od_w_out"][0]]
    layers = []
    for l in range(DEPTH):
        a = jnp.concatenate([w["ffa_w_down"][l], w["ffb_w_down"][l]], axis=0).astype(dtype)
        b = jnp.concatenate([w["ple_w_gate"][l], mix_out[l]], axis=0).astype(dtype)
        c = jnp.concatenate([w["ffa_w_gate_up"][l], w["ffb_w_gate_up"][l]], axis=0).astype(dtype)
        layers.append((a, b, c))
    strip = jnp.concatenate([w["ple_w_proj"].reshape(-1, STRIP_C), w["ev_w_ukv"][0], jnp.pad(w["ev_w_uq"][0], ((0, 0), (0, STRIP_C - 96))),
                             jnp.zeros((G3_ROWS - 896, STRIP_C), F32)], axis=0)
    m = jnp.concatenate([w["od_w_in"][0], w["ev_w_in"][0], strip, jnp.zeros((G3_ROWS, G3_COLS - STRIP0 - STRIP_C), F32)], axis=1).astype(dtype)
    return layers, m


def _ungroup_local(a, b, c, r3):
    out = {
        "ffa_w_down": jnp.stack([x[0] for x in a]), "ffb_w_down": jnp.stack([x[1] for x in a]),
        "ple_w_gate": jnp.stack([x[:128] for x in b]), "ev_w_out": b[0][128:][None], "od_w_out": b[1][128:][None],
        "ffa_w_gate_up": jnp.stack([x[0] for x in c]), "ffb_w_gate_up": jnp.stack([x[1] for x in c]),
        "od_w_in": r3[:, :OD_C][None], "ev_w_in": r3[:, OD_C:STRIP0][None],
    }
    strip = r3[:, STRIP0:STRIP0 + STRIP_C]
    out["ple_w_proj"] = strip[:512].reshape(2, PLE_DIM, STRIP_C)
    out["ev_w_ukv"] = strip[512:640][None]
    out["ev_w_uq"] = strip[640:896, :96][None]
    return out


def _cols(a):
    return jnp.transpose(a, (1, 0, 2)).reshape(a.shape[1], -1)


def _blocks(g, c):
    return jnp.transpose(g.reshape(g.shape[0], N_DEV, c), (1, 0, 2))


def _uq_permute(w):
    r = w.shape[0]
    w3 = w.reshape(r, B_HEADS, B_NOPE + B_ROPE)
    half = B_ROPE // 2
    return jnp.concatenate([w3[:, :, :B_NOPE].reshape(r, -1), w3[:, :, B_NOPE:B_NOPE + half].reshape(r, -1), w3[:, :, B_NOPE + half:].reshape(r, -1)], axis=1)


def _uq_unpermute(g):
    r = g.shape[0]
    half = B_ROPE // 2
    n = B_HEADS * B_NOPE
    parts = [g[:, :n].reshape(r, B_HEADS, B_NOPE), g[:, n:n + B_HEADS * half].reshape(r, B_HEADS, half), g[:, n + B_HEADS * half:].reshape(r, B_HEADS, half)]
    return jnp.concatenate(parts, axis=2).reshape(r, -1)


def _ukv_permute(w):
    r = w.shape[0]
    return jnp.transpose(w.reshape(r, B_HEADS, 2, B_NOPE), (0, 2, 1, 3)).reshape(r, -1)


def _ukv_unpermute(g):
    r = g.shape[0]
    return jnp.transpose(g.reshape(r, 2, B_HEADS, B_NOPE), (0, 2, 1, 3)).reshape(r, -1)


def _od_in_widen(w):
    n = C_HEADS * C_HEAD_DIM
    wide = lambda m: jnp.pad(m.reshape(-1, C_HEADS, C_HEAD_DIM), ((0, 0), (0, 0), (0, QK_PAD - C_HEAD_DIM))).reshape(m.shape[0], -1)
    return jnp.concatenate([wide(w[:, :n] * C_HEAD_DIM ** -0.5), wide(w[:, n:2 * n]), w[:, 2 * n:],
                            jnp.zeros((w.shape[0], ODD_IN_PAD - ODD_IN_AUG), w.dtype)], axis=1)


def _od_in_narrow(g):
    wp = C_HEADS * QK_PAD
    narrow = lambda m: m.reshape(-1, C_HEADS, QK_PAD)[:, :, :C_HEAD_DIM].reshape(m.shape[0], -1)
    return jnp.concatenate([narrow(g[:, :wp]) * C_HEAD_DIM ** -0.5, narrow(g[:, wp:2 * wp]), g[:, 2 * wp:ODD_IN_AUG]], axis=1)


def _misc_weights(G3):
    strip = G3[:, :, STRIP0:STRIP0 + STRIP_C]
    return {
        "od_w_in": _od_in_widen(_cols(G3[:, :, :OD_C])),
        "ev_w_in": jnp.pad(_cols(G3[:, :, OD_C:STRIP0]), ((0, 0), (0, EVEN_IN_PAD - EVEN_IN))),
        "ple_w_proj": [_cols(strip[:, i * PLE_DIM:(i + 1) * PLE_DIM]) for i in range(DEPTH)],
        "ev_w_ukv": _ukv_permute(_cols(strip[:, 512:640])),
        "ev_w_uq": _uq_permute(_cols(strip[:, 640:896, :96])),
    }


def _misc_grads(G):
    strip = jnp.concatenate([
        _blocks(G["ple_w_proj"][0], STRIP_C), _blocks(G["ple_w_proj"][1], STRIP_C), _blocks(_ukv_unpermute(G["ev_w_ukv"]), STRIP_C),
        jnp.pad(_blocks(_uq_unpermute(G["ev_w_uq"]), 96), ((0, 0), (0, 0), (0, STRIP_C - 96))),
        jnp.zeros((N_DEV, G3_ROWS - 896, STRIP_C), F32)], axis=1)
    return jnp.concatenate([_blocks(_od_in_narrow(G["od_w_in"]), OD_C), _blocks(G["ev_w_in"][:, :EVEN_IN], EV_C), strip,
                            jnp.zeros((N_DEV, G3_ROWS, G3_COLS - STRIP0 - STRIP_C), F32)], axis=2)


def _ffn_fwd(h, norm_w, W, f, i, tag, ride=None):
    job = ride() if ride else None
    res = _ffn_gate_up(h, norm_w, W["C"][i].reshape(2, 4, C_ROWS, FF_BLK), f, f"{tag}_gate_up", job=job)
    n, gu, act = res[:3]
    if job is not None:
        ride(res[3:])
    job = ride() if ride else None
    out = _ffn_down(act, W["A"][i], f, h, f"{tag}_down", job=job)
    if job is not None:
        out, got = out
        ride(got)
    return out, (h, n, gu, act)


def _ffn_bwd(dout, saved, norm_w, W, GB, f, i, tag, ride=None):
    h, n, gu, act = saved
    S = h.shape[0]
    def carried(call):
        job = ride() if ride else None
        res = call(job)
        if job is None:
            return res
        ride(res[1])
        return res[0]

    GB["A"][i][f] = carried(lambda job: _ffn_down_dw(act, dout, f"{tag}_down_dw", job=job))
    dgu = _ffn_down_dx(dout, W["A"][i], f, gu, f"{tag}_down_dx").reshape(N_DEV, S, FF_BLK)
    res = carried(lambda job: _ffn_gate_up_dx(dgu, W["C"][i], f, h, norm_w, dout, f"{tag}_gate_up_dx", job=job))
    GB["C"][i][f] = carried(lambda job: _ffn_gate_up_dw(n, dgu, f"{tag}_gate_up_dw", job=job))
    return res


def _rope_tables(S):
    inv = ROPE_THETA ** (-jnp.arange(0, B_ROPE, 2, dtype=F32) / B_ROPE)
    ang = jnp.arange(S, dtype=F32)[:, None] * inv[None, :]
    return jnp.cos(ang), jnp.sin(ang)


def _alibi_columns(S):
    t = jnp.arange(S, dtype=jnp.int32)
    hi = ((t // 16) * 16).astype(F32)
    lo = (t % 16).astype(F32)
    slopes = 2.0 ** (-8.0 * jnp.arange(1, A_HEADS + 1, dtype=F32) / A_HEADS)
    zq = jnp.zeros((S, A_HEADS), F32)
    rest = QK_PAD - A_HEAD_DIM - 4
    qc = jnp.stack([-slopes[None, :] * hi[:, None], -slopes[None, :] * lo[:, None], zq + slopes[None, :], zq + slopes[None, :]] + [zq] * rest, axis=-1)
    one = jnp.ones((S, A_KV_HEADS), F32)
    zk = jnp.zeros((S, A_KV_HEADS), F32)
    kc = jnp.stack([one, one, zk + hi[:, None], zk + lo[:, None]] + [zk] * rest, axis=-1)
    return qc, kc


def _sink_prm(sinks):
    return jnp.zeros((A_HEADS, 1, LANES), F32).at[:, 0, 0].set(sinks.astype(F32))


def _with_ride(ride, call):
    job = ride() if ride else None
    res = call(job)
    if job is None:
        return res
    n_own = len(res) - len(job["outs"])
    ride(res[n_own:])
    return res[:n_own]


def _even_fwd(hn, h, W, ride=None):
    S = hn.shape[0]
    proj = _mm(hn, W["ev_w_in"], name="ev_in")
    a_q, a_k, a_v = proj[:, :512], proj[:, 512:640], proj[:, 640:768]
    c_q, c_kv = proj[:, 768:1024], proj[:, 1024:1152]
    kr1, kr2 = proj[:, 1152:1168], proj[:, 1168:1184]
    qc, kc = _alibi_columns(S)
    qaT = _cols_only(jnp.concatenate([(a_q * A_HEAD_DIM ** -0.5).reshape(S, A_HEADS, A_HEAD_DIM), qc], axis=-1))
    ka, kaT = _rows_and_cols(jnp.concatenate([a_k.reshape(S, A_KV_HEADS, A_HEAD_DIM), kc], axis=-1))
    va3 = a_v.reshape(S, A_KV_HEADS, A_HEAD_DIM)
    va = jnp.transpose(va3.astype(BF16), (1, 0, 2))
    prm = _sink_prm(W["ev_sinks"][0])
    oaT, lse_a = _with_ride(ride, lambda job: _attn_fwd(qaT, ka, _v_with_ones(va3), tile=min(SWA_TILE, S // 2), hb=A_GROUP, window=WINDOW, sink=prm,
                                                        name="swa_fwd", job=job))
    cqn = _rms_fwd(c_q, W["ev_cq_norm"], "ev_cq_norm")
    q_all = _mm(cqn, W["ev_w_uq"], name="ev_uq")
    ckvn = _rms_fwd(c_kv, W["ev_ckv_norm"], "ev_ckv_norm")
    kv_all = _mm(ckvn, W["ev_w_ukv"], name="ev_ukv")
    cos, sin = _rope_tables(S)
    cos8, sin8 = jnp.tile(cos, (1, B_HEADS)), jnp.tile(sin, (1, B_HEADS))
    q1, q2 = _rope(q_all[:, 512:640], q_all[:, 640:768], cos8, sin8, "ev_rope_q")
    k1, k2 = _rope(kr1, kr2, cos, sin, "ev_rope_k")
    half = B_ROPE // 2
    scale = (B_NOPE + B_ROPE) ** -0.5
    qbT = _cols_only(jnp.concatenate([q_all[:, :512].reshape(S, B_HEADS, B_NOPE), q1.reshape(S, B_HEADS, half), q2.reshape(S, B_HEADS, half)], axis=-1) * scale)
    kro = jnp.broadcast_to(jnp.concatenate([k1, k2], axis=1)[:, None, :], (S, B_HEADS, B_ROPE))
    kb, kbT = _rows_and_cols(jnp.concatenate([kv_all[:, :512].reshape(S, B_HEADS, B_NOPE), kro], axis=-1))
    vb3 = kv_all[:, 512:].reshape(S, B_HEADS, B_V)
    vb = jnp.transpose(vb3.astype(BF16), (1, 0, 2))
    obT, lse_b = _with_ride(ride, lambda job: _attn_fwd(qbT, kb, _v_with_ones(vb3), tile=min(ATTN_TILE_FWD, S), hb=2, name="mla_fwd", job=job))
    cat = jnp.concatenate([_from_T(oaT), _from_T(obT)], axis=1)
    out = _mm_w128(cat, W["B"][0], MIX_OUT_BLK, res=h, name="ev_out")
    return out, (hn, proj, (qaT, ka, kaT, va, oaT, lse_a), prm, cqn, ckvn, (qbT, kb, kbT, vb, obT, lse_b), cat)


def _even_bwd(dout, saved, W, GB, norm):
    hn, proj, (qaT, ka, kaT, va, oaT, lse_a), prm, cqn, ckvn, (qbT, kb, kbT, vb, obT, lse_b), cat = saved
    S = hn.shape[0]
    G = {}
    dcat = _mm_w128(dout, W["B"][0], MIX_OUT_BLK, tb=True, out=BF16, name="ev_out_dx")
    GB["B"][0] = _mm_w128_dw(cat, dout, MIX_OUT_BLK, GB["B"][0], "ev_out_dw")
    doaT = _cols_only(dcat[:, :512].reshape(S, A_HEADS, A_HEAD_DIM))
    dqaT, dka, dva, dsink = _attn_bwd(qaT, ka, kaT, va, oaT, doaT, lse_a, tile=min(SWA_TILE, S // 2), hb=A_GROUP, window=WINDOW, sink=prm, real=A_HEAD_DIM,
                                       name="swa_bwd")
    G["ev_sinks"] = dsink[:, 0, 0]
    dqa = _from_T(dqaT) * A_HEAD_DIM ** -0.5
    dka = dka.reshape(A_KV_HEADS, A_GROUP, S, A_HEAD_DIM).sum(axis=1)
    dva = dva.reshape(A_KV_HEADS, A_GROUP, S, A_HEAD_DIM).sum(axis=1)
    dobT = _cols_only(dcat[:, 512:].reshape(S, B_HEADS, B_V))
    dqbT, dkb, dvb = _attn_bwd(qbT, kb, kbT, vb, obT, dobT, lse_b, tile=min(ATTN_TILE, S), hb=1, name="mla_bwd")
    half = B_ROPE // 2
    dqb = jnp.transpose(dqbT, (2, 0, 1)) * (B_NOPE + B_ROPE) ** -0.5
    dkb = jnp.transpose(dkb, (1, 0, 2))
    cos, sin = _rope_tables(S)
    cos8, sin8 = jnp.tile(cos, (1, B_HEADS)), jnp.tile(sin, (1, B_HEADS))
    dq1, dq2 = _rope(dqb[:, :, B_NOPE:B_NOPE + half].reshape(S, -1), dqb[:, :, B_NOPE + half:].reshape(S, -1), cos8, -sin8, "ev_rope_q_bwd")
    dq_all = jnp.concatenate([dqb[:, :, :B_NOPE].reshape(S, -1), dq1, dq2], axis=1).astype(BF16)
    dkr = dkb[:, :, B_NOPE:].sum(axis=1)
    dk1, dk2 = _rope(dkr[:, :half], dkr[:, half:], cos, -sin, "ev_rope_k_bwd")
    dkv_all = jnp.concatenate([dkb[:, :, :B_NOPE].reshape(S, -1), _unheads(dvb)], axis=1).astype(BF16)
    G["ev_w_uq"] = _mm(cqn, dq_all, ta=True, name="ev_uq_dw")
    dcqn = _mm(dq_all, W["ev_w_uq"], tb=True, name="ev_uq_dx")
    dc_q, G["ev_cq_norm"] = _rms_bwd(dcqn, proj[:, 768:1024], W["ev_cq_norm"], None, "ev_cq_norm_bwd")
    G["ev_w_ukv"] = _mm(ckvn, dkv_all, ta=True, name="ev_ukv_dw")
    dckvn = _mm(dkv_all, W["ev_w_ukv"], tb=True, name="ev_ukv_dx")
    dc_kv, G["ev_ckv_norm"] = _rms_bwd(dckvn, proj[:, 1024:1152], W["ev_ckv_norm"], None, "ev_ckv_norm_bwd")
    dproj = jnp.concatenate([dqa, _unheads(dka), _unheads(dva), dc_q, dc_kv, dk1, dk2,
                             jnp.zeros((S, EVEN_IN_PAD - EVEN_IN), F32)], axis=1).astype(BF16)
    G["ev_w_in"] = _mm(hn, dproj, ta=True, name="ev_in_dw")
    dh, dnorm = _mm(dproj, W["ev_w_in"], tb=True, norm_bwd=(*norm, dout), name="ev_in_dx")
    return dh, dnorm, G


def _odd_fwd(hn, h, W, ride=None):
    S = hn.shape[0]
    w = C_HEADS * C_HEAD_DIM
    wp = C_HEADS * QK_PAD
    proj = _mm(hn, W["od_w_in"], name="od_in")
    f_logit = proj[:, 2 * wp + w: 2 * wp + w + C_HEADS]
    logf = _logsig_fwd(f_logit, W["od_b_f"], "od_logsig")
    logc = _cumsum(logf, False, "od_cumsum")
    parts = list(_exact3(logc))
    ones = [jnp.ones((S, C_HEADS), F32)] * 3
    pad = [jnp.zeros((S, C_HEADS), F32)] * (QK_PAD - C_HEAD_DIM - 6)
    lead = ((0, 0), (0, 0), (C_HEAD_DIM, 0))
    q3 = proj[:, :wp].reshape(S, C_HEADS, QK_PAD) + jnp.pad(jnp.stack(parts + ones + pad, axis=-1), lead)
    k3 = proj[:, wp:2 * wp].reshape(S, C_HEADS, QK_PAD) + jnp.pad(jnp.stack(ones + [-p for p in parts] + pad, axis=-1), lead)
    qT = _cols_only(q3)
    k, kT = _rows_and_cols(k3)
    v3 = proj[:, 2 * wp:2 * wp + w].reshape(S, C_HEADS, C_HEAD_DIM)
    v = jnp.transpose(v3.astype(BF16), (1, 0, 2))
    oT, lse = _with_ride(ride, lambda job: _attn_fwd(qT, k, _v_with_ones(v3), tile=min(ATTN_TILE_FWD, S), hb=2, name="fox_fwd", job=job))
    cat = _from_T(oT)
    out = _mm_w128(cat, W["B"][1], MIX_OUT_BLK, res=h, name="od_out")
    return out, (hn, qT, k, kT, v, f_logit, oT, lse, cat)


def _odd_bwd(dout, saved, W, GB, norm):
    hn, qT, k, kT, v, f_logit, oT, lse, cat = saved
    S = hn.shape[0]
    G = {}
    dcat = _mm_w128(dout, W["B"][1], MIX_OUT_BLK, tb=True, out=BF16, name="od_out_dx")
    GB["B"][1] = _mm_w128_dw(cat, dout, MIX_OUT_BLK, GB["B"][1], "od_out_dw")
    doT = _cols_only(dcat.reshape(S, C_HEADS, C_HEAD_DIM))
    dqT, dk, dv, dqxT, dkx = _attn_bwd(qT, k, kT, v, oT, doT, lse, tile=min(ATTN_TILE, S), hb=1, real=C_HEAD_DIM, extra=True,
                                       full=True, name="fox_bwd")
    dlogc = jnp.transpose(dqxT[:, 0, :] - dkx[:, :, 3])
    dlogf = _cumsum(dlogc, True, "od_cumsum_bwd")
    df, db = _logsig_bwd(dlogf, f_logit, W["od_b_f"], "od_logsig_bwd")
    G["od_b_f"] = db
    dproj = jnp.concatenate([_from_T(dqT), _unheads(dk), _unheads(dv), df, jnp.zeros((S, ODD_IN_PAD - ODD_IN_AUG), F32)], axis=1).astype(BF16)
    G["od_w_in"] = _mm(hn, dproj, ta=True, name="od_in_dw")
    dh, dnorm = _mm(dproj, W["od_w_in"], tb=True, norm_bwd=(*norm, dout), name="od_in_dx")
    return dh, dnorm, G


class _Rider:
    def __init__(self, steps, tag):
        self.steps, self.tag, self.count, self.result = steps, tag, 0, None
        self.job = next(steps)

    def __call__(self, got=None):
        if got is not None:
            return self._advance(list(got))
        job = self.job
        if isinstance(job, str):
            self._advance(None)
            return None
        return job

    def _advance(self, value):
        try:
            self.job = self.steps.send(value)
        except StopIteration as done:
            self.job, self.result = None, done.value

    def finish(self):
        while self.job is not None:
            if isinstance(self.job, str):
                self._advance(None)
                continue
            self.count += 1
            self(_comm_call(self.job, f"{self.tag}_{self.count}"))
        return self.result


def _gather_plan(W, slots):
    a0, b0, c0, m, a1, b1, c1 = (slots[key] for key in ("a0", "b0", "c0", "m", "a1", "b1", "c1"))
    (m,) = yield _gather_job([m])
    W.update(_misc_weights(m))
    (b0,) = yield _gather_job([b0])
    W["B"] = [b0]
    (c0,) = yield _gather_job([c0], rows=[(D_MODEL, D_MODEL)])
    W["C"] = [c0]
    a0, c1 = yield _gather_job([a0, c1], rows=[(DOWN_ROWS, DOWN_ROWS), (0, D_MODEL)])
    W["A"] = [a0]
    W["C"].append(c1)
    (a1,) = yield _gather_job([a1], rows=[(0, DOWN_ROWS)])
    W["A"].append(a1)
    for _ in range(3):
        yield "skip"
    a1, b1, c1 = yield _gather_job([a1, b1, c1], rows=[(DOWN_ROWS, DOWN_ROWS), None, (D_MODEL, D_MODEL)])
    W["A"][1], W["C"][1] = a1, c1
    W["B"].append(b1)


def _local_step(x, p, target, W, slots):
    h = x
    saved = []
    gather = _Rider(_gather_plan(W, slots), "all_gather_rest")
    for i in range(DEPTH):
        t = f"l{i}"
        h1, s_a = _ffn_fwd(h, W["ffa_norm"][i:i + 1], W, 0, i, f"{t}_ffa", gather)
        nm = _rms_fwd(h1, W["mix_norm"][i:i + 1], f"{t}_mix_norm")
        h2, s_m = (_even_fwd if i % 2 == 0 else _odd_fwd)(nm, h1, W, gather)
        h3, s_b = _ffn_fwd(h2, W["ffb_norm"][i:i + 1], W, 1, i, f"{t}_ffb", gather)
        npl = _rms_fwd(h3, W["ple_norm"][i:i + 1], f"{t}_ple_norm")
        gpre = _mm_w128(npl, W["B"][i], PLE_GATE_BLK, name=f"{t}_ple_gate")
        pp = _mm(p[i], W["ple_w_proj"][i], name=f"{t}_ple_proj")
        h4 = _ple_fwd(h3, gpre, pp, f"{t}_ple")
        saved.append((s_a, h1, s_m, s_b, h3, npl, gpre, pp))
        h = h4
    gather.finish()
    dh, g_final, loss_cols = _final_fwd_bwd(h, W["final_norm"], target, "final")
    G = {"final_norm": g_final}
    GB = {"A": [[None, None] for _ in range(DEPTH)], "C": [[None, None] for _ in range(DEPTH)],
          "B": [lax.empty((N_DEV, B_ROWS, D_MODEL), BF16) for _ in range(DEPTH)]}
    per_layer = {n: [None] * DEPTH for n in ("ffa_norm", "mix_norm", "ffb_norm", "ple_norm", "ple_w_proj")}
    scatter = scatter_mid = None
    for i in reversed(range(DEPTH)):
        t = f"l{i}"
        s_a, h1, s_m, s_b, h3, npl, gpre, pp = saved[i]
        dgpre, dpp = _ple_bwd(dh, gpre, pp, f"{t}_ple_bwd")
        per_layer["ple_w_proj"][i] = _mm(p[i], dpp, ta=True, name=f"{t}_ple_proj_dw")
        GB["B"][i] = _mm_w128_dw(npl, dgpre, PLE_GATE_BLK, GB["B"][i], f"{t}_ple_gate_dw")
        dh, per_layer["ple_norm"][i] = _mm_w128(dgpre, W["B"][i], PLE_GATE_BLK, tb=True, norm_bwd=(h3, W["ple_norm"][i:i + 1], dh),
                                                name=f"{t}_ple_gate_dx")
        dh, per_layer["ffb_norm"][i] = _ffn_bwd(dh, s_b, W["ffb_norm"][i:i + 1], W, GB, 1, i, f"{t}_ffb", scatter)
        dh, per_layer["mix_norm"][i], g_mix = (_even_bwd if i % 2 == 0 else _odd_bwd)(dh, s_m, W, GB, (h1, W["mix_norm"][i:i + 1]))
        G.update(g_mix)
        if i == 0:
            G["ple_w_proj"] = per_layer["ple_w_proj"]
            mid = [GB["A"][0][1], GB["C"][0][1], GB["B"][0], _misc_grads(G).astype(BF16)]
            scatter_mid = _Rider(_reduce_scatter_steps(mid, "mid"), "rs_mid")
        dh, per_layer["ffa_norm"][i] = _ffn_bwd(dh, s_a, W["ffa_norm"][i:i + 1], W, GB, 0, i, f"{t}_ffa", scatter_mid)
        if i == DEPTH - 1:
            later = [GB["A"][i][0], GB["A"][i][1], GB["C"][i][0], GB["C"][i][1], GB["B"][i]]
            scatter = _Rider(_reduce_scatter_steps(later, "later"), "rs_later")
    for n in ("ffa_norm", "mix_norm", "ffb_norm", "ple_norm"):
        G[n] = jnp.concatenate(per_layer[n], axis=0)
    return loss_cols, dh, scatter.finish(), scatter_mid.finish(), [GB["A"][0][0], GB["C"][0][0]], G


def kernel(x, p, ffa_norm, ffa_w_gate_up, ffa_w_down, mix_norm, ffb_norm, ffb_w_gate_up, ffb_w_down, ple_norm, ple_w_gate, ple_w_proj, ev_w_in, ev_sinks, ev_cq_norm, ev_w_uq, ev_ckv_norm, ev_w_ukv, ev_w_out, od_w_in, od_b_f, od_w_out, final_norm, loss_target, m_ffa_norm, m_ffa_w_gate_up, m_ffa_w_down, m_mix_norm, m_ffb_norm, m_ffb_w_gate_up, m_ffb_w_down, m_ple_norm, m_ple_w_gate, m_ple_w_proj, m_ev_w_in, m_ev_sinks, m_ev_cq_norm, m_ev_w_uq, m_ev_ckv_norm, m_ev_w_ukv, m_ev_w_out, m_od_w_in, m_od_b_f, m_od_w_out, m_final_norm, v_ffa_norm, v_ffa_w_gate_up, v_ffa_w_down, v_mix_norm, v_ffb_norm, v_ffb_w_gate_up, v_ffb_w_down, v_ple_norm, v_ple_w_gate, v_ple_w_proj, v_ev_w_in, v_ev_sinks, v_ev_cq_norm, v_ev_w_uq, v_ev_ckv_norm, v_ev_w_ukv, v_ev_w_out, v_od_w_in, v_od_b_f, v_od_w_out, v_final_norm):
    given = dict(locals())
    w_in = {n: given[n] for n in WEIGHTS}

    layers, misc = _local_groups(w_in, BF16)
    (a0, b0, c0), (a1, b1, c1) = [[_in_slot(g) for g in layer] for layer in layers]
    a0, c0 = _comm_call(_gather_job([a0, c0], rows=[(0, DOWN_ROWS), (0, D_MODEL)]), "all_gather_first")
    W = {n: w_in[n] for n in SMALL}
    W["final_norm"] = final_norm.reshape(1, -1)
    W.update(A=[a0], C=[c0])
    slots = dict(a0=a0, b0=b0, c0=c0, m=_in_slot(misc), a1=a1, b1=b1, c1=c1)

    loss_cols, dx, r_later, r_mid, last, G = _local_step(x[0], p[:, 0], loss_target[0], W, slots)

    a1f, a1b, c1f, c1b, b1 = r_later
    a0b, c0b, b0, r_misc = r_mid
    a0f, c0f = _reduce_scatter(last, "last")
    grads = _ungroup_local([[a0f, a0b], [a1f, a1b]], [b0, b1], [[c0f, c0b], [c1f, c1b]], r_misc)
    layout = [(n, int(np.prod(w_in[n].shape))) for n in SMALL]
    vec = jnp.concatenate([G[n].astype(F32).reshape(-1) for n, _ in layout] + [jnp.sum(loss_cols).reshape(1)])
    vec = jnp.pad(vec, (0, N_DEV * SMALL_COLS - vec.shape[0])).reshape(N_DEV, SMALL_COLS)
    vec = _all_reduce_small(vec).reshape(-1)
    off = 0
    for n, size in layout:
        grads[n] = vec[off: off + size].reshape(w_in[n].shape)
        off += size
    loss = vec[off]

    delta, new_m, new_v = {}, {}, {}
    for n in WEIGHTS:
        shp = w_in[n].shape
        as2d = (lambda a: a.reshape(1, -1)) if len(shp) == 1 else (lambda a: a)
        d, nm, nv = _adamw(as2d(w_in[n]), as2d(grads[n]), as2d(given["m_" + n]), as2d(given["v_" + n]), f"adamw_{n}")
        delta[n], new_m[n], new_v[n] = d.reshape(shp), nm.reshape(shp), nv.reshape(shp)
    return (loss, dx[None], *[grads[n] for n in WEIGHTS], *[delta[n] for n in WEIGHTS],
            *[new_m[n] for n in WEIGHTS], *[new_v[n] for n in WEIGHTS])
```

```python
import functools

import numpy as np
import jax
import jax.numpy as jnp
from jax import lax
from jax.experimental import pallas as pl
from jax.experimental.pallas import tpu as pltpu

F32 = jnp.float32
BF16 = jnp.bfloat16
MESH = pl.DeviceIdType.MESH

D_MODEL = 1024
D_FF = 2816
RMS_EPS = 1e-6
PLE_DIM = 256
A_HEADS, A_KV_HEADS, A_HEAD_DIM, WINDOW = 8, 2, 64, 128
A_GROUP = A_HEADS // A_KV_HEADS
B_HEADS, B_NOPE, B_ROPE, B_V = 8, 64, 32, 64
ROPE_THETA = 10000.0
C_HEADS, C_HEAD_DIM = 16, 64
EVEN_IN = 1184
EVEN_IN_PAD = 1280
ODD_IN_AUG = 2 * 16 * 80 + 1024 + 16
ODD_IN_PAD = 3840
DEPTH = 2
ADAM_LR, ADAM_B1, ADAM_B2, ADAM_EPS, ADAM_WD, ADAM_STEP = 0.001, 0.9, 0.999, 1e-08, 0.01, 10

N_DEV = 8
LANES = 128
EW_TILE_BYTES = 3 << 20
MM_VMEM_BYTES = 26 << 20
NEG = -1e30
ATTN_TILE = 1024
ATTN_TILE_FWD = 1024
SWA_TILE = 512
QK_PAD = 80

FF_BLK = D_FF // 4
DOWN_ROWS = D_FF // N_DEV
B_ROWS, C_ROWS, G3_ROWS, G3_COLS = 256, 2 * D_MODEL, 1024, 768
PLE_GATE_BLK, MIX_OUT_BLK = 0, 1
OD_C, EV_C, STRIP_C = 386, 148, 128
STRIP0 = OD_C + EV_C

SMALL = ["ffa_norm", "mix_norm", "ffb_norm", "ple_norm", "ev_sinks", "ev_cq_norm", "ev_ckv_norm", "od_b_f", "final_norm"]
WEIGHTS = ["ffa_norm", "ffa_w_gate_up", "ffa_w_down", "mix_norm", "ffb_norm", "ffb_w_gate_up", "ffb_w_down", "ple_norm",
           "ple_w_gate", "ple_w_proj", "ev_w_in", "ev_sinks", "ev_cq_norm", "ev_w_uq", "ev_ckv_norm", "ev_w_ukv", "ev_w_out",
           "od_w_in", "od_b_f", "od_w_out", "final_norm"]
SMALL_COLS = 1280


def _divisor(n, cap, mult):
    if n <= cap:
        return n
    for t in range(cap - cap % mult, 0, -mult):
        if n % t == 0:
            return t
    raise ValueError(f"no tile for {n} under {cap} in steps of {mult}")


def _lanes(c):
    return -(-c // LANES) * LANES


def _ew(fn, rows, vecs, outs, reds=(), *, name):
    R = rows[0].shape[0]
    per_row = sum(_lanes(a.shape[1]) * a.dtype.itemsize for a in rows) + sum(_lanes(c) * jnp.dtype(d).itemsize for c, d in outs)
    tm = _divisor(R, max(16, EW_TILE_BYTES // per_row // 16 * 16), 16) if R % 16 == 0 else R
    n_r, n_v, n_o = len(rows), len(vecs), len(outs)

    def body(*refs):
        ins = [r[...] for r in refs[: n_r + n_v]]
        res = fn(*ins)
        if not isinstance(res, (tuple, list)):
            res = (res,)
        o_refs = refs[n_r + n_v: n_r + n_v + n_o]
        r_refs = refs[n_r + n_v + n_o:]
        for ref, val in zip(o_refs, res[:n_o]):
            ref[...] = val.astype(ref.dtype)
        if r_refs:
            @pl.when(pl.program_id(0) == 0)
            def _():
                for ref in r_refs:
                    ref[...] = jnp.zeros_like(ref)
            for ref, val in zip(r_refs, res[n_o:]):
                ref[...] += val

    in_specs = [pl.BlockSpec((tm, a.shape[1]), lambda i: (i, 0)) for a in rows]
    in_specs += [pl.BlockSpec((1, a.shape[1]), lambda i: (0, 0)) for a in vecs]
    out_specs = [pl.BlockSpec((tm, c), lambda i: (i, 0)) for c, _ in outs]
    out_specs += [pl.BlockSpec((1, c), lambda i: (0, 0)) for c in reds]
    out_shape = [jax.ShapeDtypeStruct((R, c), d) for c, d in outs] + [jax.ShapeDtypeStruct((1, c), F32) for c in reds]
    res = pl.pallas_call(body, name=name, grid=(R // tm,), in_specs=in_specs, out_specs=out_specs, out_shape=out_shape)(*rows, *vecs)
    return res[0] if len(res) == 1 else res


def _rms_fwd(x, w, name):
    def fn(x, w):
        y = x * lax.rsqrt(jnp.mean(x * x, axis=-1, keepdims=True) + RMS_EPS)
        return y * w
    return _ew(fn, [x], [w], [(x.shape[1], BF16)], name=name)


def _rms_bwd(dn, x, w, dres, name):
    def fn(dn, x, *rest):
        w = rest[-1]
        r = lax.rsqrt(jnp.mean(x * x, axis=-1, keepdims=True) + RMS_EPS)
        xh = x * r
        gw = dn * w
        dx = r * (gw - xh * jnp.mean(gw * xh, axis=-1, keepdims=True))
        if len(rest) == 2:
            dx = dx + rest[0]
        return dx, jnp.sum(dn * xh, axis=0, keepdims=True)
    rows = [dn, x] + ([dres] if dres is not None else [])
    return _ew(fn, rows, [w], [(x.shape[1], F32)], [x.shape[1]], name=name)


def _ple_fwd(h, gpre, pp, name):
    return _ew(lambda h, g, q: h + jax.nn.sigmoid(g.astype(F32)) * q.astype(F32), [h, gpre, pp], [], [(h.shape[1], F32)], name=name)


def _ple_bwd(dh, gpre, pp, name):
    def fn(dh, g, q):
        sg = jax.nn.sigmoid(g.astype(F32))
        return dh * q.astype(F32) * (sg * (1.0 - sg)), dh * sg
    return _ew(fn, [dh, gpre, pp], [], [(dh.shape[1], BF16), (dh.shape[1], BF16)], name=name)


def _rope(x1, x2, cos, sin, name):
    c = x1.shape[1]
    return _ew(lambda a, b, co, si: (a * co - b * si, a * si + b * co), [x1, x2, cos, sin], [], [(c, F32), (c, F32)], name=name)


def _logsig_fwd(f, b, name):
    def fn(f, b):
        z = f + b
        return jnp.minimum(z, 0.0) - jnp.log(1.0 + jnp.exp(-jnp.abs(z)))
    return _ew(fn, [f], [b], [(f.shape[1], F32)], name=name)


def _logsig_bwd(dlogf, f, b, name):
    def fn(d, f, b):
        df = d * jax.nn.sigmoid(-(f + b))
        return df, jnp.sum(df, axis=0, keepdims=True)
    return _ew(fn, [dlogf, f], [b], [(f.shape[1], F32)], [f.shape[1]], name=name)


def _final_fwd_bwd(h, w, target, name):
    d = h.shape[1]

    def fn(h, t, w):
        r = lax.rsqrt(jnp.mean(h * h, axis=-1, keepdims=True) + RMS_EPS)
        xh = h * r
        y = xh * w
        err = y - t
        dy = err * (1.0 / d)
        gw = dy * w
        dx = r * (gw - xh * jnp.mean(gw * xh, axis=-1, keepdims=True))
        return dx, jnp.sum(dy * xh, axis=0, keepdims=True), jnp.sum(err * err, axis=0, keepdims=True) * (0.5 / d)
    return _ew(fn, [h, target], [w], [(d, F32)], [d, d], name=name)


def _adamw(w, g, m, v, name):
    shape = w.shape
    c = shape[-1]
    w2, g2, m2, v2 = (a.reshape(-1, c) for a in (w, g, m, v))

    def fn(w, g, m, v):
        m = ADAM_B1 * m + (1.0 - ADAM_B1) * g
        v = ADAM_B2 * v + (1.0 - ADAM_B2) * jnp.square(g)
        m_hat = m / (1.0 - ADAM_B1 ** ADAM_STEP)
        v_hat = v / (1.0 - ADAM_B2 ** ADAM_STEP)
        delta = -ADAM_LR * (m_hat / (jnp.sqrt(v_hat) + ADAM_EPS) + ADAM_WD * w)
        return delta, m, v
    d, nm, nv = _ew(fn, [w2, g2, m2, v2], [], [(c, F32)] * 3, name=name)
    return d.reshape(shape), nm.reshape(shape), nv.reshape(shape)


def _split3(v):
    hi = v.astype(BF16)
    r1 = v - hi.astype(F32)
    mid = r1.astype(BF16)
    lo = (r1 - mid.astype(F32)).astype(BF16)
    return hi, mid, lo


def _cumsum(x, reverse, name):
    S, C = x.shape
    tm = _divisor(S, 512, 16)
    nt = S // tm

    def body(x_ref, o_ref, carry):
        @pl.when(pl.program_id(0) == 0)
        def _():
            carry[...] = jnp.zeros_like(carry)
        r = lax.broadcasted_iota(jnp.int32, (tm, tm), 0)
        c = lax.broadcasted_iota(jnp.int32, (tm, tm), 1)
        tri = jnp.where((c >= r) if reverse else (c <= r), 1.0, 0.0).astype(BF16)
        xv = x_ref[...]
        acc = jnp.zeros((tm, C), F32)
        for part in _split3(xv):
            acc = acc + jnp.dot(tri, part, preferred_element_type=F32)
        o_ref[...] = acc + carry[...]
        carry[...] += jnp.sum(xv, axis=0, keepdims=True)

    idx = (lambda i: (nt - 1 - i, 0)) if reverse else (lambda i: (i, 0))
    return pl.pallas_call(
        body, name=name, grid=(nt,), in_specs=[pl.BlockSpec((tm, C), idx)], out_specs=pl.BlockSpec((tm, C), idx),
        out_shape=jax.ShapeDtypeStruct((S, C), F32), scratch_shapes=[pltpu.VMEM((1, C), F32)],
    )(x)


NN = (((1,), (0,)), ((), ()))
NT = (((1,), (1,)), ((), ()))
TN = (((0,), (0,)), ((), ()))

HBM_SPEC = pl.BlockSpec(memory_space=pl.ANY)


def _job_in_body(job, refs, n_in, n_out, n_scr, grid):
    if job is None:
        return refs[n_in:], lambda: None
    ji, jo = len(job["ins"]), len(job["outs"])
    j_in = refs[n_in: n_in + ji]
    pos = n_in + ji
    own = list(refs[pos: pos + n_out])
    pos += n_out
    j_out = refs[pos: pos + jo]
    pos += jo
    own += list(refs[pos: pos + n_scr])
    ss, rs = refs[-2], refs[-1]
    first = functools.reduce(jnp.logical_and, [pl.program_id(d) == 0 for d in range(len(grid))])
    last = functools.reduce(jnp.logical_and, [pl.program_id(d) == n - 1 for d, n in enumerate(grid)])

    @pl.when(first)
    def _():
        job["start"](j_in, j_out, ss, rs)

    def finish():
        @pl.when(last)
        def _():
            job["finish"](j_in, j_out, ss, rs)

    return own, finish


def _job_call(job, body, *, name, grid, in_specs, out_specs, out_shape, args, scratch_shapes, aliases, dimension_semantics):
    in_specs, out_specs, out_shape, args, scratch_shapes = list(in_specs), list(out_specs), list(out_shape), list(args), list(scratch_shapes)
    aliases = dict(aliases)
    if job is not None:
        for i, o in job["aliases"].items():
            aliases[len(args) + i] = len(out_shape) + o
        in_specs += [HBM_SPEC] * len(job["ins"])
        args += list(job["ins"])
        out_specs += [HBM_SPEC] * len(job["outs"])
        out_shape += list(job["outs"])
        scratch_shapes += [pltpu.SemaphoreType.DMA((job["n_sems"],)), pltpu.SemaphoreType.DMA((job["n_sems"],))]
    return pl.pallas_call(
        body, name=name, grid=grid, in_specs=in_specs, out_specs=out_specs, out_shape=out_shape,
        scratch_shapes=scratch_shapes, input_output_aliases=aliases,
        compiler_params=pltpu.CompilerParams(dimension_semantics=dimension_semantics),
    )(*args)


def _comm_call(job, name):
    def body(*refs):
        ji, jo = len(job["ins"]), len(job["outs"])
        job["start"](refs[:ji], refs[ji: ji + jo], refs[-2], refs[-1])
        job["finish"](refs[:ji], refs[ji: ji + jo], refs[-2], refs[-1])

    return pl.pallas_call(
        body, name=name, in_specs=[HBM_SPEC] * len(job["ins"]), out_specs=[HBM_SPEC] * len(job["outs"]), out_shape=list(job["outs"]),
        input_output_aliases=dict(job["aliases"]),
        scratch_shapes=[pltpu.SemaphoreType.DMA((job["n_sems"],)), pltpu.SemaphoreType.DMA((job["n_sems"],))],
    )(*job["ins"])


def _mm_call(name, grid, k_axis, a, a_spec, a2d, b, b_spec, b2d, dims, out_sds, out_spec, o2d, *,
             alpha=1.0, res=None, res_spec=None, into=None, job=None, norm_bwd=None):
    nk = grid[k_axis]
    n_in = 2 + (res is not None) + (into is not None) + (3 if norm_bwd is not None else 0)
    n_out = 2 if norm_bwd is not None else 1

    def body(*refs):
        a_ref, b_ref = refs[0], refs[1]
        res_ref = refs[2] if res is not None else None
        own, finish_job = _job_in_body(job, refs, n_in, n_out, 1, grid)
        o_ref, acc_ref = own[0], own[-1]
        k = pl.program_id(k_axis)

        @pl.when(k == 0)
        def _():
            acc_ref[...] = jnp.zeros_like(acc_ref)

        if norm_bwd is not None:
            x_ref, w_ref, dres_ref = refs[n_in - 3: n_in]
            dw_ref = own[1]

            @pl.when(functools.reduce(jnp.logical_and, [pl.program_id(d) == 0 for d in range(len(grid))]))
            def _():
                dw_ref[...] = jnp.zeros_like(dw_ref)

        av = a_ref[...].reshape(a2d).astype(BF16)
        bv = b_ref[...].reshape(b2d).astype(BF16)
        acc_ref[...] += lax.dot_general(av, bv, dims, preferred_element_type=F32)

        @pl.when(k == nk - 1)
        def _():
            r = acc_ref[...]
            if alpha != 1.0:
                r = r * alpha
            if res_ref is not None:
                r = res_ref[...].reshape(o2d) + r
            if norm_bwd is not None:
                x = x_ref[...]
                rs = lax.rsqrt(jnp.mean(x * x, axis=-1, keepdims=True) + RMS_EPS)
                xh = x * rs
                gw = r * w_ref[...]
                dw_ref[...] += jnp.sum(r * xh, axis=0, keepdims=True)
                r = dres_ref[...] + rs * (gw - xh * jnp.mean(gw * xh, axis=-1, keepdims=True))
            o_ref[...] = r.reshape(o_ref.shape).astype(o_ref.dtype)

        finish_job()

    in_specs, args = [a_spec, b_spec], [a, b]
    if res is not None:
        in_specs.append(res_spec)
        args.append(res)
    aliases = {}
    if into is not None:
        aliases = {len(args): 0}
        in_specs.append(pl.BlockSpec(memory_space=pl.ANY))
        args.append(into)
        out_sds = jax.ShapeDtypeStruct(into.shape, into.dtype)
    out_specs, out_shape = [out_spec], [out_sds]
    if norm_bwd is not None:
        vec = pl.BlockSpec((1, o2d[1]), lambda *_: (0, 0))
        in_specs += [out_spec, vec, out_spec]
        args += list(norm_bwd)
        out_specs.append(vec)
        out_shape.append(jax.ShapeDtypeStruct((1, o2d[1]), F32))
    serial = job is not None or norm_bwd is not None
    sem = tuple("arbitrary" if d == k_axis or serial else "parallel" for d in range(len(grid)))
    res_all = _job_call(
        job, body, name=name, grid=grid, in_specs=in_specs, out_specs=out_specs, out_shape=out_shape, args=args,
        scratch_shapes=[pltpu.VMEM(o2d, F32)], aliases=aliases, dimension_semantics=sem)
    own = res_all[0] if n_out == 1 else tuple(res_all[:n_out])
    return own if job is None else (own, res_all[n_out:])


def _mm(a, b, *, ta=False, tb=False, out=F32, res=None, alpha=1.0, norm_bwd=None, name):
    K, M = a.shape if ta else a.shape[::-1]
    N = b.shape[0] if tb else b.shape[1]
    assert (b.shape[1] if tb else b.shape[0]) == K, (a.shape, b.shape, ta, tb)
    tk = _divisor(K, 1024, LANES)
    tn = _divisor(N, 1408, LANES)
    assert norm_bwd is None or tn == N
    for cap in (1024, 512, 256, 128):
        tm = _divisor(M, cap, LANES if ta else 16)
        est = 2 * (tm * tk * a.dtype.itemsize + tk * tn * b.dtype.itemsize + tm * tn * jnp.dtype(out).itemsize)
        est += tm * tn * 4 + (2 * tm * tn * 4 if res is not None else 0) + (4 * tm * tn * 4 if norm_bwd is not None else 0)
        if est <= MM_VMEM_BYTES:
            break
    a_spec = pl.BlockSpec((tk, tm), lambda i, j, k: (k, i)) if ta else pl.BlockSpec((tm, tk), lambda i, j, k: (i, k))
    b_spec = pl.BlockSpec((tn, tk), lambda i, j, k: (j, k)) if tb else pl.BlockSpec((tk, tn), lambda i, j, k: (k, j))
    o_spec = pl.BlockSpec((tm, tn), lambda i, j, k: (i, j))
    dims = (((0 if ta else 1,), (1 if tb else 0,)), ((), ()))
    return _mm_call(name, (M // tm, N // tn, K // tk), 2, a, a_spec, (tk, tm) if ta else (tm, tk), b, b_spec,
                    (tn, tk) if tb else (tk, tn), dims, jax.ShapeDtypeStruct((M, N), out), o_spec, (tm, tn),
                    alpha=alpha, res=res, res_spec=o_spec, norm_bwd=norm_bwd)


def _w128_spec(blk):
    return pl.BlockSpec((N_DEV, 128, D_MODEL), lambda *_: (0, blk, 0))


def _mm_w128(a, G1, blk, *, tb=False, res=None, out=F32, norm_bwd=None, name):
    S = a.shape[0]
    tm = _divisor(S, 1024 if norm_bwd is None else 512, 16)
    row = pl.BlockSpec((tm, D_MODEL), lambda i, k: (i, 0))
    return _mm_call(name, (S // tm, 1), 1, a, row, (tm, D_MODEL), G1, _w128_spec(blk), (D_MODEL, D_MODEL), NT if tb else NN,
                    jax.ShapeDtypeStruct((S, D_MODEL), out), row, (tm, D_MODEL), res=res, res_spec=row, norm_bwd=norm_bwd)


def _mm_w128_dw(a, b, blk, into, name):
    S = a.shape[0]
    tk = _divisor(S, 1024, 16)
    row = pl.BlockSpec((tk, D_MODEL), lambda i, k: (k, 0))
    return _mm_call(name, (1, S // tk), 1, a, row, (tk, D_MODEL), b, row, (tk, D_MODEL), TN, None, _w128_spec(blk),
                    (D_MODEL, D_MODEL), into=into)


def _ffn_gate_up(h, norm_w, G2v, rb, name, job=None):
    S = h.shape[0]
    tm = _divisor(S, 1024, 16)
    grid = (S // tm, 4)

    def body(*refs):
        h_ref, nw_ref, w_ref = refs[:3]
        (n_ref, gu_ref, act_ref, n_scr), finish_job = _job_in_body(job, refs, 3, 3, 1, grid)

        @pl.when(pl.program_id(1) == 0)
        def _():
            x = h_ref[...]
            y = x * lax.rsqrt(jnp.mean(x * x, axis=-1, keepdims=True) + RMS_EPS)
            n_scr[...] = (y * nw_ref[...]).astype(BF16)
            n_ref[...] = n_scr[...]

        nv = n_scr[...]
        g = jnp.dot(nv, w_ref[0, 0], preferred_element_type=F32)
        u = jnp.dot(nv, w_ref[1, 0], preferred_element_type=F32)
        sg = jax.nn.sigmoid(g)
        silu = g * sg
        gu_ref[0, 0] = (u * (sg * (1.0 + g * (1.0 - sg)))).astype(BF16)
        gu_ref[1, 0] = silu.astype(BF16)
        act_ref[0] = (silu * u).astype(BF16)
        finish_job()

    row = pl.BlockSpec((tm, D_MODEL), lambda i, j: (i, 0))
    return _job_call(
        job, body, name=name, grid=grid,
        in_specs=[row, pl.BlockSpec((1, D_MODEL), lambda i, j: (0, 0)), pl.BlockSpec((2, 1, D_MODEL, FF_BLK), lambda i, j: (0, j, rb, 0))],
        out_specs=[row, pl.BlockSpec((2, 1, tm, FF_BLK), lambda i, j: (0, j, i, 0)), pl.BlockSpec((1, tm, FF_BLK), lambda i, j: (j, i, 0))],
        out_shape=[jax.ShapeDtypeStruct((S, D_MODEL), BF16), jax.ShapeDtypeStruct((2, 4, S, FF_BLK), BF16), jax.ShapeDtypeStruct((4, S, FF_BLK), BF16)],
        args=[h, norm_w, G2v], scratch_shapes=[pltpu.VMEM((tm, D_MODEL), BF16)], aliases={},
        dimension_semantics=("arbitrary" if job is not None else "parallel", "arbitrary"))


def _ffn_down(act, G1, ob, h, name, job=None):
    S = h.shape[0]
    tm = _divisor(S, 1024, 16)
    row = pl.BlockSpec((tm, D_MODEL), lambda i, k: (i, 0))
    return _mm_call(name, (S // tm, 4), 1, act, pl.BlockSpec((1, tm, FF_BLK), lambda i, k: (k, i, 0)), (tm, FF_BLK),
                    G1, pl.BlockSpec((2, DOWN_ROWS, D_MODEL), lambda i, k: (k, ob, 0)), (FF_BLK, D_MODEL), NN,
                    jax.ShapeDtypeStruct((S, D_MODEL), F32), row, (tm, D_MODEL), alpha=0.5, res=h, res_spec=row, job=job)


def _ffn_down_dx(dh, G1, ob, gu, name):
    S = dh.shape[0]
    tm = _divisor(S, 1024, 16)

    def body(dh_ref, w_ref, gu_ref, o_ref):
        w = w_ref[...].reshape(FF_BLK, D_MODEL)
        dact = lax.dot_general(dh_ref[...].astype(BF16), w, NT, preferred_element_type=F32) * 0.5
        o_ref[0, 0] = (dact * gu_ref[0, 0].astype(F32)).astype(BF16)
        o_ref[1, 0] = (dact * gu_ref[1, 0].astype(F32)).astype(BF16)

    blk = pl.BlockSpec((2, 1, tm, FF_BLK), lambda i, j: (0, j, i, 0))
    return pl.pallas_call(
        body, name=name, grid=(S // tm, 4),
        in_specs=[pl.BlockSpec((tm, D_MODEL), lambda i, j: (i, 0)), pl.BlockSpec((2, DOWN_ROWS, D_MODEL), lambda i, j: (j, ob, 0)), blk],
        out_specs=blk, out_shape=jax.ShapeDtypeStruct((2, 4, S, FF_BLK), BF16),
    )(dh, G1, gu)


def _ffn_down_dw(act, dh, name, job=None):
    S = dh.shape[0]
    tk = _divisor(S, 1024, 16)
    return _mm_call(name, (4, S // tk), 1, act, pl.BlockSpec((1, tk, FF_BLK), lambda j, k: (j, k, 0)), (tk, FF_BLK),
                    dh, pl.BlockSpec((tk, D_MODEL), lambda j, k: (k, 0)), (tk, D_MODEL), TN,
                    jax.ShapeDtypeStruct((N_DEV, DOWN_ROWS, D_MODEL), BF16),
                    pl.BlockSpec((2, DOWN_ROWS, D_MODEL), lambda j, k: (j, 0, 0)), (FF_BLK, D_MODEL), alpha=0.5, job=job)


def _ffn_gate_up_dw(n, dgu8, name, job=None):
    S = n.shape[0]
    tk = _divisor(S, 1024, 16)
    return _mm_call(name, (N_DEV, S // tk), 1, n, pl.BlockSpec((tk, D_MODEL), lambda b, k: (k, 0)), (tk, D_MODEL),
                    dgu8, pl.BlockSpec((1, tk, FF_BLK), lambda b, k: (b, k, 0)), (tk, FF_BLK), TN,
                    jax.ShapeDtypeStruct((N_DEV, D_MODEL, FF_BLK), BF16),
                    pl.BlockSpec((1, D_MODEL, FF_BLK), lambda b, k: (b, 0, 0)), (D_MODEL, FF_BLK), job=job)


def _ffn_gate_up_dx(dgu8, G2, rb, h, norm_w, dres, name, job=None):
    S = h.shape[0]
    tm = _divisor(S, 1024, 16)
    row = pl.BlockSpec((tm, D_MODEL), lambda i, k: (i, 0))
    return _mm_call(name, (S // tm, N_DEV), 1, dgu8, pl.BlockSpec((1, tm, FF_BLK), lambda i, k: (k, i, 0)), (tm, FF_BLK),
                    G2, pl.BlockSpec((1, D_MODEL, FF_BLK), lambda i, k: (k, rb, 0)), (D_MODEL, FF_BLK), NT,
                    jax.ShapeDtypeStruct((S, D_MODEL), F32), row, (tm, D_MODEL), norm_bwd=(h, norm_w, dres), job=job)


def _unheads(x):
    h, S, d = x.shape
    return jnp.transpose(x, (1, 0, 2)).reshape(S, h * d)


def _exact3(v):
    rnd = lambda a: lax.reduce_precision(a, exponent_bits=8, mantissa_bits=7)
    hi = rnd(v)
    mid = rnd(v - hi)
    return hi, mid, rnd(v - hi - mid)


def _causal_mask(st, q0, k0, window):
    dist = (q0 + lax.broadcasted_iota(jnp.int32, st.shape, 1)) - (k0 + lax.broadcasted_iota(jnp.int32, st.shape, 0))
    mask = dist >= 0
    if window is not None:
        mask = mask & (dist < window)
    return jnp.where(mask, st, NEG)


def _attn_fwd(qT, k, vT1, *, tile, hb, window=None, sink=None, name, job=None):
    H, dqk, S = qT.shape
    G = H // k.shape[0]
    dvp = vT1.shape[1]
    dv = dvp - 16
    tq = tk = tile
    assert H % hb == 0 and (G == 1 or G % hb == 0)
    kvb = hb if G == 1 else 1
    grid = (H // hb, S // tq)
    n_in = 3 + (sink is not None)

    def body(*refs):
        q_ref, k_ref, v_ref = refs[:3]
        (o_ref, lse_ref), finish_job = _job_in_body(job, refs, n_in, 2, 0, grid)
        i = pl.program_id(1)
        carry = []
        for a in range(hb):
            if sink is not None:
                carry.append(jnp.zeros((1, tq), F32) + refs[3][a, :, 0:1])
                carry.append(jnp.where(lax.broadcasted_iota(jnp.int32, (dvp, tq), 0) == dv, 1.0, 0.0))
            else:
                carry.append(jnp.full((1, tq), NEG, F32))
                carry.append(jnp.zeros((dvp, tq), F32))

        def step(j, carry, masked, off=None, keys=tk, q_from=0):
            off = pl.multiple_of(j * tk, tk) if off is None else off
            out = []
            for a in range(hb):
                m, acc = carry[2 * a], carry[2 * a + 1]
                kv = a if kvb > 1 else 0
                st = jnp.dot(k_ref[kv, pl.ds(off, keys), :], q_ref[a][:, q_from:], preferred_element_type=F32)
                if masked:
                    st = _causal_mask(st, i * tq + q_from, off, window)
                m_old, acc_old = m[:, q_from:], acc[:, q_from:]
                m_new = jnp.maximum(m_old, jnp.max(st, axis=0, keepdims=True))
                pt = jnp.exp(st - m_new).astype(BF16)
                acc_new = jnp.exp(m_old - m_new) * acc_old + jnp.dot(v_ref[kv, :, pl.ds(off, keys)], pt, preferred_element_type=F32)
                if q_from:
                    m_new = jnp.concatenate([m[:, :q_from], m_new], axis=1)
                    acc_new = jnp.concatenate([acc[:, :q_from], acc_new], axis=1)
                out += [m_new, acc_new]
            return tuple(out)

        carry = tuple(carry)
        if window is None:
            carry = lax.fori_loop(0, i, functools.partial(step, masked=False), carry)
            if tq % (2 * LANES) == 0:
                half = tq // 2
                carry = step(None, carry, True, off=pl.multiple_of(i * tq, tq), keys=half)
                carry = step(None, carry, True, off=pl.multiple_of(i * tq + half, half), keys=half, q_from=half)
            else:
                carry = step(i, carry, True)
        else:
            assert window % LANES == 0 and tq + window <= S
            carry = step(None, carry, True, off=pl.multiple_of(jnp.maximum(i * tq - window, 0), LANES), keys=tq + window)
        for a in range(hb):
            m, acc = carry[2 * a], carry[2 * a + 1]
            l = acc[dv:dv + 1, :]
            o_ref[a] = acc[:dv, :] / l
            lse_ref[a] = m + jnp.log(l)
        finish_job()

    kv_idx = (lambda b: b) if G == 1 else (lambda b: (b * hb) // G)
    in_specs = [
        pl.BlockSpec((hb, dqk, tq), lambda b, i: (b, 0, i)),
        pl.BlockSpec((kvb, S, dqk), lambda b, i: (kv_idx(b), 0, 0)),
        pl.BlockSpec((kvb, dvp, S), lambda b, i: (kv_idx(b), 0, 0)),
    ]
    args = [qT, k, vT1]
    if sink is not None:
        in_specs += [pl.BlockSpec((hb, 1, LANES), lambda b, i: (b, 0, 0))]
        args += [sink]
    return _job_call(
        job, body, name=name, grid=grid, in_specs=in_specs,
        out_specs=[pl.BlockSpec((hb, dv, tq), lambda b, i: (b, 0, i)), pl.BlockSpec((hb, 1, tq), lambda b, i: (b, 0, i))],
        out_shape=[jax.ShapeDtypeStruct((H, dv, S), F32), jax.ShapeDtypeStruct((H, 1, S), F32)],
        args=args, scratch_shapes=[], aliases={}, dimension_semantics=("arbitrary", "arbitrary") if job is not None else ("parallel", "parallel"))


def _attn_bwd(qT, k, kT, v, oT, doT, lse, *, tile, hb, window=None, sink=None, real=None, extra=False, full=False, name):
    H, dqk, S = qT.shape
    G = H // k.shape[0]
    dv = v.shape[2]
    tq = tk = tile
    nq = S // tq
    has_p = sink is not None
    real = dqk if real is None else real
    main = dqk if full else real
    assert H % hb == 0 and (G == 1 or G % hb == 0) and not (extra and real == dqk)
    kvb = hb if G == 1 else 1

    def body(*refs):
        qT_ref, k_ref, kT_ref, v_ref, oT_ref, doT_ref, lse_ref = refs[:7]
        p_ref = refs[7] if has_p else None
        pos = 8 if has_p else 7
        dq_ref, dk_ref, dv_ref = refs[pos: pos + 3]
        pos += 3
        ds_ref = refs[pos] if has_p else None
        pos += has_p
        dqx_ref, dkx_ref = (refs[pos], refs[pos + 1]) if extra else (None, None)
        delta = refs[-1]
        j = pl.program_id(1)

        @pl.when(j == 0)
        def _():
            dq_ref[...] = jnp.zeros_like(dq_ref)
            if extra:
                dqx_ref[...] = jnp.zeros_like(dqx_ref)
            for a in range(hb):
                drow = jnp.sum(doT_ref[a].astype(F32) * oT_ref[a], axis=0, keepdims=True)
                delta[a] = drow
                if has_p:
                    w = jnp.exp(p_ref[a, :, 0:1] - lse_ref[a])
                    ds_ref[a] = jnp.zeros((1, LANES), F32) - jnp.sum(w * drow, axis=1, keepdims=True)

        def step(i, carry, masked, off=None, qs=tq, keys=tk):
            off = pl.multiple_of(i * tq, tq) if off is None else off
            out = []
            for a in range(hb):
                dk, dvv = carry[2 * a], carry[2 * a + 1]
                kv = a if kvb > 1 else 0
                qTi = qT_ref[a, :, pl.ds(off, qs)]
                doTi = doT_ref[a, :, pl.ds(off, qs)]
                st = jnp.dot(k_ref[kv, pl.ds(0, keys), :], qTi, preferred_element_type=F32)
                if masked:
                    st = _causal_mask(st, off, j * tk, window)
                pt = jnp.exp(st - lse_ref[a, :, pl.ds(off, qs)])
                dv_new = lax.dot_general(pt.astype(BF16), doTi, NT, preferred_element_type=F32)
                dpt = jnp.dot(v_ref[kv, pl.ds(0, keys), :], doTi, preferred_element_type=F32)
                dsb = (pt * (dpt - delta[a, :, pl.ds(off, qs)])).astype(BF16)
                dk_new = lax.dot_general(dsb, qTi, NT, preferred_element_type=F32)
                if keys < tk:
                    dk = jnp.concatenate([dk[:keys] + dk_new, dk[keys:]], axis=0)
                    dvv = jnp.concatenate([dvv[:keys] + dv_new, dvv[keys:]], axis=0)
                else:
                    dk, dvv = dk + dk_new, dvv + dv_new
                dqt = jnp.dot(kT_ref[kv, :, pl.ds(0, keys)], dsb, preferred_element_type=F32)
                dq_ref[a, :, pl.ds(off, qs)] += dqt[:main]
                if extra:
                    dqx_ref[a, :, pl.ds(off, qs)] += dqt[real:]
                out += [dk, dvv]
            return tuple(out)

        carry = (jnp.zeros((tk, dqk), F32), jnp.zeros((tk, dv), F32)) * hb
        if window is None:
            if tk % (2 * LANES) == 0:
                half = tk // 2
                carry = step(None, carry, True, off=pl.multiple_of(j * tk + half, half), qs=half)
                carry = step(None, carry, True, off=pl.multiple_of(j * tk, tk), qs=half, keys=half)
            else:
                carry = step(j, carry, True)
            carry = lax.fori_loop(j + 1, nq, functools.partial(step, masked=False), carry)
        else:
            assert window % LANES == 0 and tk + window <= S
            carry = step(None, carry, True, off=pl.multiple_of(jnp.minimum(j * tk, S - (tk + window)), LANES), qs=tk + window)
        for a in range(hb):
            dk_ref[a] = carry[2 * a][:, :main]
            if extra:
                dkx_ref[a] = carry[2 * a][:, real:]
            dv_ref[a] = carry[2 * a + 1]

    kv_idx = (lambda b: b) if G == 1 else (lambda b: (b * hb) // G)
    colsT = lambda d: pl.BlockSpec((hb, d, S), lambda b, j: (b, 0, 0))
    in_specs = [
        colsT(dqk),
        pl.BlockSpec((kvb, tk, dqk), lambda b, j: (kv_idx(b), j, 0)),
        pl.BlockSpec((kvb, dqk, tk), lambda b, j: (kv_idx(b), 0, j)),
        pl.BlockSpec((kvb, tk, dv), lambda b, j: (kv_idx(b), j, 0)),
        colsT(dv), colsT(dv),
        pl.BlockSpec((hb, 1, S), lambda b, j: (b, 0, 0)),
    ]
    args = [qT, k, kT, v, oT, doT, lse]
    if has_p:
        in_specs += [pl.BlockSpec((hb, 1, LANES), lambda b, j: (b, 0, 0))]
        args += [sink]
    out_specs = [colsT(main), pl.BlockSpec((hb, tk, main), lambda b, j: (b, j, 0)), pl.BlockSpec((hb, tk, dv), lambda b, j: (b, j, 0))]
    out_shape = [jax.ShapeDtypeStruct((H, main, S), F32), jax.ShapeDtypeStruct((H, S, main), F32), jax.ShapeDtypeStruct((H, S, dv), F32)]
    if has_p:
        out_specs += [pl.BlockSpec((hb, 1, LANES), lambda b, j: (b, 0, 0))]
        out_shape += [jax.ShapeDtypeStruct((H, 1, LANES), F32)]
    if extra:
        out_specs += [colsT(dqk - real), pl.BlockSpec((hb, tk, dqk - real), lambda b, j: (b, j, 0))]
        out_shape += [jax.ShapeDtypeStruct((H, dqk - real, S), F32), jax.ShapeDtypeStruct((H, S, dqk - real), F32)]
    return pl.pallas_call(
        body, name=name, grid=(H // hb, S // tk), in_specs=in_specs, out_specs=out_specs, out_shape=out_shape,
        scratch_shapes=[pltpu.VMEM((hb, 1, S), F32)],
        compiler_params=pltpu.CompilerParams(dimension_semantics=("parallel", "arbitrary")),
    )(*args)


def _rows_and_cols(x3):
    xb = x3.astype(BF16)
    return jnp.transpose(xb, (1, 0, 2)), jnp.transpose(xb, (1, 2, 0))


def _cols_only(x3):
    return jnp.transpose(x3.astype(BF16), (1, 2, 0))


def _v_with_ones(v3):
    S, h, _ = v3.shape
    vT = jnp.transpose(v3.astype(BF16), (1, 2, 0))
    return jnp.concatenate([vT, jnp.ones((h, 1, S), BF16), jnp.zeros((h, 15, S), BF16)], axis=1)


def _from_T(oT):
    h, d, S = oT.shape
    return jnp.transpose(oT, (2, 0, 1)).reshape(S, h * d)


def _coords():
    return lax.axis_index("x"), lax.axis_index("y"), lax.axis_index("c")


def _peer(axis):
    x, y, c = _coords()
    return {"x": (1 - x, y, c), "y": (x, 1 - y, c), "c": (x, y, 1 - c)}[axis]


def _gather_job(bufs, rows=None):
    n = len(bufs)

    def copies(outs, send_sems, recv_sems):
        x, y, c = _coords()
        me, sibling = (x, y, c), (x, y, 1 - c)
        chips = [(1 - x, y), (x, 1 - y), (1 - x, 1 - y)]

        def copy(t, k, block, to):
            px, py, pc = block
            ref = outs[t].at[4 * px + 2 * py + pc]
            if rows is not None and rows[t] is not None:
                ref = ref.at[pl.ds(rows[t][0], rows[t][1])]
            return pltpu.make_async_remote_copy(ref, ref, send_sems.at[7 * t + k], recv_sems.at[7 * t + k], device_id=to, device_id_type=MESH)

        return copy, me, sibling, chips, c

    def start(ins, outs, send_sems, recv_sems):
        copy, me, sibling, chips, c = copies(outs, send_sems, recv_sems)
        for t in range(n):
            copy(t, 0, me, sibling).start()
            for j, chip in enumerate(chips):
                copy(t, 1 + j, me, (*chip, c)).start()

    def finish(ins, outs, send_sems, recv_sems):
        copy, me, sibling, chips, c = copies(outs, send_sems, recv_sems)
        for j, chip in enumerate(chips):
            for t in range(n):
                copy(t, 1 + j, (*chip, c), me).wait_recv()
                copy(t, 4 + j, (*chip, c), sibling).start()
        for t in range(n):
            copy(t, 0, sibling, me).wait_recv()
            for j, chip in enumerate(chips):
                copy(t, 4 + j, (*chip, 1 - c), me).wait_recv()
        for t in range(n):
            copy(t, 0, me, sibling).wait_send()
            for j, chip in enumerate(chips):
                copy(t, 1 + j, me, (*chip, c)).wait_send()
                copy(t, 4 + j, (*chip, c), sibling).wait_send()

    return dict(ins=list(bufs), outs=[jax.ShapeDtypeStruct(b.shape, b.dtype) for b in bufs], aliases={t: t for t in range(n)},
                n_sems=7 * n, start=start, finish=finish)


def _in_slot(local):
    x, y, c = _coords()
    buf = lax.empty((N_DEV,) + local.shape, local.dtype)
    return lax.dynamic_update_slice(buf, local[None], (4 * x + 2 * y + c, 0, 0))


def _pair_job(vs, axes):
    n = len(vs)
    axes = [axes] * n if isinstance(axes, str) else axes

    def copies(ins, outs, send_sems, recv_sems):
        out = []
        for t in range(n):
            me = lax.axis_index(axes[t])
            src = ins[t].at[1 - me] if len(ins[t].shape) == 3 else ins[t].at[:, 1 - me]
            out.append(pltpu.make_async_remote_copy(src, outs[t], send_sems.at[t], recv_sems.at[t], device_id=_peer(axes[t]), device_id_type=MESH))
        return out

    def start(*refs):
        for cp in copies(*refs):
            cp.start()

    def finish(*refs):
        for cp in copies(*refs):
            cp.wait()

    return dict(ins=list(vs), outs=[jax.ShapeDtypeStruct(v.shape[:-3] + v.shape[-2:], v.dtype) for v in vs], aliases={}, n_sems=n,
                start=start, finish=finish)


def _add_kept(v, got, axis, out, name):
    R, C = v.shape[-2:]
    lead = v.shape[0] if v.ndim == 4 else 1
    tm = _divisor(R, max(16, EW_TILE_BYTES // (_lanes(C) * (v.dtype.itemsize + got.dtype.itemsize + jnp.dtype(out).itemsize)) // 16 * 16), 16)
    me = lax.axis_index(axis).astype(jnp.int32).reshape(1)
    v4 = v.reshape(lead, 2, R, C)
    g3 = got.reshape(lead, R, C)

    def body(me_ref, v_ref, g_ref, o_ref):
        o_ref[...] = (v_ref[0].astype(F32) + g_ref[...].astype(F32)).astype(o_ref.dtype)

    res = pl.pallas_call(
        body, name=name, out_shape=jax.ShapeDtypeStruct((lead, R, C), out),
        grid_spec=pltpu.PrefetchScalarGridSpec(
            num_scalar_prefetch=1, grid=(lead, R // tm),
            in_specs=[pl.BlockSpec((1, 1, tm, C), lambda b, i, me: (b, me[0], i, 0)), pl.BlockSpec((1, tm, C), lambda b, i, me: (b, i, 0))],
            out_specs=pl.BlockSpec((1, tm, C), lambda b, i, me: (b, i, 0))),
    )(me, v4, g3)
    return res


def _cross_job(vs):
    n = len(vs)

    def copies(ins, outs, send_sems, recv_sems):
        x, y, _ = _coords()
        out = []
        for t in range(n):
            h = ins[t].shape[2] // 2
            out.append(pltpu.make_async_remote_copy(ins[t].at[1 - x, :, pl.ds(0, h)], outs[2 * t], send_sems.at[2 * t], recv_sems.at[2 * t],
                                                    device_id=_peer("x"), device_id_type=MESH))
            out.append(pltpu.make_async_remote_copy(ins[t].at[:, 1 - y, pl.ds(h, h)], outs[2 * t + 1], send_sems.at[2 * t + 1], recv_sems.at[2 * t + 1],
                                                    device_id=_peer("y"), device_id_type=MESH))
        return out

    def start(*refs):
        for cp in copies(*refs):
            cp.start()

    def finish(*refs):
        for cp in copies(*refs):
            cp.wait()

    outs = []
    for v in vs:
        outs += [jax.ShapeDtypeStruct((2, v.shape[2] // 2, v.shape[3]), v.dtype)] * 2
    return dict(ins=list(vs), outs=outs, aliases={}, n_sems=2 * n, start=start, finish=finish)


def _add_picked(v, got, axis, out, name):
    _, _, R, C = v.shape
    h = R // 2
    tm = _divisor(h, max(16, EW_TILE_BYTES // (_lanes(C) * (v.dtype.itemsize + got.dtype.itemsize + jnp.dtype(out).itemsize)) // 16 * 16), 16)
    me = lax.axis_index(axis).astype(jnp.int32).reshape(1)
    if axis == "x":
        v_map = lambda b, i, me: (me[0], b, i, 0)
    else:
        v_map = lambda b, i, me: (b, me[0], i + h // tm, 0)

    def body(me_ref, v_ref, g_ref, o_ref):
        o_ref[...] = (v_ref[0].astype(F32) + g_ref[...].astype(F32)).astype(o_ref.dtype)

    return pl.pallas_call(
        body, name=name, out_shape=jax.ShapeDtypeStruct((2, h, C), out),
        grid_spec=pltpu.PrefetchScalarGridSpec(
            num_scalar_prefetch=1, grid=(2, h // tm),
            in_specs=[pl.BlockSpec((1, 1, tm, C), v_map), pl.BlockSpec((1, tm, C), lambda b, i, me: (b, i, 0))],
            out_specs=pl.BlockSpec((1, tm, C), lambda b, i, me: (b, i, 0))),
    )(me, v, got)


def _reduce_scatter_steps(gs, tag):
    n = len(gs)
    vs = [g.reshape(4, 2, *g.shape[1:]) for g in gs]
    got = yield _pair_job(vs, "c")
    vs = [_add_kept(v, r, "c", BF16, f"rs_{tag}_add_c{t}") for t, (v, r) in enumerate(zip(vs, got))]
    vs = [v.reshape(2, 2, v.shape[1], v.shape[2]) for v in vs]
    got = yield _cross_job(vs)
    up = [_add_picked(v, r, "x", BF16, f"rs_{tag}_add_x{t}") for t, (v, r) in enumerate(zip(vs, got[0::2]))]
    lo = [_add_picked(v, r, "y", BF16, f"rs_{tag}_add_y{t}") for t, (v, r) in enumerate(zip(vs, got[1::2]))]
    got = yield _pair_job(up + lo, ["y"] * n + ["x"] * n)
    out = []
    for t in range(n):
        a = _add_kept(up[t], got[t], "y", F32, f"rs_{tag}_add_y2{t}")[0]
        b = _add_kept(lo[t], got[n + t], "x", F32, f"rs_{tag}_add_x2{t}")[0]
        out.append(jnp.concatenate([a, b], axis=0))
    return out


def _reduce_scatter(gs, tag):
    steps = _reduce_scatter_steps(gs, tag)
    job = next(steps)
    for stage in ("c", "xy", "yx"):
        got = _comm_call(job, f"rs_{tag}_{stage}")
        try:
            job = steps.send(got)
        except StopIteration as done:
            return done.value


def _all_reduce_small(v):
    def body(v_ref, o_ref, buf, send_sems, recv_sems):
        x, y, c = _coords()
        me = 4 * x + 2 * y + c
        buf[me] = v_ref[...]
        copies = []
        for k in range(1, N_DEV):
            peer = tuple((1 - a) if (k >> s) & 1 else a for a, s in ((x, 2), (y, 1), (c, 0)))
            cp = pltpu.make_async_remote_copy(v_ref, buf.at[me], send_sems.at[k - 1], recv_sems.at[k - 1], device_id=peer, device_id_type=MESH)
            cp.start()
            copies.append(cp)
        for cp in copies:
            cp.wait()
        acc = buf[0]
        for d in range(1, N_DEV):
            acc = acc + buf[d]
        o_ref[...] = acc

    vm = pl.BlockSpec(memory_space=pltpu.VMEM)
    return pl.pallas_call(
        body, name="all_reduce_small", in_specs=[vm], out_specs=vm, out_shape=jax.ShapeDtypeStruct(v.shape, F32),
        scratch_shapes=[pltpu.VMEM((N_DEV,) + v.shape, F32), pltpu.SemaphoreType.DMA((N_DEV - 1,)), pltpu.SemaphoreType.DMA((N_DEV - 1,))],
    )(v)


def _local_groups(w, dtype):
    mix_out = [w["ev_w_out"][0], w["od_w_out"][0]]
    layers = []
    for l in range(DEPTH):
        a = jnp.concatenate([w["ffa_w_down"][l], w["ffb_w_down"][l]], axis=0).astype(dtype)
        b = jnp.concatenate([w["ple_w_gate"][l], mix_out[l]], axis=0).astype(dtype)
        c = jnp.concatenate([w["ffa_w_gate_up"][l], w["ffb_w_gate_up"][l]], axis=0).astype(dtype)
        layers.append((a, b, c))
    strip = jnp.concatenate([w["ple_w_proj"].reshape(-1, STRIP_C), w["ev_w_ukv"][0], jnp.pad(w["ev_w_uq"][0], ((0, 0), (0, STRIP_C - 96))),
                             jnp.zeros((G3_ROWS - 896, STRIP_C), F32)], axis=0)
    m = jnp.concatenate([w["od_w_in"][0], w["ev_w_in"][0], strip, jnp.zeros((G3_ROWS, G3_COLS - STRIP0 - STRIP_C), F32)], axis=1).astype(dtype)
    return layers, m


def _ungroup_local(a, b, c, r3):
    out = {
        "ffa_w_down": jnp.stack([x[0] for x in a]), "ffb_w_down": jnp.stack([x[1] for x in a]),
        "ple_w_gate": jnp.stack([x[:128] for x in b]), "ev_w_out": b[0][128:][None], "od_w_out": b[1][128:][None],
        "ffa_w_gate_up": jnp.stack([x[0] for x in c]), "ffb_w_gate_up": jnp.stack([x[1] for x in c]),
        "od_w_in": r3[:, :OD_C][None], "ev_w_in": r3[:, OD_C:STRIP0][None],
    }
    strip = r3[:, STRIP0:STRIP0 + STRIP_C]
    out["ple_w_proj"] = strip[:512].reshape(2, PLE_DIM, STRIP_C)
    out["ev_w_ukv"] = strip[512:640][None]
    out["ev_w_uq"] = strip[640:896, :96][None]
    return out


def _cols(a):
    return jnp.transpose(a, (1, 0, 2)).reshape(a.shape[1], -1)


def _blocks(g, c):
    return jnp.transpose(g.reshape(g.shape[0], N_DEV, c), (1, 0, 2))


def _uq_permute(w):
    r = w.shape[0]
    w3 = w.reshape(r, B_HEADS, B_NOPE + B_ROPE)
    half = B_ROPE // 2
    return jnp.concatenate([w3[:, :, :B_NOPE].reshape(r, -1), w3[:, :, B_NOPE:B_NOPE + half].reshape(r, -1), w3[:, :, B_NOPE + half:].reshape(r, -1)], axis=1)


def _uq_unpermute(g):
    r = g.shape[0]
    half = B_ROPE // 2
    n = B_HEADS * B_NOPE
    parts = [g[:, :n].reshape(r, B_HEADS, B_NOPE), g[:, n:n + B_HEADS * half].reshape(r, B_HEADS, half), g[:, n + B_HEADS * half:].reshape(r, B_HEADS, half)]
    return jnp.concatenate(parts, axis=2).reshape(r, -1)


def _ukv_permute(w):
    r = w.shape[0]
    return jnp.transpose(w.reshape(r, B_HEADS, 2, B_NOPE), (0, 2, 1, 3)).reshape(r, -1)


def _ukv_unpermute(g):
    r = g.shape[0]
    return jnp.transpose(g.reshape(r, 2, B_HEADS, B_NOPE), (0, 2, 1, 3)).reshape(r, -1)


def _od_in_widen(w):
    n = C_HEADS * C_HEAD_DIM
    wide = lambda m: jnp.pad(m.reshape(-1, C_HEADS, C_HEAD_DIM), ((0, 0), (0, 0), (0, QK_PAD - C_HEAD_DIM))).reshape(m.shape[0], -1)
    return jnp.concatenate([wide(w[:, :n] * C_HEAD_DIM ** -0.5), wide(w[:, n:2 * n]), w[:, 2 * n:],
                            jnp.zeros((w.shape[0], ODD_IN_PAD - ODD_IN_AUG), w.dtype)], axis=1)


def _od_in_narrow(g):
    wp = C_HEADS * QK_PAD
    narrow = lambda m: m.reshape(-1, C_HEADS, QK_PAD)[:, :, :C_HEAD_DIM].reshape(m.shape[0], -1)
    return jnp.concatenate([narrow(g[:, :wp]) * C_HEAD_DIM ** -0.5, narrow(g[:, wp:2 * wp]), g[:, 2 * wp:ODD_IN_AUG]], axis=1)


def _misc_weights(G3):
    strip = G3[:, :, STRIP0:STRIP0 + STRIP_C]
    return {
        "od_w_in": _od_in_widen(_cols(G3[:, :, :OD_C])),
        "ev_w_in": jnp.pad(_cols(G3[:, :, OD_C:STRIP0]), ((0, 0), (0, EVEN_IN_PAD - EVEN_IN))),
        "ple_w_proj": [_cols(strip[:, i * PLE_DIM:(i + 1) * PLE_DIM]) for i in range(DEPTH)],
        "ev_w_ukv": _ukv_permute(_cols(strip[:, 512:640])),
        "ev_w_uq": _uq_permute(_cols(strip[:, 640:896, :96])),
    }


def _misc_grads(G):
    strip = jnp.concatenate([
        _blocks(G["ple_w_proj"][0], STRIP_C), _blocks(G["ple_w_proj"][1], STRIP_C), _blocks(_ukv_unpermute(G["ev_w_ukv"]), STRIP_C),
        jnp.pad(_blocks(_uq_unpermute(G["ev_w_uq"]), 96), ((0, 0), (0, 0), (0, STRIP_C - 96))),
        jnp.zeros((N_DEV, G3_ROWS - 896, STRIP_C), F32)], axis=1)
    return jnp.concatenate([_blocks(_od_in_narrow(G["od_w_in"]), OD_C), _blocks(G["ev_w_in"][:, :EVEN_IN], EV_C), strip,
                            jnp.zeros((N_DEV, G3_ROWS, G3_COLS - STRIP0 - STRIP_C), F32)], axis=2)


def _ffn_fwd(h, norm_w, W, f, i, tag, ride=None):
    job = ride() if ride else None
    res = _ffn_gate_up(h, norm_w, W["C"][i].reshape(2, 4, C_ROWS, FF_BLK), f, f"{tag}_gate_up", job=job)
    n, gu, act = res[:3]
    if job is not None:
        ride(res[3:])
    job = ride() if ride else None
    out = _ffn_down(act, W["A"][i], f, h, f"{tag}_down", job=job)
    if job is not None:
        out, got = out
        ride(got)
    return out, (h, n, gu, act)


def _ffn_bwd(dout, saved, norm_w, W, GB, f, i, tag, ride=None):
    h, n, gu, act = saved
    S = h.shape[0]
    def carried(call):
        job = ride() if ride else None
        res = call(job)
        if job is None:
            return res
        ride(res[1])
        return res[0]

    GB["A"][i][f] = carried(lambda job: _ffn_down_dw(act, dout, f"{tag}_down_dw", job=job))
    dgu = _ffn_down_dx(dout, W["A"][i], f, gu, f"{tag}_down_dx").reshape(N_DEV, S, FF_BLK)
    res = carried(lambda job: _ffn_gate_up_dx(dgu, W["C"][i], f, h, norm_w, dout, f"{tag}_gate_up_dx", job=job))
    GB["C"][i][f] = carried(lambda job: _ffn_gate_up_dw(n, dgu, f"{tag}_gate_up_dw", job=job))
    return res


def _rope_tables(S):
    inv = ROPE_THETA ** (-jnp.arange(0, B_ROPE, 2, dtype=F32) / B_ROPE)
    ang = jnp.arange(S, dtype=F32)[:, None] * inv[None, :]
    return jnp.cos(ang), jnp.sin(ang)


def _alibi_columns(S):
    t = jnp.arange(S, dtype=jnp.int32)
    hi = ((t // 16) * 16).astype(F32)
    lo = (t % 16).astype(F32)
    slopes = 2.0 ** (-8.0 * jnp.arange(1, A_HEADS + 1, dtype=F32) / A_HEADS)
    zq = jnp.zeros((S, A_HEADS), F32)
    rest = QK_PAD - A_HEAD_DIM - 4
    qc = jnp.stack([-slopes[None, :] * hi[:, None], -slopes[None, :] * lo[:, None], zq + slopes[None, :], zq + slopes[None, :]] + [zq] * rest, axis=-1)
    one = jnp.ones((S, A_KV_HEADS), F32)
    zk = jnp.zeros((S, A_KV_HEADS), F32)
    kc = jnp.stack([one, one, zk + hi[:, None], zk + lo[:, None]] + [zk] * rest, axis=-1)
    return qc, kc


def _sink_prm(sinks):
    return jnp.zeros((A_HEADS, 1, LANES), F32).at[:, 0, 0].set(sinks.astype(F32))


def _with_ride(ride, call):
    job = ride() if ride else None
    res = call(job)
    if job is None:
        return res
    n_own = len(res) - len(job["outs"])
    ride(res[n_own:])
    return res[:n_own]


def _even_fwd(hn, h, W, ride=None):
    S = hn.shape[0]
    proj = _mm(hn, W["ev_w_in"], name="ev_in")
    a_q, a_k, a_v = proj[:, :512], proj[:, 512:640], proj[:, 640:768]
    c_q, c_kv = proj[:, 768:1024], proj[:, 1024:1152]
    kr1, kr2 = proj[:, 1152:1168], proj[:, 1168:1184]
    qc, kc = _alibi_columns(S)
    qaT = _cols_only(jnp.concatenate([(a_q * A_HEAD_DIM ** -0.5).reshape(S, A_HEADS, A_HEAD_DIM), qc], axis=-1))
    ka, kaT = _rows_and_cols(jnp.concatenate([a_k.reshape(S, A_KV_HEADS, A_HEAD_DIM), kc], axis=-1))
    va3 = a_v.reshape(S, A_KV_HEADS, A_HEAD_DIM)
    va = jnp.transpose(va3.astype(BF16), (1, 0, 2))
    prm = _sink_prm(W["ev_sinks"][0])
    oaT, lse_a = _with_ride(ride, lambda job: _attn_fwd(qaT, ka, _v_with_ones(va3), tile=min(SWA_TILE, S // 2), hb=A_GROUP, window=WINDOW, sink=prm,
                                                        name="swa_fwd", job=job))
    cqn = _rms_fwd(c_q, W["ev_cq_norm"], "ev_cq_norm")
    q_all = _mm(cqn, W["ev_w_uq"], name="ev_uq")
    ckvn = _rms_fwd(c_kv, W["ev_ckv_norm"], "ev_ckv_norm")
    kv_all = _mm(ckvn, W["ev_w_ukv"], name="ev_ukv")
    cos, sin = _rope_tables(S)
    cos8, sin8 = jnp.tile(cos, (1, B_HEADS)), jnp.tile(sin, (1, B_HEADS))
    q1, q2 = _rope(q_all[:, 512:640], q_all[:, 640:768], cos8, sin8, "ev_rope_q")
    k1, k2 = _rope(kr1, kr2, cos, sin, "ev_rope_k")
    half = B_ROPE // 2
    scale = (B_NOPE + B_ROPE) ** -0.5
    qbT = _cols_only(jnp.concatenate([q_all[:, :512].reshape(S, B_HEADS, B_NOPE), q1.reshape(S, B_HEADS, half), q2.reshape(S, B_HEADS, half)], axis=-1) * scale)
    kro = jnp.broadcast_to(jnp.concatenate([k1, k2], axis=1)[:, None, :], (S, B_HEADS, B_ROPE))
    kb, kbT = _rows_and_cols(jnp.concatenate([kv_all[:, :512].reshape(S, B_HEADS, B_NOPE), kro], axis=-1))
    vb3 = kv_all[:, 512:].reshape(S, B_HEADS, B_V)
    vb = jnp.transpose(vb3.astype(BF16), (1, 0, 2))
    obT, lse_b = _with_ride(ride, lambda job: _attn_fwd(qbT, kb, _v_with_ones(vb3), tile=min(ATTN_TILE_FWD, S), hb=2, name="mla_fwd", job=job))
    cat = jnp.concatenate([_from_T(oaT), _from_T(obT)], axis=1)
    out = _mm_w128(cat, W["B"][0], MIX_OUT_BLK, res=h, name="ev_out")
    return out, (hn, proj, (qaT, ka, kaT, va, oaT, lse_a), prm, cqn, ckvn, (qbT, kb, kbT, vb, obT, lse_b), cat)


def _even_bwd(dout, saved, W, GB, norm):
    hn, proj, (qaT, ka, kaT, va, oaT, lse_a), prm, cqn, ckvn, (qbT, kb, kbT, vb, obT, lse_b), cat = saved
    S = hn.shape[0]
    G = {}
    dcat = _mm_w128(dout, W["B"][0], MIX_OUT_BLK, tb=True, out=BF16, name="ev_out_dx")
    GB["B"][0] = _mm_w128_dw(cat, dout, MIX_OUT_BLK, GB["B"][0], "ev_out_dw")
    doaT = _cols_only(dcat[:, :512].reshape(S, A_HEADS, A_HEAD_DIM))
    dqaT, dka, dva, dsink = _attn_bwd(qaT, ka, kaT, va, oaT, doaT, lse_a, tile=min(SWA_TILE, S // 2), hb=A_GROUP, window=WINDOW, sink=prm, real=A_HEAD_DIM,
                                       name="swa_bwd")
    G["ev_sinks"] = dsink[:, 0, 0]
    dqa = _from_T(dqaT) * A_HEAD_DIM ** -0.5
    dka = dka.reshape(A_KV_HEADS, A_GROUP, S, A_HEAD_DIM).sum(axis=1)
    dva = dva.reshape(A_KV_HEADS, A_GROUP, S, A_HEAD_DIM).sum(axis=1)
    dobT = _cols_only(dcat[:, 512:].reshape(S, B_HEADS, B_V))
    dqbT, dkb, dvb = _attn_bwd(qbT, kb, kbT, vb, obT, dobT, lse_b, tile=min(ATTN_TILE, S), hb=1, name="mla_bwd")
    half = B_ROPE // 2
    dqb = jnp.transpose(dqbT, (2, 0, 1)) * (B_NOPE + B_ROPE) ** -0.5
    dkb = jnp.transpose(dkb, (1, 0, 2))
    cos, sin = _rope_tables(S)
    cos8, sin8 = jnp.tile(cos, (1, B_HEADS)), jnp.tile(sin, (1, B_HEADS))
    dq1, dq2 = _rope(dqb[:, :, B_NOPE:B_NOPE + half].reshape(S, -1), dqb[:, :, B_NOPE + half:].reshape(S, -1), cos8, -sin8, "ev_rope_q_bwd")
    dq_all = jnp.concatenate([dqb[:, :, :B_NOPE].reshape(S, -1), dq1, dq2], axis=1).astype(BF16)
    dkr = dkb[:, :, B_NOPE:].sum(axis=1)
    dk1, dk2 = _rope(dkr[:, :half], dkr[:, half:], cos, -sin, "ev_rope_k_bwd")
    dkv_all = jnp.concatenate([dkb[:, :, :B_NOPE].reshape(S, -1), _unheads(dvb)], axis=1).astype(BF16)
    G["ev_w_uq"] = _mm(cqn, dq_all, ta=True, name="ev_uq_dw")
    dcqn = _mm(dq_all, W["ev_w_uq"], tb=True, name="ev_uq_dx")
    dc_q, G["ev_cq_norm"] = _rms_bwd(dcqn, proj[:, 768:1024], W["ev_cq_norm"], None, "ev_cq_norm_bwd")
    G["ev_w_ukv"] = _mm(ckvn, dkv_all, ta=True, name="ev_ukv_dw")
    dckvn = _mm(dkv_all, W["ev_w_ukv"], tb=True, name="ev_ukv_dx")
    dc_kv, G["ev_ckv_norm"] = _rms_bwd(dckvn, proj[:, 1024:1152], W["ev_ckv_norm"], None, "ev_ckv_norm_bwd")
    dproj = jnp.concatenate([dqa, _unheads(dka), _unheads(dva), dc_q, dc_kv, dk1, dk2,
                             jnp.zeros((S, EVEN_IN_PAD - EVEN_IN), F32)], axis=1).astype(BF16)
    G["ev_w_in"] = _mm(hn, dproj, ta=True, name="ev_in_dw")
    dh, dnorm = _mm(dproj, W["ev_w_in"], tb=True, norm_bwd=(*norm, dout), name="ev_in_dx")
    return dh, dnorm, G


def _odd_fwd(hn, h, W, ride=None):
    S = hn.shape[0]
    w = C_HEADS * C_HEAD_DIM
    wp = C_HEADS * QK_PAD
    proj = _mm(hn, W["od_w_in"], name="od_in")
    f_logit = proj[:, 2 * wp + w: 2 * wp + w + C_HEADS]
    logf = _logsig_fwd(f_logit, W["od_b_f"], "od_logsig")
    logc = _cumsum(logf, False, "od_cumsum")
    parts = list(_exact3(logc))
    ones = [jnp.ones((S, C_HEADS), F32)] * 3
    pad = [jnp.zeros((S, C_HEADS), F32)] * (QK_PAD - C_HEAD_DIM - 6)
    lead = ((0, 0), (0, 0), (C_HEAD_DIM, 0))
    q3 = proj[:, :wp].reshape(S, C_HEADS, QK_PAD) + jnp.pad(jnp.stack(parts + ones + pad, axis=-1), lead)
    k3 = proj[:, wp:2 * wp].reshape(S, C_HEADS, QK_PAD) + jnp.pad(jnp.stack(ones + [-p for p in parts] + pad, axis=-1), lead)
    qT = _cols_only(q3)
    k, kT = _rows_and_cols(k3)
    v3 = proj[:, 2 * wp:2 * wp + w].reshape(S, C_HEADS, C_HEAD_DIM)
    v = jnp.transpose(v3.astype(BF16), (1, 0, 2))
    oT, lse = _with_ride(ride, lambda job: _attn_fwd(qT, k, _v_with_ones(v3), tile=min(ATTN_TILE_FWD, S), hb=2, name="fox_fwd", job=job))
    cat = _from_T(oT)
    out = _mm_w128(cat, W["B"][1], MIX_OUT_BLK, res=h, name="od_out")
    return out, (hn, qT, k, kT, v, f_logit, oT, lse, cat)


def _odd_bwd(dout, saved, W, GB, norm):
    hn, qT, k, kT, v, f_logit, oT, lse, cat = saved
    S = hn.shape[0]
    G = {}
    dcat = _mm_w128(dout, W["B"][1], MIX_OUT_BLK, tb=True, out=BF16, name="od_out_dx")
    GB["B"][1] = _mm_w128_dw(cat, dout, MIX_OUT_BLK, GB["B"][1], "od_out_dw")
    doT = _cols_only(dcat.reshape(S, C_HEADS, C_HEAD_DIM))
    dqT, dk, dv, dqxT, dkx = _attn_bwd(qT, k, kT, v, oT, doT, lse, tile=min(ATTN_TILE, S), hb=1, real=C_HEAD_DIM, extra=True,
                                       full=True, name="fox_bwd")
    dlogc = jnp.transpose(dqxT[:, 0, :] - dkx[:, :, 3])
    dlogf = _cumsum(dlogc, True, "od_cumsum_bwd")
    df, db = _logsig_bwd(dlogf, f_logit, W["od_b_f"], "od_logsig_bwd")
    G["od_b_f"] = db
    dproj = jnp.concatenate([_from_T(dqT), _unheads(dk), _unheads(dv), df, jnp.zeros((S, ODD_IN_PAD - ODD_IN_AUG), F32)], axis=1).astype(BF16)
    G["od_w_in"] = _mm(hn, dproj, ta=True, name="od_in_dw")
    dh, dnorm = _mm(dproj, W["od_w_in"], tb=True, norm_bwd=(*norm, dout), name="od_in_dx")
    return dh, dnorm, G


class _Rider:
    def __init__(self, steps, tag):
        self.steps, self.tag, self.count, self.result = steps, tag, 0, None
        self.job = next(steps)

    def __call__(self, got=None):
        if got is not None:
            return self._advance(list(got))
        job = self.job
        if isinstance(job, str):
            self._advance(None)
            return None
        return job

    def _advance(self, value):
        try:
            self.job = self.steps.send(value)
        except StopIteration as done:
            self.job, self.result = None, done.value

    def finish(self):
        while self.job is not None:
            if isinstance(self.job, str):
                self._advance(None)
                continue
            self.count += 1
            self(_comm_call(self.job, f"{self.tag}_{self.count}"))
        return self.result


def _gather_plan(W, slots):
    a0, b0, c0, m, a1, b1, c1 = (slots[key] for key in ("a0", "b0", "c0", "m", "a1", "b1", "c1"))
    (m,) = yield _gather_job([m])
    W.update(_misc_weights(m))
    (b0,) = yield _gather_job([b0])
    W["B"] = [b0]
    (c0,) = yield _gather_job([c0], rows=[(D_MODEL, D_MODEL)])
    W["C"] = [c0]
    a0, c1 = yield _gather_job([a0, c1], rows=[(DOWN_ROWS, DOWN_ROWS), (0, D_MODEL)])
    W["A"] = [a0]
    W["C"].append(c1)
    (a1,) = yield _gather_job([a1], rows=[(0, DOWN_ROWS)])
    W["A"].append(a1)
    for _ in range(3):
        yield "skip"
    a1, b1, c1 = yield _gather_job([a1, b1, c1], rows=[(DOWN_ROWS, DOWN_ROWS), None, (D_MODEL, D_MODEL)])
    W["A"][1], W["C"][1] = a1, c1
    W["B"].append(b1)


def _local_step(x, p, target, W, slots):
    h = x
    saved = []
    gather = _Rider(_gather_plan(W, slots), "all_gather_rest")
    for i in range(DEPTH):
        t = f"l{i}"
        h1, s_a = _ffn_fwd(h, W["ffa_norm"][i:i + 1], W, 0, i, f"{t}_ffa", gather)
        nm = _rms_fwd(h1, W["mix_norm"][i:i + 1], f"{t}_mix_norm")
        h2, s_m = (_even_fwd if i % 2 == 0 else _odd_fwd)(nm, h1, W, gather)
        h3, s_b = _ffn_fwd(h2, W["ffb_norm"][i:i + 1], W, 1, i, f"{t}_ffb", gather)
        npl = _rms_fwd(h3, W["ple_norm"][i:i + 1], f"{t}_ple_norm")
        gpre = _mm_w128(npl, W["B"][i], PLE_GATE_BLK, out=BF16, name=f"{t}_ple_gate")
        pp = _mm(p[i], W["ple_w_proj"][i], out=BF16, name=f"{t}_ple_proj")
        h4 = _ple_fwd(h3, gpre, pp, f"{t}_ple")
        saved.append((s_a, h1, s_m, s_b, h3, npl, gpre, pp))
        h = h4
    gather.finish()
    dh, g_final, loss_cols = _final_fwd_bwd(h, W["final_norm"], target, "final")
    G = {"final_norm": g_final}
    GB = {"A": [[None, None] for _ in range(DEPTH)], "C": [[None, None] for _ in range(DEPTH)],
          "B": [lax.empty((N_DEV, B_ROWS, D_MODEL), BF16) for _ in range(DEPTH)]}
    per_layer = {n: [None] * DEPTH for n in ("ffa_norm", "mix_norm", "ffb_norm", "ple_norm", "ple_w_proj")}
    scatter = scatter_mid = None
    for i in reversed(range(DEPTH)):
        t = f"l{i}"
        s_a, h1, s_m, s_b, h3, npl, gpre, pp = saved[i]
        dgpre, dpp = _ple_bwd(dh, gpre, pp, f"{t}_ple_bwd")
        per_layer["ple_w_proj"][i] = _mm(p[i], dpp, ta=True, name=f"{t}_ple_proj_dw")
        GB["B"][i] = _mm_w128_dw(npl, dgpre, PLE_GATE_BLK, GB["B"][i], f"{t}_ple_gate_dw")
        dh, per_layer["ple_norm"][i] = _mm_w128(dgpre, W["B"][i], PLE_GATE_BLK, tb=True, norm_bwd=(h3, W["ple_norm"][i:i + 1], dh),
                                                name=f"{t}_ple_gate_dx")
        dh, per_layer["ffb_norm"][i] = _ffn_bwd(dh, s_b, W["ffb_norm"][i:i + 1], W, GB, 1, i, f"{t}_ffb", scatter)
        dh, per_layer["mix_norm"][i], g_mix = (_even_bwd if i % 2 == 0 else _odd_bwd)(dh, s_m, W, GB, (h1, W["mix_norm"][i:i + 1]))
        G.update(g_mix)
        if i == 0:
            G["ple_w_proj"] = per_layer["ple_w_proj"]
            mid = [GB["A"][0][1], GB["C"][0][1], GB["B"][0], _misc_grads(G).astype(BF16)]
            scatter_mid = _Rider(_reduce_scatter_steps(mid, "mid"), "rs_mid")
        dh, per_layer["ffa_norm"][i] = _ffn_bwd(dh, s_a, W["ffa_norm"][i:i + 1], W, GB, 0, i, f"{t}_ffa", scatter_mid)
        if i == DEPTH - 1:
            later = [GB["A"][i][0], GB["A"][i][1], GB["C"][i][0], GB["C"][i][1], GB["B"][i]]
            scatter = _Rider(_reduce_scatter_steps(later, "later"), "rs_later")
    for n in ("ffa_norm", "mix_norm", "ffb_norm", "ple_norm"):
        G[n] = jnp.concatenate(per_layer[n], axis=0)
    return loss_cols, dh, scatter.finish(), scatter_mid.finish(), [GB["A"][0][0], GB["C"][0][0]], G


def kernel(x, p, ffa_norm, ffa_w_gate_up, ffa_w_down, mix_norm, ffb_norm, ffb_w_gate_up, ffb_w_down, ple_norm, ple_w_gate, ple_w_proj, ev_w_in, ev_sinks, ev_cq_norm, ev_w_uq, ev_ckv_norm, ev_w_ukv, ev_w_out, od_w_in, od_b_f, od_w_out, final_norm, loss_target, m_ffa_norm, m_ffa_w_gate_up, m_ffa_w_down, m_mix_norm, m_ffb_norm, m_ffb_w_gate_up, m_ffb_w_down, m_ple_norm, m_ple_w_gate, m_ple_w_proj, m_ev_w_in, m_ev_sinks, m_ev_cq_norm, m_ev_w_uq, m_ev_ckv_norm, m_ev_w_ukv, m_ev_w_out, m_od_w_in, m_od_b_f, m_od_w_out, m_final_norm, v_ffa_norm, v_ffa_w_gate_up, v_ffa_w_down, v_mix_norm, v_ffb_norm, v_ffb_w_gate_up, v_ffb_w_down, v_ple_norm, v_ple_w_gate, v_ple_w_proj, v_ev_w_in, v_ev_sinks, v_ev_cq_norm, v_ev_w_uq, v_ev_ckv_norm, v_ev_w_ukv, v_ev_w_out, v_od_w_in, v_od_b_f, v_od_w_out, v_final_norm):
    given = dict(locals())
    w_in = {n: given[n] for n in WEIGHTS}

    layers, misc = _local_groups(w_in, BF16)
    (a0, b0, c0), (a1, b1, c1) = [[_in_slot(g) for g in layer] for layer in layers]
    a0, c0 = _comm_call(_gather_job([a0, c0], rows=[(0, DOWN_ROWS), (0, D_MODEL)]), "all_gather_first")
    W = {n: w_in[n] for n in SMALL}
    W["final_norm"] = final_norm.reshape(1, -1)
    W.update(A=[a0], C=[c0])
    slots = dict(a0=a0, b0=b0, c0=c0, m=_in_slot(misc), a1=a1, b1=b1, c1=c1)

    loss_cols, dx, r_later, r_mid, last, G = _local_step(x[0], p[:, 0], loss_target[0], W, slots)

    a1f, a1b, c1f, c1b, b1 = r_later
    a0b, c0b, b0, r_misc = r_mid
    a0f, c0f = _reduce_scatter(last, "last")
    grads = _ungroup_local([[a0f, a0b], [a1f, a1b]], [b0, b1], [[c0f, c0b], [c1f, c1b]], r_misc)
    layout = [(n, int(np.prod(w_in[n].shape))) for n in SMALL]
    vec = jnp.concatenate([G[n].astype(F32).reshape(-1) for n, _ in layout] + [jnp.sum(loss_cols).reshape(1)])
    vec = jnp.pad(vec, (0, N_DEV * SMALL_COLS - vec.shape[0])).reshape(N_DEV, SMALL_COLS)
    vec = _all_reduce_small(vec).reshape(-1)
    off = 0
    for n, size in layout:
        grads[n] = vec[off: off + size].reshape(w_in[n].shape)
        off += size
    loss = vec[off]

    delta, new_m, new_v = {}, {}, {}
    for n in WEIGHTS:
        shp = w_in[n].shape
        as2d = (lambda a: a.reshape(1, -1)) if len(shp) == 1 else (lambda a: a)
        d, nm, nv = _adamw(as2d(w_in[n]), as2d(grads[n]), as2d(given["m_" + n]), as2d(given["v_" + n]), f"adamw_{n}")
        delta[n], new_m[n], new_v[n] = d.reshape(shp), nm.reshape(shp), nv.reshape(shp)
    return (loss, dx[None], *[grads[n] for n in WEIGHTS], *[delta[n] for n in WEIGHTS],
            *[new_m[n] for n in WEIGHTS], *[new_v[n] for n in WEIGHTS])
```

```python
import functools

import numpy as np
import jax
import jax.numpy as jnp
from jax import lax
from jax.experimental import pallas as pl
from jax.experimental.pallas import tpu as pltpu

F32 = jnp.float32
BF16 = jnp.bfloat16
MESH = pl.DeviceIdType.MESH

D_MODEL = 1024
D_FF = 2816
RMS_EPS = 1e-6
PLE_DIM = 256
A_HEADS, A_KV_HEADS, A_HEAD_DIM, WINDOW = 8, 2, 64, 128
A_GROUP = A_HEADS // A_KV_HEADS
B_HEADS, B_NOPE, B_ROPE, B_V = 8, 64, 32, 64
ROPE_THETA = 10000.0
C_HEADS, C_HEAD_DIM = 16, 64
EVEN_IN = 1184
EVEN_IN_PAD = 1280
ODD_IN_AUG = 2 * 16 * 80 + 1024 + 16
ODD_IN_PAD = 3840
DEPTH = 2
ADAM_LR, ADAM_B1, ADAM_B2, ADAM_EPS, ADAM_WD, ADAM_STEP = 0.001, 0.9, 0.999, 1e-08, 0.01, 10

N_DEV = 8
LANES = 128
EW_TILE_BYTES = 3 << 20
MM_VMEM_BYTES = 26 << 20
NEG = -1e30
ATTN_TILE = 1024
ATTN_TILE_FWD = 1024
SWA_TILE = 512
QK_PAD = 80

FF_BLK = D_FF // 4
DOWN_ROWS = D_FF // N_DEV
B_ROWS, C_ROWS, G3_ROWS, G3_COLS = 256, 2 * D_MODEL, 1024, 768
PLE_GATE_BLK, MIX_OUT_BLK = 0, 1
OD_C, EV_C, STRIP_C = 386, 148, 128
STRIP0 = OD_C + EV_C

SMALL = ["ffa_norm", "mix_norm", "ffb_norm", "ple_norm", "ev_sinks", "ev_cq_norm", "ev_ckv_norm", "od_b_f", "final_norm"]
WEIGHTS = ["ffa_norm", "ffa_w_gate_up", "ffa_w_down", "mix_norm", "ffb_norm", "ffb_w_gate_up", "ffb_w_down", "ple_norm",
           "ple_w_gate", "ple_w_proj", "ev_w_in", "ev_sinks", "ev_cq_norm", "ev_w_uq", "ev_ckv_norm", "ev_w_ukv", "ev_w_out",
           "od_w_in", "od_b_f", "od_w_out", "final_norm"]
SMALL_COLS = 1280


def _divisor(n, cap, mult):
    if n <= cap:
        return n
    for t in range(cap - cap % mult, 0, -mult):
        if n % t == 0:
            return t
    raise ValueError(f"no tile for {n} under {cap} in steps of {mult}")


def _lanes(c):
    return -(-c // LANES) * LANES


def _ew(fn, rows, vecs, outs, reds=(), *, name):
    R = rows[0].shape[0]
    per_row = sum(_lanes(a.shape[1]) * a.dtype.itemsize for a in rows) + sum(_lanes(c) * jnp.dtype(d).itemsize for c, d in outs)
    tm = _divisor(R, max(16, EW_TILE_BYTES // per_row // 16 * 16), 16) if R % 16 == 0 else R
    n_r, n_v, n_o = len(rows), len(vecs), len(outs)

    def body(*refs):
        ins = [r[...] for r in refs[: n_r + n_v]]
        res = fn(*ins)
        if not isinstance(res, (tuple, list)):
            res = (res,)
        o_refs = refs[n_r + n_v: n_r + n_v + n_o]
        r_refs = refs[n_r + n_v + n_o:]
        for ref, val in zip(o_refs, res[:n_o]):
            ref[...] = val.astype(ref.dtype)
        if r_refs:
            @pl.when(pl.program_id(0) == 0)
            def _():
                for ref in r_refs:
                    ref[...] = jnp.zeros_like(ref)
            for ref, val in zip(r_refs, res[n_o:]):
                ref[...] += val

    in_specs = [pl.BlockSpec((tm, a.shape[1]), lambda i: (i, 0)) for a in rows]
    in_specs += [pl.BlockSpec((1, a.shape[1]), lambda i: (0, 0)) for a in vecs]
    out_specs = [pl.BlockSpec((tm, c), lambda i: (i, 0)) for c, _ in outs]
    out_specs += [pl.BlockSpec((1, c), lambda i: (0, 0)) for c in reds]
    out_shape = [jax.ShapeDtypeStruct((R, c), d) for c, d in outs] + [jax.ShapeDtypeStruct((1, c), F32) for c in reds]
    res = pl.pallas_call(body, name=name, grid=(R // tm,), in_specs=in_specs, out_specs=out_specs, out_shape=out_shape)(*rows, *vecs)
    return res[0] if len(res) == 1 else res


def _rms_fwd(x, w, name):
    def fn(x, w):
        y = x * lax.rsqrt(jnp.mean(x * x, axis=-1, keepdims=True) + RMS_EPS)
        return y * w
    return _ew(fn, [x], [w], [(x.shape[1], BF16)], name=name)


def _rms_bwd(dn, x, w, dres, name):
    def fn(dn, x, *rest):
        w = rest[-1]
        r = lax.rsqrt(jnp.mean(x * x, axis=-1, keepdims=True) + RMS_EPS)
        xh = x * r
        gw = dn * w
        dx = r * (gw - xh * jnp.mean(gw * xh, axis=-1, keepdims=True))
        if len(rest) == 2:
            dx = dx + rest[0]
        return dx, jnp.sum(dn * xh, axis=0, keepdims=True)
    rows = [dn, x] + ([dres] if dres is not None else [])
    return _ew(fn, rows, [w], [(x.shape[1], F32)], [x.shape[1]], name=name)


def _ple_fwd(h, gpre, pp, name):
    return _ew(lambda h, g, q: h + jax.nn.sigmoid(g.astype(F32)) * q.astype(F32), [h, gpre, pp], [], [(h.shape[1], F32)], name=name)


def _ple_bwd(dh, gpre, pp, name):
    def fn(dh, g, q):
        sg = jax.nn.sigmoid(g.astype(F32))
        return dh * q.astype(F32) * (sg * (1.0 - sg)), dh * sg
    return _ew(fn, [dh, gpre, pp], [], [(dh.shape[1], BF16), (dh.shape[1], BF16)], name=name)


def _rope(x1, x2, cos, sin, name):
    c = x1.shape[1]
    return _ew(lambda a, b, co, si: (a * co - b * si, a * si + b * co), [x1, x2, cos, sin], [], [(c, F32), (c, F32)], name=name)


def _logsig_fwd(f, b, name):
    def fn(f, b):
        z = f + b
        return jnp.minimum(z, 0.0) - jnp.log(1.0 + jnp.exp(-jnp.abs(z)))
    return _ew(fn, [f], [b], [(f.shape[1], F32)], name=name)


def _logsig_bwd(dlogf, f, b, name):
    def fn(d, f, b):
        df = d * jax.nn.sigmoid(-(f + b))
        return df, jnp.sum(df, axis=0, keepdims=True)
    return _ew(fn, [dlogf, f], [b], [(f.shape[1], F32)], [f.shape[1]], name=name)


def _final_fwd_bwd(h, w, target, name):
    d = h.shape[1]

    def fn(h, t, w):
        r = lax.rsqrt(jnp.mean(h * h, axis=-1, keepdims=True) + RMS_EPS)
        xh = h * r
        y = xh * w
        err = y - t
        dy = err * (1.0 / d)
        gw = dy * w
        dx = r * (gw - xh * jnp.mean(gw * xh, axis=-1, keepdims=True))
        return dx, jnp.sum(dy * xh, axis=0, keepdims=True), jnp.sum(err * err, axis=0, keepdims=True) * (0.5 / d)
    return _ew(fn, [h, target], [w], [(d, F32)], [d, d], name=name)


def _adamw(w, g, m, v, name):
    shape = w.shape
    c = shape[-1]
    w2, g2, m2, v2 = (a.reshape(-1, c) for a in (w, g, m, v))

    def fn(w, g, m, v):
        m = ADAM_B1 * m + (1.0 - ADAM_B1) * g
        v = ADAM_B2 * v + (1.0 - ADAM_B2) * jnp.square(g)
        m_hat = m / (1.0 - ADAM_B1 ** ADAM_STEP)
        v_hat = v / (1.0 - ADAM_B2 ** ADAM_STEP)
        delta = -ADAM_LR * (m_hat / (jnp.sqrt(v_hat) + ADAM_EPS) + ADAM_WD * w)
        return delta, m, v
    d, nm, nv = _ew(fn, [w2, g2, m2, v2], [], [(c, F32)] * 3, name=name)
    return d.reshape(shape), nm.reshape(shape), nv.reshape(shape)


def _split3(v):
    hi = v.astype(BF16)
    r1 = v - hi.astype(F32)
    mid = r1.astype(BF16)
    lo = (r1 - mid.astype(F32)).astype(BF16)
    return hi, mid, lo


def _cumsum(x, reverse, name):
    S, C = x.shape
    tm = _divisor(S, 512, 16)
    nt = S // tm

    def body(x_ref, o_ref, carry):
        @pl.when(pl.program_id(0) == 0)
        def _():
            carry[...] = jnp.zeros_like(carry)
        r = lax.broadcasted_iota(jnp.int32, (tm, tm), 0)
        c = lax.broadcasted_iota(jnp.int32, (tm, tm), 1)
        tri = jnp.where((c >= r) if reverse else (c <= r), 1.0, 0.0).astype(BF16)
        xv = x_ref[...]
        acc = jnp.zeros((tm, C), F32)
        for part in _split3(xv):
            acc = acc + jnp.dot(tri, part, preferred_element_type=F32)
        o_ref[...] = acc + carry[...]
        carry[...] += jnp.sum(xv, axis=0, keepdims=True)

    idx = (lambda i: (nt - 1 - i, 0)) if reverse else (lambda i: (i, 0))
    return pl.pallas_call(
        body, name=name, grid=(nt,), in_specs=[pl.BlockSpec((tm, C), idx)], out_specs=pl.BlockSpec((tm, C), idx),
        out_shape=jax.ShapeDtypeStruct((S, C), F32), scratch_shapes=[pltpu.VMEM((1, C), F32)],
    )(x)


NN = (((1,), (0,)), ((), ()))
NT = (((1,), (1,)), ((), ()))
TN = (((0,), (0,)), ((), ()))

HBM_SPEC = pl.BlockSpec(memory_space=pl.ANY)


def _job_in_body(job, refs, n_in, n_out, n_scr, grid):
    if job is None:
        return refs[n_in:], lambda: None
    ji, jo = len(job["ins"]), len(job["outs"])
    j_in = refs[n_in: n_in + ji]
    pos = n_in + ji
    own = list(refs[pos: pos + n_out])
    pos += n_out
    j_out = refs[pos: pos + jo]
    pos += jo
    own += list(refs[pos: pos + n_scr])
    ss, rs = refs[-2], refs[-1]
    first = functools.reduce(jnp.logical_and, [pl.program_id(d) == 0 for d in range(len(grid))])
    last = functools.reduce(jnp.logical_and, [pl.program_id(d) == n - 1 for d, n in enumerate(grid)])

    @pl.when(first)
    def _():
        job["start"](j_in, j_out, ss, rs)

    def finish():
        @pl.when(last)
        def _():
            job["finish"](j_in, j_out, ss, rs)

    return own, finish


def _job_call(job, body, *, name, grid, in_specs, out_specs, out_shape, args, scratch_shapes, aliases, dimension_semantics):
    in_specs, out_specs, out_shape, args, scratch_shapes = list(in_specs), list(out_specs), list(out_shape), list(args), list(scratch_shapes)
    aliases = dict(aliases)
    if job is not None:
        for i, o in job["aliases"].items():
            aliases[len(args) + i] = len(out_shape) + o
        in_specs += [HBM_SPEC] * len(job["ins"])
        args += list(job["ins"])
        out_specs += [HBM_SPEC] * len(job["outs"])
        out_shape += list(job["outs"])
        scratch_shapes += [pltpu.SemaphoreType.DMA((job["n_sems"],)), pltpu.SemaphoreType.DMA((job["n_sems"],))]
    return pl.pallas_call(
        body, name=name, grid=grid, in_specs=in_specs, out_specs=out_specs, out_shape=out_shape,
        scratch_shapes=scratch_shapes, input_output_aliases=aliases,
        compiler_params=pltpu.CompilerParams(dimension_semantics=dimension_semantics),
    )(*args)


def _comm_call(job, name):
    def body(*refs):
        ji, jo = len(job["ins"]), len(job["outs"])
        job["start"](refs[:ji], refs[ji: ji + jo], refs[-2], refs[-1])
        job["finish"](refs[:ji], refs[ji: ji + jo], refs[-2], refs[-1])

    return pl.pallas_call(
        body, name=name, in_specs=[HBM_SPEC] * len(job["ins"]), out_specs=[HBM_SPEC] * len(job["outs"]), out_shape=list(job["outs"]),
        input_output_aliases=dict(job["aliases"]),
        scratch_shapes=[pltpu.SemaphoreType.DMA((job["n_sems"],)), pltpu.SemaphoreType.DMA((job["n_sems"],))],
    )(*job["ins"])


def _mm_call(name, grid, k_axis, a, a_spec, a2d, b, b_spec, b2d, dims, out_sds, out_spec, o2d, *,
             alpha=1.0, res=None, res_spec=None, into=None, job=None, norm_bwd=None):
    nk = grid[k_axis]
    n_in = 2 + (res is not None) + (into is not None) + (3 if norm_bwd is not None else 0)
    n_out = 2 if norm_bwd is not None else 1

    def body(*refs):
        a_ref, b_ref = refs[0], refs[1]
        res_ref = refs[2] if res is not None else None
        own, finish_job = _job_in_body(job, refs, n_in, n_out, 1, grid)
        o_ref, acc_ref = own[0], own[-1]
        k = pl.program_id(k_axis)

        @pl.when(k == 0)
        def _():
            acc_ref[...] = jnp.zeros_like(acc_ref)

        if norm_bwd is not None:
            x_ref, w_ref, dres_ref = refs[n_in - 3: n_in]
            dw_ref = own[1]

            @pl.when(functools.reduce(jnp.logical_and, [pl.program_id(d) == 0 for d in range(len(grid))]))
            def _():
                dw_ref[...] = jnp.zeros_like(dw_ref)

        av = a_ref[...].reshape(a2d).astype(BF16)
        bv = b_ref[...].reshape(b2d).astype(BF16)
        acc_ref[...] += lax.dot_general(av, bv, dims, preferred_element_type=F32)

        @pl.when(k == nk - 1)
        def _():
            r = acc_ref[...]
            if alpha != 1.0:
                r = r * alpha
            if res_ref is not None:
                r = res_ref[...].reshape(o2d) + r
            if norm_bwd is not None:
                x = x_ref[...]
                rs = lax.rsqrt(jnp.mean(x * x, axis=-1, keepdims=True) + RMS_EPS)
                xh = x * rs
                gw = r * w_ref[...]
                dw_ref[...] += jnp.sum(r * xh, axis=0, keepdims=True)
                r = dres_ref[...] + rs * (gw - xh * jnp.mean(gw * xh, axis=-1, keepdims=True))
            o_ref[...] = r.reshape(o_ref.shape).astype(o_ref.dtype)

        finish_job()

    in_specs, args = [a_spec, b_spec], [a, b]
    if res is not None:
        in_specs.append(res_spec)
        args.append(res)
    aliases = {}
    if into is not None:
        aliases = {len(args): 0}
        in_specs.append(pl.BlockSpec(memory_space=pl.ANY))
        args.append(into)
        out_sds = jax.ShapeDtypeStruct(into.shape, into.dtype)
    out_specs, out_shape = [out_spec], [out_sds]
    if norm_bwd is not None:
        vec = pl.BlockSpec((1, o2d[1]), lambda *_: (0, 0))
        in_specs += [out_spec, vec, out_spec]
        args += list(norm_bwd)
        out_specs.append(vec)
        out_shape.append(jax.ShapeDtypeStruct((1, o2d[1]), F32))
    serial = job is not None or norm_bwd is not None
    sem = tuple("arbitrary" if d == k_axis or serial else "parallel" for d in range(len(grid)))
    res_all = _job_call(
        job, body, name=name, grid=grid, in_specs=in_specs, out_specs=out_specs, out_shape=out_shape, args=args,
        scratch_shapes=[pltpu.VMEM(o2d, F32)], aliases=aliases, dimension_semantics=sem)
    own = res_all[0] if n_out == 1 else tuple(res_all[:n_out])
    return own if job is None else (own, res_all[n_out:])


def _mm(a, b, *, ta=False, tb=False, out=F32, res=None, alpha=1.0, norm_bwd=None, name):
    K, M = a.shape if ta else a.shape[::-1]
    N = b.shape[0] if tb else b.shape[1]
    assert (b.shape[1] if tb else b.shape[0]) == K, (a.shape, b.shape, ta, tb)
    tk = _divisor(K, 1024, LANES)
    tn = _divisor(N, 1408, LANES)
    assert norm_bwd is None or tn == N
    for cap in (1024, 512, 256, 128):
        tm = _divisor(M, cap, LANES if ta else 16)
        est = 2 * (tm * tk * a.dtype.itemsize + tk * tn * b.dtype.itemsize + tm * tn * jnp.dtype(out).itemsize)
        est += tm * tn * 4 + (2 * tm * tn * 4 if res is not None else 0) + (4 * tm * tn * 4 if norm_bwd is not None else 0)
        if est <= MM_VMEM_BYTES:
            break
    a_spec = pl.BlockSpec((tk, tm), lambda i, j, k: (k, i)) if ta else pl.BlockSpec((tm, tk), lambda i, j, k: (i, k))
    b_spec = pl.BlockSpec((tn, tk), lambda i, j, k: (j, k)) if tb else pl.BlockSpec((tk, tn), lambda i, j, k: (k, j))
    o_spec = pl.BlockSpec((tm, tn), lambda i, j, k: (i, j))
    dims = (((0 if ta else 1,), (1 if tb else 0,)), ((), ()))
    return _mm_call(name, (M // tm, N // tn, K // tk), 2, a, a_spec, (tk, tm) if ta else (tm, tk), b, b_spec,
                    (tn, tk) if tb else (tk, tn), dims, jax.ShapeDtypeStruct((M, N), out), o_spec, (tm, tn),
                    alpha=alpha, res=res, res_spec=o_spec, norm_bwd=norm_bwd)


def _w128_spec(blk):
    return pl.BlockSpec((N_DEV, 128, D_MODEL), lambda *_: (0, blk, 0))


def _mm_w128(a, G1, blk, *, tb=False, res=None, out=F32, norm_bwd=None, name):
    S = a.shape[0]
    tm = _divisor(S, 1024 if norm_bwd is None else 512, 16)
    row = pl.BlockSpec((tm, D_MODEL), lambda i, k: (i, 0))
    return _mm_call(name, (S // tm, 1), 1, a, row, (tm, D_MODEL), G1, _w128_spec(blk), (D_MODEL, D_MODEL), NT if tb else NN,
                    jax.ShapeDtypeStruct((S, D_MODEL), out), row, (tm, D_MODEL), res=res, res_spec=row, norm_bwd=norm_bwd)


def _mm_w128_dw(a, b, blk, into, name):
    S = a.shape[0]
    tk = _divisor(S, 1024, 16)
    row = pl.BlockSpec((tk, D_MODEL), lambda i, k: (k, 0))
    return _mm_call(name, (1, S // tk), 1, a, row, (tk, D_MODEL), b, row, (tk, D_MODEL), TN, None, _w128_spec(blk),
                    (D_MODEL, D_MODEL), into=into)


def _ffn_gate_up(h, norm_w, G2v, rb, name, job=None):
    S = h.shape[0]
    tm = _divisor(S, 1024, 16)
    grid = (S // tm, 4)

    def body(*refs):
        h_ref, nw_ref, w_ref = refs[:3]
        (n_ref, gu_ref, act_ref, n_scr), finish_job = _job_in_body(job, refs, 3, 3, 1, grid)

        @pl.when(pl.program_id(1) == 0)
        def _():
            x = h_ref[...]
            y = x * lax.rsqrt(jnp.mean(x * x, axis=-1, keepdims=True) + RMS_EPS)
            n_scr[...] = (y * nw_ref[...]).astype(BF16)
            n_ref[...] = n_scr[...]

        nv = n_scr[...]
        g = jnp.dot(nv, w_ref[0, 0], preferred_element_type=F32)
        u = jnp.dot(nv, w_ref[1, 0], preferred_element_type=F32)
        sg = jax.nn.sigmoid(g)
        silu = g * sg
        gu_ref[0, 0] = (u * (sg * (1.0 + g * (1.0 - sg)))).astype(BF16)
        gu_ref[1, 0] = silu.astype(BF16)
        act_ref[0] = (silu * u).astype(BF16)
        finish_job()

    row = pl.BlockSpec((tm, D_MODEL), lambda i, j: (i, 0))
    return _job_call(
        job, body, name=name, grid=grid,
        in_specs=[row, pl.BlockSpec((1, D_MODEL), lambda i, j: (0, 0)), pl.BlockSpec((2, 1, D_MODEL, FF_BLK), lambda i, j: (0, j, rb, 0))],
        out_specs=[row, pl.BlockSpec((2, 1, tm, FF_BLK), lambda i, j: (0, j, i, 0)), pl.BlockSpec((1, tm, FF_BLK), lambda i, j: (j, i, 0))],
        out_shape=[jax.ShapeDtypeStruct((S, D_MODEL), BF16), jax.ShapeDtypeStruct((2, 4, S, FF_BLK), BF16), jax.ShapeDtypeStruct((4, S, FF_BLK), BF16)],
        args=[h, norm_w, G2v], scratch_shapes=[pltpu.VMEM((tm, D_MODEL), BF16)], aliases={},
        dimension_semantics=("arbitrary" if job is not None else "parallel", "arbitrary"))


def _ffn_down(act, G1, ob, h, name, job=None):
    S = h.shape[0]
    tm = _divisor(S, 1024, 16)
    row = pl.BlockSpec((tm, D_MODEL), lambda i, k: (i, 0))
    return _mm_call(name, (S // tm, 4), 1, act, pl.BlockSpec((1, tm, FF_BLK), lambda i, k: (k, i, 0)), (tm, FF_BLK),
                    G1, pl.BlockSpec((2, DOWN_ROWS, D_MODEL), lambda i, k: (k, ob, 0)), (FF_BLK, D_MODEL), NN,
                    jax.ShapeDtypeStruct((S, D_MODEL), F32), row, (tm, D_MODEL), alpha=0.5, res=h, res_spec=row, job=job)


def _ffn_down_dx(dh, G1, ob, gu, name):
    S = dh.shape[0]
    tm = _divisor(S, 1024, 16)

    def body(dh_ref, w_ref, gu_ref, o_ref):
        w = w_ref[...].reshape(FF_BLK, D_MODEL)
        dact = lax.dot_general(dh_ref[...].astype(BF16), w, NT, preferred_element_type=F32) * 0.5
        o_ref[0, 0] = (dact * gu_ref[0, 0].astype(F32)).astype(BF16)
        o_ref[1, 0] = (dact * gu_ref[1, 0].astype(F32)).astype(BF16)

    blk = pl.BlockSpec((2, 1, tm, FF_BLK), lambda i, j: (0, j, i, 0))
    return pl.pallas_call(
        body, name=name, grid=(S // tm, 4),
        in_specs=[pl.BlockSpec((tm, D_MODEL), lambda i, j: (i, 0)), pl.BlockSpec((2, DOWN_ROWS, D_MODEL), lambda i, j: (j, ob, 0)), blk],
        out_specs=blk, out_shape=jax.ShapeDtypeStruct((2, 4, S, FF_BLK), BF16),
    )(dh, G1, gu)


def _ffn_down_dw(act, dh, name, job=None):
    S = dh.shape[0]
    tk = _divisor(S, 1024, 16)
    return _mm_call(name, (4, S // tk), 1, act, pl.BlockSpec((1, tk, FF_BLK), lambda j, k: (j, k, 0)), (tk, FF_BLK),
                    dh, pl.BlockSpec((tk, D_MODEL), lambda j, k: (k, 0)), (tk, D_MODEL), TN,
                    jax.ShapeDtypeStruct((N_DEV, DOWN_ROWS, D_MODEL), BF16),
                    pl.BlockSpec((2, DOWN_ROWS, D_MODEL), lambda j, k: (j, 0, 0)), (FF_BLK, D_MODEL), alpha=0.5, job=job)


def _ffn_gate_up_dw(n, dgu8, name, job=None):
    S = n.shape[0]
    tk = _divisor(S, 1024, 16)
    return _mm_call(name, (N_DEV, S // tk), 1, n, pl.BlockSpec((tk, D_MODEL), lambda b, k: (k, 0)), (tk, D_MODEL),
                    dgu8, pl.BlockSpec((1, tk, FF_BLK), lambda b, k: (b, k, 0)), (tk, FF_BLK), TN,
                    jax.ShapeDtypeStruct((N_DEV, D_MODEL, FF_BLK), BF16),
                    pl.BlockSpec((1, D_MODEL, FF_BLK), lambda b, k: (b, 0, 0)), (D_MODEL, FF_BLK), job=job)


def _ffn_gate_up_dx(dgu8, G2, rb, h, norm_w, dres, name, job=None):
    S = h.shape[0]
    tm = _divisor(S, 1024, 16)
    row = pl.BlockSpec((tm, D_MODEL), lambda i, k: (i, 0))
    return _mm_call(name, (S // tm, N_DEV), 1, dgu8, pl.BlockSpec((1, tm, FF_BLK), lambda i, k: (k, i, 0)), (tm, FF_BLK),
                    G2, pl.BlockSpec((1, D_MODEL, FF_BLK), lambda i, k: (k, rb, 0)), (D_MODEL, FF_BLK), NT,
                    jax.ShapeDtypeStruct((S, D_MODEL), F32), row, (tm, D_MODEL), norm_bwd=(h, norm_w, dres), job=job)


def _unheads(x):
    h, S, d = x.shape
    return jnp.transpose(x, (1, 0, 2)).reshape(S, h * d)


def _exact3(v):
    rnd = lambda a: lax.reduce_precision(a, exponent_bits=8, mantissa_bits=7)
    hi = rnd(v)
    mid = rnd(v - hi)
    return hi, mid, rnd(v - hi - mid)


def _causal_mask(st, q0, k0, window):
    dist = (q0 + lax.broadcasted_iota(jnp.int32, st.shape, 1)) - (k0 + lax.broadcasted_iota(jnp.int32, st.shape, 0))
    mask = dist >= 0
    if window is not None:
        mask = mask & (dist < window)
    return jnp.where(mask, st, NEG)


def _attn_fwd(qT, k, vT1, *, tile, hb, window=None, sink=None, name, job=None):
    H, dqk, S = qT.shape
    G = H // k.shape[0]
    dvp = vT1.shape[1]
    dv = dvp - 16
    tq = tk = tile
    assert H % hb == 0 and (G == 1 or G % hb == 0)
    kvb = hb if G == 1 else 1
    grid = (H // hb, S // tq)
    n_in = 3 + (sink is not None)

    def body(*refs):
        q_ref, k_ref, v_ref = refs[:3]
        (o_ref, lse_ref), finish_job = _job_in_body(job, refs, n_in, 2, 0, grid)
        i = pl.program_id(1)
        carry = []
        for a in range(hb):
            if sink is not None:
                carry.append(jnp.zeros((1, tq), F32) + refs[3][a, :, 0:1])
                carry.append(jnp.where(lax.broadcasted_iota(jnp.int32, (dvp, tq), 0) == dv, 1.0, 0.0))
            else:
                carry.append(jnp.full((1, tq), NEG, F32))
                carry.append(jnp.zeros((dvp, tq), F32))

        def step(j, carry, masked, off=None, keys=tk, q_from=0):
            off = pl.multiple_of(j * tk, tk) if off is None else off
            out = []
            for a in range(hb):
                m, acc = carry[2 * a], carry[2 * a + 1]
                kv = a if kvb > 1 else 0
                st = jnp.dot(k_ref[kv, pl.ds(off, keys), :], q_ref[a][:, q_from:], preferred_element_type=F32)
                if masked:
                    st = _causal_mask(st, i * tq + q_from, off, window)
                m_old, acc_old = m[:, q_from:], acc[:, q_from:]
                m_new = jnp.maximum(m_old, jnp.max(st, axis=0, keepdims=True))
                pt = jnp.exp(st - m_new).astype(BF16)
                acc_new = jnp.exp(m_old - m_new) * acc_old + jnp.dot(v_ref[kv, :, pl.ds(off, keys)], pt, preferred_element_type=F32)
                if q_from:
                    m_new = jnp.concatenate([m[:, :q_from], m_new], axis=1)
                    acc_new = jnp.concatenate([acc[:, :q_from], acc_new], axis=1)
                out += [m_new, acc_new]
            return tuple(out)

        carry = tuple(carry)
        if window is None:
            carry = lax.fori_loop(0, i, functools.partial(step, masked=False), carry)
            if tq % (2 * LANES) == 0:
                half = tq // 2
                carry = step(None, carry, True, off=pl.multiple_of(i * tq, tq), keys=half)
                carry = step(None, carry, True, off=pl.multiple_of(i * tq + half, half), keys=half, q_from=half)
            else:
                carry = step(i, carry, True)
        else:
            assert window % LANES == 0 and tq + window <= S
            carry = step(None, carry, True, off=pl.multiple_of(jnp.maximum(i * tq - window, 0), LANES), keys=tq + window)
        for a in range(hb):
            m, acc = carry[2 * a], carry[2 * a + 1]
            l = acc[dv:dv + 1, :]
            o_ref[a] = acc[:dv, :] / l
            lse_ref[a] = m + jnp.log(l)
        finish_job()

    kv_idx = (lambda b: b) if G == 1 else (lambda b: (b * hb) // G)
    in_specs = [
        pl.BlockSpec((hb, dqk, tq), lambda b, i: (b, 0, i)),
        pl.BlockSpec((kvb, S, dqk), lambda b, i: (kv_idx(b), 0, 0)),
        pl.BlockSpec((kvb, dvp, S), lambda b, i: (kv_idx(b), 0, 0)),
    ]
    args = [qT, k, vT1]
    if sink is not None:
        in_specs += [pl.BlockSpec((hb, 1, LANES), lambda b, i: (b, 0, 0))]
        args += [sink]
    return _job_call(
        job, body, name=name, grid=grid, in_specs=in_specs,
        out_specs=[pl.BlockSpec((hb, dv, tq), lambda b, i: (b, 0, i)), pl.BlockSpec((hb, 1, tq), lambda b, i: (b, 0, i))],
        out_shape=[jax.ShapeDtypeStruct((H, dv, S), F32), jax.ShapeDtypeStruct((H, 1, S), F32)],
        args=args, scratch_shapes=[], aliases={}, dimension_semantics=("arbitrary", "arbitrary") if job is not None else ("parallel", "parallel"))


def _attn_bwd(qT, k, kT, v, oT, doT, lse, *, tile, hb, window=None, sink=None, real=None, extra=False, full=False, name):
    H, dqk, S = qT.shape
    G = H // k.shape[0]
    dv = v.shape[2]
    tq = tk = tile
    nq = S // tq
    has_p = sink is not None
    real = dqk if real is None else real
    main = dqk if full else real
    assert H % hb == 0 and (G == 1 or G % hb == 0) and not (extra and real == dqk)
    kvb = hb if G == 1 else 1

    def body(*refs):
        qT_ref, k_ref, kT_ref, v_ref, oT_ref, doT_ref, lse_ref = refs[:7]
        p_ref = refs[7] if has_p else None
        pos = 8 if has_p else 7
        dq_ref, dk_ref, dv_ref = refs[pos: pos + 3]
        pos += 3
        ds_ref = refs[pos] if has_p else None
        pos += has_p
        dqx_ref, dkx_ref = (refs[pos], refs[pos + 1]) if extra else (None, None)
        delta = refs[-1]
        j = pl.program_id(1)

        @pl.when(j == 0)
        def _():
            dq_ref[...] = jnp.zeros_like(dq_ref)
            if extra:
                dqx_ref[...] = jnp.zeros_like(dqx_ref)
            for a in range(hb):
                drow = jnp.sum(doT_ref[a].astype(F32) * oT_ref[a], axis=0, keepdims=True)
                delta[a] = drow
                if has_p:
                    w = jnp.exp(p_ref[a, :, 0:1] - lse_ref[a])
                    ds_ref[a] = jnp.zeros((1, LANES), F32) - jnp.sum(w * drow, axis=1, keepdims=True)

        def step(i, carry, masked, off=None, qs=tq, keys=tk):
            off = pl.multiple_of(i * tq, tq) if off is None else off
            out = []
            for a in range(hb):
                dk, dvv = carry[2 * a], carry[2 * a + 1]
                kv = a if kvb > 1 else 0
                qTi = qT_ref[a, :, pl.ds(off, qs)]
                doTi = doT_ref[a, :, pl.ds(off, qs)]
                st = jnp.dot(k_ref[kv, pl.ds(0, keys), :], qTi, preferred_element_type=F32)
                if masked:
                    st = _causal_mask(st, off, j * tk, window)
                pt = jnp.exp(st - lse_ref[a, :, pl.ds(off, qs)])
                dv_new = lax.dot_general(pt.astype(BF16), doTi, NT, preferred_element_type=F32)
                dpt = jnp.dot(v_ref[kv, pl.ds(0, keys), :], doTi, preferred_element_type=F32)
                dsb = (pt * (dpt - delta[a, :, pl.ds(off, qs)])).astype(BF16)
                dk_new = lax.dot_general(dsb, qTi, NT, preferred_element_type=F32)
                if keys < tk:
                    dk = jnp.concatenate([dk[:keys] + dk_new, dk[keys:]], axis=0)
                    dvv = jnp.concatenate([dvv[:keys] + dv_new, dvv[keys:]], axis=0)
                else:
                    dk, dvv = dk + dk_new, dvv + dv_new
                dqt = jnp.dot(kT_ref[kv, :, pl.ds(0, keys)], dsb, preferred_element_type=F32)
                dq_ref[a, :, pl.ds(off, qs)] += dqt[:main]
                if extra:
                    dqx_ref[a, :, pl.ds(off, qs)] += dqt[real:]
                out += [dk, dvv]
            return tuple(out)

        carry = (jnp.zeros((tk, dqk), F32), jnp.zeros((tk, dv), F32)) * hb
        if window is None:
            if tk % (2 * LANES) == 0:
                half = tk // 2
                carry = step(None, carry, True, off=pl.multiple_of(j * tk + half, half), qs=half)
                carry = step(None, carry, True, off=pl.multiple_of(j * tk, tk), qs=half, keys=half)
            else:
                carry = step(j, carry, True)
            carry = lax.fori_loop(j + 1, nq, functools.partial(step, masked=False), carry)
        else:
            assert window % LANES == 0 and tk + window <= S
            carry = step(None, carry, True, off=pl.multiple_of(jnp.minimum(j * tk, S - (tk + window)), LANES), qs=tk + window)
        for a in range(hb):
            dk_ref[a] = carry[2 * a][:, :main]
            if extra:
                dkx_ref[a] = carry[2 * a][:, real:]
            dv_ref[a] = carry[2 * a + 1]

    kv_idx = (lambda b: b) if G == 1 else (lambda b: (b * hb) // G)
    colsT = lambda d: pl.BlockSpec((hb, d, S), lambda b, j: (b, 0, 0))
    in_specs = [
        colsT(dqk),
        pl.BlockSpec((kvb, tk, dqk), lambda b, j: (kv_idx(b), j, 0)),
        pl.BlockSpec((kvb, dqk, tk), lambda b, j: (kv_idx(b), 0, j)),
        pl.BlockSpec((kvb, tk, dv), lambda b, j: (kv_idx(b), j, 0)),
        colsT(dv), colsT(dv),
        pl.BlockSpec((hb, 1, S), lambda b, j: (b, 0, 0)),
    ]
    args = [qT, k, kT, v, oT, doT, lse]
    if has_p:
        in_specs += [pl.BlockSpec((hb, 1, LANES), lambda b, j: (b, 0, 0))]
        args += [sink]
    out_specs = [colsT(main), pl.BlockSpec((hb, tk, main), lambda b, j: (b, j, 0)), pl.BlockSpec((hb, tk, dv), lambda b, j: (b, j, 0))]
    out_shape = [jax.ShapeDtypeStruct((H, main, S), F32), jax.ShapeDtypeStruct((H, S, main), F32), jax.ShapeDtypeStruct((H, S, dv), F32)]
    if has_p:
        out_specs += [pl.BlockSpec((hb, 1, LANES), lambda b, j: (b, 0, 0))]
        out_shape += [jax.ShapeDtypeStruct((H, 1, LANES), F32)]
    if extra:
        out_specs += [colsT(dqk - real), pl.BlockSpec((hb, tk, dqk - real), lambda b, j: (b, j, 0))]
        out_shape += [jax.ShapeDtypeStruct((H, dqk - real, S), F32), jax.ShapeDtypeStruct((H, S, dqk - real), F32)]
    return pl.pallas_call(
        body, name=name, grid=(H // hb, S // tk), in_specs=in_specs, out_specs=out_specs, out_shape=out_shape,
        scratch_shapes=[pltpu.VMEM((hb, 1, S), F32)],
        compiler_params=pltpu.CompilerParams(dimension_semantics=("parallel", "arbitrary")),
    )(*args)


def _rows_and_cols(x3):
    xb = x3.astype(BF16)
    return jnp.transpose(xb, (1, 0, 2)), jnp.transpose(xb, (1, 2, 0))


def _cols_only(x3):
    return jnp.transpose(x3.astype(BF16), (1, 2, 0))


def _v_with_ones(v3):
    S, h, _ = v3.shape
    vT = jnp.transpose(v3.astype(BF16), (1, 2, 0))
    return jnp.concatenate([vT, jnp.ones((h, 1, S), BF16), jnp.zeros((h, 15, S), BF16)], axis=1)


def _from_T(oT):
    h, d, S = oT.shape
    return jnp.transpose(oT, (2, 0, 1)).reshape(S, h * d)


def _coords():
    return lax.axis_index("x"), lax.axis_index("y"), lax.axis_index("c")


def _peer(axis):
    x, y, c = _coords()
    return {"x": (1 - x, y, c), "y": (x, 1 - y, c), "c": (x, y, 1 - c)}[axis]


def _gather_job(bufs, rows=None):
    n = len(bufs)

    def copies(outs, send_sems, recv_sems):
        x, y, c = _coords()
        me, sibling = (x, y, c), (x, y, 1 - c)
        chips = [(1 - x, y), (x, 1 - y), (1 - x, 1 - y)]

        def copy(t, k, block, to):
            px, py, pc = block
            ref = outs[t].at[4 * px + 2 * py + pc]
            if rows is not None and rows[t] is not None:
                ref = ref.at[pl.ds(rows[t][0], rows[t][1])]
            return pltpu.make_async_remote_copy(ref, ref, send_sems.at[7 * t + k], recv_sems.at[7 * t + k], device_id=to, device_id_type=MESH)

        return copy, me, sibling, chips, c

    def start(ins, outs, send_sems, recv_sems):
        copy, me, sibling, chips, c = copies(outs, send_sems, recv_sems)
        for t in range(n):
            copy(t, 0, me, sibling).start()
            for j, chip in enumerate(chips):
                copy(t, 1 + j, me, (*chip, c)).start()

    def finish(ins, outs, send_sems, recv_sems):
        copy, me, sibling, chips, c = copies(outs, send_sems, recv_sems)
        for j, chip in enumerate(chips):
            for t in range(n):
                copy(t, 1 + j, (*chip, c), me).wait_recv()
                copy(t, 4 + j, (*chip, c), sibling).start()
        for t in range(n):
            copy(t, 0, sibling, me).wait_recv()
            for j, chip in enumerate(chips):
                copy(t, 4 + j, (*chip, 1 - c), me).wait_recv()
        for t in range(n):
            copy(t, 0, me, sibling).wait_send()
            for j, chip in enumerate(chips):
                copy(t, 1 + j, me, (*chip, c)).wait_send()
                copy(t, 4 + j, (*chip, c), sibling).wait_send()

    return dict(ins=list(bufs), outs=[jax.ShapeDtypeStruct(b.shape, b.dtype) for b in bufs], aliases={t: t for t in range(n)},
                n_sems=7 * n, start=start, finish=finish)


def _in_slot(local):
    x, y, c = _coords()
    buf = lax.empty((N_DEV,) + local.shape, local.dtype)
    return lax.dynamic_update_slice(buf, local[None], (4 * x + 2 * y + c, 0, 0))


def _pair_job(vs, axes):
    n = len(vs)
    axes = [axes] * n if isinstance(axes, str) else axes

    def copies(ins, outs, send_sems, recv_sems):
        out = []
        for t in range(n):
            me = lax.axis_index(axes[t])
            src = ins[t].at[1 - me] if len(ins[t].shape) == 3 else ins[t].at[:, 1 - me]
            out.append(pltpu.make_async_remote_copy(src, outs[t], send_sems.at[t], recv_sems.at[t], device_id=_peer(axes[t]), device_id_type=MESH))
        return out

    def start(*refs):
        for cp in copies(*refs):
            cp.start()

    def finish(*refs):
        for cp in copies(*refs):
            cp.wait()

    return dict(ins=list(vs), outs=[jax.ShapeDtypeStruct(v.shape[:-3] + v.shape[-2:], v.dtype) for v in vs], aliases={}, n_sems=n,
                start=start, finish=finish)


def _add_kept(v, got, axis, out, name):
    R, C = v.shape[-2:]
    lead = v.shape[0] if v.ndim == 4 else 1
    tm = _divisor(R, max(16, EW_TILE_BYTES // (_lanes(C) * (v.dtype.itemsize + got.dtype.itemsize + jnp.dtype(out).itemsize)) // 16 * 16), 16)
    me = lax.axis_index(axis).astype(jnp.int32).reshape(1)
    v4 = v.reshape(lead, 2, R, C)
    g3 = got.reshape(lead, R, C)

    def body(me_ref, v_ref, g_ref, o_ref):
        o_ref[...] = (v_ref[0].astype(F32) + g_ref[...].astype(F32)).astype(o_ref.dtype)

    res = pl.pallas_call(
        body, name=name, out_shape=jax.ShapeDtypeStruct((lead, R, C), out),
        grid_spec=pltpu.PrefetchScalarGridSpec(
            num_scalar_prefetch=1, grid=(lead, R // tm),
            in_specs=[pl.BlockSpec((1, 1, tm, C), lambda b, i, me: (b, me[0], i, 0)), pl.BlockSpec((1, tm, C), lambda b, i, me: (b, i, 0))],
            out_specs=pl.BlockSpec((1, tm, C), lambda b, i, me: (b, i, 0))),
    )(me, v4, g3)
    return res


def _cross_job(vs):
    n = len(vs)

    def copies(ins, outs, send_sems, recv_sems):
        x, y, _ = _coords()
        out = []
        for t in range(n):
            h = ins[t].shape[2] // 2
            out.append(pltpu.make_async_remote_copy(ins[t].at[1 - x, :, pl.ds(0, h)], outs[2 * t], send_sems.at[2 * t], recv_sems.at[2 * t],
                                                    device_id=_peer("x"), device_id_type=MESH))
            out.append(pltpu.make_async_remote_copy(ins[t].at[:, 1 - y, pl.ds(h, h)], outs[2 * t + 1], send_sems.at[2 * t + 1], recv_sems.at[2 * t + 1],
                                                    device_id=_peer("y"), device_id_type=MESH))
        return out

    def start(*refs):
        for cp in copies(*refs):
            cp.start()

    def finish(*refs):
        for cp in copies(*refs):
            cp.wait()

    outs = []
    for v in vs:
        outs += [jax.ShapeDtypeStruct((2, v.shape[2] // 2, v.shape[3]), v.dtype)] * 2
    return dict(ins=list(vs), outs=outs, aliases={}, n_sems=2 * n, start=start, finish=finish)


def _add_picked(v, got, axis, out, name):
    _, _, R, C = v.shape
    h = R // 2
    tm = _divisor(h, max(16, EW_TILE_BYTES // (_lanes(C) * (v.dtype.itemsize + got.dtype.itemsize + jnp.dtype(out).itemsize)) // 16 * 16), 16)
    me = lax.axis_index(axis).astype(jnp.int32).reshape(1)
    if axis == "x":
        v_map = lambda b, i, me: (me[0], b, i, 0)
    else:
        v_map = lambda b, i, me: (b, me[0], i + h // tm, 0)

    def body(me_ref, v_ref, g_ref, o_ref):
        o_ref[...] = (v_ref[0].astype(F32) + g_ref[...].astype(F32)).astype(o_ref.dtype)

    return pl.pallas_call(
        body, name=name, out_shape=jax.ShapeDtypeStruct((2, h, C), out),
        grid_spec=pltpu.PrefetchScalarGridSpec(
            num_scalar_prefetch=1, grid=(2, h // tm),
            in_specs=[pl.BlockSpec((1, 1, tm, C), v_map), pl.BlockSpec((1, tm, C), lambda b, i, me: (b, i, 0))],
            out_specs=pl.BlockSpec((1, tm, C), lambda b, i, me: (b, i, 0))),
    )(me, v, got)


def _reduce_scatter_steps(gs, tag):
    n = len(gs)
    vs = [g.reshape(4, 2, *g.shape[1:]) for g in gs]
    got = yield _pair_job(vs, "c")
    vs = [_add_kept(v, r, "c", BF16, f"rs_{tag}_add_c{t}") for t, (v, r) in enumerate(zip(vs, got))]
    vs = [v.reshape(2, 2, v.shape[1], v.shape[2]) for v in vs]
    got = yield _cross_job(vs)
    up = [_add_picked(v, r, "x", BF16, f"rs_{tag}_add_x{t}") for t, (v, r) in enumerate(zip(vs, got[0::2]))]
    lo = [_add_picked(v, r, "y", BF16, f"rs_{tag}_add_y{t}") for t, (v, r) in enumerate(zip(vs, got[1::2]))]
    got = yield _pair_job(up + lo, ["y"] * n + ["x"] * n)
    out = []
    for t in range(n):
        a = _add_kept(up[t], got[t], "y", F32, f"rs_{tag}_add_y2{t}")[0]
        b = _add_kept(lo[t], got[n + t], "x", F32, f"rs_{tag}_add_x2{t}")[0]
        out.append(jnp.concatenate([a, b], axis=0))
    return out


def _reduce_scatter(gs, tag):
    steps = _reduce_scatter_steps(gs, tag)
    job = next(steps)
    for stage in ("c", "xy", "yx"):
        got = _comm_call(job, f"rs_{tag}_{stage}")
        try:
            job = steps.send(got)
        except StopIteration as done:
            return done.value


def _all_reduce_small(v):
    def body(v_ref, o_ref, buf, send_sems, recv_sems):
        x, y, c = _coords()
        me = 4 * x + 2 * y + c
        buf[me] = v_ref[...]
        copies = []
        for k in range(1, N_DEV):
            peer = tuple((1 - a) if (k >> s) & 1 else a for a, s in ((x, 2), (y, 1), (c, 0)))
            cp = pltpu.make_async_remote_copy(v_ref, buf.at[me], send_sems.at[k - 1], recv_sems.at[k - 1], device_id=peer, device_id_type=MESH)
            cp.start()
            copies.append(cp)
        for cp in copies:
            cp.wait()
        acc = buf[0]
        for d in range(1, N_DEV):
            acc = acc + buf[d]
        o_ref[...] = acc

    vm = pl.BlockSpec(memory_space=pltpu.VMEM)
    return pl.pallas_call(
        body, name="all_reduce_small", in_specs=[vm], out_specs=vm, out_shape=jax.ShapeDtypeStruct(v.shape, F32),
        scratch_shapes=[pltpu.VMEM((N_DEV,) + v.shape, F32), pltpu.SemaphoreType.DMA((N_DEV - 1,)), pltpu.SemaphoreType.DMA((N_DEV - 1,))],
    )(v)


def _local_groups(w, dtype):
    mix_out = [w["ev_w_out"][0], w["od_w_out"][0]]
    layers = []
    for l in range(DEPTH):
        a = jnp.concatenate([w["ffa_w_down"][l], w["ffb_w_down"][l]], axis=0).astype(dtype)
        b = jnp.concatenate([w["ple_w_gate"][l], mix_out[l]], axis=0).astype(dtype)
        c = jnp.concatenate([w["ffa_w_gate_up"][l], w["ffb_w_gate_up"][l]], axis=0).astype(dtype)
        layers.append((a, b, c))
    strip = jnp.concatenate([w["ple_w_proj"].reshape(-1, STRIP_C), w["ev_w_ukv"][0], jnp.pad(w["ev_w_uq"][0], ((0, 0), (0, STRIP_C - 96))),
                             jnp.zeros((G3_ROWS - 896, STRIP_C), F32)], axis=0)
    m = jnp.concatenate([w["od_w_in"][0], w["ev_w_in"][0], strip, jnp.zeros((G3_ROWS, G3_COLS - STRIP0 - STRIP_C), F32)], axis=1).astype(dtype)
    return layers, m


def _ungroup_local(a, b, c, r3):
    out = {
        "ffa_w_down": jnp.stack([x[0] for x in a]), "ffb_w_down": jnp.stack([x[1] for x in a]),
        "ple_w_gate": jnp.stack([x[:128] for x in b]), "ev_w_out": b[0][128:][None], "od_w_out": b[1][128:][None],
        "ffa_w_gate_up": jnp.stack([x[0] for x in c]), "ffb_w_gate_up": jnp.stack([x[1] for x in c]),
        "od_w_in": r3[:, :OD_C][None], "ev_w_in": r3[:, OD_C:STRIP0][None],
    }
    strip = r3[:, STRIP0:STRIP0 + STRIP_C]
    out["ple_w_proj"] = strip[:512].reshape(2, PLE_DIM, STRIP_C)
    out["ev_w_ukv"] = strip[512:640][None]
    out["ev_w_uq"] = strip[640:896, :96][None]
    return out


def _cols(a):
    return jnp.transpose(a, (1, 0, 2)).reshape(a.shape[1], -1)


def _blocks(g, c):
    return jnp.transpose(g.reshape(g.shape[0], N_DEV, c), (1, 0, 2))


def _uq_permute(w):
    r = w.shape[0]
    w3 = w.reshape(r, B_HEADS, B_NOPE + B_ROPE)
    half = B_ROPE // 2
    return jnp.concatenate([w3[:, :, :B_NOPE].reshape(r, -1), w3[:, :, B_NOPE:B_NOPE + half].reshape(r, -1), w3[:, :, B_NOPE + half:].reshape(r, -1)], axis=1)


def _uq_unpermute(g):
    r = g.shape[0]
    half = B_ROPE // 2
    n = B_HEADS * B_NOPE
    parts = [g[:, :n].reshape(r, B_HEADS, B_NOPE), g[:, n:n + B_HEADS * half].reshape(r, B_HEADS, half), g[:, n + B_HEADS * half:].reshape(r, B_HEADS, half)]
    return jnp.concatenate(parts, axis=2).reshape(r, -1)


def _ukv_permute(w):
    r = w.shape[0]
    return jnp.transpose(w.reshape(r, B_HEADS, 2, B_NOPE), (0, 2, 1, 3)).reshape(r, -1)


def _ukv_unpermute(g):
    r = g.shape[0]
    return jnp.transpose(g.reshape(r, 2, B_HEADS, B_NOPE), (0, 2, 1, 3)).reshape(r, -1)


def _od_in_widen(w):
    n = C_HEADS * C_HEAD_DIM
    wide = lambda m: jnp.pad(m.reshape(-1, C_HEADS, C_HEAD_DIM), ((0, 0), (0, 0), (0, QK_PAD - C_HEAD_DIM))).reshape(m.shape[0], -1)
    return jnp.concatenate([wide(w[:, :n] * C_HEAD_DIM ** -0.5), wide(w[:, n:2 * n]), w[:, 2 * n:],
                            jnp.zeros((w.shape[0], ODD_IN_PAD - ODD_IN_AUG), w.dtype)], axis=1)


def _od_in_narrow(g):
    wp = C_HEADS * QK_PAD
    narrow = lambda m: m.reshape(-1, C_HEADS, QK_PAD)[:, :, :C_HEAD_DIM].reshape(m.shape[0], -1)
    return jnp.concatenate([narrow(g[:, :wp]) * C_HEAD_DIM ** -0.5, narrow(g[:, wp:2 * wp]), g[:, 2 * wp:ODD_IN_AUG]], axis=1)


def _misc_weights(G3):
    strip = G3[:, :, STRIP0:STRIP0 + STRIP_C]
    return {
        "od_w_in": _od_in_widen(_cols(G3[:, :, :OD_C])),
        "ev_w_in": jnp.pad(_cols(G3[:, :, OD_C:STRIP0]), ((0, 0), (0, EVEN_IN_PAD - EVEN_IN))),
        "ple_w_proj": [_cols(strip[:, i * PLE_DIM:(i + 1) * PLE_DIM]) for i in range(DEPTH)],
        "ev_w_ukv": _ukv_permute(_cols(strip[:, 512:640])),
        "ev_w_uq": _uq_permute(_cols(strip[:, 640:896, :96])),
    }


def _misc_grads(G):
    strip = jnp.concatenate([
        _blocks(G["ple_w_proj"][0], STRIP_C), _blocks(G["ple_w_proj"][1], STRIP_C), _blocks(_ukv_unpermute(G["ev_w_ukv"]), STRIP_C),
        jnp.pad(_blocks(_uq_unpermute(G["ev_w_uq"]), 96), ((0, 0), (0, 0), (0, STRIP_C - 96))),
        jnp.zeros((N_DEV, G3_ROWS - 896, STRIP_C), F32)], axis=1)
    return jnp.concatenate([_blocks(_od_in_narrow(G["od_w_in"]), OD_C), _blocks(G["ev_w_in"][:, :EVEN_IN], EV_C), strip,
                            jnp.zeros((N_DEV, G3_ROWS, G3_COLS - STRIP0 - STRIP_C), F32)], axis=2)


def _ffn_fwd(h, norm_w, W, f, i, tag, ride=None):
    job = ride() if ride else None
    res = _ffn_gate_up(h, norm_w, W["C"][i].reshape(2, 4, C_ROWS, FF_BLK), f, f"{tag}_gate_up", job=job)
    n, gu, act = res[:3]
    if job is not None:
        ride(res[3:])
    job = ride() if ride else None
    out = _ffn_down(act, W["A"][i], f, h, f"{tag}_down", job=job)
    if job is not None:
        out, got = out
        ride(got)
    return out, (h, n, gu, act)


def _ffn_bwd(dout, saved, norm_w, W, GB, f, i, tag, ride=None):
    h, n, gu, act = saved
    S = h.shape[0]
    def carried(call):
        job = ride() if ride else None
        res = call(job)
        if job is None:
            return res
        ride(res[1])
        return res[0]

    GB["A"][i][f] = carried(lambda job: _ffn_down_dw(act, dout, f"{tag}_down_dw", job=job))
    dgu = _ffn_down_dx(dout, W["A"][i], f, gu, f"{tag}_down_dx").reshape(N_DEV, S, FF_BLK)
    res = carried(lambda job: _ffn_gate_up_dx(dgu, W["C"][i], f, h, norm_w, dout, f"{tag}_gate_up_dx", job=job))
    GB["C"][i][f] = carried(lambda job: _ffn_gate_up_dw(n, dgu, f"{tag}_gate_up_dw", job=job))
    return res


def _rope_tables(S):
    inv = ROPE_THETA ** (-jnp.arange(0, B_ROPE, 2, dtype=F32) / B_ROPE)
    ang = jnp.arange(S, dtype=F32)[:, None] * inv[None, :]
    return jnp.cos(ang), jnp.sin(ang)


def _alibi_columns(S):
    t = jnp.arange(S, dtype=jnp.int32)
    hi = ((t // 16) * 16).astype(F32)
    lo = (t % 16).astype(F32)
    slopes = 2.0 ** (-8.0 * jnp.arange(1, A_HEADS + 1, dtype=F32) / A_HEADS)
    zq = jnp.zeros((S, A_HEADS), F32)
    rest = QK_PAD - A_HEAD_DIM - 4
    qc = jnp.stack([-slopes[None, :] * hi[:, None], -slopes[None, :] * lo[:, None], zq + slopes[None, :], zq + slopes[None, :]] + [zq] * rest, axis=-1)
    one = jnp.ones((S, A_KV_HEADS), F32)
    zk = jnp.zeros((S, A_KV_HEADS), F32)
    kc = jnp.stack([one, one, zk + hi[:, None], zk + lo[:, None]] + [zk] * rest, axis=-1)
    return qc, kc


def _sink_prm(sinks):
    return jnp.zeros((A_HEADS, 1, LANES), F32).at[:, 0, 0].set(sinks.astype(F32))


def _with_ride(ride, call):
    job = ride() if ride else None
    res = call(job)
    if job is None:
        return res
    n_own = len(res) - len(job["outs"])
    ride(res[n_own:])
    return res[:n_own]


def _even_fwd(hn, h, W, ride=None):
    S = hn.shape[0]
    proj = _mm(hn, W["ev_w_in"], name="ev_in")
    a_q, a_k, a_v = proj[:, :512], proj[:, 512:640], proj[:, 640:768]
    c_q, c_kv = proj[:, 768:1024], proj[:, 1024:1152]
    kr1, kr2 = proj[:, 1152:1168], proj[:, 1168:1184]
    qc, kc = _alibi_columns(S)
    qaT = _cols_only(jnp.concatenate([(a_q * A_HEAD_DIM ** -0.5).reshape(S, A_HEADS, A_HEAD_DIM), qc], axis=-1))
    ka, kaT = _rows_and_cols(jnp.concatenate([a_k.reshape(S, A_KV_HEADS, A_HEAD_DIM), kc], axis=-1))
    va3 = a_v.reshape(S, A_KV_HEADS, A_HEAD_DIM)
    va = jnp.transpose(va3.astype(BF16), (1, 0, 2))
    prm = _sink_prm(W["ev_sinks"][0])
    oaT, lse_a = _with_ride(ride, lambda job: _attn_fwd(qaT, ka, _v_with_ones(va3), tile=min(SWA_TILE, S // 2), hb=A_GROUP, window=WINDOW, sink=prm,
                                                        name="swa_fwd", job=job))
    cqn = _rms_fwd(c_q, W["ev_cq_norm"], "ev_cq_norm")
    q_all = _mm(cqn, W["ev_w_uq"], name="ev_uq")
    ckvn = _rms_fwd(c_kv, W["ev_ckv_norm"], "ev_ckv_norm")
    kv_all = _mm(ckvn, W["ev_w_ukv"], name="ev_ukv")
    cos, sin = _rope_tables(S)
    cos8, sin8 = jnp.tile(cos, (1, B_HEADS)), jnp.tile(sin, (1, B_HEADS))
    q1, q2 = _rope(q_all[:, 512:640], q_all[:, 640:768], cos8, sin8, "ev_rope_q")
    k1, k2 = _rope(kr1, kr2, cos, sin, "ev_rope_k")
    half = B_ROPE // 2
    scale = (B_NOPE + B_ROPE) ** -0.5
    qbT = _cols_only(jnp.concatenate([q_all[:, :512].reshape(S, B_HEADS, B_NOPE), q1.reshape(S, B_HEADS, half), q2.reshape(S, B_HEADS, half)], axis=-1) * scale)
    kro = jnp.broadcast_to(jnp.concatenate([k1, k2], axis=1)[:, None, :], (S, B_HEADS, B_ROPE))
    kb, kbT = _rows_and_cols(jnp.concatenate([kv_all[:, :512].reshape(S, B_HEADS, B_NOPE), kro], axis=-1))
    vb3 = kv_all[:, 512:].reshape(S, B_HEADS, B_V)
    vb = jnp.transpose(vb3.astype(BF16), (1, 0, 2))
    obT, lse_b = _with_ride(ride, lambda job: _attn_fwd(qbT, kb, _v_with_ones(vb3), tile=min(ATTN_TILE_FWD, S), hb=2, name="mla_fwd", job=job))
    cat = jnp.concatenate([_from_T(oaT), _from_T(obT)], axis=1)
    out = _mm_w128(cat, W["B"][0], MIX_OUT_BLK, res=h, name="ev_out")
    return out, (hn, proj, (qaT, ka, kaT, va, oaT, lse_a), prm, cqn, ckvn, (qbT, kb, kbT, vb, obT, lse_b), cat)


def _even_bwd(dout, saved, W, GB, norm):
    hn, proj, (qaT, ka, kaT, va, oaT, lse_a), prm, cqn, ckvn, (qbT, kb, kbT, vb, obT, lse_b), cat = saved
    S = hn.shape[0]
    G = {}
    dcat = _mm_w128(dout, W["B"][0], MIX_OUT_BLK, tb=True, out=BF16, name="ev_out_dx")
    GB["B"][0] = _mm_w128_dw(cat, dout, MIX_OUT_BLK, GB["B"][0], "ev_out_dw")
    doaT = _cols_only(dcat[:, :512].reshape(S, A_HEADS, A_HEAD_DIM))
    dqaT, dka, dva, dsink = _attn_bwd(qaT, ka, kaT, va, oaT, doaT, lse_a, tile=min(SWA_TILE, S // 2), hb=A_GROUP, window=WINDOW, sink=prm, real=A_HEAD_DIM,
                                       name="swa_bwd")
    G["ev_sinks"] = dsink[:, 0, 0]
    dqa = _from_T(dqaT) * A_HEAD_DIM ** -0.5
    dka = dka.reshape(A_KV_HEADS, A_GROUP, S, A_HEAD_DIM).sum(axis=1)
    dva = dva.reshape(A_KV_HEADS, A_GROUP, S, A_HEAD_DIM).sum(axis=1)
    dobT = _cols_only(dcat[:, 512:].reshape(S, B_HEADS, B_V))
    dqbT, dkb, dvb = _attn_bwd(qbT, kb, kbT, vb, obT, dobT, lse_b, tile=min(ATTN_TILE, S), hb=1, name="mla_bwd")
    half = B_ROPE // 2
    dqb = jnp.transpose(dqbT, (2, 0, 1)) * (B_NOPE + B_ROPE) ** -0.5
    dkb = jnp.transpose(dkb, (1, 0, 2))
    cos, sin = _rope_tables(S)
    cos8, sin8 = jnp.tile(cos, (1, B_HEADS)), jnp.tile(sin, (1, B_HEADS))
    dq1, dq2 = _rope(dqb[:, :, B_NOPE:B_NOPE + half].reshape(S, -1), dqb[:, :, B_NOPE + half:].reshape(S, -1), cos8, -sin8, "ev_rope_q_bwd")
    dq_all = jnp.concatenate([dqb[:, :, :B_NOPE].reshape(S, -1), dq1, dq2], axis=1).astype(BF16)
    dkr = dkb[:, :, B_NOPE:].sum(axis=1)
    dk1, dk2 = _rope(dkr[:, :half], dkr[:, half:], cos, -sin, "ev_rope_k_bwd")
    dkv_all = jnp.concatenate([dkb[:, :, :B_NOPE].reshape(S, -1), _unheads(dvb)], axis=1).astype(BF16)
    G["ev_w_uq"] = _mm(cqn, dq_all, ta=True, name="ev_uq_dw")
    dcqn = _mm(dq_all, W["ev_w_uq"], tb=True, name="ev_uq_dx")
    dc_q, G["ev_cq_norm"] = _rms_bwd(dcqn, proj[:, 768:1024], W["ev_cq_norm"], None, "ev_cq_norm_bwd")
    G["ev_w_ukv"] = _mm(ckvn, dkv_all, ta=True, name="ev_ukv_dw")
    dckvn = _mm(dkv_all, W["ev_w_ukv"], tb=True, name="ev_ukv_dx")
    dc_kv, G["ev_ckv_norm"] = _rms_bwd(dckvn, proj[:, 1024:1152], W["ev_ckv_norm"], None, "ev_ckv_norm_bwd")
    dproj = jnp.concatenate([dqa, _unheads(dka), _unheads(dva), dc_q, dc_kv, dk1, dk2,
                             jnp.zeros((S, EVEN_IN_PAD - EVEN_IN), F32)], axis=1).astype(BF16)
    G["ev_w_in"] = _mm(hn, dproj, ta=True, name="ev_in_dw")
    dh, dnorm = _mm(dproj, W["ev_w_in"], tb=True, norm_bwd=(*norm, dout), name="ev_in_dx")
    return dh, dnorm, G


def _odd_fwd(hn, h, W, ride=None):
    S = hn.shape[0]
    w = C_HEADS * C_HEAD_DIM
    wp = C_HEADS * QK_PAD
    proj = _mm(hn, W["od_w_in"], name="od_in")
    f_logit = proj[:, 2 * wp + w: 2 * wp + w + C_HEADS]
    logf = _logsig_fwd(f_logit, W["od_b_f"], "od_logsig")
    logc = _cumsum(logf, False, "od_cumsum")
    parts = list(_exact3(logc))
    ones = [jnp.ones((S, C_HEADS), F32)] * 3
    pad = [jnp.zeros((S, C_HEADS), F32)] * (QK_PAD - C_HEAD_DIM - 6)
    lead = ((0, 0), (0, 0), (C_HEAD_DIM, 0))
    q3 = proj[:, :wp].reshape(S, C_HEADS, QK_PAD) + jnp.pad(jnp.stack(parts + ones + pad, axis=-1), lead)
    k3 = proj[:, wp:2 * wp].reshape(S, C_HEADS, QK_PAD) + jnp.pad(jnp.stack(ones + [-p for p in parts] + pad, axis=-1), lead)
    qT = _cols_only(q3)
    k, kT = _rows_and_cols(k3)
    v3 = proj[:, 2 * wp:2 * wp + w].reshape(S, C_HEADS, C_HEAD_DIM)
    v = jnp.transpose(v3.astype(BF16), (1, 0, 2))
    oT, lse = _with_ride(ride, lambda job: _attn_fwd(qT, k, _v_with_ones(v3), tile=min(ATTN_TILE_FWD, S), hb=2, name="fox_fwd", job=job))
    cat = _from_T(oT)
    out = _mm_w128(cat, W["B"][1], MIX_OUT_BLK, res=h, name="od_out")
    return out, (hn, qT, k, kT, v, f_logit, oT, lse, cat)


def _odd_bwd(dout, saved, W, GB, norm):
    hn, qT, k, kT, v, f_logit, oT, lse, cat = saved
    S = hn.shape[0]
    G = {}
    dcat = _mm_w128(dout, W["B"][1], MIX_OUT_BLK, tb=True, out=BF16, name="od_out_dx")
    GB["B"][1] = _mm_w128_dw(cat, dout, MIX_OUT_BLK, GB["B"][1], "od_out_dw")
    doT = _cols_only(dcat.reshape(S, C_HEADS, C_HEAD_DIM))
    dqT, dk, dv, dqxT, dkx = _attn_bwd(qT, k, kT, v, oT, doT, lse, tile=min(ATTN_TILE, S), hb=1, real=C_HEAD_DIM, extra=True,
                                       full=True, name="fox_bwd")
    dlogc = jnp.transpose(dqxT[:, 0, :] - dkx[:, :, 3])
    dlogf = _cumsum(dlogc, True, "od_cumsum_bwd")
    df, db = _logsig_bwd(dlogf, f_logit, W["od_b_f"], "od_logsig_bwd")
    G["od_b_f"] = db
    dproj = jnp.concatenate([_from_T(dqT), _unheads(dk), _unheads(dv), df, jnp.zeros((S, ODD_IN_PAD - ODD_IN_AUG), F32)], axis=1).astype(BF16)
    G["od_w_in"] = _mm(hn, dproj, ta=True, name="od_in_dw")
    dh, dnorm = _mm(dproj, W["od_w_in"], tb=True, norm_bwd=(*norm, dout), name="od_in_dx")
    return dh, dnorm, G


class _Rider:
    def __init__(self, steps, tag):
        self.steps, self.tag, self.count, self.result = steps, tag, 0, None
        self.job = next(steps)

    def __call__(self, got=None):
        if got is not None:
            return self._advance(list(got))
        job = self.job
        if isinstance(job, str):
            self._advance(None)
            return None
        return job

    def _advance(self, value):
        try:
            self.job = self.steps.send(value)
        except StopIteration as done:
            self.job, self.result = None, done.value

    def finish(self):
        while self.job is not None:
            if isinstance(self.job, str):
                self._advance(None)
                continue
            self.count += 1
            self(_comm_call(self.job, f"{self.tag}_{self.count}"))
        return self.result


def _gather_plan(W, slots):
    a0, b0, c0, m, a1, b1, c1 = (slots[key] for key in ("a0", "b0", "c0", "m", "a1", "b1", "c1"))
    (m,) = yield _gather_job([m])
    W.update(_misc_weights(m))
    (b0,) = yield _gather_job([b0])
    W["B"] = [b0]
    (c0,) = yield _gather_job([c0], rows=[(D_MODEL, D_MODEL)])
    W["C"] = [c0]
    a0, c1 = yield _gather_job([a0, c1], rows=[(DOWN_ROWS, DOWN_ROWS), (0, D_MODEL)])
    W["A"] = [a0]
    W["C"].append(c1)
    (a1,) = yield _gather_job([a1], rows=[(0, DOWN_ROWS)])
    W["A"].append(a1)
    for _ in range(3):
        yield "skip"
    a1, b1, c1 = yield _gather_job([a1, b1, c1], rows=[(DOWN_ROWS, DOWN_ROWS), None, (D_MODEL, D_MODEL)])
    W["A"][1], W["C"][1] = a1, c1
    W["B"].append(b1)


def _local_step(x, p, target, W, slots):
    h = x
    saved = []
    gather = _Rider(_gather_plan(W, slots), "all_gather_rest")
    for i in range(DEPTH):
        t = f"l{i}"
        h1, s_a = _ffn_fwd(h, W["ffa_norm"][i:i + 1], W, 0, i, f"{t}_ffa", gather)
        nm = _rms_fwd(h1, W["mix_norm"][i:i + 1], f"{t}_mix_norm")
        h2, s_m = (_even_fwd if i % 2 == 0 else _odd_fwd)(nm, h1, W, gather)
        h3, s_b = _ffn_fwd(h2, W["ffb_norm"][i:i + 1], W, 1, i, f"{t}_ffb", gather)
        npl = _rms_fwd(h3, W["ple_norm"][i:i + 1], f"{t}_ple_norm")
        gpre = _mm_w128(npl, W["B"][i], PLE_GATE_BLK, out=BF16, name=f"{t}_ple_gate")
        pp = _mm(p[i], W["ple_w_proj"][i], out=BF16, name=f"{t}_ple_proj")
        h4 = _ple_fwd(h3, gpre, pp, f"{t}_ple")
        saved.append((s_a, h1, s_m, s_b, h3, npl, gpre, pp))
        h = h4
    gather.finish()
    dh, g_final, loss_cols = _final_fwd_bwd(h, W["final_norm"], target, "final")
    G = {"final_norm": g_final}
    GB = {"A": [[None, None] for _ in range(DEPTH)], "C": [[None, None] for _ in range(DEPTH)],
          "B": [lax.empty((N_DEV, B_ROWS, D_MODEL), BF16) for _ in range(DEPTH)]}
    per_layer = {n: [None] * DEPTH for n in ("ffa_norm", "mix_norm", "ffb_norm", "ple_norm", "ple_w_proj")}
    scatter = scatter_mid = None
    for i in reversed(range(DEPTH)):
        t = f"l{i}"
        s_a, h1, s_m, s_b, h3, npl, gpre, pp = saved[i]
        dgpre, dpp = _ple_bwd(dh, gpre, pp, f"{t}_ple_bwd")
        per_layer["ple_w_proj"][i] = _mm(p[i], dpp, ta=True, name=f"{t}_ple_proj_dw")
        GB["B"][i] = _mm_w128_dw(npl, dgpre, PLE_GATE_BLK, GB["B"][i], f"{t}_ple_gate_dw")
        dh, per_layer["ple_norm"][i] = _mm_w128(dgpre, W["B"][i], PLE_GATE_BLK, tb=True, norm_bwd=(h3, W["ple_norm"][i:i + 1], dh),
                                                name=f"{t}_ple_gate_dx")
        dh, per_layer["ffb_norm"][i] = _ffn_bwd(dh, s_b, W["ffb_norm"][i:i + 1], W, GB, 1, i, f"{t}_ffb", scatter)
        dh, per_layer["mix_norm"][i], g_mix = (_even_bwd if i % 2 == 0 else _odd_bwd)(dh, s_m, W, GB, (h1, W["mix_norm"][i:i + 1]))
        G.update(g_mix)
        if i == 0:
            G["ple_w_proj"] = per_layer["ple_w_proj"]
            mid = [GB["A"][0][1], GB["C"][0][1], GB["B"][0], _misc_grads(G).astype(BF16)]
            scatter_mid = _Rider(_reduce_scatter_steps(mid, "mid"), "rs_mid")
        else:
            scatter_early = _Rider(_reduce_scatter_steps([GB["A"][i][1], GB["C"][i][1], GB["B"][i]], "early"), "rs_early")
        dh, per_layer["ffa_norm"][i] = _ffn_bwd(dh, s_a, W["ffa_norm"][i:i + 1], W, GB, 0, i, f"{t}_ffa", scatter_mid if i == 0 else scatter_early)
        if i == DEPTH - 1:
            scatter = _Rider(_reduce_scatter_steps([GB["A"][i][0], GB["C"][i][0]], "later"), "rs_later")
    for n in ("ffa_norm", "mix_norm", "ffb_norm", "ple_norm"):
        G[n] = jnp.concatenate(per_layer[n], axis=0)
    return loss_cols, dh, scatter_early.finish(), scatter.finish(), scatter_mid.finish(), [GB["A"][0][0], GB["C"][0][0]], G


def kernel(x, p, ffa_norm, ffa_w_gate_up, ffa_w_down, mix_norm, ffb_norm, ffb_w_gate_up, ffb_w_down, ple_norm, ple_w_gate, ple_w_proj, ev_w_in, ev_sinks, ev_cq_norm, ev_w_uq, ev_ckv_norm, ev_w_ukv, ev_w_out, od_w_in, od_b_f, od_w_out, final_norm, loss_target, m_ffa_norm, m_ffa_w_gate_up, m_ffa_w_down, m_mix_norm, m_ffb_norm, m_ffb_w_gate_up, m_ffb_w_down, m_ple_norm, m_ple_w_gate, m_ple_w_proj, m_ev_w_in, m_ev_sinks, m_ev_cq_norm, m_ev_w_uq, m_ev_ckv_norm, m_ev_w_ukv, m_ev_w_out, m_od_w_in, m_od_b_f, m_od_w_out, m_final_norm, v_ffa_norm, v_ffa_w_gate_up, v_ffa_w_down, v_mix_norm, v_ffb_norm, v_ffb_w_gate_up, v_ffb_w_down, v_ple_norm, v_ple_w_gate, v_ple_w_proj, v_ev_w_in, v_ev_sinks, v_ev_cq_norm, v_ev_w_uq, v_ev_ckv_norm, v_ev_w_ukv, v_ev_w_out, v_od_w_in, v_od_b_f, v_od_w_out, v_final_norm):
    given = dict(locals())
    w_in = {n: given[n] for n in WEIGHTS}

    layers, misc = _local_groups(w_in, BF16)
    (a0, b0, c0), (a1, b1, c1) = [[_in_slot(g) for g in layer] for layer in layers]
    a0, c0 = _comm_call(_gather_job([a0, c0], rows=[(0, DOWN_ROWS), (0, D_MODEL)]), "all_gather_first")
    W = {n: w_in[n] for n in SMALL}
    W["final_norm"] = final_norm.reshape(1, -1)
    W.update(A=[a0], C=[c0])
    slots = dict(a0=a0, b0=b0, c0=c0, m=_in_slot(misc), a1=a1, b1=b1, c1=c1)

    loss_cols, dx, r_early, r_later, r_mid, last, G = _local_step(x[0], p[:, 0], loss_target[0], W, slots)

    a1b, c1b, b1 = r_early
    a1f, c1f = r_later
    a0b, c0b, b0, r_misc = r_mid
    a0f, c0f = _reduce_scatter(last, "last")
    grads = _ungroup_local([[a0f, a0b], [a1f, a1b]], [b0, b1], [[c0f, c0b], [c1f, c1b]], r_misc)
    layout = [(n, int(np.prod(w_in[n].shape))) for n in SMALL]
    vec = jnp.concatenate([G[n].astype(F32).reshape(-1) for n, _ in layout] + [jnp.sum(loss_cols).reshape(1)])
    vec = jnp.pad(vec, (0, N_DEV * SMALL_COLS - vec.shape[0])).reshape(N_DEV, SMALL_COLS)
    vec = _all_reduce_small(vec).reshape(-1)
    off = 0
    for n, size in layout:
        grads[n] = vec[off: off + size].reshape(w_in[n].shape)
        off += size
    loss = vec[off]

    delta, new_m, new_v = {}, {}, {}
    for n in WEIGHTS:
        shp = w_in[n].shape
        as2d = (lambda a: a.reshape(1, -1)) if len(shp) == 1 else (lambda a: a)
        d, nm, nv = _adamw(as2d(w_in[n]), as2d(grads[n]), as2d(given["m_" + n]), as2d(given["v_" + n]), f"adamw_{n}")
        delta[n], new_m[n], new_v[n] = d.reshape(shp), nm.reshape(shp), nv.reshape(shp)
    return (loss, dx[None], *[grads[n] for n in WEIGHTS], *[delta[n] for n in WEIGHTS],
            *[new_m[n] for n in WEIGHTS], *[new_v[n] for n in WEIGHTS])
```

```python
import functools

import numpy as np
import jax
import jax.numpy as jnp
from jax import lax
from jax.experimental import pallas as pl
from jax.experimental.pallas import tpu as pltpu

F32 = jnp.float32
BF16 = jnp.bfloat16
MESH = pl.DeviceIdType.MESH

D_MODEL = 1024
D_FF = 2816
RMS_EPS = 1e-6
PLE_DIM = 256
A_HEADS, A_KV_HEADS, A_HEAD_DIM, WINDOW = 8, 2, 64, 128
A_GROUP = A_HEADS // A_KV_HEADS
B_HEADS, B_NOPE, B_ROPE, B_V = 8, 64, 32, 64
ROPE_THETA = 10000.0
C_HEADS, C_HEAD_DIM = 16, 64
EVEN_IN = 1184
EVEN_IN_PAD = 1280
ODD_IN_AUG = 2 * 16 * 80 + 1024 + 16
ODD_IN_PAD = 3840
DEPTH = 2
ADAM_LR, ADAM_B1, ADAM_B2, ADAM_EPS, ADAM_WD, ADAM_STEP = 0.001, 0.9, 0.999, 1e-08, 0.01, 10

N_DEV = 8
LANES = 128
EW_TILE_BYTES = 3 << 20
MM_VMEM_BYTES = 26 << 20
NEG = -1e30
ATTN_TILE = 1024
ATTN_TILE_FWD = 1024
SWA_TILE = 512
QK_PAD = 80

FF_BLK = D_FF // 4
DOWN_ROWS = D_FF // N_DEV
B_ROWS, C_ROWS, G3_ROWS, G3_COLS = 256, 2 * D_MODEL, 1024, 768
PLE_GATE_BLK, MIX_OUT_BLK = 0, 1
OD_C, EV_C, STRIP_C = 386, 148, 128
STRIP0 = OD_C + EV_C

SMALL = ["ffa_norm", "mix_norm", "ffb_norm", "ple_norm", "ev_sinks", "ev_cq_norm", "ev_ckv_norm", "od_b_f", "final_norm"]
WEIGHTS = ["ffa_norm", "ffa_w_gate_up", "ffa_w_down", "mix_norm", "ffb_norm", "ffb_w_gate_up", "ffb_w_down", "ple_norm",
           "ple_w_gate", "ple_w_proj", "ev_w_in", "ev_sinks", "ev_cq_norm", "ev_w_uq", "ev_ckv_norm", "ev_w_ukv", "ev_w_out",
           "od_w_in", "od_b_f", "od_w_out", "final_norm"]
SMALL_COLS = 1280


def _divisor(n, cap, mult):
    if n <= cap:
        return n
    for t in range(cap - cap % mult, 0, -mult):
        if n % t == 0:
            return t
    raise ValueError(f"no tile for {n} under {cap} in steps of {mult}")


def _lanes(c):
    return -(-c // LANES) * LANES


def _ew(fn, rows, vecs, outs, reds=(), *, name):
    R = rows[0].shape[0]
    per_row = sum(_lanes(a.shape[1]) * a.dtype.itemsize for a in rows) + sum(_lanes(c) * jnp.dtype(d).itemsize for c, d in outs)
    tm = _divisor(R, max(16, EW_TILE_BYTES // per_row // 16 * 16), 16) if R % 16 == 0 else R
    n_r, n_v, n_o = len(rows), len(vecs), len(outs)

    def body(*refs):
        ins = [r[...] for r in refs[: n_r + n_v]]
        res = fn(*ins)
        if not isinstance(res, (tuple, list)):
            res = (res,)
        o_refs = refs[n_r + n_v: n_r + n_v + n_o]
        r_refs = refs[n_r + n_v + n_o:]
        for ref, val in zip(o_refs, res[:n_o]):
            ref[...] = val.astype(ref.dtype)
        if r_refs:
            @pl.when(pl.program_id(0) == 0)
            def _():
                for ref in r_refs:
                    ref[...] = jnp.zeros_like(ref)
            for ref, val in zip(r_refs, res[n_o:]):
                ref[...] += val

    in_specs = [pl.BlockSpec((tm, a.shape[1]), lambda i: (i, 0)) for a in rows]
    in_specs += [pl.BlockSpec((1, a.shape[1]), lambda i: (0, 0)) for a in vecs]
    out_specs = [pl.BlockSpec((tm, c), lambda i: (i, 0)) for c, _ in outs]
    out_specs += [pl.BlockSpec((1, c), lambda i: (0, 0)) for c in reds]
    out_shape = [jax.ShapeDtypeStruct((R, c), d) for c, d in outs] + [jax.ShapeDtypeStruct((1, c), F32) for c in reds]
    res = pl.pallas_call(body, name=name, grid=(R // tm,), in_specs=in_specs, out_specs=out_specs, out_shape=out_shape)(*rows, *vecs)
    return res[0] if len(res) == 1 else res


def _rms_fwd(x, w, name):
    def fn(x, w):
        y = x * lax.rsqrt(jnp.mean(x * x, axis=-1, keepdims=True) + RMS_EPS)
        return y * w
    return _ew(fn, [x], [w], [(x.shape[1], BF16)], name=name)


def _rms_bwd(dn, x, w, dres, name):
    def fn(dn, x, *rest):
        w = rest[-1]
        r = lax.rsqrt(jnp.mean(x * x, axis=-1, keepdims=True) + RMS_EPS)
        xh = x * r
        gw = dn * w
        dx = r * (gw - xh * jnp.mean(gw * xh, axis=-1, keepdims=True))
        if len(rest) == 2:
            dx = dx + rest[0]
        return dx, jnp.sum(dn * xh, axis=0, keepdims=True)
    rows = [dn, x] + ([dres] if dres is not None else [])
    return _ew(fn, rows, [w], [(x.shape[1], F32)], [x.shape[1]], name=name)


def _ple_fwd(h, gpre, pp, name):
    return _ew(lambda h, g, q: h + jax.nn.sigmoid(g.astype(F32)) * q.astype(F32), [h, gpre, pp], [], [(h.shape[1], F32)], name=name)


def _ple_bwd(dh, gpre, pp, name):
    def fn(dh, g, q):
        sg = jax.nn.sigmoid(g.astype(F32))
        return dh * q.astype(F32) * (sg * (1.0 - sg)), dh * sg
    return _ew(fn, [dh, gpre, pp], [], [(dh.shape[1], BF16), (dh.shape[1], BF16)], name=name)


def _rope(x1, x2, cos, sin, name):
    c = x1.shape[1]
    return _ew(lambda a, b, co, si: (a * co - b * si, a * si + b * co), [x1, x2, cos, sin], [], [(c, F32), (c, F32)], name=name)


def _logsig_fwd(f, b, name):
    def fn(f, b):
        z = f + b
        return jnp.minimum(z, 0.0) - jnp.log(1.0 + jnp.exp(-jnp.abs(z)))
    return _ew(fn, [f], [b], [(f.shape[1], F32)], name=name)


def _logsig_bwd(dlogf, f, b, name):
    def fn(d, f, b):
        df = d * jax.nn.sigmoid(-(f + b))
        return df, jnp.sum(df, axis=0, keepdims=True)
    return _ew(fn, [dlogf, f], [b], [(f.shape[1], F32)], [f.shape[1]], name=name)


def _final_fwd_bwd(h, w, target, name):
    d = h.shape[1]

    def fn(h, t, w):
        r = lax.rsqrt(jnp.mean(h * h, axis=-1, keepdims=True) + RMS_EPS)
        xh = h * r
        y = xh * w
        err = y - t
        dy = err * (1.0 / d)
        gw = dy * w
        dx = r * (gw - xh * jnp.mean(gw * xh, axis=-1, keepdims=True))
        return dx, jnp.sum(dy * xh, axis=0, keepdims=True), jnp.sum(err * err, axis=0, keepdims=True) * (0.5 / d)
    return _ew(fn, [h, target], [w], [(d, F32)], [d, d], name=name)


def _adamw(w, g, m, v, name):
    shape = w.shape
    c = shape[-1]
    w2, g2, m2, v2 = (a.reshape(-1, c) for a in (w, g, m, v))

    def fn(w, g, m, v):
        m = ADAM_B1 * m + (1.0 - ADAM_B1) * g
        v = ADAM_B2 * v + (1.0 - ADAM_B2) * jnp.square(g)
        m_hat = m / (1.0 - ADAM_B1 ** ADAM_STEP)
        v_hat = v / (1.0 - ADAM_B2 ** ADAM_STEP)
        delta = -ADAM_LR * (m_hat / (jnp.sqrt(v_hat) + ADAM_EPS) + ADAM_WD * w)
        return delta, m, v
    d, nm, nv = _ew(fn, [w2, g2, m2, v2], [], [(c, F32)] * 3, name=name)
    return d.reshape(shape), nm.reshape(shape), nv.reshape(shape)


def _split3(v):
    hi = v.astype(BF16)
    r1 = v - hi.astype(F32)
    mid = r1.astype(BF16)
    lo = (r1 - mid.astype(F32)).astype(BF16)
    return hi, mid, lo


def _cumsum(x, reverse, name):
    S, C = x.shape
    tm = _divisor(S, 512, 16)
    nt = S // tm

    def body(x_ref, o_ref, carry):
        @pl.when(pl.program_id(0) == 0)
        def _():
            carry[...] = jnp.zeros_like(carry)
        r = lax.broadcasted_iota(jnp.int32, (tm, tm), 0)
        c = lax.broadcasted_iota(jnp.int32, (tm, tm), 1)
        tri = jnp.where((c >= r) if reverse else (c <= r), 1.0, 0.0).astype(BF16)
        xv = x_ref[...]
        acc = jnp.zeros((tm, C), F32)
        for part in _split3(xv):
            acc = acc + jnp.dot(tri, part, preferred_element_type=F32)
        o_ref[...] = acc + carry[...]
        carry[...] += jnp.sum(xv, axis=0, keepdims=True)

    idx = (lambda i: (nt - 1 - i, 0)) if reverse else (lambda i: (i, 0))
    return pl.pallas_call(
        body, name=name, grid=(nt,), in_specs=[pl.BlockSpec((tm, C), idx)], out_specs=pl.BlockSpec((tm, C), idx),
        out_shape=jax.ShapeDtypeStruct((S, C), F32), scratch_shapes=[pltpu.VMEM((1, C), F32)],
    )(x)


NN = (((1,), (0,)), ((), ()))
NT = (((1,), (1,)), ((), ()))
TN = (((0,), (0,)), ((), ()))

HBM_SPEC = pl.BlockSpec(memory_space=pl.ANY)


def _job_in_body(job, refs, n_in, n_out, n_scr, grid):
    if job is None:
        return refs[n_in:], lambda: None
    ji, jo = len(job["ins"]), len(job["outs"])
    j_in = refs[n_in: n_in + ji]
    pos = n_in + ji
    own = list(refs[pos: pos + n_out])
    pos += n_out
    j_out = refs[pos: pos + jo]
    pos += jo
    own += list(refs[pos: pos + n_scr])
    ss, rs = refs[-2], refs[-1]
    first = functools.reduce(jnp.logical_and, [pl.program_id(d) == 0 for d in range(len(grid))])
    last = functools.reduce(jnp.logical_and, [pl.program_id(d) == n - 1 for d, n in enumerate(grid)])

    @pl.when(first)
    def _():
        job["start"](j_in, j_out, ss, rs)

    def finish():
        @pl.when(last)
        def _():
            job["finish"](j_in, j_out, ss, rs)

    return own, finish


def _job_call(job, body, *, name, grid, in_specs, out_specs, out_shape, args, scratch_shapes, aliases, dimension_semantics):
    in_specs, out_specs, out_shape, args, scratch_shapes = list(in_specs), list(out_specs), list(out_shape), list(args), list(scratch_shapes)
    aliases = dict(aliases)
    if job is not None:
        for i, o in job["aliases"].items():
            aliases[len(args) + i] = len(out_shape) + o
        in_specs += [HBM_SPEC] * len(job["ins"])
        args += list(job["ins"])
        out_specs += [HBM_SPEC] * len(job["outs"])
        out_shape += list(job["outs"])
        scratch_shapes += [pltpu.SemaphoreType.DMA((job["n_sems"],)), pltpu.SemaphoreType.DMA((job["n_sems"],))]
    return pl.pallas_call(
        body, name=name, grid=grid, in_specs=in_specs, out_specs=out_specs, out_shape=out_shape,
        scratch_shapes=scratch_shapes, input_output_aliases=aliases,
        compiler_params=pltpu.CompilerParams(dimension_semantics=dimension_semantics),
    )(*args)


def _comm_call(job, name):
    def body(*refs):
        ji, jo = len(job["ins"]), len(job["outs"])
        job["start"](refs[:ji], refs[ji: ji + jo], refs[-2], refs[-1])
        job["finish"](refs[:ji], refs[ji: ji + jo], refs[-2], refs[-1])

    return pl.pallas_call(
        body, name=name, in_specs=[HBM_SPEC] * len(job["ins"]), out_specs=[HBM_SPEC] * len(job["outs"]), out_shape=list(job["outs"]),
        input_output_aliases=dict(job["aliases"]),
        scratch_shapes=[pltpu.SemaphoreType.DMA((job["n_sems"],)), pltpu.SemaphoreType.DMA((job["n_sems"],))],
    )(*job["ins"])


def _mm_call(name, grid, k_axis, a, a_spec, a2d, b, b_spec, b2d, dims, out_sds, out_spec, o2d, *,
             alpha=1.0, res=None, res_spec=None, into=None, job=None, norm_bwd=None):
    nk = grid[k_axis]
    n_in = 2 + (res is not None) + (into is not None) + (3 if norm_bwd is not None else 0)
    n_out = 2 if norm_bwd is not None else 1

    def body(*refs):
        a_ref, b_ref = refs[0], refs[1]
        res_ref = refs[2] if res is not None else None
        own, finish_job = _job_in_body(job, refs, n_in, n_out, 1, grid)
        o_ref, acc_ref = own[0], own[-1]
        k = pl.program_id(k_axis)

        @pl.when(k == 0)
        def _():
            acc_ref[...] = jnp.zeros_like(acc_ref)

        if norm_bwd is not None:
            x_ref, w_ref, dres_ref = refs[n_in - 3: n_in]
            dw_ref = own[1]

            @pl.when(functools.reduce(jnp.logical_and, [pl.program_id(d) == 0 for d in range(len(grid))]))
            def _():
                dw_ref[...] = jnp.zeros_like(dw_ref)

        av = a_ref[...].reshape(a2d).astype(BF16)
        bv = b_ref[...].reshape(b2d).astype(BF16)
        acc_ref[...] += lax.dot_general(av, bv, dims, preferred_element_type=F32)

        @pl.when(k == nk - 1)
        def _():
            r = acc_ref[...]
            if alpha != 1.0:
                r = r * alpha
            if res_ref is not None:
                r = res_ref[...].reshape(o2d) + r
            if norm_bwd is not None:
                x = x_ref[...]
                rs = lax.rsqrt(jnp.mean(x * x, axis=-1, keepdims=True) + RMS_EPS)
                xh = x * rs
                gw = r * w_ref[...]
                dw_ref[...] += jnp.sum(r * xh, axis=0, keepdims=True)
                r = dres_ref[...] + rs * (gw - xh * jnp.mean(gw * xh, axis=-1, keepdims=True))
            o_ref[...] = r.reshape(o_ref.shape).astype(o_ref.dtype)

        finish_job()

    in_specs, args = [a_spec, b_spec], [a, b]
    if res is not None:
        in_specs.append(res_spec)
        args.append(res)
    aliases = {}
    if into is not None:
        aliases = {len(args): 0}
        in_specs.append(pl.BlockSpec(memory_space=pl.ANY))
        args.append(into)
        out_sds = jax.ShapeDtypeStruct(into.shape, into.dtype)
    out_specs, out_shape = [out_spec], [out_sds]
    if norm_bwd is not None:
        vec = pl.BlockSpec((1, o2d[1]), lambda *_: (0, 0))
        in_specs += [out_spec, vec, out_spec]
        args += list(norm_bwd)
        out_specs.append(vec)
        out_shape.append(jax.ShapeDtypeStruct((1, o2d[1]), F32))
    serial = job is not None or norm_bwd is not None
    sem = tuple("arbitrary" if d == k_axis or serial else "parallel" for d in range(len(grid)))
    res_all = _job_call(
        job, body, name=name, grid=grid, in_specs=in_specs, out_specs=out_specs, out_shape=out_shape, args=args,
        scratch_shapes=[pltpu.VMEM(o2d, F32)], aliases=aliases, dimension_semantics=sem)
    own = res_all[0] if n_out == 1 else tuple(res_all[:n_out])
    return own if job is None else (own, res_all[n_out:])


def _mm(a, b, *, ta=False, tb=False, out=F32, res=None, alpha=1.0, norm_bwd=None, name):
    K, M = a.shape if ta else a.shape[::-1]
    N = b.shape[0] if tb else b.shape[1]
    assert (b.shape[1] if tb else b.shape[0]) == K, (a.shape, b.shape, ta, tb)
    tk = _divisor(K, 1024, LANES)
    tn = _divisor(N, 1408, LANES)
    assert norm_bwd is None or tn == N
    for cap in (1024, 512, 256, 128):
        tm = _divisor(M, cap, LANES if ta else 16)
        est = 2 * (tm * tk * a.dtype.itemsize + tk * tn * b.dtype.itemsize + tm * tn * jnp.dtype(out).itemsize)
        est += tm * tn * 4 + (2 * tm * tn * 4 if res is not None else 0) + (4 * tm * tn * 4 if norm_bwd is not None else 0)
        if est <= MM_VMEM_BYTES:
            break
    a_spec = pl.BlockSpec((tk, tm), lambda i, j, k: (k, i)) if ta else pl.BlockSpec((tm, tk), lambda i, j, k: (i, k))
    b_spec = pl.BlockSpec((tn, tk), lambda i, j, k: (j, k)) if tb else pl.BlockSpec((tk, tn), lambda i, j, k: (k, j))
    o_spec = pl.BlockSpec((tm, tn), lambda i, j, k: (i, j))
    dims = (((0 if ta else 1,), (1 if tb else 0,)), ((), ()))
    return _mm_call(name, (M // tm, N // tn, K // tk), 2, a, a_spec, (tk, tm) if ta else (tm, tk), b, b_spec,
                    (tn, tk) if tb else (tk, tn), dims, jax.ShapeDtypeStruct((M, N), out), o_spec, (tm, tn),
                    alpha=alpha, res=res, res_spec=o_spec, norm_bwd=norm_bwd)


def _w128_spec(blk):
    return pl.BlockSpec((N_DEV, 128, D_MODEL), lambda *_: (0, blk, 0))


def _mm_w128(a, G1, blk, *, tb=False, res=None, out=F32, norm_bwd=None, name):
    S = a.shape[0]
    tm = _divisor(S, 1024 if norm_bwd is None else 512, 16)
    row = pl.BlockSpec((tm, D_MODEL), lambda i, k: (i, 0))
    return _mm_call(name, (S // tm, 1), 1, a, row, (tm, D_MODEL), G1, _w128_spec(blk), (D_MODEL, D_MODEL), NT if tb else NN,
                    jax.ShapeDtypeStruct((S, D_MODEL), out), row, (tm, D_MODEL), res=res, res_spec=row, norm_bwd=norm_bwd)


def _mm_w128_dw(a, b, blk, into, name):
    S = a.shape[0]
    tk = _divisor(S, 1024, 16)
    row = pl.BlockSpec((tk, D_MODEL), lambda i, k: (k, 0))
    return _mm_call(name, (1, S // tk), 1, a, row, (tk, D_MODEL), b, row, (tk, D_MODEL), TN, None, _w128_spec(blk),
                    (D_MODEL, D_MODEL), into=into)


def _ffn_gate_up(h, norm_w, G2v, rb, name, job=None):
    S = h.shape[0]
    tm = _divisor(S, 1024, 16)
    grid = (S // tm, 4)

    def body(*refs):
        h_ref, nw_ref, w_ref = refs[:3]
        (n_ref, gu_ref, act_ref, n_scr), finish_job = _job_in_body(job, refs, 3, 3, 1, grid)

        @pl.when(pl.program_id(1) == 0)
        def _():
            x = h_ref[...]
            y = x * lax.rsqrt(jnp.mean(x * x, axis=-1, keepdims=True) + RMS_EPS)
            n_scr[...] = (y * nw_ref[...]).astype(BF16)
            n_ref[...] = n_scr[...]

        nv = n_scr[...]
        g = jnp.dot(nv, w_ref[0, 0], preferred_element_type=F32)
        u = jnp.dot(nv, w_ref[1, 0], preferred_element_type=F32)
        sg = jax.nn.sigmoid(g)
        silu = g * sg
        gu_ref[0, 0] = (u * (sg * (1.0 + g * (1.0 - sg)))).astype(BF16)
        gu_ref[1, 0] = silu.astype(BF16)
        act_ref[0] = (silu * u).astype(BF16)
        finish_job()

    row = pl.BlockSpec((tm, D_MODEL), lambda i, j: (i, 0))
    return _job_call(
        job, body, name=name, grid=grid,
        in_specs=[row, pl.BlockSpec((1, D_MODEL), lambda i, j: (0, 0)), pl.BlockSpec((2, 1, D_MODEL, FF_BLK), lambda i, j: (0, j, rb, 0))],
        out_specs=[row, pl.BlockSpec((2, 1, tm, FF_BLK), lambda i, j: (0, j, i, 0)), pl.BlockSpec((1, tm, FF_BLK), lambda i, j: (j, i, 0))],
        out_shape=[jax.ShapeDtypeStruct((S, D_MODEL), BF16), jax.ShapeDtypeStruct((2, 4, S, FF_BLK), BF16), jax.ShapeDtypeStruct((4, S, FF_BLK), BF16)],
        args=[h, norm_w, G2v], scratch_shapes=[pltpu.VMEM((tm, D_MODEL), BF16)], aliases={},
        dimension_semantics=("arbitrary" if job is not None else "parallel", "arbitrary"))


def _ffn_down(act, G1, ob, h, name, job=None):
    S = h.shape[0]
    tm = _divisor(S, 1024, 16)
    row = pl.BlockSpec((tm, D_MODEL), lambda i, k: (i, 0))
    return _mm_call(name, (S // tm, 4), 1, act, pl.BlockSpec((1, tm, FF_BLK), lambda i, k: (k, i, 0)), (tm, FF_BLK),
                    G1, pl.BlockSpec((2, DOWN_ROWS, D_MODEL), lambda i, k: (k, ob, 0)), (FF_BLK, D_MODEL), NN,
                    jax.ShapeDtypeStruct((S, D_MODEL), F32), row, (tm, D_MODEL), alpha=0.5, res=h, res_spec=row, job=job)


def _ffn_down_dx(dh, G1, ob, gu, name):
    S = dh.shape[0]
    tm = _divisor(S, 1024, 16)

    def body(dh_ref, w_ref, gu_ref, o_ref):
        w = w_ref[...].reshape(FF_BLK, D_MODEL)
        dact = lax.dot_general(dh_ref[...].astype(BF16), w, NT, preferred_element_type=F32) * 0.5
        o_ref[0, 0] = (dact * gu_ref[0, 0].astype(F32)).astype(BF16)
        o_ref[1, 0] = (dact * gu_ref[1, 0].astype(F32)).astype(BF16)

    blk = pl.BlockSpec((2, 1, tm, FF_BLK), lambda i, j: (0, j, i, 0))
    return pl.pallas_call(
        body, name=name, grid=(S // tm, 4),
        in_specs=[pl.BlockSpec((tm, D_MODEL), lambda i, j: (i, 0)), pl.BlockSpec((2, DOWN_ROWS, D_MODEL), lambda i, j: (j, ob, 0)), blk],
        out_specs=blk, out_shape=jax.ShapeDtypeStruct((2, 4, S, FF_BLK), BF16),
    )(dh, G1, gu)


def _ffn_down_dw(act, dh, name, job=None):
    S = dh.shape[0]
    tk = _divisor(S, 1024, 16)
    return _mm_call(name, (4, S // tk), 1, act, pl.BlockSpec((1, tk, FF_BLK), lambda j, k: (j, k, 0)), (tk, FF_BLK),
                    dh, pl.BlockSpec((tk, D_MODEL), lambda j, k: (k, 0)), (tk, D_MODEL), TN,
                    jax.ShapeDtypeStruct((N_DEV, DOWN_ROWS, D_MODEL), BF16),
                    pl.BlockSpec((2, DOWN_ROWS, D_MODEL), lambda j, k: (j, 0, 0)), (FF_BLK, D_MODEL), alpha=0.5, job=job)


def _ffn_gate_up_dw(n, dgu8, name, job=None):
    S = n.shape[0]
    tk = _divisor(S, 1024, 16)
    return _mm_call(name, (N_DEV, S // tk), 1, n, pl.BlockSpec((tk, D_MODEL), lambda b, k: (k, 0)), (tk, D_MODEL),
                    dgu8, pl.BlockSpec((1, tk, FF_BLK), lambda b, k: (b, k, 0)), (tk, FF_BLK), TN,
                    jax.ShapeDtypeStruct((N_DEV, D_MODEL, FF_BLK), BF16),
                    pl.BlockSpec((1, D_MODEL, FF_BLK), lambda b, k: (b, 0, 0)), (D_MODEL, FF_BLK), job=job)


def _ffn_gate_up_dx(dgu8, G2, rb, h, norm_w, dres, name, job=None):
    S = h.shape[0]
    tm = _divisor(S, 1024, 16)
    row = pl.BlockSpec((tm, D_MODEL), lambda i, k: (i, 0))
    return _mm_call(name, (S // tm, N_DEV), 1, dgu8, pl.BlockSpec((1, tm, FF_BLK), lambda i, k: (k, i, 0)), (tm, FF_BLK),
                    G2, pl.BlockSpec((1, D_MODEL, FF_BLK), lambda i, k: (k, rb, 0)), (D_MODEL, FF_BLK), NT,
                    jax.ShapeDtypeStruct((S, D_MODEL), F32), row, (tm, D_MODEL), norm_bwd=(h, norm_w, dres), job=job)


def _unheads(x):
    h, S, d = x.shape
    return jnp.transpose(x, (1, 0, 2)).reshape(S, h * d)


def _exact3(v):
    rnd = lambda a: lax.reduce_precision(a, exponent_bits=8, mantissa_bits=7)
    hi = rnd(v)
    mid = rnd(v - hi)
    return hi, mid, rnd(v - hi - mid)


def _causal_mask(st, q0, k0, window):
    dist = (q0 + lax.broadcasted_iota(jnp.int32, st.shape, 1)) - (k0 + lax.broadcasted_iota(jnp.int32, st.shape, 0))
    mask = dist >= 0
    if window is not None:
        mask = mask & (dist < window)
    return jnp.where(mask, st, NEG)


def _attn_fwd(qT, k, vT1, *, tile, hb, window=None, sink=None, name, job=None):
    H, dqk, S = qT.shape
    G = H // k.shape[0]
    dvp = vT1.shape[1]
    dv = dvp - 16
    tq = tk = tile
    assert H % hb == 0 and (G == 1 or G % hb == 0)
    kvb = hb if G == 1 else 1
    grid = (H // hb, S // tq)
    n_in = 3 + (sink is not None)

    def body(*refs):
        q_ref, k_ref, v_ref = refs[:3]
        (o_ref, lse_ref), finish_job = _job_in_body(job, refs, n_in, 2, 0, grid)
        i = pl.program_id(1)
        carry = []
        for a in range(hb):
            if sink is not None:
                carry.append(jnp.zeros((1, tq), F32) + refs[3][a, :, 0:1])
                carry.append(jnp.where(lax.broadcasted_iota(jnp.int32, (dvp, tq), 0) == dv, 1.0, 0.0))
            else:
                carry.append(jnp.full((1, tq), NEG, F32))
                carry.append(jnp.zeros((dvp, tq), F32))

        def step(j, carry, masked, off=None, keys=tk, q_from=0):
            off = pl.multiple_of(j * tk, tk) if off is None else off
            out = []
            for a in range(hb):
                m, acc = carry[2 * a], carry[2 * a + 1]
                kv = a if kvb > 1 else 0
                st = jnp.dot(k_ref[kv, pl.ds(off, keys), :], q_ref[a][:, q_from:], preferred_element_type=F32)
                if masked:
                    st = _causal_mask(st, i * tq + q_from, off, window)
                m_old, acc_old = m[:, q_from:], acc[:, q_from:]
                m_new = jnp.maximum(m_old, jnp.max(st, axis=0, keepdims=True))
                pt = jnp.exp(st - m_new).astype(BF16)
                acc_new = jnp.exp(m_old - m_new) * acc_old + jnp.dot(v_ref[kv, :, pl.ds(off, keys)], pt, preferred_element_type=F32)
                if q_from:
                    m_new = jnp.concatenate([m[:, :q_from], m_new], axis=1)
                    acc_new = jnp.concatenate([acc[:, :q_from], acc_new], axis=1)
                out += [m_new, acc_new]
            return tuple(out)

        carry = tuple(carry)
        if window is None:
            carry = lax.fori_loop(0, i, functools.partial(step, masked=False), carry)
            if tq % (2 * LANES) == 0:
                half = tq // 2
                carry = step(None, carry, True, off=pl.multiple_of(i * tq, tq), keys=half)
                carry = step(None, carry, True, off=pl.multiple_of(i * tq + half, half), keys=half, q_from=half)
            else:
                carry = step(i, carry, True)
        else:
            assert window % LANES == 0 and tq + window <= S
            carry = step(None, carry, True, off=pl.multiple_of(jnp.maximum(i * tq - window, 0), LANES), keys=tq + window)
        for a in range(hb):
            m, acc = carry[2 * a], carry[2 * a + 1]
            l = acc[dv:dv + 1, :]
            o_ref[a] = acc[:dv, :] / l
            lse_ref[a] = m + jnp.log(l)
        finish_job()

    kv_idx = (lambda b: b) if G == 1 else (lambda b: (b * hb) // G)
    in_specs = [
        pl.BlockSpec((hb, dqk, tq), lambda b, i: (b, 0, i)),
        pl.BlockSpec((kvb, S, dqk), lambda b, i: (kv_idx(b), 0, 0)),
        pl.BlockSpec((kvb, dvp, S), lambda b, i: (kv_idx(b), 0, 0)),
    ]
    args = [qT, k, vT1]
    if sink is not None:
        in_specs += [pl.BlockSpec((hb, 1, LANES), lambda b, i: (b, 0, 0))]
        args += [sink]
    return _job_call(
        job, body, name=name, grid=grid, in_specs=in_specs,
        out_specs=[pl.BlockSpec((hb, dv, tq), lambda b, i: (b, 0, i)), pl.BlockSpec((hb, 1, tq), lambda b, i: (b, 0, i))],
        out_shape=[jax.ShapeDtypeStruct((H, dv, S), F32), jax.ShapeDtypeStruct((H, 1, S), F32)],
        args=args, scratch_shapes=[], aliases={}, dimension_semantics=("arbitrary", "arbitrary") if job is not None else ("parallel", "parallel"))


def _attn_bwd(qT, k, kT, v, oT, doT, lse, *, tile, hb, window=None, sink=None, real=None, extra=False, full=False, name):
    H, dqk, S = qT.shape
    G = H // k.shape[0]
    dv = v.shape[2]
    tq = tk = tile
    nq = S // tq
    has_p = sink is not None
    real = dqk if real is None else real
    main = dqk if full else real
    assert H % hb == 0 and (G == 1 or G % hb == 0) and not (extra and real == dqk)
    kvb = hb if G == 1 else 1

    def body(*refs):
        qT_ref, k_ref, kT_ref, v_ref, oT_ref, doT_ref, lse_ref = refs[:7]
        p_ref = refs[7] if has_p else None
        pos = 8 if has_p else 7
        dq_ref, dk_ref, dv_ref = refs[pos: pos + 3]
        pos += 3
        ds_ref = refs[pos] if has_p else None
        pos += has_p
        dqx_ref, dkx_ref = (refs[pos], refs[pos + 1]) if extra else (None, None)
        delta = refs[-1]
        j = pl.program_id(1)

        @pl.when(j == 0)
        def _():
            dq_ref[...] = jnp.zeros_like(dq_ref)
            if extra:
                dqx_ref[...] = jnp.zeros_like(dqx_ref)
            for a in range(hb):
                drow = jnp.sum(doT_ref[a].astype(F32) * oT_ref[a], axis=0, keepdims=True)
                delta[a] = drow
                if has_p:
                    w = jnp.exp(p_ref[a, :, 0:1] - lse_ref[a])
                    ds_ref[a] = jnp.zeros((1, LANES), F32) - jnp.sum(w * drow, axis=1, keepdims=True)

        def step(i, carry, masked, off=None, qs=tq, keys=tk):
            off = pl.multiple_of(i * tq, tq) if off is None else off
            out = []
            for a in range(hb):
                dk, dvv = carry[2 * a], carry[2 * a + 1]
                kv = a if kvb > 1 else 0
                qTi = qT_ref[a, :, pl.ds(off, qs)]
                doTi = doT_ref[a, :, pl.ds(off, qs)]
                st = jnp.dot(k_ref[kv, pl.ds(0, keys), :], qTi, preferred_element_type=F32)
                if masked:
                    st = _causal_mask(st, off, j * tk, window)
                pt = jnp.exp(st - lse_ref[a, :, pl.ds(off, qs)])
                dv_new = lax.dot_general(pt.astype(BF16), doTi, NT, preferred_element_type=F32)
                dpt = jnp.dot(v_ref[kv, pl.ds(0, keys), :], doTi, preferred_element_type=F32)
                dsb = (pt * (dpt - delta[a, :, pl.ds(off, qs)])).astype(BF16)
                dk_new = lax.dot_general(dsb, qTi, NT, preferred_element_type=F32)
                if keys < tk:
                    dk = jnp.concatenate([dk[:keys] + dk_new, dk[keys:]], axis=0)
                    dvv = jnp.concatenate([dvv[:keys] + dv_new, dvv[keys:]], axis=0)
                else:
                    dk, dvv = dk + dk_new, dvv + dv_new
                dqt = jnp.dot(kT_ref[kv, :, pl.ds(0, keys)], dsb, preferred_element_type=F32)
                dq_ref[a, :, pl.ds(off, qs)] += dqt[:main]
                if extra:
                    dqx_ref[a, :, pl.ds(off, qs)] += dqt[real:]
                out += [dk, dvv]
            return tuple(out)

        carry = (jnp.zeros((tk, dqk), F32), jnp.zeros((tk, dv), F32)) * hb
        if window is None:
            if tk % (2 * LANES) == 0:
                half = tk // 2
                carry = step(None, carry, True, off=pl.multiple_of(j * tk + half, half), qs=half)
                carry = step(None, carry, True, off=pl.multiple_of(j * tk, tk), qs=half, keys=half)
            else:
                carry = step(j, carry, True)
            carry = lax.fori_loop(j + 1, nq, functools.partial(step, masked=False), carry)
        else:
            assert window % LANES == 0 and tk + window <= S
            carry = step(None, carry, True, off=pl.multiple_of(jnp.minimum(j * tk, S - (tk + window)), LANES), qs=tk + window)
        for a in range(hb):
            dk_ref[a] = carry[2 * a][:, :main]
            if extra:
                dkx_ref[a] = carry[2 * a][:, real:]
            dv_ref[a] = carry[2 * a + 1]

    kv_idx = (lambda b: b) if G == 1 else (lambda b: (b * hb) // G)
    colsT = lambda d: pl.BlockSpec((hb, d, S), lambda b, j: (b, 0, 0))
    in_specs = [
        colsT(dqk),
        pl.BlockSpec((kvb, tk, dqk), lambda b, j: (kv_idx(b), j, 0)),
        pl.BlockSpec((kvb, dqk, tk), lambda b, j: (kv_idx(b), 0, j)),
        pl.BlockSpec((kvb, tk, dv), lambda b, j: (kv_idx(b), j, 0)),
        colsT(dv), colsT(dv),
        pl.BlockSpec((hb, 1, S), lambda b, j: (b, 0, 0)),
    ]
    args = [qT, k, kT, v, oT, doT, lse]
    if has_p:
        in_specs += [pl.BlockSpec((hb, 1, LANES), lambda b, j: (b, 0, 0))]
        args += [sink]
    out_specs = [colsT(main), pl.BlockSpec((hb, tk, main), lambda b, j: (b, j, 0)), pl.BlockSpec((hb, tk, dv), lambda b, j: (b, j, 0))]
    out_shape = [jax.ShapeDtypeStruct((H, main, S), F32), jax.ShapeDtypeStruct((H, S, main), F32), jax.ShapeDtypeStruct((H, S, dv), F32)]
    if has_p:
        out_specs += [pl.BlockSpec((hb, 1, LANES), lambda b, j: (b, 0, 0))]
        out_shape += [jax.ShapeDtypeStruct((H, 1, LANES), F32)]
    if extra:
        out_specs += [colsT(dqk - real), pl.BlockSpec((hb, tk, dqk - real), lambda b, j: (b, j, 0))]
        out_shape += [jax.ShapeDtypeStruct((H, dqk - real, S), F32), jax.ShapeDtypeStruct((H, S, dqk - real), F32)]
    return pl.pallas_call(
        body, name=name, grid=(H // hb, S // tk), in_specs=in_specs, out_specs=out_specs, out_shape=out_shape,
        scratch_shapes=[pltpu.VMEM((hb, 1, S), F32)],
        compiler_params=pltpu.CompilerParams(dimension_semantics=("parallel", "arbitrary")),
    )(*args)


def _rows_and_cols(x3):
    xb = x3.astype(BF16)
    return jnp.transpose(xb, (1, 0, 2)), jnp.transpose(xb, (1, 2, 0))


def _cols_only(x3):
    return jnp.transpose(x3.astype(BF16), (1, 2, 0))


def _v_with_ones(v3):
    S, h, _ = v3.shape
    vT = jnp.transpose(v3.astype(BF16), (1, 2, 0))
    return jnp.concatenate([vT, jnp.ones((h, 1, S), BF16), jnp.zeros((h, 15, S), BF16)], axis=1)


def _from_T(oT):
    h, d, S = oT.shape
    return jnp.transpose(oT, (2, 0, 1)).reshape(S, h * d)


def _coords():
    return lax.axis_index("x"), lax.axis_index("y"), lax.axis_index("c")


def _peer(axis):
    x, y, c = _coords()
    return {"x": (1 - x, y, c), "y": (x, 1 - y, c), "c": (x, y, 1 - c)}[axis]


def _gather_job(bufs, rows=None):
    n = len(bufs)

    def copies(outs, send_sems, recv_sems):
        x, y, c = _coords()
        me, sibling = (x, y, c), (x, y, 1 - c)
        chips = [(1 - x, y), (x, 1 - y), (1 - x, 1 - y)]

        def copy(t, k, block, to):
            px, py, pc = block
            ref = outs[t].at[4 * px + 2 * py + pc]
            if rows is not None and rows[t] is not None:
                ref = ref.at[pl.ds(rows[t][0], rows[t][1])]
            return pltpu.make_async_remote_copy(ref, ref, send_sems.at[7 * t + k], recv_sems.at[7 * t + k], device_id=to, device_id_type=MESH)

        return copy, me, sibling, chips, c

    def start(ins, outs, send_sems, recv_sems):
        copy, me, sibling, chips, c = copies(outs, send_sems, recv_sems)
        for t in range(n):
            copy(t, 0, me, sibling).start()
            for j, chip in enumerate(chips):
                copy(t, 1 + j, me, (*chip, c)).start()

    def finish(ins, outs, send_sems, recv_sems):
        copy, me, sibling, chips, c = copies(outs, send_sems, recv_sems)
        for j, chip in enumerate(chips):
            for t in range(n):
                copy(t, 1 + j, (*chip, c), me).wait_recv()
                copy(t, 4 + j, (*chip, c), sibling).start()
        for t in range(n):
            copy(t, 0, sibling, me).wait_recv()
            for j, chip in enumerate(chips):
                copy(t, 4 + j, (*chip, 1 - c), me).wait_recv()
        for t in range(n):
            copy(t, 0, me, sibling).wait_send()
            for j, chip in enumerate(chips):
                copy(t, 1 + j, me, (*chip, c)).wait_send()
                copy(t, 4 + j, (*chip, c), sibling).wait_send()

    return dict(ins=list(bufs), outs=[jax.ShapeDtypeStruct(b.shape, b.dtype) for b in bufs], aliases={t: t for t in range(n)},
                n_sems=7 * n, start=start, finish=finish)


def _in_slot(local):
    x, y, c = _coords()
    buf = lax.empty((N_DEV,) + local.shape, local.dtype)
    return lax.dynamic_update_slice(buf, local[None], (4 * x + 2 * y + c, 0, 0))


def _pair_job(vs, axes):
    n = len(vs)
    axes = [axes] * n if isinstance(axes, str) else axes

    def copies(ins, outs, send_sems, recv_sems):
        out = []
        for t in range(n):
            me = lax.axis_index(axes[t])
            src = ins[t].at[1 - me] if len(ins[t].shape) == 3 else ins[t].at[:, 1 - me]
            out.append(pltpu.make_async_remote_copy(src, outs[t], send_sems.at[t], recv_sems.at[t], device_id=_peer(axes[t]), device_id_type=MESH))
        return out

    def start(*refs):
        for cp in copies(*refs):
            cp.start()

    def finish(*refs):
        for cp in copies(*refs):
            cp.wait()

    return dict(ins=list(vs), outs=[jax.ShapeDtypeStruct(v.shape[:-3] + v.shape[-2:], v.dtype) for v in vs], aliases={}, n_sems=n,
                start=start, finish=finish)


def _add_kept(v, got, axis, out, name):
    R, C = v.shape[-2:]
    lead = v.shape[0] if v.ndim == 4 else 1
    tm = _divisor(R, max(16, EW_TILE_BYTES // (_lanes(C) * (v.dtype.itemsize + got.dtype.itemsize + jnp.dtype(out).itemsize)) // 16 * 16), 16)
    me = lax.axis_index(axis).astype(jnp.int32).reshape(1)
    v4 = v.reshape(lead, 2, R, C)
    g3 = got.reshape(lead, R, C)

    def body(me_ref, v_ref, g_ref, o_ref):
        o_ref[...] = (v_ref[0].astype(F32) + g_ref[...].astype(F32)).astype(o_ref.dtype)

    res = pl.pallas_call(
        body, name=name, out_shape=jax.ShapeDtypeStruct((lead, R, C), out),
        grid_spec=pltpu.PrefetchScalarGridSpec(
            num_scalar_prefetch=1, grid=(lead, R // tm),
            in_specs=[pl.BlockSpec((1, 1, tm, C), lambda b, i, me: (b, me[0], i, 0)), pl.BlockSpec((1, tm, C), lambda b, i, me: (b, i, 0))],
            out_specs=pl.BlockSpec((1, tm, C), lambda b, i, me: (b, i, 0))),
    )(me, v4, g3)
    return res


def _cross_job(vs):
    n = len(vs)

    def copies(ins, outs, send_sems, recv_sems):
        x, y, _ = _coords()
        out = []
        for t in range(n):
            h = ins[t].shape[2] // 2
            out.append(pltpu.make_async_remote_copy(ins[t].at[1 - x, :, pl.ds(0, h)], outs[2 * t], send_sems.at[2 * t], recv_sems.at[2 * t],
                                                    device_id=_peer("x"), device_id_type=MESH))
            out.append(pltpu.make_async_remote_copy(ins[t].at[:, 1 - y, pl.ds(h, h)], outs[2 * t + 1], send_sems.at[2 * t + 1], recv_sems.at[2 * t + 1],
                                                    device_id=_peer("y"), device_id_type=MESH))
        return out

    def start(*refs):
        for cp in copies(*refs):
            cp.start()

    def finish(*refs):
        for cp in copies(*refs):
            cp.wait()

    outs = []
    for v in vs:
        outs += [jax.ShapeDtypeStruct((2, v.shape[2] // 2, v.shape[3]), v.dtype)] * 2
    return dict(ins=list(vs), outs=outs, aliases={}, n_sems=2 * n, start=start, finish=finish)


def _add_picked(v, got, axis, out, name):
    _, _, R, C = v.shape
    h = R // 2
    tm = _divisor(h, max(16, EW_TILE_BYTES // (_lanes(C) * (v.dtype.itemsize + got.dtype.itemsize + jnp.dtype(out).itemsize)) // 16 * 16), 16)
    me = lax.axis_index(axis).astype(jnp.int32).reshape(1)
    if axis == "x":
        v_map = lambda b, i, me: (me[0], b, i, 0)
    else:
        v_map = lambda b, i, me: (b, me[0], i + h // tm, 0)

    def body(me_ref, v_ref, g_ref, o_ref):
        o_ref[...] = (v_ref[0].astype(F32) + g_ref[...].astype(F32)).astype(o_ref.dtype)

    return pl.pallas_call(
        body, name=name, out_shape=jax.ShapeDtypeStruct((2, h, C), out),
        grid_spec=pltpu.PrefetchScalarGridSpec(
            num_scalar_prefetch=1, grid=(2, h // tm),
            in_specs=[pl.BlockSpec((1, 1, tm, C), v_map), pl.BlockSpec((1, tm, C), lambda b, i, me: (b, i, 0))],
            out_specs=pl.BlockSpec((1, tm, C), lambda b, i, me: (b, i, 0))),
    )(me, v, got)


def _reduce_scatter_steps(gs, tag):
    n = len(gs)
    vs = [g.reshape(4, 2, *g.shape[1:]) for g in gs]
    got = yield _pair_job(vs, "c")
    vs = [_add_kept(v, r, "c", BF16, f"rs_{tag}_add_c{t}") for t, (v, r) in enumerate(zip(vs, got))]
    vs = [v.reshape(2, 2, v.shape[1], v.shape[2]) for v in vs]
    got = yield _cross_job(vs)
    up = [_add_picked(v, r, "x", BF16, f"rs_{tag}_add_x{t}") for t, (v, r) in enumerate(zip(vs, got[0::2]))]
    lo = [_add_picked(v, r, "y", BF16, f"rs_{tag}_add_y{t}") for t, (v, r) in enumerate(zip(vs, got[1::2]))]
    got = yield _pair_job(up + lo, ["y"] * n + ["x"] * n)
    out = []
    for t in range(n):
        a = _add_kept(up[t], got[t], "y", F32, f"rs_{tag}_add_y2{t}")[0]
        b = _add_kept(lo[t], got[n + t], "x", F32, f"rs_{tag}_add_x2{t}")[0]
        out.append(jnp.concatenate([a, b], axis=0))
    return out


def _reduce_scatter(gs, tag):
    steps = _reduce_scatter_steps(gs, tag)
    job = next(steps)
    for stage in ("c", "xy", "yx"):
        got = _comm_call(job, f"rs_{tag}_{stage}")
        try:
            job = steps.send(got)
        except StopIteration as done:
            return done.value


def _all_reduce_small(v):
    def body(v_ref, o_ref, buf, send_sems, recv_sems):
        x, y, c = _coords()
        me = 4 * x + 2 * y + c
        buf[me] = v_ref[...]
        copies = []
        for k in range(1, N_DEV):
            peer = tuple((1 - a) if (k >> s) & 1 else a for a, s in ((x, 2), (y, 1), (c, 0)))
            cp = pltpu.make_async_remote_copy(v_ref, buf.at[me], send_sems.at[k - 1], recv_sems.at[k - 1], device_id=peer, device_id_type=MESH)
            cp.start()
            copies.append(cp)
        for cp in copies:
            cp.wait()
        acc = buf[0]
        for d in range(1, N_DEV):
            acc = acc + buf[d]
        o_ref[...] = acc

    vm = pl.BlockSpec(memory_space=pltpu.VMEM)
    return pl.pallas_call(
        body, name="all_reduce_small", in_specs=[vm], out_specs=vm, out_shape=jax.ShapeDtypeStruct(v.shape, F32),
        scratch_shapes=[pltpu.VMEM((N_DEV,) + v.shape, F32), pltpu.SemaphoreType.DMA((N_DEV - 1,)), pltpu.SemaphoreType.DMA((N_DEV - 1,))],
    )(v)


def _local_groups(w, dtype):
    mix_out = [w["ev_w_out"][0], w["od_w_out"][0]]
    layers = []
    for l in range(DEPTH):
        a = jnp.concatenate([w["ffa_w_down"][l], w["ffb_w_down"][l]], axis=0).astype(dtype)
        b = jnp.concatenate([w["ple_w_gate"][l], mix_out[l]], axis=0).astype(dtype)
        c = jnp.concatenate([w["ffa_w_gate_up"][l], w["ffb_w_gate_up"][l]], axis=0).astype(dtype)
        layers.append((a, b, c))
    strip = jnp.concatenate([w["ple_w_proj"].reshape(-1, STRIP_C), w["ev_w_ukv"][0], jnp.pad(w["ev_w_uq"][0], ((0, 0), (0, STRIP_C - 96))),
                             jnp.zeros((G3_ROWS - 896, STRIP_C), F32)], axis=0)
    m = jnp.concatenate([w["od_w_in"][0], w["ev_w_in"][0], strip, jnp.zeros((G3_ROWS, G3_COLS - STRIP0 - STRIP_C), F32)], axis=1).astype(dtype)
    return layers, m


def _ungroup_local(a, b, c, r3):
    out = {
        "ffa_w_down": jnp.stack([x[0] for x in a]), "ffb_w_down": jnp.stack([x[1] for x in a]),
        "ple_w_gate": jnp.stack([x[:128] for x in b]), "ev_w_out": b[0][128:][None], "od_w_out": b[1][128:][None],
        "ffa_w_gate_up": jnp.stack([x[0] for x in c]), "ffb_w_gate_up": jnp.stack([x[1] for x in c]),
        "od_w_in": r3[:, :OD_C][None], "ev_w_in": r3[:, OD_C:STRIP0][None],
    }
    strip = r3[:, STRIP0:STRIP0 + STRIP_C]
    out["ple_w_proj"] = strip[:512].reshape(2, PLE_DIM, STRIP_C)
    out["ev_w_ukv"] = strip[512:640][None]
    out["ev_w_uq"] = strip[640:896, :96][None]
    return out


def _cols(a):
    return jnp.transpose(a, (1, 0, 2)).reshape(a.shape[1], -1)


def _blocks(g, c):
    return jnp.transpose(g.reshape(g.shape[0], N_DEV, c), (1, 0, 2))


def _uq_permute(w):
    r = w.shape[0]
    w3 = w.reshape(r, B_HEADS, B_NOPE + B_ROPE)
    half = B_ROPE // 2
    return jnp.concatenate([w3[:, :, :B_NOPE].reshape(r, -1), w3[:, :, B_NOPE:B_NOPE + half].reshape(r, -1), w3[:, :, B_NOPE + half:].reshape(r, -1)], axis=1)


def _uq_unpermute(g):
    r = g.shape[0]
    half = B_ROPE // 2
    n = B_HEADS * B_NOPE
    parts = [g[:, :n].reshape(r, B_HEADS, B_NOPE), g[:, n:n + B_HEADS * half].reshape(r, B_HEADS, half), g[:, n + B_HEADS * half:].reshape(r, B_HEADS, half)]
    return jnp.concatenate(parts, axis=2).reshape(r, -1)


def _ukv_permute(w):
    r = w.shape[0]
    return jnp.transpose(w.reshape(r, B_HEADS, 2, B_NOPE), (0, 2, 1, 3)).reshape(r, -1)


def _ukv_unpermute(g):
    r = g.shape[0]
    return jnp.transpose(g.reshape(r, 2, B_HEADS, B_NOPE), (0, 2, 1, 3)).reshape(r, -1)


def _od_in_widen(w):
    n = C_HEADS * C_HEAD_DIM
    wide = lambda m: jnp.pad(m.reshape(-1, C_HEADS, C_HEAD_DIM), ((0, 0), (0, 0), (0, QK_PAD - C_HEAD_DIM))).reshape(m.shape[0], -1)
    return jnp.concatenate([wide(w[:, :n] * C_HEAD_DIM ** -0.5), wide(w[:, n:2 * n]), w[:, 2 * n:],
                            jnp.zeros((w.shape[0], ODD_IN_PAD - ODD_IN_AUG), w.dtype)], axis=1)


def _od_in_narrow(g):
    wp = C_HEADS * QK_PAD
    narrow = lambda m: m.reshape(-1, C_HEADS, QK_PAD)[:, :, :C_HEAD_DIM].reshape(m.shape[0], -1)
    return jnp.concatenate([narrow(g[:, :wp]) * C_HEAD_DIM ** -0.5, narrow(g[:, wp:2 * wp]), g[:, 2 * wp:ODD_IN_AUG]], axis=1)


def _misc_weights(G3):
    strip = G3[:, :, STRIP0:STRIP0 + STRIP_C]
    return {
        "od_w_in": _od_in_widen(_cols(G3[:, :, :OD_C])),
        "ev_w_in": jnp.pad(_cols(G3[:, :, OD_C:STRIP0]), ((0, 0), (0, EVEN_IN_PAD - EVEN_IN))),
        "ple_w_proj": [_cols(strip[:, i * PLE_DIM:(i + 1) * PLE_DIM]) for i in range(DEPTH)],
        "ev_w_ukv": _ukv_permute(_cols(strip[:, 512:640])),
        "ev_w_uq": _uq_permute(_cols(strip[:, 640:896, :96])),
    }


def _misc_grads(G):
    strip = jnp.concatenate([
        _blocks(G["ple_w_proj"][0], STRIP_C), _blocks(G["ple_w_proj"][1], STRIP_C), _blocks(_ukv_unpermute(G["ev_w_ukv"]), STRIP_C),
        jnp.pad(_blocks(_uq_unpermute(G["ev_w_uq"]), 96), ((0, 0), (0, 0), (0, STRIP_C - 96))),
        jnp.zeros((N_DEV, G3_ROWS - 896, STRIP_C), F32)], axis=1)
    return jnp.concatenate([_blocks(_od_in_narrow(G["od_w_in"]), OD_C), _blocks(G["ev_w_in"][:, :EVEN_IN], EV_C), strip,
                            jnp.zeros((N_DEV, G3_ROWS, G3_COLS - STRIP0 - STRIP_C), F32)], axis=2)


def _ffn_fwd(h, norm_w, W, f, i, tag, ride=None):
    job = ride() if ride else None
    res = _ffn_gate_up(h, norm_w, W["C"][i].reshape(2, 4, C_ROWS, FF_BLK), f, f"{tag}_gate_up", job=job)
    n, gu, act = res[:3]
    if job is not None:
        ride(res[3:])
    job = ride() if ride else None
    out = _ffn_down(act, W["A"][i], f, h, f"{tag}_down", job=job)
    if job is not None:
        out, got = out
        ride(got)
    return out, (h, n, gu, act)


def _ffn_bwd(dout, saved, norm_w, W, GB, f, i, tag, ride=None):
    h, n, gu, act = saved
    S = h.shape[0]
    def carried(call):
        job = ride() if ride else None
        res = call(job)
        if job is None:
            return res
        ride(res[1])
        return res[0]

    GB["A"][i][f] = carried(lambda job: _ffn_down_dw(act, dout, f"{tag}_down_dw", job=job))
    dgu = _ffn_down_dx(dout, W["A"][i], f, gu, f"{tag}_down_dx").reshape(N_DEV, S, FF_BLK)
    res = carried(lambda job: _ffn_gate_up_dx(dgu, W["C"][i], f, h, norm_w, dout, f"{tag}_gate_up_dx", job=job))
    GB["C"][i][f] = carried(lambda job: _ffn_gate_up_dw(n, dgu, f"{tag}_gate_up_dw", job=job))
    return res


def _rope_tables(S):
    inv = ROPE_THETA ** (-jnp.arange(0, B_ROPE, 2, dtype=F32) / B_ROPE)
    ang = jnp.arange(S, dtype=F32)[:, None] * inv[None, :]
    return jnp.cos(ang), jnp.sin(ang)


def _alibi_columns(S):
    t = jnp.arange(S, dtype=jnp.int32)
    hi = ((t // 16) * 16).astype(F32)
    lo = (t % 16).astype(F32)
    slopes = 2.0 ** (-8.0 * jnp.arange(1, A_HEADS + 1, dtype=F32) / A_HEADS)
    zq = jnp.zeros((S, A_HEADS), F32)
    rest = QK_PAD - A_HEAD_DIM - 4
    qc = jnp.stack([-slopes[None, :] * hi[:, None], -slopes[None, :] * lo[:, None], zq + slopes[None, :], zq + slopes[None, :]] + [zq] * rest, axis=-1)
    one = jnp.ones((S, A_KV_HEADS), F32)
    zk = jnp.zeros((S, A_KV_HEADS), F32)
    kc = jnp.stack([one, one, zk + hi[:, None], zk + lo[:, None]] + [zk] * rest, axis=-1)
    return qc, kc


def _sink_prm(sinks):
    return jnp.zeros((A_HEADS, 1, LANES), F32).at[:, 0, 0].set(sinks.astype(F32))


def _with_ride(ride, call):
    job = ride() if ride else None
    res = call(job)
    if job is None:
        return res
    n_own = len(res) - len(job["outs"])
    ride(res[n_own:])
    return res[:n_own]


def _even_fwd(hn, h, W, ride=None):
    S = hn.shape[0]
    proj = _mm(hn, W["ev_w_in"], name="ev_in")
    a_q, a_k, a_v = proj[:, :512], proj[:, 512:640], proj[:, 640:768]
    c_q, c_kv = proj[:, 768:1024], proj[:, 1024:1152]
    kr1, kr2 = proj[:, 1152:1168], proj[:, 1168:1184]
    qc, kc = _alibi_columns(S)
    qaT = _cols_only(jnp.concatenate([(a_q * A_HEAD_DIM ** -0.5).reshape(S, A_HEADS, A_HEAD_DIM), qc], axis=-1))
    ka, kaT = _rows_and_cols(jnp.concatenate([a_k.reshape(S, A_KV_HEADS, A_HEAD_DIM), kc], axis=-1))
    va3 = a_v.reshape(S, A_KV_HEADS, A_HEAD_DIM)
    va = jnp.transpose(va3.astype(BF16), (1, 0, 2))
    prm = _sink_prm(W["ev_sinks"][0])
    oaT, lse_a = _with_ride(ride, lambda job: _attn_fwd(qaT, ka, _v_with_ones(va3), tile=min(SWA_TILE, S // 2), hb=A_GROUP, window=WINDOW, sink=prm,
                                                        name="swa_fwd", job=job))
    cqn = _rms_fwd(c_q, W["ev_cq_norm"], "ev_cq_norm")
    q_all = _mm(cqn, W["ev_w_uq"], name="ev_uq")
    ckvn = _rms_fwd(c_kv, W["ev_ckv_norm"], "ev_ckv_norm")
    kv_all = _mm(ckvn, W["ev_w_ukv"], name="ev_ukv")
    cos, sin = _rope_tables(S)
    cos8, sin8 = jnp.tile(cos, (1, B_HEADS)), jnp.tile(sin, (1, B_HEADS))
    q1, q2 = _rope(q_all[:, 512:640], q_all[:, 640:768], cos8, sin8, "ev_rope_q")
    k1, k2 = _rope(kr1, kr2, cos, sin, "ev_rope_k")
    half = B_ROPE // 2
    scale = (B_NOPE + B_ROPE) ** -0.5
    qbT = _cols_only(jnp.concatenate([q_all[:, :512].reshape(S, B_HEADS, B_NOPE), q1.reshape(S, B_HEADS, half), q2.reshape(S, B_HEADS, half)], axis=-1) * scale)
    kro = jnp.broadcast_to(jnp.concatenate([k1, k2], axis=1)[:, None, :], (S, B_HEADS, B_ROPE))
    kb, kbT = _rows_and_cols(jnp.concatenate([kv_all[:, :512].reshape(S, B_HEADS, B_NOPE), kro], axis=-1))
    vb3 = kv_all[:, 512:].reshape(S, B_HEADS, B_V)
    vb = jnp.transpose(vb3.astype(BF16), (1, 0, 2))
    obT, lse_b = _with_ride(ride, lambda job: _attn_fwd(qbT, kb, _v_with_ones(vb3), tile=min(ATTN_TILE_FWD, S), hb=2, name="mla_fwd", job=job))
    cat = jnp.concatenate([_from_T(oaT), _from_T(obT)], axis=1)
    out = _mm_w128(cat, W["B"][0], MIX_OUT_BLK, res=h, name="ev_out")
    return out, (hn, proj, (qaT, ka, kaT, va, oaT, lse_a), prm, cqn, ckvn, (qbT, kb, kbT, vb, obT, lse_b), cat)


def _even_bwd(dout, saved, W, GB, norm):
    hn, proj, (qaT, ka, kaT, va, oaT, lse_a), prm, cqn, ckvn, (qbT, kb, kbT, vb, obT, lse_b), cat = saved
    S = hn.shape[0]
    G = {}
    dcat = _mm_w128(dout, W["B"][0], MIX_OUT_BLK, tb=True, out=BF16, name="ev_out_dx")
    GB["B"][0] = _mm_w128_dw(cat, dout, MIX_OUT_BLK, GB["B"][0], "ev_out_dw")
    doaT = _cols_only(dcat[:, :512].reshape(S, A_HEADS, A_HEAD_DIM))
    dqaT, dka, dva, dsink = _attn_bwd(qaT, ka, kaT, va, oaT, doaT, lse_a, tile=min(SWA_TILE, S // 2), hb=A_GROUP, window=WINDOW, sink=prm, real=A_HEAD_DIM,
                                       name="swa_bwd")
    G["ev_sinks"] = dsink[:, 0, 0]
    dqa = _from_T(dqaT) * A_HEAD_DIM ** -0.5
    dka = dka.reshape(A_KV_HEADS, A_GROUP, S, A_HEAD_DIM).sum(axis=1)
    dva = dva.reshape(A_KV_HEADS, A_GROUP, S, A_HEAD_DIM).sum(axis=1)
    dobT = _cols_only(dcat[:, 512:].reshape(S, B_HEADS, B_V))
    dqbT, dkb, dvb = _attn_bwd(qbT, kb, kbT, vb, obT, dobT, lse_b, tile=min(ATTN_TILE, S), hb=2, name="mla_bwd")
    half = B_ROPE // 2
    dqb = jnp.transpose(dqbT, (2, 0, 1)) * (B_NOPE + B_ROPE) ** -0.5
    dkb = jnp.transpose(dkb, (1, 0, 2))
    cos, sin = _rope_tables(S)
    cos8, sin8 = jnp.tile(cos, (1, B_HEADS)), jnp.tile(sin, (1, B_HEADS))
    dq1, dq2 = _rope(dqb[:, :, B_NOPE:B_NOPE + half].reshape(S, -1), dqb[:, :, B_NOPE + half:].reshape(S, -1), cos8, -sin8, "ev_rope_q_bwd")
    dq_all = jnp.concatenate([dqb[:, :, :B_NOPE].reshape(S, -1), dq1, dq2], axis=1).astype(BF16)
    dkr = dkb[:, :, B_NOPE:].sum(axis=1)
    dk1, dk2 = _rope(dkr[:, :half], dkr[:, half:], cos, -sin, "ev_rope_k_bwd")
    dkv_all = jnp.concatenate([dkb[:, :, :B_NOPE].reshape(S, -1), _unheads(dvb)], axis=1).astype(BF16)
    G["ev_w_uq"] = _mm(cqn, dq_all, ta=True, name="ev_uq_dw")
    dcqn = _mm(dq_all, W["ev_w_uq"], tb=True, name="ev_uq_dx")
    dc_q, G["ev_cq_norm"] = _rms_bwd(dcqn, proj[:, 768:1024], W["ev_cq_norm"], None, "ev_cq_norm_bwd")
    G["ev_w_ukv"] = _mm(ckvn, dkv_all, ta=True, name="ev_ukv_dw")
    dckvn = _mm(dkv_all, W["ev_w_ukv"], tb=True, name="ev_ukv_dx")
    dc_kv, G["ev_ckv_norm"] = _rms_bwd(dckvn, proj[:, 1024:1152], W["ev_ckv_norm"], None, "ev_ckv_norm_bwd")
    dproj = jnp.concatenate([dqa, _unheads(dka), _unheads(dva), dc_q, dc_kv, dk1, dk2,
                             jnp.zeros((S, EVEN_IN_PAD - EVEN_IN), F32)], axis=1).astype(BF16)
    G["ev_w_in"] = _mm(hn, dproj, ta=True, name="ev_in_dw")
    dh, dnorm = _mm(dproj, W["ev_w_in"], tb=True, norm_bwd=(*norm, dout), name="ev_in_dx")
    return dh, dnorm, G


def _odd_fwd(hn, h, W, ride=None):
    S = hn.shape[0]
    w = C_HEADS * C_HEAD_DIM
    wp = C_HEADS * QK_PAD
    proj = _mm(hn, W["od_w_in"], name="od_in")
    f_logit = proj[:, 2 * wp + w: 2 * wp + w + C_HEADS]
    logf = _logsig_fwd(f_logit, W["od_b_f"], "od_logsig")
    logc = _cumsum(logf, False, "od_cumsum")
    parts = list(_exact3(logc))
    ones = [jnp.ones((S, C_HEADS), F32)] * 3
    pad = [jnp.zeros((S, C_HEADS), F32)] * (QK_PAD - C_HEAD_DIM - 6)
    lead = ((0, 0), (0, 0), (C_HEAD_DIM, 0))
    q3 = proj[:, :wp].reshape(S, C_HEADS, QK_PAD) + jnp.pad(jnp.stack(parts + ones + pad, axis=-1), lead)
    k3 = proj[:, wp:2 * wp].reshape(S, C_HEADS, QK_PAD) + jnp.pad(jnp.stack(ones + [-p for p in parts] + pad, axis=-1), lead)
    qT = _cols_only(q3)
    k, kT = _rows_and_cols(k3)
    v3 = proj[:, 2 * wp:2 * wp + w].reshape(S, C_HEADS, C_HEAD_DIM)
    v = jnp.transpose(v3.astype(BF16), (1, 0, 2))
    oT, lse = _with_ride(ride, lambda job: _attn_fwd(qT, k, _v_with_ones(v3), tile=min(ATTN_TILE_FWD, S), hb=2, name="fox_fwd", job=job))
    cat = _from_T(oT)
    out = _mm_w128(cat, W["B"][1], MIX_OUT_BLK, res=h, name="od_out")
    return out, (hn, qT, k, kT, v, f_logit, oT, lse, cat)


def _odd_bwd(dout, saved, W, GB, norm):
    hn, qT, k, kT, v, f_logit, oT, lse, cat = saved
    S = hn.shape[0]
    G = {}
    dcat = _mm_w128(dout, W["B"][1], MIX_OUT_BLK, tb=True, out=BF16, name="od_out_dx")
    GB["B"][1] = _mm_w128_dw(cat, dout, MIX_OUT_BLK, GB["B"][1], "od_out_dw")
    doT = _cols_only(dcat.reshape(S, C_HEADS, C_HEAD_DIM))
    dqT, dk, dv, dqxT, dkx = _attn_bwd(qT, k, kT, v, oT, doT, lse, tile=min(ATTN_TILE, S), hb=2, real=C_HEAD_DIM, extra=True,
                                       full=True, name="fox_bwd")
    dlogc = jnp.transpose(dqxT[:, 0, :] - dkx[:, :, 3])
    dlogf = _cumsum(dlogc, True, "od_cumsum_bwd")
    df, db = _logsig_bwd(dlogf, f_logit, W["od_b_f"], "od_logsig_bwd")
    G["od_b_f"] = db
    dproj = jnp.concatenate([_from_T(dqT), _unheads(dk), _unheads(dv), df, jnp.zeros((S, ODD_IN_PAD - ODD_IN_AUG), F32)], axis=1).astype(BF16)
    G["od_w_in"] = _mm(hn, dproj, ta=True, name="od_in_dw")
    dh, dnorm = _mm(dproj, W["od_w_in"], tb=True, norm_bwd=(*norm, dout), name="od_in_dx")
    return dh, dnorm, G


class _Rider:
    def __init__(self, steps, tag):
        self.steps, self.tag, self.count, self.result = steps, tag, 0, None
        self.job = next(steps)

    def __call__(self, got=None):
        if got is not None:
            return self._advance(list(got))
        job = self.job
        if isinstance(job, str):
            self._advance(None)
            return None
        return job

    def _advance(self, value):
        try:
            self.job = self.steps.send(value)
        except StopIteration as done:
            self.job, self.result = None, done.value

    def finish(self):
        while self.job is not None:
            if isinstance(self.job, str):
                self._advance(None)
                continue
            self.count += 1
            self(_comm_call(self.job, f"{self.tag}_{self.count}"))
        return self.result


def _gather_plan(W, slots):
    a0, b0, c0, m, a1, b1, c1 = (slots[key] for key in ("a0", "b0", "c0", "m", "a1", "b1", "c1"))
    (m,) = yield _gather_job([m])
    W.update(_misc_weights(m))
    (b0,) = yield _gather_job([b0])
    W["B"] = [b0]
    (c0,) = yield _gather_job([c0], rows=[(D_MODEL, D_MODEL)])
    W["C"] = [c0]
    a0, c1 = yield _gather_job([a0, c1], rows=[(DOWN_ROWS, DOWN_ROWS), (0, D_MODEL)])
    W["A"] = [a0]
    W["C"].append(c1)
    (a1,) = yield _gather_job([a1], rows=[(0, DOWN_ROWS)])
    W["A"].append(a1)
    for _ in range(3):
        yield "skip"
    a1, b1, c1 = yield _gather_job([a1, b1, c1], rows=[(DOWN_ROWS, DOWN_ROWS), None, (D_MODEL, D_MODEL)])
    W["A"][1], W["C"][1] = a1, c1
    W["B"].append(b1)


def _local_step(x, p, target, W, slots):
    h = x
    saved = []
    gather = _Rider(_gather_plan(W, slots), "all_gather_rest")
    for i in range(DEPTH):
        t = f"l{i}"
        h1, s_a = _ffn_fwd(h, W["ffa_norm"][i:i + 1], W, 0, i, f"{t}_ffa", gather)
        nm = _rms_fwd(h1, W["mix_norm"][i:i + 1], f"{t}_mix_norm")
        h2, s_m = (_even_fwd if i % 2 == 0 else _odd_fwd)(nm, h1, W, gather)
        h3, s_b = _ffn_fwd(h2, W["ffb_norm"][i:i + 1], W, 1, i, f"{t}_ffb", gather)
        npl = _rms_fwd(h3, W["ple_norm"][i:i + 1], f"{t}_ple_norm")
        gpre = _mm_w128(npl, W["B"][i], PLE_GATE_BLK, out=BF16, name=f"{t}_ple_gate")
        pp = _mm(p[i], W["ple_w_proj"][i], out=BF16, name=f"{t}_ple_proj")
        h4 = _ple_fwd(h3, gpre, pp, f"{t}_ple")
        saved.append((s_a, h1, s_m, s_b, h3, npl, gpre, pp))
        h = h4
    gather.finish()
    dh, g_final, loss_cols = _final_fwd_bwd(h, W["final_norm"], target, "final")
    G = {"final_norm": g_final}
    GB = {"A": [[None, None] for _ in range(DEPTH)], "C": [[None, None] for _ in range(DEPTH)],
          "B": [lax.empty((N_DEV, B_ROWS, D_MODEL), BF16) for _ in range(DEPTH)]}
    per_layer = {n: [None] * DEPTH for n in ("ffa_norm", "mix_norm", "ffb_norm", "ple_norm", "ple_w_proj")}
    scatter = scatter_mid = None
    for i in reversed(range(DEPTH)):
        t = f"l{i}"
        s_a, h1, s_m, s_b, h3, npl, gpre, pp = saved[i]
        dgpre, dpp = _ple_bwd(dh, gpre, pp, f"{t}_ple_bwd")
        per_layer["ple_w_proj"][i] = _mm(p[i], dpp, ta=True, name=f"{t}_ple_proj_dw")
        GB["B"][i] = _mm_w128_dw(npl, dgpre, PLE_GATE_BLK, GB["B"][i], f"{t}_ple_gate_dw")
        dh, per_layer["ple_norm"][i] = _mm_w128(dgpre, W["B"][i], PLE_GATE_BLK, tb=True, norm_bwd=(h3, W["ple_norm"][i:i + 1], dh),
                                                name=f"{t}_ple_gate_dx")
        dh, per_layer["ffb_norm"][i] = _ffn_bwd(dh, s_b, W["ffb_norm"][i:i + 1], W, GB, 1, i, f"{t}_ffb", scatter)
        dh, per_layer["mix_norm"][i], g_mix = (_even_bwd if i % 2 == 0 else _odd_bwd)(dh, s_m, W, GB, (h1, W["mix_norm"][i:i + 1]))
        G.update(g_mix)
        if i == 0:
            G["ple_w_proj"] = per_layer["ple_w_proj"]
            mid = [GB["A"][0][1], GB["C"][0][1], GB["B"][0], _misc_grads(G).astype(BF16)]
            scatter_mid = _Rider(_reduce_scatter_steps(mid, "mid"), "rs_mid")
        else:
            scatter_early = _Rider(_reduce_scatter_steps([GB["A"][i][1], GB["C"][i][1], GB["B"][i]], "early"), "rs_early")
        dh, per_layer["ffa_norm"][i] = _ffn_bwd(dh, s_a, W["ffa_norm"][i:i + 1], W, GB, 0, i, f"{t}_ffa", scatter_mid if i == 0 else scatter_early)
        if i == DEPTH - 1:
            scatter = _Rider(_reduce_scatter_steps([GB["A"][i][0], GB["C"][i][0]], "later"), "rs_later")
    for n in ("ffa_norm", "mix_norm", "ffb_norm", "ple_norm"):
        G[n] = jnp.concatenate(per_layer[n], axis=0)
    return loss_cols, dh, scatter_early.finish(), scatter.finish(), scatter_mid.finish(), [GB["A"][0][0], GB["C"][0][0]], G


def kernel(x, p, ffa_norm, ffa_w_gate_up, ffa_w_down, mix_norm, ffb_norm, ffb_w_gate_up, ffb_w_down, ple_norm, ple_w_gate, ple_w_proj, ev_w_in, ev_sinks, ev_cq_norm, ev_w_uq, ev_ckv_norm, ev_w_ukv, ev_w_out, od_w_in, od_b_f, od_w_out, final_norm, loss_target, m_ffa_norm, m_ffa_w_gate_up, m_ffa_w_down, m_mix_norm, m_ffb_norm, m_ffb_w_gate_up, m_ffb_w_down, m_ple_norm, m_ple_w_gate, m_ple_w_proj, m_ev_w_in, m_ev_sinks, m_ev_cq_norm, m_ev_w_uq, m_ev_ckv_norm, m_ev_w_ukv, m_ev_w_out, m_od_w_in, m_od_b_f, m_od_w_out, m_final_norm, v_ffa_norm, v_ffa_w_gate_up, v_ffa_w_down, v_mix_norm, v_ffb_norm, v_ffb_w_gate_up, v_ffb_w_down, v_ple_norm, v_ple_w_gate, v_ple_w_proj, v_ev_w_in, v_ev_sinks, v_ev_cq_norm, v_ev_w_uq, v_ev_ckv_norm, v_ev_w_ukv, v_ev_w_out, v_od_w_in, v_od_b_f, v_od_w_out, v_final_norm):
    given = dict(locals())
    w_in = {n: given[n] for n in WEIGHTS}

    layers, misc = _local_groups(w_in, BF16)
    (a0, b0, c0), (a1, b1, c1) = [[_in_slot(g) for g in layer] for layer in layers]
    a0, c0 = _comm_call(_gather_job([a0, c0], rows=[(0, DOWN_ROWS), (0, D_MODEL)]), "all_gather_first")
    W = {n: w_in[n] for n in SMALL}
    W["final_norm"] = final_norm.reshape(1, -1)
    W.update(A=[a0], C=[c0])
    slots = dict(a0=a0, b0=b0, c0=c0, m=_in_slot(misc), a1=a1, b1=b1, c1=c1)

    loss_cols, dx, r_early, r_later, r_mid, last, G = _local_step(x[0], p[:, 0], loss_target[0], W, slots)

    a1b, c1b, b1 = r_early
    a1f, c1f = r_later
    a0b, c0b, b0, r_misc = r_mid
    a0f, c0f = _reduce_scatter(last, "last")
    grads = _ungroup_local([[a0f, a0b], [a1f, a1b]], [b0, b1], [[c0f, c0b], [c1f, c1b]], r_misc)
    layout = [(n, int(np.prod(w_in[n].shape))) for n in SMALL]
    vec = jnp.concatenate([G[n].astype(F32).reshape(-1) for n, _ in layout] + [jnp.sum(loss_cols).reshape(1)])
    vec = jnp.pad(vec, (0, N_DEV * SMALL_COLS - vec.shape[0])).reshape(N_DEV, SMALL_COLS)
    vec = _all_reduce_small(vec).reshape(-1)
    off = 0
    for n, size in layout:
        grads[n] = vec[off: off + size].reshape(w_in[n].shape)
        off += size
    loss = vec[off]

    delta, new_m, new_v = {}, {}, {}
    for n in WEIGHTS:
        shp = w_in[n].shape
        as2d = (lambda a: a.reshape(1, -1)) if len(shp) == 1 else (lambda a: a)
        d, nm, nv = _adamw(as2d(w_in[n]), as2d(grads[n]), as2d(given["m_" + n]), as2d(given["v_" + n]), f"adamw_{n}")
        delta[n], new_m[n], new_v[n] = d.reshape(shp), nm.reshape(shp), nv.reshape(shp)
    return (loss, dx[None], *[grads[n] for n in WEIGHTS], *[delta[n] for n in WEIGHTS],
            *[new_m[n] for n in WEIGHTS], *[new_v[n] for n in WEIGHTS])
```

```python
import functools

import numpy as np
import jax
import jax.numpy as jnp
from jax import lax
from jax.experimental import pallas as pl
from jax.experimental.pallas import tpu as pltpu

F32 = jnp.float32
BF16 = jnp.bfloat16
MESH = pl.DeviceIdType.MESH

D_MODEL = 1024
D_FF = 2816
RMS_EPS = 1e-6
PLE_DIM = 256
A_HEADS, A_KV_HEADS, A_HEAD_DIM, WINDOW = 8, 2, 64, 128
A_GROUP = A_HEADS // A_KV_HEADS
B_HEADS, B_NOPE, B_ROPE, B_V = 8, 64, 32, 64
ROPE_THETA = 10000.0
C_HEADS, C_HEAD_DIM = 16, 64
EVEN_IN = 1184
EVEN_IN_PAD = 1280
ODD_IN_AUG = 2 * 16 * 80 + 1024 + 16
ODD_IN_PAD = 3840
DEPTH = 2
ADAM_LR, ADAM_B1, ADAM_B2, ADAM_EPS, ADAM_WD, ADAM_STEP = 0.001, 0.9, 0.999, 1e-08, 0.01, 10

N_DEV = 8
LANES = 128
EW_TILE_BYTES = 3 << 20
MM_VMEM_BYTES = 26 << 20
NEG = -1e30
ATTN_TILE = 1024
ATTN_TILE_FWD = 1024
SWA_TILE = 512
QK_PAD = 80

FF_BLK = D_FF // 4
DOWN_ROWS = D_FF // N_DEV
B_ROWS, C_ROWS, G3_ROWS, G3_COLS = 256, 2 * D_MODEL, 1024, 768
PLE_GATE_BLK, MIX_OUT_BLK = 0, 1
OD_C, EV_C, STRIP_C = 386, 148, 128
STRIP0 = OD_C + EV_C

SMALL = ["ffa_norm", "mix_norm", "ffb_norm", "ple_norm", "ev_sinks", "ev_cq_norm", "ev_ckv_norm", "od_b_f", "final_norm"]
WEIGHTS = ["ffa_norm", "ffa_w_gate_up", "ffa_w_down", "mix_norm", "ffb_norm", "ffb_w_gate_up", "ffb_w_down", "ple_norm",
           "ple_w_gate", "ple_w_proj", "ev_w_in", "ev_sinks", "ev_cq_norm", "ev_w_uq", "ev_ckv_norm", "ev_w_ukv", "ev_w_out",
           "od_w_in", "od_b_f", "od_w_out", "final_norm"]
SMALL_COLS = 1280


def _divisor(n, cap, mult):
    if n <= cap:
        return n
    for t in range(cap - cap % mult, 0, -mult):
        if n % t == 0:
            return t
    raise ValueError(f"no tile for {n} under {cap} in steps of {mult}")


def _lanes(c):
    return -(-c // LANES) * LANES


def _ew(fn, rows, vecs, outs, reds=(), *, name):
    R = rows[0].shape[0]
    per_row = sum(_lanes(a.shape[1]) * a.dtype.itemsize for a in rows) + sum(_lanes(c) * jnp.dtype(d).itemsize for c, d in outs)
    tm = _divisor(R, max(16, EW_TILE_BYTES // per_row // 16 * 16), 16) if R % 16 == 0 else R
    n_r, n_v, n_o = len(rows), len(vecs), len(outs)

    def body(*refs):
        ins = [r[...] for r in refs[: n_r + n_v]]
        res = fn(*ins)
        if not isinstance(res, (tuple, list)):
            res = (res,)
        o_refs = refs[n_r + n_v: n_r + n_v + n_o]
        r_refs = refs[n_r + n_v + n_o:]
        for ref, val in zip(o_refs, res[:n_o]):
            ref[...] = val.astype(ref.dtype)
        if r_refs:
            @pl.when(pl.program_id(0) == 0)
            def _():
                for ref in r_refs:
                    ref[...] = jnp.zeros_like(ref)
            for ref, val in zip(r_refs, res[n_o:]):
                ref[...] += val

    in_specs = [pl.BlockSpec((tm, a.shape[1]), lambda i: (i, 0)) for a in rows]
    in_specs += [pl.BlockSpec((1, a.shape[1]), lambda i: (0, 0)) for a in vecs]
    out_specs = [pl.BlockSpec((tm, c), lambda i: (i, 0)) for c, _ in outs]
    out_specs += [pl.BlockSpec((1, c), lambda i: (0, 0)) for c in reds]
    out_shape = [jax.ShapeDtypeStruct((R, c), d) for c, d in outs] + [jax.ShapeDtypeStruct((1, c), F32) for c in reds]
    res = pl.pallas_call(body, name=name, grid=(R // tm,), in_specs=in_specs, out_specs=out_specs, out_shape=out_shape)(*rows, *vecs)
    return res[0] if len(res) == 1 else res


def _rms_fwd(x, w, name):
    def fn(x, w):
        y = x * lax.rsqrt(jnp.mean(x * x, axis=-1, keepdims=True) + RMS_EPS)
        return y * w
    return _ew(fn, [x], [w], [(x.shape[1], BF16)], name=name)


def _rms_bwd(dn, x, w, dres, name):
    def fn(dn, x, *rest):
        w = rest[-1]
        r = lax.rsqrt(jnp.mean(x * x, axis=-1, keepdims=True) + RMS_EPS)
        xh = x * r
        gw = dn * w
        dx = r * (gw - xh * jnp.mean(gw * xh, axis=-1, keepdims=True))
        if len(rest) == 2:
            dx = dx + rest[0]
        return dx, jnp.sum(dn * xh, axis=0, keepdims=True)
    rows = [dn, x] + ([dres] if dres is not None else [])
    return _ew(fn, rows, [w], [(x.shape[1], F32)], [x.shape[1]], name=name)


def _ple_fwd(h, gpre, pp, name):
    return _ew(lambda h, g, q: h + jax.nn.sigmoid(g.astype(F32)) * q.astype(F32), [h, gpre, pp], [], [(h.shape[1], F32)], name=name)


def _ple_bwd(dh, gpre, pp, name):
    def fn(dh, g, q):
        sg = jax.nn.sigmoid(g.astype(F32))
        return dh * q.astype(F32) * (sg * (1.0 - sg)), dh * sg
    return _ew(fn, [dh, gpre, pp], [], [(dh.shape[1], BF16), (dh.shape[1], BF16)], name=name)


def _rope(x1, x2, cos, sin, name):
    c = x1.shape[1]
    return _ew(lambda a, b, co, si: (a * co - b * si, a * si + b * co), [x1, x2, cos, sin], [], [(c, F32), (c, F32)], name=name)


def _logsig_fwd(f, b, name):
    def fn(f, b):
        z = f + b
        return jnp.minimum(z, 0.0) - jnp.log(1.0 + jnp.exp(-jnp.abs(z)))
    return _ew(fn, [f], [b], [(f.shape[1], F32)], name=name)


def _logsig_bwd(dlogf, f, b, name):
    def fn(d, f, b):
        df = d * jax.nn.sigmoid(-(f + b))
        return df, jnp.sum(df, axis=0, keepdims=True)
    return _ew(fn, [dlogf, f], [b], [(f.shape[1], F32)], [f.shape[1]], name=name)


def _final_fwd_bwd(h, w, target, name):
    d = h.shape[1]

    def fn(h, t, w):
        r = lax.rsqrt(jnp.mean(h * h, axis=-1, keepdims=True) + RMS_EPS)
        xh = h * r
        y = xh * w
        err = y - t
        dy = err * (1.0 / d)
        gw = dy * w
        dx = r * (gw - xh * jnp.mean(gw * xh, axis=-1, keepdims=True))
        return dx, jnp.sum(dy * xh, axis=0, keepdims=True), jnp.sum(err * err, axis=0, keepdims=True) * (0.5 / d)
    return _ew(fn, [h, target], [w], [(d, F32)], [d, d], name=name)


def _adamw(w, g, m, v, name):
    shape = w.shape
    c = shape[-1]
    w2, g2, m2, v2 = (a.reshape(-1, c) for a in (w, g, m, v))

    def fn(w, g, m, v):
        m = ADAM_B1 * m + (1.0 - ADAM_B1) * g
        v = ADAM_B2 * v + (1.0 - ADAM_B2) * jnp.square(g)
        m_hat = m / (1.0 - ADAM_B1 ** ADAM_STEP)
        v_hat = v / (1.0 - ADAM_B2 ** ADAM_STEP)
        delta = -ADAM_LR * (m_hat / (jnp.sqrt(v_hat) + ADAM_EPS) + ADAM_WD * w)
        return delta, m, v
    d, nm, nv = _ew(fn, [w2, g2, m2, v2], [], [(c, F32)] * 3, name=name)
    return d.reshape(shape), nm.reshape(shape), nv.reshape(shape)


def _split3(v):
    hi = v.astype(BF16)
    r1 = v - hi.astype(F32)
    mid = r1.astype(BF16)
    lo = (r1 - mid.astype(F32)).astype(BF16)
    return hi, mid, lo


def _cumsum(x, reverse, name):
    S, C = x.shape
    tm = _divisor(S, 512, 16)
    nt = S // tm

    def body(x_ref, o_ref, carry):
        @pl.when(pl.program_id(0) == 0)
        def _():
            carry[...] = jnp.zeros_like(carry)
        r = lax.broadcasted_iota(jnp.int32, (tm, tm), 0)
        c = lax.broadcasted_iota(jnp.int32, (tm, tm), 1)
        tri = jnp.where((c >= r) if reverse else (c <= r), 1.0, 0.0).astype(BF16)
        xv = x_ref[...]
        acc = jnp.zeros((tm, C), F32)
        for part in _split3(xv):
            acc = acc + jnp.dot(tri, part, preferred_element_type=F32)
        o_ref[...] = acc + carry[...]
        carry[...] += jnp.sum(xv, axis=0, keepdims=True)

    idx = (lambda i: (nt - 1 - i, 0)) if reverse else (lambda i: (i, 0))
    return pl.pallas_call(
        body, name=name, grid=(nt,), in_specs=[pl.BlockSpec((tm, C), idx)], out_specs=pl.BlockSpec((tm, C), idx),
        out_shape=jax.ShapeDtypeStruct((S, C), F32), scratch_shapes=[pltpu.VMEM((1, C), F32)],
    )(x)


NN = (((1,), (0,)), ((), ()))
NT = (((1,), (1,)), ((), ()))
TN = (((0,), (0,)), ((), ()))

HBM_SPEC = pl.BlockSpec(memory_space=pl.ANY)


def _job_in_body(job, refs, n_in, n_out, n_scr, grid):
    if job is None:
        return refs[n_in:], lambda: None
    ji, jo = len(job["ins"]), len(job["outs"])
    j_in = refs[n_in: n_in + ji]
    pos = n_in + ji
    own = list(refs[pos: pos + n_out])
    pos += n_out
    j_out = refs[pos: pos + jo]
    pos += jo
    own += list(refs[pos: pos + n_scr])
    ss, rs = refs[-2], refs[-1]
    first = functools.reduce(jnp.logical_and, [pl.program_id(d) == 0 for d in range(len(grid))])
    last = functools.reduce(jnp.logical_and, [pl.program_id(d) == n - 1 for d, n in enumerate(grid)])

    @pl.when(first)
    def _():
        job["start"](j_in, j_out, ss, rs)

    def finish():
        @pl.when(last)
        def _():
            job["finish"](j_in, j_out, ss, rs)

    return own, finish


def _job_call(job, body, *, name, grid, in_specs, out_specs, out_shape, args, scratch_shapes, aliases, dimension_semantics):
    in_specs, out_specs, out_shape, args, scratch_shapes = list(in_specs), list(out_specs), list(out_shape), list(args), list(scratch_shapes)
    aliases = dict(aliases)
    if job is not None:
        for i, o in job["aliases"].items():
            aliases[len(args) + i] = len(out_shape) + o
        in_specs += [HBM_SPEC] * len(job["ins"])
        args += list(job["ins"])
        out_specs += [HBM_SPEC] * len(job["outs"])
        out_shape += list(job["outs"])
        scratch_shapes += [pltpu.SemaphoreType.DMA((job["n_sems"],)), pltpu.SemaphoreType.DMA((job["n_sems"],))]
    return pl.pallas_call(
        body, name=name, grid=grid, in_specs=in_specs, out_specs=out_specs, out_shape=out_shape,
        scratch_shapes=scratch_shapes, input_output_aliases=aliases,
        compiler_params=pltpu.CompilerParams(dimension_semantics=dimension_semantics),
    )(*args)


def _comm_call(job, name):
    def body(*refs):
        ji, jo = len(job["ins"]), len(job["outs"])
        job["start"](refs[:ji], refs[ji: ji + jo], refs[-2], refs[-1])
        job["finish"](refs[:ji], refs[ji: ji + jo], refs[-2], refs[-1])

    return pl.pallas_call(
        body, name=name, in_specs=[HBM_SPEC] * len(job["ins"]), out_specs=[HBM_SPEC] * len(job["outs"]), out_shape=list(job["outs"]),
        input_output_aliases=dict(job["aliases"]),
        scratch_shapes=[pltpu.SemaphoreType.DMA((job["n_sems"],)), pltpu.SemaphoreType.DMA((job["n_sems"],))],
    )(*job["ins"])


def _mm_call(name, grid, k_axis, a, a_spec, a2d, b, b_spec, b2d, dims, out_sds, out_spec, o2d, *,
             alpha=1.0, res=None, res_spec=None, into=None, job=None, norm_bwd=None):
    nk = grid[k_axis]
    n_in = 2 + (res is not None) + (into is not None) + (3 if norm_bwd is not None else 0)
    n_out = 2 if norm_bwd is not None else 1

    def body(*refs):
        a_ref, b_ref = refs[0], refs[1]
        res_ref = refs[2] if res is not None else None
        own, finish_job = _job_in_body(job, refs, n_in, n_out, 1, grid)
        o_ref, acc_ref = own[0], own[-1]
        k = pl.program_id(k_axis)

        @pl.when(k == 0)
        def _():
            acc_ref[...] = jnp.zeros_like(acc_ref)

        if norm_bwd is not None:
            x_ref, w_ref, dres_ref = refs[n_in - 3: n_in]
            dw_ref = own[1]

            @pl.when(functools.reduce(jnp.logical_and, [pl.program_id(d) == 0 for d in range(len(grid))]))
            def _():
                dw_ref[...] = jnp.zeros_like(dw_ref)

        av = a_ref[...].reshape(a2d).astype(BF16)
        bv = b_ref[...].reshape(b2d).astype(BF16)
        acc_ref[...] += lax.dot_general(av, bv, dims, preferred_element_type=F32)

        @pl.when(k == nk - 1)
        def _():
            r = acc_ref[...]
            if alpha != 1.0:
                r = r * alpha
            if res_ref is not None:
                r = res_ref[...].reshape(o2d) + r
            if norm_bwd is not None:
                x = x_ref[...]
                rs = lax.rsqrt(jnp.mean(x * x, axis=-1, keepdims=True) + RMS_EPS)
                xh = x * rs
                gw = r * w_ref[...]
                dw_ref[...] += jnp.sum(r * xh, axis=0, keepdims=True)
                r = dres_ref[...] + rs * (gw - xh * jnp.mean(gw * xh, axis=-1, keepdims=True))
            o_ref[...] = r.reshape(o_ref.shape).astype(o_ref.dtype)

        finish_job()

    in_specs, args = [a_spec, b_spec], [a, b]
    if res is not None:
        in_specs.append(res_spec)
        args.append(res)
    aliases = {}
    if into is not None:
        aliases = {len(args): 0}
        in_specs.append(pl.BlockSpec(memory_space=pl.ANY))
        args.append(into)
        out_sds = jax.ShapeDtypeStruct(into.shape, into.dtype)
    out_specs, out_shape = [out_spec], [out_sds]
    if norm_bwd is not None:
        vec = pl.BlockSpec((1, o2d[1]), lambda *_: (0, 0))
        in_specs += [out_spec, vec, out_spec]
        args += list(norm_bwd)
        out_specs.append(vec)
        out_shape.append(jax.ShapeDtypeStruct((1, o2d[1]), F32))
    serial = job is not None or norm_bwd is not None
    sem = tuple("arbitrary" if d == k_axis or serial else "parallel" for d in range(len(grid)))
    res_all = _job_call(
        job, body, name=name, grid=grid, in_specs=in_specs, out_specs=out_specs, out_shape=out_shape, args=args,
        scratch_shapes=[pltpu.VMEM(o2d, F32)], aliases=aliases, dimension_semantics=sem)
    own = res_all[0] if n_out == 1 else tuple(res_all[:n_out])
    return own if job is None else (own, res_all[n_out:])


def _mm(a, b, *, ta=False, tb=False, out=F32, res=None, alpha=1.0, norm_bwd=None, name):
    K, M = a.shape if ta else a.shape[::-1]
    N = b.shape[0] if tb else b.shape[1]
    assert (b.shape[1] if tb else b.shape[0]) == K, (a.shape, b.shape, ta, tb)
    tk = _divisor(K, 1024, LANES)
    tn = _divisor(N, 1408, LANES)
    assert norm_bwd is None or tn == N
    for cap in (1024, 512, 256, 128):
        tm = _divisor(M, cap, LANES if ta else 16)
        est = 2 * (tm * tk * a.dtype.itemsize + tk * tn * b.dtype.itemsize + tm * tn * jnp.dtype(out).itemsize)
        est += tm * tn * 4 + (2 * tm * tn * 4 if res is not None else 0) + (4 * tm * tn * 4 if norm_bwd is not None else 0)
        if est <= MM_VMEM_BYTES:
            break
    a_spec = pl.BlockSpec((tk, tm), lambda i, j, k: (k, i)) if ta else pl.BlockSpec((tm, tk), lambda i, j, k: (i, k))
    b_spec = pl.BlockSpec((tn, tk), lambda i, j, k: (j, k)) if tb else pl.BlockSpec((tk, tn), lambda i, j, k: (k, j))
    o_spec = pl.BlockSpec((tm, tn), lambda i, j, k: (i, j))
    dims = (((0 if ta else 1,), (1 if tb else 0,)), ((), ()))
    return _mm_call(name, (M // tm, N // tn, K // tk), 2, a, a_spec, (tk, tm) if ta else (tm, tk), b, b_spec,
                    (tn, tk) if tb else (tk, tn), dims, jax.ShapeDtypeStruct((M, N), out), o_spec, (tm, tn),
                    alpha=alpha, res=res, res_spec=o_spec, norm_bwd=norm_bwd)


def _w128_spec(blk):
    return pl.BlockSpec((N_DEV, 128, D_MODEL), lambda *_: (0, blk, 0))


def _mm_w128(a, G1, blk, *, tb=False, res=None, out=F32, norm_bwd=None, name):
    S = a.shape[0]
    tm = _divisor(S, 1024 if norm_bwd is None else 512, 16)
    row = pl.BlockSpec((tm, D_MODEL), lambda i, k: (i, 0))
    return _mm_call(name, (S // tm, 1), 1, a, row, (tm, D_MODEL), G1, _w128_spec(blk), (D_MODEL, D_MODEL), NT if tb else NN,
                    jax.ShapeDtypeStruct((S, D_MODEL), out), row, (tm, D_MODEL), res=res, res_spec=row, norm_bwd=norm_bwd)


def _mm_w128_dw(a, b, blk, into, name):
    S = a.shape[0]
    tk = _divisor(S, 1024, 16)
    row = pl.BlockSpec((tk, D_MODEL), lambda i, k: (k, 0))
    return _mm_call(name, (1, S // tk), 1, a, row, (tk, D_MODEL), b, row, (tk, D_MODEL), TN, None, _w128_spec(blk),
                    (D_MODEL, D_MODEL), into=into)


def _ffn_gate_up(h, norm_w, G2v, rb, name, job=None):
    S = h.shape[0]
    tm = _divisor(S, 1024, 16)
    grid = (S // tm, 4)

    def body(*refs):
        h_ref, nw_ref, w_ref = refs[:3]
        (n_ref, gu_ref, act_ref, n_scr), finish_job = _job_in_body(job, refs, 3, 3, 1, grid)

        @pl.when(pl.program_id(1) == 0)
        def _():
            x = h_ref[...]
            y = x * lax.rsqrt(jnp.mean(x * x, axis=-1, keepdims=True) + RMS_EPS)
            n_scr[...] = (y * nw_ref[...]).astype(BF16)
            n_ref[...] = n_scr[...]

        nv = n_scr[...]
        g = jnp.dot(nv, w_ref[0, 0], preferred_element_type=F32)
        u = jnp.dot(nv, w_ref[1, 0], preferred_element_type=F32)
        sg = jax.nn.sigmoid(g)
        silu = g * sg
        gu_ref[0, 0] = (u * (sg * (1.0 + g * (1.0 - sg)))).astype(BF16)
        gu_ref[1, 0] = silu.astype(BF16)
        act_ref[0] = (silu * u).astype(BF16)
        finish_job()

    row = pl.BlockSpec((tm, D_MODEL), lambda i, j: (i, 0))
    return _job_call(
        job, body, name=name, grid=grid,
        in_specs=[row, pl.BlockSpec((1, D_MODEL), lambda i, j: (0, 0)), pl.BlockSpec((2, 1, D_MODEL, FF_BLK), lambda i, j: (0, j, rb, 0))],
        out_specs=[row, pl.BlockSpec((2, 1, tm, FF_BLK), lambda i, j: (0, j, i, 0)), pl.BlockSpec((1, tm, FF_BLK), lambda i, j: (j, i, 0))],
        out_shape=[jax.ShapeDtypeStruct((S, D_MODEL), BF16), jax.ShapeDtypeStruct((2, 4, S, FF_BLK), BF16), jax.ShapeDtypeStruct((4, S, FF_BLK), BF16)],
        args=[h, norm_w, G2v], scratch_shapes=[pltpu.VMEM((tm, D_MODEL), BF16)], aliases={},
        dimension_semantics=("arbitrary" if job is not None else "parallel", "arbitrary"))


def _ffn_down(act, G1, ob, h, name, job=None):
    S = h.shape[0]
    tm = _divisor(S, 1024, 16)
    row = pl.BlockSpec((tm, D_MODEL), lambda i, k: (i, 0))
    return _mm_call(name, (S // tm, 4), 1, act, pl.BlockSpec((1, tm, FF_BLK), lambda i, k: (k, i, 0)), (tm, FF_BLK),
                    G1, pl.BlockSpec((2, DOWN_ROWS, D_MODEL), lambda i, k: (k, ob, 0)), (FF_BLK, D_MODEL), NN,
                    jax.ShapeDtypeStruct((S, D_MODEL), F32), row, (tm, D_MODEL), alpha=0.5, res=h, res_spec=row, job=job)


def _ffn_down_dx(dh, G1, ob, gu, name):
    S = dh.shape[0]
    tm = _divisor(S, 1024, 16)

    def body(dh_ref, w_ref, gu_ref, o_ref):
        w = w_ref[...].reshape(FF_BLK, D_MODEL)
        dact = lax.dot_general(dh_ref[...].astype(BF16), w, NT, preferred_element_type=F32) * 0.5
        o_ref[0, 0] = (dact * gu_ref[0, 0].astype(F32)).astype(BF16)
        o_ref[1, 0] = (dact * gu_ref[1, 0].astype(F32)).astype(BF16)

    blk = pl.BlockSpec((2, 1, tm, FF_BLK), lambda i, j: (0, j, i, 0))
    return pl.pallas_call(
        body, name=name, grid=(S // tm, 4),
        in_specs=[pl.BlockSpec((tm, D_MODEL), lambda i, j: (i, 0)), pl.BlockSpec((2, DOWN_ROWS, D_MODEL), lambda i, j: (j, ob, 0)), blk],
        out_specs=blk, out_shape=jax.ShapeDtypeStruct((2, 4, S, FF_BLK), BF16),
    )(dh, G1, gu)


def _ffn_down_dw(act, dh, name, job=None):
    S = dh.shape[0]
    tk = _divisor(S, 1024, 16)
    return _mm_call(name, (4, S // tk), 1, act, pl.BlockSpec((1, tk, FF_BLK), lambda j, k: (j, k, 0)), (tk, FF_BLK),
                    dh, pl.BlockSpec((tk, D_MODEL), lambda j, k: (k, 0)), (tk, D_MODEL), TN,
                    jax.ShapeDtypeStruct((N_DEV, DOWN_ROWS, D_MODEL), BF16),
                    pl.BlockSpec((2, DOWN_ROWS, D_MODEL), lambda j, k: (j, 0, 0)), (FF_BLK, D_MODEL), alpha=0.5, job=job)


def _ffn_gate_up_dw(n, dgu8, name, job=None):
    S = n.shape[0]
    tk = _divisor(S, 1024, 16)
    return _mm_call(name, (N_DEV, S // tk), 1, n, pl.BlockSpec((tk, D_MODEL), lambda b, k: (k, 0)), (tk, D_MODEL),
                    dgu8, pl.BlockSpec((1, tk, FF_BLK), lambda b, k: (b, k, 0)), (tk, FF_BLK), TN,
                    jax.ShapeDtypeStruct((N_DEV, D_MODEL, FF_BLK), BF16),
                    pl.BlockSpec((1, D_MODEL, FF_BLK), lambda b, k: (b, 0, 0)), (D_MODEL, FF_BLK), job=job)


def _ffn_gate_up_dx(dgu8, G2, rb, h, norm_w, dres, name, job=None):
    S = h.shape[0]
    tm = _divisor(S, 1024, 16)
    row = pl.BlockSpec((tm, D_MODEL), lambda i, k: (i, 0))
    return _mm_call(name, (S // tm, N_DEV), 1, dgu8, pl.BlockSpec((1, tm, FF_BLK), lambda i, k: (k, i, 0)), (tm, FF_BLK),
                    G2, pl.BlockSpec((1, D_MODEL, FF_BLK), lambda i, k: (k, rb, 0)), (D_MODEL, FF_BLK), NT,
                    jax.ShapeDtypeStruct((S, D_MODEL), F32), row, (tm, D_MODEL), norm_bwd=(h, norm_w, dres), job=job)


def _unheads(x):
    h, S, d = x.shape
    return jnp.transpose(x, (1, 0, 2)).reshape(S, h * d)


def _exact3(v):
    rnd = lambda a: lax.reduce_precision(a, exponent_bits=8, mantissa_bits=7)
    hi = rnd(v)
    mid = rnd(v - hi)
    return hi, mid, rnd(v - hi - mid)


def _causal_mask(st, q0, k0, window):
    dist = (q0 + lax.broadcasted_iota(jnp.int32, st.shape, 1)) - (k0 + lax.broadcasted_iota(jnp.int32, st.shape, 0))
    mask = dist >= 0
    if window is not None:
        mask = mask & (dist < window)
    return jnp.where(mask, st, NEG)


def _attn_fwd(qT, k, vT1, *, tile, hb, window=None, sink=None, name, job=None):
    H, dqk, S = qT.shape
    G = H // k.shape[0]
    dvp = vT1.shape[1]
    dv = dvp - 16
    tq = tk = tile
    assert H % hb == 0 and (G == 1 or G % hb == 0)
    kvb = hb if G == 1 else 1
    grid = (H // hb, S // tq)
    n_in = 3 + (sink is not None)

    def body(*refs):
        q_ref, k_ref, v_ref = refs[:3]
        (o_ref, lse_ref), finish_job = _job_in_body(job, refs, n_in, 2, 0, grid)
        i = pl.program_id(1)
        carry = []
        for a in range(hb):
            if sink is not None:
                carry.append(jnp.zeros((1, tq), F32) + refs[3][a, :, 0:1])
                carry.append(jnp.where(lax.broadcasted_iota(jnp.int32, (dvp, tq), 0) == dv, 1.0, 0.0))
            else:
                carry.append(jnp.full((1, tq), NEG, F32))
                carry.append(jnp.zeros((dvp, tq), F32))

        def step(j, carry, masked, off=None, keys=tk, q_from=0):
            off = pl.multiple_of(j * tk, tk) if off is None else off
            out = []
            for a in range(hb):
                m, acc = carry[2 * a], carry[2 * a + 1]
                kv = a if kvb > 1 else 0
                st = jnp.dot(k_ref[kv, pl.ds(off, keys), :], q_ref[a][:, q_from:], preferred_element_type=F32)
                if masked:
                    st = _causal_mask(st, i * tq + q_from, off, window)
                m_old, acc_old = m[:, q_from:], acc[:, q_from:]
                m_new = jnp.maximum(m_old, jnp.max(st, axis=0, keepdims=True))
                pt = jnp.exp(st - m_new).astype(BF16)
                acc_new = jnp.exp(m_old - m_new) * acc_old + jnp.dot(v_ref[kv, :, pl.ds(off, keys)], pt, preferred_element_type=F32)
                if q_from:
                    m_new = jnp.concatenate([m[:, :q_from], m_new], axis=1)
                    acc_new = jnp.concatenate([acc[:, :q_from], acc_new], axis=1)
                out += [m_new, acc_new]
            return tuple(out)

        carry = tuple(carry)
        if window is None:
            carry = lax.fori_loop(0, i, functools.partial(step, masked=False), carry)
            if tq % (2 * LANES) == 0:
                half = tq // 2
                carry = step(None, carry, True, off=pl.multiple_of(i * tq, tq), keys=half)
                carry = step(None, carry, True, off=pl.multiple_of(i * tq + half, half), keys=half, q_from=half)
            else:
                carry = step(i, carry, True)
        else:
            assert window % LANES == 0 and tq + window <= S
            carry = step(None, carry, True, off=pl.multiple_of(jnp.maximum(i * tq - window, 0), LANES), keys=tq + window)
        for a in range(hb):
            m, acc = carry[2 * a], carry[2 * a + 1]
            l = acc[dv:dv + 1, :]
            o_ref[a] = acc[:dv, :] / l
            lse_ref[a] = m + jnp.log(l)
        finish_job()

    kv_idx = (lambda b: b) if G == 1 else (lambda b: (b * hb) // G)
    in_specs = [
        pl.BlockSpec((hb, dqk, tq), lambda b, i: (b, 0, i)),
        pl.BlockSpec((kvb, S, dqk), lambda b, i: (kv_idx(b), 0, 0)),
        pl.BlockSpec((kvb, dvp, S), lambda b, i: (kv_idx(b), 0, 0)),
    ]
    args = [qT, k, vT1]
    if sink is not None:
        in_specs += [pl.BlockSpec((hb, 1, LANES), lambda b, i: (b, 0, 0))]
        args += [sink]
    return _job_call(
        job, body, name=name, grid=grid, in_specs=in_specs,
        out_specs=[pl.BlockSpec((hb, dv, tq), lambda b, i: (b, 0, i)), pl.BlockSpec((hb, 1, tq), lambda b, i: (b, 0, i))],
        out_shape=[jax.ShapeDtypeStruct((H, dv, S), F32), jax.ShapeDtypeStruct((H, 1, S), F32)],
        args=args, scratch_shapes=[], aliases={}, dimension_semantics=("arbitrary", "arbitrary") if job is not None else ("parallel", "parallel"))


def _attn_bwd(qT, k, kT, v, oT, doT, lse, *, tile, hb, window=None, sink=None, real=None, extra=False, full=False, name):
    H, dqk, S = qT.shape
    G = H // k.shape[0]
    dv = v.shape[2]
    tq = tk = tile
    nq = S // tq
    has_p = sink is not None
    real = dqk if real is None else real
    main = dqk if full else real
    assert H % hb == 0 and (G == 1 or G % hb == 0) and not (extra and real == dqk)
    kvb = hb if G == 1 else 1

    def body(*refs):
        qT_ref, k_ref, kT_ref, v_ref, oT_ref, doT_ref, lse_ref = refs[:7]
        p_ref = refs[7] if has_p else None
        pos = 8 if has_p else 7
        dq_ref, dk_ref, dv_ref = refs[pos: pos + 3]
        pos += 3
        ds_ref = refs[pos] if has_p else None
        pos += has_p
        dqx_ref, dkx_ref = (refs[pos], refs[pos + 1]) if extra else (None, None)
        delta = refs[-1]
        j = pl.program_id(1)

        @pl.when(j == 0)
        def _():
            dq_ref[...] = jnp.zeros_like(dq_ref)
            if extra:
                dqx_ref[...] = jnp.zeros_like(dqx_ref)
            for a in range(hb):
                drow = jnp.sum(doT_ref[a].astype(F32) * oT_ref[a], axis=0, keepdims=True)
                delta[a] = drow
                if has_p:
                    w = jnp.exp(p_ref[a, :, 0:1] - lse_ref[a])
                    ds_ref[a] = jnp.zeros((1, LANES), F32) - jnp.sum(w * drow, axis=1, keepdims=True)

        def step(i, carry, masked, off=None, qs=tq, keys=tk):
            off = pl.multiple_of(i * tq, tq) if off is None else off
            out = []
            for a in range(hb):
                dk, dvv = carry[2 * a], carry[2 * a + 1]
                kv = a if kvb > 1 else 0
                qTi = qT_ref[a, :, pl.ds(off, qs)]
                doTi = doT_ref[a, :, pl.ds(off, qs)]
                st = jnp.dot(k_ref[kv, pl.ds(0, keys), :], qTi, preferred_element_type=F32)
                if masked:
                    st = _causal_mask(st, off, j * tk, window)
                pt = jnp.exp(st - lse_ref[a, :, pl.ds(off, qs)])
                dv_new = lax.dot_general(pt.astype(BF16), doTi, NT, preferred_element_type=F32)
                dpt = jnp.dot(v_ref[kv, pl.ds(0, keys), :], doTi, preferred_element_type=F32)
                dsb = (pt * (dpt - delta[a, :, pl.ds(off, qs)])).astype(BF16)
                dk_new = lax.dot_general(dsb, qTi, NT, preferred_element_type=F32)
                if keys < tk:
                    dk = jnp.concatenate([dk[:keys] + dk_new, dk[keys:]], axis=0)
                    dvv = jnp.concatenate([dvv[:keys] + dv_new, dvv[keys:]], axis=0)
                else:
                    dk, dvv = dk + dk_new, dvv + dv_new
                dqt = jnp.dot(kT_ref[kv, :, pl.ds(0, keys)], dsb, preferred_element_type=F32)
                dq_ref[a, :, pl.ds(off, qs)] += dqt[:main]
                if extra:
                    dqx_ref[a, :, pl.ds(off, qs)] += dqt[real:]
                out += [dk, dvv]
            return tuple(out)

        carry = (jnp.zeros((tk, dqk), F32), jnp.zeros((tk, dv), F32)) * hb
        if window is None:
            if tk % (2 * LANES) == 0:
                half = tk // 2
                carry = step(None, carry, True, off=pl.multiple_of(j * tk + half, half), qs=half)
                carry = step(None, carry, True, off=pl.multiple_of(j * tk, tk), qs=half, keys=half)
            else:
                carry = step(j, carry, True)
            carry = lax.fori_loop(j + 1, nq, functools.partial(step, masked=False), carry)
        else:
            assert window % LANES == 0 and tk + window <= S
            carry = step(None, carry, True, off=pl.multiple_of(jnp.minimum(j * tk, S - (tk + window)), LANES), qs=tk + window)
        for a in range(hb):
            dk_ref[a] = carry[2 * a][:, :main]
            if extra:
                dkx_ref[a] = carry[2 * a][:, real:]
            dv_ref[a] = carry[2 * a + 1]

    kv_idx = (lambda b: b) if G == 1 else (lambda b: (b * hb) // G)
    colsT = lambda d: pl.BlockSpec((hb, d, S), lambda b, j: (b, 0, 0))
    in_specs = [
        colsT(dqk),
        pl.BlockSpec((kvb, tk, dqk), lambda b, j: (kv_idx(b), j, 0)),
        pl.BlockSpec((kvb, dqk, tk), lambda b, j: (kv_idx(b), 0, j)),
        pl.BlockSpec((kvb, tk, dv), lambda b, j: (kv_idx(b), j, 0)),
        colsT(dv), colsT(dv),
        pl.BlockSpec((hb, 1, S), lambda b, j: (b, 0, 0)),
    ]
    args = [qT, k, kT, v, oT, doT, lse]
    if has_p:
        in_specs += [pl.BlockSpec((hb, 1, LANES), lambda b, j: (b, 0, 0))]
        args += [sink]
    out_specs = [colsT(main), pl.BlockSpec((hb, tk, main), lambda b, j: (b, j, 0)), pl.BlockSpec((hb, tk, dv), lambda b, j: (b, j, 0))]
    out_shape = [jax.ShapeDtypeStruct((H, main, S), F32), jax.ShapeDtypeStruct((H, S, main), F32), jax.ShapeDtypeStruct((H, S, dv), F32)]
    if has_p:
        out_specs += [pl.BlockSpec((hb, 1, LANES), lambda b, j: (b, 0, 0))]
        out_shape += [jax.ShapeDtypeStruct((H, 1, LANES), F32)]
    if extra:
        out_specs += [colsT(dqk - real), pl.BlockSpec((hb, tk, dqk - real), lambda b, j: (b, j, 0))]
        out_shape += [jax.ShapeDtypeStruct((H, dqk - real, S), F32), jax.ShapeDtypeStruct((H, S, dqk - real), F32)]
    return pl.pallas_call(
        body, name=name, grid=(H // hb, S // tk), in_specs=in_specs, out_specs=out_specs, out_shape=out_shape,
        scratch_shapes=[pltpu.VMEM((hb, 1, S), F32)],
        compiler_params=pltpu.CompilerParams(dimension_semantics=("parallel", "arbitrary")),
    )(*args)


def _rows_and_cols(x3):
    xb = x3.astype(BF16)
    return jnp.transpose(xb, (1, 0, 2)), jnp.transpose(xb, (1, 2, 0))


def _cols_only(x3):
    return jnp.transpose(x3.astype(BF16), (1, 2, 0))


def _v_with_ones(v3):
    S, h, _ = v3.shape
    vT = jnp.transpose(v3.astype(BF16), (1, 2, 0))
    return jnp.concatenate([vT, jnp.ones((h, 1, S), BF16), jnp.zeros((h, 15, S), BF16)], axis=1)


def _from_T(oT):
    h, d, S = oT.shape
    return jnp.transpose(oT, (2, 0, 1)).reshape(S, h * d)


def _coords():
    return lax.axis_index("x"), lax.axis_index("y"), lax.axis_index("c")


def _peer(axis):
    x, y, c = _coords()
    return {"x": (1 - x, y, c), "y": (x, 1 - y, c), "c": (x, y, 1 - c)}[axis]


def _gather_job(bufs, rows=None):
    n = len(bufs)

    def copies(outs, send_sems, recv_sems):
        x, y, c = _coords()
        me, sibling = (x, y, c), (x, y, 1 - c)
        chips = [(1 - x, y), (x, 1 - y), (1 - x, 1 - y)]

        def copy(t, k, block, to):
            px, py, pc = block
            ref = outs[t].at[4 * px + 2 * py + pc]
            if rows is not None and rows[t] is not None:
                ref = ref.at[pl.ds(rows[t][0], rows[t][1])]
            return pltpu.make_async_remote_copy(ref, ref, send_sems.at[7 * t + k], recv_sems.at[7 * t + k], device_id=to, device_id_type=MESH)

        return copy, me, sibling, chips, c

    def start(ins, outs, send_sems, recv_sems):
        copy, me, sibling, chips, c = copies(outs, send_sems, recv_sems)
        for t in range(n):
            copy(t, 0, me, sibling).start()
            for j, chip in enumerate(chips):
                copy(t, 1 + j, me, (*chip, c)).start()

    def finish(ins, outs, send_sems, recv_sems):
        copy, me, sibling, chips, c = copies(outs, send_sems, recv_sems)
        for j, chip in enumerate(chips):
            for t in range(n):
                copy(t, 1 + j, (*chip, c), me).wait_recv()
                copy(t, 4 + j, (*chip, c), sibling).start()
        for t in range(n):
            copy(t, 0, sibling, me).wait_recv()
            for j, chip in enumerate(chips):
                copy(t, 4 + j, (*chip, 1 - c), me).wait_recv()
        for t in range(n):
            copy(t, 0, me, sibling).wait_send()
            for j, chip in enumerate(chips):
                copy(t, 1 + j, me, (*chip, c)).wait_send()
                copy(t, 4 + j, (*chip, c), sibling).wait_send()

    return dict(ins=list(bufs), outs=[jax.ShapeDtypeStruct(b.shape, b.dtype) for b in bufs], aliases={t: t for t in range(n)},
                n_sems=7 * n, start=start, finish=finish)


def _in_slot(local):
    x, y, c = _coords()
    buf = lax.empty((N_DEV,) + local.shape, local.dtype)
    return lax.dynamic_update_slice(buf, local[None], (4 * x + 2 * y + c, 0, 0))


def _pair_job(vs, axes):
    n = len(vs)
    axes = [axes] * n if isinstance(axes, str) else axes

    def copies(ins, outs, send_sems, recv_sems):
        out = []
        for t in range(n):
            me = lax.axis_index(axes[t])
            src = ins[t].at[1 - me] if len(ins[t].shape) == 3 else ins[t].at[:, 1 - me]
            out.append(pltpu.make_async_remote_copy(src, outs[t], send_sems.at[t], recv_sems.at[t], device_id=_peer(axes[t]), device_id_type=MESH))
        return out

    def start(*refs):
        for cp in copies(*refs):
            cp.start()

    def finish(*refs):
        for cp in copies(*refs):
            cp.wait()

    return dict(ins=list(vs), outs=[jax.ShapeDtypeStruct(v.shape[:-3] + v.shape[-2:], v.dtype) for v in vs], aliases={}, n_sems=n,
                start=start, finish=finish)


def _add_kept(v, got, axis, out, name):
    R, C = v.shape[-2:]
    lead = v.shape[0] if v.ndim == 4 else 1
    tm = _divisor(R, max(16, EW_TILE_BYTES // (_lanes(C) * (v.dtype.itemsize + got.dtype.itemsize + jnp.dtype(out).itemsize)) // 16 * 16), 16)
    me = lax.axis_index(axis).astype(jnp.int32).reshape(1)
    v4 = v.reshape(lead, 2, R, C)
    g3 = got.reshape(lead, R, C)

    def body(me_ref, v_ref, g_ref, o_ref):
        o_ref[...] = (v_ref[0].astype(F32) + g_ref[...].astype(F32)).astype(o_ref.dtype)

    res = pl.pallas_call(
        body, name=name, out_shape=jax.ShapeDtypeStruct((lead, R, C), out),
        grid_spec=pltpu.PrefetchScalarGridSpec(
            num_scalar_prefetch=1, grid=(lead, R // tm),
            in_specs=[pl.BlockSpec((1, 1, tm, C), lambda b, i, me: (b, me[0], i, 0)), pl.BlockSpec((1, tm, C), lambda b, i, me: (b, i, 0))],
            out_specs=pl.BlockSpec((1, tm, C), lambda b, i, me: (b, i, 0))),
    )(me, v4, g3)
    return res


def _cross_job(vs):
    n = len(vs)

    def copies(ins, outs, send_sems, recv_sems):
        x, y, _ = _coords()
        out = []
        for t in range(n):
            h = ins[t].shape[2] // 2
            out.append(pltpu.make_async_remote_copy(ins[t].at[1 - x, :, pl.ds(0, h)], outs[2 * t], send_sems.at[2 * t], recv_sems.at[2 * t],
                                                    device_id=_peer("x"), device_id_type=MESH))
            out.append(pltpu.make_async_remote_copy(ins[t].at[:, 1 - y, pl.ds(h, h)], outs[2 * t + 1], send_sems.at[2 * t + 1], recv_sems.at[2 * t + 1],
                                                    device_id=_peer("y"), device_id_type=MESH))
        return out

    def start(*refs):
        for cp in copies(*refs):
            cp.start()

    def finish(*refs):
        for cp in copies(*refs):
            cp.wait()

    outs = []
    for v in vs:
        outs += [jax.ShapeDtypeStruct((2, v.shape[2] // 2, v.shape[3]), v.dtype)] * 2
    return dict(ins=list(vs), outs=outs, aliases={}, n_sems=2 * n, start=start, finish=finish)


def _add_picked(v, got, axis, out, name):
    _, _, R, C = v.shape
    h = R // 2
    tm = _divisor(h, max(16, EW_TILE_BYTES // (_lanes(C) * (v.dtype.itemsize + got.dtype.itemsize + jnp.dtype(out).itemsize)) // 16 * 16), 16)
    me = lax.axis_index(axis).astype(jnp.int32).reshape(1)
    if axis == "x":
        v_map = lambda b, i, me: (me[0], b, i, 0)
    else:
        v_map = lambda b, i, me: (b, me[0], i + h // tm, 0)

    def body(me_ref, v_ref, g_ref, o_ref):
        o_ref[...] = (v_ref[0].astype(F32) + g_ref[...].astype(F32)).astype(o_ref.dtype)

    return pl.pallas_call(
        body, name=name, out_shape=jax.ShapeDtypeStruct((2, h, C), out),
        grid_spec=pltpu.PrefetchScalarGridSpec(
            num_scalar_prefetch=1, grid=(2, h // tm),
            in_specs=[pl.BlockSpec((1, 1, tm, C), v_map), pl.BlockSpec((1, tm, C), lambda b, i, me: (b, i, 0))],
            out_specs=pl.BlockSpec((1, tm, C), lambda b, i, me: (b, i, 0))),
    )(me, v, got)


def _reduce_scatter_steps(gs, tag):
    n = len(gs)
    vs = [g.reshape(4, 2, *g.shape[1:]) for g in gs]
    got = yield _pair_job(vs, "c")
    vs = [_add_kept(v, r, "c", BF16, f"rs_{tag}_add_c{t}") for t, (v, r) in enumerate(zip(vs, got))]
    vs = [v.reshape(2, 2, v.shape[1], v.shape[2]) for v in vs]
    got = yield _cross_job(vs)
    up = [_add_picked(v, r, "x", BF16, f"rs_{tag}_add_x{t}") for t, (v, r) in enumerate(zip(vs, got[0::2]))]
    lo = [_add_picked(v, r, "y", BF16, f"rs_{tag}_add_y{t}") for t, (v, r) in enumerate(zip(vs, got[1::2]))]
    got = yield _pair_job(up + lo, ["y"] * n + ["x"] * n)
    out = []
    for t in range(n):
        a = _add_kept(up[t], got[t], "y", F32, f"rs_{tag}_add_y2{t}")[0]
        b = _add_kept(lo[t], got[n + t], "x", F32, f"rs_{tag}_add_x2{t}")[0]
        out.append(jnp.concatenate([a, b], axis=0))
    return out


def _reduce_scatter(gs, tag):
    steps = _reduce_scatter_steps(gs, tag)
    job = next(steps)
    for stage in ("c", "xy", "yx"):
        got = _comm_call(job, f"rs_{tag}_{stage}")
        try:
            job = steps.send(got)
        except StopIteration as done:
            return done.value


def _all_reduce_small(v):
    def body(v_ref, o_ref, buf, send_sems, recv_sems):
        x, y, c = _coords()
        me = 4 * x + 2 * y + c
        buf[me] = v_ref[...]
        copies = []
        for k in range(1, N_DEV):
            peer = tuple((1 - a) if (k >> s) & 1 else a for a, s in ((x, 2), (y, 1), (c, 0)))
            cp = pltpu.make_async_remote_copy(v_ref, buf.at[me], send_sems.at[k - 1], recv_sems.at[k - 1], device_id=peer, device_id_type=MESH)
            cp.start()
            copies.append(cp)
        for cp in copies:
            cp.wait()
        acc = buf[0]
        for d in range(1, N_DEV):
            acc = acc + buf[d]
        o_ref[...] = acc

    vm = pl.BlockSpec(memory_space=pltpu.VMEM)
    return pl.pallas_call(
        body, name="all_reduce_small", in_specs=[vm], out_specs=vm, out_shape=jax.ShapeDtypeStruct(v.shape, F32),
        scratch_shapes=[pltpu.VMEM((N_DEV,) + v.shape, F32), pltpu.SemaphoreType.DMA((N_DEV - 1,)), pltpu.SemaphoreType.DMA((N_DEV - 1,))],
    )(v)


def _local_groups(w, dtype):
    mix_out = [w["ev_w_out"][0], w["od_w_out"][0]]
    layers = []
    for l in range(DEPTH):
        a = jnp.concatenate([w["ffa_w_down"][l], w["ffb_w_down"][l]], axis=0).astype(dtype)
        b = jnp.concatenate([w["ple_w_gate"][l], mix_out[l]], axis=0).astype(dtype)
        c = jnp.concatenate([w["ffa_w_gate_up"][l], w["ffb_w_gate_up"][l]], axis=0).astype(dtype)
        layers.append((a, b, c))
    strip = jnp.concatenate([w["ple_w_proj"].reshape(-1, STRIP_C), w["ev_w_ukv"][0], jnp.pad(w["ev_w_uq"][0], ((0, 0), (0, STRIP_C - 96))),
                             jnp.zeros((G3_ROWS - 896, STRIP_C), F32)], axis=0)
    m = jnp.concatenate([w["od_w_in"][0], w["ev_w_in"][0], strip, jnp.zeros((G3_ROWS, G3_COLS - STRIP0 - STRIP_C), F32)], axis=1).astype(dtype)
    return layers, m


def _ungroup_local(a, b, c, r3):
    out = {
        "ffa_w_down": jnp.stack([x[0] for x in a]), "ffb_w_down": jnp.stack([x[1] for x in a]),
        "ple_w_gate": jnp.stack([x[:128] for x in b]), "ev_w_out": b[0][128:][None], "od_w_out": b[1][128:][None],
        "ffa_w_gate_up": jnp.stack([x[0] for x in c]), "ffb_w_gate_up": jnp.stack([x[1] for x in c]),
        "od_w_in": r3[:, :OD_C][None], "ev_w_in": r3[:, OD_C:STRIP0][None],
    }
    strip = r3[:, STRIP0:STRIP0 + STRIP_C]
    out["ple_w_proj"] = strip[:512].reshape(2, PLE_DIM, STRIP_C)
    out["ev_w_ukv"] = strip[512:640][None]
    out["ev_w_uq"] = strip[640:896, :96][None]
    return out


def _cols(a):
    return jnp.transpose(a, (1, 0, 2)).reshape(a.shape[1], -1)


def _blocks(g, c):
    return jnp.transpose(g.reshape(g.shape[0], N_DEV, c), (1, 0, 2))


def _uq_permute(w):
    r = w.shape[0]
    w3 = w.reshape(r, B_HEADS, B_NOPE + B_ROPE)
    half = B_ROPE // 2
    return jnp.concatenate([w3[:, :, :B_NOPE].reshape(r, -1), w3[:, :, B_NOPE:B_NOPE + half].reshape(r, -1), w3[:, :, B_NOPE + half:].reshape(r, -1)], axis=1)


def _uq_unpermute(g):
    r = g.shape[0]
    half = B_ROPE // 2
    n = B_HEADS * B_NOPE
    parts = [g[:, :n].reshape(r, B_HEADS, B_NOPE), g[:, n:n + B_HEADS * half].reshape(r, B_HEADS, half), g[:, n + B_HEADS * half:].reshape(r, B_HEADS, half)]
    return jnp.concatenate(parts, axis=2).reshape(r, -1)


def _ukv_permute(w):
    r = w.shape[0]
    return jnp.transpose(w.reshape(r, B_HEADS, 2, B_NOPE), (0, 2, 1, 3)).reshape(r, -1)


def _ukv_unpermute(g):
    r = g.shape[0]
    return jnp.transpose(g.reshape(r, 2, B_HEADS, B_NOPE), (0, 2, 1, 3)).reshape(r, -1)


def _od_in_widen(w):
    n = C_HEADS * C_HEAD_DIM
    wide = lambda m: jnp.pad(m.reshape(-1, C_HEADS, C_HEAD_DIM), ((0, 0), (0, 0), (0, QK_PAD - C_HEAD_DIM))).reshape(m.shape[0], -1)
    return jnp.concatenate([wide(w[:, :n] * C_HEAD_DIM ** -0.5), wide(w[:, n:2 * n]), w[:, 2 * n:],
                            jnp.zeros((w.shape[0], ODD_IN_PAD - ODD_IN_AUG), w.dtype)], axis=1)


def _od_in_narrow(g):
    wp = C_HEADS * QK_PAD
    narrow = lambda m: m.reshape(-1, C_HEADS, QK_PAD)[:, :, :C_HEAD_DIM].reshape(m.shape[0], -1)
    return jnp.concatenate([narrow(g[:, :wp]) * C_HEAD_DIM ** -0.5, narrow(g[:, wp:2 * wp]), g[:, 2 * wp:ODD_IN_AUG]], axis=1)


def _misc_weights(G3):
    strip = G3[:, :, STRIP0:STRIP0 + STRIP_C]
    return {
        "od_w_in": _od_in_widen(_cols(G3[:, :, :OD_C])),
        "ev_w_in": jnp.pad(_cols(G3[:, :, OD_C:STRIP0]), ((0, 0), (0, EVEN_IN_PAD - EVEN_IN))),
        "ple_w_proj": [_cols(strip[:, i * PLE_DIM:(i + 1) * PLE_DIM]) for i in range(DEPTH)],
        "ev_w_ukv": _ukv_permute(_cols(strip[:, 512:640])),
        "ev_w_uq": _uq_permute(_cols(strip[:, 640:896, :96])),
    }


def _misc_grads(G):
    strip = jnp.concatenate([
        _blocks(G["ple_w_proj"][0], STRIP_C), _blocks(G["ple_w_proj"][1], STRIP_C), _blocks(_ukv_unpermute(G["ev_w_ukv"]), STRIP_C),
        jnp.pad(_blocks(_uq_unpermute(G["ev_w_uq"]), 96), ((0, 0), (0, 0), (0, STRIP_C - 96))),
        jnp.zeros((N_DEV, G3_ROWS - 896, STRIP_C), F32)], axis=1)
    return jnp.concatenate([_blocks(_od_in_narrow(G["od_w_in"]), OD_C), _blocks(G["ev_w_in"][:, :EVEN_IN], EV_C), strip,
                            jnp.zeros((N_DEV, G3_ROWS, G3_COLS - STRIP0 - STRIP_C), F32)], axis=2)


def _ffn_fwd(h, norm_w, W, f, i, tag, ride=None):
    job = ride() if ride else None
    res = _ffn_gate_up(h, norm_w, W["C"][i].reshape(2, 4, C_ROWS, FF_BLK), f, f"{tag}_gate_up", job=job)
    n, gu, act = res[:3]
    if job is not None:
        ride(res[3:])
    job = ride() if ride else None
    out = _ffn_down(act, W["A"][i], f, h, f"{tag}_down", job=job)
    if job is not None:
        out, got = out
        ride(got)
    return out, (h, n, gu, act)


def _ffn_bwd(dout, saved, norm_w, W, GB, f, i, tag, ride=None):
    h, n, gu, act = saved
    S = h.shape[0]
    def carried(call):
        job = ride() if ride else None
        res = call(job)
        if job is None:
            return res
        ride(res[1])
        return res[0]

    GB["A"][i][f] = carried(lambda job: _ffn_down_dw(act, dout, f"{tag}_down_dw", job=job))
    dgu = _ffn_down_dx(dout, W["A"][i], f, gu, f"{tag}_down_dx").reshape(N_DEV, S, FF_BLK)
    res = carried(lambda job: _ffn_gate_up_dx(dgu, W["C"][i], f, h, norm_w, dout, f"{tag}_gate_up_dx", job=job))
    GB["C"][i][f] = carried(lambda job: _ffn_gate_up_dw(n, dgu, f"{tag}_gate_up_dw", job=job))
    return res


def _rope_tables(S):
    inv = ROPE_THETA ** (-jnp.arange(0, B_ROPE, 2, dtype=F32) / B_ROPE)
    ang = jnp.arange(S, dtype=F32)[:, None] * inv[None, :]
    return jnp.cos(ang), jnp.sin(ang)


def _alibi_columns(S):
    t = jnp.arange(S, dtype=jnp.int32)
    hi = ((t // 16) * 16).astype(F32)
    lo = (t % 16).astype(F32)
    slopes = 2.0 ** (-8.0 * jnp.arange(1, A_HEADS + 1, dtype=F32) / A_HEADS)
    zq = jnp.zeros((S, A_HEADS), F32)
    rest = QK_PAD - A_HEAD_DIM - 4
    qc = jnp.stack([-slopes[None, :] * hi[:, None], -slopes[None, :] * lo[:, None], zq + slopes[None, :], zq + slopes[None, :]] + [zq] * rest, axis=-1)
    one = jnp.ones((S, A_KV_HEADS), F32)
    zk = jnp.zeros((S, A_KV_HEADS), F32)
    kc = jnp.stack([one, one, zk + hi[:, None], zk + lo[:, None]] + [zk] * rest, axis=-1)
    return qc, kc


def _sink_prm(sinks):
    return jnp.zeros((A_HEADS, 1, LANES), F32).at[:, 0, 0].set(sinks.astype(F32))


def _with_ride(ride, call):
    job = ride() if ride else None
    res = call(job)
    if job is None:
        return res
    n_own = len(res) - len(job["outs"])
    ride(res[n_own:])
    return res[:n_own]


def _even_fwd(hn, h, W, ride=None):
    S = hn.shape[0]
    proj = _mm(hn, W["ev_w_in"], name="ev_in")
    a_q, a_k, a_v = proj[:, :512], proj[:, 512:640], proj[:, 640:768]
    c_q, c_kv = proj[:, 768:1024], proj[:, 1024:1152]
    kr1, kr2 = proj[:, 1152:1168], proj[:, 1168:1184]
    qc, kc = _alibi_columns(S)
    qaT = _cols_only(jnp.concatenate([(a_q * A_HEAD_DIM ** -0.5).reshape(S, A_HEADS, A_HEAD_DIM), qc], axis=-1))
    ka, kaT = _rows_and_cols(jnp.concatenate([a_k.reshape(S, A_KV_HEADS, A_HEAD_DIM), kc], axis=-1))
    va3 = a_v.reshape(S, A_KV_HEADS, A_HEAD_DIM)
    va = jnp.transpose(va3.astype(BF16), (1, 0, 2))
    prm = _sink_prm(W["ev_sinks"][0])
    oaT, lse_a = _with_ride(ride, lambda job: _attn_fwd(qaT, ka, _v_with_ones(va3), tile=min(SWA_TILE, S // 2), hb=A_GROUP, window=WINDOW, sink=prm,
                                                        name="swa_fwd", job=job))
    cqn = _rms_fwd(c_q, W["ev_cq_norm"], "ev_cq_norm")
    q_all = _mm(cqn, W["ev_w_uq"], name="ev_uq")
    ckvn = _rms_fwd(c_kv, W["ev_ckv_norm"], "ev_ckv_norm")
    kv_all = _mm(ckvn, W["ev_w_ukv"], name="ev_ukv")
    cos, sin = _rope_tables(S)
    cos8, sin8 = jnp.tile(cos, (1, B_HEADS)), jnp.tile(sin, (1, B_HEADS))
    q1, q2 = _rope(q_all[:, 512:640], q_all[:, 640:768], cos8, sin8, "ev_rope_q")
    k1, k2 = _rope(kr1, kr2, cos, sin, "ev_rope_k")
    half = B_ROPE // 2
    scale = (B_NOPE + B_ROPE) ** -0.5
    qbT = _cols_only(jnp.concatenate([q_all[:, :512].reshape(S, B_HEADS, B_NOPE), q1.reshape(S, B_HEADS, half), q2.reshape(S, B_HEADS, half)], axis=-1) * scale)
    kro = jnp.broadcast_to(jnp.concatenate([k1, k2], axis=1)[:, None, :], (S, B_HEADS, B_ROPE))
    kb, kbT = _rows_and_cols(jnp.concatenate([kv_all[:, :512].reshape(S, B_HEADS, B_NOPE), kro], axis=-1))
    vb3 = kv_all[:, 512:].reshape(S, B_HEADS, B_V)
    vb = jnp.transpose(vb3.astype(BF16), (1, 0, 2))
    obT, lse_b = _with_ride(ride, lambda job: _attn_fwd(qbT, kb, _v_with_ones(vb3), tile=min(ATTN_TILE_FWD, S), hb=4, name="mla_fwd", job=job))
    cat = jnp.concatenate([_from_T(oaT), _from_T(obT)], axis=1)
    out = _mm_w128(cat, W["B"][0], MIX_OUT_BLK, res=h, name="ev_out")
    return out, (hn, proj, (qaT, ka, kaT, va, oaT, lse_a), prm, cqn, ckvn, (qbT, kb, kbT, vb, obT, lse_b), cat)


def _even_bwd(dout, saved, W, GB, norm):
    hn, proj, (qaT, ka, kaT, va, oaT, lse_a), prm, cqn, ckvn, (qbT, kb, kbT, vb, obT, lse_b), cat = saved
    S = hn.shape[0]
    G = {}
    dcat = _mm_w128(dout, W["B"][0], MIX_OUT_BLK, tb=True, out=BF16, name="ev_out_dx")
    GB["B"][0] = _mm_w128_dw(cat, dout, MIX_OUT_BLK, GB["B"][0], "ev_out_dw")
    doaT = _cols_only(dcat[:, :512].reshape(S, A_HEADS, A_HEAD_DIM))
    dqaT, dka, dva, dsink = _attn_bwd(qaT, ka, kaT, va, oaT, doaT, lse_a, tile=min(SWA_TILE, S // 2), hb=A_GROUP, window=WINDOW, sink=prm, real=A_HEAD_DIM,
                                       name="swa_bwd")
    G["ev_sinks"] = dsink[:, 0, 0]
    dqa = _from_T(dqaT) * A_HEAD_DIM ** -0.5
    dka = dka.reshape(A_KV_HEADS, A_GROUP, S, A_HEAD_DIM).sum(axis=1)
    dva = dva.reshape(A_KV_HEADS, A_GROUP, S, A_HEAD_DIM).sum(axis=1)
    dobT = _cols_only(dcat[:, 512:].reshape(S, B_HEADS, B_V))
    dqbT, dkb, dvb = _attn_bwd(qbT, kb, kbT, vb, obT, dobT, lse_b, tile=min(ATTN_TILE, S), hb=2, name="mla_bwd")
    half = B_ROPE // 2
    dqb = jnp.transpose(dqbT, (2, 0, 1)) * (B_NOPE + B_ROPE) ** -0.5
    dkb = jnp.transpose(dkb, (1, 0, 2))
    cos, sin = _rope_tables(S)
    cos8, sin8 = jnp.tile(cos, (1, B_HEADS)), jnp.tile(sin, (1, B_HEADS))
    dq1, dq2 = _rope(dqb[:, :, B_NOPE:B_NOPE + half].reshape(S, -1), dqb[:, :, B_NOPE + half:].reshape(S, -1), cos8, -sin8, "ev_rope_q_bwd")
    dq_all = jnp.concatenate([dqb[:, :, :B_NOPE].reshape(S, -1), dq1, dq2], axis=1).astype(BF16)
    dkr = dkb[:, :, B_NOPE:].sum(axis=1)
    dk1, dk2 = _rope(dkr[:, :half], dkr[:, half:], cos, -sin, "ev_rope_k_bwd")
    dkv_all = jnp.concatenate([dkb[:, :, :B_NOPE].reshape(S, -1), _unheads(dvb)], axis=1).astype(BF16)
    G["ev_w_uq"] = _mm(cqn, dq_all, ta=True, name="ev_uq_dw")
    dcqn = _mm(dq_all, W["ev_w_uq"], tb=True, name="ev_uq_dx")
    dc_q, G["ev_cq_norm"] = _rms_bwd(dcqn, proj[:, 768:1024], W["ev_cq_norm"], None, "ev_cq_norm_bwd")
    G["ev_w_ukv"] = _mm(ckvn, dkv_all, ta=True, name="ev_ukv_dw")
    dckvn = _mm(dkv_all, W["ev_w_ukv"], tb=True, name="ev_ukv_dx")
    dc_kv, G["ev_ckv_norm"] = _rms_bwd(dckvn, proj[:, 1024:1152], W["ev_ckv_norm"], None, "ev_ckv_norm_bwd")
    dproj = jnp.concatenate([dqa, _unheads(dka), _unheads(dva), dc_q, dc_kv, dk1, dk2,
                             jnp.zeros((S, EVEN_IN_PAD - EVEN_IN), F32)], axis=1).astype(BF16)
    G["ev_w_in"] = _mm(hn, dproj, ta=True, name="ev_in_dw")
    dh, dnorm = _mm(dproj, W["ev_w_in"], tb=True, norm_bwd=(*norm, dout), name="ev_in_dx")
    return dh, dnorm, G


def _odd_fwd(hn, h, W, ride=None):
    S = hn.shape[0]
    w = C_HEADS * C_HEAD_DIM
    wp = C_HEADS * QK_PAD
    proj = _mm(hn, W["od_w_in"], name="od_in")
    f_logit = proj[:, 2 * wp + w: 2 * wp + w + C_HEADS]
    logf = _logsig_fwd(f_logit, W["od_b_f"], "od_logsig")
    logc = _cumsum(logf, False, "od_cumsum")
    parts = list(_exact3(logc))
    ones = [jnp.ones((S, C_HEADS), F32)] * 3
    pad = [jnp.zeros((S, C_HEADS), F32)] * (QK_PAD - C_HEAD_DIM - 6)
    lead = ((0, 0), (0, 0), (C_HEAD_DIM, 0))
    q3 = proj[:, :wp].reshape(S, C_HEADS, QK_PAD) + jnp.pad(jnp.stack(parts + ones + pad, axis=-1), lead)
    k3 = proj[:, wp:2 * wp].reshape(S, C_HEADS, QK_PAD) + jnp.pad(jnp.stack(ones + [-p for p in parts] + pad, axis=-1), lead)
    qT = _cols_only(q3)
    k, kT = _rows_and_cols(k3)
    v3 = proj[:, 2 * wp:2 * wp + w].reshape(S, C_HEADS, C_HEAD_DIM)
    v = jnp.transpose(v3.astype(BF16), (1, 0, 2))
    oT, lse = _with_ride(ride, lambda job: _attn_fwd(qT, k, _v_with_ones(v3), tile=min(ATTN_TILE_FWD, S), hb=4, name="fox_fwd", job=job))
    cat = _from_T(oT)
    out = _mm_w128(cat, W["B"][1], MIX_OUT_BLK, res=h, name="od_out")
    return out, (hn, qT, k, kT, v, f_logit, oT, lse, cat)


def _odd_bwd(dout, saved, W, GB, norm):
    hn, qT, k, kT, v, f_logit, oT, lse, cat = saved
    S = hn.shape[0]
    G = {}
    dcat = _mm_w128(dout, W["B"][1], MIX_OUT_BLK, tb=True, out=BF16, name="od_out_dx")
    GB["B"][1] = _mm_w128_dw(cat, dout, MIX_OUT_BLK, GB["B"][1], "od_out_dw")
    doT = _cols_only(dcat.reshape(S, C_HEADS, C_HEAD_DIM))
    dqT, dk, dv, dqxT, dkx = _attn_bwd(qT, k, kT, v, oT, doT, lse, tile=min(ATTN_TILE, S), hb=2, real=C_HEAD_DIM, extra=True,
                                       full=True, name="fox_bwd")
    dlogc = jnp.transpose(dqxT[:, 0, :] - dkx[:, :, 3])
    dlogf = _cumsum(dlogc, True, "od_cumsum_bwd")
    df, db = _logsig_bwd(dlogf, f_logit, W["od_b_f"], "od_logsig_bwd")
    G["od_b_f"] = db
    dproj = jnp.concatenate([_from_T(dqT), _unheads(dk), _unheads(dv), df, jnp.zeros((S, ODD_IN_PAD - ODD_IN_AUG), F32)], axis=1).astype(BF16)
    G["od_w_in"] = _mm(hn, dproj, ta=True, name="od_in_dw")
    dh, dnorm = _mm(dproj, W["od_w_in"], tb=True, norm_bwd=(*norm, dout), name="od_in_dx")
    return dh, dnorm, G


class _Rider:
    def __init__(self, steps, tag):
        self.steps, self.tag, self.count, self.result = steps, tag, 0, None
        self.job = next(steps)

    def __call__(self, got=None):
        if got is not None:
            return self._advance(list(got))
        job = self.job
        if isinstance(job, str):
            self._advance(None)
            return None
        return job

    def _advance(self, value):
        try:
            self.job = self.steps.send(value)
        except StopIteration as done:
            self.job, self.result = None, done.value

    def finish(self):
        while self.job is not None:
            if isinstance(self.job, str):
                self._advance(None)
                continue
            self.count += 1
            self(_comm_call(self.job, f"{self.tag}_{self.count}"))
        return self.result


def _gather_plan(W, slots):
    a0, b0, c0, m, a1, b1, c1 = (slots[key] for key in ("a0", "b0", "c0", "m", "a1", "b1", "c1"))
    (m,) = yield _gather_job([m])
    W.update(_misc_weights(m))
    (b0,) = yield _gather_job([b0])
    W["B"] = [b0]
    (c0,) = yield _gather_job([c0], rows=[(D_MODEL, D_MODEL)])
    W["C"] = [c0]
    a0, c1 = yield _gather_job([a0, c1], rows=[(DOWN_ROWS, DOWN_ROWS), (0, D_MODEL)])
    W["A"] = [a0]
    W["C"].append(c1)
    (a1,) = yield _gather_job([a1], rows=[(0, DOWN_ROWS)])
    W["A"].append(a1)
    for _ in range(3):
        yield "skip"
    a1, b1, c1 = yield _gather_job([a1, b1, c1], rows=[(DOWN_ROWS, DOWN_ROWS), None, (D_MODEL, D_MODEL)])
    W["A"][1], W["C"][1] = a1, c1
    W["B"].append(b1)


def _local_step(x, p, target, W, slots):
    h = x
    saved = []
    gather = _Rider(_gather_plan(W, slots), "all_gather_rest")
    for i in range(DEPTH):
        t = f"l{i}"
        h1, s_a = _ffn_fwd(h, W["ffa_norm"][i:i + 1], W, 0, i, f"{t}_ffa", gather)
        nm = _rms_fwd(h1, W["mix_norm"][i:i + 1], f"{t}_mix_norm")
        h2, s_m = (_even_fwd if i % 2 == 0 else _odd_fwd)(nm, h1, W, gather)
        h3, s_b = _ffn_fwd(h2, W["ffb_norm"][i:i + 1], W, 1, i, f"{t}_ffb", gather)
        npl = _rms_fwd(h3, W["ple_norm"][i:i + 1], f"{t}_ple_norm")
        gpre = _mm_w128(npl, W["B"][i], PLE_GATE_BLK, out=BF16, name=f"{t}_ple_gate")
        pp = _mm(p[i], W["ple_w_proj"][i], out=BF16, name=f"{t}_ple_proj")
        h4 = _ple_fwd(h3, gpre, pp, f"{t}_ple")
        saved.append((s_a, h1, s_m, s_b, h3, npl, gpre, pp))
        h = h4
    gather.finish()
    dh, g_final, loss_cols = _final_fwd_bwd(h, W["final_norm"], target, "final")
    G = {"final_norm": g_final}
    GB = {"A": [[None, None] for _ in range(DEPTH)], "C": [[None, None] for _ in range(DEPTH)],
          "B": [lax.empty((N_DEV, B_ROWS, D_MODEL), BF16) for _ in range(DEPTH)]}
    per_layer = {n: [None] * DEPTH for n in ("ffa_norm", "mix_norm", "ffb_norm", "ple_norm", "ple_w_proj")}
    scatter = scatter_mid = None
    for i in reversed(range(DEPTH)):
        t = f"l{i}"
        s_a, h1, s_m, s_b, h3, npl, gpre, pp = saved[i]
        dgpre, dpp = _ple_bwd(dh, gpre, pp, f"{t}_ple_bwd")
        per_layer["ple_w_proj"][i] = _mm(p[i], dpp, ta=True, name=f"{t}_ple_proj_dw")
        GB["B"][i] = _mm_w128_dw(npl, dgpre, PLE_GATE_BLK, GB["B"][i], f"{t}_ple_gate_dw")
        dh, per_layer["ple_norm"][i] = _mm_w128(dgpre, W["B"][i], PLE_GATE_BLK, tb=True, norm_bwd=(h3, W["ple_norm"][i:i + 1], dh),
                                                name=f"{t}_ple_gate_dx")
        dh, per_layer["ffb_norm"][i] = _ffn_bwd(dh, s_b, W["ffb_norm"][i:i + 1], W, GB, 1, i, f"{t}_ffb", scatter)
        dh, per_layer["mix_norm"][i], g_mix = (_even_bwd if i % 2 == 0 else _odd_bwd)(dh, s_m, W, GB, (h1, W["mix_norm"][i:i + 1]))
        G.update(g_mix)
        if i == 0:
            G["ple_w_proj"] = per_layer["ple_w_proj"]
            mid = [GB["A"][0][1], GB["C"][0][1], GB["B"][0], _misc_grads(G).astype(BF16)]
            scatter_mid = _Rider(_reduce_scatter_steps(mid, "mid"), "rs_mid")
        else:
            scatter_early = _Rider(_reduce_scatter_steps([GB["A"][i][1], GB["C"][i][1], GB["B"][i]], "early"), "rs_early")
        dh, per_layer["ffa_norm"][i] = _ffn_bwd(dh, s_a, W["ffa_norm"][i:i + 1], W, GB, 0, i, f"{t}_ffa", scatter_mid if i == 0 else scatter_early)
        if i == DEPTH - 1:
            scatter = _Rider(_reduce_scatter_steps([GB["A"][i][0], GB["C"][i][0]], "later"), "rs_later")
    for n in ("ffa_norm", "mix_norm", "ffb_norm", "ple_norm"):
        G[n] = jnp.concatenate(per_layer[n], axis=0)
    return loss_cols, dh, scatter_early.finish(), scatter.finish(), scatter_mid.finish(), [GB["A"][0][0], GB["C"][0][0]], G


def kernel(x, p, ffa_norm, ffa_w_gate_up, ffa_w_down, mix_norm, ffb_norm, ffb_w_gate_up, ffb_w_down, ple_norm, ple_w_gate, ple_w_proj, ev_w_in, ev_sinks, ev_cq_norm, ev_w_uq, ev_ckv_norm, ev_w_ukv, ev_w_out, od_w_in, od_b_f, od_w_out, final_norm, loss_target, m_ffa_norm, m_ffa_w_gate_up, m_ffa_w_down, m_mix_norm, m_ffb_norm, m_ffb_w_gate_up, m_ffb_w_down, m_ple_norm, m_ple_w_gate, m_ple_w_proj, m_ev_w_in, m_ev_sinks, m_ev_cq_norm, m_ev_w_uq, m_ev_ckv_norm, m_ev_w_ukv, m_ev_w_out, m_od_w_in, m_od_b_f, m_od_w_out, m_final_norm, v_ffa_norm, v_ffa_w_gate_up, v_ffa_w_down, v_mix_norm, v_ffb_norm, v_ffb_w_gate_up, v_ffb_w_down, v_ple_norm, v_ple_w_gate, v_ple_w_proj, v_ev_w_in, v_ev_sinks, v_ev_cq_norm, v_ev_w_uq, v_ev_ckv_norm, v_ev_w_ukv, v_ev_w_out, v_od_w_in, v_od_b_f, v_od_w_out, v_final_norm):
    given = dict(locals())
    w_in = {n: given[n] for n in WEIGHTS}

    layers, misc = _local_groups(w_in, BF16)
    (a0, b0, c0), (a1, b1, c1) = [[_in_slot(g) for g in layer] for layer in layers]
    a0, c0 = _comm_call(_gather_job([a0, c0], rows=[(0, DOWN_ROWS), (0, D_MODEL)]), "all_gather_first")
    W = {n: w_in[n] for n in SMALL}
    W["final_norm"] = final_norm.reshape(1, -1)
    W.update(A=[a0], C=[c0])
    slots = dict(a0=a0, b0=b0, c0=c0, m=_in_slot(misc), a1=a1, b1=b1, c1=c1)

    loss_cols, dx, r_early, r_later, r_mid, last, G = _local_step(x[0], p[:, 0], loss_target[0], W, slots)

    a1b, c1b, b1 = r_early
    a1f, c1f = r_later
    a0b, c0b, b0, r_misc = r_mid
    a0f, c0f = _reduce_scatter(last, "last")
    grads = _ungroup_local([[a0f, a0b], [a1f, a1b]], [b0, b1], [[c0f, c0b], [c1f, c1b]], r_misc)
    layout = [(n, int(np.prod(w_in[n].shape))) for n in SMALL]
    vec = jnp.concatenate([G[n].astype(F32).reshape(-1) for n, _ in layout] + [jnp.sum(loss_cols).reshape(1)])
    vec = jnp.pad(vec, (0, N_DEV * SMALL_COLS - vec.shape[0])).reshape(N_DEV, SMALL_COLS)
    vec = _all_reduce_small(vec).reshape(-1)
    off = 0
    for n, size in layout:
        grads[n] = vec[off: off + size].reshape(w_in[n].shape)
        off += size
    loss = vec[off]

    delta, new_m, new_v = {}, {}, {}
    for n in WEIGHTS:
        shp = w_in[n].shape
        as2d = (lambda a: a.reshape(1, -1)) if len(shp) == 1 else (lambda a: a)
        d, nm, nv = _adamw(as2d(w_in[n]), as2d(grads[n]), as2d(given["m_" + n]), as2d(given["v_" + n]), f"adamw_{n}")
        delta[n], new_m[n], new_v[n] = d.reshape(shp), nm.reshape(shp), nv.reshape(shp)
    return (loss, dx[None], *[grads[n] for n in WEIGHTS], *[delta[n] for n in WEIGHTS],
            *[new_m[n] for n in WEIGHTS], *[new_v[n] for n in WEIGHTS])
```

```python
import functools

import numpy as np
import jax
import jax.numpy as jnp
from jax import lax
from jax.experimental import pallas as pl
from jax.experimental.pallas import tpu as pltpu

F32 = jnp.float32
BF16 = jnp.bfloat16
MESH = pl.DeviceIdType.MESH

D_MODEL = 1024
D_FF = 2816
RMS_EPS = 1e-6
PLE_DIM = 256
A_HEADS, A_KV_HEADS, A_HEAD_DIM, WINDOW = 8, 2, 64, 128
A_GROUP = A_HEADS // A_KV_HEADS
B_HEADS, B_NOPE, B_ROPE, B_V = 8, 64, 32, 64
ROPE_THETA = 10000.0
C_HEADS, C_HEAD_DIM = 16, 64
EVEN_IN = 1184
EVEN_IN_PAD = 1280
ODD_IN_AUG = 2 * 16 * 80 + 1024 + 16
ODD_IN_PAD = 3840
DEPTH = 2
ADAM_LR, ADAM_B1, ADAM_B2, ADAM_EPS, ADAM_WD, ADAM_STEP = 0.001, 0.9, 0.999, 1e-08, 0.01, 10

N_DEV = 8
LANES = 128
EW_TILE_BYTES = 3 << 20
MM_VMEM_BYTES = 26 << 20
NEG = -1e30
ATTN_TILE = 1024
ATTN_TILE_FWD = 1024
SWA_TILE = 512
QK_PAD = 80

FF_BLK = D_FF // 4
DOWN_ROWS = D_FF // N_DEV
B_ROWS, C_ROWS, G3_ROWS, G3_COLS = 256, 2 * D_MODEL, 1024, 768
PLE_GATE_BLK, MIX_OUT_BLK = 0, 1
OD_C, EV_C, STRIP_C = 386, 148, 128
STRIP0 = OD_C + EV_C

SMALL = ["ffa_norm", "mix_norm", "ffb_norm", "ple_norm", "ev_sinks", "ev_cq_norm", "ev_ckv_norm", "od_b_f", "final_norm"]
WEIGHTS = ["ffa_norm", "ffa_w_gate_up", "ffa_w_down", "mix_norm", "ffb_norm", "ffb_w_gate_up", "ffb_w_down", "ple_norm",
           "ple_w_gate", "ple_w_proj", "ev_w_in", "ev_sinks", "ev_cq_norm", "ev_w_uq", "ev_ckv_norm", "ev_w_ukv", "ev_w_out",
           "od_w_in", "od_b_f", "od_w_out", "final_norm"]
SMALL_COLS = 1280


def _divisor(n, cap, mult):
    if n <= cap:
        return n
    for t in range(cap - cap % mult, 0, -mult):
        if n % t == 0:
            return t
    raise ValueError(f"no tile for {n} under {cap} in steps of {mult}")


def _lanes(c):
    return -(-c // LANES) * LANES


def _ew(fn, rows, vecs, outs, reds=(), *, name):
    R = rows[0].shape[0]
    per_row = sum(_lanes(a.shape[1]) * a.dtype.itemsize for a in rows) + sum(_lanes(c) * jnp.dtype(d).itemsize for c, d in outs)
    tm = _divisor(R, max(16, EW_TILE_BYTES // per_row // 16 * 16), 16) if R % 16 == 0 else R
    n_r, n_v, n_o = len(rows), len(vecs), len(outs)

    def body(*refs):
        ins = [r[...] for r in refs[: n_r + n_v]]
        res = fn(*ins)
        if not isinstance(res, (tuple, list)):
            res = (res,)
        o_refs = refs[n_r + n_v: n_r + n_v + n_o]
        r_refs = refs[n_r + n_v + n_o:]
        for ref, val in zip(o_refs, res[:n_o]):
            ref[...] = val.astype(ref.dtype)
        if r_refs:
            @pl.when(pl.program_id(0) == 0)
            def _():
                for ref in r_refs:
                    ref[...] = jnp.zeros_like(ref)
            for ref, val in zip(r_refs, res[n_o:]):
                ref[...] += val

    in_specs = [pl.BlockSpec((tm, a.shape[1]), lambda i: (i, 0)) for a in rows]
    in_specs += [pl.BlockSpec((1, a.shape[1]), lambda i: (0, 0)) for a in vecs]
    out_specs = [pl.BlockSpec((tm, c), lambda i: (i, 0)) for c, _ in outs]
    out_specs += [pl.BlockSpec((1, c), lambda i: (0, 0)) for c in reds]
    out_shape = [jax.ShapeDtypeStruct((R, c), d) for c, d in outs] + [jax.ShapeDtypeStruct((1, c), F32) for c in reds]
    res = pl.pallas_call(body, name=name, grid=(R // tm,), in_specs=in_specs, out_specs=out_specs, out_shape=out_shape)(*rows, *vecs)
    return res[0] if len(res) == 1 else res


def _rms_fwd(x, w, name):
    def fn(x, w):
        y = x * lax.rsqrt(jnp.mean(x * x, axis=-1, keepdims=True) + RMS_EPS)
        return y * w
    return _ew(fn, [x], [w], [(x.shape[1], BF16)], name=name)


def _rms_bwd(dn, x, w, dres, name):
    def fn(dn, x, *rest):
        w = rest[-1]
        r = lax.rsqrt(jnp.mean(x * x, axis=-1, keepdims=True) + RMS_EPS)
        xh = x * r
        gw = dn * w
        dx = r * (gw - xh * jnp.mean(gw * xh, axis=-1, keepdims=True))
        if len(rest) == 2:
            dx = dx + rest[0]
        return dx, jnp.sum(dn * xh, axis=0, keepdims=True)
    rows = [dn, x] + ([dres] if dres is not None else [])
    return _ew(fn, rows, [w], [(x.shape[1], F32)], [x.shape[1]], name=name)


def _ple_fwd(h, gpre, pp, name):
    return _ew(lambda h, g, q: h + jax.nn.sigmoid(g.astype(F32)) * q.astype(F32), [h, gpre, pp], [], [(h.shape[1], F32)], name=name)


def _ple_bwd(dh, gpre, pp, name):
    def fn(dh, g, q):
        sg = jax.nn.sigmoid(g.astype(F32))
        return dh * q.astype(F32) * (sg * (1.0 - sg)), dh * sg
    return _ew(fn, [dh, gpre, pp], [], [(dh.shape[1], BF16), (dh.shape[1], BF16)], name=name)


def _rope(x1, x2, cos, sin, name):
    c = x1.shape[1]
    return _ew(lambda a, b, co, si: (a * co - b * si, a * si + b * co), [x1, x2, cos, sin], [], [(c, F32), (c, F32)], name=name)


def _logsig_fwd(f, b, name):
    def fn(f, b):
        z = f + b
        return jnp.minimum(z, 0.0) - jnp.log(1.0 + jnp.exp(-jnp.abs(z)))
    return _ew(fn, [f], [b], [(f.shape[1], F32)], name=name)


def _logsig_bwd(dlogf, f, b, name):
    def fn(d, f, b):
        df = d * jax.nn.sigmoid(-(f + b))
        return df, jnp.sum(df, axis=0, keepdims=True)
    return _ew(fn, [dlogf, f], [b], [(f.shape[1], F32)], [f.shape[1]], name=name)


def _final_fwd_bwd(h, w, target, name):
    d = h.shape[1]

    def fn(h, t, w):
        r = lax.rsqrt(jnp.mean(h * h, axis=-1, keepdims=True) + RMS_EPS)
        xh = h * r
        y = xh * w
        err = y - t
        dy = err * (1.0 / d)
        gw = dy * w
        dx = r * (gw - xh * jnp.mean(gw * xh, axis=-1, keepdims=True))
        return dx, jnp.sum(dy * xh, axis=0, keepdims=True), jnp.sum(err * err, axis=0, keepdims=True) * (0.5 / d)
    return _ew(fn, [h, target], [w], [(d, F32)], [d, d], name=name)


def _adamw(w, g, m, v, name):
    shape = w.shape
    c = shape[-1]
    w2, g2, m2, v2 = (a.reshape(-1, c) for a in (w, g, m, v))

    def fn(w, g, m, v):
        m = ADAM_B1 * m + (1.0 - ADAM_B1) * g
        v = ADAM_B2 * v + (1.0 - ADAM_B2) * jnp.square(g)
        m_hat = m / (1.0 - ADAM_B1 ** ADAM_STEP)
        v_hat = v / (1.0 - ADAM_B2 ** ADAM_STEP)
        delta = -ADAM_LR * (m_hat / (jnp.sqrt(v_hat) + ADAM_EPS) + ADAM_WD * w)
        return delta, m, v
    d, nm, nv = _ew(fn, [w2, g2, m2, v2], [], [(c, F32)] * 3, name=name)
    return d.reshape(shape), nm.reshape(shape), nv.reshape(shape)


def _split3(v):
    hi = v.astype(BF16)
    r1 = v - hi.astype(F32)
    mid = r1.astype(BF16)
    lo = (r1 - mid.astype(F32)).astype(BF16)
    return hi, mid, lo


def _cumsum(x, reverse, name):
    S, C = x.shape
    tm = _divisor(S, 512, 16)
    nt = S // tm

    def body(x_ref, o_ref, carry):
        @pl.when(pl.program_id(0) == 0)
        def _():
            carry[...] = jnp.zeros_like(carry)
        r = lax.broadcasted_iota(jnp.int32, (tm, tm), 0)
        c = lax.broadcasted_iota(jnp.int32, (tm, tm), 1)
        tri = jnp.where((c >= r) if reverse else (c <= r), 1.0, 0.0).astype(BF16)
        xv = x_ref[...]
        acc = jnp.zeros((tm, C), F32)
        for part in _split3(xv):
            acc = acc + jnp.dot(tri, part, preferred_element_type=F32)
        o_ref[...] = acc + carry[...]
        carry[...] += jnp.sum(xv, axis=0, keepdims=True)

    idx = (lambda i: (nt - 1 - i, 0)) if reverse else (lambda i: (i, 0))
    return pl.pallas_call(
        body, name=name, grid=(nt,), in_specs=[pl.BlockSpec((tm, C), idx)], out_specs=pl.BlockSpec((tm, C), idx),
        out_shape=jax.ShapeDtypeStruct((S, C), F32), scratch_shapes=[pltpu.VMEM((1, C), F32)],
    )(x)


NN = (((1,), (0,)), ((), ()))
NT = (((1,), (1,)), ((), ()))
TN = (((0,), (0,)), ((), ()))

HBM_SPEC = pl.BlockSpec(memory_space=pl.ANY)


def _job_in_body(job, refs, n_in, n_out, n_scr, grid):
    if job is None:
        return refs[n_in:], lambda: None
    ji, jo = len(job["ins"]), len(job["outs"])
    j_in = refs[n_in: n_in + ji]
    pos = n_in + ji
    own = list(refs[pos: pos + n_out])
    pos += n_out
    j_out = refs[pos: pos + jo]
    pos += jo
    own += list(refs[pos: pos + n_scr])
    ss, rs = refs[-2], refs[-1]
    first = functools.reduce(jnp.logical_and, [pl.program_id(d) == 0 for d in range(len(grid))])
    last = functools.reduce(jnp.logical_and, [pl.program_id(d) == n - 1 for d, n in enumerate(grid)])

    @pl.when(first)
    def _():
        job["start"](j_in, j_out, ss, rs)

    def finish():
        @pl.when(last)
        def _():
            job["finish"](j_in, j_out, ss, rs)

    return own, finish


def _job_call(job, body, *, name, grid, in_specs, out_specs, out_shape, args, scratch_shapes, aliases, dimension_semantics):
    in_specs, out_specs, out_shape, args, scratch_shapes = list(in_specs), list(out_specs), list(out_shape), list(args), list(scratch_shapes)
    aliases = dict(aliases)
    if job is not None:
        for i, o in job["aliases"].items():
            aliases[len(args) + i] = len(out_shape) + o
        in_specs += [HBM_SPEC] * len(job["ins"])
        args += list(job["ins"])
        out_specs += [HBM_SPEC] * len(job["outs"])
        out_shape += list(job["outs"])
        scratch_shapes += [pltpu.SemaphoreType.DMA((job["n_sems"],)), pltpu.SemaphoreType.DMA((job["n_sems"],))]
    return pl.pallas_call(
        body, name=name, grid=grid, in_specs=in_specs, out_specs=out_specs, out_shape=out_shape,
        scratch_shapes=scratch_shapes, input_output_aliases=aliases,
        compiler_params=pltpu.CompilerParams(dimension_semantics=dimension_semantics),
    )(*args)


def _comm_call(job, name):
    def body(*refs):
        ji, jo = len(job["ins"]), len(job["outs"])
        job["start"](refs[:ji], refs[ji: ji + jo], refs[-2], refs[-1])
        job["finish"](refs[:ji], refs[ji: ji + jo], refs[-2], refs[-1])

    return pl.pallas_call(
        body, name=name, in_specs=[HBM_SPEC] * len(job["ins"]), out_specs=[HBM_SPEC] * len(job["outs"]), out_shape=list(job["outs"]),
        input_output_aliases=dict(job["aliases"]),
        scratch_shapes=[pltpu.SemaphoreType.DMA((job["n_sems"],)), pltpu.SemaphoreType.DMA((job["n_sems"],))],
    )(*job["ins"])


def _mm_call(name, grid, k_axis, a, a_spec, a2d, b, b_spec, b2d, dims, out_sds, out_spec, o2d, *,
             alpha=1.0, res=None, res_spec=None, into=None, job=None, norm_bwd=None):
    nk = grid[k_axis]
    n_in = 2 + (res is not None) + (into is not None) + (3 if norm_bwd is not None else 0)
    n_out = 2 if norm_bwd is not None else 1

    def body(*refs):
        a_ref, b_ref = refs[0], refs[1]
        res_ref = refs[2] if res is not None else None
        own, finish_job = _job_in_body(job, refs, n_in, n_out, 1, grid)
        o_ref, acc_ref = own[0], own[-1]
        k = pl.program_id(k_axis)

        @pl.when(k == 0)
        def _():
            acc_ref[...] = jnp.zeros_like(acc_ref)

        if norm_bwd is not None:
            x_ref, w_ref, dres_ref = refs[n_in - 3: n_in]
            dw_ref = own[1]

            @pl.when(functools.reduce(jnp.logical_and, [pl.program_id(d) == 0 for d in range(len(grid))]))
            def _():
                dw_ref[...] = jnp.zeros_like(dw_ref)

        av = a_ref[...].reshape(a2d).astype(BF16)
        bv = b_ref[...].reshape(b2d).astype(BF16)
        acc_ref[...] += lax.dot_general(av, bv, dims, preferred_element_type=F32)

        @pl.when(k == nk - 1)
        def _():
            r = acc_ref[...]
            if alpha != 1.0:
                r = r * alpha
            if res_ref is not None:
                r = res_ref[...].reshape(o2d) + r
            if norm_bwd is not None:
                x = x_ref[...]
                rs = lax.rsqrt(jnp.mean(x * x, axis=-1, keepdims=True) + RMS_EPS)
                xh = x * rs
                gw = r * w_ref[...]
                dw_ref[...] += jnp.sum(r * xh, axis=0, keepdims=True)
                r = dres_ref[...] + rs * (gw - xh * jnp.mean(gw * xh, axis=-1, keepdims=True))
            o_ref[...] = r.reshape(o_ref.shape).astype(o_ref.dtype)

        finish_job()

    in_specs, args = [a_spec, b_spec], [a, b]
    if res is not None:
        in_specs.append(res_spec)
        args.append(res)
    aliases = {}
    if into is not None:
        aliases = {len(args): 0}
        in_specs.append(pl.BlockSpec(memory_space=pl.ANY))
        args.append(into)
        out_sds = jax.ShapeDtypeStruct(into.shape, into.dtype)
    out_specs, out_shape = [out_spec], [out_sds]
    if norm_bwd is not None:
        vec = pl.BlockSpec((1, o2d[1]), lambda *_: (0, 0))
        in_specs += [out_spec, vec, out_spec]
        args += list(norm_bwd)
        out_specs.append(vec)
        out_shape.append(jax.ShapeDtypeStruct((1, o2d[1]), F32))
    serial = job is not None or norm_bwd is not None
    sem = tuple("arbitrary" if d == k_axis or serial else "parallel" for d in range(len(grid)))
    res_all = _job_call(
        job, body, name=name, grid=grid, in_specs=in_specs, out_specs=out_specs, out_shape=out_shape, args=args,
        scratch_shapes=[pltpu.VMEM(o2d, F32)], aliases=aliases, dimension_semantics=sem)
    own = res_all[0] if n_out == 1 else tuple(res_all[:n_out])
    return own if job is None else (own, res_all[n_out:])


def _mm(a, b, *, ta=False, tb=False, out=F32, res=None, alpha=1.0, norm_bwd=None, name):
    K, M = a.shape if ta else a.shape[::-1]
    N = b.shape[0] if tb else b.shape[1]
    assert (b.shape[1] if tb else b.shape[0]) == K, (a.shape, b.shape, ta, tb)
    tk = _divisor(K, 1024, LANES)
    tn = _divisor(N, 1408, LANES)
    assert norm_bwd is None or tn == N
    for cap in (1024, 512, 256, 128):
        tm = _divisor(M, cap, LANES if ta else 16)
        est = 2 * (tm * tk * a.dtype.itemsize + tk * tn * b.dtype.itemsize + tm * tn * jnp.dtype(out).itemsize)
        est += tm * tn * 4 + (2 * tm * tn * 4 if res is not None else 0) + (4 * tm * tn * 4 if norm_bwd is not None else 0)
        if est <= MM_VMEM_BYTES:
            break
    a_spec = pl.BlockSpec((tk, tm), lambda i, j, k: (k, i)) if ta else pl.BlockSpec((tm, tk), lambda i, j, k: (i, k))
    b_spec = pl.BlockSpec((tn, tk), lambda i, j, k: (j, k)) if tb else pl.BlockSpec((tk, tn), lambda i, j, k: (k, j))
    o_spec = pl.BlockSpec((tm, tn), lambda i, j, k: (i, j))
    dims = (((0 if ta else 1,), (1 if tb else 0,)), ((), ()))
    return _mm_call(name, (M // tm, N // tn, K // tk), 2, a, a_spec, (tk, tm) if ta else (tm, tk), b, b_spec,
                    (tn, tk) if tb else (tk, tn), dims, jax.ShapeDtypeStruct((M, N), out), o_spec, (tm, tn),
                    alpha=alpha, res=res, res_spec=o_spec, norm_bwd=norm_bwd)


def _w128_spec(blk):
    return pl.BlockSpec((N_DEV, 128, D_MODEL), lambda *_: (0, blk, 0))


def _mm_w128(a, G1, blk, *, tb=False, res=None, out=F32, norm_bwd=None, name):
    S = a.shape[0]
    tm = _divisor(S, 1024 if norm_bwd is None else 512, 16)
    row = pl.BlockSpec((tm, D_MODEL), lambda i, k: (i, 0))
    return _mm_call(name, (S // tm, 1), 1, a, row, (tm, D_MODEL), G1, _w128_spec(blk), (D_MODEL, D_MODEL), NT if tb else NN,
                    jax.ShapeDtypeStruct((S, D_MODEL), out), row, (tm, D_MODEL), res=res, res_spec=row, norm_bwd=norm_bwd)


def _mm_w128_dw(a, b, blk, into, name):
    S = a.shape[0]
    tk = _divisor(S, 1024, 16)
    row = pl.BlockSpec((tk, D_MODEL), lambda i, k: (k, 0))
    return _mm_call(name, (1, S // tk), 1, a, row, (tk, D_MODEL), b, row, (tk, D_MODEL), TN, None, _w128_spec(blk),
                    (D_MODEL, D_MODEL), into=into)


def _ffn_gate_up(h, norm_w, G2v, rb, name, job=None):
    S = h.shape[0]
    tm = _divisor(S, 1024, 16)
    grid = (S // tm, 4)

    def body(*refs):
        h_ref, nw_ref, w_ref = refs[:3]
        (n_ref, gu_ref, act_ref, n_scr), finish_job = _job_in_body(job, refs, 3, 3, 1, grid)

        @pl.when(pl.program_id(1) == 0)
        def _():
            x = h_ref[...]
            y = x * lax.rsqrt(jnp.mean(x * x, axis=-1, keepdims=True) + RMS_EPS)
            n_scr[...] = (y * nw_ref[...]).astype(BF16)
            n_ref[...] = n_scr[...]

        nv = n_scr[...]
        g = jnp.dot(nv, w_ref[0, 0], preferred_element_type=F32)
        u = jnp.dot(nv, w_ref[1, 0], preferred_element_type=F32)
        sg = jax.nn.sigmoid(g)
        silu = g * sg
        gu_ref[0, 0] = (u * (sg * (1.0 + g * (1.0 - sg)))).astype(BF16)
        gu_ref[1, 0] = silu.astype(BF16)
        act_ref[0] = (silu * u).astype(BF16)
        finish_job()

    row = pl.BlockSpec((tm, D_MODEL), lambda i, j: (i, 0))
    return _job_call(
        job, body, name=name, grid=grid,
        in_specs=[row, pl.BlockSpec((1, D_MODEL), lambda i, j: (0, 0)), pl.BlockSpec((2, 1, D_MODEL, FF_BLK), lambda i, j: (0, j, rb, 0))],
        out_specs=[row, pl.BlockSpec((2, 1, tm, FF_BLK), lambda i, j: (0, j, i, 0)), pl.BlockSpec((1, tm, FF_BLK), lambda i, j: (j, i, 0))],
        out_shape=[jax.ShapeDtypeStruct((S, D_MODEL), BF16), jax.ShapeDtypeStruct((2, 4, S, FF_BLK), BF16), jax.ShapeDtypeStruct((4, S, FF_BLK), BF16)],
        args=[h, norm_w, G2v], scratch_shapes=[pltpu.VMEM((tm, D_MODEL), BF16)], aliases={},
        dimension_semantics=("arbitrary" if job is not None else "parallel", "arbitrary"))


def _ffn_down(act, G1, ob, h, name, job=None):
    S = h.shape[0]
    tm = _divisor(S, 1024, 16)
    row = pl.BlockSpec((tm, D_MODEL), lambda i, k: (i, 0))
    return _mm_call(name, (S // tm, 4), 1, act, pl.BlockSpec((1, tm, FF_BLK), lambda i, k: (k, i, 0)), (tm, FF_BLK),
                    G1, pl.BlockSpec((2, DOWN_ROWS, D_MODEL), lambda i, k: (k, ob, 0)), (FF_BLK, D_MODEL), NN,
                    jax.ShapeDtypeStruct((S, D_MODEL), F32), row, (tm, D_MODEL), alpha=0.5, res=h, res_spec=row, job=job)


def _ffn_down_dx(dh, G1, ob, gu, name):
    S = dh.shape[0]
    tm = _divisor(S, 1024, 16)

    def body(dh_ref, w_ref, gu_ref, o_ref):
        w = w_ref[...].reshape(FF_BLK, D_MODEL)
        dact = lax.dot_general(dh_ref[...].astype(BF16), w, NT, preferred_element_type=F32) * 0.5
        o_ref[0, 0] = (dact * gu_ref[0, 0].astype(F32)).astype(BF16)
        o_ref[1, 0] = (dact * gu_ref[1, 0].astype(F32)).astype(BF16)

    blk = pl.BlockSpec((2, 1, tm, FF_BLK), lambda i, j: (0, j, i, 0))
    return pl.pallas_call(
        body, name=name, grid=(S // tm, 4),
        in_specs=[pl.BlockSpec((tm, D_MODEL), lambda i, j: (i, 0)), pl.BlockSpec((2, DOWN_ROWS, D_MODEL), lambda i, j: (j, ob, 0)), blk],
        out_specs=blk, out_shape=jax.ShapeDtypeStruct((2, 4, S, FF_BLK), BF16),
    )(dh, G1, gu)


def _ffn_down_dw(act, dh, name, job=None):
    S = dh.shape[0]
    tk = _divisor(S, 1024, 16)
    return _mm_call(name, (4, S // tk), 1, act, pl.BlockSpec((1, tk, FF_BLK), lambda j, k: (j, k, 0)), (tk, FF_BLK),
                    dh, pl.BlockSpec((tk, D_MODEL), lambda j, k: (k, 0)), (tk, D_MODEL), TN,
                    jax.ShapeDtypeStruct((N_DEV, DOWN_ROWS, D_MODEL), BF16),
                    pl.BlockSpec((2, DOWN_ROWS, D_MODEL), lambda j, k: (j, 0, 0)), (FF_BLK, D_MODEL), alpha=0.5, job=job)


def _ffn_gate_up_dw(n, dgu8, name, job=None):
    S = n.shape[0]
    tk = _divisor(S, 1024, 16)
    return _mm_call(name, (N_DEV, S // tk), 1, n, pl.BlockSpec((tk, D_MODEL), lambda b, k: (k, 0)), (tk, D_MODEL),
                    dgu8, pl.BlockSpec((1, tk, FF_BLK), lambda b, k: (b, k, 0)), (tk, FF_BLK), TN,
                    jax.ShapeDtypeStruct((N_DEV, D_MODEL, FF_BLK), BF16),
                    pl.BlockSpec((1, D_MODEL, FF_BLK), lambda b, k: (b, 0, 0)), (D_MODEL, FF_BLK), job=job)


def _ffn_gate_up_dx(dgu8, G2, rb, h, norm_w, dres, name, job=None):
    S = h.shape[0]
    tm = _divisor(S, 1024, 16)
    row = pl.BlockSpec((tm, D_MODEL), lambda i, k: (i, 0))
    return _mm_call(name, (S // tm, N_DEV), 1, dgu8, pl.BlockSpec((1, tm, FF_BLK), lambda i, k: (k, i, 0)), (tm, FF_BLK),
                    G2, pl.BlockSpec((1, D_MODEL, FF_BLK), lambda i, k: (k, rb, 0)), (D_MODEL, FF_BLK), NT,
                    jax.ShapeDtypeStruct((S, D_MODEL), F32), row, (tm, D_MODEL), norm_bwd=(h, norm_w, dres), job=job)


def _unheads(x):
    h, S, d = x.shape
    return jnp.transpose(x, (1, 0, 2)).reshape(S, h * d)


def _exact3(v):
    rnd = lambda a: lax.reduce_precision(a, exponent_bits=8, mantissa_bits=7)
    hi = rnd(v)
    mid = rnd(v - hi)
    return hi, mid, rnd(v - hi - mid)


def _causal_mask(st, q0, k0, window):
    dist = (q0 + lax.broadcasted_iota(jnp.int32, st.shape, 1)) - (k0 + lax.broadcasted_iota(jnp.int32, st.shape, 0))
    mask = dist >= 0
    if window is not None:
        mask = mask & (dist < window)
    return jnp.where(mask, st, NEG)


def _attn_fwd(qT, k, vT1, *, tile, hb, window=None, sink=None, name, job=None):
    H, dqk, S = qT.shape
    G = H // k.shape[0]
    dvp = vT1.shape[1]
    dv = dvp - 16
    tq = tk = tile
    assert H % hb == 0 and (G == 1 or G % hb == 0)
    kvb = hb if G == 1 else 1
    grid = (H // hb, S // tq)
    n_in = 3 + (sink is not None)

    def body(*refs):
        q_ref, k_ref, v_ref = refs[:3]
        (o_ref, lse_ref), finish_job = _job_in_body(job, refs, n_in, 2, 0, grid)
        i = pl.program_id(1)
        carry = []
        for a in range(hb):
            if sink is not None:
                carry.append(jnp.zeros((1, tq), F32) + refs[3][a, :, 0:1])
                carry.append(jnp.where(lax.broadcasted_iota(jnp.int32, (dvp, tq), 0) == dv, 1.0, 0.0))
            else:
                carry.append(jnp.full((1, tq), NEG, F32))
                carry.append(jnp.zeros((dvp, tq), F32))

        def step(j, carry, masked, off=None, keys=tk, q_from=0):
            off = pl.multiple_of(j * tk, tk) if off is None else off
            out = []
            for a in range(hb):
                m, acc = carry[2 * a], carry[2 * a + 1]
                kv = a if kvb > 1 else 0
                st = jnp.dot(k_ref[kv, pl.ds(off, keys), :], q_ref[a][:, q_from:], preferred_element_type=F32)
                if masked:
                    st = _causal_mask(st, i * tq + q_from, off, window)
                m_old, acc_old = m[:, q_from:], acc[:, q_from:]
                m_new = jnp.maximum(m_old, jnp.max(st, axis=0, keepdims=True))
                pt = jnp.exp(st - m_new).astype(BF16)
                acc_new = jnp.exp(m_old - m_new) * acc_old + jnp.dot(v_ref[kv, :, pl.ds(off, keys)], pt, preferred_element_type=F32)
                if q_from:
                    m_new = jnp.concatenate([m[:, :q_from], m_new], axis=1)
                    acc_new = jnp.concatenate([acc[:, :q_from], acc_new], axis=1)
                out += [m_new, acc_new]
            return tuple(out)

        carry = tuple(carry)
        if window is None:
            carry = lax.fori_loop(0, i, functools.partial(step, masked=False), carry)
            if tq % (2 * LANES) == 0:
                half = tq // 2
                carry = step(None, carry, True, off=pl.multiple_of(i * tq, tq), keys=half)
                carry = step(None, carry, True, off=pl.multiple_of(i * tq + half, half), keys=half, q_from=half)
            else:
                carry = step(i, carry, True)
        else:
            assert window % LANES == 0 and tq + window <= S
            carry = step(None, carry, True, off=pl.multiple_of(jnp.maximum(i * tq - window, 0), LANES), keys=tq + window)
        for a in range(hb):
            m, acc = carry[2 * a], carry[2 * a + 1]
            l = acc[dv:dv + 1, :]
            o_ref[a] = acc[:dv, :] / l
            lse_ref[a] = m + jnp.log(l)
        finish_job()

    kv_idx = (lambda b: b) if G == 1 else (lambda b: (b * hb) // G)
    in_specs = [
        pl.BlockSpec((hb, dqk, tq), lambda b, i: (b, 0, i)),
        pl.BlockSpec((kvb, S, dqk), lambda b, i: (kv_idx(b), 0, 0)),
        pl.BlockSpec((kvb, dvp, S), lambda b, i: (kv_idx(b), 0, 0)),
    ]
    args = [qT, k, vT1]
    if sink is not None:
        in_specs += [pl.BlockSpec((hb, 1, LANES), lambda b, i: (b, 0, 0))]
        args += [sink]
    return _job_call(
        job, body, name=name, grid=grid, in_specs=in_specs,
        out_specs=[pl.BlockSpec((hb, dv, tq), lambda b, i: (b, 0, i)), pl.BlockSpec((hb, 1, tq), lambda b, i: (b, 0, i))],
        out_shape=[jax.ShapeDtypeStruct((H, dv, S), F32), jax.ShapeDtypeStruct((H, 1, S), F32)],
        args=args, scratch_shapes=[], aliases={}, dimension_semantics=("arbitrary", "arbitrary") if job is not None else ("parallel", "parallel"))


def _attn_bwd(qT, k, kT, v, oT, doT, lse, *, tile, hb, window=None, sink=None, real=None, extra=False, full=False, name):
    H, dqk, S = qT.shape
    G = H // k.shape[0]
    dv = v.shape[2]
    tq = tk = tile
    nq = S // tq
    has_p = sink is not None
    real = dqk if real is None else real
    main = dqk if full else real
    assert H % hb == 0 and (G == 1 or G % hb == 0) and not (extra and real == dqk)
    kvb = hb if G == 1 else 1

    def body(*refs):
        qT_ref, k_ref, kT_ref, v_ref, oT_ref, doT_ref, lse_ref = refs[:7]
        p_ref = refs[7] if has_p else None
        pos = 8 if has_p else 7
        dq_ref, dk_ref, dv_ref = refs[pos: pos + 3]
        pos += 3
        ds_ref = refs[pos] if has_p else None
        pos += has_p
        dqx_ref, dkx_ref = (refs[pos], refs[pos + 1]) if extra else (None, None)
        delta = refs[-1]
        j = pl.program_id(1)

        @pl.when(j == 0)
        def _():
            dq_ref[...] = jnp.zeros_like(dq_ref)
            if extra:
                dqx_ref[...] = jnp.zeros_like(dqx_ref)
            for a in range(hb):
                drow = jnp.sum(doT_ref[a].astype(F32) * oT_ref[a], axis=0, keepdims=True)
                delta[a] = drow
                if has_p:
                    w = jnp.exp(p_ref[a, :, 0:1] - lse_ref[a])
                    ds_ref[a] = jnp.zeros((1, LANES), F32) - jnp.sum(w * drow, axis=1, keepdims=True)

        def step(i, carry, masked, off=None, qs=tq, keys=tk):
            off = pl.multiple_of(i * tq, tq) if off is None else off
            out = []
            for a in range(hb):
                dk, dvv = carry[2 * a], carry[2 * a + 1]
                kv = a if kvb > 1 else 0
                qTi = qT_ref[a, :, pl.ds(off, qs)]
                doTi = doT_ref[a, :, pl.ds(off, qs)]
                st = jnp.dot(k_ref[kv, pl.ds(0, keys), :], qTi, preferred_element_type=F32)
                if masked:
                    st = _causal_mask(st, off, j * tk, window)
                pt = jnp.exp(st - lse_ref[a, :, pl.ds(off, qs)])
                dv_new = lax.dot_general(pt.astype(BF16), doTi, NT, preferred_element_type=F32)
                dpt = jnp.dot(v_ref[kv, pl.ds(0, keys), :], doTi, preferred_element_type=F32)
                dsb = (pt * (dpt - delta[a, :, pl.ds(off, qs)])).astype(BF16)
                dk_new = lax.dot_general(dsb, qTi, NT, preferred_element_type=F32)
                if keys < tk:
                    dk = jnp.concatenate([dk[:keys] + dk_new, dk[keys:]], axis=0)
                    dvv = jnp.concatenate([dvv[:keys] + dv_new, dvv[keys:]], axis=0)
                else:
                    dk, dvv = dk + dk_new, dvv + dv_new
                dqt = jnp.dot(kT_ref[kv, :, pl.ds(0, keys)], dsb, preferred_element_type=F32)
                dq_ref[a, :, pl.ds(off, qs)] += dqt[:main]
                if extra:
                    dqx_ref[a, :, pl.ds(off, qs)] += dqt[real:]
                out += [dk, dvv]
            return tuple(out)

        carry = (jnp.zeros((tk, dqk), F32), jnp.zeros((tk, dv), F32)) * hb
        if window is None:
            if tk % (2 * LANES) == 0:
                half = tk // 2
                carry = step(None, carry, True, off=pl.multiple_of(j * tk + half, half), qs=half)
                carry = step(None, carry, True, off=pl.multiple_of(j * tk, tk), qs=half, keys=half)
            else:
                carry = step(j, carry, True)
            carry = lax.fori_loop(j + 1, nq, functools.partial(step, masked=False), carry)
        else:
            assert window % LANES == 0 and tk + window <= S
            carry = step(None, carry, True, off=pl.multiple_of(jnp.minimum(j * tk, S - (tk + window)), LANES), qs=tk + window)
        for a in range(hb):
            dk_ref[a] = carry[2 * a][:, :main]
            if extra:
                dkx_ref[a] = carry[2 * a][:, real:]
            dv_ref[a] = carry[2 * a + 1]

    kv_idx = (lambda b: b) if G == 1 else (lambda b: (b * hb) // G)
    colsT = lambda d: pl.BlockSpec((hb, d, S), lambda b, j: (b, 0, 0))
    in_specs = [
        colsT(dqk),
        pl.BlockSpec((kvb, tk, dqk), lambda b, j: (kv_idx(b), j, 0)),
        pl.BlockSpec((kvb, dqk, tk), lambda b, j: (kv_idx(b), 0, j)),
        pl.BlockSpec((kvb, tk, dv), lambda b, j: (kv_idx(b), j, 0)),
        colsT(dv), colsT(dv),
        pl.BlockSpec((hb, 1, S), lambda b, j: (b, 0, 0)),
    ]
    args = [qT, k, kT, v, oT, doT, lse]
    if has_p:
        in_specs += [pl.BlockSpec((hb, 1, LANES), lambda b, j: (b, 0, 0))]
        args += [sink]
    out_specs = [colsT(main), pl.BlockSpec((hb, tk, main), lambda b, j: (b, j, 0)), pl.BlockSpec((hb, tk, dv), lambda b, j: (b, j, 0))]
    out_shape = [jax.ShapeDtypeStruct((H, main, S), F32), jax.ShapeDtypeStruct((H, S, main), F32), jax.ShapeDtypeStruct((H, S, dv), F32)]
    if has_p:
        out_specs += [pl.BlockSpec((hb, 1, LANES), lambda b, j: (b, 0, 0))]
        out_shape += [jax.ShapeDtypeStruct((H, 1, LANES), F32)]
    if extra:
        out_specs += [colsT(dqk - real), pl.BlockSpec((hb, tk, dqk - real), lambda b, j: (b, j, 0))]
        out_shape += [jax.ShapeDtypeStruct((H, dqk - real, S), F32), jax.ShapeDtypeStruct((H, S, dqk - real), F32)]
    return pl.pallas_call(
        body, name=name, grid=(H // hb, S // tk), in_specs=in_specs, out_specs=out_specs, out_shape=out_shape,
        scratch_shapes=[pltpu.VMEM((hb, 1, S), F32)],
        compiler_params=pltpu.CompilerParams(dimension_semantics=("parallel", "arbitrary")),
    )(*args)


def _rows_and_cols(x3):
    xb = x3.astype(BF16)
    return jnp.transpose(xb, (1, 0, 2)), jnp.transpose(xb, (1, 2, 0))


def _cols_only(x3):
    return jnp.transpose(x3.astype(BF16), (1, 2, 0))


def _v_with_ones(v3):
    S, h, _ = v3.shape
    vT = jnp.transpose(v3.astype(BF16), (1, 2, 0))
    return jnp.concatenate([vT, jnp.ones((h, 1, S), BF16), jnp.zeros((h, 15, S), BF16)], axis=1)


def _from_T(oT):
    h, d, S = oT.shape
    return jnp.transpose(oT, (2, 0, 1)).reshape(S, h * d)


def _coords():
    return lax.axis_index("x"), lax.axis_index("y"), lax.axis_index("c")


def _peer(axis):
    x, y, c = _coords()
    return {"x": (1 - x, y, c), "y": (x, 1 - y, c), "c": (x, y, 1 - c)}[axis]


def _gather_job(bufs, rows=None):
    n = len(bufs)

    def copies(outs, send_sems, recv_sems):
        x, y, c = _coords()
        me, sibling = (x, y, c), (x, y, 1 - c)
        chips = [(1 - x, y), (x, 1 - y), (1 - x, 1 - y)]

        def copy(t, k, block, to):
            px, py, pc = block
            ref = outs[t].at[4 * px + 2 * py + pc]
            if rows is not None and rows[t] is not None:
                ref = ref.at[pl.ds(rows[t][0], rows[t][1])]
            return pltpu.make_async_remote_copy(ref, ref, send_sems.at[7 * t + k], recv_sems.at[7 * t + k], device_id=to, device_id_type=MESH)

        return copy, me, sibling, chips, c

    def start(ins, outs, send_sems, recv_sems):
        copy, me, sibling, chips, c = copies(outs, send_sems, recv_sems)
        for t in range(n):
            copy(t, 0, me, sibling).start()
            for j, chip in enumerate(chips):
                copy(t, 1 + j, me, (*chip, c)).start()

    def finish(ins, outs, send_sems, recv_sems):
        copy, me, sibling, chips, c = copies(outs, send_sems, recv_sems)
        for j, chip in enumerate(chips):
            for t in range(n):
                copy(t, 1 + j, (*chip, c), me).wait_recv()
                copy(t, 4 + j, (*chip, c), sibling).start()
        for t in range(n):
            copy(t, 0, sibling, me).wait_recv()
            for j, chip in enumerate(chips):
                copy(t, 4 + j, (*chip, 1 - c), me).wait_recv()
        for t in range(n):
            copy(t, 0, me, sibling).wait_send()
            for j, chip in enumerate(chips):
                copy(t, 1 + j, me, (*chip, c)).wait_send()
                copy(t, 4 + j, (*chip, c), sibling).wait_send()

    return dict(ins=list(bufs), outs=[jax.ShapeDtypeStruct(b.shape, b.dtype) for b in bufs], aliases={t: t for t in range(n)},
                n_sems=7 * n, start=start, finish=finish)


def _in_slot(local):
    x, y, c = _coords()
    buf = lax.empty((N_DEV,) + local.shape, local.dtype)
    return lax.dynamic_update_slice(buf, local[None], (4 * x + 2 * y + c, 0, 0))


def _pair_job(vs, axes):
    n = len(vs)
    axes = [axes] * n if isinstance(axes, str) else axes

    def copies(ins, outs, send_sems, recv_sems):
        out = []
        for t in range(n):
            me = lax.axis_index(axes[t])
            src = ins[t].at[1 - me] if len(ins[t].shape) == 3 else ins[t].at[:, 1 - me]
            out.append(pltpu.make_async_remote_copy(src, outs[t], send_sems.at[t], recv_sems.at[t], device_id=_peer(axes[t]), device_id_type=MESH))
        return out

    def start(*refs):
        for cp in copies(*refs):
            cp.start()

    def finish(*refs):
        for cp in copies(*refs):
            cp.wait()

    return dict(ins=list(vs), outs=[jax.ShapeDtypeStruct(v.shape[:-3] + v.shape[-2:], v.dtype) for v in vs], aliases={}, n_sems=n,
                start=start, finish=finish)


def _add_kept(v, got, axis, out, name):
    R, C = v.shape[-2:]
    lead = v.shape[0] if v.ndim == 4 else 1
    tm = _divisor(R, max(16, EW_TILE_BYTES // (_lanes(C) * (v.dtype.itemsize + got.dtype.itemsize + jnp.dtype(out).itemsize)) // 16 * 16), 16)
    me = lax.axis_index(axis).astype(jnp.int32).reshape(1)
    v4 = v.reshape(lead, 2, R, C)
    g3 = got.reshape(lead, R, C)

    def body(me_ref, v_ref, g_ref, o_ref):
        o_ref[...] = (v_ref[0].astype(F32) + g_ref[...].astype(F32)).astype(o_ref.dtype)

    res = pl.pallas_call(
        body, name=name, out_shape=jax.ShapeDtypeStruct((lead, R, C), out),
        grid_spec=pltpu.PrefetchScalarGridSpec(
            num_scalar_prefetch=1, grid=(lead, R // tm),
            in_specs=[pl.BlockSpec((1, 1, tm, C), lambda b, i, me: (b, me[0], i, 0)), pl.BlockSpec((1, tm, C), lambda b, i, me: (b, i, 0))],
            out_specs=pl.BlockSpec((1, tm, C), lambda b, i, me: (b, i, 0))),
    )(me, v4, g3)
    return res


def _cross_job(vs):
    n = len(vs)

    def copies(ins, outs, send_sems, recv_sems):
        x, y, _ = _coords()
        out = []
        for t in range(n):
            h = ins[t].shape[2] // 2
            out.append(pltpu.make_async_remote_copy(ins[t].at[1 - x, :, pl.ds(0, h)], outs[2 * t], send_sems.at[2 * t], recv_sems.at[2 * t],
                                                    device_id=_peer("x"), device_id_type=MESH))
            out.append(pltpu.make_async_remote_copy(ins[t].at[:, 1 - y, pl.ds(h, h)], outs[2 * t + 1], send_sems.at[2 * t + 1], recv_sems.at[2 * t + 1],
                                                    device_id=_peer("y"), device_id_type=MESH))
        return out

    def start(*refs):
        for cp in copies(*refs):
            cp.start()

    def finish(*refs):
        for cp in copies(*refs):
            cp.wait()

    outs = []
    for v in vs:
        outs += [jax.ShapeDtypeStruct((2, v.shape[2] // 2, v.shape[3]), v.dtype)] * 2
    return dict(ins=list(vs), outs=outs, aliases={}, n_sems=2 * n, start=start, finish=finish)


def _add_picked(v, got, axis, out, name):
    _, _, R, C = v.shape
    h = R // 2
    tm = _divisor(h, max(16, EW_TILE_BYTES // (_lanes(C) * (v.dtype.itemsize + got.dtype.itemsize + jnp.dtype(out).itemsize)) // 16 * 16), 16)
    me = lax.axis_index(axis).astype(jnp.int32).reshape(1)
    if axis == "x":
        v_map = lambda b, i, me: (me[0], b, i, 0)
    else:
        v_map = lambda b, i, me: (b, me[0], i + h // tm, 0)

    def body(me_ref, v_ref, g_ref, o_ref):
        o_ref[...] = (v_ref[0].astype(F32) + g_ref[...].astype(F32)).astype(o_ref.dtype)

    return pl.pallas_call(
        body, name=name, out_shape=jax.ShapeDtypeStruct((2, h, C), out),
        grid_spec=pltpu.PrefetchScalarGridSpec(
            num_scalar_prefetch=1, grid=(2, h // tm),
            in_specs=[pl.BlockSpec((1, 1, tm, C), v_map), pl.BlockSpec((1, tm, C), lambda b, i, me: (b, i, 0))],
            out_specs=pl.BlockSpec((1, tm, C), lambda b, i, me: (b, i, 0))),
    )(me, v, got)


def _reduce_scatter_steps(gs, tag):
    n = len(gs)
    vs = [g.reshape(4, 2, *g.shape[1:]) for g in gs]
    got = yield _pair_job(vs, "c")
    vs = [_add_kept(v, r, "c", BF16, f"rs_{tag}_add_c{t}") for t, (v, r) in enumerate(zip(vs, got))]
    vs = [v.reshape(2, 2, v.shape[1], v.shape[2]) for v in vs]
    got = yield _cross_job(vs)
    up = [_add_picked(v, r, "x", BF16, f"rs_{tag}_add_x{t}") for t, (v, r) in enumerate(zip(vs, got[0::2]))]
    lo = [_add_picked(v, r, "y", BF16, f"rs_{tag}_add_y{t}") for t, (v, r) in enumerate(zip(vs, got[1::2]))]
    got = yield _pair_job(up + lo, ["y"] * n + ["x"] * n)
    out = []
    for t in range(n):
        a = _add_kept(up[t], got[t], "y", F32, f"rs_{tag}_add_y2{t}")[0]
        b = _add_kept(lo[t], got[n + t], "x", F32, f"rs_{tag}_add_x2{t}")[0]
        out.append(jnp.concatenate([a, b], axis=0))
    return out


def _reduce_scatter(gs, tag):
    steps = _reduce_scatter_steps(gs, tag)
    job = next(steps)
    for stage in ("c", "xy", "yx"):
        got = _comm_call(job, f"rs_{tag}_{stage}")
        try:
            job = steps.send(got)
        except StopIteration as done:
            return done.value


def _all_reduce_small(v):
    def body(v_ref, o_ref, buf, send_sems, recv_sems):
        x, y, c = _coords()
        me = 4 * x + 2 * y + c
        buf[me] = v_ref[...]
        copies = []
        for k in range(1, N_DEV):
            peer = tuple((1 - a) if (k >> s) & 1 else a for a, s in ((x, 2), (y, 1), (c, 0)))
            cp = pltpu.make_async_remote_copy(v_ref, buf.at[me], send_sems.at[k - 1], recv_sems.at[k - 1], device_id=peer, device_id_type=MESH)
            cp.start()
            copies.append(cp)
        for cp in copies:
            cp.wait()
        acc = buf[0]
        for d in range(1, N_DEV):
            acc = acc + buf[d]
        o_ref[...] = acc

    vm = pl.BlockSpec(memory_space=pltpu.VMEM)
    return pl.pallas_call(
        body, name="all_reduce_small", in_specs=[vm], out_specs=vm, out_shape=jax.ShapeDtypeStruct(v.shape, F32),
        scratch_shapes=[pltpu.VMEM((N_DEV,) + v.shape, F32), pltpu.SemaphoreType.DMA((N_DEV - 1,)), pltpu.SemaphoreType.DMA((N_DEV - 1,))],
    )(v)


def _local_groups(w, dtype):
    mix_out = [w["ev_w_out"][0], w["od_w_out"][0]]
    layers = []
    for l in range(DEPTH):
        a = jnp.concatenate([w["ffa_w_down"][l], w["ffb_w_down"][l]], axis=0).astype(dtype)
        b = jnp.concatenate([w["ple_w_gate"][l], mix_out[l]], axis=0).astype(dtype)
        c = jnp.concatenate([w["ffa_w_gate_up"][l], w["ffb_w_gate_up"][l]], axis=0).astype(dtype)
        layers.append((a, b, c))
    strip = jnp.concatenate([w["ple_w_proj"].reshape(-1, STRIP_C), w["ev_w_ukv"][0], jnp.pad(w["ev_w_uq"][0], ((0, 0), (0, STRIP_C - 96))),
                             jnp.zeros((G3_ROWS - 896, STRIP_C), F32)], axis=0)
    m = jnp.concatenate([w["od_w_in"][0], w["ev_w_in"][0], strip, jnp.zeros((G3_ROWS, G3_COLS - STRIP0 - STRIP_C), F32)], axis=1).astype(dtype)
    return layers, m


def _ungroup_local(a, b, c, r3):
    out = {
        "ffa_w_down": jnp.stack([x[0] for x in a]), "ffb_w_down": jnp.stack([x[1] for x in a]),
        "ple_w_gate": jnp.stack([x[:128] for x in b]), "ev_w_out": b[0][128:][None], "od_w_out": b[1][128:][None],
        "ffa_w_gate_up": jnp.stack([x[0] for x in c]), "ffb_w_gate_up": jnp.stack([x[1] for x in c]),
        "od_w_in": r3[:, :OD_C][None], "ev_w_in": r3[:, OD_C:STRIP0][None],
    }
    strip = r3[:, STRIP0:STRIP0 + STRIP_C]
    out["ple_w_proj"] = strip[:512].reshape(2, PLE_DIM, STRIP_C)
    out["ev_w_ukv"] = strip[512:640][None]
    out["ev_w_uq"] = strip[640:896, :96][None]
    return out


def _cols(a):
    return jnp.transpose(a, (1, 0, 2)).reshape(a.shape[1], -1)


def _blocks(g, c):
    return jnp.transpose(g.reshape(g.shape[0], N_DEV, c), (1, 0, 2))


def _uq_permute(w):
    r = w.shape[0]
    w3 = w.reshape(r, B_HEADS, B_NOPE + B_ROPE)
    half = B_ROPE // 2
    return jnp.concatenate([w3[:, :, :B_NOPE].reshape(r, -1), w3[:, :, B_NOPE:B_NOPE + half].reshape(r, -1), w3[:, :, B_NOPE + half:].reshape(r, -1)], axis=1)


def _uq_unpermute(g):
    r = g.shape[0]
    half = B_ROPE // 2
    n = B_HEADS * B_NOPE
    parts = [g[:, :n].reshape(r, B_HEADS, B_NOPE), g[:, n:n + B_HEADS * half].reshape(r, B_HEADS, half), g[:, n + B_HEADS * half:].reshape(r, B_HEADS, half)]
    return jnp.concatenate(parts, axis=2).reshape(r, -1)


def _ukv_permute(w):
    r = w.shape[0]
    return jnp.transpose(w.reshape(r, B_HEADS, 2, B_NOPE), (0, 2, 1, 3)).reshape(r, -1)


def _ukv_unpermute(g):
    r = g.shape[0]
    return jnp.transpose(g.reshape(r, 2, B_HEADS, B_NOPE), (0, 2, 1, 3)).reshape(r, -1)


def _od_in_widen(w):
    n = C_HEADS * C_HEAD_DIM
    wide = lambda m: jnp.pad(m.reshape(-1, C_HEADS, C_HEAD_DIM), ((0, 0), (0, 0), (0, QK_PAD - C_HEAD_DIM))).reshape(m.shape[0], -1)
    return jnp.concatenate([wide(w[:, :n] * C_HEAD_DIM ** -0.5), wide(w[:, n:2 * n]), w[:, 2 * n:],
                            jnp.zeros((w.shape[0], ODD_IN_PAD - ODD_IN_AUG), w.dtype)], axis=1)


def _od_in_narrow(g):
    wp = C_HEADS * QK_PAD
    narrow = lambda m: m.reshape(-1, C_HEADS, QK_PAD)[:, :, :C_HEAD_DIM].reshape(m.shape[0], -1)
    return jnp.concatenate([narrow(g[:, :wp]) * C_HEAD_DIM ** -0.5, narrow(g[:, wp:2 * wp]), g[:, 2 * wp:ODD_IN_AUG]], axis=1)


def _misc_weights(G3):
    strip = G3[:, :, STRIP0:STRIP0 + STRIP_C]
    return {
        "od_w_in": _od_in_widen(_cols(G3[:, :, :OD_C])),
        "ev_w_in": jnp.pad(_cols(G3[:, :, OD_C:STRIP0]), ((0, 0), (0, EVEN_IN_PAD - EVEN_IN))),
        "ple_w_proj": [_cols(strip[:, i * PLE_DIM:(i + 1) * PLE_DIM]) for i in range(DEPTH)],
        "ev_w_ukv": _ukv_permute(_cols(strip[:, 512:640])),
        "ev_w_uq": _uq_permute(_cols(strip[:, 640:896, :96])),
    }


def _misc_grads(G):
    strip = jnp.concatenate([
        _blocks(G["ple_w_proj"][0], STRIP_C), _blocks(G["ple_w_proj"][1], STRIP_C), _blocks(_ukv_unpermute(G["ev_w_ukv"]), STRIP_C),
        jnp.pad(_blocks(_uq_unpermute(G["ev_w_uq"]), 96), ((0, 0), (0, 0), (0, STRIP_C - 96))),
        jnp.zeros((N_DEV, G3_ROWS - 896, STRIP_C), F32)], axis=1)
    return jnp.concatenate([_blocks(_od_in_narrow(G["od_w_in"]), OD_C), _blocks(G["ev_w_in"][:, :EVEN_IN], EV_C), strip,
                            jnp.zeros((N_DEV, G3_ROWS, G3_COLS - STRIP0 - STRIP_C), F32)], axis=2)


def _ffn_fwd(h, norm_w, W, f, i, tag, ride=None):
    job = ride() if ride else None
    res = _ffn_gate_up(h, norm_w, W["C"][i].reshape(2, 4, C_ROWS, FF_BLK), f, f"{tag}_gate_up", job=job)
    n, gu, act = res[:3]
    if job is not None:
        ride(res[3:])
    job = ride() if ride else None
    out = _ffn_down(act, W["A"][i], f, h, f"{tag}_down", job=job)
    if job is not None:
        out, got = out
        ride(got)
    return out, (h, n, gu, act)


def _ffn_bwd(dout, saved, norm_w, W, GB, f, i, tag, ride=None):
    h, n, gu, act = saved
    S = h.shape[0]
    def carried(call):
        job = ride() if ride else None
        res = call(job)
        if job is None:
            return res
        ride(res[1])
        return res[0]

    dout_b = dout.astype(BF16)
    GB["A"][i][f] = carried(lambda job: _ffn_down_dw(act, dout_b, f"{tag}_down_dw", job=job))
    dgu = _ffn_down_dx(dout_b, W["A"][i], f, gu, f"{tag}_down_dx").reshape(N_DEV, S, FF_BLK)
    res = carried(lambda job: _ffn_gate_up_dx(dgu, W["C"][i], f, h, norm_w, dout, f"{tag}_gate_up_dx", job=job))
    GB["C"][i][f] = carried(lambda job: _ffn_gate_up_dw(n, dgu, f"{tag}_gate_up_dw", job=job))
    return res


def _rope_tables(S):
    inv = ROPE_THETA ** (-jnp.arange(0, B_ROPE, 2, dtype=F32) / B_ROPE)
    ang = jnp.arange(S, dtype=F32)[:, None] * inv[None, :]
    return jnp.cos(ang), jnp.sin(ang)


def _alibi_columns(S):
    t = jnp.arange(S, dtype=jnp.int32)
    hi = ((t // 16) * 16).astype(F32)
    lo = (t % 16).astype(F32)
    slopes = 2.0 ** (-8.0 * jnp.arange(1, A_HEADS + 1, dtype=F32) / A_HEADS)
    zq = jnp.zeros((S, A_HEADS), F32)
    rest = QK_PAD - A_HEAD_DIM - 4
    qc = jnp.stack([-slopes[None, :] * hi[:, None], -slopes[None, :] * lo[:, None], zq + slopes[None, :], zq + slopes[None, :]] + [zq] * rest, axis=-1)
    one = jnp.ones((S, A_KV_HEADS), F32)
    zk = jnp.zeros((S, A_KV_HEADS), F32)
    kc = jnp.stack([one, one, zk + hi[:, None], zk + lo[:, None]] + [zk] * rest, axis=-1)
    return qc, kc


def _sink_prm(sinks):
    return jnp.zeros((A_HEADS, 1, LANES), F32).at[:, 0, 0].set(sinks.astype(F32))


def _with_ride(ride, call):
    job = ride() if ride else None
    res = call(job)
    if job is None:
        return res
    n_own = len(res) - len(job["outs"])
    ride(res[n_own:])
    return res[:n_own]


def _even_fwd(hn, h, W, ride=None):
    S = hn.shape[0]
    proj = _mm(hn, W["ev_w_in"], name="ev_in")
    a_q, a_k, a_v = proj[:, :512], proj[:, 512:640], proj[:, 640:768]
    c_q, c_kv = proj[:, 768:1024], proj[:, 1024:1152]
    kr1, kr2 = proj[:, 1152:1168], proj[:, 1168:1184]
    qc, kc = _alibi_columns(S)
    qaT = _cols_only(jnp.concatenate([(a_q * A_HEAD_DIM ** -0.5).reshape(S, A_HEADS, A_HEAD_DIM), qc], axis=-1))
    ka, kaT = _rows_and_cols(jnp.concatenate([a_k.reshape(S, A_KV_HEADS, A_HEAD_DIM), kc], axis=-1))
    va3 = a_v.reshape(S, A_KV_HEADS, A_HEAD_DIM)
    va = jnp.transpose(va3.astype(BF16), (1, 0, 2))
    prm = _sink_prm(W["ev_sinks"][0])
    oaT, lse_a = _with_ride(ride, lambda job: _attn_fwd(qaT, ka, _v_with_ones(va3), tile=min(SWA_TILE, S // 2), hb=A_GROUP, window=WINDOW, sink=prm,
                                                        name="swa_fwd", job=job))
    cqn = _rms_fwd(c_q, W["ev_cq_norm"], "ev_cq_norm")
    q_all = _mm(cqn, W["ev_w_uq"], name="ev_uq")
    ckvn = _rms_fwd(c_kv, W["ev_ckv_norm"], "ev_ckv_norm")
    kv_all = _mm(ckvn, W["ev_w_ukv"], name="ev_ukv")
    cos, sin = _rope_tables(S)
    cos8, sin8 = jnp.tile(cos, (1, B_HEADS)), jnp.tile(sin, (1, B_HEADS))
    q1, q2 = _rope(q_all[:, 512:640], q_all[:, 640:768], cos8, sin8, "ev_rope_q")
    k1, k2 = _rope(kr1, kr2, cos, sin, "ev_rope_k")
    half = B_ROPE // 2
    scale = (B_NOPE + B_ROPE) ** -0.5
    qbT = _cols_only(jnp.concatenate([q_all[:, :512].reshape(S, B_HEADS, B_NOPE), q1.reshape(S, B_HEADS, half), q2.reshape(S, B_HEADS, half)], axis=-1) * scale)
    kro = jnp.broadcast_to(jnp.concatenate([k1, k2], axis=1)[:, None, :], (S, B_HEADS, B_ROPE))
    kb, kbT = _rows_and_cols(jnp.concatenate([kv_all[:, :512].reshape(S, B_HEADS, B_NOPE), kro], axis=-1))
    vb3 = kv_all[:, 512:].reshape(S, B_HEADS, B_V)
    vb = jnp.transpose(vb3.astype(BF16), (1, 0, 2))
    obT, lse_b = _with_ride(ride, lambda job: _attn_fwd(qbT, kb, _v_with_ones(vb3), tile=min(ATTN_TILE_FWD, S), hb=2, name="mla_fwd", job=job))
    cat = jnp.concatenate([_from_T(oaT), _from_T(obT)], axis=1).astype(BF16)
    out = _mm_w128(cat, W["B"][0], MIX_OUT_BLK, res=h, name="ev_out")
    return out, (hn, proj, (qaT, ka, kaT, va, oaT, lse_a), prm, cqn, ckvn, (qbT, kb, kbT, vb, obT, lse_b), cat)


def _even_bwd(dout, saved, W, GB, norm):
    hn, proj, (qaT, ka, kaT, va, oaT, lse_a), prm, cqn, ckvn, (qbT, kb, kbT, vb, obT, lse_b), cat = saved
    S = hn.shape[0]
    G = {}
    dcat = _mm_w128(dout, W["B"][0], MIX_OUT_BLK, tb=True, out=BF16, name="ev_out_dx")
    GB["B"][0] = _mm_w128_dw(cat, dout, MIX_OUT_BLK, GB["B"][0], "ev_out_dw")
    doaT = _cols_only(dcat[:, :512].reshape(S, A_HEADS, A_HEAD_DIM))
    dqaT, dka, dva, dsink = _attn_bwd(qaT, ka, kaT, va, oaT, doaT, lse_a, tile=min(SWA_TILE, S // 2), hb=A_GROUP, window=WINDOW, sink=prm, real=A_HEAD_DIM,
                                       name="swa_bwd")
    G["ev_sinks"] = dsink[:, 0, 0]
    dqa = _from_T(dqaT) * A_HEAD_DIM ** -0.5
    dka = dka.reshape(A_KV_HEADS, A_GROUP, S, A_HEAD_DIM).sum(axis=1)
    dva = dva.reshape(A_KV_HEADS, A_GROUP, S, A_HEAD_DIM).sum(axis=1)
    dobT = _cols_only(dcat[:, 512:].reshape(S, B_HEADS, B_V))
    dqbT, dkb, dvb = _attn_bwd(qbT, kb, kbT, vb, obT, dobT, lse_b, tile=min(ATTN_TILE, S), hb=2, name="mla_bwd")
    half = B_ROPE // 2
    dqb = jnp.transpose(dqbT, (2, 0, 1)) * (B_NOPE + B_ROPE) ** -0.5
    dkb = jnp.transpose(dkb, (1, 0, 2))
    cos, sin = _rope_tables(S)
    cos8, sin8 = jnp.tile(cos, (1, B_HEADS)), jnp.tile(sin, (1, B_HEADS))
    dq1, dq2 = _rope(dqb[:, :, B_NOPE:B_NOPE + half].reshape(S, -1), dqb[:, :, B_NOPE + half:].reshape(S, -1), cos8, -sin8, "ev_rope_q_bwd")
    dq_all = jnp.concatenate([dqb[:, :, :B_NOPE].reshape(S, -1), dq1, dq2], axis=1).astype(BF16)
    dkr = dkb[:, :, B_NOPE:].sum(axis=1)
    dk1, dk2 = _rope(dkr[:, :half], dkr[:, half:], cos, -sin, "ev_rope_k_bwd")
    dkv_all = jnp.concatenate([dkb[:, :, :B_NOPE].reshape(S, -1), _unheads(dvb)], axis=1).astype(BF16)
    G["ev_w_uq"] = _mm(cqn, dq_all, ta=True, name="ev_uq_dw")
    dcqn = _mm(dq_all, W["ev_w_uq"], tb=True, name="ev_uq_dx")
    dc_q, G["ev_cq_norm"] = _rms_bwd(dcqn, proj[:, 768:1024], W["ev_cq_norm"], None, "ev_cq_norm_bwd")
    G["ev_w_ukv"] = _mm(ckvn, dkv_all, ta=True, name="ev_ukv_dw")
    dckvn = _mm(dkv_all, W["ev_w_ukv"], tb=True, name="ev_ukv_dx")
    dc_kv, G["ev_ckv_norm"] = _rms_bwd(dckvn, proj[:, 1024:1152], W["ev_ckv_norm"], None, "ev_ckv_norm_bwd")
    dproj = jnp.concatenate([dqa, _unheads(dka), _unheads(dva), dc_q, dc_kv, dk1, dk2,
                             jnp.zeros((S, EVEN_IN_PAD - EVEN_IN), F32)], axis=1).astype(BF16)
    G["ev_w_in"] = _mm(hn, dproj, ta=True, name="ev_in_dw")
    dh, dnorm = _mm(dproj, W["ev_w_in"], tb=True, norm_bwd=(*norm, dout), name="ev_in_dx")
    return dh, dnorm, G


def _odd_fwd(hn, h, W, ride=None):
    S = hn.shape[0]
    w = C_HEADS * C_HEAD_DIM
    wp = C_HEADS * QK_PAD
    proj = _mm(hn, W["od_w_in"], name="od_in")
    f_logit = proj[:, 2 * wp + w: 2 * wp + w + C_HEADS]
    logf = _logsig_fwd(f_logit, W["od_b_f"], "od_logsig")
    logc = _cumsum(logf, False, "od_cumsum")
    parts = list(_exact3(logc))
    ones = [jnp.ones((S, C_HEADS), F32)] * 3
    pad = [jnp.zeros((S, C_HEADS), F32)] * (QK_PAD - C_HEAD_DIM - 6)
    lead = ((0, 0), (0, 0), (C_HEAD_DIM, 0))
    q3 = proj[:, :wp].reshape(S, C_HEADS, QK_PAD) + jnp.pad(jnp.stack(parts + ones + pad, axis=-1), lead)
    k3 = proj[:, wp:2 * wp].reshape(S, C_HEADS, QK_PAD) + jnp.pad(jnp.stack(ones + [-p for p in parts] + pad, axis=-1), lead)
    qT = _cols_only(q3)
    k, kT = _rows_and_cols(k3)
    v3 = proj[:, 2 * wp:2 * wp + w].reshape(S, C_HEADS, C_HEAD_DIM)
    v = jnp.transpose(v3.astype(BF16), (1, 0, 2))
    oT, lse = _with_ride(ride, lambda job: _attn_fwd(qT, k, _v_with_ones(v3), tile=min(ATTN_TILE_FWD, S), hb=2, name="fox_fwd", job=job))
    cat = _from_T(oT).astype(BF16)
    out = _mm_w128(cat, W["B"][1], MIX_OUT_BLK, res=h, name="od_out")
    return out, (hn, qT, k, kT, v, f_logit, oT, lse, cat)


def _odd_bwd(dout, saved, W, GB, norm):
    hn, qT, k, kT, v, f_logit, oT, lse, cat = saved
    S = hn.shape[0]
    G = {}
    dcat = _mm_w128(dout, W["B"][1], MIX_OUT_BLK, tb=True, out=BF16, name="od_out_dx")
    GB["B"][1] = _mm_w128_dw(cat, dout, MIX_OUT_BLK, GB["B"][1], "od_out_dw")
    doT = _cols_only(dcat.reshape(S, C_HEADS, C_HEAD_DIM))
    dqT, dk, dv, dqxT, dkx = _attn_bwd(qT, k, kT, v, oT, doT, lse, tile=min(ATTN_TILE, S), hb=2, real=C_HEAD_DIM, extra=True,
                                       full=True, name="fox_bwd")
    dlogc = jnp.transpose(dqxT[:, 0, :] - dkx[:, :, 3])
    dlogf = _cumsum(dlogc, True, "od_cumsum_bwd")
    df, db = _logsig_bwd(dlogf, f_logit, W["od_b_f"], "od_logsig_bwd")
    G["od_b_f"] = db
    dproj = jnp.concatenate([_from_T(dqT), _unheads(dk), _unheads(dv), df, jnp.zeros((S, ODD_IN_PAD - ODD_IN_AUG), F32)], axis=1).astype(BF16)
    G["od_w_in"] = _mm(hn, dproj, ta=True, name="od_in_dw")
    dh, dnorm = _mm(dproj, W["od_w_in"], tb=True, norm_bwd=(*norm, dout), name="od_in_dx")
    return dh, dnorm, G


class _Rider:
    def __init__(self, steps, tag):
        self.steps, self.tag, self.count, self.result = steps, tag, 0, None
        self.job = next(steps)

    def __call__(self, got=None):
        if got is not None:
            return self._advance(list(got))
        job = self.job
        if isinstance(job, str):
            self._advance(None)
            return None
        return job

    def _advance(self, value):
        try:
            self.job = self.steps.send(value)
        except StopIteration as done:
            self.job, self.result = None, done.value

    def finish(self):
        while self.job is not None:
            if isinstance(self.job, str):
                self._advance(None)
                continue
            self.count += 1
            self(_comm_call(self.job, f"{self.tag}_{self.count}"))
        return self.result


def _gather_plan(W, slots):
    a0, b0, c0, m, a1, b1, c1 = (slots[key] for key in ("a0", "b0", "c0", "m", "a1", "b1", "c1"))
    (m,) = yield _gather_job([m])
    W.update(_misc_weights(m))
    (b0,) = yield _gather_job([b0])
    W["B"] = [b0]
    (c0,) = yield _gather_job([c0], rows=[(D_MODEL, D_MODEL)])
    W["C"] = [c0]
    a0, c1 = yield _gather_job([a0, c1], rows=[(DOWN_ROWS, DOWN_ROWS), (0, D_MODEL)])
    W["A"] = [a0]
    W["C"].append(c1)
    (a1,) = yield _gather_job([a1], rows=[(0, DOWN_ROWS)])
    W["A"].append(a1)
    for _ in range(3):
        yield "skip"
    a1, b1, c1 = yield _gather_job([a1, b1, c1], rows=[(DOWN_ROWS, DOWN_ROWS), None, (D_MODEL, D_MODEL)])
    W["A"][1], W["C"][1] = a1, c1
    W["B"].append(b1)


def _local_step(x, p, target, W, slots):
    h = x
    saved = []
    gather = _Rider(_gather_plan(W, slots), "all_gather_rest")
    for i in range(DEPTH):
        t = f"l{i}"
        h1, s_a = _ffn_fwd(h, W["ffa_norm"][i:i + 1], W, 0, i, f"{t}_ffa", gather)
        nm = _rms_fwd(h1, W["mix_norm"][i:i + 1], f"{t}_mix_norm")
        h2, s_m = (_even_fwd if i % 2 == 0 else _odd_fwd)(nm, h1, W, gather)
        h3, s_b = _ffn_fwd(h2, W["ffb_norm"][i:i + 1], W, 1, i, f"{t}_ffb", gather)
        npl = _rms_fwd(h3, W["ple_norm"][i:i + 1], f"{t}_ple_norm")
        gpre = _mm_w128(npl, W["B"][i], PLE_GATE_BLK, out=BF16, name=f"{t}_ple_gate")
        pp = _mm(p[i], W["ple_w_proj"][i], out=BF16, name=f"{t}_ple_proj")
        h4 = _ple_fwd(h3, gpre, pp, f"{t}_ple")
        saved.append((s_a, h1, s_m, s_b, h3, npl, gpre, pp))
        h = h4
    gather.finish()
    dh, g_final, loss_cols = _final_fwd_bwd(h, W["final_norm"], target, "final")
    G = {"final_norm": g_final}
    GB = {"A": [[None, None] for _ in range(DEPTH)], "C": [[None, None] for _ in range(DEPTH)],
          "B": [lax.empty((N_DEV, B_ROWS, D_MODEL), BF16) for _ in range(DEPTH)]}
    per_layer = {n: [None] * DEPTH for n in ("ffa_norm", "mix_norm", "ffb_norm", "ple_norm", "ple_w_proj")}
    scatter = scatter_mid = None
    for i in reversed(range(DEPTH)):
        t = f"l{i}"
        s_a, h1, s_m, s_b, h3, npl, gpre, pp = saved[i]
        dgpre, dpp = _ple_bwd(dh, gpre, pp, f"{t}_ple_bwd")
        per_layer["ple_w_proj"][i] = _mm(p[i], dpp, ta=True, name=f"{t}_ple_proj_dw")
        GB["B"][i] = _mm_w128_dw(npl, dgpre, PLE_GATE_BLK, GB["B"][i], f"{t}_ple_gate_dw")
        dh, per_layer["ple_norm"][i] = _mm_w128(dgpre, W["B"][i], PLE_GATE_BLK, tb=True, norm_bwd=(h3, W["ple_norm"][i:i + 1], dh),
                                                name=f"{t}_ple_gate_dx")
        dh, per_layer["ffb_norm"][i] = _ffn_bwd(dh, s_b, W["ffb_norm"][i:i + 1], W, GB, 1, i, f"{t}_ffb", scatter)
        dh, per_layer["mix_norm"][i], g_mix = (_even_bwd if i % 2 == 0 else _odd_bwd)(dh, s_m, W, GB, (h1, W["mix_norm"][i:i + 1]))
        G.update(g_mix)
        if i == 0:
            G["ple_w_proj"] = per_layer["ple_w_proj"]
            mid = [GB["A"][0][1], GB["C"][0][1], GB["B"][0], _misc_grads(G).astype(BF16)]
            scatter_mid = _Rider(_reduce_scatter_steps(mid, "mid"), "rs_mid")
        else:
            scatter_early = _Rider(_reduce_scatter_steps([GB["A"][i][1], GB["C"][i][1], GB["B"][i]], "early"), "rs_early")
        dh, per_layer["ffa_norm"][i] = _ffn_bwd(dh, s_a, W["ffa_norm"][i:i + 1], W, GB, 0, i, f"{t}_ffa", scatter_mid if i == 0 else scatter_early)
        if i == DEPTH - 1:
            scatter = _Rider(_reduce_scatter_steps([GB["A"][i][0], GB["C"][i][0]], "later"), "rs_later")
    for n in ("ffa_norm", "mix_norm", "ffb_norm", "ple_norm"):
        G[n] = jnp.concatenate(per_layer[n], axis=0)
    return loss_cols, dh, scatter_early.finish(), scatter.finish(), scatter_mid.finish(), [GB["A"][0][0], GB["C"][0][0]], G


def kernel(x, p, ffa_norm, ffa_w_gate_up, ffa_w_down, mix_norm, ffb_norm, ffb_w_gate_up, ffb_w_down, ple_norm, ple_w_gate, ple_w_proj, ev_w_in, ev_sinks, ev_cq_norm, ev_w_uq, ev_ckv_norm, ev_w_ukv, ev_w_out, od_w_in, od_b_f, od_w_out, final_norm, loss_target, m_ffa_norm, m_ffa_w_gate_up, m_ffa_w_down, m_mix_norm, m_ffb_norm, m_ffb_w_gate_up, m_ffb_w_down, m_ple_norm, m_ple_w_gate, m_ple_w_proj, m_ev_w_in, m_ev_sinks, m_ev_cq_norm, m_ev_w_uq, m_ev_ckv_norm, m_ev_w_ukv, m_ev_w_out, m_od_w_in, m_od_b_f, m_od_w_out, m_final_norm, v_ffa_norm, v_ffa_w_gate_up, v_ffa_w_down, v_mix_norm, v_ffb_norm, v_ffb_w_gate_up, v_ffb_w_down, v_ple_norm, v_ple_w_gate, v_ple_w_proj, v_ev_w_in, v_ev_sinks, v_ev_cq_norm, v_ev_w_uq, v_ev_ckv_norm, v_ev_w_ukv, v_ev_w_out, v_od_w_in, v_od_b_f, v_od_w_out, v_final_norm):
    given = dict(locals())
    w_in = {n: given[n] for n in WEIGHTS}

    layers, misc = _local_groups(w_in, BF16)
    (a0, b0, c0), (a1, b1, c1) = [[_in_slot(g) for g in layer] for layer in layers]
    a0, c0 = _comm_call(_gather_job([a0, c0], rows=[(0, DOWN_ROWS), (0, D_MODEL)]), "all_gather_first")
    W = {n: w_in[n] for n in SMALL}
    W["final_norm"] = final_norm.reshape(1, -1)
    W.update(A=[a0], C=[c0])
    slots = dict(a0=a0, b0=b0, c0=c0, m=_in_slot(misc), a1=a1, b1=b1, c1=c1)

    loss_cols, dx, r_early, r_later, r_mid, last, G = _local_step(x[0], p[:, 0], loss_target[0], W, slots)

    a1b, c1b, b1 = r_early
    a1f, c1f = r_later
    a0b, c0b, b0, r_misc = r_mid
    a0f, c0f = _reduce_scatter(last, "last")
    grads = _ungroup_local([[a0f, a0b], [a1f, a1b]], [b0, b1], [[c0f, c0b], [c1f, c1b]], r_misc)
    layout = [(n, int(np.prod(w_in[n].shape))) for n in SMALL]
    vec = jnp.concatenate([G[n].astype(F32).reshape(-1) for n, _ in layout] + [jnp.sum(loss_cols).reshape(1)])
    vec = jnp.pad(vec, (0, N_DEV * SMALL_COLS - vec.shape[0])).reshape(N_DEV, SMALL_COLS)
    vec = _all_reduce_small(vec).reshape(-1)
    off = 0
    for n, size in layout:
        grads[n] = vec[off: off + size].reshape(w_in[n].shape)
        off += size
    loss = vec[off]

    delta, new_m, new_v = {}, {}, {}
    for n in WEIGHTS:
        shp = w_in[n].shape
        as2d = (lambda a: a.reshape(1, -1)) if len(shp) == 1 else (lambda a: a)
        d, nm, nv = _adamw(as2d(w_in[n]), as2d(grads[n]), as2d(given["m_" + n]), as2d(given["v_" + n]), f"adamw_{n}")
        delta[n], new_m[n], new_v[n] = d.reshape(shp), nm.reshape(shp), nv.reshape(shp)
    return (loss, dx[None], *[grads[n] for n in WEIGHTS], *[delta[n] for n in WEIGHTS],
            *[new_m[n] for n in WEIGHTS], *[new_v[n] for n in WEIGHTS])
```
